```python
import jax, jax.numpy as jnp
from jax import lax
import numpy as np

D_MODEL = 1024
BATCH = 8
SEQ = 2048
DEPTH = 2

MIX_W = D_MODEL
N_BRANCH = 3
CONV_WIDTH = 4
DN_HEADS = 8
DN_HEAD_DIM = MIX_W // DN_HEADS
DN_CHUNK = 64
SB_HEADS = 16
SB_HEAD_DIM = MIX_W // SB_HEADS
SB_BLOCK = 128
SSM_HEADS = 16
SSM_HEAD_DIM = MIX_W // SSM_HEADS
SSM_STATE = 128
SSM_GROUPS = 4
SSM_CHUNK = 64
D_FF = 4 * D_MODEL
EPS = 1e-6

DN_QKV = 3 * MIX_W
SSM_CONV_DIM = MIX_W + 2 * SSM_GROUPS * SSM_STATE
IN_SIZES = (DN_QKV, MIX_W, DN_HEADS, DN_HEADS, 3 * MIX_W, MIX_W, SSM_CONV_DIM, SSM_HEADS, N_BRANCH * D_MODEL)
IN_DIM = sum(IN_SIZES)

kernel_name = "hybrid_gdn_stickbreak_mamba2_block"


def rms_norm(x, w):
    xf = x.astype(jnp.float32)
    xf = xf * lax.rsqrt(jnp.mean(xf * xf, axis=-1, keepdims=True) + EPS)
    return (xf * w.astype(jnp.float32)).astype(x.dtype)


def group_rms_norm(x, w, groups):
    shp = x.shape
    xf = x.astype(jnp.float32).reshape(*shp[:-1], groups, shp[-1] // groups)
    xf = xf * lax.rsqrt(jnp.mean(xf * xf, axis=-1, keepdims=True) + EPS)
    return (xf.reshape(shp) * w.astype(jnp.float32)).astype(x.dtype)


def l2_normalize(x):
    return x * lax.rsqrt(jnp.sum(x * x, axis=-1, keepdims=True) + EPS)


def split_columns(t, sizes):
    out, start = [], 0
    for s in sizes:
        out.append(t[..., start:start + s])
        start += s
    return out


def causal_dwconv(x, w, b=None):
    k_width, seq = w.shape[0], x.shape[1]
    xp = jnp.pad(x, ((0, 0), (k_width - 1, 0), (0, 0)))
    y = xp[:, 0:seq] * w[0]
    for k in range(1, k_width):
        y = y + xp[:, k:k + seq] * w[k]
    return y if b is None else y + b


def gated_delta_rule(q, k, v, g, beta):
    dtype = v.dtype
    q, k, v, g, beta = (t.astype(jnp.float32) for t in (q, k, v, g, beta))
    bsz, seq, h, dk = q.shape
    dv = v.shape[-1]
    c = DN_CHUNK
    n = seq // c

    def to_chunks(t):
        t = t.reshape(bsz, n, c, *t.shape[2:])
        return jnp.swapaxes(t, 2, 3)

    q = to_chunks(q) * dk ** -0.5
    k, v = to_chunks(k), to_chunks(v)
    g, beta = to_chunks(g), to_chunks(beta)
    gc = jnp.cumsum(g, axis=-1)
    causal = jnp.tril(jnp.ones((c, c), dtype=bool))
    strict = jnp.tril(jnp.ones((c, c), dtype=bool), -1)
    decay = jnp.exp(jnp.where(causal, gc[..., :, None] - gc[..., None, :], -jnp.inf))

    kb = k * beta[..., None]
    vb = v * beta[..., None]
    lower = jnp.where(strict, jnp.einsum('bnhid,bnhjd->bnhij', kb, k) * decay, 0.0)
    tmat = lower + jnp.eye(c, dtype=jnp.float32)
    rhs = jnp.concatenate([vb, kb * jnp.exp(gc)[..., None]], axis=-1)
    sol = lax.linalg.triangular_solve(tmat, rhs, left_side=True, lower=True, unit_diagonal=True)
    u, w = sol[..., :dv], sol[..., dv:]

    attn = jnp.einsum('bnhid,bnhjd->bnhij', q, k) * decay
    qg = q * jnp.exp(gc)[..., None]
    kd = k * jnp.exp(gc[..., -1:] - gc)[..., None]
    glast = jnp.exp(gc[..., -1])

    def step(state, xs):
        u_c, w_c, attn_c, qg_c, kd_c, gl_c = xs
        v_new = u_c - jnp.einsum('bhcd,bhde->bhce', w_c, state)
        o_c = jnp.einsum('bhcd,bhde->bhce', qg_c, state) + jnp.einsum('bhij,bhje->bhie', attn_c, v_new)
        state = state * gl_c[..., None, None] + jnp.einsum('bhcd,bhce->bhde', kd_c, v_new)
        return state, o_c

    xs = tuple(jnp.moveaxis(t, 1, 0) for t in (u, w, attn, qg, kd, glast))
    s0 = jnp.zeros((bsz, h, dk, dv), jnp.float32)
    _, o = lax.scan(step, s0, xs)
    o = jnp.transpose(o, (1, 0, 3, 2, 4)).reshape(bsz, seq, h, dv)
    return o.astype(dtype)


def gated_deltanet_branch(qkv, gate, a, b, conv_w, a_log, dt_bias, norm_w):
    bsz, seq, _ = qkv.shape
    qkv = jax.nn.silu(causal_dwconv(qkv, conv_w))
    q, k, v = (t.reshape(bsz, seq, DN_HEADS, DN_HEAD_DIM) for t in split_columns(qkv, (MIX_W, MIX_W, MIX_W)))
    q, k = l2_normalize(q), l2_normalize(k)
    beta = jax.nn.sigmoid(b.astype(jnp.float32))
    g = -jnp.exp(a_log.astype(jnp.float32)) * jax.nn.softplus(a.astype(jnp.float32) + dt_bias.astype(jnp.float32))
    o = gated_delta_rule(q, k, v, g, beta)
    o = rms_norm(o, norm_w) * jax.nn.silu(gate.reshape(bsz, seq, DN_HEADS, DN_HEAD_DIM))
    return o.reshape(bsz, seq, MIX_W)


def stick_breaking_branch(qkv):
    bsz, seq, _ = qkv.shape
    dtype = qkv.dtype
    q, k, v = (t.reshape(bsz, seq, SB_HEADS, SB_HEAD_DIM).astype(jnp.float32)
               for t in split_columns(qkv, (MIX_W, MIX_W, MIX_W)))
    scale = SB_HEAD_DIM ** -0.5
    outs = []
    for i in range(seq // SB_BLOCK):
        t0, t1 = i * SB_BLOCK, (i + 1) * SB_BLOCK
        qb, kb, vb = q[:, t0:t1], k[:, :t1], v[:, :t1]
        z = jnp.einsum('bthd,bshd->bhts', qb, kb) * scale
        t_idx = t0 + jnp.arange(SB_BLOCK)
        s_idx = jnp.arange(t1)
        mask = s_idx[None, :] < t_idx[:, None]
        log_keep = jnp.where(mask, -jax.nn.softplus(z), 0.0)
        reach = lax.cumsum(log_keep, axis=3, reverse=True) - log_keep
        log_a = jax.nn.log_sigmoid(z) + reach
        weights = jnp.exp(jnp.where(mask, log_a, -jnp.inf))
        outs.append(jnp.einsum('bhts,bshd->bthd', weights, vb))
    o = jnp.concatenate(outs, axis=1)
    return o.reshape(bsz, seq, MIX_W).astype(dtype)


def ssd_chunked(x, a, bm, cm):
    dtype = x.dtype
    x, a, bm, cm = (t.astype(jnp.float32) for t in (x, a, bm, cm))
    bsz, seq, h, p = x.shape
    g, n_state = bm.shape[2], bm.shape[3]
    r = h // g
    c = SSM_CHUNK
    nc = seq // c
    x = x.reshape(bsz, nc, c, g, r, p)
    a = a.reshape(bsz, nc, c, g, r)
    bm = bm.reshape(bsz, nc, c, g, n_state)
    cm = cm.reshape(bsz, nc, c, g, n_state)
    a_cum = jnp.cumsum(a, axis=2)
    causal = jnp.tril(jnp.ones((c, c), dtype=bool))
    seg = a_cum[:, :, :, None] - a_cum[:, :, None, :]
    lmat = jnp.exp(jnp.where(causal[:, :, None, None], seg, -jnp.inf))
    scores = jnp.einsum('bclgn,bcsgn->bclsg', cm, bm)
    y_diag = jnp.einsum('bclsg,bclsgr,bcsgrp->bclgrp', scores, lmat, x)
    decay_states = jnp.exp(a_cum[:, :, -1:] - a_cum)
    chunk_states = jnp.einsum('bclgn,bclgr,bclgrp->bcgrpn', bm, decay_states, x)
    chunk_decay = jnp.exp(a_cum[:, :, -1])

    def step(state, xs):
        st, dec = xs
        return state * dec[..., None, None] + st, state

    h0 = jnp.zeros((bsz, g, r, p, n_state), jnp.float32)
    _, h_prev = lax.scan(step, h0, (jnp.moveaxis(chunk_states, 1, 0), jnp.moveaxis(chunk_decay, 1, 0)))
    h_prev = jnp.moveaxis(h_prev, 0, 1)
    y_off = jnp.einsum('bclgn,bcgrpn,bclgr->bclgrp', cm, h_prev, jnp.exp(a_cum))
    return (y_diag + y_off).reshape(bsz, seq, h, p).astype(dtype)


def mamba2_branch(z, xbc, dt, conv_w, conv_b, a_log, dt_bias, d_skip, norm_w):
    bsz, seq, _ = z.shape
    gn = SSM_GROUPS * SSM_STATE
    xbc = jax.nn.silu(causal_dwconv(xbc, conv_w, conv_b))
    xs, bm, cm = split_columns(xbc, (MIX_W, gn, gn))
    xs = xs.reshape(bsz, seq, SSM_HEADS, SSM_HEAD_DIM)
    bm = bm.reshape(bsz, seq, SSM_GROUPS, SSM_STATE)
    cm = cm.reshape(bsz, seq, SSM_GROUPS, SSM_STATE)
    dt = jax.nn.softplus(dt.astype(jnp.float32) + dt_bias.astype(jnp.float32))
    a = -jnp.exp(a_log.astype(jnp.float32)) * dt
    y = ssd_chunked(xs * dt[..., None].astype(xs.dtype), a, bm, cm)
    y = y + xs * d_skip[:, None]
    y = y.reshape(bsz, seq, MIX_W) * jax.nn.silu(z)
    return group_rms_norm(y, norm_w, SSM_GROUPS)


def hybrid_mixer(xn, w_in, dn_conv_w, dn_a_log, dn_dt_bias, dn_norm_w,
                 ssm_conv_w, ssm_conv_b, ssm_a_log, ssm_dt_bias, ssm_d, ssm_norm_w,
                 w_branch, w_out):
    bsz, seq, _ = xn.shape
    proj = xn @ w_in
    (dn_qkv, dn_gate, dn_a, dn_b, sb_qkv, ssm_z, ssm_xbc, ssm_dt, gate_logits) = split_columns(proj, IN_SIZES)
    o_dn = gated_deltanet_branch(dn_qkv, dn_gate, dn_a, dn_b, dn_conv_w, dn_a_log, dn_dt_bias, dn_norm_w)
    o_sb = stick_breaking_branch(sb_qkv)
    o_ssm = mamba2_branch(ssm_z, ssm_xbc, ssm_dt, ssm_conv_w, ssm_conv_b, ssm_a_log, ssm_dt_bias, ssm_d, ssm_norm_w)
    branches = jnp.stack([o_dn, o_sb, o_ssm], axis=2)
    projected = jnp.einsum('bsim,imd->bsid', branches, w_branch)
    gates = jax.nn.sigmoid(gate_logits.reshape(bsz, seq, N_BRANCH, D_MODEL))
    merged = jnp.sum(gates * projected, axis=2)
    return merged @ w_out


def _fwd_setup_inputs(seed: int = 0) -> dict:
    key = jax.random.key(seed)
    ks = jax.random.split(key, 20)
    L = DEPTH

    def nrm(k, shape, scale):
        return jax.random.normal(k, shape, jnp.float32) * scale

    def log_uniform_a(k, shape):
        return jnp.log(jax.random.uniform(k, shape, jnp.float32, minval=1.0, maxval=16.0))

    def dt_bias_init(k, shape):
        dt = jnp.exp(jax.random.uniform(k, shape, jnp.float32, minval=np.log(1e-3), maxval=np.log(1e-1)))
        return dt + jnp.log(-jnp.expm1(-dt))

    return {
        "x": nrm(ks[0], (BATCH, SEQ, D_MODEL), 1.0),
        "norm_mix": 1.0 + nrm(ks[1], (L, D_MODEL), 0.02),
        "w_in": nrm(ks[2], (L, D_MODEL, IN_DIM), D_MODEL ** -0.5),
        "dn_conv_w": nrm(ks[3], (L, CONV_WIDTH, DN_QKV), CONV_WIDTH ** -0.5),
        "dn_a_log": log_uniform_a(ks[4], (L, DN_HEADS)),
        "dn_dt_bias": dt_bias_init(ks[5], (L, DN_HEADS)),
        "dn_norm_w": 1.0 + nrm(ks[6], (L, DN_HEAD_DIM), 0.02),
        "ssm_conv_w": nrm(ks[7], (L, CONV_WIDTH, SSM_CONV_DIM), CONV_WIDTH ** -0.5),
        "ssm_conv_b": nrm(ks[8], (L, SSM_CONV_DIM), 0.02),
        "ssm_a_log": log_uniform_a(ks[9], (L, SSM_HEADS)),
        "ssm_dt_bias": dt_bias_init(ks[10], (L, SSM_HEADS)),
        "ssm_d": 1.0 + nrm(ks[11], (L, SSM_HEADS), 0.02),
        "ssm_norm_w": 1.0 + nrm(ks[12], (L, MIX_W), 0.02),
        "w_branch": nrm(ks[13], (L, N_BRANCH, MIX_W, D_MODEL), MIX_W ** -0.5),
        "w_out": nrm(ks[14], (L, D_MODEL, D_MODEL), D_MODEL ** -0.5),
        "norm_mlp": 1.0 + nrm(ks[15], (L, D_MODEL), 0.02),
        "w_up": nrm(ks[16], (L, D_MODEL, D_FF), D_MODEL ** -0.5),
        "w_down": nrm(ks[17], (L, D_FF, D_MODEL), D_FF ** -0.5),
        "norm_final": 1.0 + nrm(ks[18], (D_MODEL,), 0.02),
    }


def _fwd_reference(x, norm_mix, w_in, dn_conv_w, dn_a_log, dn_dt_bias, dn_norm_w,
              ssm_conv_w, ssm_conv_b, ssm_a_log, ssm_dt_bias, ssm_d, ssm_norm_w,
              w_branch, w_out, norm_mlp, w_up, w_down, norm_final):
    for l in range(DEPTH):
        h = rms_norm(x, norm_mix[l])
        x = x + hybrid_mixer(h, w_in[l], dn_conv_w[l], dn_a_log[l], dn_dt_bias[l], dn_norm_w[l],
                             ssm_conv_w[l], ssm_conv_b[l], ssm_a_log[l], ssm_dt_bias[l], ssm_d[l],
                             ssm_norm_w[l], w_branch[l], w_out[l])
        h = rms_norm(x, norm_mlp[l])
        x = x + jnp.square(jax.nn.relu(h @ w_up[l])) @ w_down[l]
    return rms_norm(x, norm_final)


import jax as _jax
import jax.numpy as _jnp

TWIN_FORMAT = 'train_step'
FWD_PARAMS = ['x', 'norm_mix', 'w_in', 'dn_conv_w', 'dn_a_log', 'dn_dt_bias', 'dn_norm_w', 'ssm_conv_w', 'ssm_conv_b', 'ssm_a_log', 'ssm_dt_bias', 'ssm_d', 'ssm_norm_w', 'w_branch', 'w_out', 'norm_mlp', 'w_up', 'w_down', 'norm_final']
TWIN_WEIGHTS = ['norm_mix', 'w_in', 'dn_conv_w', 'dn_a_log', 'dn_dt_bias', 'dn_norm_w', 'ssm_conv_w', 'ssm_conv_b', 'ssm_a_log', 'ssm_dt_bias', 'ssm_d', 'ssm_norm_w', 'w_branch', 'w_out', 'norm_mlp', 'w_up', 'w_down', 'norm_final']
TWIN_DIFF_INPUT = 'x'
TWIN_INPUTS = ['x', 'norm_mix', 'w_in', 'dn_conv_w', 'dn_a_log', 'dn_dt_bias', 'dn_norm_w', 'ssm_conv_w', 'ssm_conv_b', 'ssm_a_log', 'ssm_dt_bias', 'ssm_d', 'ssm_norm_w', 'w_branch', 'w_out', 'norm_mlp', 'w_up', 'w_down', 'norm_final', 'loss_target', 'm_norm_mix', 'm_w_in', 'm_dn_conv_w', 'm_dn_a_log', 'm_dn_dt_bias', 'm_dn_norm_w', 'm_ssm_conv_w', 'm_ssm_conv_b', 'm_ssm_a_log', 'm_ssm_dt_bias', 'm_ssm_d', 'm_ssm_norm_w', 'm_w_branch', 'm_w_out', 'm_norm_mlp', 'm_w_up', 'm_w_down', 'm_norm_final', 'v_norm_mix', 'v_w_in', 'v_dn_conv_w', 'v_dn_a_log', 'v_dn_dt_bias', 'v_dn_norm_w', 'v_ssm_conv_w', 'v_ssm_conv_b', 'v_ssm_a_log', 'v_ssm_dt_bias', 'v_ssm_d', 'v_ssm_norm_w', 'v_w_branch', 'v_w_out', 'v_norm_mlp', 'v_w_up', 'v_w_down', 'v_norm_final']
TWIN_OUTPUTS = ['loss', 'grad_x', 'grad_norm_mix', 'grad_w_in', 'grad_dn_conv_w', 'grad_dn_a_log', 'grad_dn_dt_bias', 'grad_dn_norm_w', 'grad_ssm_conv_w', 'grad_ssm_conv_b', 'grad_ssm_a_log', 'grad_ssm_dt_bias', 'grad_ssm_d', 'grad_ssm_norm_w', 'grad_w_branch', 'grad_w_out', 'grad_norm_mlp', 'grad_w_up', 'grad_w_down', 'grad_norm_final', 'delta_norm_mix', 'delta_w_in', 'delta_dn_conv_w', 'delta_dn_a_log', 'delta_dn_dt_bias', 'delta_dn_norm_w', 'delta_ssm_conv_w', 'delta_ssm_conv_b', 'delta_ssm_a_log', 'delta_ssm_dt_bias', 'delta_ssm_d', 'delta_ssm_norm_w', 'delta_w_branch', 'delta_w_out', 'delta_norm_mlp', 'delta_w_up', 'delta_w_down', 'delta_norm_final', 'new_m_norm_mix', 'new_m_w_in', 'new_m_dn_conv_w', 'new_m_dn_a_log', 'new_m_dn_dt_bias', 'new_m_dn_norm_w', 'new_m_ssm_conv_w', 'new_m_ssm_conv_b', 'new_m_ssm_a_log', 'new_m_ssm_dt_bias', 'new_m_ssm_d', 'new_m_ssm_norm_w', 'new_m_w_branch', 'new_m_w_out', 'new_m_norm_mlp', 'new_m_w_up', 'new_m_w_down', 'new_m_norm_final', 'new_v_norm_mix', 'new_v_w_in', 'new_v_dn_conv_w', 'new_v_dn_a_log', 'new_v_dn_dt_bias', 'new_v_dn_norm_w', 'new_v_ssm_conv_w', 'new_v_ssm_conv_b', 'new_v_ssm_a_log', 'new_v_ssm_dt_bias', 'new_v_ssm_d', 'new_v_ssm_norm_w', 'new_v_w_branch', 'new_v_w_out', 'new_v_norm_mlp', 'new_v_w_up', 'new_v_w_down', 'new_v_norm_final']
TWIN_LEAF_KINDS = {'loss': 'loss', 'grad_x': 'grad_x', 'grad_norm_mix': 'grad_w', 'grad_w_in': 'grad_w', 'grad_dn_conv_w': 'grad_w', 'grad_dn_a_log': 'grad_w', 'grad_dn_dt_bias': 'grad_w', 'grad_dn_norm_w': 'grad_w', 'grad_ssm_conv_w': 'grad_w', 'grad_ssm_conv_b': 'grad_w', 'grad_ssm_a_log': 'grad_w', 'grad_ssm_dt_bias': 'grad_w', 'grad_ssm_d': 'grad_w', 'grad_ssm_norm_w': 'grad_w', 'grad_w_branch': 'grad_w', 'grad_w_out': 'grad_w', 'grad_norm_mlp': 'grad_w', 'grad_w_up': 'grad_w', 'grad_w_down': 'grad_w', 'grad_norm_final': 'grad_w', 'delta_norm_mix': 'delta_w', 'delta_w_in': 'delta_w', 'delta_dn_conv_w': 'delta_w', 'delta_dn_a_log': 'delta_w', 'delta_dn_dt_bias': 'delta_w', 'delta_dn_norm_w': 'delta_w', 'delta_ssm_conv_w': 'delta_w', 'delta_ssm_conv_b': 'delta_w', 'delta_ssm_a_log': 'delta_w', 'delta_ssm_dt_bias': 'delta_w', 'delta_ssm_d': 'delta_w', 'delta_ssm_norm_w': 'delta_w', 'delta_w_branch': 'delta_w', 'delta_w_out': 'delta_w', 'delta_norm_mlp': 'delta_w', 'delta_w_up': 'delta_w', 'delta_w_down': 'delta_w', 'delta_norm_final': 'delta_w', 'new_m_norm_mix': 'new_m', 'new_m_w_in': 'new_m', 'new_m_dn_conv_w': 'new_m', 'new_m_dn_a_log': 'new_m', 'new_m_dn_dt_bias': 'new_m', 'new_m_dn_norm_w': 'new_m', 'new_m_ssm_conv_w': 'new_m', 'new_m_ssm_conv_b': 'new_m', 'new_m_ssm_a_log': 'new_m', 'new_m_ssm_dt_bias': 'new_m', 'new_m_ssm_d': 'new_m', 'new_m_ssm_norm_w': 'new_m', 'new_m_w_branch': 'new_m', 'new_m_w_out': 'new_m', 'new_m_norm_mlp': 'new_m', 'new_m_w_up': 'new_m', 'new_m_w_down': 'new_m', 'new_m_norm_final': 'new_m', 'new_v_norm_mix': 'new_v', 'new_v_w_in': 'new_v', 'new_v_dn_conv_w': 'new_v', 'new_v_dn_a_log': 'new_v', 'new_v_dn_dt_bias': 'new_v', 'new_v_dn_norm_w': 'new_v', 'new_v_ssm_conv_w': 'new_v', 'new_v_ssm_conv_b': 'new_v', 'new_v_ssm_a_log': 'new_v', 'new_v_ssm_dt_bias': 'new_v', 'new_v_ssm_d': 'new_v', 'new_v_ssm_norm_w': 'new_v', 'new_v_w_branch': 'new_v', 'new_v_w_out': 'new_v', 'new_v_norm_mlp': 'new_v', 'new_v_w_up': 'new_v', 'new_v_w_down': 'new_v', 'new_v_norm_final': 'new_v'}


def _forward(args):
    return _fwd_reference(*[args[k] for k in FWD_PARAMS])


def _output_shape():
    out = _jax.eval_shape(lambda: _forward(_fwd_setup_inputs(0)))
    return out.shape, out.dtype

N_MICROBATCH = 1
ADAM_LR = 0.001
ADAM_B1 = 0.9
ADAM_B2 = 0.999
ADAM_EPS = 1e-08
ADAM_WD = 0.01
ADAM_STEP = 10
PER_EXAMPLE_BATCH_AXIS = {'x': 0, 'loss_target': 0}
SHARED_INPUTS = []
_WEIGHT_DTYPES = {'norm_mix': _jnp.float32, 'w_in': _jnp.float32, 'dn_conv_w': _jnp.float32, 'dn_a_log': _jnp.float32, 'dn_dt_bias': _jnp.float32, 'dn_norm_w': _jnp.float32, 'ssm_conv_w': _jnp.float32, 'ssm_conv_b': _jnp.float32, 'ssm_a_log': _jnp.float32, 'ssm_dt_bias': _jnp.float32, 'ssm_d': _jnp.float32, 'ssm_norm_w': _jnp.float32, 'w_branch': _jnp.float32, 'w_out': _jnp.float32, 'norm_mlp': _jnp.float32, 'w_up': _jnp.float32, 'w_down': _jnp.float32, 'norm_final': _jnp.float32}
MOMENT_SCALE = {'norm_mix': 1.159802e-01, 'w_in': 3.123623e-02, 'dn_conv_w': 2.561190e-02, 'dn_a_log': 1.692163e-01, 'dn_dt_bias': 1.621075e-01, 'dn_norm_w': 9.561311e-02, 'ssm_conv_w': 4.193412e-02, 'ssm_conv_b': 5.488111e-02, 'ssm_a_log': 2.342568e-01, 'ssm_dt_bias': 9.839403e-02, 'ssm_d': 3.034178e-01, 'ssm_norm_w': 5.484913e-02, 'w_branch': 4.268212e-02, 'w_out': 7.403570e-02, 'norm_mlp': 9.914352e-02, 'w_up': 4.994248e-02, 'w_down': 9.305604e-02, 'norm_final': 1.625389e+01}


def _to_microbatches(a, axis):
    t = _jnp.moveaxis(a, axis, 0)
    t = t.reshape((N_MICROBATCH, t.shape[0] // N_MICROBATCH) + t.shape[1:])
    return _jnp.moveaxis(t, 1, axis + 1)


def setup_inputs(seed: int = 0) -> dict:
    inp = _fwd_setup_inputs(seed)
    key = _jax.random.fold_in(_jax.random.key(seed), 7919)
    shape, _ = _output_shape()
    out = dict(inp)
    out["loss_target"] = _jax.random.normal(_jax.random.fold_in(key, 0), shape, _jnp.float32)
    for i, name in enumerate(TWIN_WEIGHTS):
        w = inp[name].astype(_jnp.float32)
        if MOMENT_SCALE is None:
            s = _jnp.sqrt(_jnp.mean(_jnp.square(w)) + 1e-30)
        else:
            s = MOMENT_SCALE[name]
        km, kv = _jax.random.split(_jax.random.fold_in(key, i + 1))
        out[name] = w
        out["m_" + name] = s * _jax.random.normal(km, w.shape, _jnp.float32)
        out["v_" + name] = (s * s) * _jax.random.uniform(kv, w.shape, _jnp.float32, 0.5, 1.5)
    if N_MICROBATCH > 1:
        for name, axis in PER_EXAMPLE_BATCH_AXIS.items():
            out[name] = _to_microbatches(out[name], axis)
    return {'x': out['x'], 'norm_mix': out['norm_mix'], 'w_in': out['w_in'], 'dn_conv_w': out['dn_conv_w'], 'dn_a_log': out['dn_a_log'], 'dn_dt_bias': out['dn_dt_bias'], 'dn_norm_w': out['dn_norm_w'], 'ssm_conv_w': out['ssm_conv_w'], 'ssm_conv_b': out['ssm_conv_b'], 'ssm_a_log': out['ssm_a_log'], 'ssm_dt_bias': out['ssm_dt_bias'], 'ssm_d': out['ssm_d'], 'ssm_norm_w': out['ssm_norm_w'], 'w_branch': out['w_branch'], 'w_out': out['w_out'], 'norm_mlp': out['norm_mlp'], 'w_up': out['w_up'], 'w_down': out['w_down'], 'norm_final': out['norm_final'], 'loss_target': out['loss_target'], 'm_norm_mix': out['m_norm_mix'], 'm_w_in': out['m_w_in'], 'm_dn_conv_w': out['m_dn_conv_w'], 'm_dn_a_log': out['m_dn_a_log'], 'm_dn_dt_bias': out['m_dn_dt_bias'], 'm_dn_norm_w': out['m_dn_norm_w'], 'm_ssm_conv_w': out['m_ssm_conv_w'], 'm_ssm_conv_b': out['m_ssm_conv_b'], 'm_ssm_a_log': out['m_ssm_a_log'], 'm_ssm_dt_bias': out['m_ssm_dt_bias'], 'm_ssm_d': out['m_ssm_d'], 'm_ssm_norm_w': out['m_ssm_norm_w'], 'm_w_branch': out['m_w_branch'], 'm_w_out': out['m_w_out'], 'm_norm_mlp': out['m_norm_mlp'], 'm_w_up': out['m_w_up'], 'm_w_down': out['m_w_down'], 'm_norm_final': out['m_norm_final'], 'v_norm_mix': out['v_norm_mix'], 'v_w_in': out['v_w_in'], 'v_dn_conv_w': out['v_dn_conv_w'], 'v_dn_a_log': out['v_dn_a_log'], 'v_dn_dt_bias': out['v_dn_dt_bias'], 'v_dn_norm_w': out['v_dn_norm_w'], 'v_ssm_conv_w': out['v_ssm_conv_w'], 'v_ssm_conv_b': out['v_ssm_conv_b'], 'v_ssm_a_log': out['v_ssm_a_log'], 'v_ssm_dt_bias': out['v_ssm_dt_bias'], 'v_ssm_d': out['v_ssm_d'], 'v_ssm_norm_w': out['v_ssm_norm_w'], 'v_w_branch': out['v_w_branch'], 'v_w_out': out['v_w_out'], 'v_norm_mlp': out['v_norm_mlp'], 'v_w_up': out['v_w_up'], 'v_w_down': out['v_w_down'], 'v_norm_final': out['v_norm_final']}


def _loss(weights, diff, rest, loss_target):
    with _jax.named_scope("forward"):
        args = {**rest, TWIN_DIFF_INPUT: diff, **{k: w.astype(_WEIGHT_DTYPES[k]) for k, w in weights.items()}}
        y = _forward(args)
    with _jax.named_scope("loss_head"):
        err = _jnp.square(y.astype(_jnp.float32) - loss_target)
        return 0.5 * _jnp.sum(_jnp.mean(err, axis=-1)) if err.ndim else 0.5 * err


def _adamw(w, g, m, v):
    m = ADAM_B1 * m + (1.0 - ADAM_B1) * g
    v = ADAM_B2 * v + (1.0 - ADAM_B2) * _jnp.square(g)
    m_hat = m / (1.0 - ADAM_B1 ** ADAM_STEP)
    v_hat = v / (1.0 - ADAM_B2 ** ADAM_STEP)
    delta = -ADAM_LR * (m_hat / (_jnp.sqrt(v_hat) + ADAM_EPS) + ADAM_WD * w)
    return delta, m, v


def reference(x, norm_mix, w_in, dn_conv_w, dn_a_log, dn_dt_bias, dn_norm_w, ssm_conv_w, ssm_conv_b, ssm_a_log, ssm_dt_bias, ssm_d, ssm_norm_w, w_branch, w_out, norm_mlp, w_up, w_down, norm_final, loss_target, m_norm_mix, m_w_in, m_dn_conv_w, m_dn_a_log, m_dn_dt_bias, m_dn_norm_w, m_ssm_conv_w, m_ssm_conv_b, m_ssm_a_log, m_ssm_dt_bias, m_ssm_d, m_ssm_norm_w, m_w_branch, m_w_out, m_norm_mlp, m_w_up, m_w_down, m_norm_final, v_norm_mix, v_w_in, v_dn_conv_w, v_dn_a_log, v_dn_dt_bias, v_dn_norm_w, v_ssm_conv_w, v_ssm_conv_b, v_ssm_a_log, v_ssm_dt_bias, v_ssm_d, v_ssm_norm_w, v_w_branch, v_w_out, v_norm_mlp, v_w_up, v_w_down, v_norm_final):
    given = dict(x=x, norm_mix=norm_mix, w_in=w_in, dn_conv_w=dn_conv_w, dn_a_log=dn_a_log, dn_dt_bias=dn_dt_bias, dn_norm_w=dn_norm_w, ssm_conv_w=ssm_conv_w, ssm_conv_b=ssm_conv_b, ssm_a_log=ssm_a_log, ssm_dt_bias=ssm_dt_bias, ssm_d=ssm_d, ssm_norm_w=ssm_norm_w, w_branch=w_branch, w_out=w_out, norm_mlp=norm_mlp, w_up=w_up, w_down=w_down, norm_final=norm_final, loss_target=loss_target, m_norm_mix=m_norm_mix, m_w_in=m_w_in, m_dn_conv_w=m_dn_conv_w, m_dn_a_log=m_dn_a_log, m_dn_dt_bias=m_dn_dt_bias, m_dn_norm_w=m_dn_norm_w, m_ssm_conv_w=m_ssm_conv_w, m_ssm_conv_b=m_ssm_conv_b, m_ssm_a_log=m_ssm_a_log, m_ssm_dt_bias=m_ssm_dt_bias, m_ssm_d=m_ssm_d, m_ssm_norm_w=m_ssm_norm_w, m_w_branch=m_w_branch, m_w_out=m_w_out, m_norm_mlp=m_norm_mlp, m_w_up=m_w_up, m_w_down=m_w_down, m_norm_final=m_norm_final, v_norm_mix=v_norm_mix, v_w_in=v_w_in, v_dn_conv_w=v_dn_conv_w, v_dn_a_log=v_dn_a_log, v_dn_dt_bias=v_dn_dt_bias, v_dn_norm_w=v_dn_norm_w, v_ssm_conv_w=v_ssm_conv_w, v_ssm_conv_b=v_ssm_conv_b, v_ssm_a_log=v_ssm_a_log, v_ssm_dt_bias=v_ssm_dt_bias, v_ssm_d=v_ssm_d, v_ssm_norm_w=v_ssm_norm_w, v_w_branch=v_w_branch, v_w_out=v_w_out, v_norm_mlp=v_norm_mlp, v_w_up=v_w_up, v_w_down=v_w_down, v_norm_final=v_norm_final)
    weights = {n: given[n] for n in TWIN_WEIGHTS}
    shared = {n: given[n] for n in SHARED_INPUTS}
    per_example = {n: given[n] for n in ['x']}
    grad_fn = _jax.value_and_grad(_loss, argnums=(0, 1))

    def one_microbatch(ex, loss_target):
        ex = dict(ex)
        diff = ex.pop(TWIN_DIFF_INPUT)
        return grad_fn(weights, diff, {**shared, **ex}, loss_target)

    if N_MICROBATCH == 1:
        loss, (grad_w, grad_x) = one_microbatch(per_example, given["loss_target"])
    else:
        def body(carry, xs):
            loss_sum, grad_sum = carry
            l_k, (gw_k, gx_k) = one_microbatch(xs[0], xs[1])
            with _jax.named_scope("update"):
                return (loss_sum + l_k, _jax.tree.map(_jnp.add, grad_sum, gw_k)), gx_k

        init = (_jnp.zeros((), _jnp.float32), _jax.tree.map(_jnp.zeros_like, weights))
        (loss, grad_w), grad_x = _jax.lax.scan(body, init, (per_example, given["loss_target"]))
    with _jax.named_scope("update"):
        delta_w, new_m, new_v = {}, {}, {}
        for n in TWIN_WEIGHTS:
            delta_w[n], new_m[n], new_v[n] = _adamw(weights[n], grad_w[n], given["m_" + n], given["v_" + n])
    return (loss, grad_x, *[grad_w[n] for n in TWIN_WEIGHTS], *[delta_w[n] for n in TWIN_WEIGHTS],
            *[new_m[n] for n in TWIN_WEIGHTS], *[new_v[n] for n in TWIN_WEIGHTS])
```

```python
import functools
import math

import jax
import jax.numpy as jnp
from jax import lax
from jax.experimental import pallas as pl
from jax.experimental.pallas import tpu as pltpu

F32 = jnp.float32
BF16 = jnp.bfloat16
MXU_DTYPE = BF16
HIGHEST = lax.Precision.HIGHEST

N_DEV = 8
DEPTH = 2
EPS = 1e-6
CONV_K = 4
DN_HEAD_DIM = 128
SB_HEAD_DIM = 64
SSM_HEAD_DIM = 64
SSM_STATE = 128
SSM_GROUPS = 4
CHUNK = 64
SB_BLOCK = 128
LANES = 128
ADAM_LR, ADAM_B1, ADAM_B2, ADAM_EPS, ADAM_WD, ADAM_STEP = 0.001, 0.9, 0.999, 1e-08, 0.01, 10
NEG_BIG = -1e30
BIG_ROW_MULT = 512

ARB = "arbitrary"


def _cparams(n_axes):
    return pltpu.CompilerParams(dimension_semantics=(ARB,) * n_axes)


def _softplus(x):
    return jnp.maximum(x, 0.0) + jnp.log1p(jnp.exp(-jnp.abs(x)))


def _sigmoid(x):
    return 1.0 / (1.0 + jnp.exp(-x))


def _silu(x):
    return x * _sigmoid(x)


def _silu_grad(x):
    s = _sigmoid(x)
    return s * (1.0 + x * (1.0 - s))


def _dot(a, b, dims, prec=None):
    return lax.dot_general(a, b, (dims, ((), ())), precision=prec, preferred_element_type=F32)


NN = ((1,), (0,))
NT = ((1,), (1,))
TN = ((0,), (0,))


def _hdot(a, b, dims=NN):
    return _dot(a, b, dims, HIGHEST)


def _bdot(a, b, dims=NN):
    return _dot(a.astype(MXU_DTYPE), b.astype(MXU_DTYPE), dims)


def _split_dot(a, m_bf16, nsplit=3):
    out = None
    rem = a
    for _ in range(nsplit):
        piece = rem.astype(BF16)
        rem = rem - piece.astype(F32)
        term = _dot(piece, m_bf16, NN)
        out = term if out is None else out + term
    return out


def _pick(n, pref):
    for t in pref:
        if n % t == 0:
            return t
    return n


def _matmul(a, b, *, ta=False, tb=False, name, epilogue=None, extras=(), out_dtypes=(F32,), tm=None, tn=None, tk=None):
    m, k = (a.shape[1], a.shape[0]) if ta else a.shape
    k2, n = (b.shape[1], b.shape[0]) if tb else b.shape
    assert k == k2, (a.shape, b.shape, ta, tb)
    tm = tm or _pick(m, (512, 256, 128))
    tn = tn or _pick(n, (1024, 640, 512, 384, 256, 128))
    tk = tk or _pick(k, (1920, 1024, 640, 512, 256, 128))
    nk = k // tk
    a_spec = pl.BlockSpec((tk, tm), lambda i, j, kk: (kk, i)) if ta else pl.BlockSpec((tm, tk), lambda i, j, kk: (i, kk))
    b_spec = pl.BlockSpec((tn, tk), lambda i, j, kk: (j, kk)) if tb else pl.BlockSpec((tk, tn), lambda i, j, kk: (kk, j))
    o_spec = pl.BlockSpec((tm, tn), lambda i, j, kk: (i, j))
    dims = (((0,) if ta else (1,)), ((1,) if tb else (0,)))
    n_extra = len(extras)
    n_out = len(out_dtypes)

    def body(*refs):
        a_ref, b_ref = refs[0], refs[1]
        extra_refs = refs[2:2 + n_extra]
        out_refs = refs[2 + n_extra:2 + n_extra + n_out]
        acc_ref = refs[-1]
        kk = pl.program_id(2)

        @pl.when(kk == 0)
        def _():
            acc_ref[...] = jnp.zeros_like(acc_ref)

        acc_ref[...] += _dot(a_ref[...].astype(MXU_DTYPE), b_ref[...].astype(MXU_DTYPE), dims)

        @pl.when(kk == nk - 1)
        def _():
            acc = acc_ref[...]
            outs = (acc,) if epilogue is None else epilogue(acc, *[r[...] for r in extra_refs])
            for o_ref, o in zip(out_refs, outs):
                o_ref[...] = o.astype(o_ref.dtype)

    outs = pl.pallas_call(
        body,
        grid=(m // tm, n // tn, nk),
        in_specs=[a_spec, b_spec] + [o_spec] * n_extra,
        out_specs=[o_spec] * n_out,
        out_shape=[jax.ShapeDtypeStruct((m, n), dt) for dt in out_dtypes],
        scratch_shapes=[pltpu.VMEM((tm, tn), F32)],
        compiler_params=pltpu.CompilerParams(dimension_semantics=("parallel", "parallel", ARB)),
        name=name,
    )(a, b, *extras)
    return outs[0] if n_out == 1 else tuple(outs)


def _rms_fwd(x, w, *, name, tm=256):
    s, d = x.shape
    out_dtype = MXU_DTYPE

    def body(x_ref, w_ref, o_ref):
        xv = x_ref[...]
        r = lax.rsqrt(jnp.mean(xv * xv, axis=-1, keepdims=True) + EPS)
        o_ref[...] = (xv * r * w_ref[...]).astype(o_ref.dtype)

    return pl.pallas_call(
        body, grid=(s // tm,),
        in_specs=[pl.BlockSpec((tm, d), lambda i: (i, 0)), pl.BlockSpec((1, d), lambda i: (0, 0))],
        out_specs=pl.BlockSpec((tm, d), lambda i: (i, 0)),
        out_shape=jax.ShapeDtypeStruct((s, d), out_dtype),
        compiler_params=_cparams(1), name=name,
    )(x, w.reshape(1, d))


def _rms_bwd(x, w, dh, dres, *, name, tm=256):
    s, d = x.shape

    def body(x_ref, w_ref, dh_ref, dres_ref, dx_ref, dw_ref):
        xv = x_ref[...]
        r = lax.rsqrt(jnp.mean(xv * xv, axis=-1, keepdims=True) + EPS)
        xh = xv * r
        dhv = dh_ref[...].astype(F32)
        dxn = dhv * w_ref[...]
        dx = r * (dxn - xh * jnp.mean(dxn * xh, axis=-1, keepdims=True))
        dx_ref[...] = dres_ref[...] + dx

        @pl.when(pl.program_id(0) == 0)
        def _():
            dw_ref[...] = jnp.zeros_like(dw_ref)

        dw_ref[...] += jnp.sum(dhv * xh, axis=0, keepdims=True)

    dx, dw = pl.pallas_call(
        body, grid=(s // tm,),
        in_specs=[pl.BlockSpec((tm, d), lambda i: (i, 0)), pl.BlockSpec((1, d), lambda i: (0, 0)),
                  pl.BlockSpec((tm, d), lambda i: (i, 0)), pl.BlockSpec((tm, d), lambda i: (i, 0))],
        out_specs=[pl.BlockSpec((tm, d), lambda i: (i, 0)), pl.BlockSpec((1, d), lambda i: (0, 0))],
        out_shape=[jax.ShapeDtypeStruct((s, d), F32), jax.ShapeDtypeStruct((1, d), F32)],
        compiler_params=_cparams(1), name=name,
    )(x, w.reshape(1, d), dh, dres)
    return dx, dw.reshape(d)


def _final_loss(x, w, target, *, name, tm=256):
    s, d = x.shape

    def body(x_ref, w_ref, t_ref, loss_ref, dx_ref, dw_ref):
        xv = x_ref[...]
        r = lax.rsqrt(jnp.mean(xv * xv, axis=-1, keepdims=True) + EPS)
        xh = xv * r
        err = xh * w_ref[...] - t_ref[...]
        dy = err * (1.0 / d)
        dxn = dy * w_ref[...]
        dx_ref[...] = r * (dxn - xh * jnp.mean(dxn * xh, axis=-1, keepdims=True))

        @pl.when(pl.program_id(0) == 0)
        def _():
            dw_ref[...] = jnp.zeros_like(dw_ref)
            loss_ref[...] = jnp.zeros_like(loss_ref)

        dw_ref[...] += jnp.sum(dy * xh, axis=0, keepdims=True)
        row = jnp.sum(err * err, axis=1, keepdims=True) * (0.5 / d)
        loss_ref[...] += jnp.sum(row, axis=0, keepdims=True)

    loss, dx, dw = pl.pallas_call(
        body, grid=(s // tm,),
        in_specs=[pl.BlockSpec((tm, d), lambda i: (i, 0)), pl.BlockSpec((1, d), lambda i: (0, 0)),
                  pl.BlockSpec((tm, d), lambda i: (i, 0))],
        out_specs=[pl.BlockSpec((1, 1), lambda i: (0, 0)), pl.BlockSpec((tm, d), lambda i: (i, 0)),
                   pl.BlockSpec((1, d), lambda i: (0, 0))],
        out_shape=[jax.ShapeDtypeStruct((1, 1), F32), jax.ShapeDtypeStruct((s, d), F32), jax.ShapeDtypeStruct((1, d), F32)],
        compiler_params=_cparams(1), name=name,
    )(x, w.reshape(1, d), target)
    return loss[0, 0], dx, dw.reshape(d)


def _shift_down(x, sh, t_idx):
    return jnp.where(t_idx >= sh, pltpu.roll(x, sh, 0), 0.0)


def _shift_up(x, sh, t_idx, s):
    return jnp.where(t_idx < s - sh, pltpu.roll(x, s - sh, 0), 0.0)


def _conv_pre(x, w_rows, b, t_idx):
    c = w_rows[CONV_K - 1] * x + b
    for sh in range(1, CONV_K):
        c = c + w_rows[CONV_K - 1 - sh] * _shift_down(x, sh, t_idx)
    return c


def _conv_fwd(src, col0, w, b, n_l2, *, name):
    s = src.shape[0]
    c_tot = w.shape[1]
    nblk = c_tot // LANES

    def body(x_ref, w_ref, b_ref, o_ref):
        j = pl.program_id(0)
        t_idx = lax.broadcasted_iota(jnp.int32, (s, LANES), 0)
        w_rows = [w_ref[kk:kk + 1, :] for kk in range(CONV_K)]
        y = _silu(_conv_pre(x_ref[...], w_rows, b_ref[...], t_idx))
        if n_l2 > 0:
            yn = y * lax.rsqrt(jnp.sum(y * y, axis=1, keepdims=True) + EPS)
            y = jnp.where(j < n_l2, yn, y)
        o_ref[...] = y

    return pl.pallas_call(
        body, grid=(nblk,),
        in_specs=[pl.BlockSpec((s, LANES), lambda j: (0, col0 + j)), pl.BlockSpec((CONV_K, LANES), lambda j: (0, j)),
                  pl.BlockSpec((1, LANES), lambda j: (0, j))],
        out_specs=pl.BlockSpec((s, LANES), lambda j: (0, j)),
        out_shape=jax.ShapeDtypeStruct((s, c_tot), F32),
        compiler_params=_cparams(1), name=name,
    )(src, w, b)


def _conv_bwd(src, col0, w, b, n_l2, dout, *, name):
    s = src.shape[0]
    c_tot = w.shape[1]
    nblk = c_tot // LANES

    def body(x_ref, w_ref, b_ref, do_ref, dx_ref, dw_ref, db_ref):
        j = pl.program_id(0)
        t_idx = lax.broadcasted_iota(jnp.int32, (s, LANES), 0)
        xv = x_ref[...]
        w_rows = [w_ref[kk:kk + 1, :] for kk in range(CONV_K)]
        c = _conv_pre(xv, w_rows, b_ref[...], t_idx)
        dy = do_ref[...]
        if n_l2 > 0:
            y = _silu(c)
            r = lax.rsqrt(jnp.sum(y * y, axis=1, keepdims=True) + EPS)
            dyn = r * dy - y * (r * r * r) * jnp.sum(dy * y, axis=1, keepdims=True)
            dy = jnp.where(j < n_l2, dyn, dy)
        dc = dy * _silu_grad(c)
        dx = w_rows[CONV_K - 1] * dc
        rows = [None] * CONV_K
        rows[CONV_K - 1] = jnp.sum(dc * xv, axis=0, keepdims=True)
        for sh in range(1, CONV_K):
            dx = dx + w_rows[CONV_K - 1 - sh] * _shift_up(dc, sh, t_idx, s)
            rows[CONV_K - 1 - sh] = jnp.sum(dc * _shift_down(xv, sh, t_idx), axis=0, keepdims=True)
        dx_ref[...] = dx.astype(dx_ref.dtype)
        for kk in range(CONV_K):
            dw_ref[kk:kk + 1, :] = rows[kk]
        db_ref[...] = jnp.sum(dc, axis=0, keepdims=True)

    return pl.pallas_call(
        body, grid=(nblk,),
        in_specs=[pl.BlockSpec((s, LANES), lambda j: (0, col0 + j)), pl.BlockSpec((CONV_K, LANES), lambda j: (0, j)),
                  pl.BlockSpec((1, LANES), lambda j: (0, j)), pl.BlockSpec((s, LANES), lambda j: (0, j))],
        out_specs=[pl.BlockSpec((s, LANES), lambda j: (0, j)), pl.BlockSpec((CONV_K, LANES), lambda j: (0, j)),
                   pl.BlockSpec((1, LANES), lambda j: (0, j))],
        out_shape=[jax.ShapeDtypeStruct((s, c_tot), MXU_DTYPE), jax.ShapeDtypeStruct((CONV_K, c_tot), F32),
                   jax.ShapeDtypeStruct((1, c_tot), F32)],
        compiler_params=_cparams(1), name=name,
    )(src, w, b, dout)


def _chunk_masks(c):
    ii = lax.broadcasted_iota(jnp.int32, (c, c), 0)
    jj = lax.broadcasted_iota(jnp.int32, (c, c), 1)
    return ii, jj


def _row_to_col(row, eye):
    return jnp.sum(jnp.where(eye, row, 0.0), axis=1, keepdims=True)


def _dn_chunk(q, k, v, a_row, b_row, alog, dtb, s0):
    c = q.shape[0]
    ii, jj = _chunk_masks(c)
    causal, strict, eye = ii >= jj, ii > jj, ii == jj
    g_row = -jnp.exp(alog) * _softplus(a_row + dtb)
    beta_col = _row_to_col(_sigmoid(b_row), eye)
    g_col = _row_to_col(g_row, eye)
    gc_col = jnp.sum(jnp.where(causal, g_row, 0.0), axis=1, keepdims=True)
    gc_row = jnp.sum(jnp.where(jj >= ii, g_col, 0.0), axis=0, keepdims=True)
    decay = jnp.exp(jnp.where(causal, gc_col - gc_row, NEG_BIG))
    kb = k * beta_col
    vb = v * beta_col
    nmat = -jnp.where(strict, _hdot(kb, k, NT) * decay, 0.0)
    xinv = jnp.where(eye, 1.0, 0.0) + nmat
    pw = nmat
    for _ in range(int(math.log2(c)) - 1):
        pw = _hdot(pw, pw)
        xinv = xinv + _hdot(xinv, pw)
    egc = jnp.exp(gc_col)
    u = _hdot(xinv, vb)
    w = _hdot(xinv, kb * egc)
    qs = q * (q.shape[1] ** -0.5)
    attn = _hdot(qs, k, NT) * decay
    gl = jnp.sum(g_row, axis=1, keepdims=True)
    kd = k * jnp.exp(gl - gc_col)
    v_new = u - _hdot(w, s0)
    o = _hdot(qs * egc, s0) + _hdot(attn, v_new)
    s1 = s0 * jnp.exp(gl) + _hdot(kd, v_new, TN)
    return o, s1


def _dn_specs(nh, nc, rev):
    n_of = (lambda n: nc - 1 - n) if rev else (lambda n: n)
    qkv = [pl.BlockSpec((CHUNK, DN_HEAD_DIM), (lambda h, n, o=o: (n_of(n), o * nh + h))) for o in range(3)]
    row = pl.BlockSpec((None, None, 1, CHUNK), lambda h, n: (h, n_of(n), 0, 0))
    scal = pl.BlockSpec((None, 1, 1), lambda h, n: (h, 0, 0))
    o_spec = pl.BlockSpec((CHUNK, DN_HEAD_DIM), lambda h, n: (n_of(n), h))
    st = pl.BlockSpec((None, None, DN_HEAD_DIM, DN_HEAD_DIM), lambda h, n: (h, n_of(n), 0, 0))
    return qkv, row, scal, o_spec, st


def _dn_fwd(qkv, a_rows, b_rows, alog, dtb, *, name):
    s = qkv.shape[0]
    nh, nc = a_rows.shape[0], a_rows.shape[1]
    qkv_specs, row, scal, o_spec, st = _dn_specs(nh, nc, False)

    def body(q_ref, k_ref, v_ref, a_ref, b_ref, al_ref, dt_ref, o_ref, st_ref, state):
        @pl.when(pl.program_id(1) == 0)
        def _():
            state[...] = jnp.zeros_like(state)

        s0 = state[...]
        st_ref[...] = s0
        o, s1 = _dn_chunk(q_ref[...], k_ref[...], v_ref[...], a_ref[...], b_ref[...], al_ref[...], dt_ref[...], s0)
        o_ref[...] = o
        state[...] = s1

    return pl.pallas_call(
        body, grid=(nh, nc),
        in_specs=qkv_specs + [row, row, scal, scal],
        out_specs=[o_spec, st],
        out_shape=[jax.ShapeDtypeStruct((s, nh * DN_HEAD_DIM), F32),
                   jax.ShapeDtypeStruct((nh, nc, DN_HEAD_DIM, DN_HEAD_DIM), F32)],
        scratch_shapes=[pltpu.VMEM((DN_HEAD_DIM, DN_HEAD_DIM), F32)],
        compiler_params=_cparams(2), name=name,
    )(qkv, qkv, qkv, a_rows, b_rows, alog, dtb)


def _dn_bwd(qkv, a_rows, b_rows, alog, dtb, states, do, *, name):
    s = qkv.shape[0]
    nh, nc = a_rows.shape[0], a_rows.shape[1]
    qkv_specs, row, scal, o_spec, st = _dn_specs(nh, nc, True)

    def body(q_ref, k_ref, v_ref, a_ref, b_ref, al_ref, dt_ref, st_ref, do_ref,
             dq_ref, dk_ref, dv_ref, da_ref, db_ref, dal_ref, ddt_ref, dstate):
        @pl.when(pl.program_id(1) == 0)
        def _():
            dstate[...] = jnp.zeros_like(dstate)
            dal_ref[...] = jnp.zeros_like(dal_ref)
            ddt_ref[...] = jnp.zeros_like(ddt_ref)

        args = (q_ref[...], k_ref[...], v_ref[...], a_ref[...], b_ref[...], al_ref[...], dt_ref[...], st_ref[...])
        _, vjp = jax.vjp(_dn_chunk, *args)
        dq, dk, dv, da, db, dal, ddt, ds0 = vjp((do_ref[...], dstate[...]))
        dq_ref[...] = dq
        dk_ref[...] = dk
        dv_ref[...] = dv
        da_ref[...] = da
        db_ref[...] = db
        dal_ref[...] += dal
        ddt_ref[...] += ddt
        dstate[...] = ds0

    w = nh * DN_HEAD_DIM
    outs = pl.pallas_call(
        body, grid=(nh, nc),
        in_specs=qkv_specs + [row, row, scal, scal, st, o_spec],
        out_specs=[o_spec, o_spec, o_spec, row, row, scal, scal],
        out_shape=[jax.ShapeDtypeStruct((s, w), F32)] * 3
        + [jax.ShapeDtypeStruct(a_rows.shape, F32)] * 2 + [jax.ShapeDtypeStruct((nh, 1, 1), F32)] * 2,
        scratch_shapes=[pltpu.VMEM((DN_HEAD_DIM, DN_HEAD_DIM), F32)],
        compiler_params=_cparams(2), name=name,
    )(qkv, qkv, qkv, a_rows, b_rows, alog, dtb, states, do)
    return outs


def _dn_post_fwd(o, src, gate_col0, nw, *, name, tm=256):
    s, w = o.shape
    nh = w // DN_HEAD_DIM

    def body(o_ref, g_ref, w_ref, y_ref):
        ov = o_ref[...]
        r = lax.rsqrt(jnp.mean(ov * ov, axis=-1, keepdims=True) + EPS)
        y_ref[...] = (ov * r * w_ref[...] * _silu(g_ref[...])).astype(y_ref.dtype)

    blk = pl.BlockSpec((tm, DN_HEAD_DIM), lambda i, h: (i, h))
    return pl.pallas_call(
        body, grid=(s // tm, nh),
        in_specs=[blk, pl.BlockSpec((tm, DN_HEAD_DIM), lambda i, h: (i, gate_col0 + h)),
                  pl.BlockSpec((1, DN_HEAD_DIM), lambda i, h: (0, 0))],
        out_specs=blk, out_shape=jax.ShapeDtypeStruct((s, w), MXU_DTYPE),
        compiler_params=_cparams(2), name=name,
    )(o, src, nw.reshape(1, DN_HEAD_DIM))


def _dn_post_bwd(o, src, gate_col0, nw, dy, *, name, tm=256):
    s, w = o.shape
    nh = w // DN_HEAD_DIM

    def body(o_ref, g_ref, w_ref, dy_ref, do_ref, dg_ref, dw_ref):
        ov = o_ref[...]
        gv = g_ref[...]
        dyv = dy_ref[...]
        r = lax.rsqrt(jnp.mean(ov * ov, axis=-1, keepdims=True) + EPS)
        oh = ov * r
        dn = dyv * _silu(gv)
        dg_ref[...] = (dyv * (oh * w_ref[...]) * _silu_grad(gv)).astype(dg_ref.dtype)
        don = dn * w_ref[...]
        do_ref[...] = r * (don - oh * jnp.mean(don * oh, axis=-1, keepdims=True))

        @pl.when((pl.program_id(0) == 0) & (pl.program_id(1) == 0))
        def _():
            dw_ref[...] = jnp.zeros_like(dw_ref)

        dw_ref[...] += jnp.sum(dn * oh, axis=0, keepdims=True)

    blk = pl.BlockSpec((tm, DN_HEAD_DIM), lambda i, h: (i, h))
    wspec = pl.BlockSpec((1, DN_HEAD_DIM), lambda i, h: (0, 0))
    do, dg, dw = pl.pallas_call(
        body, grid=(s // tm, nh),
        in_specs=[blk, pl.BlockSpec((tm, DN_HEAD_DIM), lambda i, h: (i, gate_col0 + h)), wspec, blk],
        out_specs=[blk, blk, wspec],
        out_shape=[jax.ShapeDtypeStruct((s, w), F32), jax.ShapeDtypeStruct((s, w), MXU_DTYPE),
                   jax.ShapeDtypeStruct((1, DN_HEAD_DIM), F32)],
        compiler_params=_cparams(2), name=name,
    )(o, src, nw.reshape(1, DN_HEAD_DIM), dy)
    return do, dg, dw.reshape(DN_HEAD_DIM)


def _sb_consts():
    r = lax.broadcasted_iota(jnp.int32, (SB_BLOCK, SB_BLOCK), 0)
    c = lax.broadcasted_iota(jnp.int32, (SB_BLOCK, SB_BLOCK), 1)
    lane_head = c // SB_HEAD_DIM
    return r, c, lane_head


def _sb_fwd(src, col0, width, *, name):
    s = src.shape[0]
    nq = s // SB_BLOCK
    npair = width // LANES
    scale = SB_HEAD_DIM ** -0.5

    def body(q_ref, k_ref, v_ref, o_ref, r_ref):
        i = pl.program_id(1)
        r, c, lane_head = _sb_consts()
        m_gt = jnp.where(r > c, 1.0, 0.0).astype(BF16)
        qv = q_ref[...]
        o_acc = jnp.zeros((SB_BLOCK, LANES), F32)
        r_out = jnp.zeros((SB_BLOCK, LANES), F32)
        for h in range(LANES // SB_HEAD_DIM):
            lm = lane_head == h
            qh = jnp.where(lm, qv, 0.0).astype(MXU_DTYPE)

            def step(jr, carry, lm=lm, qh=qh):
                o_h, rsum = carry
                j = i - jr
                off = pl.multiple_of(j * SB_BLOCK, SB_BLOCK)
                kb = k_ref[pl.ds(off, SB_BLOCK), :]
                vb = v_ref[pl.ds(off, SB_BLOCK), :]
                z = _dot(qh, kb.astype(MXU_DTYPE), NT) * scale
                mask = (j * SB_BLOCK + c) < (i * SB_BLOCK + r)
                t = jnp.log1p(jnp.exp(-jnp.abs(z)))
                lk = jnp.where(mask, -(jnp.maximum(z, 0.0) + t), 0.0)
                loga = (jnp.minimum(z, 0.0) - t) + rsum + _split_dot(lk, m_gt)
                wgt = jnp.where(mask, jnp.exp(loga), 0.0)
                o_h = o_h + _dot(wgt.astype(MXU_DTYPE), jnp.where(lm, vb, 0.0).astype(MXU_DTYPE), NN)
                rsum = rsum + jnp.sum(lk, axis=1, keepdims=True)
                return o_h, rsum

            o_h, rsum = lax.fori_loop(0, i + 1, step, (jnp.zeros((SB_BLOCK, LANES), F32), jnp.zeros((SB_BLOCK, 1), F32)))
            o_acc = o_acc + o_h
            r_out = jnp.where(lm, rsum, r_out)
        o_ref[...] = o_acc
        r_ref[...] = r_out

    blk = pl.BlockSpec((SB_BLOCK, LANES), lambda p, i: (i, p))
    return pl.pallas_call(
        body, grid=(npair, nq),
        in_specs=[pl.BlockSpec((SB_BLOCK, LANES), lambda p, i: (i, col0 + p)),
                  pl.BlockSpec((s, LANES), lambda p, i: (0, col0 + npair + p)),
                  pl.BlockSpec((s, LANES), lambda p, i: (0, col0 + 2 * npair + p))],
        out_specs=[blk, blk],
        out_shape=[jax.ShapeDtypeStruct((s, width), F32), jax.ShapeDtypeStruct((s, width), F32)],
        compiler_params=_cparams(2), name=name,
    )(src, src, src)


def _sb_bwd(src, col0, width, rtot, do, *, name):
    s = src.shape[0]
    nq = s // SB_BLOCK
    npair = width // LANES
    scale = SB_HEAD_DIM ** -0.5

    def body(q_ref, k_ref, v_ref, r_ref, do_ref, dq_ref, dk_ref, dv_ref):
        i = pl.program_id(1)

        @pl.when(i == 0)
        def _():
            dk_ref[...] = jnp.zeros_like(dk_ref)
            dv_ref[...] = jnp.zeros_like(dv_ref)

        r, c, lane_head = _sb_consts()
        m_gt = jnp.where(r > c, 1.0, 0.0).astype(BF16)
        m_lt = jnp.where(r < c, 1.0, 0.0).astype(BF16)
        qv = q_ref[...]
        dov = do_ref[...]
        rv = r_ref[...]
        dq_acc = jnp.zeros((SB_BLOCK, LANES), F32)
        for h in range(LANES // SB_HEAD_DIM):
            lm = lane_head == h
            qh = jnp.where(lm, qv, 0.0).astype(MXU_DTYPE)
            doh = jnp.where(lm, dov, 0.0).astype(MXU_DTYPE)
            rt = jnp.max(jnp.where(lm, rv, NEG_BIG), axis=1, keepdims=True)

            def step(j, carry, lm=lm, qh=qh, doh=doh, rt=rt):
                dq_h, psum, csum = carry
                off = pl.multiple_of(j * SB_BLOCK, SB_BLOCK)
                kb = jnp.where(lm, k_ref[pl.ds(off, SB_BLOCK), :], 0.0).astype(MXU_DTYPE)
                vb = jnp.where(lm, v_ref[pl.ds(off, SB_BLOCK), :], 0.0).astype(MXU_DTYPE)
                z = _dot(qh, kb, NT) * scale
                mask = (j * SB_BLOCK + c) < (i * SB_BLOCK + r)
                t = jnp.log1p(jnp.exp(-jnp.abs(z)))
                lk = jnp.where(mask, -(jnp.maximum(z, 0.0) + t), 0.0)
                lsum = jnp.sum(lk, axis=1, keepdims=True)
                logsig = jnp.minimum(z, 0.0) - t
                reach = (rt - psum - lsum) + _split_dot(lk, m_gt)
                wgt = jnp.where(mask, jnp.exp(logsig + reach), 0.0)
                dw = _dot(doh, vb, NT)
                dloga = wgt * dw
                dv_blk = _dot(wgt.astype(MXU_DTYPE), doh, TN)
                sig = jnp.exp(logsig)
                dlk = jnp.where(mask, csum + _split_dot(dloga, m_lt), 0.0)
                dz = (dloga * (1.0 - sig) - dlk * sig) * scale
                dzb = dz.astype(MXU_DTYPE)
                dq_h = dq_h + _dot(dzb, kb, NN)
                dk_blk = _dot(dzb, qh, TN)
                dk_ref[pl.ds(off, SB_BLOCK), :] += dk_blk
                dv_ref[pl.ds(off, SB_BLOCK), :] += dv_blk
                return dq_h, psum + lsum, csum + jnp.sum(dloga, axis=1, keepdims=True)

            zero_col = jnp.zeros((SB_BLOCK, 1), F32)
            dq_h, _, _ = lax.fori_loop(0, i + 1, step, (jnp.zeros((SB_BLOCK, LANES), F32), zero_col, zero_col))
            dq_acc = dq_acc + dq_h
        dq_ref[...] = dq_acc

    blk = pl.BlockSpec((SB_BLOCK, LANES), lambda p, i: (i, p))
    full = pl.BlockSpec((s, LANES), lambda p, i: (0, p))
    return pl.pallas_call(
        body, grid=(npair, nq),
        in_specs=[pl.BlockSpec((SB_BLOCK, LANES), lambda p, i: (i, col0 + p)),
                  pl.BlockSpec((s, LANES), lambda p, i: (0, col0 + npair + p)),
                  pl.BlockSpec((s, LANES), lambda p, i: (0, col0 + 2 * npair + p)),
                  blk, blk],
        out_specs=[blk, full, full],
        out_shape=[jax.ShapeDtypeStruct((s, width), F32)] * 3,
        compiler_params=_cparams(2), name=name,
    )(src, src, src, rtot, do)


def _ssd_group(xs, dt_rows, alogs, dtbs, bm, cm, h0s):
    c = bm.shape[0]
    ii, jj = _chunk_masks(c)
    causal, eye = ii >= jj, ii == jj
    scores = _hdot(cm, bm, NT)
    ys, h1s = [], []
    for x, dt_row, alog, dtb, h0 in zip(xs, dt_rows, alogs, dtbs, h0s):
        dt_r = _softplus(dt_row + dtb)
        a_r = -jnp.exp(alog) * dt_r
        dt_col = _row_to_col(dt_r, eye)
        a_col = _row_to_col(a_r, eye)
        ac_col = jnp.sum(jnp.where(causal, a_r, 0.0), axis=1, keepdims=True)
        ac_row = jnp.sum(jnp.where(jj >= ii, a_col, 0.0), axis=0, keepdims=True)
        lmat = jnp.exp(jnp.where(causal, ac_col - ac_row, NEG_BIG))
        xdt = x * dt_col
        al = jnp.sum(a_r, axis=1, keepdims=True)
        y = _hdot(scores * lmat, xdt) + _hdot(cm, h0, NT) * jnp.exp(ac_col)
        h1 = h0 * jnp.exp(al) + _hdot(xdt * jnp.exp(al - ac_col), bm, TN)
        ys.append(y)
        h1s.append(h1)
    return ys, h1s


def _ssd_specs(ng, nc, r, rev):
    n_of = (lambda n: nc - 1 - n) if rev else (lambda n: n)
    gw = r * SSM_HEAD_DIM
    x_spec = pl.BlockSpec((CHUNK, gw), lambda g, n: (n_of(n), g))
    b_spec = pl.BlockSpec((CHUNK, SSM_STATE), lambda g, n: (n_of(n), (ng * gw) // SSM_STATE + g))
    c_spec = pl.BlockSpec((CHUNK, SSM_STATE), lambda g, n: (n_of(n), (ng * gw) // SSM_STATE + ng + g))
    dt_spec = pl.BlockSpec((None, None, r, CHUNK), lambda g, n: (g, n_of(n), 0, 0))
    sc_spec = pl.BlockSpec((None, r, 1), lambda g, n: (g, 0, 0))
    st_spec = pl.BlockSpec((None, None, r, SSM_HEAD_DIM, SSM_STATE), lambda g, n: (g, n_of(n), 0, 0, 0))
    y_spec = pl.BlockSpec((CHUNK, gw), lambda g, n: (n_of(n), g))
    bc_out = pl.BlockSpec((CHUNK, SSM_STATE), lambda g, n: (n_of(n), g))
    return x_spec, b_spec, c_spec, dt_spec, sc_spec, st_spec, y_spec, bc_out


def _ssd_fwd(xbc, dt_rows, alog, dtb, *, name):
    s = xbc.shape[0]
    ng, nc, r = dt_rows.shape[0], dt_rows.shape[1], dt_rows.shape[2]
    w = ng * r * SSM_HEAD_DIM
    x_spec, b_spec, c_spec, dt_spec, sc_spec, st_spec, y_spec, _ = _ssd_specs(ng, nc, r, False)
    p = SSM_HEAD_DIM

    def body(x_ref, b_ref, c_ref, dt_ref, al_ref, db_ref, y_ref, st_ref, state):
        @pl.when(pl.program_id(1) == 0)
        def _():
            state[...] = jnp.zeros_like(state)

        st_ref[...] = state[...]
        xs = [x_ref[:, h * p:(h + 1) * p] for h in range(r)]
        dts = [dt_ref[h:h + 1, :] for h in range(r)]
        als = [al_ref[h:h + 1, :] for h in range(r)]
        dbs = [db_ref[h:h + 1, :] for h in range(r)]
        h0s = [state[h] for h in range(r)]
        ys, h1s = _ssd_group(xs, dts, als, dbs, b_ref[...], c_ref[...], h0s)
        for h in range(r):
            y_ref[:, h * p:(h + 1) * p] = ys[h]
            state[h] = h1s[h]

    return pl.pallas_call(
        body, grid=(ng, nc),
        in_specs=[x_spec, b_spec, c_spec, dt_spec, sc_spec, sc_spec],
        out_specs=[y_spec, st_spec],
        out_shape=[jax.ShapeDtypeStruct((s, w), F32), jax.ShapeDtypeStruct((ng, nc, r, p, SSM_STATE), F32)],
        scratch_shapes=[pltpu.VMEM((r, p, SSM_STATE), F32)],
        compiler_params=_cparams(2), name=name,
    )(xbc, xbc, xbc, dt_rows, alog, dtb)


def _ssd_bwd(xbc, dt_rows, alog, dtb, states, dy, *, name):
    s = xbc.shape[0]
    ng, nc, r = dt_rows.shape[0], dt_rows.shape[1], dt_rows.shape[2]
    w = ng * r * SSM_HEAD_DIM
    x_spec, b_spec, c_spec, dt_spec, sc_spec, st_spec, y_spec, bc_out = _ssd_specs(ng, nc, r, True)
    p = SSM_HEAD_DIM

    def body(x_ref, b_ref, c_ref, dt_ref, al_ref, db_ref, st_ref, dy_ref,
             dx_ref, dbm_ref, dcm_ref, ddt_ref, dal_ref, ddb_ref, dstate):
        @pl.when(pl.program_id(1) == 0)
        def _():
            dstate[...] = jnp.zeros_like(dstate)
            dal_ref[...] = jnp.zeros_like(dal_ref)
            ddb_ref[...] = jnp.zeros_like(ddb_ref)

        xs = [x_ref[:, h * p:(h + 1) * p] for h in range(r)]
        dts = [dt_ref[h:h + 1, :] for h in range(r)]
        als = [al_ref[h:h + 1, :] for h in range(r)]
        dbs = [db_ref[h:h + 1, :] for h in range(r)]
        h0s = [st_ref[h] for h in range(r)]
        _, vjp = jax.vjp(_ssd_group, xs, dts, als, dbs, b_ref[...], c_ref[...], h0s)
        dys = [dy_ref[:, h * p:(h + 1) * p] for h in range(r)]
        dh1s = [dstate[h] for h in range(r)]
        dxs, ddts, dals, ddbs, dbm, dcm, dh0s = vjp((dys, dh1s))
        dbm_ref[...] = dbm
        dcm_ref[...] = dcm
        for h in range(r):
            dx_ref[:, h * p:(h + 1) * p] = dxs[h]
            ddt_ref[h:h + 1, :] = ddts[h]
            dal_ref[h:h + 1, :] += dals[h]
            ddb_ref[h:h + 1, :] += ddbs[h]
            dstate[h] = dh0s[h]

    gn = ng * SSM_STATE
    return pl.pallas_call(
        body, grid=(ng, nc),
        in_specs=[x_spec, b_spec, c_spec, dt_spec, sc_spec, sc_spec, st_spec, y_spec],
        out_specs=[y_spec, bc_out, bc_out, dt_spec, sc_spec, sc_spec],
        out_shape=[jax.ShapeDtypeStruct((s, w), F32), jax.ShapeDtypeStruct((s, gn), F32), jax.ShapeDtypeStruct((s, gn), F32),
                   jax.ShapeDtypeStruct(dt_rows.shape, F32), jax.ShapeDtypeStruct((ng, r, 1), F32),
                   jax.ShapeDtypeStruct((ng, r, 1), F32)],
        scratch_shapes=[pltpu.VMEM((r, p, SSM_STATE), F32)],
        compiler_params=_cparams(2), name=name,
    )(xbc, xbc, xbc, dt_rows, alog, dtb, states, dy)


def _ssm_post_fwd(y, xbc, src, z_col0, dexp, nw, *, name, tm=256):
    s, w = y.shape
    gw = w // SSM_GROUPS
    zc = z_col0 * LANES // gw

    def body(y_ref, x_ref, z_ref, d_ref, w_ref, o_ref):
        yy = (y_ref[...] + x_ref[...] * d_ref[...]) * _silu(z_ref[...])
        r = lax.rsqrt(jnp.mean(yy * yy, axis=-1, keepdims=True) + EPS)
        o_ref[...] = (yy * r * w_ref[...]).astype(o_ref.dtype)

    blk = pl.BlockSpec((tm, gw), lambda g, i: (i, g))
    vec = pl.BlockSpec((1, gw), lambda g, i: (0, g))
    return pl.pallas_call(
        body, grid=(SSM_GROUPS, s // tm),
        in_specs=[blk, blk, pl.BlockSpec((tm, gw), lambda g, i: (i, zc + g)), vec, vec],
        out_specs=blk, out_shape=jax.ShapeDtypeStruct((s, w), MXU_DTYPE),
        compiler_params=_cparams(2), name=name,
    )(y, xbc, src, dexp.reshape(1, w), nw.reshape(1, w))


def _ssm_post_bwd(y, xbc, src, z_col0, dexp, nw, dout, *, name, tm=256):
    s, w = y.shape
    gw = w // SSM_GROUPS
    zc = z_col0 * LANES // gw

    def body(y_ref, x_ref, z_ref, d_ref, w_ref, do_ref, dy_ref, dx_ref, dz_ref, dd_ref, dw_ref):
        xv, zv, dv = x_ref[...], z_ref[...], d_ref[...]
        pre = y_ref[...] + xv * dv
        sz = _silu(zv)
        yy = pre * sz
        r = lax.rsqrt(jnp.mean(yy * yy, axis=-1, keepdims=True) + EPS)
        yh = yy * r
        dov = do_ref[...]
        dyn = dov * w_ref[...]
        dyy = r * (dyn - yh * jnp.mean(dyn * yh, axis=-1, keepdims=True))
        dpre = dyy * sz
        dy_ref[...] = dpre
        dx_ref[...] = dpre * dv
        dz_ref[...] = (dyy * pre * _silu_grad(zv)).astype(dz_ref.dtype)

        @pl.when(pl.program_id(1) == 0)
        def _():
            dd_ref[...] = jnp.zeros_like(dd_ref)
            dw_ref[...] = jnp.zeros_like(dw_ref)

        dd_ref[...] += jnp.sum(dpre * xv, axis=0, keepdims=True)
        dw_ref[...] += jnp.sum(dov * yh, axis=0, keepdims=True)

    blk = pl.BlockSpec((tm, gw), lambda g, i: (i, g))
    vec = pl.BlockSpec((1, gw), lambda g, i: (0, g))
    dy, dx, dz, dd, dw = pl.pallas_call(
        body, grid=(SSM_GROUPS, s // tm),
        in_specs=[blk, blk, pl.BlockSpec((tm, gw), lambda g, i: (i, zc + g)), vec, vec, blk],
        out_specs=[blk, blk, blk, vec, vec],
        out_shape=[jax.ShapeDtypeStruct((s, w), F32), jax.ShapeDtypeStruct((s, w), F32), jax.ShapeDtypeStruct((s, w), MXU_DTYPE),
                   jax.ShapeDtypeStruct((1, w), F32), jax.ShapeDtypeStruct((1, w), F32)],
        compiler_params=_cparams(2), name=name,
    )(y, xbc, src, dexp.reshape(1, w), nw.reshape(1, w), dout)
    return dy, dx, dz, dd.reshape(w), dw.reshape(w)


def _merge_fwd(proj3, src, gate_col0, d, *, name, tm=256):
    s = proj3.shape[0]
    nb = proj3.shape[1] // d
    gc = gate_col0 * LANES // d

    def body(*refs):
        p_refs, g_refs, o_ref = refs[:nb], refs[nb:2 * nb], refs[-1]
        acc = None
        for p_ref, g_ref in zip(p_refs, g_refs):
            term = _sigmoid(g_ref[...]) * p_ref[...]
            acc = term if acc is None else acc + term
        o_ref[...] = acc.astype(o_ref.dtype)

    p_specs = [pl.BlockSpec((tm, d), lambda i, b=b: (i, b)) for b in range(nb)]
    g_specs = [pl.BlockSpec((tm, d), lambda i, b=b: (i, gc + b)) for b in range(nb)]
    return pl.pallas_call(
        body, grid=(s // tm,), in_specs=p_specs + g_specs,
        out_specs=pl.BlockSpec((tm, d), lambda i: (i, 0)), out_shape=jax.ShapeDtypeStruct((s, d), MXU_DTYPE),
        compiler_params=_cparams(1), name=name,
    )(*([proj3] * nb), *([src] * nb))


def _merge_bwd(proj3, src, gate_col0, d, dmerged, *, name, tm=256):
    s = proj3.shape[0]
    nb = proj3.shape[1] // d
    gc = gate_col0 * LANES // d

    def body(p_ref, g_ref, dm_ref, dp_ref, dg_ref):
        sg = _sigmoid(g_ref[...])
        dm = dm_ref[...]
        dp_ref[...] = (dm * sg).astype(dp_ref.dtype)
        dg_ref[...] = (dm * p_ref[...] * sg * (1.0 - sg)).astype(dg_ref.dtype)

    blk = pl.BlockSpec((tm, d), lambda i, b: (i, b))
    return pl.pallas_call(
        body, grid=(s // tm, nb),
        in_specs=[blk, pl.BlockSpec((tm, d), lambda i, b: (i, gc + b)), pl.BlockSpec((tm, d), lambda i, b: (i, 0))],
        out_specs=[blk, blk],
        out_shape=[jax.ShapeDtypeStruct(proj3.shape, MXU_DTYPE), jax.ShapeDtypeStruct(proj3.shape, MXU_DTYPE)],
        compiler_params=_cparams(2), name=name,
    )(proj3, src, dmerged)


ANY = pl.BlockSpec(memory_space=pl.ANY)
MESH = pl.DeviceIdType.MESH


def _all_gather(shard, *, name):
    def body(x_ref, out_ref, send_sems, recv_sems, local_sem):
        x, y, c = lax.axis_index("x"), lax.axis_index("y"), lax.axis_index("c")
        me, sibling = (x, y, c), (x, y, 1 - c)
        chips = [(1 - x, y), (x, 1 - y), (1 - x, 1 - y)]

        def rows(px, py, pc):
            return out_ref.at[4 * px + 2 * py + pc]

        def copy(k, block, to, src=None):
            return pltpu.make_async_remote_copy(
                src_ref=rows(*block) if src is None else src, dst_ref=rows(*block),
                send_sem=send_sems.at[k], recv_sem=recv_sems.at[k], device_id=to, device_id_type=MESH)

        mine = pltpu.make_async_copy(x_ref, rows(*me), local_sem)
        mine.start()
        first = [copy(0, me, sibling, src=x_ref)]
        first += [copy(1 + j, me, (*chip, c), src=x_ref) for j, chip in enumerate(chips)]
        for cp in first:
            cp.start()
        passed = [copy(4 + j, (*chip, c), sibling) for j, chip in enumerate(chips)]
        for j, chip in enumerate(chips):
            copy(1 + j, (*chip, c), me).wait_recv()
            passed[j].start()
        copy(0, sibling, me).wait_recv()
        for j, chip in enumerate(chips):
            copy(4 + j, (*chip, 1 - c), me).wait_recv()
        for cp in first + passed:
            cp.wait_send()
        mine.wait()

    return pl.pallas_call(
        body, out_shape=jax.ShapeDtypeStruct((N_DEV,) + shard.shape, shard.dtype),
        in_specs=[ANY], out_specs=ANY,
        scratch_shapes=[pltpu.SemaphoreType.DMA((7,)), pltpu.SemaphoreType.DMA((7,)), pltpu.SemaphoreType.DMA],
        name=name,
    )(shard)


def _grad_exchange(big, small, *, name):
    def body(big_ref, small_ref, bigr_ref, smallr_ref, send_sems, recv_sems, local_sems):
        x, y, c = lax.axis_index("x"), lax.axis_index("y"), lax.axis_index("c")
        me = 4 * x + 2 * y + c
        l0 = pltpu.make_async_copy(big_ref.at[me], bigr_ref.at[me], local_sems.at[0])
        l1 = pltpu.make_async_copy(small_ref, smallr_ref.at[me], local_sems.at[1])
        l0.start()
        l1.start()
        copies = []
        for k in range(1, N_DEV):
            px = x ^ ((k >> 2) & 1)
            py = y ^ ((k >> 1) & 1)
            pc = c ^ (k & 1)
            peer = 4 * px + 2 * py + pc
            copies.append(pltpu.make_async_remote_copy(
                src_ref=big_ref.at[peer], dst_ref=bigr_ref.at[me],
                send_sem=send_sems.at[2 * (k - 1)], recv_sem=recv_sems.at[2 * (k - 1)],
                device_id=(px, py, pc), device_id_type=MESH))
            copies.append(pltpu.make_async_remote_copy(
                src_ref=small_ref, dst_ref=smallr_ref.at[me],
                send_sem=send_sems.at[2 * (k - 1) + 1], recv_sem=recv_sems.at[2 * (k - 1) + 1],
                device_id=(px, py, pc), device_id_type=MESH))
        for cp in copies:
            cp.start()
        for cp in copies:
            cp.wait_recv()
        for cp in copies:
            cp.wait_send()
        l0.wait()
        l1.wait()

    return pl.pallas_call(
        body,
        out_shape=[jax.ShapeDtypeStruct(big.shape, big.dtype), jax.ShapeDtypeStruct((N_DEV,) + small.shape, small.dtype)],
        in_specs=[ANY, ANY], out_specs=[ANY, ANY],
        scratch_shapes=[pltpu.SemaphoreType.DMA((14,)), pltpu.SemaphoreType.DMA((14,)), pltpu.SemaphoreType.DMA((2,))],
        name=name,
    )(big, small)


def _adam_math(w, g, m, v):
    m1 = ADAM_B1 * m + (1.0 - ADAM_B1) * g
    v1 = ADAM_B2 * v + (1.0 - ADAM_B2) * (g * g)
    m_hat = m1 / (1.0 - ADAM_B1 ** ADAM_STEP)
    v_hat = v1 / (1.0 - ADAM_B2 ** ADAM_STEP)
    delta = -ADAM_LR * (m_hat / (jnp.sqrt(v_hat) + ADAM_EPS) + ADAM_WD * w)
    return delta, m1, v1


def _sum_adamw(parts, w, m, v, *, name, tr=512):
    rows = w.shape[0]
    tr = _pick(rows, (tr, 256, 128, 64, 32, 16, 8))

    def body(p_ref, w_ref, m_ref, v_ref, g_ref, d_ref, m1_ref, v1_ref):
        g = p_ref[0].astype(F32)
        for src in range(1, N_DEV):
            g = g + p_ref[src].astype(F32)
        delta, m1, v1 = _adam_math(w_ref[...], g, m_ref[...], v_ref[...])
        g_ref[...] = g
        d_ref[...] = delta
        m1_ref[...] = m1
        v1_ref[...] = v1

    blk = pl.BlockSpec((tr, LANES), lambda i: (i, 0))
    return pl.pallas_call(
        body, grid=(rows // tr,),
        in_specs=[pl.BlockSpec((N_DEV, tr, LANES), lambda i: (0, i, 0)), blk, blk, blk],
        out_specs=[blk] * 4, out_shape=[jax.ShapeDtypeStruct((rows, LANES), F32)] * 4,
        compiler_params=_cparams(1), name=name,
    )(parts, w, m, v)


def _sum_parts(parts, *, name):
    rows = parts.shape[1]

    def body(p_ref, o_ref):
        g = p_ref[0]
        for src in range(1, N_DEV):
            g = g + p_ref[src]
        o_ref[...] = g

    return pl.pallas_call(
        body, grid=(1,), in_specs=[pl.BlockSpec((N_DEV, rows, LANES), lambda i: (0, 0, 0))],
        out_specs=pl.BlockSpec((rows, LANES), lambda i: (0, 0)), out_shape=jax.ShapeDtypeStruct((rows, LANES), F32),
        compiler_params=_cparams(1), name=name,
    )(parts)


def _adamw(w, g, m, v, *, name):
    rows = w.shape[0]

    def body(w_ref, g_ref, m_ref, v_ref, d_ref, m1_ref, v1_ref):
        delta, m1, v1 = _adam_math(w_ref[...], g_ref[...], m_ref[...], v_ref[...])
        d_ref[...] = delta
        m1_ref[...] = m1
        v1_ref[...] = v1

    blk = pl.BlockSpec((rows, LANES), lambda i: (0, 0))
    return pl.pallas_call(
        body, grid=(1,), in_specs=[blk] * 4, out_specs=[blk] * 3,
        out_shape=[jax.ShapeDtypeStruct((rows, LANES), F32)] * 3,
        compiler_params=_cparams(1), name=name,
    )(w, g, m, v)


def _pack(arrs, dtype, row_mult=16):
    flat = jnp.concatenate([a.reshape(-1).astype(dtype) for a in arrs])
    n = flat.shape[0]
    rows = -(-n // (LANES * row_mult)) * row_mult
    flat = jnp.pad(flat, (0, rows * LANES - n))
    return flat.reshape(rows, LANES)


def _unpack(packed, shapes):
    flat = packed.reshape(-1)
    out, off = [], 0
    for shp in shapes:
        n = math.prod(shp)
        out.append(flat[off:off + n].reshape(shp))
        off += n
    return out


class _Layout:
    def __init__(self, d):
        self.d = d
        w = d
        self.dn_heads = w // DN_HEAD_DIM
        self.ssm_heads = w // SSM_HEAD_DIM
        gn = SSM_GROUPS * SSM_STATE
        self.sizes = (3 * w, w, self.dn_heads, self.dn_heads, 3 * w, w, w + 2 * gn, self.ssm_heads, 3 * d)
        offs, o = [], 0
        for sz in self.sizes:
            offs.append(o)
            o += sz
        self.offs = offs
        self.in_dim = o
        self.big = (0, 1, 4, 5, 6, 8)
        self.small = (2, 3, 7)
        cols, o = {}, 0
        for idx in self.big:
            cols[idx] = o
            o += self.sizes[idx]
        self.small_col = o
        self.cols = cols
        self.padded = o + LANES
        self.n_small = sum(self.sizes[i] for i in self.small)

    def reorder_w(self, w_in):
        parts = [w_in[:, self.offs[i]:self.offs[i] + self.sizes[i]] for i in self.big + self.small]
        parts.append(jnp.zeros((w_in.shape[0], LANES - self.n_small), w_in.dtype))
        return jnp.concatenate(parts, axis=1)

    def restore_w(self, wp):
        pieces = {}
        for idx in self.big:
            pieces[idx] = wp[:, self.cols[idx]:self.cols[idx] + self.sizes[idx]]
        o = self.small_col
        for idx in self.small:
            pieces[idx] = wp[:, o:o + self.sizes[idx]]
            o += self.sizes[idx]
        return jnp.concatenate([pieces[i] for i in range(len(self.sizes))], axis=1)


def _rows_form(cols_t, nh, nc):
    return cols_t.T.reshape(nh, nc, 1, CHUNK)


def _layer_fwd(x, p, lay, tag):
    s, d = x.shape
    nc = s // CHUNK
    w = d
    dnh, smh = lay.dn_heads, lay.ssm_heads
    r = smh // SSM_GROUPS
    cb = {k: v // LANES for k, v in lay.cols.items()}
    sv = {}
    h1 = _rms_fwd(x, p["norm_mix"], name=f"rms_mix_{tag}")
    proj = _matmul(h1, p["w_in"], name=f"mm_in_{tag}")
    small = proj[:, lay.small_col:lay.small_col + LANES]
    a_rows = _rows_form(small[:, 0:dnh], dnh, nc)
    b_rows = _rows_form(small[:, dnh:2 * dnh], dnh, nc)
    dt_rows = small[:, 2 * dnh:2 * dnh + smh].T.reshape(SSM_GROUPS, r, nc, CHUNK).transpose(0, 2, 1, 3)
    zero_b = jnp.zeros((1, 3 * w), F32)
    dn_qkv = _conv_fwd(proj, cb[0], p["dn_conv_w"], zero_b, 2 * dnh, name=f"dn_conv_{tag}")
    dn_alog = p["dn_a_log"].reshape(dnh, 1, 1)
    dn_dtb = p["dn_dt_bias"].reshape(dnh, 1, 1)
    o_dn, dn_states = _dn_fwd(dn_qkv, a_rows, b_rows, dn_alog, dn_dtb, name=f"dn_chunk_{tag}")
    y_dn = _dn_post_fwd(o_dn, proj, cb[1], p["dn_norm_w"], name=f"dn_post_{tag}")
    o_sb, sb_r = _sb_fwd(proj, cb[4], w, name=f"sb_{tag}")
    xbc = _conv_fwd(proj, cb[6], p["ssm_conv_w"], p["ssm_conv_b"].reshape(1, -1), 0, name=f"ssm_conv_{tag}")
    ssm_alog = p["ssm_a_log"].reshape(SSM_GROUPS, r, 1)
    ssm_dtb = p["ssm_dt_bias"].reshape(SSM_GROUPS, r, 1)
    y_ssd, ssm_states = _ssd_fwd(xbc, dt_rows, ssm_alog, ssm_dtb, name=f"ssd_{tag}")
    dexp = jnp.repeat(p["ssm_d"], SSM_HEAD_DIM)
    y_ssm = _ssm_post_fwd(y_ssd, xbc, proj, cb[5], dexp, p["ssm_norm_w"], name=f"ssm_post_{tag}")
    branches = (y_dn, o_sb, y_ssm)
    proj3 = jnp.concatenate(
        [_matmul(br, p["w_branch"][i], name=f"mm_branch{i}_{tag}") for i, br in enumerate(branches)], axis=1)
    merged = _merge_fwd(proj3, proj, cb[8], d, name=f"merge_{tag}")
    x1 = _matmul(merged, p["w_out"], name=f"mm_out_{tag}", epilogue=lambda acc, res: (acc + res,), extras=(x,))
    h2 = _rms_fwd(x1, p["norm_mlp"], name=f"rms_mlp_{tag}")
    u, act = _matmul(h2, p["w_up"], name=f"mm_up_{tag}", out_dtypes=(F32, MXU_DTYPE),
                     epilogue=lambda acc: (acc, jnp.square(jnp.maximum(acc, 0.0))))
    x2 = _matmul(act, p["w_down"], name=f"mm_down_{tag}", epilogue=lambda acc, res: (acc + res,), extras=(x1,))
    sv.update(x=x, h1=h1, proj=proj, a_rows=a_rows, b_rows=b_rows, dt_rows=dt_rows, dn_qkv=dn_qkv, dn_alog=dn_alog,
              dn_dtb=dn_dtb, o_dn=o_dn, dn_states=dn_states, y_dn=y_dn, o_sb=o_sb, sb_r=sb_r, xbc=xbc, ssm_alog=ssm_alog,
              ssm_dtb=ssm_dtb, y_ssd=y_ssd, ssm_states=ssm_states, dexp=dexp, y_ssm=y_ssm, proj3=proj3, merged=merged,
              x1=x1, h2=h2, u=u, act=act)
    return x2, sv


def _layer_bwd(dx2, p, sv, lay, tag):
    x = sv["x"]
    s, d = x.shape
    nc = s // CHUNK
    w = d
    dnh, smh = lay.dn_heads, lay.ssm_heads
    r = smh // SSM_GROUPS
    gn = SSM_GROUPS * SSM_STATE
    cb = {k: v // LANES for k, v in lay.cols.items()}
    proj = sv["proj"]
    g = {}
    dx2_b = dx2.astype(MXU_DTYPE)
    du = _matmul(dx2_b, p["w_down"], tb=True, name=f"mm_down_dx_{tag}", out_dtypes=(MXU_DTYPE,),
                 epilogue=lambda acc, uu: (acc * (2.0 * jnp.maximum(uu, 0.0)),), extras=(sv["u"],))
    g["w_down"] = _matmul(sv["act"], dx2_b, ta=True, name=f"mm_down_dw_{tag}")
    g["w_up"] = _matmul(sv["h2"], du, ta=True, name=f"mm_up_dw_{tag}")
    dh2 = _matmul(du, p["w_up"], tb=True, name=f"mm_up_dx_{tag}")
    dx1, g["norm_mlp"] = _rms_bwd(sv["x1"], p["norm_mlp"], dh2, dx2, name=f"rms_mlp_bwd_{tag}")
    dx1_b = dx1.astype(MXU_DTYPE)
    dmerged = _matmul(dx1_b, p["w_out"], tb=True, name=f"mm_out_dx_{tag}")
    g["w_out"] = _matmul(sv["merged"], dx1_b, ta=True, name=f"mm_out_dw_{tag}")
    dproj3, dgates = _merge_bwd(sv["proj3"], proj, cb[8], d, dmerged, name=f"merge_bwd_{tag}")
    branches = (sv["y_dn"], sv["o_sb"], sv["y_ssm"])
    dwb, dbr = [], []
    for i, br in enumerate(branches):
        dp_i = dproj3[:, i * d:(i + 1) * d]
        dwb.append(_matmul(br, dp_i, ta=True, name=f"mm_branch{i}_dw_{tag}"))
        dbr.append(_matmul(dp_i, p["w_branch"][i], tb=True, name=f"mm_branch{i}_dx_{tag}"))
    g["w_branch"] = jnp.stack(dwb)
    dy_dn, do_sb, dy_ssm = dbr
    dy_ssd, dxs_skip, dz, ddexp, g["ssm_norm_w"] = _ssm_post_bwd(
        sv["y_ssd"], sv["xbc"], proj, cb[5], sv["dexp"], p["ssm_norm_w"], dy_ssm, name=f"ssm_post_bwd_{tag}")
    g["ssm_d"] = ddexp.reshape(smh, SSM_HEAD_DIM).sum(axis=1)
    dxs, dbm, dcm, ddt_rows, dalog, ddtb = _ssd_bwd(
        sv["xbc"], sv["dt_rows"], sv["ssm_alog"], sv["ssm_dtb"], sv["ssm_states"], dy_ssd, name=f"ssd_bwd_{tag}")
    g["ssm_a_log"] = dalog.reshape(smh)
    g["ssm_dt_bias"] = ddtb.reshape(smh)
    dxbc_post = jnp.concatenate([dxs + dxs_skip, dbm, dcm], axis=1)
    dxbc, g["ssm_conv_w"], dcb = _conv_bwd(proj, cb[6], p["ssm_conv_w"], p["ssm_conv_b"].reshape(1, -1), 0, dxbc_post,
                                           name=f"ssm_conv_bwd_{tag}")
    g["ssm_conv_b"] = dcb.reshape(-1)
    ddt = ddt_rows.transpose(0, 2, 1, 3).reshape(smh, s).T
    dq_sb, dk_sb, dv_sb = _sb_bwd(proj, cb[4], w, sv["sb_r"], do_sb, name=f"sb_bwd_{tag}")
    do_dn, dgate_dn, g["dn_norm_w"] = _dn_post_bwd(sv["o_dn"], proj, cb[1], p["dn_norm_w"], dy_dn, name=f"dn_post_bwd_{tag}")
    dq, dk, dv, da_rows, db_rows, dal, ddtb_dn = _dn_bwd(
        sv["dn_qkv"], sv["a_rows"], sv["b_rows"], sv["dn_alog"], sv["dn_dtb"], sv["dn_states"], do_dn, name=f"dn_chunk_bwd_{tag}")
    g["dn_a_log"] = dal.reshape(dnh)
    g["dn_dt_bias"] = ddtb_dn.reshape(dnh)
    zero_b = jnp.zeros((1, 3 * w), F32)
    ddn_qkv, g["dn_conv_w"], _ = _conv_bwd(proj, cb[0], p["dn_conv_w"], zero_b, 2 * dnh,
                                           jnp.concatenate([dq, dk, dv], axis=1), name=f"dn_conv_bwd_{tag}")
    da = da_rows.reshape(dnh, s).T
    db = db_rows.reshape(dnh, s).T
    dsmall = jnp.concatenate([da, db, ddt, jnp.zeros((s, LANES - lay.n_small), F32)], axis=1).astype(MXU_DTYPE)
    dproj = jnp.concatenate(
        [ddn_qkv, dgate_dn, dq_sb.astype(MXU_DTYPE), dk_sb.astype(MXU_DTYPE), dv_sb.astype(MXU_DTYPE), dz, dxbc, dgates, dsmall],
        axis=1)
    g["w_in"] = _matmul(sv["h1"], dproj, ta=True, name=f"mm_in_dw_{tag}")
    dh1 = _matmul(dproj, p["w_in"], tb=True, name=f"mm_in_dx_{tag}")
    dx0, g["norm_mix"] = _rms_bwd(x, p["norm_mix"], dh1, dx1, name=f"rms_mix_bwd_{tag}")
    return dx0, g


SHARDED = ("w_in", "dn_conv_w", "ssm_conv_w", "w_branch", "w_out", "w_up", "w_down")
BIG = ("w_in", "w_branch", "w_out", "w_up", "w_down")
SMALL = ("norm_mix", "dn_conv_w", "dn_a_log", "dn_dt_bias", "dn_norm_w", "ssm_conv_w", "ssm_conv_b", "ssm_a_log",
         "ssm_dt_bias", "ssm_d", "ssm_norm_w", "norm_mlp", "norm_final")
WEIGHTS = ("norm_mix", "w_in", "dn_conv_w", "dn_a_log", "dn_dt_bias", "dn_norm_w", "ssm_conv_w", "ssm_conv_b", "ssm_a_log",
           "ssm_dt_bias", "ssm_d", "ssm_norm_w", "w_branch", "w_out", "norm_mlp", "w_up", "w_down", "norm_final")
SHARD_AXIS = {"w_in": 2, "dn_conv_w": 2, "ssm_conv_w": 2, "w_branch": 2, "w_out": 1, "w_up": 2, "w_down": 1}


def _to_shards(full, axis):
    shp = full.shape
    n = shp[axis] // N_DEV
    t = full.reshape(shp[:axis] + (N_DEV, n) + shp[axis + 1:])
    return jnp.moveaxis(t, axis, 0)


def _from_shards(parts, axis):
    t = jnp.moveaxis(parts, 0, axis)
    shp = t.shape
    return t.reshape(shp[:axis] + (shp[axis] * shp[axis + 1],) + shp[axis + 2:])


def _f32_as_bf16_pairs(a):
    return lax.bitcast_convert_type(a, BF16)


def _bf16_pairs_as_f32(a):
    return lax.bitcast_convert_type(a, F32)


def _step(w, m, v, x, target):
    s, d = x.shape
    lay = _Layout(d)
    me = 4 * lax.axis_index("x") + 2 * lax.axis_index("y") + lax.axis_index("c")

    send = [w[n].astype(BF16) for n in BIG] + [_f32_as_bf16_pairs(w["dn_conv_w"]), _f32_as_bf16_pairs(w["ssm_conv_w"])]
    send_shapes = [a.shape for a in send]
    gathered = _all_gather(_pack(send, BF16), name="weight_all_gather")
    flat = gathered.reshape(N_DEV, -1)
    full, off = {}, 0
    for name, shp in zip(BIG + ("dn_conv_w", "ssm_conv_w"), send_shapes):
        n = math.prod(shp)
        piece = flat[:, off:off + n].reshape((N_DEV,) + shp)
        off += n
        if name in BIG:
            full[name] = _from_shards(piece, SHARD_AXIS[name])
        else:
            full[name] = _from_shards(_bf16_pairs_as_f32(piece), SHARD_AXIS[name])

    def layer_params(l):
        p = {n: (full[n][l] if n in full else w[n][l]) for n in WEIGHTS if n != "norm_final"}
        p["w_in"] = lay.reorder_w(p["w_in"])
        return p

    params = [layer_params(l) for l in range(DEPTH)]

    saved = []
    h = x
    for l in range(DEPTH):
        h, sv = _layer_fwd(h, params[l], lay, f"l{l}")
        saved.append(sv)
    loss, dh, g_norm_final = _final_loss(h, w["norm_final"], target, name="final_loss")
    grads = [None] * DEPTH
    for l in reversed(range(DEPTH)):
        dh, grads[l] = _layer_bwd(dh, params[l], saved[l], lay, f"l{l}")
        grads[l]["w_in"] = lay.restore_w(grads[l]["w_in"])
    grad_x = dh
    gfull = {n: jnp.stack([grads[l][n] for l in range(DEPTH)]) for n in WEIGHTS if n != "norm_final"}
    gfull["norm_final"] = g_norm_final

    big_send = jnp.stack([
        jnp.concatenate([_to_shards(gfull[n], SHARD_AXIS[n])[j].reshape(-1) for n in BIG]) for j in range(N_DEV)])
    n_big = big_send.shape[1]
    rows_big = -(-n_big // (BIG_ROW_MULT * LANES)) * BIG_ROW_MULT
    big_send = jnp.pad(big_send, ((0, 0), (0, rows_big * LANES - n_big))).astype(BF16).reshape(N_DEV, rows_big, LANES)
    small_send = _pack([gfull[n] for n in SMALL] + [loss.reshape(1)], F32)
    big_recv, small_recv = _grad_exchange(big_send, small_send, name="grad_exchange")

    big_shapes = [w[n].shape for n in BIG]
    wb, mb, vb = (_pack([t[n] for n in BIG], F32, BIG_ROW_MULT) for t in (w, m, v))
    assert wb.shape[0] == rows_big, (wb.shape, rows_big)
    gb, db, m1b, v1b = _sum_adamw(big_recv, wb, mb, vb, name="sum_adamw_big")
    out = {"grad": {}, "delta": {}, "new_m": {}, "new_v": {}}
    for key, packed in (("grad", gb), ("delta", db), ("new_m", m1b), ("new_v", v1b)):
        for n, a in zip(BIG, _unpack(packed, big_shapes)):
            out[key][n] = a
    small_sum = _sum_parts(small_recv, name="sum_small")
    small_full = _unpack(small_sum, [gfull[n].shape for n in SMALL] + [(1,)])
    loss_total = small_full[-1][0]
    gsmall = {}
    for n, a in zip(SMALL, small_full[:-1]):
        if n in SHARD_AXIS:
            a = lax.dynamic_index_in_dim(_to_shards(a, SHARD_AXIS[n]), me, axis=0, keepdims=False)
        gsmall[n] = a
    small_shapes = [w[n].shape for n in SMALL]
    ws, gs, ms, vs = (_pack([t[n] for n in SMALL], F32) for t in (w, gsmall, m, v))
    ds, m1s, v1s = _adamw(ws, gs, ms, vs, name="adamw_small")
    for n in SMALL:
        out["grad"][n] = gsmall[n]
    for key, packed in (("delta", ds), ("new_m", m1s), ("new_v", v1s)):
        for n, a in zip(SMALL, _unpack(packed, small_shapes)):
            out[key][n] = a
    return loss_total, grad_x, out


def kernel(x, norm_mix, w_in, dn_conv_w, dn_a_log, dn_dt_bias, dn_norm_w, ssm_conv_w, ssm_conv_b, ssm_a_log, ssm_dt_bias, ssm_d, ssm_norm_w, w_branch, w_out, norm_mlp, w_up, w_down, norm_final, loss_target, m_norm_mix, m_w_in, m_dn_conv_w, m_dn_a_log, m_dn_dt_bias, m_dn_norm_w, m_ssm_conv_w, m_ssm_conv_b, m_ssm_a_log, m_ssm_dt_bias, m_ssm_d, m_ssm_norm_w, m_w_branch, m_w_out, m_norm_mlp, m_w_up, m_w_down, m_norm_final, v_norm_mix, v_w_in, v_dn_conv_w, v_dn_a_log, v_dn_dt_bias, v_dn_norm_w, v_ssm_conv_w, v_ssm_conv_b, v_ssm_a_log, v_ssm_dt_bias, v_ssm_d, v_ssm_norm_w, v_w_branch, v_w_out, v_norm_mlp, v_w_up, v_w_down, v_norm_final):
    w = dict(norm_mix=norm_mix, w_in=w_in, dn_conv_w=dn_conv_w, dn_a_log=dn_a_log, dn_dt_bias=dn_dt_bias, dn_norm_w=dn_norm_w,
             ssm_conv_w=ssm_conv_w, ssm_conv_b=ssm_conv_b, ssm_a_log=ssm_a_log, ssm_dt_bias=ssm_dt_bias, ssm_d=ssm_d,
             ssm_norm_w=ssm_norm_w, w_branch=w_branch, w_out=w_out, norm_mlp=norm_mlp, w_up=w_up, w_down=w_down,
             norm_final=norm_final)
    m = dict(norm_mix=m_norm_mix, w_in=m_w_in, dn_conv_w=m_dn_conv_w, dn_a_log=m_dn_a_log, dn_dt_bias=m_dn_dt_bias,
             dn_norm_w=m_dn_norm_w, ssm_conv_w=m_ssm_conv_w, ssm_conv_b=m_ssm_conv_b, ssm_a_log=m_ssm_a_log,
             ssm_dt_bias=m_ssm_dt_bias, ssm_d=m_ssm_d, ssm_norm_w=m_ssm_norm_w, w_branch=m_w_branch, w_out=m_w_out,
             norm_mlp=m_norm_mlp, w_up=m_w_up, w_down=m_w_down, norm_final=m_norm_final)
    v = dict(norm_mix=v_norm_mix, w_in=v_w_in, dn_conv_w=v_dn_conv_w, dn_a_log=v_dn_a_log, dn_dt_bias=v_dn_dt_bias,
             dn_norm_w=v_dn_norm_w, ssm_conv_w=v_ssm_conv_w, ssm_conv_b=v_ssm_conv_b, ssm_a_log=v_ssm_a_log,
             ssm_dt_bias=v_ssm_dt_bias, ssm_d=v_ssm_d, ssm_norm_w=v_ssm_norm_w, w_branch=v_w_branch, w_out=v_w_out,
             norm_mlp=v_norm_mlp, w_up=v_w_up, w_down=v_w_down, norm_final=v_norm_final)
    loss, grad_x, out = _step(w, m, v, x[0], loss_target[0])
    return (loss, grad_x[None], *[out["grad"][n] for n in WEIGHTS], *[out["delta"][n] for n in WEIGHTS],
            *[out["new_m"][n] for n in WEIGHTS], *[out["new_v"][n] for n in WEIGHTS])
```

```python
import functools
import math

import jax
import jax.numpy as jnp
from jax import lax
from jax.experimental import pallas as pl
from jax.experimental.pallas import tpu as pltpu

F32 = jnp.float32
BF16 = jnp.bfloat16
MXU_DTYPE = BF16
HIGHEST = lax.Precision.HIGHEST

N_DEV = 8
DEPTH = 2
EPS = 1e-6
CONV_K = 4
DN_HEAD_DIM = 128
SB_HEAD_DIM = 64
SSM_HEAD_DIM = 64
SSM_STATE = 128
SSM_GROUPS = 4
CHUNK = 64
SB_BLOCK = 128
LANES = 128
ADAM_LR, ADAM_B1, ADAM_B2, ADAM_EPS, ADAM_WD, ADAM_STEP = 0.001, 0.9, 0.999, 1e-08, 0.01, 10
NEG_BIG = -1e30
DN_HEADS_PER_STEP = 4
CHUNK_PREC = lax.Precision.HIGH

ARB = "arbitrary"


def _cparams(n_axes):
    return pltpu.CompilerParams(dimension_semantics=(ARB,) * n_axes)


def _softplus(x):
    return jnp.maximum(x, 0.0) + jnp.log1p(jnp.exp(-jnp.abs(x)))


def _sigmoid(x):
    return 1.0 / (1.0 + jnp.exp(-x))


def _silu(x):
    return x * _sigmoid(x)


def _silu_grad(x):
    s = _sigmoid(x)
    return s * (1.0 + x * (1.0 - s))


def _dot(a, b, dims, prec=None):
    return lax.dot_general(a, b, (dims, ((), ())), precision=prec, preferred_element_type=F32)


NN = ((1,), (0,))
NT = ((1,), (1,))
TN = ((0,), (0,))


def _hdot(a, b, dims=NN):
    return _dot(a, b, dims, CHUNK_PREC)


def _bdot(a, b, dims=NN):
    return _dot(a.astype(MXU_DTYPE), b.astype(MXU_DTYPE), dims)


def _split_dot(a, m_bf16, nsplit=3):
    out = None
    rem = a
    for _ in range(nsplit):
        piece = rem.astype(BF16)
        rem = rem - piece.astype(F32)
        term = _dot(piece, m_bf16, NN)
        out = term if out is None else out + term
    return out


def _pick(n, pref):
    for t in pref:
        if n % t == 0:
            return t
    return n


def _matmul(a, b, *, ta=False, tb=False, name, epilogue=None, extras=(), out_dtypes=(F32,), col_shards=1,
            tm=None, tn=None, tk=None):
    m, k = (a.shape[1], a.shape[0]) if ta else a.shape
    k2, n = (b.shape[1], b.shape[0]) if tb else b.shape
    assert k == k2, (a.shape, b.shape, ta, tb)
    ncs = n // col_shards
    tm = tm or _pick(m, (512, 256, 128))
    tn = tn or _pick(ncs, (1024, 640, 512, 384, 256, 128))
    tk = tk or _pick(k, (1920, 1024, 640, 512, 256, 128))
    nk = k // tk
    a_spec = pl.BlockSpec((tk, tm), lambda i, j, kk: (kk, i)) if ta else pl.BlockSpec((tm, tk), lambda i, j, kk: (i, kk))
    b_spec = pl.BlockSpec((tn, tk), lambda i, j, kk: (j, kk)) if tb else pl.BlockSpec((tk, tn), lambda i, j, kk: (kk, j))
    e_spec = pl.BlockSpec((tm, tn), lambda i, j, kk: (i, j))
    if col_shards == 1:
        o_spec, o_shape = e_spec, (m, n)
    else:
        per = ncs // tn
        o_spec, o_shape = pl.BlockSpec((None, tm, tn), lambda i, j, kk: (j // per, i, j % per)), (col_shards, m, ncs)
    dims = (((0,) if ta else (1,)), ((1,) if tb else (0,)))
    n_extra = len(extras)
    n_out = len(out_dtypes)

    def body(*refs):
        a_ref, b_ref = refs[0], refs[1]
        extra_refs = refs[2:2 + n_extra]
        out_refs = refs[2 + n_extra:2 + n_extra + n_out]
        acc_ref = refs[-1]
        kk = pl.program_id(2)

        @pl.when(kk == 0)
        def _():
            acc_ref[...] = jnp.zeros_like(acc_ref)

        acc_ref[...] += _dot(a_ref[...].astype(MXU_DTYPE), b_ref[...].astype(MXU_DTYPE), dims)

        @pl.when(kk == nk - 1)
        def _():
            acc = acc_ref[...]
            outs = (acc,) if epilogue is None else epilogue(acc, *[r[...] for r in extra_refs])
            for o_ref, o in zip(out_refs, outs):
                o_ref[...] = o.astype(o_ref.dtype)

    outs = pl.pallas_call(
        body,
        grid=(m // tm, n // tn, nk),
        in_specs=[a_spec, b_spec] + [e_spec] * n_extra,
        out_specs=[o_spec] * n_out,
        out_shape=[jax.ShapeDtypeStruct(o_shape, dt) for dt in out_dtypes],
        scratch_shapes=[pltpu.VMEM((tm, tn), F32)],
        compiler_params=pltpu.CompilerParams(dimension_semantics=("parallel", "parallel", ARB)),
        name=name,
    )(a, b, *extras)
    return outs[0] if n_out == 1 else tuple(outs)


def _rms_fwd(x, w, *, name, tm=256):
    s, d = x.shape
    out_dtype = MXU_DTYPE

    def body(x_ref, w_ref, o_ref):
        xv = x_ref[...]
        r = lax.rsqrt(jnp.mean(xv * xv, axis=-1, keepdims=True) + EPS)
        o_ref[...] = (xv * r * w_ref[...]).astype(o_ref.dtype)

    return pl.pallas_call(
        body, grid=(s // tm,),
        in_specs=[pl.BlockSpec((tm, d), lambda i: (i, 0)), pl.BlockSpec((1, d), lambda i: (0, 0))],
        out_specs=pl.BlockSpec((tm, d), lambda i: (i, 0)),
        out_shape=jax.ShapeDtypeStruct((s, d), out_dtype),
        compiler_params=_cparams(1), name=name,
    )(x, w.reshape(1, d))


def _rms_bwd(x, w, dh, dres, *, name, tm=256):
    s, d = x.shape

    def body(x_ref, w_ref, dh_ref, dres_ref, dx_ref, dw_ref):
        xv = x_ref[...]
        r = lax.rsqrt(jnp.mean(xv * xv, axis=-1, keepdims=True) + EPS)
        xh = xv * r
        dhv = dh_ref[...].astype(F32)
        dxn = dhv * w_ref[...]
        dx = r * (dxn - xh * jnp.mean(dxn * xh, axis=-1, keepdims=True))
        dx_ref[...] = dres_ref[...] + dx

        @pl.when(pl.program_id(0) == 0)
        def _():
            dw_ref[...] = jnp.zeros_like(dw_ref)

        dw_ref[...] += jnp.sum(dhv * xh, axis=0, keepdims=True)

    dx, dw = pl.pallas_call(
        body, grid=(s // tm,),
        in_specs=[pl.BlockSpec((tm, d), lambda i: (i, 0)), pl.BlockSpec((1, d), lambda i: (0, 0)),
                  pl.BlockSpec((tm, d), lambda i: (i, 0)), pl.BlockSpec((tm, d), lambda i: (i, 0))],
        out_specs=[pl.BlockSpec((tm, d), lambda i: (i, 0)), pl.BlockSpec((1, d), lambda i: (0, 0))],
        out_shape=[jax.ShapeDtypeStruct((s, d), F32), jax.ShapeDtypeStruct((1, d), F32)],
        compiler_params=_cparams(1), name=name,
    )(x, w.reshape(1, d), dh, dres)
    return dx, dw.reshape(d)


def _final_loss(x, w, target, *, name, tm=256):
    s, d = x.shape

    def body(x_ref, w_ref, t_ref, loss_ref, dx_ref, dw_ref):
        xv = x_ref[...]
        r = lax.rsqrt(jnp.mean(xv * xv, axis=-1, keepdims=True) + EPS)
        xh = xv * r
        err = xh * w_ref[...] - t_ref[...]
        dy = err * (1.0 / d)
        dxn = dy * w_ref[...]
        dx_ref[...] = r * (dxn - xh * jnp.mean(dxn * xh, axis=-1, keepdims=True))

        @pl.when(pl.program_id(0) == 0)
        def _():
            dw_ref[...] = jnp.zeros_like(dw_ref)
            loss_ref[...] = jnp.zeros_like(loss_ref)

        dw_ref[...] += jnp.sum(dy * xh, axis=0, keepdims=True)
        row = jnp.sum(err * err, axis=1, keepdims=True) * (0.5 / d)
        loss_ref[...] += jnp.sum(row, axis=0, keepdims=True)

    loss, dx, dw = pl.pallas_call(
        body, grid=(s // tm,),
        in_specs=[pl.BlockSpec((tm, d), lambda i: (i, 0)), pl.BlockSpec((1, d), lambda i: (0, 0)),
                  pl.BlockSpec((tm, d), lambda i: (i, 0))],
        out_specs=[pl.BlockSpec((1, 1), lambda i: (0, 0)), pl.BlockSpec((tm, d), lambda i: (i, 0)),
                   pl.BlockSpec((1, d), lambda i: (0, 0))],
        out_shape=[jax.ShapeDtypeStruct((1, 1), F32), jax.ShapeDtypeStruct((s, d), F32), jax.ShapeDtypeStruct((1, d), F32)],
        compiler_params=_cparams(1), name=name,
    )(x, w.reshape(1, d), target)
    return loss[0, 0], dx, dw.reshape(d)


def _shift_down(x, sh, t_idx):
    return jnp.where(t_idx >= sh, pltpu.roll(x, sh, 0), 0.0)


def _shift_up(x, sh, t_idx, s):
    return jnp.where(t_idx < s - sh, pltpu.roll(x, s - sh, 0), 0.0)


def _conv_pre(x, w_rows, b, t_idx):
    c = w_rows[CONV_K - 1] * x + b
    for sh in range(1, CONV_K):
        c = c + w_rows[CONV_K - 1 - sh] * _shift_down(x, sh, t_idx)
    return c


def _conv_fwd(src, col0, w, b, n_l2, *, name):
    s = src.shape[0]
    c_tot = w.shape[1]
    nblk = c_tot // LANES

    def body(x_ref, w_ref, b_ref, o_ref):
        j = pl.program_id(0)
        t_idx = lax.broadcasted_iota(jnp.int32, (s, LANES), 0)
        w_rows = [w_ref[kk:kk + 1, :] for kk in range(CONV_K)]
        y = _silu(_conv_pre(x_ref[...], w_rows, b_ref[...], t_idx))
        if n_l2 > 0:
            yn = y * lax.rsqrt(jnp.sum(y * y, axis=1, keepdims=True) + EPS)
            y = jnp.where(j < n_l2, yn, y)
        o_ref[...] = y

    return pl.pallas_call(
        body, grid=(nblk,),
        in_specs=[pl.BlockSpec((s, LANES), lambda j: (0, col0 + j)), pl.BlockSpec((CONV_K, LANES), lambda j: (0, j)),
                  pl.BlockSpec((1, LANES), lambda j: (0, j))],
        out_specs=pl.BlockSpec((s, LANES), lambda j: (0, j)),
        out_shape=jax.ShapeDtypeStruct((s, c_tot), F32),
        compiler_params=_cparams(1), name=name,
    )(src, w, b)


def _conv_bwd(src, col0, w, b, n_l2, dout, *, name):
    s = src.shape[0]
    c_tot = w.shape[1]
    nblk = c_tot // LANES

    def body(x_ref, w_ref, b_ref, do_ref, dx_ref, dw_ref, db_ref):
        j = pl.program_id(0)
        t_idx = lax.broadcasted_iota(jnp.int32, (s, LANES), 0)
        xv = x_ref[...]
        w_rows = [w_ref[kk:kk + 1, :] for kk in range(CONV_K)]
        c = _conv_pre(xv, w_rows, b_ref[...], t_idx)
        dy = do_ref[...]
        if n_l2 > 0:
            y = _silu(c)
            r = lax.rsqrt(jnp.sum(y * y, axis=1, keepdims=True) + EPS)
            dyn = r * dy - y * (r * r * r) * jnp.sum(dy * y, axis=1, keepdims=True)
            dy = jnp.where(j < n_l2, dyn, dy)
        dc = dy * _silu_grad(c)
        dx = w_rows[CONV_K - 1] * dc
        rows = [None] * CONV_K
        rows[CONV_K - 1] = jnp.sum(dc * xv, axis=0, keepdims=True)
        for sh in range(1, CONV_K):
            dx = dx + w_rows[CONV_K - 1 - sh] * _shift_up(dc, sh, t_idx, s)
            rows[CONV_K - 1 - sh] = jnp.sum(dc * _shift_down(xv, sh, t_idx), axis=0, keepdims=True)
        dx_ref[...] = dx.astype(dx_ref.dtype)
        for kk in range(CONV_K):
            dw_ref[kk:kk + 1, :] = rows[kk]
        db_ref[...] = jnp.sum(dc, axis=0, keepdims=True)

    return pl.pallas_call(
        body, grid=(nblk,),
        in_specs=[pl.BlockSpec((s, LANES), lambda j: (0, col0 + j)), pl.BlockSpec((CONV_K, LANES), lambda j: (0, j)),
                  pl.BlockSpec((1, LANES), lambda j: (0, j)), pl.BlockSpec((s, LANES), lambda j: (0, j))],
        out_specs=[pl.BlockSpec((s, LANES), lambda j: (0, j)), pl.BlockSpec((CONV_K, LANES), lambda j: (0, j)),
                   pl.BlockSpec((1, LANES), lambda j: (0, j))],
        out_shape=[jax.ShapeDtypeStruct((s, c_tot), MXU_DTYPE), jax.ShapeDtypeStruct((CONV_K, c_tot), F32),
                   jax.ShapeDtypeStruct((1, c_tot), F32)],
        compiler_params=_cparams(1), name=name,
    )(src, w, b, dout)


def _chunk_masks(c):
    ii = lax.broadcasted_iota(jnp.int32, (c, c), 0)
    jj = lax.broadcasted_iota(jnp.int32, (c, c), 1)
    return ii, jj


def _row_to_col(row, eye):
    return jnp.sum(jnp.where(eye, row, 0.0), axis=1, keepdims=True)


def _dn_chunk(q, k, v, a_row, b_row, alog, dtb, s0):
    c = q.shape[0]
    ii, jj = _chunk_masks(c)
    causal, strict, eye = ii >= jj, ii > jj, ii == jj
    g_row = -jnp.exp(alog) * _softplus(a_row + dtb)
    beta_col = _row_to_col(_sigmoid(b_row), eye)
    g_col = _row_to_col(g_row, eye)
    gc_col = jnp.sum(jnp.where(causal, g_row, 0.0), axis=1, keepdims=True)
    gc_row = jnp.sum(jnp.where(jj >= ii, g_col, 0.0), axis=0, keepdims=True)
    decay = jnp.exp(jnp.where(causal, gc_col - gc_row, NEG_BIG))
    kb = k * beta_col
    vb = v * beta_col
    nmat = -jnp.where(strict, _hdot(kb, k, NT) * decay, 0.0)
    xinv = jnp.where(eye, 1.0, 0.0) + nmat
    pw = nmat
    for _ in range(int(math.log2(c)) - 1):
        pw = _hdot(pw, pw)
        xinv = xinv + _hdot(xinv, pw)
    egc = jnp.exp(gc_col)
    u = _hdot(xinv, vb)
    w = _hdot(xinv, kb * egc)
    qs = q * (q.shape[1] ** -0.5)
    attn = _hdot(qs, k, NT) * decay
    gl = jnp.sum(g_row, axis=1, keepdims=True)
    kd = k * jnp.exp(gl - gc_col)
    v_new = u - _hdot(w, s0)
    o = _hdot(qs * egc, s0) + _hdot(attn, v_new)
    s1 = s0 * jnp.exp(gl) + _hdot(kd, v_new, TN)
    return o, s1


def _dn_specs(nh, nc, hb, rev):
    n_of = (lambda n: nc - 1 - n) if rev else (lambda n: n)
    ng = nh // hb
    qkv = [pl.BlockSpec((CHUNK, hb * DN_HEAD_DIM), (lambda h, n, o=o: (n_of(n), o * ng + h))) for o in range(3)]
    row = pl.BlockSpec((hb, None, 1, CHUNK), lambda h, n: (h, n_of(n), 0, 0))
    scal = pl.BlockSpec((hb, 1, 1), lambda h, n: (h, 0, 0))
    o_spec = pl.BlockSpec((CHUNK, hb * DN_HEAD_DIM), lambda h, n: (n_of(n), h))
    st = pl.BlockSpec((hb, None, DN_HEAD_DIM, DN_HEAD_DIM), lambda h, n: (h, n_of(n), 0, 0))
    return qkv, row, scal, o_spec, st


def _dn_fwd(qkv, a_rows, b_rows, alog, dtb, *, name):
    s = qkv.shape[0]
    nh, nc = a_rows.shape[0], a_rows.shape[1]
    hb = DN_HEADS_PER_STEP
    qkv_specs, row, scal, o_spec, st = _dn_specs(nh, nc, hb, False)
    hd = DN_HEAD_DIM

    def body(q_ref, k_ref, v_ref, a_ref, b_ref, al_ref, dt_ref, o_ref, st_ref, state):
        @pl.when(pl.program_id(1) == 0)
        def _():
            state[...] = jnp.zeros_like(state)

        for h in range(hb):
            cols = slice(h * hd, (h + 1) * hd)
            s0 = state[h]
            st_ref[h] = s0
            o, s1 = _dn_chunk(q_ref[:, cols], k_ref[:, cols], v_ref[:, cols], a_ref[h], b_ref[h], al_ref[h], dt_ref[h], s0)
            o_ref[:, cols] = o
            state[h] = s1

    return pl.pallas_call(
        body, grid=(nh // hb, nc),
        in_specs=qkv_specs + [row, row, scal, scal],
        out_specs=[o_spec, st],
        out_shape=[jax.ShapeDtypeStruct((s, nh * hd), F32), jax.ShapeDtypeStruct((nh, nc, hd, hd), F32)],
        scratch_shapes=[pltpu.VMEM((hb, hd, hd), F32)],
        compiler_params=_cparams(2), name=name,
    )(qkv, qkv, qkv, a_rows, b_rows, alog, dtb)


def _dn_bwd(qkv, a_rows, b_rows, alog, dtb, states, do, *, name):
    s = qkv.shape[0]
    nh, nc = a_rows.shape[0], a_rows.shape[1]
    hb = DN_HEADS_PER_STEP
    qkv_specs, row, scal, o_spec, st = _dn_specs(nh, nc, hb, True)
    hd = DN_HEAD_DIM

    def body(q_ref, k_ref, v_ref, a_ref, b_ref, al_ref, dt_ref, st_ref, do_ref,
             dq_ref, dk_ref, dv_ref, da_ref, db_ref, dal_ref, ddt_ref, dstate):
        @pl.when(pl.program_id(1) == 0)
        def _():
            dstate[...] = jnp.zeros_like(dstate)
            dal_ref[...] = jnp.zeros_like(dal_ref)
            ddt_ref[...] = jnp.zeros_like(ddt_ref)

        for h in range(hb):
            cols = slice(h * hd, (h + 1) * hd)
            args = (q_ref[:, cols], k_ref[:, cols], v_ref[:, cols], a_ref[h], b_ref[h], al_ref[h], dt_ref[h], st_ref[h])
            _, vjp = jax.vjp(_dn_chunk, *args)
            dq, dk, dv, da, db, dal, ddt, ds0 = vjp((do_ref[:, cols], dstate[h]))
            dq_ref[:, cols] = dq
            dk_ref[:, cols] = dk
            dv_ref[:, cols] = dv
            da_ref[h] = da
            db_ref[h] = db
            dal_ref[h] += dal
            ddt_ref[h] += ddt
            dstate[h] = ds0

    w = nh * hd
    outs = pl.pallas_call(
        body, grid=(nh // hb, nc),
        in_specs=qkv_specs + [row, row, scal, scal, st, o_spec],
        out_specs=[o_spec, o_spec, o_spec, row, row, scal, scal],
        out_shape=[jax.ShapeDtypeStruct((s, w), F32)] * 3
        + [jax.ShapeDtypeStruct(a_rows.shape, F32)] * 2 + [jax.ShapeDtypeStruct((nh, 1, 1), F32)] * 2,
        scratch_shapes=[pltpu.VMEM((hb, hd, hd), F32)],
        compiler_params=_cparams(2), name=name,
    )(qkv, qkv, qkv, a_rows, b_rows, alog, dtb, states, do)
    return outs


def _dn_post_fwd(o, src, gate_col0, nw, *, name, tm=256):
    s, w = o.shape
    nh = w // DN_HEAD_DIM

    def body(o_ref, g_ref, w_ref, y_ref):
        ov = o_ref[...]
        r = lax.rsqrt(jnp.mean(ov * ov, axis=-1, keepdims=True) + EPS)
        y_ref[...] = (ov * r * w_ref[...] * _silu(g_ref[...])).astype(y_ref.dtype)

    blk = pl.BlockSpec((tm, DN_HEAD_DIM), lambda i, h: (i, h))
    return pl.pallas_call(
        body, grid=(s // tm, nh),
        in_specs=[blk, pl.BlockSpec((tm, DN_HEAD_DIM), lambda i, h: (i, gate_col0 + h)),
                  pl.BlockSpec((1, DN_HEAD_DIM), lambda i, h: (0, 0))],
        out_specs=blk, out_shape=jax.ShapeDtypeStruct((s, w), MXU_DTYPE),
        compiler_params=_cparams(2), name=name,
    )(o, src, nw.reshape(1, DN_HEAD_DIM))


def _dn_post_bwd(o, src, gate_col0, nw, dy, *, name, tm=256):
    s, w = o.shape
    nh = w // DN_HEAD_DIM

    def body(o_ref, g_ref, w_ref, dy_ref, do_ref, dg_ref, dw_ref):
        ov = o_ref[...]
        gv = g_ref[...]
        dyv = dy_ref[...]
        r = lax.rsqrt(jnp.mean(ov * ov, axis=-1, keepdims=True) + EPS)
        oh = ov * r
        dn = dyv * _silu(gv)
        dg_ref[...] = (dyv * (oh * w_ref[...]) * _silu_grad(gv)).astype(dg_ref.dtype)
        don = dn * w_ref[...]
        do_ref[...] = r * (don - oh * jnp.mean(don * oh, axis=-1, keepdims=True))

        @pl.when((pl.program_id(0) == 0) & (pl.program_id(1) == 0))
        def _():
            dw_ref[...] = jnp.zeros_like(dw_ref)

        dw_ref[...] += jnp.sum(dn * oh, axis=0, keepdims=True)

    blk = pl.BlockSpec((tm, DN_HEAD_DIM), lambda i, h: (i, h))
    wspec = pl.BlockSpec((1, DN_HEAD_DIM), lambda i, h: (0, 0))
    do, dg, dw = pl.pallas_call(
        body, grid=(s // tm, nh),
        in_specs=[blk, pl.BlockSpec((tm, DN_HEAD_DIM), lambda i, h: (i, gate_col0 + h)), wspec, blk],
        out_specs=[blk, blk, wspec],
        out_shape=[jax.ShapeDtypeStruct((s, w), F32), jax.ShapeDtypeStruct((s, w), MXU_DTYPE),
                   jax.ShapeDtypeStruct((1, DN_HEAD_DIM), F32)],
        compiler_params=_cparams(2), name=name,
    )(o, src, nw.reshape(1, DN_HEAD_DIM), dy)
    return do, dg, dw.reshape(DN_HEAD_DIM)


def _sb_consts():
    r2 = lax.broadcasted_iota(jnp.int32, (2 * SB_BLOCK, SB_BLOCK), 0)
    c2 = lax.broadcasted_iota(jnp.int32, (2 * SB_BLOCK, SB_BLOCK), 1)
    r = lax.broadcasted_iota(jnp.int32, (SB_BLOCK, SB_BLOCK), 0)
    c = lax.broadcasted_iota(jnp.int32, (SB_BLOCK, SB_BLOCK), 1)
    lm0 = c < SB_HEAD_DIM
    m_gt = jnp.where(r > c, 1.0, 0.0).astype(BF16)
    m_lt = jnp.where(r < c, 1.0, 0.0).astype(BF16)
    return r2, c2, lm0, m_gt, m_lt


def _sb_stack(x, lm0):
    return jnp.concatenate([jnp.where(lm0, x, 0.0), jnp.where(lm0, 0.0, x)], axis=0)


def _sb_unstack(x2, lm0):
    return jnp.where(lm0, x2[:SB_BLOCK], x2[SB_BLOCK:])


def _sb_fwd(src, col0, width, *, name):
    s = src.shape[0]
    nq = s // SB_BLOCK
    npair = width // LANES
    scale = SB_HEAD_DIM ** -0.5

    def body(q_ref, k_ref, v_ref, o_ref, r_ref):
        i = pl.program_id(1)
        r2, c2, lm0, m_gt, _ = _sb_consts()
        t_glob = i * SB_BLOCK + (r2 & (SB_BLOCK - 1))
        q2 = _sb_stack(q_ref[...], lm0).astype(MXU_DTYPE)

        def sub(j, o2, rsum):
            off = pl.multiple_of(j * SB_BLOCK, SB_BLOCK)
            kb = k_ref[pl.ds(off, SB_BLOCK), :].astype(MXU_DTYPE)
            vb = v_ref[pl.ds(off, SB_BLOCK), :].astype(MXU_DTYPE)
            z = _dot(q2, kb, NT) * scale
            mask = (j * SB_BLOCK + c2) < t_glob
            t = jnp.log1p(jnp.exp(-jnp.abs(z)))
            lk = jnp.where(mask, -(jnp.maximum(z, 0.0) + t), 0.0)
            loga = (jnp.minimum(z, 0.0) - t) + rsum + _split_dot(lk, m_gt)
            wgt = jnp.where(mask, jnp.exp(loga), 0.0)
            return o2 + _dot(wgt.astype(MXU_DTYPE), vb, NN), rsum + jnp.sum(lk, axis=1, keepdims=True)

        def step(g, carry):
            o2, rsum = carry
            top = (i // 2 - g) * 2
            o2, rsum = sub(top + 1, o2, rsum)
            return sub(top, o2, rsum)

        o2, rsum = lax.fori_loop(0, i // 2 + 1, step,
                                 (jnp.zeros((2 * SB_BLOCK, LANES), F32), jnp.zeros((2 * SB_BLOCK, 1), F32)))
        o_ref[...] = _sb_unstack(o2, lm0)
        r_ref[...] = _sb_unstack(jnp.broadcast_to(rsum, (2 * SB_BLOCK, LANES)), lm0)

    blk = pl.BlockSpec((SB_BLOCK, LANES), lambda p, i: (i, p))
    return pl.pallas_call(
        body, grid=(npair, nq),
        in_specs=[pl.BlockSpec((SB_BLOCK, LANES), lambda p, i: (i, col0 + p)),
                  pl.BlockSpec((s, LANES), lambda p, i: (0, col0 + npair + p)),
                  pl.BlockSpec((s, LANES), lambda p, i: (0, col0 + 2 * npair + p))],
        out_specs=[blk, blk],
        out_shape=[jax.ShapeDtypeStruct((s, width), F32), jax.ShapeDtypeStruct((s, width), F32)],
        compiler_params=_cparams(2), name=name,
    )(src, src, src)


def _sb_bwd(src, col0, width, rtot, do, *, name):
    s = src.shape[0]
    nq = s // SB_BLOCK
    npair = width // LANES
    scale = SB_HEAD_DIM ** -0.5

    def body(q_ref, k_ref, v_ref, r_ref, do_ref, dq_ref, dk_ref, dv_ref):
        i = pl.program_id(1)

        @pl.when(i == 0)
        def _():
            dk_ref[...] = jnp.zeros_like(dk_ref)
            dv_ref[...] = jnp.zeros_like(dv_ref)

        r2, c2, lm0, m_gt, m_lt = _sb_consts()
        t_glob = i * SB_BLOCK + (r2 & (SB_BLOCK - 1))
        q2 = _sb_stack(q_ref[...], lm0).astype(MXU_DTYPE)
        do2 = _sb_stack(do_ref[...], lm0).astype(MXU_DTYPE)
        rv = r_ref[...]
        rt = jnp.concatenate([jnp.max(jnp.where(lm0, rv, NEG_BIG), axis=1, keepdims=True),
                              jnp.max(jnp.where(lm0, NEG_BIG, rv), axis=1, keepdims=True)], axis=0)

        def sub(j, dq2, psum, csum):
            off = pl.multiple_of(j * SB_BLOCK, SB_BLOCK)
            kb = k_ref[pl.ds(off, SB_BLOCK), :].astype(MXU_DTYPE)
            vb = v_ref[pl.ds(off, SB_BLOCK), :].astype(MXU_DTYPE)
            z = _dot(q2, kb, NT) * scale
            mask = (j * SB_BLOCK + c2) < t_glob
            t = jnp.log1p(jnp.exp(-jnp.abs(z)))
            lk = jnp.where(mask, -(jnp.maximum(z, 0.0) + t), 0.0)
            lsum = jnp.sum(lk, axis=1, keepdims=True)
            logsig = jnp.minimum(z, 0.0) - t
            reach = (rt - psum - lsum) + _split_dot(lk, m_gt)
            wgt = jnp.where(mask, jnp.exp(logsig + reach), 0.0)
            dloga = wgt * _dot(do2, vb, NT)
            sig = jnp.exp(logsig)
            dlk = jnp.where(mask, csum + _split_dot(dloga, m_lt), 0.0)
            dzb = ((dloga * (1.0 - sig) - dlk * sig) * scale).astype(MXU_DTYPE)
            dk_ref[pl.ds(off, SB_BLOCK), :] += _dot(dzb, q2, TN)
            dv_ref[pl.ds(off, SB_BLOCK), :] += _dot(wgt.astype(MXU_DTYPE), do2, TN)
            return dq2 + _dot(dzb, kb, NN), psum + lsum, csum + jnp.sum(dloga, axis=1, keepdims=True)

        def step(g, carry):
            carry = sub(2 * g, *carry)
            return sub(2 * g + 1, *carry)

        zero_col = jnp.zeros((2 * SB_BLOCK, 1), F32)
        dq2, _, _ = lax.fori_loop(0, i // 2 + 1, step, (jnp.zeros((2 * SB_BLOCK, LANES), F32), zero_col, zero_col))
        dq_ref[...] = _sb_unstack(dq2, lm0)

    blk = pl.BlockSpec((SB_BLOCK, LANES), lambda p, i: (i, p))
    full = pl.BlockSpec((s, LANES), lambda p, i: (0, p))
    return pl.pallas_call(
        body, grid=(npair, nq),
        in_specs=[pl.BlockSpec((SB_BLOCK, LANES), lambda p, i: (i, col0 + p)),
                  pl.BlockSpec((s, LANES), lambda p, i: (0, col0 + npair + p)),
                  pl.BlockSpec((s, LANES), lambda p, i: (0, col0 + 2 * npair + p)),
                  blk, blk],
        out_specs=[blk, full, full],
        out_shape=[jax.ShapeDtypeStruct((s, width), F32)] * 3,
        compiler_params=_cparams(2), name=name,
    )(src, src, src, rtot, do)


def _ssd_group(xs, dt_rows, alogs, dtbs, bm, cm, h0s):
    c = bm.shape[0]
    ii, jj = _chunk_masks(c)
    causal, eye = ii >= jj, ii == jj
    scores = _hdot(cm, bm, NT)
    ys, h1s = [], []
    for x, dt_row, alog, dtb, h0 in zip(xs, dt_rows, alogs, dtbs, h0s):
        dt_r = _softplus(dt_row + dtb)
        a_r = -jnp.exp(alog) * dt_r
        dt_col = _row_to_col(dt_r, eye)
        a_col = _row_to_col(a_r, eye)
        ac_col = jnp.sum(jnp.where(causal, a_r, 0.0), axis=1, keepdims=True)
        ac_row = jnp.sum(jnp.where(jj >= ii, a_col, 0.0), axis=0, keepdims=True)
        lmat = jnp.exp(jnp.where(causal, ac_col - ac_row, NEG_BIG))
        xdt = x * dt_col
        al = jnp.sum(a_r, axis=1, keepdims=True)
        y = _hdot(scores * lmat, xdt) + _hdot(cm, h0, NT) * jnp.exp(ac_col)
        h1 = h0 * jnp.exp(al) + _hdot(xdt * jnp.exp(al - ac_col), bm, TN)
        ys.append(y)
        h1s.append(h1)
    return ys, h1s


def _ssd_specs(ng, nc, r, rev):
    n_of = (lambda n: nc - 1 - n) if rev else (lambda n: n)
    gw = r * SSM_HEAD_DIM
    x_spec = pl.BlockSpec((CHUNK, gw), lambda g, n: (n_of(n), g))
    b_spec = pl.BlockSpec((CHUNK, SSM_STATE), lambda g, n: (n_of(n), (ng * gw) // SSM_STATE + g))
    c_spec = pl.BlockSpec((CHUNK, SSM_STATE), lambda g, n: (n_of(n), (ng * gw) // SSM_STATE + ng + g))
    dt_spec = pl.BlockSpec((None, None, r, CHUNK), lambda g, n: (g, n_of(n), 0, 0))
    sc_spec = pl.BlockSpec((None, r, 1), lambda g, n: (g, 0, 0))
    st_spec = pl.BlockSpec((None, None, r, SSM_HEAD_DIM, SSM_STATE), lambda g, n: (g, n_of(n), 0, 0, 0))
    y_spec = pl.BlockSpec((CHUNK, gw), lambda g, n: (n_of(n), g))
    bc_out = pl.BlockSpec((CHUNK, SSM_STATE), lambda g, n: (n_of(n), g))
    return x_spec, b_spec, c_spec, dt_spec, sc_spec, st_spec, y_spec, bc_out


def _ssd_fwd(xbc, dt_rows, alog, dtb, *, name):
    s = xbc.shape[0]
    ng, nc, r = dt_rows.shape[0], dt_rows.shape[1], dt_rows.shape[2]
    w = ng * r * SSM_HEAD_DIM
    x_spec, b_spec, c_spec, dt_spec, sc_spec, st_spec, y_spec, _ = _ssd_specs(ng, nc, r, False)
    p = SSM_HEAD_DIM

    def body(x_ref, b_ref, c_ref, dt_ref, al_ref, db_ref, y_ref, st_ref, state):
        @pl.when(pl.program_id(1) == 0)
        def _():
            state[...] = jnp.zeros_like(state)

        st_ref[...] = state[...]
        xs = [x_ref[:, h * p:(h + 1) * p] for h in range(r)]
        dts = [dt_ref[h:h + 1, :] for h in range(r)]
        als = [al_ref[h:h + 1, :] for h in range(r)]
        dbs = [db_ref[h:h + 1, :] for h in range(r)]
        h0s = [state[h] for h in range(r)]
        ys, h1s = _ssd_group(xs, dts, als, dbs, b_ref[...], c_ref[...], h0s)
        for h in range(r):
            y_ref[:, h * p:(h + 1) * p] = ys[h]
            state[h] = h1s[h]

    return pl.pallas_call(
        body, grid=(ng, nc),
        in_specs=[x_spec, b_spec, c_spec, dt_spec, sc_spec, sc_spec],
        out_specs=[y_spec, st_spec],
        out_shape=[jax.ShapeDtypeStruct((s, w), F32), jax.ShapeDtypeStruct((ng, nc, r, p, SSM_STATE), F32)],
        scratch_shapes=[pltpu.VMEM((r, p, SSM_STATE), F32)],
        compiler_params=_cparams(2), name=name,
    )(xbc, xbc, xbc, dt_rows, alog, dtb)


def _ssd_bwd(xbc, dt_rows, alog, dtb, states, dy, *, name):
    s = xbc.shape[0]
    ng, nc, r = dt_rows.shape[0], dt_rows.shape[1], dt_rows.shape[2]
    w = ng * r * SSM_HEAD_DIM
    x_spec, b_spec, c_spec, dt_spec, sc_spec, st_spec, y_spec, bc_out = _ssd_specs(ng, nc, r, True)
    p = SSM_HEAD_DIM

    def body(x_ref, b_ref, c_ref, dt_ref, al_ref, db_ref, st_ref, dy_ref,
             dx_ref, dbm_ref, dcm_ref, ddt_ref, dal_ref, ddb_ref, dstate):
        @pl.when(pl.program_id(1) == 0)
        def _():
            dstate[...] = jnp.zeros_like(dstate)
            dal_ref[...] = jnp.zeros_like(dal_ref)
            ddb_ref[...] = jnp.zeros_like(ddb_ref)

        xs = [x_ref[:, h * p:(h + 1) * p] for h in range(r)]
        dts = [dt_ref[h:h + 1, :] for h in range(r)]
        als = [al_ref[h:h + 1, :] for h in range(r)]
        dbs = [db_ref[h:h + 1, :] for h in range(r)]
        h0s = [st_ref[h] for h in range(r)]
        _, vjp = jax.vjp(_ssd_group, xs, dts, als, dbs, b_ref[...], c_ref[...], h0s)
        dys = [dy_ref[:, h * p:(h + 1) * p] for h in range(r)]
        dh1s = [dstate[h] for h in range(r)]
        dxs, ddts, dals, ddbs, dbm, dcm, dh0s = vjp((dys, dh1s))
        dbm_ref[...] = dbm
        dcm_ref[...] = dcm
        for h in range(r):
            dx_ref[:, h * p:(h + 1) * p] = dxs[h]
            ddt_ref[h:h + 1, :] = ddts[h]
            dal_ref[h:h + 1, :] += dals[h]
            ddb_ref[h:h + 1, :] += ddbs[h]
            dstate[h] = dh0s[h]

    gn = ng * SSM_STATE
    return pl.pallas_call(
        body, grid=(ng, nc),
        in_specs=[x_spec, b_spec, c_spec, dt_spec, sc_spec, sc_spec, st_spec, y_spec],
        out_specs=[y_spec, bc_out, bc_out, dt_spec, sc_spec, sc_spec],
        out_shape=[jax.ShapeDtypeStruct((s, w), F32), jax.ShapeDtypeStruct((s, gn), F32), jax.ShapeDtypeStruct((s, gn), F32),
                   jax.ShapeDtypeStruct(dt_rows.shape, F32), jax.ShapeDtypeStruct((ng, r, 1), F32),
                   jax.ShapeDtypeStruct((ng, r, 1), F32)],
        scratch_shapes=[pltpu.VMEM((r, p, SSM_STATE), F32)],
        compiler_params=_cparams(2), name=name,
    )(xbc, xbc, xbc, dt_rows, alog, dtb, states, dy)


def _ssm_post_fwd(y, xbc, src, z_col0, dexp, nw, *, name, tm=256):
    s, w = y.shape
    gw = w // SSM_GROUPS
    zc = z_col0 * LANES // gw

    def body(y_ref, x_ref, z_ref, d_ref, w_ref, o_ref):
        yy = (y_ref[...] + x_ref[...] * d_ref[...]) * _silu(z_ref[...])
        r = lax.rsqrt(jnp.mean(yy * yy, axis=-1, keepdims=True) + EPS)
        o_ref[...] = (yy * r * w_ref[...]).astype(o_ref.dtype)

    blk = pl.BlockSpec((tm, gw), lambda g, i: (i, g))
    vec = pl.BlockSpec((1, gw), lambda g, i: (0, g))
    return pl.pallas_call(
        body, grid=(SSM_GROUPS, s // tm),
        in_specs=[blk, blk, pl.BlockSpec((tm, gw), lambda g, i: (i, zc + g)), vec, vec],
        out_specs=blk, out_shape=jax.ShapeDtypeStruct((s, w), MXU_DTYPE),
        compiler_params=_cparams(2), name=name,
    )(y, xbc, src, dexp.reshape(1, w), nw.reshape(1, w))


def _ssm_post_bwd(y, xbc, src, z_col0, dexp, nw, dout, *, name, tm=256):
    s, w = y.shape
    gw = w // SSM_GROUPS
    zc = z_col0 * LANES // gw

    def body(y_ref, x_ref, z_ref, d_ref, w_ref, do_ref, dy_ref, dx_ref, dz_ref, dd_ref, dw_ref):
        xv, zv, dv = x_ref[...], z_ref[...], d_ref[...]
        pre = y_ref[...] + xv * dv
        sz = _silu(zv)
        yy = pre * sz
        r = lax.rsqrt(jnp.mean(yy * yy, axis=-1, keepdims=True) + EPS)
        yh = yy * r
        dov = do_ref[...]
        dyn = dov * w_ref[...]
        dyy = r * (dyn - yh * jnp.mean(dyn * yh, axis=-1, keepdims=True))
        dpre = dyy * sz
        dy_ref[...] = dpre
        dx_ref[...] = dpre * dv
        dz_ref[...] = (dyy * pre * _silu_grad(zv)).astype(dz_ref.dtype)

        @pl.when(pl.program_id(1) == 0)
        def _():
            dd_ref[...] = jnp.zeros_like(dd_ref)
            dw_ref[...] = jnp.zeros_like(dw_ref)

        dd_ref[...] += jnp.sum(dpre * xv, axis=0, keepdims=True)
        dw_ref[...] += jnp.sum(dov * yh, axis=0, keepdims=True)

    blk = pl.BlockSpec((tm, gw), lambda g, i: (i, g))
    vec = pl.BlockSpec((1, gw), lambda g, i: (0, g))
    dy, dx, dz, dd, dw = pl.pallas_call(
        body, grid=(SSM_GROUPS, s // tm),
        in_specs=[blk, blk, pl.BlockSpec((tm, gw), lambda g, i: (i, zc + g)), vec, vec, blk],
        out_specs=[blk, blk, blk, vec, vec],
        out_shape=[jax.ShapeDtypeStruct((s, w), F32), jax.ShapeDtypeStruct((s, w), F32), jax.ShapeDtypeStruct((s, w), MXU_DTYPE),
                   jax.ShapeDtypeStruct((1, w), F32), jax.ShapeDtypeStruct((1, w), F32)],
        compiler_params=_cparams(2), name=name,
    )(y, xbc, src, dexp.reshape(1, w), nw.reshape(1, w), dout)
    return dy, dx, dz, dd.reshape(w), dw.reshape(w)


def _merge_fwd(proj3, src, gate_col0, d, *, name, tm=256):
    s = proj3.shape[0]
    nb = proj3.shape[1] // d
    gc = gate_col0 * LANES // d

    def body(*refs):
        p_refs, g_refs, o_ref = refs[:nb], refs[nb:2 * nb], refs[-1]
        acc = None
        for p_ref, g_ref in zip(p_refs, g_refs):
            term = _sigmoid(g_ref[...]) * p_ref[...]
            acc = term if acc is None else acc + term
        o_ref[...] = acc.astype(o_ref.dtype)

    p_specs = [pl.BlockSpec((tm, d), lambda i, b=b: (i, b)) for b in range(nb)]
    g_specs = [pl.BlockSpec((tm, d), lambda i, b=b: (i, gc + b)) for b in range(nb)]
    return pl.pallas_call(
        body, grid=(s // tm,), in_specs=p_specs + g_specs,
        out_specs=pl.BlockSpec((tm, d), lambda i: (i, 0)), out_shape=jax.ShapeDtypeStruct((s, d), MXU_DTYPE),
        compiler_params=_cparams(1), name=name,
    )(*([proj3] * nb), *([src] * nb))


def _merge_bwd(proj3, src, gate_col0, d, dmerged, *, name, tm=256):
    s = proj3.shape[0]
    nb = proj3.shape[1] // d
    gc = gate_col0 * LANES // d

    def body(p_ref, g_ref, dm_ref, dp_ref, dg_ref):
        sg = _sigmoid(g_ref[...])
        dm = dm_ref[...]
        dp_ref[...] = (dm * sg).astype(dp_ref.dtype)
        dg_ref[...] = (dm * p_ref[...] * sg * (1.0 - sg)).astype(dg_ref.dtype)

    blk = pl.BlockSpec((tm, d), lambda i, b: (i, b))
    return pl.pallas_call(
        body, grid=(s // tm, nb),
        in_specs=[blk, pl.BlockSpec((tm, d), lambda i, b: (i, gc + b)), pl.BlockSpec((tm, d), lambda i, b: (i, 0))],
        out_specs=[blk, blk],
        out_shape=[jax.ShapeDtypeStruct(proj3.shape, MXU_DTYPE), jax.ShapeDtypeStruct(proj3.shape, MXU_DTYPE)],
        compiler_params=_cparams(2), name=name,
    )(proj3, src, dmerged)


ANY = pl.BlockSpec(memory_space=pl.ANY)
MESH = pl.DeviceIdType.MESH


def _all_gather(shards, *, name):
    nt = len(shards)

    def body(*refs):
        x_refs, out_refs = refs[:nt], refs[nt:2 * nt]
        send_sems, recv_sems, local_sems = refs[2 * nt:]
        x, y, c = lax.axis_index("x"), lax.axis_index("y"), lax.axis_index("c")
        me, sibling = (x, y, c), (x, y, 1 - c)
        chips = [(1 - x, y), (x, 1 - y), (1 - x, 1 - y)]

        def slot(t, px, py, pc):
            return out_refs[t].at[4 * px + 2 * py + pc]

        def copy(t, k, block, to, from_input=False):
            return pltpu.make_async_remote_copy(
                src_ref=x_refs[t] if from_input else slot(t, *block), dst_ref=slot(t, *block),
                send_sem=send_sems.at[7 * t + k], recv_sem=recv_sems.at[7 * t + k], device_id=to, device_id_type=MESH)

        mine = [pltpu.make_async_copy(x_refs[t], slot(t, *me), local_sems.at[t]) for t in range(nt)]
        for cp in mine:
            cp.start()
        first = [copy(t, 0, me, sibling, True) for t in range(nt)]
        first += [copy(t, 1 + j, me, (*chip, c), True) for j, chip in enumerate(chips) for t in range(nt)]
        for cp in first:
            cp.start()
        passed = []
        for j, chip in enumerate(chips):
            for t in range(nt):
                copy(t, 1 + j, (*chip, c), me).wait_recv()
                fwd = copy(t, 4 + j, (*chip, c), sibling)
                fwd.start()
                passed.append(fwd)
        for t in range(nt):
            copy(t, 0, sibling, me).wait_recv()
            for j, chip in enumerate(chips):
                copy(t, 4 + j, (*chip, 1 - c), me).wait_recv()
        for cp in first + passed:
            cp.wait_send()
        for cp in mine:
            cp.wait()

    return pl.pallas_call(
        body, out_shape=[jax.ShapeDtypeStruct((N_DEV,) + a.shape, a.dtype) for a in shards],
        in_specs=[ANY] * nt, out_specs=[ANY] * nt,
        scratch_shapes=[pltpu.SemaphoreType.DMA((7 * nt,)), pltpu.SemaphoreType.DMA((7 * nt,)),
                        pltpu.SemaphoreType.DMA((nt,))],
        name=name,
    )(*shards)


def _grad_exchange(bigs, small, *, name):
    nl = len(bigs[0])
    flat = [a for per_layer in bigs for a in per_layer]
    nslot = len(flat)

    def body(*refs):
        in_refs, small_ref = refs[:nslot], refs[nslot]
        out_refs, smallr_ref = refs[nslot + 1:nslot + 1 + len(bigs)], refs[nslot + 1 + len(bigs)]
        send_sems, recv_sems, local_sems = refs[nslot + 2 + len(bigs):]
        x, y, c = lax.axis_index("x"), lax.axis_index("y"), lax.axis_index("c")
        me = 4 * x + 2 * y + c
        local = [pltpu.make_async_copy(in_refs[i].at[me], out_refs[i // nl].at[me, i % nl], local_sems.at[i])
                 for i in range(nslot)]
        local.append(pltpu.make_async_copy(small_ref, smallr_ref.at[me], local_sems.at[nslot]))
        for cp in local:
            cp.start()
        copies = []
        for k in range(1, N_DEV):
            px = x ^ ((k >> 2) & 1)
            py = y ^ ((k >> 1) & 1)
            pc = c ^ (k & 1)
            peer = 4 * px + 2 * py + pc
            for i in range(nslot + 1):
                sem = 7 * i + (k - 1)
                src = in_refs[i].at[peer] if i < nslot else small_ref
                dst = out_refs[i // nl].at[me, i % nl] if i < nslot else smallr_ref.at[me]
                copies.append(pltpu.make_async_remote_copy(
                    src_ref=src, dst_ref=dst, send_sem=send_sems.at[sem], recv_sem=recv_sems.at[sem],
                    device_id=(px, py, pc), device_id_type=MESH))
        for cp in copies:
            cp.start()
        for cp in copies:
            cp.wait_recv()
        for cp in copies:
            cp.wait_send()
        for cp in local:
            cp.wait()

    out_shape = [jax.ShapeDtypeStruct((N_DEV, nl) + per_layer[0].shape[1:], per_layer[0].dtype) for per_layer in bigs]
    out_shape.append(jax.ShapeDtypeStruct((N_DEV,) + small.shape, small.dtype))
    nsem = 7 * (nslot + 1)
    outs = pl.pallas_call(
        body, out_shape=out_shape,
        in_specs=[ANY] * (nslot + 1), out_specs=[ANY] * (len(bigs) + 1),
        scratch_shapes=[pltpu.SemaphoreType.DMA((nsem,)), pltpu.SemaphoreType.DMA((nsem,)),
                        pltpu.SemaphoreType.DMA((nslot + 1,))],
        name=name,
    )(*flat, small)
    return outs[:-1], outs[-1]


def _adam_math(w, g, m, v):
    m1 = ADAM_B1 * m + (1.0 - ADAM_B1) * g
    v1 = ADAM_B2 * v + (1.0 - ADAM_B2) * (g * g)
    m_hat = m1 / (1.0 - ADAM_B1 ** ADAM_STEP)
    v_hat = v1 / (1.0 - ADAM_B2 ** ADAM_STEP)
    delta = -ADAM_LR * (m_hat / (jnp.sqrt(v_hat) + ADAM_EPS) + ADAM_WD * w)
    return delta, m1, v1


def _sum_adamw(parts, w, m, v, *, name):
    shape = w.shape
    r, c = shape[-2], shape[-1]
    a = math.prod(shape[:-2])
    tr = _pick(r, (256,) if c <= 1024 else (128,))
    w3, m3, v3 = (t.reshape(a, r, c) for t in (w, m, v))

    def body(p_ref, w_ref, m_ref, v_ref, g_ref, d_ref, m1_ref, v1_ref):
        g = p_ref[0].astype(F32)
        for src in range(1, N_DEV):
            g = g + p_ref[src].astype(F32)
        delta, m1, v1 = _adam_math(w_ref[...], g, m_ref[...], v_ref[...])
        g_ref[...] = g
        d_ref[...] = delta
        m1_ref[...] = m1
        v1_ref[...] = v1

    blk = pl.BlockSpec((None, tr, c), lambda i, j: (i, j, 0))
    outs = pl.pallas_call(
        body, grid=(a, r // tr),
        in_specs=[pl.BlockSpec((N_DEV, None, tr, c), lambda i, j: (0, i, j, 0)), blk, blk, blk],
        out_specs=[blk] * 4, out_shape=[jax.ShapeDtypeStruct((a, r, c), F32)] * 4,
        compiler_params=_cparams(2), name=name,
    )(parts.reshape(N_DEV, a, r, c), w3, m3, v3)
    return [o.reshape(shape) for o in outs]


def _sum_parts(parts, *, name):
    rows = parts.shape[1]

    def body(p_ref, o_ref):
        g = p_ref[0]
        for src in range(1, N_DEV):
            g = g + p_ref[src]
        o_ref[...] = g

    return pl.pallas_call(
        body, grid=(1,), in_specs=[pl.BlockSpec((N_DEV, rows, LANES), lambda i: (0, 0, 0))],
        out_specs=pl.BlockSpec((rows, LANES), lambda i: (0, 0)), out_shape=jax.ShapeDtypeStruct((rows, LANES), F32),
        compiler_params=_cparams(1), name=name,
    )(parts)


def _adamw(w, g, m, v, *, name):
    rows = w.shape[0]

    def body(w_ref, g_ref, m_ref, v_ref, d_ref, m1_ref, v1_ref):
        delta, m1, v1 = _adam_math(w_ref[...], g_ref[...], m_ref[...], v_ref[...])
        d_ref[...] = delta
        m1_ref[...] = m1
        v1_ref[...] = v1

    blk = pl.BlockSpec((rows, LANES), lambda i: (0, 0))
    return pl.pallas_call(
        body, grid=(1,), in_specs=[blk] * 4, out_specs=[blk] * 3,
        out_shape=[jax.ShapeDtypeStruct((rows, LANES), F32)] * 3,
        compiler_params=_cparams(1), name=name,
    )(w, g, m, v)


def _pack(arrs, dtype, row_mult=16):
    flat = jnp.concatenate([a.reshape(-1).astype(dtype) for a in arrs])
    n = flat.shape[0]
    rows = -(-n // (LANES * row_mult)) * row_mult
    flat = jnp.pad(flat, (0, rows * LANES - n))
    return flat.reshape(rows, LANES)


def _unpack(packed, shapes):
    flat = packed.reshape(-1)
    out, off = [], 0
    for shp in shapes:
        n = math.prod(shp)
        out.append(flat[off:off + n].reshape(shp))
        off += n
    return out


class _Layout:
    def __init__(self, d):
        self.d = d
        w = d
        self.dn_heads = w // DN_HEAD_DIM
        self.ssm_heads = w // SSM_HEAD_DIM
        gn = SSM_GROUPS * SSM_STATE
        self.sizes = (3 * w, w, self.dn_heads, self.dn_heads, 3 * w, w, w + 2 * gn, self.ssm_heads, 3 * d)
        offs, o = [], 0
        for sz in self.sizes:
            offs.append(o)
            o += sz
        self.offs = offs
        self.in_dim = o
        self.big = (0, 1, 4, 5, 6, 8)
        self.small = (2, 3, 7)
        cols, o = {}, 0
        for idx in self.big:
            cols[idx] = o
            o += self.sizes[idx]
        self.small_col = o
        self.cols = cols
        self.padded = o + LANES
        self.n_small = sum(self.sizes[i] for i in self.small)

    def reorder_w(self, w_in):
        parts = [w_in[:, self.offs[i]:self.offs[i] + self.sizes[i]] for i in self.big + self.small]
        parts.append(jnp.zeros((w_in.shape[0], LANES - self.n_small), w_in.dtype))
        return jnp.concatenate(parts, axis=1)

    def from_shards(self, parts):
        cs = self.in_dim // N_DEV
        pieces = []
        for i in self.big + self.small:
            a, b = self.offs[i], self.offs[i] + self.sizes[i]
            while a < b:
                j = a // cs
                hi = min(b, (j + 1) * cs)
                pieces.append(parts[j][:, a - j * cs:hi - j * cs])
                a = hi
        pieces.append(jnp.zeros((parts.shape[1], LANES - self.n_small), parts.dtype))
        return jnp.concatenate(pieces, axis=1)

    def to_shards(self, wp):
        cs = self.in_dim // N_DEV
        pcol = dict(self.cols)
        o = self.small_col
        for i in self.small:
            pcol[i] = o
            o += self.sizes[i]
        shards = []
        for j in range(N_DEV):
            a, b = j * cs, (j + 1) * cs
            pieces = []
            for i in range(len(self.sizes)):
                lo, hi = max(a, self.offs[i]), min(b, self.offs[i] + self.sizes[i])
                if lo < hi:
                    pieces.append(wp[:, pcol[i] + lo - self.offs[i]:pcol[i] + hi - self.offs[i]])
            shards.append(jnp.concatenate(pieces, axis=1))
        return jnp.stack(shards)

    def restore_w(self, wp):
        pieces = {}
        for idx in self.big:
            pieces[idx] = wp[:, self.cols[idx]:self.cols[idx] + self.sizes[idx]]
        o = self.small_col
        for idx in self.small:
            pieces[idx] = wp[:, o:o + self.sizes[idx]]
            o += self.sizes[idx]
        return jnp.concatenate([pieces[i] for i in range(len(self.sizes))], axis=1)


def _rows_form(cols_t, nh, nc):
    return cols_t.T.reshape(nh, nc, 1, CHUNK)


def _layer_fwd(x, p, lay, tag):
    s, d = x.shape
    nc = s // CHUNK
    w = d
    dnh, smh = lay.dn_heads, lay.ssm_heads
    r = smh // SSM_GROUPS
    cb = {k: v // LANES for k, v in lay.cols.items()}
    sv = {}
    h1 = _rms_fwd(x, p["norm_mix"], name=f"rms_mix_{tag}")
    proj = _matmul(h1, p["w_in"], name=f"mm_in_{tag}")
    small = proj[:, lay.small_col:lay.small_col + LANES]
    a_rows = _rows_form(small[:, 0:dnh], dnh, nc)
    b_rows = _rows_form(small[:, dnh:2 * dnh], dnh, nc)
    dt_rows = small[:, 2 * dnh:2 * dnh + smh].T.reshape(SSM_GROUPS, r, nc, CHUNK).transpose(0, 2, 1, 3)
    zero_b = jnp.zeros((1, 3 * w), F32)
    dn_qkv = _conv_fwd(proj, cb[0], p["dn_conv_w"], zero_b, 2 * dnh, name=f"dn_conv_{tag}")
    dn_alog = p["dn_a_log"].reshape(dnh, 1, 1)
    dn_dtb = p["dn_dt_bias"].reshape(dnh, 1, 1)
    o_dn, dn_states = _dn_fwd(dn_qkv, a_rows, b_rows, dn_alog, dn_dtb, name=f"dn_chunk_{tag}")
    y_dn = _dn_post_fwd(o_dn, proj, cb[1], p["dn_norm_w"], name=f"dn_post_{tag}")
    o_sb, sb_r = _sb_fwd(proj, cb[4], w, name=f"sb_{tag}")
    xbc = _conv_fwd(proj, cb[6], p["ssm_conv_w"], p["ssm_conv_b"].reshape(1, -1), 0, name=f"ssm_conv_{tag}")
    ssm_alog = p["ssm_a_log"].reshape(SSM_GROUPS, r, 1)
    ssm_dtb = p["ssm_dt_bias"].reshape(SSM_GROUPS, r, 1)
    y_ssd, ssm_states = _ssd_fwd(xbc, dt_rows, ssm_alog, ssm_dtb, name=f"ssd_{tag}")
    dexp = jnp.repeat(p["ssm_d"], SSM_HEAD_DIM)
    y_ssm = _ssm_post_fwd(y_ssd, xbc, proj, cb[5], dexp, p["ssm_norm_w"], name=f"ssm_post_{tag}")
    branches = (y_dn, o_sb, y_ssm)
    proj3 = jnp.concatenate(
        [_matmul(br, p["w_branch"][i], name=f"mm_branch{i}_{tag}") for i, br in enumerate(branches)], axis=1)
    merged = _merge_fwd(proj3, proj, cb[8], d, name=f"merge_{tag}")
    x1 = _matmul(merged, p["w_out"], name=f"mm_out_{tag}", epilogue=lambda acc, res: (acc + res,), extras=(x,))
    h2 = _rms_fwd(x1, p["norm_mlp"], name=f"rms_mlp_{tag}")
    u, act = _matmul(h2, p["w_up"], name=f"mm_up_{tag}", out_dtypes=(F32, MXU_DTYPE),
                     epilogue=lambda acc: (acc, jnp.square(jnp.maximum(acc, 0.0))))
    x2 = _matmul(act, p["w_down"], name=f"mm_down_{tag}", epilogue=lambda acc, res: (acc + res,), extras=(x1,))
    sv.update(x=x, h1=h1, proj=proj, a_rows=a_rows, b_rows=b_rows, dt_rows=dt_rows, dn_qkv=dn_qkv, dn_alog=dn_alog,
              dn_dtb=dn_dtb, o_dn=o_dn, dn_states=dn_states, y_dn=y_dn, o_sb=o_sb, sb_r=sb_r, xbc=xbc, ssm_alog=ssm_alog,
              ssm_dtb=ssm_dtb, y_ssd=y_ssd, ssm_states=ssm_states, dexp=dexp, y_ssm=y_ssm, proj3=proj3, merged=merged,
              x1=x1, h2=h2, u=u, act=act)
    return x2, sv


def _layer_bwd(dx2, p, sv, lay, tag):
    x = sv["x"]
    s, d = x.shape
    nc = s // CHUNK
    w = d
    dnh, smh = lay.dn_heads, lay.ssm_heads
    r = smh // SSM_GROUPS
    gn = SSM_GROUPS * SSM_STATE
    cb = {k: v // LANES for k, v in lay.cols.items()}
    proj = sv["proj"]
    g = {}
    dx2_b = dx2.astype(MXU_DTYPE)
    du = _matmul(dx2_b, p["w_down"], tb=True, name=f"mm_down_dx_{tag}", out_dtypes=(MXU_DTYPE,),
                 epilogue=lambda acc, uu: (acc * (2.0 * jnp.maximum(uu, 0.0)),), extras=(sv["u"],))
    g["w_down"] = _matmul(sv["act"], dx2_b, ta=True, name=f"mm_down_dw_{tag}", out_dtypes=(BF16,)).reshape(N_DEV, -1, d)
    g["w_up"] = _matmul(sv["h2"], du, ta=True, name=f"mm_up_dw_{tag}", out_dtypes=(BF16,), col_shards=N_DEV)
    dh2 = _matmul(du, p["w_up"], tb=True, name=f"mm_up_dx_{tag}")
    dx1, g["norm_mlp"] = _rms_bwd(sv["x1"], p["norm_mlp"], dh2, dx2, name=f"rms_mlp_bwd_{tag}")
    dx1_b = dx1.astype(MXU_DTYPE)
    dmerged = _matmul(dx1_b, p["w_out"], tb=True, name=f"mm_out_dx_{tag}")
    g["w_out"] = _matmul(sv["merged"], dx1_b, ta=True, name=f"mm_out_dw_{tag}", out_dtypes=(BF16,)).reshape(N_DEV, -1, d)
    dproj3, dgates = _merge_bwd(sv["proj3"], proj, cb[8], d, dmerged, name=f"merge_bwd_{tag}")
    branches = (sv["y_dn"], sv["o_sb"], sv["y_ssm"])
    dwb, dbr = [], []
    for i, br in enumerate(branches):
        dp_i = dproj3[:, i * d:(i + 1) * d]
        dwb.append(_matmul(br, dp_i, ta=True, name=f"mm_branch{i}_dw_{tag}", out_dtypes=(BF16,)).reshape(N_DEV, -1, d))
        dbr.append(_matmul(dp_i, p["w_branch"][i], tb=True, name=f"mm_branch{i}_dx_{tag}"))
    g["w_branch"] = jnp.stack(dwb, axis=1)
    dy_dn, do_sb, dy_ssm = dbr
    dy_ssd, dxs_skip, dz, ddexp, g["ssm_norm_w"] = _ssm_post_bwd(
        sv["y_ssd"], sv["xbc"], proj, cb[5], sv["dexp"], p["ssm_norm_w"], dy_ssm, name=f"ssm_post_bwd_{tag}")
    g["ssm_d"] = ddexp.reshape(smh, SSM_HEAD_DIM).sum(axis=1)
    dxs, dbm, dcm, ddt_rows, dalog, ddtb = _ssd_bwd(
        sv["xbc"], sv["dt_rows"], sv["ssm_alog"], sv["ssm_dtb"], sv["ssm_states"], dy_ssd, name=f"ssd_bwd_{tag}")
    g["ssm_a_log"] = dalog.reshape(smh)
    g["ssm_dt_bias"] = ddtb.reshape(smh)
    dxbc_post = jnp.concatenate([dxs + dxs_skip, dbm, dcm], axis=1)
    dxbc, g["ssm_conv_w"], dcb = _conv_bwd(proj, cb[6], p["ssm_conv_w"], p["ssm_conv_b"].reshape(1, -1), 0, dxbc_post,
                                           name=f"ssm_conv_bwd_{tag}")
    g["ssm_conv_b"] = dcb.reshape(-1)
    ddt = ddt_rows.transpose(0, 2, 1, 3).reshape(smh, s).T
    dq_sb, dk_sb, dv_sb = _sb_bwd(proj, cb[4], w, sv["sb_r"], do_sb, name=f"sb_bwd_{tag}")
    do_dn, dgate_dn, g["dn_norm_w"] = _dn_post_bwd(sv["o_dn"], proj, cb[1], p["dn_norm_w"], dy_dn, name=f"dn_post_bwd_{tag}")
    dq, dk, dv, da_rows, db_rows, dal, ddtb_dn = _dn_bwd(
        sv["dn_qkv"], sv["a_rows"], sv["b_rows"], sv["dn_alog"], sv["dn_dtb"], sv["dn_states"], do_dn, name=f"dn_chunk_bwd_{tag}")
    g["dn_a_log"] = dal.reshape(dnh)
    g["dn_dt_bias"] = ddtb_dn.reshape(dnh)
    zero_b = jnp.zeros((1, 3 * w), F32)
    ddn_qkv, g["dn_conv_w"], _ = _conv_bwd(proj, cb[0], p["dn_conv_w"], zero_b, 2 * dnh,
                                           jnp.concatenate([dq, dk, dv], axis=1), name=f"dn_conv_bwd_{tag}")
    da = da_rows.reshape(dnh, s).T
    db = db_rows.reshape(dnh, s).T
    dsmall = jnp.concatenate([da, db, ddt, jnp.zeros((s, LANES - lay.n_small), F32)], axis=1).astype(MXU_DTYPE)
    dproj = jnp.concatenate(
        [ddn_qkv, dgate_dn, dq_sb.astype(MXU_DTYPE), dk_sb.astype(MXU_DTYPE), dv_sb.astype(MXU_DTYPE), dz, dxbc, dgates, dsmall],
        axis=1)
    g["w_in"] = lay.to_shards(_matmul(sv["h1"], dproj, ta=True, name=f"mm_in_dw_{tag}", out_dtypes=(BF16,)))
    dh1 = _matmul(dproj, p["w_in"], tb=True, name=f"mm_in_dx_{tag}")
    dx0, g["norm_mix"] = _rms_bwd(x, p["norm_mix"], dh1, dx1, name=f"rms_mix_bwd_{tag}")
    return dx0, g


BIG = ("w_in", "w_branch", "w_out", "w_up", "w_down")
CONV = ("dn_conv_w", "ssm_conv_w")
SMALL = ("norm_mix", "dn_conv_w", "dn_a_log", "dn_dt_bias", "dn_norm_w", "ssm_conv_w", "ssm_conv_b", "ssm_a_log",
         "ssm_dt_bias", "ssm_d", "ssm_norm_w", "norm_mlp", "norm_final")
WEIGHTS = ("norm_mix", "w_in", "dn_conv_w", "dn_a_log", "dn_dt_bias", "dn_norm_w", "ssm_conv_w", "ssm_conv_b", "ssm_a_log",
           "ssm_dt_bias", "ssm_d", "ssm_norm_w", "w_branch", "w_out", "norm_mlp", "w_up", "w_down", "norm_final")
SHARD_AXIS = {"w_in": 2, "dn_conv_w": 2, "ssm_conv_w": 2, "w_branch": 2, "w_out": 1, "w_up": 2, "w_down": 1}


def _to_shards(full, axis):
    shp = full.shape
    n = shp[axis] // N_DEV
    t = full.reshape(shp[:axis] + (N_DEV, n) + shp[axis + 1:])
    return jnp.moveaxis(t, axis, 0)


def _from_shards(parts, axis):
    t = jnp.moveaxis(parts, 0, axis)
    shp = t.shape
    return t.reshape(shp[:axis] + (shp[axis] * shp[axis + 1],) + shp[axis + 2:])


def _step(w, m, v, x, target):
    s, d = x.shape
    lay = _Layout(d)
    me = 4 * lax.axis_index("x") + 2 * lax.axis_index("y") + lax.axis_index("c")

    send = [w[n].astype(BF16) for n in BIG] + [w[n] for n in CONV]
    gathered = dict(zip(BIG + CONV, _all_gather(send, name="weight_all_gather")))

    def layer_params(l):
        p = {n: w[n][l] for n in WEIGHTS if n not in BIG + CONV + ("norm_final",)}
        p["w_in"] = lay.from_shards(gathered["w_in"][:, l])
        for n in BIG[1:] + CONV:
            p[n] = _from_shards(gathered[n][:, l], SHARD_AXIS[n] - 1)
        return p

    params = [layer_params(l) for l in range(DEPTH)]

    saved = []
    h = x
    for l in range(DEPTH):
        h, sv = _layer_fwd(h, params[l], lay, f"l{l}")
        saved.append(sv)
    loss, dh, g_norm_final = _final_loss(h, w["norm_final"], target, name="final_loss")
    grads = [None] * DEPTH
    for l in reversed(range(DEPTH)):
        dh, grads[l] = _layer_bwd(dh, params[l], saved[l], lay, f"l{l}")
    grad_x = dh
    gfull = {n: jnp.stack([grads[l][n] for l in range(DEPTH)]) for n in SMALL if n != "norm_final"}
    gfull["norm_final"] = g_norm_final

    small_send = _pack([gfull[n] for n in SMALL] + [loss.reshape(1)], F32)
    big_recv, small_recv = _grad_exchange([[grads[l][n] for l in range(DEPTH)] for n in BIG], small_send,
                                          name="grad_exchange")

    out = {"grad": {}, "delta": {}, "new_m": {}, "new_v": {}}
    for n, parts in zip(BIG, big_recv):
        res = _sum_adamw(parts, w[n], m[n], v[n], name=f"sum_adamw_{n}")
        for key, a in zip(("grad", "delta", "new_m", "new_v"), res):
            out[key][n] = a
    small_sum = _sum_parts(small_recv, name="sum_small")
    small_full = _unpack(small_sum, [gfull[n].shape for n in SMALL] + [(1,)])
    loss_total = small_full[-1][0]
    gsmall = {}
    for n, a in zip(SMALL, small_full[:-1]):
        if n in SHARD_AXIS:
            a = lax.dynamic_index_in_dim(_to_shards(a, SHARD_AXIS[n]), me, axis=0, keepdims=False)
        gsmall[n] = a
    small_shapes = [w[n].shape for n in SMALL]
    ws, gs, ms, vs = (_pack([t[n] for n in SMALL], F32) for t in (w, gsmall, m, v))
    ds, m1s, v1s = _adamw(ws, gs, ms, vs, name="adamw_small")
    for n in SMALL:
        out["grad"][n] = gsmall[n]
    for key, packed in (("delta", ds), ("new_m", m1s), ("new_v", v1s)):
        for n, a in zip(SMALL, _unpack(packed, small_shapes)):
            out[key][n] = a
    return loss_total, grad_x, out


def kernel(x, norm_mix, w_in, dn_conv_w, dn_a_log, dn_dt_bias, dn_norm_w, ssm_conv_w, ssm_conv_b, ssm_a_log, ssm_dt_bias, ssm_d, ssm_norm_w, w_branch, w_out, norm_mlp, w_up, w_down, norm_final, loss_target, m_norm_mix, m_w_in, m_dn_conv_w, m_dn_a_log, m_dn_dt_bias, m_dn_norm_w, m_ssm_conv_w, m_ssm_conv_b, m_ssm_a_log, m_ssm_dt_bias, m_ssm_d, m_ssm_norm_w, m_w_branch, m_w_out, m_norm_mlp, m_w_up, m_w_down, m_norm_final, v_norm_mix, v_w_in, v_dn_conv_w, v_dn_a_log, v_dn_dt_bias, v_dn_norm_w, v_ssm_conv_w, v_ssm_conv_b, v_ssm_a_log, v_ssm_dt_bias, v_ssm_d, v_ssm_norm_w, v_w_branch, v_w_out, v_norm_mlp, v_w_up, v_w_down, v_norm_final):
    w = dict(norm_mix=norm_mix, w_in=w_in, dn_conv_w=dn_conv_w, dn_a_log=dn_a_log, dn_dt_bias=dn_dt_bias, dn_norm_w=dn_norm_w,
             ssm_conv_w=ssm_conv_w, ssm_conv_b=ssm_conv_b, ssm_a_log=ssm_a_log, ssm_dt_bias=ssm_dt_bias, ssm_d=ssm_d,
             ssm_norm_w=ssm_norm_w, w_branch=w_branch, w_out=w_out, norm_mlp=norm_mlp, w_up=w_up, w_down=w_down,
             norm_final=norm_final)
    m = dict(norm_mix=m_norm_mix, w_in=m_w_in, dn_conv_w=m_dn_conv_w, dn_a_log=m_dn_a_log, dn_dt_bias=m_dn_dt_bias,
             dn_norm_w=m_dn_norm_w, ssm_conv_w=m_ssm_conv_w, ssm_conv_b=m_ssm_conv_b, ssm_a_log=m_ssm_a_log,
             ssm_dt_bias=m_ssm_dt_bias, ssm_d=m_ssm_d, ssm_norm_w=m_ssm_norm_w, w_branch=m_w_branch, w_out=m_w_out,
             norm_mlp=m_norm_mlp, w_up=m_w_up, w_down=m_w_down, norm_final=m_norm_final)
    v = dict(norm_mix=v_norm_mix, w_in=v_w_in, dn_conv_w=v_dn_conv_w, dn_a_log=v_dn_a_log, dn_dt_bias=v_dn_dt_bias,
             dn_norm_w=v_dn_norm_w, ssm_conv_w=v_ssm_conv_w, ssm_conv_b=v_ssm_conv_b, ssm_a_log=v_ssm_a_log,
             ssm_dt_bias=v_ssm_dt_bias, ssm_d=v_ssm_d, ssm_norm_w=v_ssm_norm_w, w_branch=v_w_branch, w_out=v_w_out,
             norm_mlp=v_norm_mlp, w_up=v_w_up, w_down=v_w_down, norm_final=v_norm_final)
    loss, grad_x, out = _step(w, m, v, x[0], loss_target[0])
    return (loss, grad_x[None], *[out["grad"][n] for n in WEIGHTS], *[out["delta"][n] for n in WEIGHTS],
            *[out["new_m"][n] for n in WEIGHTS], *[out["new_v"][n] for n in WEIGHTS])
```

```python
import functools
import math

import jax
import jax.numpy as jnp
from jax import lax
from jax.experimental import pallas as pl
from jax.experimental.pallas import tpu as pltpu

F32 = jnp.float32
BF16 = jnp.bfloat16
MXU_DTYPE = BF16
HIGHEST = lax.Precision.HIGHEST

N_DEV = 8
DEPTH = 2
EPS = 1e-6
CONV_K = 4
DN_HEAD_DIM = 128
SB_HEAD_DIM = 64
SSM_HEAD_DIM = 64
SSM_STATE = 128
SSM_GROUPS = 4
CHUNK = 64
SB_BLOCK = 128
LANES = 128
ADAM_LR, ADAM_B1, ADAM_B2, ADAM_EPS, ADAM_WD, ADAM_STEP = 0.001, 0.9, 0.999, 1e-08, 0.01, 10
NEG_BIG = -1e30
DN_HEADS_PER_STEP = 8
SB_UNROLL = 4
SB_SPLIT = 2
CHUNK_PREC = lax.Precision.HIGH

ARB = "arbitrary"


def _cparams(n_axes):
    return pltpu.CompilerParams(dimension_semantics=(ARB,) * n_axes)


def _softplus(x):
    return jnp.maximum(x, 0.0) + jnp.log1p(jnp.exp(-jnp.abs(x)))


def _sigmoid(x):
    return 1.0 / (1.0 + jnp.exp(-x))


def _silu(x):
    return x * _sigmoid(x)


def _silu_grad(x):
    s = _sigmoid(x)
    return s * (1.0 + x * (1.0 - s))


def _dot(a, b, dims, prec=None):
    return lax.dot_general(a, b, (dims, ((), ())), precision=prec, preferred_element_type=F32)


NN = ((1,), (0,))
NT = ((1,), (1,))
TN = ((0,), (0,))


def _hdot(a, b, dims=NN):
    return _dot(a, b, dims, CHUNK_PREC)


def _bdot(a, b, dims=NN):
    return _dot(a.astype(MXU_DTYPE), b.astype(MXU_DTYPE), dims)


def _split_dot(a, m_bf16, nsplit=3):
    out = None
    rem = a
    for _ in range(nsplit):
        piece = rem.astype(BF16)
        rem = rem - piece.astype(F32)
        term = _dot(piece, m_bf16, NN)
        out = term if out is None else out + term
    return out


def _pick(n, pref):
    for t in pref:
        if n % t == 0:
            return t
    return n


def _matmul(a, b, *, ta=False, tb=False, name, epilogue=None, extras=(), out_dtypes=(F32,), col_shards=1,
            tm=None, tn=None, tk=None):
    m, k = (a.shape[1], a.shape[0]) if ta else a.shape
    k2, n = (b.shape[1], b.shape[0]) if tb else b.shape
    assert k == k2, (a.shape, b.shape, ta, tb)
    ncs = n // col_shards
    tm = tm or _pick(m, (512, 256, 128))
    tn = tn or _pick(ncs, (1024, 640, 512, 384, 256, 128))
    tk = tk or _pick(k, (1920, 1024, 640, 512, 256, 128))
    nk = k // tk
    a_spec = pl.BlockSpec((tk, tm), lambda i, j, kk: (kk, i)) if ta else pl.BlockSpec((tm, tk), lambda i, j, kk: (i, kk))
    b_spec = pl.BlockSpec((tn, tk), lambda i, j, kk: (j, kk)) if tb else pl.BlockSpec((tk, tn), lambda i, j, kk: (kk, j))
    e_spec = pl.BlockSpec((tm, tn), lambda i, j, kk: (i, j))
    if col_shards == 1:
        o_spec, o_shape = e_spec, (m, n)
    else:
        per = ncs // tn
        o_spec, o_shape = pl.BlockSpec((None, tm, tn), lambda i, j, kk: (j // per, i, j % per)), (col_shards, m, ncs)
    dims = (((0,) if ta else (1,)), ((1,) if tb else (0,)))
    n_extra = len(extras)
    n_out = len(out_dtypes)

    def body(*refs):
        a_ref, b_ref = refs[0], refs[1]
        extra_refs = refs[2:2 + n_extra]
        out_refs = refs[2 + n_extra:2 + n_extra + n_out]
        acc_ref = refs[-1]
        kk = pl.program_id(2)

        @pl.when(kk == 0)
        def _():
            acc_ref[...] = jnp.zeros_like(acc_ref)

        acc_ref[...] += _dot(a_ref[...].astype(MXU_DTYPE), b_ref[...].astype(MXU_DTYPE), dims)

        @pl.when(kk == nk - 1)
        def _():
            acc = acc_ref[...]
            outs = (acc,) if epilogue is None else epilogue(acc, *[r[...] for r in extra_refs])
            for o_ref, o in zip(out_refs, outs):
                o_ref[...] = o.astype(o_ref.dtype)

    outs = pl.pallas_call(
        body,
        grid=(m // tm, n // tn, nk),
        in_specs=[a_spec, b_spec] + [e_spec] * n_extra,
        out_specs=[o_spec] * n_out,
        out_shape=[jax.ShapeDtypeStruct(o_shape, dt) for dt in out_dtypes],
        scratch_shapes=[pltpu.VMEM((tm, tn), F32)],
        compiler_params=pltpu.CompilerParams(dimension_semantics=("parallel", "parallel", ARB)),
        name=name,
    )(a, b, *extras)
    return outs[0] if n_out == 1 else tuple(outs)


def _rms_fwd(x, w, *, name, tm=256):
    s, d = x.shape
    out_dtype = MXU_DTYPE

    def body(x_ref, w_ref, o_ref):
        xv = x_ref[...]
        r = lax.rsqrt(jnp.mean(xv * xv, axis=-1, keepdims=True) + EPS)
        o_ref[...] = (xv * r * w_ref[...]).astype(o_ref.dtype)

    return pl.pallas_call(
        body, grid=(s // tm,),
        in_specs=[pl.BlockSpec((tm, d), lambda i: (i, 0)), pl.BlockSpec((1, d), lambda i: (0, 0))],
        out_specs=pl.BlockSpec((tm, d), lambda i: (i, 0)),
        out_shape=jax.ShapeDtypeStruct((s, d), out_dtype),
        compiler_params=_cparams(1), name=name,
    )(x, w.reshape(1, d))


def _rms_bwd(x, w, dh, dres, *, name, tm=256):
    s, d = x.shape

    def body(x_ref, w_ref, dh_ref, dres_ref, dx_ref, dw_ref):
        xv = x_ref[...]
        r = lax.rsqrt(jnp.mean(xv * xv, axis=-1, keepdims=True) + EPS)
        xh = xv * r
        dhv = dh_ref[...].astype(F32)
        dxn = dhv * w_ref[...]
        dx = r * (dxn - xh * jnp.mean(dxn * xh, axis=-1, keepdims=True))
        dx_ref[...] = dres_ref[...] + dx

        @pl.when(pl.program_id(0) == 0)
        def _():
            dw_ref[...] = jnp.zeros_like(dw_ref)

        dw_ref[...] += jnp.sum(dhv * xh, axis=0, keepdims=True)

    dx, dw = pl.pallas_call(
        body, grid=(s // tm,),
        in_specs=[pl.BlockSpec((tm, d), lambda i: (i, 0)), pl.BlockSpec((1, d), lambda i: (0, 0)),
                  pl.BlockSpec((tm, d), lambda i: (i, 0)), pl.BlockSpec((tm, d), lambda i: (i, 0))],
        out_specs=[pl.BlockSpec((tm, d), lambda i: (i, 0)), pl.BlockSpec((1, d), lambda i: (0, 0))],
        out_shape=[jax.ShapeDtypeStruct((s, d), F32), jax.ShapeDtypeStruct((1, d), F32)],
        compiler_params=_cparams(1), name=name,
    )(x, w.reshape(1, d), dh, dres)
    return dx, dw.reshape(d)


def _final_loss(x, w, target, *, name, tm=256):
    s, d = x.shape

    def body(x_ref, w_ref, t_ref, loss_ref, dx_ref, dw_ref):
        xv = x_ref[...]
        r = lax.rsqrt(jnp.mean(xv * xv, axis=-1, keepdims=True) + EPS)
        xh = xv * r
        err = xh * w_ref[...] - t_ref[...]
        dy = err * (1.0 / d)
        dxn = dy * w_ref[...]
        dx_ref[...] = r * (dxn - xh * jnp.mean(dxn * xh, axis=-1, keepdims=True))

        @pl.when(pl.program_id(0) == 0)
        def _():
            dw_ref[...] = jnp.zeros_like(dw_ref)
            loss_ref[...] = jnp.zeros_like(loss_ref)

        dw_ref[...] += jnp.sum(dy * xh, axis=0, keepdims=True)
        row = jnp.sum(err * err, axis=1, keepdims=True) * (0.5 / d)
        loss_ref[...] += jnp.sum(row, axis=0, keepdims=True)

    loss, dx, dw = pl.pallas_call(
        body, grid=(s // tm,),
        in_specs=[pl.BlockSpec((tm, d), lambda i: (i, 0)), pl.BlockSpec((1, d), lambda i: (0, 0)),
                  pl.BlockSpec((tm, d), lambda i: (i, 0))],
        out_specs=[pl.BlockSpec((1, 1), lambda i: (0, 0)), pl.BlockSpec((tm, d), lambda i: (i, 0)),
                   pl.BlockSpec((1, d), lambda i: (0, 0))],
        out_shape=[jax.ShapeDtypeStruct((1, 1), F32), jax.ShapeDtypeStruct((s, d), F32), jax.ShapeDtypeStruct((1, d), F32)],
        compiler_params=_cparams(1), name=name,
    )(x, w.reshape(1, d), target)
    return loss[0, 0], dx, dw.reshape(d)


def _shift_down(x, sh, t_idx):
    return jnp.where(t_idx >= sh, pltpu.roll(x, sh, 0), 0.0)


def _shift_up(x, sh, t_idx, s):
    return jnp.where(t_idx < s - sh, pltpu.roll(x, s - sh, 0), 0.0)


def _conv_pre(x, w_rows, b, t_idx):
    c = w_rows[CONV_K - 1] * x + b
    for sh in range(1, CONV_K):
        c = c + w_rows[CONV_K - 1 - sh] * _shift_down(x, sh, t_idx)
    return c


def _conv_fwd(src, col0, w, b, n_l2, *, name):
    s = src.shape[0]
    c_tot = w.shape[1]
    nblk = c_tot // LANES

    def body(x_ref, w_ref, b_ref, o_ref):
        j = pl.program_id(0)
        t_idx = lax.broadcasted_iota(jnp.int32, (s, LANES), 0)
        w_rows = [w_ref[kk:kk + 1, :] for kk in range(CONV_K)]
        y = _silu(_conv_pre(x_ref[...], w_rows, b_ref[...], t_idx))
        if n_l2 > 0:
            yn = y * lax.rsqrt(jnp.sum(y * y, axis=1, keepdims=True) + EPS)
            y = jnp.where(j < n_l2, yn, y)
        o_ref[...] = y

    return pl.pallas_call(
        body, grid=(nblk,),
        in_specs=[pl.BlockSpec((s, LANES), lambda j: (0, col0 + j)), pl.BlockSpec((CONV_K, LANES), lambda j: (0, j)),
                  pl.BlockSpec((1, LANES), lambda j: (0, j))],
        out_specs=pl.BlockSpec((s, LANES), lambda j: (0, j)),
        out_shape=jax.ShapeDtypeStruct((s, c_tot), F32),
        compiler_params=_cparams(1), name=name,
    )(src, w, b)


def _conv_bwd(src, col0, w, b, n_l2, dout, *, name):
    s = src.shape[0]
    c_tot = w.shape[1]
    nblk = c_tot // LANES

    def body(x_ref, w_ref, b_ref, do_ref, dx_ref, dw_ref, db_ref):
        j = pl.program_id(0)
        t_idx = lax.broadcasted_iota(jnp.int32, (s, LANES), 0)
        xv = x_ref[...]
        w_rows = [w_ref[kk:kk + 1, :] for kk in range(CONV_K)]
        c = _conv_pre(xv, w_rows, b_ref[...], t_idx)
        dy = do_ref[...]
        if n_l2 > 0:
            y = _silu(c)
            r = lax.rsqrt(jnp.sum(y * y, axis=1, keepdims=True) + EPS)
            dyn = r * dy - y * (r * r * r) * jnp.sum(dy * y, axis=1, keepdims=True)
            dy = jnp.where(j < n_l2, dyn, dy)
        dc = dy * _silu_grad(c)
        dx = w_rows[CONV_K - 1] * dc
        rows = [None] * CONV_K
        rows[CONV_K - 1] = jnp.sum(dc * xv, axis=0, keepdims=True)
        for sh in range(1, CONV_K):
            dx = dx + w_rows[CONV_K - 1 - sh] * _shift_up(dc, sh, t_idx, s)
            rows[CONV_K - 1 - sh] = jnp.sum(dc * _shift_down(xv, sh, t_idx), axis=0, keepdims=True)
        dx_ref[...] = dx.astype(dx_ref.dtype)
        for kk in range(CONV_K):
            dw_ref[kk:kk + 1, :] = rows[kk]
        db_ref[...] = jnp.sum(dc, axis=0, keepdims=True)

    return pl.pallas_call(
        body, grid=(nblk,),
        in_specs=[pl.BlockSpec((s, LANES), lambda j: (0, col0 + j)), pl.BlockSpec((CONV_K, LANES), lambda j: (0, j)),
                  pl.BlockSpec((1, LANES), lambda j: (0, j)), pl.BlockSpec((s, LANES), lambda j: (0, j))],
        out_specs=[pl.BlockSpec((s, LANES), lambda j: (0, j)), pl.BlockSpec((CONV_K, LANES), lambda j: (0, j)),
                   pl.BlockSpec((1, LANES), lambda j: (0, j))],
        out_shape=[jax.ShapeDtypeStruct((s, c_tot), MXU_DTYPE), jax.ShapeDtypeStruct((CONV_K, c_tot), F32),
                   jax.ShapeDtypeStruct((1, c_tot), F32)],
        compiler_params=_cparams(1), name=name,
    )(src, w, b, dout)


def _chunk_masks(c):
    ii = lax.broadcasted_iota(jnp.int32, (c, c), 0)
    jj = lax.broadcasted_iota(jnp.int32, (c, c), 1)
    return ii, jj


def _row_to_col(row, eye):
    return jnp.sum(jnp.where(eye, row, 0.0), axis=1, keepdims=True)


def _each(f, *lists):
    return [f(*xs) for xs in zip(*lists)]


def _dn_chunk(q, k, v, a_row, b_row, alog, dtb, s0):
    c = q[0].shape[0]
    ii, jj = _chunk_masks(c)
    causal, strict, eye = ii >= jj, ii > jj, ii == jj
    g_row = _each(lambda al, a, dt: -jnp.exp(al) * _softplus(a + dt), alog, a_row, dtb)
    beta_col = _each(lambda b: _row_to_col(_sigmoid(b), eye), b_row)
    g_col = _each(lambda g: _row_to_col(g, eye), g_row)
    gc_col = _each(lambda g: jnp.sum(jnp.where(causal, g, 0.0), axis=1, keepdims=True), g_row)
    gc_row = _each(lambda g: jnp.sum(jnp.where(jj >= ii, g, 0.0), axis=0, keepdims=True), g_col)
    decay = _each(lambda gc, gr: jnp.exp(jnp.where(causal, gc - gr, NEG_BIG)), gc_col, gc_row)
    kb = _each(jnp.multiply, k, beta_col)
    vb = _each(jnp.multiply, v, beta_col)
    nmat = _each(lambda kb_, k_, dc: -jnp.where(strict, _hdot(kb_, k_, NT) * dc, 0.0), kb, k, decay)
    xinv = _each(lambda n: jnp.where(eye, 1.0, 0.0) + n, nmat)
    pw = nmat
    for _ in range(int(math.log2(c)) - 1):
        pw = _each(lambda p: _hdot(p, p), pw)
        xinv = _each(lambda x, p: x + _hdot(x, p), xinv, pw)
    egc = _each(jnp.exp, gc_col)
    u = _each(_hdot, xinv, vb)
    w = _each(lambda x, kb_, e: _hdot(x, kb_ * e), xinv, kb, egc)
    qs = _each(lambda q_: q_ * (q_.shape[1] ** -0.5), q)
    attn = _each(lambda q_, k_, dc: _hdot(q_, k_, NT) * dc, qs, k, decay)
    gl = _each(lambda g: jnp.sum(g, axis=1, keepdims=True), g_row)
    kd = _each(lambda k_, gl_, gc: k_ * jnp.exp(gl_ - gc), k, gl, gc_col)
    v_new = _each(lambda u_, w_, s: u_ - _hdot(w_, s), u, w, s0)
    o = _each(lambda q_, e, s, at, vn: _hdot(q_ * e, s) + _hdot(at, vn), qs, egc, s0, attn, v_new)
    s1 = _each(lambda s, gl_, kd_, vn: s * jnp.exp(gl_) + _hdot(kd_, vn, TN), s0, gl, kd, v_new)
    return o, s1


def _dn_specs(nh, nc, hb, rev):
    n_of = (lambda n: nc - 1 - n) if rev else (lambda n: n)
    ng = nh // hb
    qkv = [pl.BlockSpec((CHUNK, hb * DN_HEAD_DIM), (lambda h, n, o=o: (n_of(n), o * ng + h))) for o in range(3)]
    row = pl.BlockSpec((hb, None, 1, CHUNK), lambda h, n: (h, n_of(n), 0, 0))
    scal = pl.BlockSpec((hb, 1, 1), lambda h, n: (h, 0, 0))
    o_spec = pl.BlockSpec((CHUNK, hb * DN_HEAD_DIM), lambda h, n: (n_of(n), h))
    st = pl.BlockSpec((hb, None, DN_HEAD_DIM, DN_HEAD_DIM), lambda h, n: (h, n_of(n), 0, 0))
    return qkv, row, scal, o_spec, st


def _dn_fwd(qkv, a_rows, b_rows, alog, dtb, *, name):
    s = qkv.shape[0]
    nh, nc = a_rows.shape[0], a_rows.shape[1]
    hb = min(DN_HEADS_PER_STEP, nh)
    qkv_specs, row, scal, o_spec, st = _dn_specs(nh, nc, hb, False)
    hd = DN_HEAD_DIM

    def body(q_ref, k_ref, v_ref, a_ref, b_ref, al_ref, dt_ref, o_ref, st_ref, state):
        @pl.when(pl.program_id(1) == 0)
        def _():
            state[...] = jnp.zeros_like(state)

        cols = [slice(h * hd, (h + 1) * hd) for h in range(hb)]
        s0 = [state[h] for h in range(hb)]
        for h in range(hb):
            st_ref[h] = s0[h]
        o, s1 = _dn_chunk([q_ref[:, cl] for cl in cols], [k_ref[:, cl] for cl in cols], [v_ref[:, cl] for cl in cols],
                          [a_ref[h] for h in range(hb)], [b_ref[h] for h in range(hb)],
                          [al_ref[h] for h in range(hb)], [dt_ref[h] for h in range(hb)], s0)
        for h in range(hb):
            o_ref[:, cols[h]] = o[h]
            state[h] = s1[h]

    return pl.pallas_call(
        body, grid=(nh // hb, nc),
        in_specs=qkv_specs + [row, row, scal, scal],
        out_specs=[o_spec, st],
        out_shape=[jax.ShapeDtypeStruct((s, nh * hd), F32), jax.ShapeDtypeStruct((nh, nc, hd, hd), F32)],
        scratch_shapes=[pltpu.VMEM((hb, hd, hd), F32)],
        compiler_params=_cparams(2), name=name,
    )(qkv, qkv, qkv, a_rows, b_rows, alog, dtb)


def _dn_bwd(qkv, a_rows, b_rows, alog, dtb, states, do, *, name):
    s = qkv.shape[0]
    nh, nc = a_rows.shape[0], a_rows.shape[1]
    hb = min(DN_HEADS_PER_STEP, nh)
    qkv_specs, row, scal, o_spec, st = _dn_specs(nh, nc, hb, True)
    hd = DN_HEAD_DIM

    def body(q_ref, k_ref, v_ref, a_ref, b_ref, al_ref, dt_ref, st_ref, do_ref,
             dq_ref, dk_ref, dv_ref, da_ref, db_ref, dal_ref, ddt_ref, dstate):
        @pl.when(pl.program_id(1) == 0)
        def _():
            dstate[...] = jnp.zeros_like(dstate)
            dal_ref[...] = jnp.zeros_like(dal_ref)
            ddt_ref[...] = jnp.zeros_like(ddt_ref)

        cols = [slice(h * hd, (h + 1) * hd) for h in range(hb)]
        heads = range(hb)
        args = ([q_ref[:, cl] for cl in cols], [k_ref[:, cl] for cl in cols], [v_ref[:, cl] for cl in cols],
                [a_ref[h] for h in heads], [b_ref[h] for h in heads], [al_ref[h] for h in heads],
                [dt_ref[h] for h in heads], [st_ref[h] for h in heads])
        _, vjp = jax.vjp(_dn_chunk, *args)
        dq, dk, dv, da, db, dal, ddt, ds0 = vjp(([do_ref[:, cl] for cl in cols], [dstate[h] for h in heads]))
        for h in heads:
            dq_ref[:, cols[h]] = dq[h]
            dk_ref[:, cols[h]] = dk[h]
            dv_ref[:, cols[h]] = dv[h]
            da_ref[h] = da[h]
            db_ref[h] = db[h]
            dal_ref[h] += dal[h]
            ddt_ref[h] += ddt[h]
            dstate[h] = ds0[h]

    w = nh * hd
    outs = pl.pallas_call(
        body, grid=(nh // hb, nc),
        in_specs=qkv_specs + [row, row, scal, scal, st, o_spec],
        out_specs=[o_spec, o_spec, o_spec, row, row, scal, scal],
        out_shape=[jax.ShapeDtypeStruct((s, w), F32)] * 3
        + [jax.ShapeDtypeStruct(a_rows.shape, F32)] * 2 + [jax.ShapeDtypeStruct((nh, 1, 1), F32)] * 2,
        scratch_shapes=[pltpu.VMEM((hb, hd, hd), F32)],
        compiler_params=_cparams(2), name=name,
    )(qkv, qkv, qkv, a_rows, b_rows, alog, dtb, states, do)
    return outs


def _dn_post_fwd(o, src, gate_col0, nw, *, name, tm=256):
    s, w = o.shape
    nh = w // DN_HEAD_DIM

    def body(o_ref, g_ref, w_ref, y_ref):
        ov = o_ref[...]
        r = lax.rsqrt(jnp.mean(ov * ov, axis=-1, keepdims=True) + EPS)
        y_ref[...] = (ov * r * w_ref[...] * _silu(g_ref[...])).astype(y_ref.dtype)

    blk = pl.BlockSpec((tm, DN_HEAD_DIM), lambda i, h: (i, h))
    return pl.pallas_call(
        body, grid=(s // tm, nh),
        in_specs=[blk, pl.BlockSpec((tm, DN_HEAD_DIM), lambda i, h: (i, gate_col0 + h)),
                  pl.BlockSpec((1, DN_HEAD_DIM), lambda i, h: (0, 0))],
        out_specs=blk, out_shape=jax.ShapeDtypeStruct((s, w), MXU_DTYPE),
        compiler_params=_cparams(2), name=name,
    )(o, src, nw.reshape(1, DN_HEAD_DIM))


def _dn_post_bwd(o, src, gate_col0, nw, dy, *, name, tm=256):
    s, w = o.shape
    nh = w // DN_HEAD_DIM

    def body(o_ref, g_ref, w_ref, dy_ref, do_ref, dg_ref, dw_ref):
        ov = o_ref[...]
        gv = g_ref[...]
        dyv = dy_ref[...]
        r = lax.rsqrt(jnp.mean(ov * ov, axis=-1, keepdims=True) + EPS)
        oh = ov * r
        dn = dyv * _silu(gv)
        dg_ref[...] = (dyv * (oh * w_ref[...]) * _silu_grad(gv)).astype(dg_ref.dtype)
        don = dn * w_ref[...]
        do_ref[...] = r * (don - oh * jnp.mean(don * oh, axis=-1, keepdims=True))

        @pl.when((pl.program_id(0) == 0) & (pl.program_id(1) == 0))
        def _():
            dw_ref[...] = jnp.zeros_like(dw_ref)

        dw_ref[...] += jnp.sum(dn * oh, axis=0, keepdims=True)

    blk = pl.BlockSpec((tm, DN_HEAD_DIM), lambda i, h: (i, h))
    wspec = pl.BlockSpec((1, DN_HEAD_DIM), lambda i, h: (0, 0))
    do, dg, dw = pl.pallas_call(
        body, grid=(s // tm, nh),
        in_specs=[blk, pl.BlockSpec((tm, DN_HEAD_DIM), lambda i, h: (i, gate_col0 + h)), wspec, blk],
        out_specs=[blk, blk, wspec],
        out_shape=[jax.ShapeDtypeStruct((s, w), F32), jax.ShapeDtypeStruct((s, w), MXU_DTYPE),
                   jax.ShapeDtypeStruct((1, DN_HEAD_DIM), F32)],
        compiler_params=_cparams(2), name=name,
    )(o, src, nw.reshape(1, DN_HEAD_DIM), dy)
    return do, dg, dw.reshape(DN_HEAD_DIM)


def _sb_consts():
    r2 = lax.broadcasted_iota(jnp.int32, (2 * SB_BLOCK, SB_BLOCK), 0)
    c2 = lax.broadcasted_iota(jnp.int32, (2 * SB_BLOCK, SB_BLOCK), 1)
    r = lax.broadcasted_iota(jnp.int32, (SB_BLOCK, SB_BLOCK), 0)
    c = lax.broadcasted_iota(jnp.int32, (SB_BLOCK, SB_BLOCK), 1)
    lm0 = c < SB_HEAD_DIM
    m_gt = jnp.where(r > c, 1.0, 0.0).astype(BF16)
    m_lt = jnp.where(r < c, 1.0, 0.0).astype(BF16)
    return r2, c2, lm0, m_gt, m_lt


def _sb_stack(x, lm0):
    return jnp.concatenate([jnp.where(lm0, x, 0.0), jnp.where(lm0, 0.0, x)], axis=0)


def _sb_unstack(x2, lm0):
    return jnp.where(lm0, x2[:SB_BLOCK], x2[SB_BLOCK:])


def _sb_fwd(src, col0, width, *, name):
    s = src.shape[0]
    nq = s // SB_BLOCK
    npair = width // LANES
    scale = SB_HEAD_DIM ** -0.5
    nu = math.gcd(SB_UNROLL, nq)

    def body(q_ref, k_ref, v_ref, o_ref, r_ref):
        i = pl.program_id(1)
        r2, c2, lm0, m_gt, _ = _sb_consts()
        t_glob = i * SB_BLOCK + (r2 & (SB_BLOCK - 1))
        q2 = (_sb_stack(q_ref[...], lm0) * scale).astype(MXU_DTYPE)

        def group(base, carry, masked):
            o2, rsum = carry
            js = [base + nu - 1 - u for u in range(nu)]
            offs = [pl.multiple_of(j * SB_BLOCK, SB_BLOCK) for j in js]
            zs = [_dot(q2, k_ref[pl.ds(off, SB_BLOCK), :].astype(MXU_DTYPE), NT) for off in offs]
            ts = [jnp.log(1.0 + jnp.exp(-jnp.abs(z))) for z in zs]
            lks = [-(jnp.maximum(z, 0.0) + t) for z, t in zip(zs, ts)]
            if masked:
                masks = [(j * SB_BLOCK + c2) < t_glob for j in js]
                lks = [jnp.where(mk, lk, 0.0) for mk, lk in zip(masks, lks)]
            sufs = [_split_dot(lk, m_gt, SB_SPLIT) for lk in lks]
            rs = [rsum]
            for lk in lks:
                rs.append(rs[-1] + jnp.sum(lk, axis=1, keepdims=True))
            wgts = [jnp.exp((jnp.minimum(z, 0.0) - t) + r_ + sf) for z, t, r_, sf in zip(zs, ts, rs, sufs)]
            if masked:
                wgts = [jnp.where(mk, wg, 0.0) for mk, wg in zip(masks, wgts)]
            for off, wg in zip(offs, wgts):
                o2 = o2 + _dot(wg.astype(MXU_DTYPE), v_ref[pl.ds(off, SB_BLOCK), :].astype(MXU_DTYPE), NN)
            return o2, rs[-1]

        top0 = (i // nu) * nu
        carry = group(top0, (jnp.zeros((2 * SB_BLOCK, LANES), F32), jnp.zeros((2 * SB_BLOCK, 1), F32)), True)
        o2, rsum = lax.fori_loop(1, i // nu + 1, lambda g, cr: group(top0 - nu * g, cr, False), carry)
        o_ref[...] = _sb_unstack(o2, lm0)
        r_ref[...] = _sb_unstack(jnp.broadcast_to(rsum, (2 * SB_BLOCK, LANES)), lm0)

    blk = pl.BlockSpec((SB_BLOCK, LANES), lambda p, i: (i, p))
    return pl.pallas_call(
        body, grid=(npair, nq),
        in_specs=[pl.BlockSpec((SB_BLOCK, LANES), lambda p, i: (i, col0 + p)),
                  pl.BlockSpec((s, LANES), lambda p, i: (0, col0 + npair + p)),
                  pl.BlockSpec((s, LANES), lambda p, i: (0, col0 + 2 * npair + p))],
        out_specs=[blk, blk],
        out_shape=[jax.ShapeDtypeStruct((s, width), F32), jax.ShapeDtypeStruct((s, width), F32)],
        compiler_params=_cparams(2), name=name,
    )(src, src, src)


def _sb_bwd(src, col0, width, rtot, do, *, name):
    s = src.shape[0]
    nq = s // SB_BLOCK
    npair = width // LANES
    scale = SB_HEAD_DIM ** -0.5
    nu = math.gcd(SB_UNROLL, nq)

    def body(q_ref, k_ref, v_ref, r_ref, do_ref, dq_ref, dk_ref, dv_ref):
        i = pl.program_id(1)

        @pl.when(i == 0)
        def _():
            dk_ref[...] = jnp.zeros_like(dk_ref)
            dv_ref[...] = jnp.zeros_like(dv_ref)

        r2, c2, lm0, m_gt, m_lt = _sb_consts()
        t_glob = i * SB_BLOCK + (r2 & (SB_BLOCK - 1))
        q2 = (_sb_stack(q_ref[...], lm0) * scale).astype(MXU_DTYPE)
        do2 = _sb_stack(do_ref[...], lm0).astype(MXU_DTYPE)
        rv = r_ref[...]
        rt = jnp.concatenate([jnp.max(jnp.where(lm0, rv, NEG_BIG), axis=1, keepdims=True),
                              jnp.max(jnp.where(lm0, NEG_BIG, rv), axis=1, keepdims=True)], axis=0)

        def group(g, carry, masked):
            dq2, psum, csum = carry
            js = [nu * g + u for u in range(nu)]
            offs = [pl.multiple_of(j * SB_BLOCK, SB_BLOCK) for j in js]
            kbs = [k_ref[pl.ds(off, SB_BLOCK), :].astype(MXU_DTYPE) for off in offs]
            zs = [_dot(q2, kb, NT) for kb in kbs]
            dws = [_dot(do2, v_ref[pl.ds(off, SB_BLOCK), :].astype(MXU_DTYPE), NT) for off in offs]
            ts = [jnp.log(1.0 + jnp.exp(-jnp.abs(z))) for z in zs]
            lks = [-(jnp.maximum(z, 0.0) + t) for z, t in zip(zs, ts)]
            if masked:
                masks = [(j * SB_BLOCK + c2) < t_glob for j in js]
                lks = [jnp.where(mk, lk, 0.0) for mk, lk in zip(masks, lks)]
            sufs = [_split_dot(lk, m_gt, SB_SPLIT) for lk in lks]
            lsums = [jnp.sum(lk, axis=1, keepdims=True) for lk in lks]
            logsigs = [jnp.minimum(z, 0.0) - t for z, t in zip(zs, ts)]
            wgts = []
            for lsg, lsum, sf in zip(logsigs, lsums, sufs):
                psum = psum + lsum
                wgts.append(jnp.exp(lsg + (rt - psum) + sf))
            if masked:
                wgts = [jnp.where(mk, wg, 0.0) for mk, wg in zip(masks, wgts)]
            dlogas = [wg * dw for wg, dw in zip(wgts, dws)]
            pres = [_split_dot(dl, m_lt, SB_SPLIT) for dl in dlogas]
            dlks = []
            for dl, pre in zip(dlogas, pres):
                dlks.append(csum + pre)
                csum = csum + jnp.sum(dl, axis=1, keepdims=True)
            if masked:
                dlks = [jnp.where(mk, dlk, 0.0) for mk, dlk in zip(masks, dlks)]
            sigs = [jnp.exp(lsg) for lsg in logsigs]
            dzbs = [(dl * (1.0 - sg) - dlk * sg).astype(MXU_DTYPE) for dl, sg, dlk in zip(dlogas, sigs, dlks)]
            for off, dzb, wg, kb in zip(offs, dzbs, wgts, kbs):
                dk_ref[pl.ds(off, SB_BLOCK), :] += _dot(dzb, q2, TN)
                dv_ref[pl.ds(off, SB_BLOCK), :] += _dot(wg.astype(MXU_DTYPE), do2, TN)
                dq2 = dq2 + _dot(dzb, kb, NN)
            return dq2, psum, csum

        zero_col = jnp.zeros((2 * SB_BLOCK, 1), F32)
        carry = lax.fori_loop(0, i // nu, lambda g, cr: group(g, cr, False),
                              (jnp.zeros((2 * SB_BLOCK, LANES), F32), zero_col, zero_col))
        dq2, _, _ = group(i // nu, carry, True)
        dq_ref[...] = _sb_unstack(dq2, lm0) * scale

    blk = pl.BlockSpec((SB_BLOCK, LANES), lambda p, i: (i, p))
    full = pl.BlockSpec((s, LANES), lambda p, i: (0, p))
    return pl.pallas_call(
        body, grid=(npair, nq),
        in_specs=[pl.BlockSpec((SB_BLOCK, LANES), lambda p, i: (i, col0 + p)),
                  pl.BlockSpec((s, LANES), lambda p, i: (0, col0 + npair + p)),
                  pl.BlockSpec((s, LANES), lambda p, i: (0, col0 + 2 * npair + p)),
                  blk, blk],
        out_specs=[blk, full, full],
        out_shape=[jax.ShapeDtypeStruct((s, width), F32)] * 3,
        compiler_params=_cparams(2), name=name,
    )(src, src, src, rtot, do)


def _ssd_group(xs, dt_rows, alogs, dtbs, bm, cm, h0s):
    c = bm.shape[0]
    ii, jj = _chunk_masks(c)
    causal, eye = ii >= jj, ii == jj
    scores = _hdot(cm, bm, NT)
    dt_r = _each(lambda dt, b: _softplus(dt + b), dt_rows, dtbs)
    a_r = _each(lambda al, dt: -jnp.exp(al) * dt, alogs, dt_r)
    dt_col = _each(lambda dt: _row_to_col(dt, eye), dt_r)
    a_col = _each(lambda a: _row_to_col(a, eye), a_r)
    ac_col = _each(lambda a: jnp.sum(jnp.where(causal, a, 0.0), axis=1, keepdims=True), a_r)
    ac_row = _each(lambda a: jnp.sum(jnp.where(jj >= ii, a, 0.0), axis=0, keepdims=True), a_col)
    lmat = _each(lambda c_, r_: jnp.exp(jnp.where(causal, c_ - r_, NEG_BIG)), ac_col, ac_row)
    xdt = _each(jnp.multiply, xs, dt_col)
    al = _each(lambda a: jnp.sum(a, axis=1, keepdims=True), a_r)
    ys = _each(lambda lm, xd, h0, ac: _hdot(scores * lm, xd) + _hdot(cm, h0, NT) * jnp.exp(ac), lmat, xdt, h0s, ac_col)
    h1s = _each(lambda h0, al_, xd, ac: h0 * jnp.exp(al_) + _hdot(xd * jnp.exp(al_ - ac), bm, TN), h0s, al, xdt, ac_col)
    return ys, h1s


def _ssd_specs(ng, nc, r, rev):
    n_of = (lambda n: nc - 1 - n) if rev else (lambda n: n)
    gw = r * SSM_HEAD_DIM
    x_spec = pl.BlockSpec((CHUNK, gw), lambda g, n: (n_of(n), g))
    b_spec = pl.BlockSpec((CHUNK, SSM_STATE), lambda g, n: (n_of(n), (ng * gw) // SSM_STATE + g))
    c_spec = pl.BlockSpec((CHUNK, SSM_STATE), lambda g, n: (n_of(n), (ng * gw) // SSM_STATE + ng + g))
    dt_spec = pl.BlockSpec((None, None, r, CHUNK), lambda g, n: (g, n_of(n), 0, 0))
    sc_spec = pl.BlockSpec((None, r, 1), lambda g, n: (g, 0, 0))
    st_spec = pl.BlockSpec((None, None, r, SSM_HEAD_DIM, SSM_STATE), lambda g, n: (g, n_of(n), 0, 0, 0))
    y_spec = pl.BlockSpec((CHUNK, gw), lambda g, n: (n_of(n), g))
    bc_out = pl.BlockSpec((CHUNK, SSM_STATE), lambda g, n: (n_of(n), g))
    return x_spec, b_spec, c_spec, dt_spec, sc_spec, st_spec, y_spec, bc_out


def _ssd_fwd(xbc, dt_rows, alog, dtb, *, name):
    s = xbc.shape[0]
    ng, nc, r = dt_rows.shape[0], dt_rows.shape[1], dt_rows.shape[2]
    w = ng * r * SSM_HEAD_DIM
    x_spec, b_spec, c_spec, dt_spec, sc_spec, st_spec, y_spec, _ = _ssd_specs(ng, nc, r, False)
    p = SSM_HEAD_DIM

    def body(x_ref, b_ref, c_ref, dt_ref, al_ref, db_ref, y_ref, st_ref, state):
        @pl.when(pl.program_id(1) == 0)
        def _():
            state[...] = jnp.zeros_like(state)

        st_ref[...] = state[...]
        xs = [x_ref[:, h * p:(h + 1) * p] for h in range(r)]
        dts = [dt_ref[h:h + 1, :] for h in range(r)]
        als = [al_ref[h:h + 1, :] for h in range(r)]
        dbs = [db_ref[h:h + 1, :] for h in range(r)]
        h0s = [state[h] for h in range(r)]
        ys, h1s = _ssd_group(xs, dts, als, dbs, b_ref[...], c_ref[...], h0s)
        for h in range(r):
            y_ref[:, h * p:(h + 1) * p] = ys[h]
            state[h] = h1s[h]

    return pl.pallas_call(
        body, grid=(ng, nc),
        in_specs=[x_spec, b_spec, c_spec, dt_spec, sc_spec, sc_spec],
        out_specs=[y_spec, st_spec],
        out_shape=[jax.ShapeDtypeStruct((s, w), F32), jax.ShapeDtypeStruct((ng, nc, r, p, SSM_STATE), F32)],
        scratch_shapes=[pltpu.VMEM((r, p, SSM_STATE), F32)],
        compiler_params=_cparams(2), name=name,
    )(xbc, xbc, xbc, dt_rows, alog, dtb)


def _ssd_bwd(xbc, dt_rows, alog, dtb, states, dy, *, name):
    s = xbc.shape[0]
    ng, nc, r = dt_rows.shape[0], dt_rows.shape[1], dt_rows.shape[2]
    w = ng * r * SSM_HEAD_DIM
    x_spec, b_spec, c_spec, dt_spec, sc_spec, st_spec, y_spec, bc_out = _ssd_specs(ng, nc, r, True)
    p = SSM_HEAD_DIM

    def body(x_ref, b_ref, c_ref, dt_ref, al_ref, db_ref, st_ref, dy_ref,
             dx_ref, dbm_ref, dcm_ref, ddt_ref, dal_ref, ddb_ref, dstate):
        @pl.when(pl.program_id(1) == 0)
        def _():
            dstate[...] = jnp.zeros_like(dstate)
            dal_ref[...] = jnp.zeros_like(dal_ref)
            ddb_ref[...] = jnp.zeros_like(ddb_ref)

        xs = [x_ref[:, h * p:(h + 1) * p] for h in range(r)]
        dts = [dt_ref[h:h + 1, :] for h in range(r)]
        als = [al_ref[h:h + 1, :] for h in range(r)]
        dbs = [db_ref[h:h + 1, :] for h in range(r)]
        h0s = [st_ref[h] for h in range(r)]
        _, vjp = jax.vjp(_ssd_group, xs, dts, als, dbs, b_ref[...], c_ref[...], h0s)
        dys = [dy_ref[:, h * p:(h + 1) * p] for h in range(r)]
        dh1s = [dstate[h] for h in range(r)]
        dxs, ddts, dals, ddbs, dbm, dcm, dh0s = vjp((dys, dh1s))
        dbm_ref[...] = dbm
        dcm_ref[...] = dcm
        for h in range(r):
            dx_ref[:, h * p:(h + 1) * p] = dxs[h]
            ddt_ref[h:h + 1, :] = ddts[h]
            dal_ref[h:h + 1, :] += dals[h]
            ddb_ref[h:h + 1, :] += ddbs[h]
            dstate[h] = dh0s[h]

    gn = ng * SSM_STATE
    return pl.pallas_call(
        body, grid=(ng, nc),
        in_specs=[x_spec, b_spec, c_spec, dt_spec, sc_spec, sc_spec, st_spec, y_spec],
        out_specs=[y_spec, bc_out, bc_out, dt_spec, sc_spec, sc_spec],
        out_shape=[jax.ShapeDtypeStruct((s, w), F32), jax.ShapeDtypeStruct((s, gn), F32), jax.ShapeDtypeStruct((s, gn), F32),
                   jax.ShapeDtypeStruct(dt_rows.shape, F32), jax.ShapeDtypeStruct((ng, r, 1), F32),
                   jax.ShapeDtypeStruct((ng, r, 1), F32)],
        scratch_shapes=[pltpu.VMEM((r, p, SSM_STATE), F32)],
        compiler_params=_cparams(2), name=name,
    )(xbc, xbc, xbc, dt_rows, alog, dtb, states, dy)


def _ssm_post_fwd(y, xbc, src, z_col0, dexp, nw, *, name, tm=256):
    s, w = y.shape
    gw = w // SSM_GROUPS
    zc = z_col0 * LANES // gw

    def body(y_ref, x_ref, z_ref, d_ref, w_ref, o_ref):
        yy = (y_ref[...] + x_ref[...] * d_ref[...]) * _silu(z_ref[...])
        r = lax.rsqrt(jnp.mean(yy * yy, axis=-1, keepdims=True) + EPS)
        o_ref[...] = (yy * r * w_ref[...]).astype(o_ref.dtype)

    blk = pl.BlockSpec((tm, gw), lambda g, i: (i, g))
    vec = pl.BlockSpec((1, gw), lambda g, i: (0, g))
    return pl.pallas_call(
        body, grid=(SSM_GROUPS, s // tm),
        in_specs=[blk, blk, pl.BlockSpec((tm, gw), lambda g, i: (i, zc + g)), vec, vec],
        out_specs=blk, out_shape=jax.ShapeDtypeStruct((s, w), MXU_DTYPE),
        compiler_params=_cparams(2), name=name,
    )(y, xbc, src, dexp.reshape(1, w), nw.reshape(1, w))


def _ssm_post_bwd(y, xbc, src, z_col0, dexp, nw, dout, *, name, tm=256):
    s, w = y.shape
    gw = w // SSM_GROUPS
    zc = z_col0 * LANES // gw

    def body(y_ref, x_ref, z_ref, d_ref, w_ref, do_ref, dy_ref, dx_ref, dz_ref, dd_ref, dw_ref):
        xv, zv, dv = x_ref[...], z_ref[...], d_ref[...]
        pre = y_ref[...] + xv * dv
        sz = _silu(zv)
        yy = pre * sz
        r = lax.rsqrt(jnp.mean(yy * yy, axis=-1, keepdims=True) + EPS)
        yh = yy * r
        dov = do_ref[...]
        dyn = dov * w_ref[...]
        dyy = r * (dyn - yh * jnp.mean(dyn * yh, axis=-1, keepdims=True))
        dpre = dyy * sz
        dy_ref[...] = dpre
        dx_ref[...] = dpre * dv
        dz_ref[...] = (dyy * pre * _silu_grad(zv)).astype(dz_ref.dtype)

        @pl.when(pl.program_id(1) == 0)
        def _():
            dd_ref[...] = jnp.zeros_like(dd_ref)
            dw_ref[...] = jnp.zeros_like(dw_ref)

        dd_ref[...] += jnp.sum(dpre * xv, axis=0, keepdims=True)
        dw_ref[...] += jnp.sum(dov * yh, axis=0, keepdims=True)

    blk = pl.BlockSpec((tm, gw), lambda g, i: (i, g))
    vec = pl.BlockSpec((1, gw), lambda g, i: (0, g))
    dy, dx, dz, dd, dw = pl.pallas_call(
        body, grid=(SSM_GROUPS, s // tm),
        in_specs=[blk, blk, pl.BlockSpec((tm, gw), lambda g, i: (i, zc + g)), vec, vec, blk],
        out_specs=[blk, blk, blk, vec, vec],
        out_shape=[jax.ShapeDtypeStruct((s, w), F32), jax.ShapeDtypeStruct((s, w), F32), jax.ShapeDtypeStruct((s, w), MXU_DTYPE),
                   jax.ShapeDtypeStruct((1, w), F32), jax.ShapeDtypeStruct((1, w), F32)],
        compiler_params=_cparams(2), name=name,
    )(y, xbc, src, dexp.reshape(1, w), nw.reshape(1, w), dout)
    return dy, dx, dz, dd.reshape(w), dw.reshape(w)


def _merge_fwd(proj3, src, gate_col0, d, *, name, tm=256):
    s = proj3.shape[0]
    nb = proj3.shape[1] // d
    gc = gate_col0 * LANES // d

    def body(*refs):
        p_refs, g_refs, o_ref = refs[:nb], refs[nb:2 * nb], refs[-1]
        acc = None
        for p_ref, g_ref in zip(p_refs, g_refs):
            term = _sigmoid(g_ref[...]) * p_ref[...]
            acc = term if acc is None else acc + term
        o_ref[...] = acc.astype(o_ref.dtype)

    p_specs = [pl.BlockSpec((tm, d), lambda i, b=b: (i, b)) for b in range(nb)]
    g_specs = [pl.BlockSpec((tm, d), lambda i, b=b: (i, gc + b)) for b in range(nb)]
    return pl.pallas_call(
        body, grid=(s // tm,), in_specs=p_specs + g_specs,
        out_specs=pl.BlockSpec((tm, d), lambda i: (i, 0)), out_shape=jax.ShapeDtypeStruct((s, d), MXU_DTYPE),
        compiler_params=_cparams(1), name=name,
    )(*([proj3] * nb), *([src] * nb))


def _merge_bwd(proj3, src, gate_col0, d, dmerged, *, name, tm=256):
    s = proj3.shape[0]
    nb = proj3.shape[1] // d
    gc = gate_col0 * LANES // d

    def body(p_ref, g_ref, dm_ref, dp_ref, dg_ref):
        sg = _sigmoid(g_ref[...])
        dm = dm_ref[...]
        dp_ref[...] = (dm * sg).astype(dp_ref.dtype)
        dg_ref[...] = (dm * p_ref[...] * sg * (1.0 - sg)).astype(dg_ref.dtype)

    blk = pl.BlockSpec((tm, d), lambda i, b: (i, b))
    return pl.pallas_call(
        body, grid=(s // tm, nb),
        in_specs=[blk, pl.BlockSpec((tm, d), lambda i, b: (i, gc + b)), pl.BlockSpec((tm, d), lambda i, b: (i, 0))],
        out_specs=[blk, blk],
        out_shape=[jax.ShapeDtypeStruct(proj3.shape, MXU_DTYPE), jax.ShapeDtypeStruct(proj3.shape, MXU_DTYPE)],
        compiler_params=_cparams(2), name=name,
    )(proj3, src, dmerged)


ANY = pl.BlockSpec(memory_space=pl.ANY)
MESH = pl.DeviceIdType.MESH


def _all_gather(shards, *, name):
    nt = len(shards)

    def body(*refs):
        x_refs, out_refs = refs[:nt], refs[nt:2 * nt]
        send_sems, recv_sems, local_sems = refs[2 * nt:]
        x, y, c = lax.axis_index("x"), lax.axis_index("y"), lax.axis_index("c")
        me, sibling = (x, y, c), (x, y, 1 - c)
        chips = [(1 - x, y), (x, 1 - y), (1 - x, 1 - y)]

        def slot(t, px, py, pc):
            return out_refs[t].at[4 * px + 2 * py + pc]

        def copy(t, k, block, to, from_input=False):
            return pltpu.make_async_remote_copy(
                src_ref=x_refs[t] if from_input else slot(t, *block), dst_ref=slot(t, *block),
                send_sem=send_sems.at[7 * t + k], recv_sem=recv_sems.at[7 * t + k], device_id=to, device_id_type=MESH)

        mine = [pltpu.make_async_copy(x_refs[t], slot(t, *me), local_sems.at[t]) for t in range(nt)]
        for cp in mine:
            cp.start()
        first = [copy(t, 0, me, sibling, True) for t in range(nt)]
        first += [copy(t, 1 + j, me, (*chip, c), True) for j, chip in enumerate(chips) for t in range(nt)]
        for cp in first:
            cp.start()
        passed = []
        for j, chip in enumerate(chips):
            for t in range(nt):
                copy(t, 1 + j, (*chip, c), me).wait_recv()
                fwd = copy(t, 4 + j, (*chip, c), sibling)
                fwd.start()
                passed.append(fwd)
        for t in range(nt):
            copy(t, 0, sibling, me).wait_recv()
            for j, chip in enumerate(chips):
                copy(t, 4 + j, (*chip, 1 - c), me).wait_recv()
        for cp in first + passed:
            cp.wait_send()
        for cp in mine:
            cp.wait()

    return pl.pallas_call(
        body, out_shape=[jax.ShapeDtypeStruct((N_DEV,) + a.shape, a.dtype) for a in shards],
        in_specs=[ANY] * nt, out_specs=[ANY] * nt,
        scratch_shapes=[pltpu.SemaphoreType.DMA((7 * nt,)), pltpu.SemaphoreType.DMA((7 * nt,)),
                        pltpu.SemaphoreType.DMA((nt,))],
        name=name,
    )(*shards)


def _grad_exchange(bigs, small, *, name):
    nl = len(bigs[0])
    flat = [a for per_layer in bigs for a in per_layer]
    nslot = len(flat)

    def body(*refs):
        in_refs, small_ref = refs[:nslot], refs[nslot]
        out_refs, smallr_ref = refs[nslot + 1:nslot + 1 + len(bigs)], refs[nslot + 1 + len(bigs)]
        send_sems, recv_sems, local_sems = refs[nslot + 2 + len(bigs):]
        x, y, c = lax.axis_index("x"), lax.axis_index("y"), lax.axis_index("c")
        me = 4 * x + 2 * y + c
        local = [pltpu.make_async_copy(in_refs[i].at[me], out_refs[i // nl].at[me, i % nl], local_sems.at[i])
                 for i in range(nslot)]
        local.append(pltpu.make_async_copy(small_ref, smallr_ref.at[me], local_sems.at[nslot]))
        for cp in local:
            cp.start()
        copies = []
        for k in range(1, N_DEV):
            px = x ^ ((k >> 2) & 1)
            py = y ^ ((k >> 1) & 1)
            pc = c ^ (k & 1)
            peer = 4 * px + 2 * py + pc
            for i in range(nslot + 1):
                sem = 7 * i + (k - 1)
                src = in_refs[i].at[peer] if i < nslot else small_ref
                dst = out_refs[i // nl].at[me, i % nl] if i < nslot else smallr_ref.at[me]
                copies.append(pltpu.make_async_remote_copy(
                    src_ref=src, dst_ref=dst, send_sem=send_sems.at[sem], recv_sem=recv_sems.at[sem],
                    device_id=(px, py, pc), device_id_type=MESH))
        for cp in copies:
            cp.start()
        for cp in copies:
            cp.wait_recv()
        for cp in copies:
            cp.wait_send()
        for cp in local:
            cp.wait()

    out_shape = [jax.ShapeDtypeStruct((N_DEV, nl) + per_layer[0].shape[1:], per_layer[0].dtype) for per_layer in bigs]
    out_shape.append(jax.ShapeDtypeStruct((N_DEV,) + small.shape, small.dtype))
    nsem = 7 * (nslot + 1)
    outs = pl.pallas_call(
        body, out_shape=out_shape,
        in_specs=[ANY] * (nslot + 1), out_specs=[ANY] * (len(bigs) + 1),
        scratch_shapes=[pltpu.SemaphoreType.DMA((nsem,)), pltpu.SemaphoreType.DMA((nsem,)),
                        pltpu.SemaphoreType.DMA((nslot + 1,))],
        name=name,
    )(*flat, small)
    return outs[:-1], outs[-1]


def _adam_math(w, g, m, v):
    m1 = ADAM_B1 * m + (1.0 - ADAM_B1) * g
    v1 = ADAM_B2 * v + (1.0 - ADAM_B2) * (g * g)
    m_hat = m1 / (1.0 - ADAM_B1 ** ADAM_STEP)
    v_hat = v1 / (1.0 - ADAM_B2 ** ADAM_STEP)
    delta = -ADAM_LR * (m_hat / (jnp.sqrt(v_hat) + ADAM_EPS) + ADAM_WD * w)
    return delta, m1, v1


def _sum_adamw(parts, w, m, v, *, name):
    shape = w.shape
    r, c = shape[-2], shape[-1]
    a = math.prod(shape[:-2])
    tr = _pick(r, (256,) if c <= 1024 else (128,))
    w3, m3, v3 = (t.reshape(a, r, c) for t in (w, m, v))

    def body(p_ref, w_ref, m_ref, v_ref, g_ref, d_ref, m1_ref, v1_ref):
        g = p_ref[0].astype(F32)
        for src in range(1, N_DEV):
            g = g + p_ref[src].astype(F32)
        delta, m1, v1 = _adam_math(w_ref[...], g, m_ref[...], v_ref[...])
        g_ref[...] = g
        d_ref[...] = delta
        m1_ref[...] = m1
        v1_ref[...] = v1

    blk = pl.BlockSpec((None, tr, c), lambda i, j: (i, j, 0))
    outs = pl.pallas_call(
        body, grid=(a, r // tr),
        in_specs=[pl.BlockSpec((N_DEV, None, tr, c), lambda i, j: (0, i, j, 0)), blk, blk, blk],
        out_specs=[blk] * 4, out_shape=[jax.ShapeDtypeStruct((a, r, c), F32)] * 4,
        compiler_params=_cparams(2), name=name,
    )(parts.reshape(N_DEV, a, r, c), w3, m3, v3)
    return [o.reshape(shape) for o in outs]


def _sum_parts(parts, *, name):
    rows = parts.shape[1]

    def body(p_ref, o_ref):
        g = p_ref[0]
        for src in range(1, N_DEV):
            g = g + p_ref[src]
        o_ref[...] = g

    return pl.pallas_call(
        body, grid=(1,), in_specs=[pl.BlockSpec((N_DEV, rows, LANES), lambda i: (0, 0, 0))],
        out_specs=pl.BlockSpec((rows, LANES), lambda i: (0, 0)), out_shape=jax.ShapeDtypeStruct((rows, LANES), F32),
        compiler_params=_cparams(1), name=name,
    )(parts)


def _adamw(w, g, m, v, *, name):
    rows = w.shape[0]

    def body(w_ref, g_ref, m_ref, v_ref, d_ref, m1_ref, v1_ref):
        delta, m1, v1 = _adam_math(w_ref[...], g_ref[...], m_ref[...], v_ref[...])
        d_ref[...] = delta
        m1_ref[...] = m1
        v1_ref[...] = v1

    blk = pl.BlockSpec((rows, LANES), lambda i: (0, 0))
    return pl.pallas_call(
        body, grid=(1,), in_specs=[blk] * 4, out_specs=[blk] * 3,
        out_shape=[jax.ShapeDtypeStruct((rows, LANES), F32)] * 3,
        compiler_params=_cparams(1), name=name,
    )(w, g, m, v)


def _pack(arrs, dtype, row_mult=16):
    flat = jnp.concatenate([a.reshape(-1).astype(dtype) for a in arrs])
    n = flat.shape[0]
    rows = -(-n // (LANES * row_mult)) * row_mult
    flat = jnp.pad(flat, (0, rows * LANES - n))
    return flat.reshape(rows, LANES)


def _unpack(packed, shapes):
    flat = packed.reshape(-1)
    out, off = [], 0
    for shp in shapes:
        n = math.prod(shp)
        out.append(flat[off:off + n].reshape(shp))
        off += n
    return out


class _Layout:
    def __init__(self, d):
        self.d = d
        w = d
        self.dn_heads = w // DN_HEAD_DIM
        self.ssm_heads = w // SSM_HEAD_DIM
        gn = SSM_GROUPS * SSM_STATE
        self.sizes = (3 * w, w, self.dn_heads, self.dn_heads, 3 * w, w, w + 2 * gn, self.ssm_heads, 3 * d)
        offs, o = [], 0
        for sz in self.sizes:
            offs.append(o)
            o += sz
        self.offs = offs
        self.in_dim = o
        self.big = (0, 1, 4, 5, 6, 8)
        self.small = (2, 3, 7)
        cols, o = {}, 0
        for idx in self.big:
            cols[idx] = o
            o += self.sizes[idx]
        self.small_col = o
        self.cols = cols
        self.padded = o + LANES
        self.n_small = sum(self.sizes[i] for i in self.small)

    def reorder_w(self, w_in):
        parts = [w_in[:, self.offs[i]:self.offs[i] + self.sizes[i]] for i in self.big + self.small]
        parts.append(jnp.zeros((w_in.shape[0], LANES - self.n_small), w_in.dtype))
        return jnp.concatenate(parts, axis=1)

    def from_shards(self, parts):
        cs = self.in_dim // N_DEV
        pieces = []
        for i in self.big + self.small:
            a, b = self.offs[i], self.offs[i] + self.sizes[i]
            while a < b:
                j = a // cs
                hi = min(b, (j + 1) * cs)
                pieces.append(parts[j][:, a - j * cs:hi - j * cs])
                a = hi
        pieces.append(jnp.zeros((parts.shape[1], LANES - self.n_small), parts.dtype))
        return jnp.concatenate(pieces, axis=1)

    def to_shards(self, wp):
        cs = self.in_dim // N_DEV
        pcol = dict(self.cols)
        o = self.small_col
        for i in self.small:
            pcol[i] = o
            o += self.sizes[i]
        shards = []
        for j in range(N_DEV):
            a, b = j * cs, (j + 1) * cs
            pieces = []
            for i in range(len(self.sizes)):
                lo, hi = max(a, self.offs[i]), min(b, self.offs[i] + self.sizes[i])
                if lo < hi:
                    pieces.append(wp[:, pcol[i] + lo - self.offs[i]:pcol[i] + hi - self.offs[i]])
            shards.append(jnp.concatenate(pieces, axis=1))
        return jnp.stack(shards)

    def restore_w(self, wp):
        pieces = {}
        for idx in self.big:
            pieces[idx] = wp[:, self.cols[idx]:self.cols[idx] + self.sizes[idx]]
        o = self.small_col
        for idx in self.small:
            pieces[idx] = wp[:, o:o + self.sizes[idx]]
            o += self.sizes[idx]
        return jnp.concatenate([pieces[i] for i in range(len(self.sizes))], axis=1)


def _rows_form(cols_t, nh, nc):
    return cols_t.T.reshape(nh, nc, 1, CHUNK)


def _layer_fwd(x, p, lay, tag):
    s, d = x.shape
    nc = s // CHUNK
    w = d
    dnh, smh = lay.dn_heads, lay.ssm_heads
    r = smh // SSM_GROUPS
    cb = {k: v // LANES for k, v in lay.cols.items()}
    sv = {}
    h1 = _rms_fwd(x, p["norm_mix"], name=f"rms_mix_{tag}")
    proj = _matmul(h1, p["w_in"], name=f"mm_in_{tag}")
    small = proj[:, lay.small_col:lay.small_col + LANES]
    a_rows = _rows_form(small[:, 0:dnh], dnh, nc)
    b_rows = _rows_form(small[:, dnh:2 * dnh], dnh, nc)
    dt_rows = small[:, 2 * dnh:2 * dnh + smh].T.reshape(SSM_GROUPS, r, nc, CHUNK).transpose(0, 2, 1, 3)
    zero_b = jnp.zeros((1, 3 * w), F32)
    dn_qkv = _conv_fwd(proj, cb[0], p["dn_conv_w"], zero_b, 2 * dnh, name=f"dn_conv_{tag}")
    dn_alog = p["dn_a_log"].reshape(dnh, 1, 1)
    dn_dtb = p["dn_dt_bias"].reshape(dnh, 1, 1)
    o_dn, dn_states = _dn_fwd(dn_qkv, a_rows, b_rows, dn_alog, dn_dtb, name=f"dn_chunk_{tag}")
    y_dn = _dn_post_fwd(o_dn, proj, cb[1], p["dn_norm_w"], name=f"dn_post_{tag}")
    o_sb, sb_r = _sb_fwd(proj, cb[4], w, name=f"sb_{tag}")
    xbc = _conv_fwd(proj, cb[6], p["ssm_conv_w"], p["ssm_conv_b"].reshape(1, -1), 0, name=f"ssm_conv_{tag}")
    ssm_alog = p["ssm_a_log"].reshape(SSM_GROUPS, r, 1)
    ssm_dtb = p["ssm_dt_bias"].reshape(SSM_GROUPS, r, 1)
    y_ssd, ssm_states = _ssd_fwd(xbc, dt_rows, ssm_alog, ssm_dtb, name=f"ssd_{tag}")
    dexp = jnp.repeat(p["ssm_d"], SSM_HEAD_DIM)
    y_ssm = _ssm_post_fwd(y_ssd, xbc, proj, cb[5], dexp, p["ssm_norm_w"], name=f"ssm_post_{tag}")
    branches = (y_dn, o_sb, y_ssm)
    proj3 = jnp.concatenate(
        [_matmul(br, p["w_branch"][i], name=f"mm_branch{i}_{tag}") for i, br in enumerate(branches)], axis=1)
    merged = _merge_fwd(proj3, proj, cb[8], d, name=f"merge_{tag}")
    x1 = _matmul(merged, p["w_out"], name=f"mm_out_{tag}", epilogue=lambda acc, res: (acc + res,), extras=(x,))
    h2 = _rms_fwd(x1, p["norm_mlp"], name=f"rms_mlp_{tag}")
    u, act = _matmul(h2, p["w_up"], name=f"mm_up_{tag}", out_dtypes=(F32, MXU_DTYPE),
                     epilogue=lambda acc: (acc, jnp.square(jnp.maximum(acc, 0.0))))
    x2 = _matmul(act, p["w_down"], name=f"mm_down_{tag}", epilogue=lambda acc, res: (acc + res,), extras=(x1,))
    sv.update(x=x, h1=h1, proj=proj, a_rows=a_rows, b_rows=b_rows, dt_rows=dt_rows, dn_qkv=dn_qkv, dn_alog=dn_alog,
              dn_dtb=dn_dtb, o_dn=o_dn, dn_states=dn_states, y_dn=y_dn, o_sb=o_sb, sb_r=sb_r, xbc=xbc, ssm_alog=ssm_alog,
              ssm_dtb=ssm_dtb, y_ssd=y_ssd, ssm_states=ssm_states, dexp=dexp, y_ssm=y_ssm, proj3=proj3, merged=merged,
              x1=x1, h2=h2, u=u, act=act)
    return x2, sv


def _layer_bwd(dx2, p, sv, lay, tag):
    x = sv["x"]
    s, d = x.shape
    nc = s // CHUNK
    w = d
    dnh, smh = lay.dn_heads, lay.ssm_heads
    r = smh // SSM_GROUPS
    gn = SSM_GROUPS * SSM_STATE
    cb = {k: v // LANES for k, v in lay.cols.items()}
    proj = sv["proj"]
    g = {}
    dx2_b = dx2.astype(MXU_DTYPE)
    du = _matmul(dx2_b, p["w_down"], tb=True, name=f"mm_down_dx_{tag}", out_dtypes=(MXU_DTYPE,),
                 epilogue=lambda acc, uu: (acc * (2.0 * jnp.maximum(uu, 0.0)),), extras=(sv["u"],))
    g["w_down"] = _matmul(sv["act"], dx2_b, ta=True, name=f"mm_down_dw_{tag}", out_dtypes=(BF16,)).reshape(N_DEV, -1, d)
    g["w_up"] = _matmul(sv["h2"], du, ta=True, name=f"mm_up_dw_{tag}", out_dtypes=(BF16,), col_shards=N_DEV)
    dh2 = _matmul(du, p["w_up"], tb=True, name=f"mm_up_dx_{tag}")
    dx1, g["norm_mlp"] = _rms_bwd(sv["x1"], p["norm_mlp"], dh2, dx2, name=f"rms_mlp_bwd_{tag}")
    dx1_b = dx1.astype(MXU_DTYPE)
    dmerged = _matmul(dx1_b, p["w_out"], tb=True, name=f"mm_out_dx_{tag}")
    g["w_out"] = _matmul(sv["merged"], dx1_b, ta=True, name=f"mm_out_dw_{tag}", out_dtypes=(BF16,)).reshape(N_DEV, -1, d)
    dproj3, dgates = _merge_bwd(sv["proj3"], proj, cb[8], d, dmerged, name=f"merge_bwd_{tag}")
    branches = (sv["y_dn"], sv["o_sb"], sv["y_ssm"])
    dwb, dbr = [], []
    for i, br in enumerate(branches):
        dp_i = dproj3[:, i * d:(i + 1) * d]
        dwb.append(_matmul(br, dp_i, ta=True, name=f"mm_branch{i}_dw_{tag}", out_dtypes=(BF16,)).reshape(N_DEV, -1, d))
        dbr.append(_matmul(dp_i, p["w_branch"][i], tb=True, name=f"mm_branch{i}_dx_{tag}"))
    g["w_branch"] = jnp.stack(dwb, axis=1)
    dy_dn, do_sb, dy_ssm = dbr
    dy_ssd, dxs_skip, dz, ddexp, g["ssm_norm_w"] = _ssm_post_bwd(
        sv["y_ssd"], sv["xbc"], proj, cb[5], sv["dexp"], p["ssm_norm_w"], dy_ssm, name=f"ssm_post_bwd_{tag}")
    g["ssm_d"] = ddexp.reshape(smh, SSM_HEAD_DIM).sum(axis=1)
    dxs, dbm, dcm, ddt_rows, dalog, ddtb = _ssd_bwd(
        sv["xbc"], sv["dt_rows"], sv["ssm_alog"], sv["ssm_dtb"], sv["ssm_states"], dy_ssd, name=f"ssd_bwd_{tag}")
    g["ssm_a_log"] = dalog.reshape(smh)
    g["ssm_dt_bias"] = ddtb.reshape(smh)
    dxbc_post = jnp.concatenate([dxs + dxs_skip, dbm, dcm], axis=1)
    dxbc, g["ssm_conv_w"], dcb = _conv_bwd(proj, cb[6], p["ssm_conv_w"], p["ssm_conv_b"].reshape(1, -1), 0, dxbc_post,
                                           name=f"ssm_conv_bwd_{tag}")
    g["ssm_conv_b"] = dcb.reshape(-1)
    ddt = ddt_rows.transpose(0, 2, 1, 3).reshape(smh, s).T
    dq_sb, dk_sb, dv_sb = _sb_bwd(proj, cb[4], w, sv["sb_r"], do_sb, name=f"sb_bwd_{tag}")
    do_dn, dgate_dn, g["dn_norm_w"] = _dn_post_bwd(sv["o_dn"], proj, cb[1], p["dn_norm_w"], dy_dn, name=f"dn_post_bwd_{tag}")
    dq, dk, dv, da_rows, db_rows, dal, ddtb_dn = _dn_bwd(
        sv["dn_qkv"], sv["a_rows"], sv["b_rows"], sv["dn_alog"], sv["dn_dtb"], sv["dn_states"], do_dn, name=f"dn_chunk_bwd_{tag}")
    g["dn_a_log"] = dal.reshape(dnh)
    g["dn_dt_bias"] = ddtb_dn.reshape(dnh)
    zero_b = jnp.zeros((1, 3 * w), F32)
    ddn_qkv, g["dn_conv_w"], _ = _conv_bwd(proj, cb[0], p["dn_conv_w"], zero_b, 2 * dnh,
                                           jnp.concatenate([dq, dk, dv], axis=1), name=f"dn_conv_bwd_{tag}")
    da = da_rows.reshape(dnh, s).T
    db = db_rows.reshape(dnh, s).T
    dsmall = jnp.concatenate([da, db, ddt, jnp.zeros((s, LANES - lay.n_small), F32)], axis=1).astype(MXU_DTYPE)
    dproj = jnp.concatenate(
        [ddn_qkv, dgate_dn, dq_sb.astype(MXU_DTYPE), dk_sb.astype(MXU_DTYPE), dv_sb.astype(MXU_DTYPE), dz, dxbc, dgates, dsmall],
        axis=1)
    g["w_in"] = lay.to_shards(_matmul(sv["h1"], dproj, ta=True, name=f"mm_in_dw_{tag}", out_dtypes=(BF16,)))
    dh1 = _matmul(dproj, p["w_in"], tb=True, name=f"mm_in_dx_{tag}")
    dx0, g["norm_mix"] = _rms_bwd(x, p["norm_mix"], dh1, dx1, name=f"rms_mix_bwd_{tag}")
    return dx0, g


BIG = ("w_in", "w_branch", "w_out", "w_up", "w_down")
CONV = ("dn_conv_w", "ssm_conv_w")
SMALL = ("norm_mix", "dn_conv_w", "dn_a_log", "dn_dt_bias", "dn_norm_w", "ssm_conv_w", "ssm_conv_b", "ssm_a_log",
         "ssm_dt_bias", "ssm_d", "ssm_norm_w", "norm_mlp", "norm_final")
WEIGHTS = ("norm_mix", "w_in", "dn_conv_w", "dn_a_log", "dn_dt_bias", "dn_norm_w", "ssm_conv_w", "ssm_conv_b", "ssm_a_log",
           "ssm_dt_bias", "ssm_d", "ssm_norm_w", "w_branch", "w_out", "norm_mlp", "w_up", "w_down", "norm_final")
SHARD_AXIS = {"w_in": 2, "dn_conv_w": 2, "ssm_conv_w": 2, "w_branch": 2, "w_out": 1, "w_up": 2, "w_down": 1}


def _to_shards(full, axis):
    shp = full.shape
    n = shp[axis] // N_DEV
    t = full.reshape(shp[:axis] + (N_DEV, n) + shp[axis + 1:])
    return jnp.moveaxis(t, axis, 0)


def _from_shards(parts, axis):
    t = jnp.moveaxis(parts, 0, axis)
    shp = t.shape
    return t.reshape(shp[:axis] + (shp[axis] * shp[axis + 1],) + shp[axis + 2:])


def _step(w, m, v, x, target):
    s, d = x.shape
    lay = _Layout(d)
    me = 4 * lax.axis_index("x") + 2 * lax.axis_index("y") + lax.axis_index("c")

    send = [w[n].astype(BF16) for n in BIG] + [w[n] for n in CONV]
    gathered = dict(zip(BIG + CONV, _all_gather(send, name="weight_all_gather")))

    def layer_params(l):
        p = {n: w[n][l] for n in WEIGHTS if n not in BIG + CONV + ("norm_final",)}
        p["w_in"] = lay.from_shards(gathered["w_in"][:, l])
        for n in BIG[1:] + CONV:
            p[n] = _from_shards(gathered[n][:, l], SHARD_AXIS[n] - 1)
        return p

    params = [layer_params(l) for l in range(DEPTH)]

    saved = []
    h = x
    for l in range(DEPTH):
        h, sv = _layer_fwd(h, params[l], lay, f"l{l}")
        saved.append(sv)
    loss, dh, g_norm_final = _final_loss(h, w["norm_final"], target, name="final_loss")
    grads = [None] * DEPTH
    for l in reversed(range(DEPTH)):
        dh, grads[l] = _layer_bwd(dh, params[l], saved[l], lay, f"l{l}")
    grad_x = dh
    gfull = {n: jnp.stack([grads[l][n] for l in range(DEPTH)]) for n in SMALL if n != "norm_final"}
    gfull["norm_final"] = g_norm_final

    small_send = _pack([gfull[n] for n in SMALL] + [loss.reshape(1)], F32)
    big_recv, small_recv = _grad_exchange([[grads[l][n] for l in range(DEPTH)] for n in BIG], small_send,
                                          name="grad_exchange")

    out = {"grad": {}, "delta": {}, "new_m": {}, "new_v": {}}
    for n, parts in zip(BIG, big_recv):
        res = _sum_adamw(parts, w[n], m[n], v[n], name=f"sum_adamw_{n}")
        for key, a in zip(("grad", "delta", "new_m", "new_v"), res):
            out[key][n] = a
    small_sum = _sum_parts(small_recv, name="sum_small")
    small_full = _unpack(small_sum, [gfull[n].shape for n in SMALL] + [(1,)])
    loss_total = small_full[-1][0]
    gsmall = {}
    for n, a in zip(SMALL, small_full[:-1]):
        if n in SHARD_AXIS:
            a = lax.dynamic_index_in_dim(_to_shards(a, SHARD_AXIS[n]), me, axis=0, keepdims=False)
        gsmall[n] = a
    small_shapes = [w[n].shape for n in SMALL]
    ws, gs, ms, vs = (_pack([t[n] for n in SMALL], F32) for t in (w, gsmall, m, v))
    ds, m1s, v1s = _adamw(ws, gs, ms, vs, name="adamw_small")
    for n in SMALL:
        out["grad"][n] = gsmall[n]
    for key, packed in (("delta", ds), ("new_m", m1s), ("new_v", v1s)):
        for n, a in zip(SMALL, _unpack(packed, small_shapes)):
            out[key][n] = a
    return loss_total, grad_x, out


def kernel(x, norm_mix, w_in, dn_conv_w, dn_a_log, dn_dt_bias, dn_norm_w, ssm_conv_w, ssm_conv_b, ssm_a_log, ssm_dt_bias, ssm_d, ssm_norm_w, w_branch, w_out, norm_mlp, w_up, w_down, norm_final, loss_target, m_norm_mix, m_w_in, m_dn_conv_w, m_dn_a_log, m_dn_dt_bias, m_dn_norm_w, m_ssm_conv_w, m_ssm_conv_b, m_ssm_a_log, m_ssm_dt_bias, m_ssm_d, m_ssm_norm_w, m_w_branch, m_w_out, m_norm_mlp, m_w_up, m_w_down, m_norm_final, v_norm_mix, v_w_in, v_dn_conv_w, v_dn_a_log, v_dn_dt_bias, v_dn_norm_w, v_ssm_conv_w, v_ssm_conv_b, v_ssm_a_log, v_ssm_dt_bias, v_ssm_d, v_ssm_norm_w, v_w_branch, v_w_out, v_norm_mlp, v_w_up, v_w_down, v_norm_final):
    w = dict(norm_mix=norm_mix, w_in=w_in, dn_conv_w=dn_conv_w, dn_a_log=dn_a_log, dn_dt_bias=dn_dt_bias, dn_norm_w=dn_norm_w,
             ssm_conv_w=ssm_conv_w, ssm_conv_b=ssm_conv_b, ssm_a_log=ssm_a_log, ssm_dt_bias=ssm_dt_bias, ssm_d=ssm_d,
             ssm_norm_w=ssm_norm_w, w_branch=w_branch, w_out=w_out, norm_mlp=norm_mlp, w_up=w_up, w_down=w_down,
             norm_final=norm_final)
    m = dict(norm_mix=m_norm_mix, w_in=m_w_in, dn_conv_w=m_dn_conv_w, dn_a_log=m_dn_a_log, dn_dt_bias=m_dn_dt_bias,
             dn_norm_w=m_dn_norm_w, ssm_conv_w=m_ssm_conv_w, ssm_conv_b=m_ssm_conv_b, ssm_a_log=m_ssm_a_log,
             ssm_dt_bias=m_ssm_dt_bias, ssm_d=m_ssm_d, ssm_norm_w=m_ssm_norm_w, w_branch=m_w_branch, w_out=m_w_out,
             norm_mlp=m_norm_mlp, w_up=m_w_up, w_down=m_w_down, norm_final=m_norm_final)
    v = dict(norm_mix=v_norm_mix, w_in=v_w_in, dn_conv_w=v_dn_conv_w, dn_a_log=v_dn_a_log, dn_dt_bias=v_dn_dt_bias,
             dn_norm_w=v_dn_norm_w, ssm_conv_w=v_ssm_conv_w, ssm_conv_b=v_ssm_conv_b, ssm_a_log=v_ssm_a_log,
             ssm_dt_bias=v_ssm_dt_bias, ssm_d=v_ssm_d, ssm_norm_w=v_ssm_norm_w, w_branch=v_w_branch, w_out=v_w_out,
             norm_mlp=v_norm_mlp, w_up=v_w_up, w_down=v_w_down, norm_final=v_norm_final)
    loss, grad_x, out = _step(w, m, v, x[0], loss_target[0])
    return (loss, grad_x[None], *[out["grad"][n] for n in WEIGHTS], *[out["delta"][n] for n in WEIGHTS],
            *[out["new_m"][n] for n in WEIGHTS], *[out["new_v"][n] for n in WEIGHTS])
```

```python
import functools
import math

import jax
import jax.numpy as jnp
from jax import lax
from jax.experimental import pallas as pl
from jax.experimental.pallas import tpu as pltpu

F32 = jnp.float32
BF16 = jnp.bfloat16
MXU_DTYPE = BF16
HIGHEST = lax.Precision.HIGHEST

N_DEV = 8
DEPTH = 2
EPS = 1e-6
CONV_K = 4
DN_HEAD_DIM = 128
SB_HEAD_DIM = 64
SSM_HEAD_DIM = 64
SSM_STATE = 128
SSM_GROUPS = 4
CHUNK = 64
SB_BLOCK = 128
LANES = 128
ADAM_LR, ADAM_B1, ADAM_B2, ADAM_EPS, ADAM_WD, ADAM_STEP = 0.001, 0.9, 0.999, 1e-08, 0.01, 10
NEG_BIG = -1e30
DN_HEADS_PER_STEP = 8
SB_UNROLL = 4
SB_SPLIT = 2
CHUNK_PREC = lax.Precision.HIGH

ARB = "arbitrary"


def _cparams(n_axes):
    return pltpu.CompilerParams(dimension_semantics=(ARB,) * n_axes)


def _softplus(x):
    return jnp.maximum(x, 0.0) + jnp.log1p(jnp.exp(-jnp.abs(x)))


def _sigmoid(x):
    return 1.0 / (1.0 + jnp.exp(-x))


def _silu(x):
    return x * _sigmoid(x)


def _silu_grad(x):
    s = _sigmoid(x)
    return s * (1.0 + x * (1.0 - s))


def _dot(a, b, dims, prec=None):
    return lax.dot_general(a, b, (dims, ((), ())), precision=prec, preferred_element_type=F32)


NN = ((1,), (0,))
NT = ((1,), (1,))
TN = ((0,), (0,))


def _hdot(a, b, dims=NN):
    return _dot(a, b, dims, CHUNK_PREC)


def _bdot(a, b, dims=NN):
    return _dot(a.astype(MXU_DTYPE), b.astype(MXU_DTYPE), dims)


def _split_dot(a, m_bf16, nsplit=3):
    out = None
    rem = a
    for _ in range(nsplit):
        piece = rem.astype(BF16)
        rem = rem - piece.astype(F32)
        term = _dot(piece, m_bf16, NN)
        out = term if out is None else out + term
    return out


def _pick(n, pref):
    for t in pref:
        if n % t == 0:
            return t
    return n


def _matmul(a, b, *, ta=False, tb=False, name, epilogue=None, extras=(), out_dtypes=(F32,), col_shards=1,
            tm=None, tn=None, tk=None):
    m, k = (a.shape[1], a.shape[0]) if ta else a.shape
    k2, n = (b.shape[1], b.shape[0]) if tb else b.shape
    assert k == k2, (a.shape, b.shape, ta, tb)
    ncs = n // col_shards
    tm = tm or _pick(m, (512, 256, 128))
    tn = tn or _pick(ncs, (1024, 640, 512, 384, 256, 128))
    tk = tk or _pick(k, (1920, 1024, 640, 512, 256, 128))
    nk = k // tk
    a_spec = pl.BlockSpec((tk, tm), lambda i, j, kk: (kk, i)) if ta else pl.BlockSpec((tm, tk), lambda i, j, kk: (i, kk))
    b_spec = pl.BlockSpec((tn, tk), lambda i, j, kk: (j, kk)) if tb else pl.BlockSpec((tk, tn), lambda i, j, kk: (kk, j))
    e_spec = pl.BlockSpec((tm, tn), lambda i, j, kk: (i, j))
    if col_shards == 1:
        o_spec, o_shape = e_spec, (m, n)
    else:
        per = ncs // tn
        o_spec, o_shape = pl.BlockSpec((None, tm, tn), lambda i, j, kk: (j // per, i, j % per)), (col_shards, m, ncs)
    dims = (((0,) if ta else (1,)), ((1,) if tb else (0,)))
    n_extra = len(extras)
    n_out = len(out_dtypes)

    def body(*refs):
        a_ref, b_ref = refs[0], refs[1]
        extra_refs = refs[2:2 + n_extra]
        out_refs = refs[2 + n_extra:2 + n_extra + n_out]
        acc_ref = refs[-1]
        kk = pl.program_id(2)

        @pl.when(kk == 0)
        def _():
            acc_ref[...] = jnp.zeros_like(acc_ref)

        acc_ref[...] += _dot(a_ref[...].astype(MXU_DTYPE), b_ref[...].astype(MXU_DTYPE), dims)

        @pl.when(kk == nk - 1)
        def _():
            acc = acc_ref[...]
            outs = (acc,) if epilogue is None else epilogue(acc, *[r[...] for r in extra_refs])
            for o_ref, o in zip(out_refs, outs):
                o_ref[...] = o.astype(o_ref.dtype)

    outs = pl.pallas_call(
        body,
        grid=(m // tm, n // tn, nk),
        in_specs=[a_spec, b_spec] + [e_spec] * n_extra,
        out_specs=[o_spec] * n_out,
        out_shape=[jax.ShapeDtypeStruct(o_shape, dt) for dt in out_dtypes],
        scratch_shapes=[pltpu.VMEM((tm, tn), F32)],
        compiler_params=pltpu.CompilerParams(dimension_semantics=("parallel", "parallel", ARB)),
        name=name,
    )(a, b, *extras)
    return outs[0] if n_out == 1 else tuple(outs)


def _rms_fwd(x, w, *, name, tm=256):
    s, d = x.shape
    out_dtype = MXU_DTYPE

    def body(x_ref, w_ref, o_ref):
        xv = x_ref[...]
        r = lax.rsqrt(jnp.mean(xv * xv, axis=-1, keepdims=True) + EPS)
        o_ref[...] = (xv * r * w_ref[...]).astype(o_ref.dtype)

    return pl.pallas_call(
        body, grid=(s // tm,),
        in_specs=[pl.BlockSpec((tm, d), lambda i: (i, 0)), pl.BlockSpec((1, d), lambda i: (0, 0))],
        out_specs=pl.BlockSpec((tm, d), lambda i: (i, 0)),
        out_shape=jax.ShapeDtypeStruct((s, d), out_dtype),
        compiler_params=_cparams(1), name=name,
    )(x, w.reshape(1, d))


def _rms_bwd(x, w, dh, dres, *, name, tm=256):
    s, d = x.shape

    def body(x_ref, w_ref, dh_ref, dres_ref, dx_ref, dw_ref):
        xv = x_ref[...]
        r = lax.rsqrt(jnp.mean(xv * xv, axis=-1, keepdims=True) + EPS)
        xh = xv * r
        dhv = dh_ref[...].astype(F32)
        dxn = dhv * w_ref[...]
        dx = r * (dxn - xh * jnp.mean(dxn * xh, axis=-1, keepdims=True))
        dx_ref[...] = dres_ref[...] + dx

        @pl.when(pl.program_id(0) == 0)
        def _():
            dw_ref[...] = jnp.zeros_like(dw_ref)

        dw_ref[...] += jnp.sum(dhv * xh, axis=0, keepdims=True)

    dx, dw = pl.pallas_call(
        body, grid=(s // tm,),
        in_specs=[pl.BlockSpec((tm, d), lambda i: (i, 0)), pl.BlockSpec((1, d), lambda i: (0, 0)),
                  pl.BlockSpec((tm, d), lambda i: (i, 0)), pl.BlockSpec((tm, d), lambda i: (i, 0))],
        out_specs=[pl.BlockSpec((tm, d), lambda i: (i, 0)), pl.BlockSpec((1, d), lambda i: (0, 0))],
        out_shape=[jax.ShapeDtypeStruct((s, d), F32), jax.ShapeDtypeStruct((1, d), F32)],
        compiler_params=_cparams(1), name=name,
    )(x, w.reshape(1, d), dh, dres)
    return dx, dw.reshape(d)


def _final_loss(x, w, target, *, name, tm=256):
    s, d = x.shape

    def body(x_ref, w_ref, t_ref, loss_ref, dx_ref, dw_ref):
        xv = x_ref[...]
        r = lax.rsqrt(jnp.mean(xv * xv, axis=-1, keepdims=True) + EPS)
        xh = xv * r
        err = xh * w_ref[...] - t_ref[...]
        dy = err * (1.0 / d)
        dxn = dy * w_ref[...]
        dx_ref[...] = r * (dxn - xh * jnp.mean(dxn * xh, axis=-1, keepdims=True))

        @pl.when(pl.program_id(0) == 0)
        def _():
            dw_ref[...] = jnp.zeros_like(dw_ref)
            loss_ref[...] = jnp.zeros_like(loss_ref)

        dw_ref[...] += jnp.sum(dy * xh, axis=0, keepdims=True)
        row = jnp.sum(err * err, axis=1, keepdims=True) * (0.5 / d)
        loss_ref[...] += jnp.sum(row, axis=0, keepdims=True)

    loss, dx, dw = pl.pallas_call(
        body, grid=(s // tm,),
        in_specs=[pl.BlockSpec((tm, d), lambda i: (i, 0)), pl.BlockSpec((1, d), lambda i: (0, 0)),
                  pl.BlockSpec((tm, d), lambda i: (i, 0))],
        out_specs=[pl.BlockSpec((1, 1), lambda i: (0, 0)), pl.BlockSpec((tm, d), lambda i: (i, 0)),
                   pl.BlockSpec((1, d), lambda i: (0, 0))],
        out_shape=[jax.ShapeDtypeStruct((1, 1), F32), jax.ShapeDtypeStruct((s, d), F32), jax.ShapeDtypeStruct((1, d), F32)],
        compiler_params=_cparams(1), name=name,
    )(x, w.reshape(1, d), target)
    return loss[0, 0], dx, dw.reshape(d)


def _shift_down(x, sh, t_idx):
    return jnp.where(t_idx >= sh, pltpu.roll(x, sh, 0), 0.0)


def _shift_up(x, sh, t_idx, s):
    return jnp.where(t_idx < s - sh, pltpu.roll(x, s - sh, 0), 0.0)


def _conv_pre(x, w_rows, b, t_idx):
    c = w_rows[CONV_K - 1] * x + b
    for sh in range(1, CONV_K):
        c = c + w_rows[CONV_K - 1 - sh] * _shift_down(x, sh, t_idx)
    return c


def _conv_fwd(src, col0, w, b, n_l2, *, name):
    s = src.shape[0]
    c_tot = w.shape[1]
    nblk = c_tot // LANES

    def body(x_ref, w_ref, b_ref, o_ref):
        j = pl.program_id(0)
        t_idx = lax.broadcasted_iota(jnp.int32, (s, LANES), 0)
        w_rows = [w_ref[kk:kk + 1, :] for kk in range(CONV_K)]
        y = _silu(_conv_pre(x_ref[...], w_rows, b_ref[...], t_idx))
        if n_l2 > 0:
            yn = y * lax.rsqrt(jnp.sum(y * y, axis=1, keepdims=True) + EPS)
            y = jnp.where(j < n_l2, yn, y)
        o_ref[...] = y

    return pl.pallas_call(
        body, grid=(nblk,),
        in_specs=[pl.BlockSpec((s, LANES), lambda j: (0, col0 + j)), pl.BlockSpec((CONV_K, LANES), lambda j: (0, j)),
                  pl.BlockSpec((1, LANES), lambda j: (0, j))],
        out_specs=pl.BlockSpec((s, LANES), lambda j: (0, j)),
        out_shape=jax.ShapeDtypeStruct((s, c_tot), F32),
        compiler_params=_cparams(1), name=name,
    )(src, w, b)


def _conv_bwd(src, col0, w, b, n_l2, dout, *, name):
    s = src.shape[0]
    c_tot = w.shape[1]
    nblk = c_tot // LANES

    def body(x_ref, w_ref, b_ref, do_ref, dx_ref, dw_ref, db_ref):
        j = pl.program_id(0)
        t_idx = lax.broadcasted_iota(jnp.int32, (s, LANES), 0)
        xv = x_ref[...]
        w_rows = [w_ref[kk:kk + 1, :] for kk in range(CONV_K)]
        c = _conv_pre(xv, w_rows, b_ref[...], t_idx)
        dy = do_ref[...]
        if n_l2 > 0:
            y = _silu(c)
            r = lax.rsqrt(jnp.sum(y * y, axis=1, keepdims=True) + EPS)
            dyn = r * dy - y * (r * r * r) * jnp.sum(dy * y, axis=1, keepdims=True)
            dy = jnp.where(j < n_l2, dyn, dy)
        dc = dy * _silu_grad(c)
        dx = w_rows[CONV_K - 1] * dc
        rows = [None] * CONV_K
        rows[CONV_K - 1] = jnp.sum(dc * xv, axis=0, keepdims=True)
        for sh in range(1, CONV_K):
            dx = dx + w_rows[CONV_K - 1 - sh] * _shift_up(dc, sh, t_idx, s)
            rows[CONV_K - 1 - sh] = jnp.sum(dc * _shift_down(xv, sh, t_idx), axis=0, keepdims=True)
        dx_ref[...] = dx.astype(dx_ref.dtype)
        for kk in range(CONV_K):
            dw_ref[kk:kk + 1, :] = rows[kk]
        db_ref[...] = jnp.sum(dc, axis=0, keepdims=True)

    return pl.pallas_call(
        body, grid=(nblk,),
        in_specs=[pl.BlockSpec((s, LANES), lambda j: (0, col0 + j)), pl.BlockSpec((CONV_K, LANES), lambda j: (0, j)),
                  pl.BlockSpec((1, LANES), lambda j: (0, j)), pl.BlockSpec((s, LANES), lambda j: (0, j))],
        out_specs=[pl.BlockSpec((s, LANES), lambda j: (0, j)), pl.BlockSpec((CONV_K, LANES), lambda j: (0, j)),
                   pl.BlockSpec((1, LANES), lambda j: (0, j))],
        out_shape=[jax.ShapeDtypeStruct((s, c_tot), MXU_DTYPE), jax.ShapeDtypeStruct((CONV_K, c_tot), F32),
                   jax.ShapeDtypeStruct((1, c_tot), F32)],
        compiler_params=_cparams(1), name=name,
    )(src, w, b, dout)


def _chunk_masks(c):
    ii = lax.broadcasted_iota(jnp.int32, (c, c), 0)
    jj = lax.broadcasted_iota(jnp.int32, (c, c), 1)
    return ii, jj


def _row_to_col(row, eye):
    return jnp.sum(jnp.where(eye, row, 0.0), axis=1, keepdims=True)


def _each(f, *lists):
    return [f(*xs) for xs in zip(*lists)]


def _dn_chunk(q, k, v, a_row, b_row, alog, dtb, s0):
    c = q[0].shape[0]
    ii, jj = _chunk_masks(c)
    causal, strict, eye = ii >= jj, ii > jj, ii == jj
    g_row = _each(lambda al, a, dt: -jnp.exp(al) * _softplus(a + dt), alog, a_row, dtb)
    beta_col = _each(lambda b: _row_to_col(_sigmoid(b), eye), b_row)
    g_col = _each(lambda g: _row_to_col(g, eye), g_row)
    gc_col = _each(lambda g: jnp.sum(jnp.where(causal, g, 0.0), axis=1, keepdims=True), g_row)
    gc_row = _each(lambda g: jnp.sum(jnp.where(jj >= ii, g, 0.0), axis=0, keepdims=True), g_col)
    decay = _each(lambda gc, gr: jnp.exp(jnp.where(causal, gc - gr, NEG_BIG)), gc_col, gc_row)
    kb = _each(jnp.multiply, k, beta_col)
    vb = _each(jnp.multiply, v, beta_col)
    nmat = _each(lambda kb_, k_, dc: -jnp.where(strict, _hdot(kb_, k_, NT) * dc, 0.0), kb, k, decay)
    xinv = _each(lambda n: jnp.where(eye, 1.0, 0.0) + n, nmat)
    pw = nmat
    for _ in range(int(math.log2(c)) - 1):
        pw = _each(lambda p: _hdot(p, p), pw)
        xinv = _each(lambda x, p: x + _hdot(x, p), xinv, pw)
    egc = _each(jnp.exp, gc_col)
    u = _each(_hdot, xinv, vb)
    w = _each(lambda x, kb_, e: _hdot(x, kb_ * e), xinv, kb, egc)
    qs = _each(lambda q_: q_ * (q_.shape[1] ** -0.5), q)
    attn = _each(lambda q_, k_, dc: _hdot(q_, k_, NT) * dc, qs, k, decay)
    gl = _each(lambda g: jnp.sum(g, axis=1, keepdims=True), g_row)
    kd = _each(lambda k_, gl_, gc: k_ * jnp.exp(gl_ - gc), k, gl, gc_col)
    v_new = _each(lambda u_, w_, s: u_ - _hdot(w_, s), u, w, s0)
    o = _each(lambda q_, e, s, at, vn: _hdot(q_ * e, s) + _hdot(at, vn), qs, egc, s0, attn, v_new)
    s1 = _each(lambda s, gl_, kd_, vn: s * jnp.exp(gl_) + _hdot(kd_, vn, TN), s0, gl, kd, v_new)
    return o, s1


def _dn_specs(nh, nc, hb, rev):
    n_of = (lambda n: nc - 1 - n) if rev else (lambda n: n)
    ng = nh // hb
    qkv = [pl.BlockSpec((CHUNK, hb * DN_HEAD_DIM), (lambda h, n, o=o: (n_of(n), o * ng + h))) for o in range(3)]
    row = pl.BlockSpec((hb, None, 1, CHUNK), lambda h, n: (h, n_of(n), 0, 0))
    scal = pl.BlockSpec((hb, 1, 1), lambda h, n: (h, 0, 0))
    o_spec = pl.BlockSpec((CHUNK, hb * DN_HEAD_DIM), lambda h, n: (n_of(n), h))
    st = pl.BlockSpec((hb, None, DN_HEAD_DIM, DN_HEAD_DIM), lambda h, n: (h, n_of(n), 0, 0))
    return qkv, row, scal, o_spec, st


def _dn_fwd(qkv, a_rows, b_rows, alog, dtb, *, name):
    s = qkv.shape[0]
    nh, nc = a_rows.shape[0], a_rows.shape[1]
    hb = min(DN_HEADS_PER_STEP, nh)
    qkv_specs, row, scal, o_spec, st = _dn_specs(nh, nc, hb, False)
    hd = DN_HEAD_DIM

    def body(q_ref, k_ref, v_ref, a_ref, b_ref, al_ref, dt_ref, o_ref, st_ref, state):
        @pl.when(pl.program_id(1) == 0)
        def _():
            state[...] = jnp.zeros_like(state)

        cols = [slice(h * hd, (h + 1) * hd) for h in range(hb)]
        s0 = [state[h] for h in range(hb)]
        for h in range(hb):
            st_ref[h] = s0[h]
        o, s1 = _dn_chunk([q_ref[:, cl] for cl in cols], [k_ref[:, cl] for cl in cols], [v_ref[:, cl] for cl in cols],
                          [a_ref[h] for h in range(hb)], [b_ref[h] for h in range(hb)],
                          [al_ref[h] for h in range(hb)], [dt_ref[h] for h in range(hb)], s0)
        for h in range(hb):
            o_ref[:, cols[h]] = o[h]
            state[h] = s1[h]

    return pl.pallas_call(
        body, grid=(nh // hb, nc),
        in_specs=qkv_specs + [row, row, scal, scal],
        out_specs=[o_spec, st],
        out_shape=[jax.ShapeDtypeStruct((s, nh * hd), F32), jax.ShapeDtypeStruct((nh, nc, hd, hd), F32)],
        scratch_shapes=[pltpu.VMEM((hb, hd, hd), F32)],
        compiler_params=_cparams(2), name=name,
    )(qkv, qkv, qkv, a_rows, b_rows, alog, dtb)


def _dn_bwd(qkv, a_rows, b_rows, alog, dtb, states, do, *, name):
    s = qkv.shape[0]
    nh, nc = a_rows.shape[0], a_rows.shape[1]
    hb = min(DN_HEADS_PER_STEP, nh)
    qkv_specs, row, scal, o_spec, st = _dn_specs(nh, nc, hb, True)
    hd = DN_HEAD_DIM

    def body(q_ref, k_ref, v_ref, a_ref, b_ref, al_ref, dt_ref, st_ref, do_ref,
             dq_ref, dk_ref, dv_ref, da_ref, db_ref, dal_ref, ddt_ref, dstate):
        @pl.when(pl.program_id(1) == 0)
        def _():
            dstate[...] = jnp.zeros_like(dstate)
            dal_ref[...] = jnp.zeros_like(dal_ref)
            ddt_ref[...] = jnp.zeros_like(ddt_ref)

        cols = [slice(h * hd, (h + 1) * hd) for h in range(hb)]
        heads = range(hb)
        args = ([q_ref[:, cl] for cl in cols], [k_ref[:, cl] for cl in cols], [v_ref[:, cl] for cl in cols],
                [a_ref[h] for h in heads], [b_ref[h] for h in heads], [al_ref[h] for h in heads],
                [dt_ref[h] for h in heads], [st_ref[h] for h in heads])
        _, vjp = jax.vjp(_dn_chunk, *args)
        dq, dk, dv, da, db, dal, ddt, ds0 = vjp(([do_ref[:, cl] for cl in cols], [dstate[h] for h in heads]))
        for h in heads:
            dq_ref[:, cols[h]] = dq[h]
            dk_ref[:, cols[h]] = dk[h]
            dv_ref[:, cols[h]] = dv[h]
            da_ref[h] = da[h]
            db_ref[h] = db[h]
            dal_ref[h] += dal[h]
            ddt_ref[h] += ddt[h]
            dstate[h] = ds0[h]

    w = nh * hd
    outs = pl.pallas_call(
        body, grid=(nh // hb, nc),
        in_specs=qkv_specs + [row, row, scal, scal, st, o_spec],
        out_specs=[o_spec, o_spec, o_spec, row, row, scal, scal],
        out_shape=[jax.ShapeDtypeStruct((s, w), F32)] * 3
        + [jax.ShapeDtypeStruct(a_rows.shape, F32)] * 2 + [jax.ShapeDtypeStruct((nh, 1, 1), F32)] * 2,
        scratch_shapes=[pltpu.VMEM((hb, hd, hd), F32)],
        compiler_params=_cparams(2), name=name,
    )(qkv, qkv, qkv, a_rows, b_rows, alog, dtb, states, do)
    return outs


def _dn_post_fwd(o, src, gate_col0, nw, *, name, tm=256):
    s, w = o.shape
    nh = w // DN_HEAD_DIM

    def body(o_ref, g_ref, w_ref, y_ref):
        ov = o_ref[...]
        r = lax.rsqrt(jnp.mean(ov * ov, axis=-1, keepdims=True) + EPS)
        y_ref[...] = (ov * r * w_ref[...] * _silu(g_ref[...])).astype(y_ref.dtype)

    blk = pl.BlockSpec((tm, DN_HEAD_DIM), lambda i, h: (i, h))
    return pl.pallas_call(
        body, grid=(s // tm, nh),
        in_specs=[blk, pl.BlockSpec((tm, DN_HEAD_DIM), lambda i, h: (i, gate_col0 + h)),
                  pl.BlockSpec((1, DN_HEAD_DIM), lambda i, h: (0, 0))],
        out_specs=blk, out_shape=jax.ShapeDtypeStruct((s, w), MXU_DTYPE),
        compiler_params=_cparams(2), name=name,
    )(o, src, nw.reshape(1, DN_HEAD_DIM))


def _dn_post_bwd(o, src, gate_col0, nw, dy, *, name, tm=256):
    s, w = o.shape
    nh = w // DN_HEAD_DIM

    def body(o_ref, g_ref, w_ref, dy_ref, do_ref, dg_ref, dw_ref):
        ov = o_ref[...]
        gv = g_ref[...]
        dyv = dy_ref[...]
        r = lax.rsqrt(jnp.mean(ov * ov, axis=-1, keepdims=True) + EPS)
        oh = ov * r
        dn = dyv * _silu(gv)
        dg_ref[...] = (dyv * (oh * w_ref[...]) * _silu_grad(gv)).astype(dg_ref.dtype)
        don = dn * w_ref[...]
        do_ref[...] = r * (don - oh * jnp.mean(don * oh, axis=-1, keepdims=True))

        @pl.when((pl.program_id(0) == 0) & (pl.program_id(1) == 0))
        def _():
            dw_ref[...] = jnp.zeros_like(dw_ref)

        dw_ref[...] += jnp.sum(dn * oh, axis=0, keepdims=True)

    blk = pl.BlockSpec((tm, DN_HEAD_DIM), lambda i, h: (i, h))
    wspec = pl.BlockSpec((1, DN_HEAD_DIM), lambda i, h: (0, 0))
    do, dg, dw = pl.pallas_call(
        body, grid=(s // tm, nh),
        in_specs=[blk, pl.BlockSpec((tm, DN_HEAD_DIM), lambda i, h: (i, gate_col0 + h)), wspec, blk],
        out_specs=[blk, blk, wspec],
        out_shape=[jax.ShapeDtypeStruct((s, w), F32), jax.ShapeDtypeStruct((s, w), MXU_DTYPE),
                   jax.ShapeDtypeStruct((1, DN_HEAD_DIM), F32)],
        compiler_params=_cparams(2), name=name,
    )(o, src, nw.reshape(1, DN_HEAD_DIM), dy)
    return do, dg, dw.reshape(DN_HEAD_DIM)


def _sb_consts():
    r2 = lax.broadcasted_iota(jnp.int32, (2 * SB_BLOCK, SB_BLOCK), 0)
    c2 = lax.broadcasted_iota(jnp.int32, (2 * SB_BLOCK, SB_BLOCK), 1)
    r = lax.broadcasted_iota(jnp.int32, (SB_BLOCK, SB_BLOCK), 0)
    c = lax.broadcasted_iota(jnp.int32, (SB_BLOCK, SB_BLOCK), 1)
    lm0 = c < SB_HEAD_DIM
    m_gt = jnp.where(r > c, 1.0, 0.0).astype(BF16)
    m_lt = jnp.where(r < c, 1.0, 0.0).astype(BF16)
    return r2, c2, lm0, m_gt, m_lt


def _sb_stack(x, lm0):
    return jnp.concatenate([jnp.where(lm0, x, 0.0), jnp.where(lm0, 0.0, x)], axis=0)


def _sb_unstack(x2, lm0):
    return jnp.where(lm0, x2[:SB_BLOCK], x2[SB_BLOCK:])


def _sb_fwd(src, col0, width, *, name):
    s = src.shape[0]
    nq = s // SB_BLOCK
    npair = width // LANES
    scale = SB_HEAD_DIM ** -0.5
    nu = math.gcd(SB_UNROLL, nq)

    def body(q_ref, k_ref, v_ref, o_ref, r_ref):
        i = pl.program_id(1)
        r2, c2, lm0, m_gt, _ = _sb_consts()
        t_glob = i * SB_BLOCK + (r2 & (SB_BLOCK - 1))
        q2 = (_sb_stack(q_ref[...], lm0) * scale).astype(MXU_DTYPE)

        def group(base, carry, masked):
            o2, rsum = carry
            js = [base + nu - 1 - u for u in range(nu)]
            offs = [pl.multiple_of(j * SB_BLOCK, SB_BLOCK) for j in js]
            zs = [_dot(q2, k_ref[pl.ds(off, SB_BLOCK), :].astype(MXU_DTYPE), NT) for off in offs]
            ts = [jnp.log(1.0 + jnp.exp(-jnp.abs(z))) for z in zs]
            lks = [-(jnp.maximum(z, 0.0) + t) for z, t in zip(zs, ts)]
            if masked:
                masks = [(j * SB_BLOCK + c2) < t_glob for j in js]
                lks = [jnp.where(mk, lk, 0.0) for mk, lk in zip(masks, lks)]
            sufs = [_split_dot(lk, m_gt, SB_SPLIT) for lk in lks]
            rs = [rsum]
            for lk in lks:
                rs.append(rs[-1] + jnp.sum(lk, axis=1, keepdims=True))
            wgts = [jnp.exp((jnp.minimum(z, 0.0) - t) + r_ + sf) for z, t, r_, sf in zip(zs, ts, rs, sufs)]
            if masked:
                wgts = [jnp.where(mk, wg, 0.0) for mk, wg in zip(masks, wgts)]
            for off, wg in zip(offs, wgts):
                o2 = o2 + _dot(wg.astype(MXU_DTYPE), v_ref[pl.ds(off, SB_BLOCK), :].astype(MXU_DTYPE), NN)
            return o2, rs[-1]

        top0 = (i // nu) * nu
        carry = group(top0, (jnp.zeros((2 * SB_BLOCK, LANES), F32), jnp.zeros((2 * SB_BLOCK, 1), F32)), True)
        o2, rsum = lax.fori_loop(1, i // nu + 1, lambda g, cr: group(top0 - nu * g, cr, False), carry)
        o_ref[...] = _sb_unstack(o2, lm0)
        r_ref[...] = _sb_unstack(jnp.broadcast_to(rsum, (2 * SB_BLOCK, LANES)), lm0)

    blk = pl.BlockSpec((SB_BLOCK, LANES), lambda p, i: (i, p))
    return pl.pallas_call(
        body, grid=(npair, nq),
        in_specs=[pl.BlockSpec((SB_BLOCK, LANES), lambda p, i: (i, col0 + p)),
                  pl.BlockSpec((s, LANES), lambda p, i: (0, col0 + npair + p)),
                  pl.BlockSpec((s, LANES), lambda p, i: (0, col0 + 2 * npair + p))],
        out_specs=[blk, blk],
        out_shape=[jax.ShapeDtypeStruct((s, width), F32), jax.ShapeDtypeStruct((s, width), F32)],
        compiler_params=_cparams(2), name=name,
    )(src, src, src)


def _sb_bwd(src, col0, width, rtot, do, *, name):
    s = src.shape[0]
    nq = s // SB_BLOCK
    npair = width // LANES
    scale = SB_HEAD_DIM ** -0.5
    nu = math.gcd(SB_UNROLL, nq)

    def body(q_ref, k_ref, v_ref, r_ref, do_ref, dq_ref, dk_ref, dv_ref):
        i = pl.program_id(1)

        @pl.when(i == 0)
        def _():
            dk_ref[...] = jnp.zeros_like(dk_ref)
            dv_ref[...] = jnp.zeros_like(dv_ref)

        r2, c2, lm0, m_gt, m_lt = _sb_consts()
        t_glob = i * SB_BLOCK + (r2 & (SB_BLOCK - 1))
        q2 = (_sb_stack(q_ref[...], lm0) * scale).astype(MXU_DTYPE)
        do2 = _sb_stack(do_ref[...], lm0).astype(MXU_DTYPE)
        rv = r_ref[...]
        rt = jnp.concatenate([jnp.max(jnp.where(lm0, rv, NEG_BIG), axis=1, keepdims=True),
                              jnp.max(jnp.where(lm0, NEG_BIG, rv), axis=1, keepdims=True)], axis=0)

        def group(g, carry, masked):
            dq2, psum, csum = carry
            js = [nu * g + u for u in range(nu)]
            offs = [pl.multiple_of(j * SB_BLOCK, SB_BLOCK) for j in js]
            kbs = [k_ref[pl.ds(off, SB_BLOCK), :].astype(MXU_DTYPE) for off in offs]
            zs = [_dot(q2, kb, NT) for kb in kbs]
            dws = [_dot(do2, v_ref[pl.ds(off, SB_BLOCK), :].astype(MXU_DTYPE), NT) for off in offs]
            ts = [jnp.log(1.0 + jnp.exp(-jnp.abs(z))) for z in zs]
            lks = [-(jnp.maximum(z, 0.0) + t) for z, t in zip(zs, ts)]
            if masked:
                masks = [(j * SB_BLOCK + c2) < t_glob for j in js]
                lks = [jnp.where(mk, lk, 0.0) for mk, lk in zip(masks, lks)]
            sufs = [_split_dot(lk, m_gt, SB_SPLIT) for lk in lks]
            lsums = [jnp.sum(lk, axis=1, keepdims=True) for lk in lks]
            logsigs = [jnp.minimum(z, 0.0) - t for z, t in zip(zs, ts)]
            wgts = []
            for lsg, lsum, sf in zip(logsigs, lsums, sufs):
                psum = psum + lsum
                wgts.append(jnp.exp(lsg + (rt - psum) + sf))
            if masked:
                wgts = [jnp.where(mk, wg, 0.0) for mk, wg in zip(masks, wgts)]
            dlogas = [wg * dw for wg, dw in zip(wgts, dws)]
            pres = [_split_dot(dl, m_lt, SB_SPLIT) for dl in dlogas]
            dlks = []
            for dl, pre in zip(dlogas, pres):
                dlks.append(csum + pre)
                csum = csum + jnp.sum(dl, axis=1, keepdims=True)
            if masked:
                dlks = [jnp.where(mk, dlk, 0.0) for mk, dlk in zip(masks, dlks)]
            sigs = [jnp.exp(lsg) for lsg in logsigs]
            dzbs = [(dl * (1.0 - sg) - dlk * sg).astype(MXU_DTYPE) for dl, sg, dlk in zip(dlogas, sigs, dlks)]
            for off, dzb, wg, kb in zip(offs, dzbs, wgts, kbs):
                dk_ref[pl.ds(off, SB_BLOCK), :] += _dot(dzb, q2, TN)
                dv_ref[pl.ds(off, SB_BLOCK), :] += _dot(wg.astype(MXU_DTYPE), do2, TN)
                dq2 = dq2 + _dot(dzb, kb, NN)
            return dq2, psum, csum

        zero_col = jnp.zeros((2 * SB_BLOCK, 1), F32)
        carry = lax.fori_loop(0, i // nu, lambda g, cr: group(g, cr, False),
                              (jnp.zeros((2 * SB_BLOCK, LANES), F32), zero_col, zero_col))
        dq2, _, _ = group(i // nu, carry, True)
        dq_ref[...] = _sb_unstack(dq2, lm0) * scale

    blk = pl.BlockSpec((SB_BLOCK, LANES), lambda p, i: (i, p))
    full = pl.BlockSpec((s, LANES), lambda p, i: (0, p))
    return pl.pallas_call(
        body, grid=(npair, nq),
        in_specs=[pl.BlockSpec((SB_BLOCK, LANES), lambda p, i: (i, col0 + p)),
                  pl.BlockSpec((s, LANES), lambda p, i: (0, col0 + npair + p)),
                  pl.BlockSpec((s, LANES), lambda p, i: (0, col0 + 2 * npair + p)),
                  blk, blk],
        out_specs=[blk, full, full],
        out_shape=[jax.ShapeDtypeStruct((s, width), F32)] * 3,
        compiler_params=_cparams(2), name=name,
    )(src, src, src, rtot, do)


def _ssd_group(xs, dt_rows, alogs, dtbs, bm, cm, h0s):
    c = bm.shape[0]
    ii, jj = _chunk_masks(c)
    causal, eye = ii >= jj, ii == jj
    scores = _hdot(cm, bm, NT)
    dt_r = _each(lambda dt, b: _softplus(dt + b), dt_rows, dtbs)
    a_r = _each(lambda al, dt: -jnp.exp(al) * dt, alogs, dt_r)
    dt_col = _each(lambda dt: _row_to_col(dt, eye), dt_r)
    a_col = _each(lambda a: _row_to_col(a, eye), a_r)
    ac_col = _each(lambda a: jnp.sum(jnp.where(causal, a, 0.0), axis=1, keepdims=True), a_r)
    ac_row = _each(lambda a: jnp.sum(jnp.where(jj >= ii, a, 0.0), axis=0, keepdims=True), a_col)
    lmat = _each(lambda c_, r_: jnp.exp(jnp.where(causal, c_ - r_, NEG_BIG)), ac_col, ac_row)
    xdt = _each(jnp.multiply, xs, dt_col)
    al = _each(lambda a: jnp.sum(a, axis=1, keepdims=True), a_r)
    ys = _each(lambda lm, xd, h0, ac: _hdot(scores * lm, xd) + _hdot(cm, h0, NT) * jnp.exp(ac), lmat, xdt, h0s, ac_col)
    h1s = _each(lambda h0, al_, xd, ac: h0 * jnp.exp(al_) + _hdot(xd * jnp.exp(al_ - ac), bm, TN), h0s, al, xdt, ac_col)
    return ys, h1s


def _ssd_specs(ng, nc, r, rev):
    n_of = (lambda n: nc - 1 - n) if rev else (lambda n: n)
    gw = r * SSM_HEAD_DIM
    x_spec = pl.BlockSpec((CHUNK, gw), lambda g, n: (n_of(n), g))
    b_spec = pl.BlockSpec((CHUNK, SSM_STATE), lambda g, n: (n_of(n), (ng * gw) // SSM_STATE + g))
    c_spec = pl.BlockSpec((CHUNK, SSM_STATE), lambda g, n: (n_of(n), (ng * gw) // SSM_STATE + ng + g))
    dt_spec = pl.BlockSpec((None, None, r, CHUNK), lambda g, n: (g, n_of(n), 0, 0))
    sc_spec = pl.BlockSpec((None, r, 1), lambda g, n: (g, 0, 0))
    st_spec = pl.BlockSpec((None, None, r, SSM_HEAD_DIM, SSM_STATE), lambda g, n: (g, n_of(n), 0, 0, 0))
    y_spec = pl.BlockSpec((CHUNK, gw), lambda g, n: (n_of(n), g))
    bc_out = pl.BlockSpec((CHUNK, SSM_STATE), lambda g, n: (n_of(n), g))
    return x_spec, b_spec, c_spec, dt_spec, sc_spec, st_spec, y_spec, bc_out


def _ssd_fwd(xbc, dt_rows, alog, dtb, *, name):
    s = xbc.shape[0]
    ng, nc, r = dt_rows.shape[0], dt_rows.shape[1], dt_rows.shape[2]
    w = ng * r * SSM_HEAD_DIM
    x_spec, b_spec, c_spec, dt_spec, sc_spec, st_spec, y_spec, _ = _ssd_specs(ng, nc, r, False)
    p = SSM_HEAD_DIM

    def body(x_ref, b_ref, c_ref, dt_ref, al_ref, db_ref, y_ref, st_ref, state):
        @pl.when(pl.program_id(1) == 0)
        def _():
            state[...] = jnp.zeros_like(state)

        st_ref[...] = state[...]
        xs = [x_ref[:, h * p:(h + 1) * p] for h in range(r)]
        dts = [dt_ref[h:h + 1, :] for h in range(r)]
        als = [al_ref[h:h + 1, :] for h in range(r)]
        dbs = [db_ref[h:h + 1, :] for h in range(r)]
        h0s = [state[h] for h in range(r)]
        ys, h1s = _ssd_group(xs, dts, als, dbs, b_ref[...], c_ref[...], h0s)
        for h in range(r):
            y_ref[:, h * p:(h + 1) * p] = ys[h]
            state[h] = h1s[h]

    return pl.pallas_call(
        body, grid=(ng, nc),
        in_specs=[x_spec, b_spec, c_spec, dt_spec, sc_spec, sc_spec],
        out_specs=[y_spec, st_spec],
        out_shape=[jax.ShapeDtypeStruct((s, w), F32), jax.ShapeDtypeStruct((ng, nc, r, p, SSM_STATE), F32)],
        scratch_shapes=[pltpu.VMEM((r, p, SSM_STATE), F32)],
        compiler_params=_cparams(2), name=name,
    )(xbc, xbc, xbc, dt_rows, alog, dtb)


def _ssd_bwd(xbc, dt_rows, alog, dtb, states, dy, *, name):
    s = xbc.shape[0]
    ng, nc, r = dt_rows.shape[0], dt_rows.shape[1], dt_rows.shape[2]
    w = ng * r * SSM_HEAD_DIM
    x_spec, b_spec, c_spec, dt_spec, sc_spec, st_spec, y_spec, bc_out = _ssd_specs(ng, nc, r, True)
    p = SSM_HEAD_DIM

    def body(x_ref, b_ref, c_ref, dt_ref, al_ref, db_ref, st_ref, dy_ref,
             dx_ref, dbm_ref, dcm_ref, ddt_ref, dal_ref, ddb_ref, dstate):
        @pl.when(pl.program_id(1) == 0)
        def _():
            dstate[...] = jnp.zeros_like(dstate)
            dal_ref[...] = jnp.zeros_like(dal_ref)
            ddb_ref[...] = jnp.zeros_like(ddb_ref)

        xs = [x_ref[:, h * p:(h + 1) * p] for h in range(r)]
        dts = [dt_ref[h:h + 1, :] for h in range(r)]
        als = [al_ref[h:h + 1, :] for h in range(r)]
        dbs = [db_ref[h:h + 1, :] for h in range(r)]
        h0s = [st_ref[h] for h in range(r)]
        _, vjp = jax.vjp(_ssd_group, xs, dts, als, dbs, b_ref[...], c_ref[...], h0s)
        dys = [dy_ref[:, h * p:(h + 1) * p] for h in range(r)]
        dh1s = [dstate[h] for h in range(r)]
        dxs, ddts, dals, ddbs, dbm, dcm, dh0s = vjp((dys, dh1s))
        dbm_ref[...] = dbm
        dcm_ref[...] = dcm
        for h in range(r):
            dx_ref[:, h * p:(h + 1) * p] = dxs[h]
            ddt_ref[h:h + 1, :] = ddts[h]
            dal_ref[h:h + 1, :] += dals[h]
            ddb_ref[h:h + 1, :] += ddbs[h]
            dstate[h] = dh0s[h]

    gn = ng * SSM_STATE
    return pl.pallas_call(
        body, grid=(ng, nc),
        in_specs=[x_spec, b_spec, c_spec, dt_spec, sc_spec, sc_spec, st_spec, y_spec],
        out_specs=[y_spec, bc_out, bc_out, dt_spec, sc_spec, sc_spec],
        out_shape=[jax.ShapeDtypeStruct((s, w), F32), jax.ShapeDtypeStruct((s, gn), F32), jax.ShapeDtypeStruct((s, gn), F32),
                   jax.ShapeDtypeStruct(dt_rows.shape, F32), jax.ShapeDtypeStruct((ng, r, 1), F32),
                   jax.ShapeDtypeStruct((ng, r, 1), F32)],
        scratch_shapes=[pltpu.VMEM((r, p, SSM_STATE), F32)],
        compiler_params=_cparams(2), name=name,
    )(xbc, xbc, xbc, dt_rows, alog, dtb, states, dy)


def _ssm_post_fwd(y, xbc, src, z_col0, dexp, nw, *, name, tm=256):
    s, w = y.shape
    gw = w // SSM_GROUPS
    zc = z_col0 * LANES // gw

    def body(y_ref, x_ref, z_ref, d_ref, w_ref, o_ref):
        yy = (y_ref[...] + x_ref[...] * d_ref[...]) * _silu(z_ref[...])
        r = lax.rsqrt(jnp.mean(yy * yy, axis=-1, keepdims=True) + EPS)
        o_ref[...] = (yy * r * w_ref[...]).astype(o_ref.dtype)

    blk = pl.BlockSpec((tm, gw), lambda g, i: (i, g))
    vec = pl.BlockSpec((1, gw), lambda g, i: (0, g))
    return pl.pallas_call(
        body, grid=(SSM_GROUPS, s // tm),
        in_specs=[blk, blk, pl.BlockSpec((tm, gw), lambda g, i: (i, zc + g)), vec, vec],
        out_specs=blk, out_shape=jax.ShapeDtypeStruct((s, w), MXU_DTYPE),
        compiler_params=_cparams(2), name=name,
    )(y, xbc, src, dexp.reshape(1, w), nw.reshape(1, w))


def _ssm_post_bwd(y, xbc, src, z_col0, dexp, nw, dout, *, name, tm=256):
    s, w = y.shape
    gw = w // SSM_GROUPS
    zc = z_col0 * LANES // gw

    def body(y_ref, x_ref, z_ref, d_ref, w_ref, do_ref, dy_ref, dx_ref, dz_ref, dd_ref, dw_ref):
        xv, zv, dv = x_ref[...], z_ref[...], d_ref[...]
        pre = y_ref[...] + xv * dv
        sz = _silu(zv)
        yy = pre * sz
        r = lax.rsqrt(jnp.mean(yy * yy, axis=-1, keepdims=True) + EPS)
        yh = yy * r
        dov = do_ref[...]
        dyn = dov * w_ref[...]
        dyy = r * (dyn - yh * jnp.mean(dyn * yh, axis=-1, keepdims=True))
        dpre = dyy * sz
        dy_ref[...] = dpre
        dx_ref[...] = dpre * dv
        dz_ref[...] = (dyy * pre * _silu_grad(zv)).astype(dz_ref.dtype)

        @pl.when(pl.program_id(1) == 0)
        def _():
            dd_ref[...] = jnp.zeros_like(dd_ref)
            dw_ref[...] = jnp.zeros_like(dw_ref)

        dd_ref[...] += jnp.sum(dpre * xv, axis=0, keepdims=True)
        dw_ref[...] += jnp.sum(dov * yh, axis=0, keepdims=True)

    blk = pl.BlockSpec((tm, gw), lambda g, i: (i, g))
    vec = pl.BlockSpec((1, gw), lambda g, i: (0, g))
    dy, dx, dz, dd, dw = pl.pallas_call(
        body, grid=(SSM_GROUPS, s // tm),
        in_specs=[blk, blk, pl.BlockSpec((tm, gw), lambda g, i: (i, zc + g)), vec, vec, blk],
        out_specs=[blk, blk, blk, vec, vec],
        out_shape=[jax.ShapeDtypeStruct((s, w), F32), jax.ShapeDtypeStruct((s, w), F32), jax.ShapeDtypeStruct((s, w), MXU_DTYPE),
                   jax.ShapeDtypeStruct((1, w), F32), jax.ShapeDtypeStruct((1, w), F32)],
        compiler_params=_cparams(2), name=name,
    )(y, xbc, src, dexp.reshape(1, w), nw.reshape(1, w), dout)
    return dy, dx, dz, dd.reshape(w), dw.reshape(w)


def _merge_fwd(proj3, src, gate_col0, d, *, name, tm=256):
    s = proj3.shape[0]
    nb = proj3.shape[1] // d
    gc = gate_col0 * LANES // d

    def body(*refs):
        p_refs, g_refs, o_ref = refs[:nb], refs[nb:2 * nb], refs[-1]
        acc = None
        for p_ref, g_ref in zip(p_refs, g_refs):
            term = _sigmoid(g_ref[...]) * p_ref[...]
            acc = term if acc is None else acc + term
        o_ref[...] = acc.astype(o_ref.dtype)

    p_specs = [pl.BlockSpec((tm, d), lambda i, b=b: (i, b)) for b in range(nb)]
    g_specs = [pl.BlockSpec((tm, d), lambda i, b=b: (i, gc + b)) for b in range(nb)]
    return pl.pallas_call(
        body, grid=(s // tm,), in_specs=p_specs + g_specs,
        out_specs=pl.BlockSpec((tm, d), lambda i: (i, 0)), out_shape=jax.ShapeDtypeStruct((s, d), MXU_DTYPE),
        compiler_params=_cparams(1), name=name,
    )(*([proj3] * nb), *([src] * nb))


def _merge_bwd(proj3, src, gate_col0, d, dmerged, *, name, tm=256):
    s = proj3.shape[0]
    nb = proj3.shape[1] // d
    gc = gate_col0 * LANES // d

    def body(p_ref, g_ref, dm_ref, dp_ref, dg_ref):
        sg = _sigmoid(g_ref[...])
        dm = dm_ref[...]
        dp_ref[...] = (dm * sg).astype(dp_ref.dtype)
        dg_ref[...] = (dm * p_ref[...] * sg * (1.0 - sg)).astype(dg_ref.dtype)

    blk = pl.BlockSpec((tm, d), lambda i, b: (i, b))
    return pl.pallas_call(
        body, grid=(s // tm, nb),
        in_specs=[blk, pl.BlockSpec((tm, d), lambda i, b: (i, gc + b)), pl.BlockSpec((tm, d), lambda i, b: (i, 0))],
        out_specs=[blk, blk],
        out_shape=[jax.ShapeDtypeStruct(proj3.shape, MXU_DTYPE), jax.ShapeDtypeStruct(proj3.shape, MXU_DTYPE)],
        compiler_params=_cparams(2), name=name,
    )(proj3, src, dmerged)


ANY = pl.BlockSpec(memory_space=pl.ANY)
MESH = pl.DeviceIdType.MESH


def _all_gather(shards, *, name):
    nt = len(shards)

    def body(*refs):
        x_refs, out_refs = refs[:nt], refs[nt:2 * nt]
        send_sems, recv_sems, local_sems = refs[2 * nt:]
        x, y, c = lax.axis_index("x"), lax.axis_index("y"), lax.axis_index("c")
        me, sibling = (x, y, c), (x, y, 1 - c)
        chips = [(1 - x, y), (x, 1 - y), (1 - x, 1 - y)]

        def slot(t, px, py, pc):
            return out_refs[t].at[4 * px + 2 * py + pc]

        def copy(t, k, block, to, from_input=False):
            return pltpu.make_async_remote_copy(
                src_ref=x_refs[t] if from_input else slot(t, *block), dst_ref=slot(t, *block),
                send_sem=send_sems.at[7 * t + k], recv_sem=recv_sems.at[7 * t + k], device_id=to, device_id_type=MESH)

        mine = [pltpu.make_async_copy(x_refs[t], slot(t, *me), local_sems.at[t]) for t in range(nt)]
        for cp in mine:
            cp.start()
        first = [copy(t, 0, me, sibling, True) for t in range(nt)]
        first += [copy(t, 1 + j, me, (*chip, c), True) for j, chip in enumerate(chips) for t in range(nt)]
        for cp in first:
            cp.start()
        passed = []
        for j, chip in enumerate(chips):
            for t in range(nt):
                copy(t, 1 + j, (*chip, c), me).wait_recv()
                fwd = copy(t, 4 + j, (*chip, c), sibling)
                fwd.start()
                passed.append(fwd)
        for t in range(nt):
            copy(t, 0, sibling, me).wait_recv()
            for j, chip in enumerate(chips):
                copy(t, 4 + j, (*chip, 1 - c), me).wait_recv()
        for cp in first + passed:
            cp.wait_send()
        for cp in mine:
            cp.wait()

    return pl.pallas_call(
        body, out_shape=[jax.ShapeDtypeStruct((N_DEV,) + a.shape, a.dtype) for a in shards],
        in_specs=[ANY] * nt, out_specs=[ANY] * nt,
        scratch_shapes=[pltpu.SemaphoreType.DMA((7 * nt,)), pltpu.SemaphoreType.DMA((7 * nt,)),
                        pltpu.SemaphoreType.DMA((nt,))],
        name=name,
    )(*shards)


def _grad_exchange(bigs, small, *, name):
    nl = len(bigs[0])
    flat = [a for per_layer in bigs for a in per_layer]
    nslot = len(flat)

    def body(*refs):
        in_refs, small_ref = refs[:nslot], refs[nslot]
        out_refs, smallr_ref = refs[nslot + 1:nslot + 1 + len(bigs)], refs[nslot + 1 + len(bigs)]
        send_sems, recv_sems, local_sems = refs[nslot + 2 + len(bigs):]
        x, y, c = lax.axis_index("x"), lax.axis_index("y"), lax.axis_index("c")
        me = 4 * x + 2 * y + c
        local = [pltpu.make_async_copy(in_refs[i].at[me], out_refs[i // nl].at[me, i % nl], local_sems.at[i])
                 for i in range(nslot)]
        local.append(pltpu.make_async_copy(small_ref, smallr_ref.at[me], local_sems.at[nslot]))
        for cp in local:
            cp.start()
        copies = []
        for k in range(1, N_DEV):
            px = x ^ ((k >> 2) & 1)
            py = y ^ ((k >> 1) & 1)
            pc = c ^ (k & 1)
            peer = 4 * px + 2 * py + pc
            for i in range(nslot + 1):
                sem = 7 * i + (k - 1)
                src = in_refs[i].at[peer] if i < nslot else small_ref
                dst = out_refs[i // nl].at[me, i % nl] if i < nslot else smallr_ref.at[me]
                copies.append(pltpu.make_async_remote_copy(
                    src_ref=src, dst_ref=dst, send_sem=send_sems.at[sem], recv_sem=recv_sems.at[sem],
                    device_id=(px, py, pc), device_id_type=MESH))
        for cp in copies:
            cp.start()
        for cp in copies:
            cp.wait_recv()
        for cp in copies:
            cp.wait_send()
        for cp in local:
            cp.wait()

    out_shape = [jax.ShapeDtypeStruct((N_DEV, nl) + per_layer[0].shape[1:], per_layer[0].dtype) for per_layer in bigs]
    out_shape.append(jax.ShapeDtypeStruct((N_DEV,) + small.shape, small.dtype))
    nsem = 7 * (nslot + 1)
    outs = pl.pallas_call(
        body, out_shape=out_shape,
        in_specs=[ANY] * (nslot + 1), out_specs=[ANY] * (len(bigs) + 1),
        scratch_shapes=[pltpu.SemaphoreType.DMA((nsem,)), pltpu.SemaphoreType.DMA((nsem,)),
                        pltpu.SemaphoreType.DMA((nslot + 1,))],
        name=name,
    )(*flat, small)
    return outs[:-1], outs[-1]


HBM = pl.BlockSpec(memory_space=pltpu.HBM)
SEM = pl.BlockSpec(memory_space=pltpu.SEMAPHORE)
EFFECT = pltpu.SideEffectType.DATAFLOW_SIDE_EFFECTING


def _peers():
    x, y, c = lax.axis_index("x"), lax.axis_index("y"), lax.axis_index("c")
    peers = []
    for k in range(1, N_DEV):
        px, py, pc = x ^ ((k >> 2) & 1), y ^ ((k >> 1) & 1), c ^ (k & 1)
        peers.append(((px, py, pc), 4 * px + 2 * py + pc))
    return 4 * x + 2 * y + c, peers


def _split_copies(slots, src_refs, land_refs, send_sems, recv_sems):
    me, peers = _peers()
    copies = []
    for t, (whole, layer) in enumerate(slots):
        dst = land_refs[t].at[me] if layer is None else land_refs[t].at[me, layer]
        for k, (dev, lin) in enumerate(peers):
            copies.append(pltpu.make_async_remote_copy(
                src_ref=src_refs[t] if whole else src_refs[t].at[lin], dst_ref=dst,
                send_sem=send_sems.at[7 * t + k], recv_sem=recv_sems.at[7 * t + k], device_id=dev, device_id_type=MESH))
    return copies


def _split_start(srcs, lands, slots, carry, *, name):
    n = len(srcs)

    def body(*refs):
        copies = _split_copies(slots, refs[:n], refs[n:2 * n], refs[2 * n + 1], refs[2 * n + 2])
        for cp in copies:
            cp.start()

    def hbm(a):
        return pltpu.HBM(a.shape, a.dtype)

    outs = pl.pallas_call(
        body, name=name,
        out_shape=[pltpu.SemaphoreType.DMA((7 * n,)), pltpu.SemaphoreType.DMA((7 * n,))]
        + [hbm(a) for a in srcs] + [hbm(a) for a in lands] + [hbm(carry)],
        in_specs=[HBM] * (2 * n + 1), out_specs=[SEM, SEM] + [HBM] * (2 * n + 1),
        input_output_aliases={i: 2 + i for i in range(2 * n + 1)},
        compiler_params=pltpu.CompilerParams(has_side_effects=EFFECT),
    )(*[pltpu.with_memory_space_constraint(a, pltpu.HBM) for a in list(srcs) + list(lands) + [carry]])
    return outs[0], outs[1], outs[2:2 + n], outs[2 + n:2 + 2 * n], outs[2 + 2 * n]


def _split_wait(send_sems, recv_sems, srcs, lands, slots, after, *, name):
    n = len(srcs)

    def body(*refs):
        copies = _split_copies(slots, refs[:n], refs[n:2 * n], refs[2 * n], refs[2 * n + 1])
        for cp in copies:
            cp.wait_send()
        for cp in copies:
            cp.wait_recv()

    outs = pl.pallas_call(
        body, name=name,
        out_shape=[pltpu.HBM(a.shape, a.dtype) for a in list(srcs) + list(lands)],
        in_specs=[HBM] * (2 * n) + [SEM, SEM, ANY], out_specs=[HBM] * (2 * n),
        input_output_aliases={i: i for i in range(2 * n)},
        compiler_params=pltpu.CompilerParams(has_side_effects=EFFECT),
    )(*srcs, *lands, send_sems, recv_sems, after)
    return outs[n:]


def _adam_math(w, g, m, v):
    m1 = ADAM_B1 * m + (1.0 - ADAM_B1) * g
    v1 = ADAM_B2 * v + (1.0 - ADAM_B2) * (g * g)
    m_hat = m1 / (1.0 - ADAM_B1 ** ADAM_STEP)
    v_hat = v1 / (1.0 - ADAM_B2 ** ADAM_STEP)
    delta = -ADAM_LR * (m_hat / (jnp.sqrt(v_hat) + ADAM_EPS) + ADAM_WD * w)
    return delta, m1, v1


def _sum_adamw(parts, w, m, v, *, name):
    shape = w.shape
    r, c = shape[-2], shape[-1]
    a = math.prod(shape[:-2])
    tr = _pick(r, (256,) if c <= 1024 else (128,))
    w3, m3, v3 = (t.reshape(a, r, c) for t in (w, m, v))

    def body(p_ref, w_ref, m_ref, v_ref, g_ref, d_ref, m1_ref, v1_ref):
        g = p_ref[0].astype(F32)
        for src in range(1, N_DEV):
            g = g + p_ref[src].astype(F32)
        delta, m1, v1 = _adam_math(w_ref[...], g, m_ref[...], v_ref[...])
        g_ref[...] = g
        d_ref[...] = delta
        m1_ref[...] = m1
        v1_ref[...] = v1

    blk = pl.BlockSpec((None, tr, c), lambda i, j: (i, j, 0))
    outs = pl.pallas_call(
        body, grid=(a, r // tr),
        in_specs=[pl.BlockSpec((N_DEV, None, tr, c), lambda i, j: (0, i, j, 0)), blk, blk, blk],
        out_specs=[blk] * 4, out_shape=[jax.ShapeDtypeStruct((a, r, c), F32)] * 4,
        compiler_params=_cparams(2), name=name,
    )(parts.reshape(N_DEV, a, r, c), w3, m3, v3)
    return [o.reshape(shape) for o in outs]


def _sum_parts(parts, *, name):
    rows = parts.shape[1]

    def body(p_ref, o_ref):
        g = p_ref[0]
        for src in range(1, N_DEV):
            g = g + p_ref[src]
        o_ref[...] = g

    return pl.pallas_call(
        body, grid=(1,), in_specs=[pl.BlockSpec((N_DEV, rows, LANES), lambda i: (0, 0, 0))],
        out_specs=pl.BlockSpec((rows, LANES), lambda i: (0, 0)), out_shape=jax.ShapeDtypeStruct((rows, LANES), F32),
        compiler_params=_cparams(1), name=name,
    )(parts)


def _adamw(w, g, m, v, *, name):
    rows = w.shape[0]

    def body(w_ref, g_ref, m_ref, v_ref, d_ref, m1_ref, v1_ref):
        delta, m1, v1 = _adam_math(w_ref[...], g_ref[...], m_ref[...], v_ref[...])
        d_ref[...] = delta
        m1_ref[...] = m1
        v1_ref[...] = v1

    blk = pl.BlockSpec((rows, LANES), lambda i: (0, 0))
    return pl.pallas_call(
        body, grid=(1,), in_specs=[blk] * 4, out_specs=[blk] * 3,
        out_shape=[jax.ShapeDtypeStruct((rows, LANES), F32)] * 3,
        compiler_params=_cparams(1), name=name,
    )(w, g, m, v)


def _pack(arrs, dtype, row_mult=16):
    flat = jnp.concatenate([a.reshape(-1).astype(dtype) for a in arrs])
    n = flat.shape[0]
    rows = -(-n // (LANES * row_mult)) * row_mult
    flat = jnp.pad(flat, (0, rows * LANES - n))
    return flat.reshape(rows, LANES)


def _unpack(packed, shapes):
    flat = packed.reshape(-1)
    out, off = [], 0
    for shp in shapes:
        n = math.prod(shp)
        out.append(flat[off:off + n].reshape(shp))
        off += n
    return out


class _Layout:
    def __init__(self, d):
        self.d = d
        w = d
        self.dn_heads = w // DN_HEAD_DIM
        self.ssm_heads = w // SSM_HEAD_DIM
        gn = SSM_GROUPS * SSM_STATE
        self.sizes = (3 * w, w, self.dn_heads, self.dn_heads, 3 * w, w, w + 2 * gn, self.ssm_heads, 3 * d)
        offs, o = [], 0
        for sz in self.sizes:
            offs.append(o)
            o += sz
        self.offs = offs
        self.in_dim = o
        self.big = (0, 1, 4, 5, 6, 8)
        self.small = (2, 3, 7)
        cols, o = {}, 0
        for idx in self.big:
            cols[idx] = o
            o += self.sizes[idx]
        self.small_col = o
        self.cols = cols
        self.padded = o + LANES
        self.n_small = sum(self.sizes[i] for i in self.small)

    def reorder_w(self, w_in):
        parts = [w_in[:, self.offs[i]:self.offs[i] + self.sizes[i]] for i in self.big + self.small]
        parts.append(jnp.zeros((w_in.shape[0], LANES - self.n_small), w_in.dtype))
        return jnp.concatenate(parts, axis=1)

    def from_shards(self, parts):
        cs = self.in_dim // N_DEV
        pieces = []
        for i in self.big + self.small:
            a, b = self.offs[i], self.offs[i] + self.sizes[i]
            while a < b:
                j = a // cs
                hi = min(b, (j + 1) * cs)
                pieces.append(parts[j][:, a - j * cs:hi - j * cs])
                a = hi
        pieces.append(jnp.zeros((parts.shape[1], LANES - self.n_small), parts.dtype))
        return jnp.concatenate(pieces, axis=1)

    def to_shards(self, wp):
        cs = self.in_dim // N_DEV
        pcol = dict(self.cols)
        o = self.small_col
        for i in self.small:
            pcol[i] = o
            o += self.sizes[i]
        shards = []
        for j in range(N_DEV):
            a, b = j * cs, (j + 1) * cs
            pieces = []
            for i in range(len(self.sizes)):
                lo, hi = max(a, self.offs[i]), min(b, self.offs[i] + self.sizes[i])
                if lo < hi:
                    pieces.append(wp[:, pcol[i] + lo - self.offs[i]:pcol[i] + hi - self.offs[i]])
            shards.append(jnp.concatenate(pieces, axis=1))
        return jnp.stack(shards)

    def restore_w(self, wp):
        pieces = {}
        for idx in self.big:
            pieces[idx] = wp[:, self.cols[idx]:self.cols[idx] + self.sizes[idx]]
        o = self.small_col
        for idx in self.small:
            pieces[idx] = wp[:, o:o + self.sizes[idx]]
            o += self.sizes[idx]
        return jnp.concatenate([pieces[i] for i in range(len(self.sizes))], axis=1)


def _rows_form(cols_t, nh, nc):
    return cols_t.T.reshape(nh, nc, 1, CHUNK)


def _layer_fwd(x, p, lay, tag, late=None):
    s, d = x.shape
    nc = s // CHUNK
    w = d
    dnh, smh = lay.dn_heads, lay.ssm_heads
    r = smh // SSM_GROUPS
    cb = {k: v // LANES for k, v in lay.cols.items()}
    sv = {}
    h1 = _rms_fwd(x, p["norm_mix"], name=f"rms_mix_{tag}")
    proj = _matmul(h1, p["w_in"], name=f"mm_in_{tag}")
    small = proj[:, lay.small_col:lay.small_col + LANES]
    a_rows = _rows_form(small[:, 0:dnh], dnh, nc)
    b_rows = _rows_form(small[:, dnh:2 * dnh], dnh, nc)
    dt_rows = small[:, 2 * dnh:2 * dnh + smh].T.reshape(SSM_GROUPS, r, nc, CHUNK).transpose(0, 2, 1, 3)
    zero_b = jnp.zeros((1, 3 * w), F32)
    dn_qkv = _conv_fwd(proj, cb[0], p["dn_conv_w"], zero_b, 2 * dnh, name=f"dn_conv_{tag}")
    dn_alog = p["dn_a_log"].reshape(dnh, 1, 1)
    dn_dtb = p["dn_dt_bias"].reshape(dnh, 1, 1)
    o_dn, dn_states = _dn_fwd(dn_qkv, a_rows, b_rows, dn_alog, dn_dtb, name=f"dn_chunk_{tag}")
    y_dn = _dn_post_fwd(o_dn, proj, cb[1], p["dn_norm_w"], name=f"dn_post_{tag}")
    o_sb, sb_r = _sb_fwd(proj, cb[4], w, name=f"sb_{tag}")
    xbc = _conv_fwd(proj, cb[6], p["ssm_conv_w"], p["ssm_conv_b"].reshape(1, -1), 0, name=f"ssm_conv_{tag}")
    ssm_alog = p["ssm_a_log"].reshape(SSM_GROUPS, r, 1)
    ssm_dtb = p["ssm_dt_bias"].reshape(SSM_GROUPS, r, 1)
    y_ssd, ssm_states = _ssd_fwd(xbc, dt_rows, ssm_alog, ssm_dtb, name=f"ssd_{tag}")
    dexp = jnp.repeat(p["ssm_d"], SSM_HEAD_DIM)
    y_ssm = _ssm_post_fwd(y_ssd, xbc, proj, cb[5], dexp, p["ssm_norm_w"], name=f"ssm_post_{tag}")
    if late is not None:
        p.update(late(y_ssm))
    branches = (y_dn, o_sb, y_ssm)
    proj3 = jnp.concatenate(
        [_matmul(br, p["w_branch"][i], name=f"mm_branch{i}_{tag}") for i, br in enumerate(branches)], axis=1)
    merged = _merge_fwd(proj3, proj, cb[8], d, name=f"merge_{tag}")
    x1 = _matmul(merged, p["w_out"], name=f"mm_out_{tag}", epilogue=lambda acc, res: (acc + res,), extras=(x,))
    h2 = _rms_fwd(x1, p["norm_mlp"], name=f"rms_mlp_{tag}")
    u, act = _matmul(h2, p["w_up"], name=f"mm_up_{tag}", out_dtypes=(F32, MXU_DTYPE),
                     epilogue=lambda acc: (acc, jnp.square(jnp.maximum(acc, 0.0))))
    x2 = _matmul(act, p["w_down"], name=f"mm_down_{tag}", epilogue=lambda acc, res: (acc + res,), extras=(x1,))
    sv.update(x=x, h1=h1, proj=proj, a_rows=a_rows, b_rows=b_rows, dt_rows=dt_rows, dn_qkv=dn_qkv, dn_alog=dn_alog,
              dn_dtb=dn_dtb, o_dn=o_dn, dn_states=dn_states, y_dn=y_dn, o_sb=o_sb, sb_r=sb_r, xbc=xbc, ssm_alog=ssm_alog,
              ssm_dtb=ssm_dtb, y_ssd=y_ssd, ssm_states=ssm_states, dexp=dexp, y_ssm=y_ssm, proj3=proj3, merged=merged,
              x1=x1, h2=h2, u=u, act=act)
    return x2, sv


def _layer_bwd(dx2, p, sv, lay, tag):
    x = sv["x"]
    s, d = x.shape
    nc = s // CHUNK
    w = d
    dnh, smh = lay.dn_heads, lay.ssm_heads
    r = smh // SSM_GROUPS
    gn = SSM_GROUPS * SSM_STATE
    cb = {k: v // LANES for k, v in lay.cols.items()}
    proj = sv["proj"]
    g = {}
    dx2_b = dx2.astype(MXU_DTYPE)
    du = _matmul(dx2_b, p["w_down"], tb=True, name=f"mm_down_dx_{tag}", out_dtypes=(MXU_DTYPE,),
                 epilogue=lambda acc, uu: (acc * (2.0 * jnp.maximum(uu, 0.0)),), extras=(sv["u"],))
    g["w_down"] = _matmul(sv["act"], dx2_b, ta=True, name=f"mm_down_dw_{tag}", out_dtypes=(BF16,)).reshape(N_DEV, -1, d)
    g["w_up"] = _matmul(sv["h2"], du, ta=True, name=f"mm_up_dw_{tag}", out_dtypes=(BF16,), col_shards=N_DEV)
    dh2 = _matmul(du, p["w_up"], tb=True, name=f"mm_up_dx_{tag}")
    dx1, g["norm_mlp"] = _rms_bwd(sv["x1"], p["norm_mlp"], dh2, dx2, name=f"rms_mlp_bwd_{tag}")
    dx1_b = dx1.astype(MXU_DTYPE)
    dmerged = _matmul(dx1_b, p["w_out"], tb=True, name=f"mm_out_dx_{tag}")
    g["w_out"] = _matmul(sv["merged"], dx1_b, ta=True, name=f"mm_out_dw_{tag}", out_dtypes=(BF16,)).reshape(N_DEV, -1, d)
    dproj3, dgates = _merge_bwd(sv["proj3"], proj, cb[8], d, dmerged, name=f"merge_bwd_{tag}")
    branches = (sv["y_dn"], sv["o_sb"], sv["y_ssm"])
    dwb, dbr = [], []
    for i, br in enumerate(branches):
        dp_i = dproj3[:, i * d:(i + 1) * d]
        dwb.append(_matmul(br, dp_i, ta=True, name=f"mm_branch{i}_dw_{tag}", out_dtypes=(BF16,)).reshape(N_DEV, -1, d))
        dbr.append(_matmul(dp_i, p["w_branch"][i], tb=True, name=f"mm_branch{i}_dx_{tag}"))
    g["w_branch"] = jnp.stack(dwb, axis=1)
    dy_dn, do_sb, dy_ssm = dbr
    dy_ssd, dxs_skip, dz, ddexp, g["ssm_norm_w"] = _ssm_post_bwd(
        sv["y_ssd"], sv["xbc"], proj, cb[5], sv["dexp"], p["ssm_norm_w"], dy_ssm, name=f"ssm_post_bwd_{tag}")
    g["ssm_d"] = ddexp.reshape(smh, SSM_HEAD_DIM).sum(axis=1)
    dxs, dbm, dcm, ddt_rows, dalog, ddtb = _ssd_bwd(
        sv["xbc"], sv["dt_rows"], sv["ssm_alog"], sv["ssm_dtb"], sv["ssm_states"], dy_ssd, name=f"ssd_bwd_{tag}")
    g["ssm_a_log"] = dalog.reshape(smh)
    g["ssm_dt_bias"] = ddtb.reshape(smh)
    dxbc_post = jnp.concatenate([dxs + dxs_skip, dbm, dcm], axis=1)
    dxbc, g["ssm_conv_w"], dcb = _conv_bwd(proj, cb[6], p["ssm_conv_w"], p["ssm_conv_b"].reshape(1, -1), 0, dxbc_post,
                                           name=f"ssm_conv_bwd_{tag}")
    g["ssm_conv_b"] = dcb.reshape(-1)
    ddt = ddt_rows.transpose(0, 2, 1, 3).reshape(smh, s).T
    dq_sb, dk_sb, dv_sb = _sb_bwd(proj, cb[4], w, sv["sb_r"], do_sb, name=f"sb_bwd_{tag}")
    do_dn, dgate_dn, g["dn_norm_w"] = _dn_post_bwd(sv["o_dn"], proj, cb[1], p["dn_norm_w"], dy_dn, name=f"dn_post_bwd_{tag}")
    dq, dk, dv, da_rows, db_rows, dal, ddtb_dn = _dn_bwd(
        sv["dn_qkv"], sv["a_rows"], sv["b_rows"], sv["dn_alog"], sv["dn_dtb"], sv["dn_states"], do_dn, name=f"dn_chunk_bwd_{tag}")
    g["dn_a_log"] = dal.reshape(dnh)
    g["dn_dt_bias"] = ddtb_dn.reshape(dnh)
    zero_b = jnp.zeros((1, 3 * w), F32)
    ddn_qkv, g["dn_conv_w"], _ = _conv_bwd(proj, cb[0], p["dn_conv_w"], zero_b, 2 * dnh,
                                           jnp.concatenate([dq, dk, dv], axis=1), name=f"dn_conv_bwd_{tag}")
    da = da_rows.reshape(dnh, s).T
    db = db_rows.reshape(dnh, s).T
    dsmall = jnp.concatenate([da, db, ddt, jnp.zeros((s, LANES - lay.n_small), F32)], axis=1).astype(MXU_DTYPE)
    dproj = jnp.concatenate(
        [ddn_qkv, dgate_dn, dq_sb.astype(MXU_DTYPE), dk_sb.astype(MXU_DTYPE), dv_sb.astype(MXU_DTYPE), dz, dxbc, dgates, dsmall],
        axis=1)
    g["w_in"] = lay.to_shards(_matmul(sv["h1"], dproj, ta=True, name=f"mm_in_dw_{tag}", out_dtypes=(BF16,)))
    dh1 = _matmul(dproj, p["w_in"], tb=True, name=f"mm_in_dx_{tag}")
    dx0, g["norm_mix"] = _rms_bwd(x, p["norm_mix"], dh1, dx1, name=f"rms_mix_bwd_{tag}")
    return dx0, g


BIG = ("w_in", "w_branch", "w_out", "w_up", "w_down")
CONV = ("dn_conv_w", "ssm_conv_w")
SMALL = ("norm_mix", "dn_conv_w", "dn_a_log", "dn_dt_bias", "dn_norm_w", "ssm_conv_w", "ssm_conv_b", "ssm_a_log",
         "ssm_dt_bias", "ssm_d", "ssm_norm_w", "norm_mlp", "norm_final")
WEIGHTS = ("norm_mix", "w_in", "dn_conv_w", "dn_a_log", "dn_dt_bias", "dn_norm_w", "ssm_conv_w", "ssm_conv_b", "ssm_a_log",
           "ssm_dt_bias", "ssm_d", "ssm_norm_w", "w_branch", "w_out", "norm_mlp", "w_up", "w_down", "norm_final")
SHARD_AXIS = {"w_in": 2, "dn_conv_w": 2, "ssm_conv_w": 2, "w_branch": 2, "w_out": 1, "w_up": 2, "w_down": 1}


def _to_shards(full, axis):
    shp = full.shape
    n = shp[axis] // N_DEV
    t = full.reshape(shp[:axis] + (N_DEV, n) + shp[axis + 1:])
    return jnp.moveaxis(t, axis, 0)


def _from_shards(parts, axis):
    t = jnp.moveaxis(parts, 0, axis)
    shp = t.shape
    return t.reshape(shp[:axis] + (shp[axis] * shp[axis + 1],) + shp[axis + 2:])


def _unshard(parts, axis, *, name):
    shard = parts.shape[1:]
    nd = len(shard)
    if axis == 0:
        return parts.reshape((N_DEV * shard[0],) + shard[1:])

    def copy_block(i_ref, o_ref):
        o_ref[...] = i_ref[...]

    if axis == nd - 1:
        rows, n = math.prod(shard[:-1]), shard[-1]
        out = pl.pallas_call(
            copy_block, grid=(N_DEV,),
            in_specs=[pl.BlockSpec((None, rows, n), lambda j: (j, 0, 0))],
            out_specs=pl.BlockSpec((rows, n), lambda j: (0, j)),
            out_shape=jax.ShapeDtypeStruct((rows, N_DEV * n), parts.dtype),
            compiler_params=_cparams(1), name=name,
        )(parts.reshape(N_DEV, rows, n))
        return out.reshape(shard[:-1] + (N_DEV * n,))
    assert axis == nd - 2, (parts.shape, axis)
    a, n, c = math.prod(shard[:-2]), shard[-2], shard[-1]
    out = pl.pallas_call(
        copy_block, grid=(N_DEV, a),
        in_specs=[pl.BlockSpec((None, None, n, c), lambda j, i: (j, i, 0, 0))],
        out_specs=pl.BlockSpec((None, n, c), lambda j, i: (i, j, 0)),
        out_shape=jax.ShapeDtypeStruct((a, N_DEV * n, c), parts.dtype),
        compiler_params=_cparams(2), name=name,
    )(parts.reshape(N_DEV, a, n, c))
    return out.reshape(shard[:-2] + (N_DEV * n, c))


def _step(w, m, v, x, target):
    s, d = x.shape
    lay = _Layout(d)
    me = 4 * lax.axis_index("x") + 2 * lax.axis_index("y") + lax.axis_index("c")

    def shard(n, l):
        return w[n][l].astype(BF16) if n in BIG else w[n][l]

    def empty_land(a):
        return lax.empty((N_DEV,) + a.shape, a.dtype)

    def with_own(land, own):
        return lax.dynamic_update_index_in_dim(land, own, me, 0)

    def assemble(n, parts, l):
        return lay.from_shards(parts) if n == "w_in" else _unshard(parts, SHARD_AXIS[n] - 1, name=f"unshard_{n}_l{l}")

    small_names = tuple(n for n in WEIGHTS if n not in BIG + CONV + ("norm_final",))

    first, rest = ("w_in",) + CONV, BIG[1:]
    got = _all_gather([shard(n, 0) for n in first], name="gather_l0_first")
    p0 = {n: w[n][0] for n in small_names}
    p0.update({n: assemble(n, g, 0) for n, g in zip(first, got)})
    whole = (True, None)
    names_a, names_b = rest, BIG + CONV
    srcs_a, srcs_b = [shard(n, 0) for n in names_a], [shard(n, 1) for n in names_b]
    sem_sa, sem_ra, srcs_a, lands_a, x = _split_start(
        srcs_a, [empty_land(a) for a in srcs_a], [whole] * len(srcs_a), x, name="gather_l0_rest_start")
    sem_sb, sem_rb, srcs_b, lands_b, x = _split_start(
        srcs_b, [empty_land(a) for a in srcs_b], [whole] * len(srcs_b), x, name="gather_l1_start")

    def late_l0(after):
        lands = _split_wait(sem_sa, sem_ra, srcs_a, lands_a, [whole] * len(srcs_a), after, name="gather_l0_rest_wait")
        return {n: assemble(n, with_own(ld, s_), 0) for n, ld, s_ in zip(names_a, lands, srcs_a)}

    h, sv0 = _layer_fwd(x, p0, lay, "l0", late=late_l0)
    lands = _split_wait(sem_sb, sem_rb, srcs_b, lands_b, [whole] * len(srcs_b), h, name="gather_l1_wait")
    p1 = {n: w[n][1] for n in small_names}
    p1.update({n: assemble(n, with_own(ld, s_), 1) for n, ld, s_ in zip(names_b, lands, srcs_b)})
    h, sv1 = _layer_fwd(h, p1, lay, "l1")
    loss, dh, g_norm_final = _final_loss(h, w["norm_final"], target, name="final_loss")
    grads = [None] * DEPTH
    dh, grads[1] = _layer_bwd(dh, p1, sv1, lay, "l1")
    slots1, slots0 = [(False, 1)] * len(BIG), [(False, 0)] * len(BIG) + [whole]
    srcs1 = [grads[1][n] for n in BIG]
    sem_s1, sem_r1, srcs1, lands_g, dh = _split_start(
        srcs1, [lax.empty((N_DEV, DEPTH) + a.shape[1:], a.dtype) for a in srcs1], slots1, dh, name="grad_l1_start")
    grad_x, grads[0] = _layer_bwd(dh, p0, sv0, lay, "l0")
    lands_g = _split_wait(sem_s1, sem_r1, srcs1, lands_g, slots1, grad_x, name="grad_l1_wait")
    gfull = {n: jnp.stack([grads[l][n] for l in range(DEPTH)]) for n in SMALL if n != "norm_final"}
    gfull["norm_final"] = g_norm_final

    small_send = _pack([gfull[n] for n in SMALL] + [loss.reshape(1)], F32)
    srcs0 = [grads[0][n] for n in BIG] + [small_send]
    sem_s0, sem_r0, srcs0, lands0, grad_x = _split_start(
        srcs0, list(lands_g) + [empty_land(small_send)], slots0, grad_x, name="grad_l0_start")
    lands0 = _split_wait(sem_s0, sem_r0, srcs0, lands0, slots0, grad_x, name="grad_l0_wait")
    big_recv = []
    for i, n in enumerate(BIG):
        own = jnp.stack([lax.dynamic_index_in_dim(g_[i], me, 0, keepdims=False) for g_ in (srcs0, srcs1)])
        big_recv.append(with_own(lands0[i], own))
    small_recv = with_own(lands0[-1], srcs0[-1])

    out = {"grad": {}, "delta": {}, "new_m": {}, "new_v": {}}
    for n, parts in zip(BIG, big_recv):
        res = _sum_adamw(parts, w[n], m[n], v[n], name=f"sum_adamw_{n}")
        for key, a in zip(("grad", "delta", "new_m", "new_v"), res):
            out[key][n] = a
    small_sum = _sum_parts(small_recv, name="sum_small")
    small_full = _unpack(small_sum, [gfull[n].shape for n in SMALL] + [(1,)])
    loss_total = small_full[-1][0]
    gsmall = {}
    for n, a in zip(SMALL, small_full[:-1]):
        if n in SHARD_AXIS:
            a = lax.dynamic_index_in_dim(_to_shards(a, SHARD_AXIS[n]), me, axis=0, keepdims=False)
        gsmall[n] = a
    small_shapes = [w[n].shape for n in SMALL]
    ws, gs, ms, vs = (_pack([t[n] for n in SMALL], F32) for t in (w, gsmall, m, v))
    ds, m1s, v1s = _adamw(ws, gs, ms, vs, name="adamw_small")
    for n in SMALL:
        out["grad"][n] = gsmall[n]
    for key, packed in (("delta", ds), ("new_m", m1s), ("new_v", v1s)):
        for n, a in zip(SMALL, _unpack(packed, small_shapes)):
            out[key][n] = a
    return loss_total, grad_x, out


def kernel(x, norm_mix, w_in, dn_conv_w, dn_a_log, dn_dt_bias, dn_norm_w, ssm_conv_w, ssm_conv_b, ssm_a_log, ssm_dt_bias, ssm_d, ssm_norm_w, w_branch, w_out, norm_mlp, w_up, w_down, norm_final, loss_target, m_norm_mix, m_w_in, m_dn_conv_w, m_dn_a_log, m_dn_dt_bias, m_dn_norm_w, m_ssm_conv_w, m_ssm_conv_b, m_ssm_a_log, m_ssm_dt_bias, m_ssm_d, m_ssm_norm_w, m_w_branch, m_w_out, m_norm_mlp, m_w_up, m_w_down, m_norm_final, v_norm_mix, v_w_in, v_dn_conv_w, v_dn_a_log, v_dn_dt_bias, v_dn_norm_w, v_ssm_conv_w, v_ssm_conv_b, v_ssm_a_log, v_ssm_dt_bias, v_ssm_d, v_ssm_norm_w, v_w_branch, v_w_out, v_norm_mlp, v_w_up, v_w_down, v_norm_final):
    w = dict(norm_mix=norm_mix, w_in=w_in, dn_conv_w=dn_conv_w, dn_a_log=dn_a_log, dn_dt_bias=dn_dt_bias, dn_norm_w=dn_norm_w,
             ssm_conv_w=ssm_conv_w, ssm_conv_b=ssm_conv_b, ssm_a_log=ssm_a_log, ssm_dt_bias=ssm_dt_bias, ssm_d=ssm_d,
             ssm_norm_w=ssm_norm_w, w_branch=w_branch, w_out=w_out, norm_mlp=norm_mlp, w_up=w_up, w_down=w_down,
             norm_final=norm_final)
    m = dict(norm_mix=m_norm_mix, w_in=m_w_in, dn_conv_w=m_dn_conv_w, dn_a_log=m_dn_a_log, dn_dt_bias=m_dn_dt_bias,
             dn_norm_w=m_dn_norm_w, ssm_conv_w=m_ssm_conv_w, ssm_conv_b=m_ssm_conv_b, ssm_a_log=m_ssm_a_log,
             ssm_dt_bias=m_ssm_dt_bias, ssm_d=m_ssm_d, ssm_norm_w=m_ssm_norm_w, w_branch=m_w_branch, w_out=m_w_out,
             norm_mlp=m_norm_mlp, w_up=m_w_up, w_down=m_w_down, norm_final=m_norm_final)
    v = dict(norm_mix=v_norm_mix, w_in=v_w_in, dn_conv_w=v_dn_conv_w, dn_a_log=v_dn_a_log, dn_dt_bias=v_dn_dt_bias,
             dn_norm_w=v_dn_norm_w, ssm_conv_w=v_ssm_conv_w, ssm_conv_b=v_ssm_conv_b, ssm_a_log=v_ssm_a_log,
             ssm_dt_bias=v_ssm_dt_bias, ssm_d=v_ssm_d, ssm_norm_w=v_ssm_norm_w, w_branch=v_w_branch, w_out=v_w_out,
             norm_mlp=v_norm_mlp, w_up=v_w_up, w_down=v_w_down, norm_final=v_norm_final)
    loss, grad_x, out = _step(w, m, v, x[0], loss_target[0])
    return (loss, grad_x[None], *[out["grad"][n] for n in WEIGHTS], *[out["delta"][n] for n in WEIGHTS],
            *[out["new_m"][n] for n in WEIGHTS], *[out["new_v"][n] for n in WEIGHTS])
```

```python
import functools
import math

import jax
import jax.numpy as jnp
from jax import lax
from jax.experimental import pallas as pl
from jax.experimental.pallas import tpu as pltpu

F32 = jnp.float32
BF16 = jnp.bfloat16
MXU_DTYPE = BF16
HIGHEST = lax.Precision.HIGHEST

N_DEV = 8
DEPTH = 2
EPS = 1e-6
CONV_K = 4
DN_HEAD_DIM = 128
SB_HEAD_DIM = 64
SSM_HEAD_DIM = 64
SSM_STATE = 128
SSM_GROUPS = 4
CHUNK = 64
SB_BLOCK = 128
LANES = 128
ADAM_LR, ADAM_B1, ADAM_B2, ADAM_EPS, ADAM_WD, ADAM_STEP = 0.001, 0.9, 0.999, 1e-08, 0.01, 10
NEG_BIG = -1e30
DN_HEADS_PER_STEP = 8
SB_UNROLL = 4
SB_SPLIT = 2
CHUNK_PREC = lax.Precision.HIGH

ARB = "arbitrary"


def _cparams(n_axes):
    return pltpu.CompilerParams(dimension_semantics=(ARB,) * n_axes)


def _softplus(x):
    return jnp.maximum(x, 0.0) + jnp.log1p(jnp.exp(-jnp.abs(x)))


def _sigmoid(x):
    return 1.0 / (1.0 + jnp.exp(-x))


def _silu(x):
    return x * _sigmoid(x)


def _silu_grad(x):
    s = _sigmoid(x)
    return s * (1.0 + x * (1.0 - s))


def _dot(a, b, dims, prec=None):
    return lax.dot_general(a, b, (dims, ((), ())), precision=prec, preferred_element_type=F32)


NN = ((1,), (0,))
NT = ((1,), (1,))
TN = ((0,), (0,))


def _hdot(a, b, dims=NN):
    return _dot(a, b, dims, CHUNK_PREC)


def _bdot(a, b, dims=NN):
    return _dot(a.astype(MXU_DTYPE), b.astype(MXU_DTYPE), dims)


def _split_dot(a, m_bf16, nsplit=3):
    out = None
    rem = a
    for _ in range(nsplit):
        piece = rem.astype(BF16)
        rem = rem - piece.astype(F32)
        term = _dot(piece, m_bf16, NN)
        out = term if out is None else out + term
    return out


def _pick(n, pref):
    for t in pref:
        if n % t == 0:
            return t
    return n


def _matmul(a, b, *, ta=False, tb=False, name, epilogue=None, extras=(), out_dtypes=(F32,), col_shards=1,
            tm=None, tn=None, tk=None):
    m, k = (a.shape[1], a.shape[0]) if ta else a.shape
    k2, n = (b.shape[1], b.shape[0]) if tb else b.shape
    assert k == k2, (a.shape, b.shape, ta, tb)
    ncs = n // col_shards
    tm = tm or _pick(m, (512, 256, 128))
    tn = tn or _pick(ncs, (1024, 640, 512, 384, 256, 128))
    tk = tk or _pick(k, (1920, 1024, 640, 512, 256, 128))
    nk = k // tk
    a_spec = pl.BlockSpec((tk, tm), lambda i, j, kk: (kk, i)) if ta else pl.BlockSpec((tm, tk), lambda i, j, kk: (i, kk))
    b_spec = pl.BlockSpec((tn, tk), lambda i, j, kk: (j, kk)) if tb else pl.BlockSpec((tk, tn), lambda i, j, kk: (kk, j))
    e_spec = pl.BlockSpec((tm, tn), lambda i, j, kk: (i, j))
    if col_shards == 1:
        o_spec, o_shape = e_spec, (m, n)
    else:
        per = ncs // tn
        o_spec, o_shape = pl.BlockSpec((None, tm, tn), lambda i, j, kk: (j // per, i, j % per)), (col_shards, m, ncs)
    dims = (((0,) if ta else (1,)), ((1,) if tb else (0,)))
    n_extra = len(extras)
    n_out = len(out_dtypes)

    def body(*refs):
        a_ref, b_ref = refs[0], refs[1]
        extra_refs = refs[2:2 + n_extra]
        out_refs = refs[2 + n_extra:2 + n_extra + n_out]
        acc_ref = refs[-1]
        kk = pl.program_id(2)

        @pl.when(kk == 0)
        def _():
            acc_ref[...] = jnp.zeros_like(acc_ref)

        acc_ref[...] += _dot(a_ref[...].astype(MXU_DTYPE), b_ref[...].astype(MXU_DTYPE), dims)

        @pl.when(kk == nk - 1)
        def _():
            acc = acc_ref[...]
            outs = (acc,) if epilogue is None else epilogue(acc, *[r[...] for r in extra_refs])
            for o_ref, o in zip(out_refs, outs):
                o_ref[...] = o.astype(o_ref.dtype)

    outs = pl.pallas_call(
        body,
        grid=(m // tm, n // tn, nk),
        in_specs=[a_spec, b_spec] + [e_spec] * n_extra,
        out_specs=[o_spec] * n_out,
        out_shape=[jax.ShapeDtypeStruct(o_shape, dt) for dt in out_dtypes],
        scratch_shapes=[pltpu.VMEM((tm, tn), F32)],
        compiler_params=pltpu.CompilerParams(dimension_semantics=("parallel", "parallel", ARB)),
        name=name,
    )(a, b, *extras)
    return outs[0] if n_out == 1 else tuple(outs)


def _rms_fwd(x, w, *, name, tm=256):
    s, d = x.shape
    out_dtype = MXU_DTYPE

    def body(x_ref, w_ref, o_ref):
        xv = x_ref[...]
        r = lax.rsqrt(jnp.mean(xv * xv, axis=-1, keepdims=True) + EPS)
        o_ref[...] = (xv * r * w_ref[...]).astype(o_ref.dtype)

    return pl.pallas_call(
        body, grid=(s // tm,),
        in_specs=[pl.BlockSpec((tm, d), lambda i: (i, 0)), pl.BlockSpec((1, d), lambda i: (0, 0))],
        out_specs=pl.BlockSpec((tm, d), lambda i: (i, 0)),
        out_shape=jax.ShapeDtypeStruct((s, d), out_dtype),
        compiler_params=_cparams(1), name=name,
    )(x, w.reshape(1, d))


def _rms_bwd(x, w, dh, dres, *, name, tm=256):
    s, d = x.shape

    def body(x_ref, w_ref, dh_ref, dres_ref, dx_ref, dw_ref):
        xv = x_ref[...]
        r = lax.rsqrt(jnp.mean(xv * xv, axis=-1, keepdims=True) + EPS)
        xh = xv * r
        dhv = dh_ref[...].astype(F32)
        dxn = dhv * w_ref[...]
        dx = r * (dxn - xh * jnp.mean(dxn * xh, axis=-1, keepdims=True))
        dx_ref[...] = dres_ref[...] + dx

        @pl.when(pl.program_id(0) == 0)
        def _():
            dw_ref[...] = jnp.zeros_like(dw_ref)

        dw_ref[...] += jnp.sum(dhv * xh, axis=0, keepdims=True)

    dx, dw = pl.pallas_call(
        body, grid=(s // tm,),
        in_specs=[pl.BlockSpec((tm, d), lambda i: (i, 0)), pl.BlockSpec((1, d), lambda i: (0, 0)),
                  pl.BlockSpec((tm, d), lambda i: (i, 0)), pl.BlockSpec((tm, d), lambda i: (i, 0))],
        out_specs=[pl.BlockSpec((tm, d), lambda i: (i, 0)), pl.BlockSpec((1, d), lambda i: (0, 0))],
        out_shape=[jax.ShapeDtypeStruct((s, d), F32), jax.ShapeDtypeStruct((1, d), F32)],
        compiler_params=_cparams(1), name=name,
    )(x, w.reshape(1, d), dh, dres)
    return dx, dw.reshape(d)


def _final_loss(x, w, target, *, name, tm=256):
    s, d = x.shape

    def body(x_ref, w_ref, t_ref, loss_ref, dx_ref, dw_ref):
        xv = x_ref[...]
        r = lax.rsqrt(jnp.mean(xv * xv, axis=-1, keepdims=True) + EPS)
        xh = xv * r
        err = xh * w_ref[...] - t_ref[...]
        dy = err * (1.0 / d)
        dxn = dy * w_ref[...]
        dx_ref[...] = r * (dxn - xh * jnp.mean(dxn * xh, axis=-1, keepdims=True))

        @pl.when(pl.program_id(0) == 0)
        def _():
            dw_ref[...] = jnp.zeros_like(dw_ref)
            loss_ref[...] = jnp.zeros_like(loss_ref)

        dw_ref[...] += jnp.sum(dy * xh, axis=0, keepdims=True)
        row = jnp.sum(err * err, axis=1, keepdims=True) * (0.5 / d)
        loss_ref[...] += jnp.sum(row, axis=0, keepdims=True)

    loss, dx, dw = pl.pallas_call(
        body, grid=(s // tm,),
        in_specs=[pl.BlockSpec((tm, d), lambda i: (i, 0)), pl.BlockSpec((1, d), lambda i: (0, 0)),
                  pl.BlockSpec((tm, d), lambda i: (i, 0))],
        out_specs=[pl.BlockSpec((1, 1), lambda i: (0, 0)), pl.BlockSpec((tm, d), lambda i: (i, 0)),
                   pl.BlockSpec((1, d), lambda i: (0, 0))],
        out_shape=[jax.ShapeDtypeStruct((1, 1), F32), jax.ShapeDtypeStruct((s, d), F32), jax.ShapeDtypeStruct((1, d), F32)],
        compiler_params=_cparams(1), name=name,
    )(x, w.reshape(1, d), target)
    return loss[0, 0], dx, dw.reshape(d)


def _shift_down(x, sh, t_idx):
    return jnp.where(t_idx >= sh, pltpu.roll(x, sh, 0), 0.0)


def _shift_up(x, sh, t_idx, s):
    return jnp.where(t_idx < s - sh, pltpu.roll(x, s - sh, 0), 0.0)


def _conv_pre(x, w_rows, b, t_idx):
    c = w_rows[CONV_K - 1] * x + b
    for sh in range(1, CONV_K):
        c = c + w_rows[CONV_K - 1 - sh] * _shift_down(x, sh, t_idx)
    return c


def _conv_fwd(src, col0, w, b, n_l2, *, name):
    s = src.shape[0]
    c_tot = w.shape[1]
    nblk = c_tot // LANES

    def body(x_ref, w_ref, b_ref, o_ref):
        j = pl.program_id(0)
        t_idx = lax.broadcasted_iota(jnp.int32, (s, LANES), 0)
        w_rows = [w_ref[kk:kk + 1, :] for kk in range(CONV_K)]
        y = _silu(_conv_pre(x_ref[...], w_rows, b_ref[...], t_idx))
        if n_l2 > 0:
            yn = y * lax.rsqrt(jnp.sum(y * y, axis=1, keepdims=True) + EPS)
            y = jnp.where(j < n_l2, yn, y)
        o_ref[...] = y

    return pl.pallas_call(
        body, grid=(nblk,),
        in_specs=[pl.BlockSpec((s, LANES), lambda j: (0, col0 + j)), pl.BlockSpec((CONV_K, LANES), lambda j: (0, j)),
                  pl.BlockSpec((1, LANES), lambda j: (0, j))],
        out_specs=pl.BlockSpec((s, LANES), lambda j: (0, j)),
        out_shape=jax.ShapeDtypeStruct((s, c_tot), F32),
        compiler_params=_cparams(1), name=name,
    )(src, w, b)


def _conv_bwd(src, col0, w, b, n_l2, dout, *, name):
    s = src.shape[0]
    c_tot = w.shape[1]
    nblk = c_tot // LANES

    def body(x_ref, w_ref, b_ref, do_ref, dx_ref, dw_ref, db_ref):
        j = pl.program_id(0)
        t_idx = lax.broadcasted_iota(jnp.int32, (s, LANES), 0)
        xv = x_ref[...]
        w_rows = [w_ref[kk:kk + 1, :] for kk in range(CONV_K)]
        c = _conv_pre(xv, w_rows, b_ref[...], t_idx)
        dy = do_ref[...]
        if n_l2 > 0:
            y = _silu(c)
            r = lax.rsqrt(jnp.sum(y * y, axis=1, keepdims=True) + EPS)
            dyn = r * dy - y * (r * r * r) * jnp.sum(dy * y, axis=1, keepdims=True)
            dy = jnp.where(j < n_l2, dyn, dy)
        dc = dy * _silu_grad(c)
        dx = w_rows[CONV_K - 1] * dc
        rows = [None] * CONV_K
        rows[CONV_K - 1] = jnp.sum(dc * xv, axis=0, keepdims=True)
        for sh in range(1, CONV_K):
            dx = dx + w_rows[CONV_K - 1 - sh] * _shift_up(dc, sh, t_idx, s)
            rows[CONV_K - 1 - sh] = jnp.sum(dc * _shift_down(xv, sh, t_idx), axis=0, keepdims=True)
        dx_ref[...] = dx.astype(dx_ref.dtype)
        for kk in range(CONV_K):
            dw_ref[kk:kk + 1, :] = rows[kk]
        db_ref[...] = jnp.sum(dc, axis=0, keepdims=True)

    return pl.pallas_call(
        body, grid=(nblk,),
        in_specs=[pl.BlockSpec((s, LANES), lambda j: (0, col0 + j)), pl.BlockSpec((CONV_K, LANES), lambda j: (0, j)),
                  pl.BlockSpec((1, LANES), lambda j: (0, j)), pl.BlockSpec((s, LANES), lambda j: (0, j))],
        out_specs=[pl.BlockSpec((s, LANES), lambda j: (0, j)), pl.BlockSpec((CONV_K, LANES), lambda j: (0, j)),
                   pl.BlockSpec((1, LANES), lambda j: (0, j))],
        out_shape=[jax.ShapeDtypeStruct((s, c_tot), MXU_DTYPE), jax.ShapeDtypeStruct((CONV_K, c_tot), F32),
                   jax.ShapeDtypeStruct((1, c_tot), F32)],
        compiler_params=_cparams(1), name=name,
    )(src, w, b, dout)


def _chunk_masks(c):
    ii = lax.broadcasted_iota(jnp.int32, (c, c), 0)
    jj = lax.broadcasted_iota(jnp.int32, (c, c), 1)
    return ii, jj


def _row_to_col(row, eye):
    return jnp.sum(jnp.where(eye, row, 0.0), axis=1, keepdims=True)


def _each(f, *lists):
    return [f(*xs) for xs in zip(*lists)]


def _dn_chunk(q, k, v, a_row, b_row, alog, dtb, s0):
    c = q[0].shape[0]
    ii, jj = _chunk_masks(c)
    causal, strict, eye = ii >= jj, ii > jj, ii == jj
    g_row = _each(lambda al, a, dt: -jnp.exp(al) * _softplus(a + dt), alog, a_row, dtb)
    beta_col = _each(lambda b: _row_to_col(_sigmoid(b), eye), b_row)
    g_col = _each(lambda g: _row_to_col(g, eye), g_row)
    gc_col = _each(lambda g: jnp.sum(jnp.where(causal, g, 0.0), axis=1, keepdims=True), g_row)
    gc_row = _each(lambda g: jnp.sum(jnp.where(jj >= ii, g, 0.0), axis=0, keepdims=True), g_col)
    decay = _each(lambda gc, gr: jnp.exp(jnp.where(causal, gc - gr, NEG_BIG)), gc_col, gc_row)
    kb = _each(jnp.multiply, k, beta_col)
    vb = _each(jnp.multiply, v, beta_col)
    nmat = _each(lambda kb_, k_, dc: -jnp.where(strict, _hdot(kb_, k_, NT) * dc, 0.0), kb, k, decay)
    xinv = _each(lambda n: jnp.where(eye, 1.0, 0.0) + n, nmat)
    pw = nmat
    for _ in range(int(math.log2(c)) - 1):
        pw = _each(lambda p: _hdot(p, p), pw)
        xinv = _each(lambda x, p: x + _hdot(x, p), xinv, pw)
    egc = _each(jnp.exp, gc_col)
    u = _each(_hdot, xinv, vb)
    w = _each(lambda x, kb_, e: _hdot(x, kb_ * e), xinv, kb, egc)
    qs = _each(lambda q_: q_ * (q_.shape[1] ** -0.5), q)
    attn = _each(lambda q_, k_, dc: _hdot(q_, k_, NT) * dc, qs, k, decay)
    gl = _each(lambda g: jnp.sum(g, axis=1, keepdims=True), g_row)
    kd = _each(lambda k_, gl_, gc: k_ * jnp.exp(gl_ - gc), k, gl, gc_col)
    v_new = _each(lambda u_, w_, s: u_ - _hdot(w_, s), u, w, s0)
    o = _each(lambda q_, e, s, at, vn: _hdot(q_ * e, s) + _hdot(at, vn), qs, egc, s0, attn, v_new)
    s1 = _each(lambda s, gl_, kd_, vn: s * jnp.exp(gl_) + _hdot(kd_, vn, TN), s0, gl, kd, v_new)
    return o, s1


def _dn_specs(nh, nc, hb, rev):
    n_of = (lambda n: nc - 1 - n) if rev else (lambda n: n)
    ng = nh // hb
    qkv = [pl.BlockSpec((CHUNK, hb * DN_HEAD_DIM), (lambda h, n, o=o: (n_of(n), o * ng + h))) for o in range(3)]
    row = pl.BlockSpec((hb, None, 1, CHUNK), lambda h, n: (h, n_of(n), 0, 0))
    scal = pl.BlockSpec((hb, 1, 1), lambda h, n: (h, 0, 0))
    o_spec = pl.BlockSpec((CHUNK, hb * DN_HEAD_DIM), lambda h, n: (n_of(n), h))
    st = pl.BlockSpec((hb, None, DN_HEAD_DIM, DN_HEAD_DIM), lambda h, n: (h, n_of(n), 0, 0))
    return qkv, row, scal, o_spec, st


def _dn_fwd(qkv, a_rows, b_rows, alog, dtb, *, name):
    s = qkv.shape[0]
    nh, nc = a_rows.shape[0], a_rows.shape[1]
    hb = min(DN_HEADS_PER_STEP, nh)
    qkv_specs, row, scal, o_spec, st = _dn_specs(nh, nc, hb, False)
    hd = DN_HEAD_DIM

    def body(q_ref, k_ref, v_ref, a_ref, b_ref, al_ref, dt_ref, o_ref, st_ref, state):
        @pl.when(pl.program_id(1) == 0)
        def _():
            state[...] = jnp.zeros_like(state)

        cols = [slice(h * hd, (h + 1) * hd) for h in range(hb)]
        s0 = [state[h] for h in range(hb)]
        for h in range(hb):
            st_ref[h] = s0[h]
        o, s1 = _dn_chunk([q_ref[:, cl] for cl in cols], [k_ref[:, cl] for cl in cols], [v_ref[:, cl] for cl in cols],
                          [a_ref[h] for h in range(hb)], [b_ref[h] for h in range(hb)],
                          [al_ref[h] for h in range(hb)], [dt_ref[h] for h in range(hb)], s0)
        for h in range(hb):
            o_ref[:, cols[h]] = o[h]
            state[h] = s1[h]

    return pl.pallas_call(
        body, grid=(nh // hb, nc),
        in_specs=qkv_specs + [row, row, scal, scal],
        out_specs=[o_spec, st],
        out_shape=[jax.ShapeDtypeStruct((s, nh * hd), F32), jax.ShapeDtypeStruct((nh, nc, hd, hd), F32)],
        scratch_shapes=[pltpu.VMEM((hb, hd, hd), F32)],
        compiler_params=_cparams(2), name=name,
    )(qkv, qkv, qkv, a_rows, b_rows, alog, dtb)


def _dn_bwd(qkv, a_rows, b_rows, alog, dtb, states, do, *, name):
    s = qkv.shape[0]
    nh, nc = a_rows.shape[0], a_rows.shape[1]
    hb = min(DN_HEADS_PER_STEP, nh)
    qkv_specs, row, scal, o_spec, st = _dn_specs(nh, nc, hb, True)
    hd = DN_HEAD_DIM

    def body(q_ref, k_ref, v_ref, a_ref, b_ref, al_ref, dt_ref, st_ref, do_ref,
             dq_ref, dk_ref, dv_ref, da_ref, db_ref, dal_ref, ddt_ref, dstate):
        @pl.when(pl.program_id(1) == 0)
        def _():
            dstate[...] = jnp.zeros_like(dstate)
            dal_ref[...] = jnp.zeros_like(dal_ref)
            ddt_ref[...] = jnp.zeros_like(ddt_ref)

        cols = [slice(h * hd, (h + 1) * hd) for h in range(hb)]
        heads = range(hb)
        args = ([q_ref[:, cl] for cl in cols], [k_ref[:, cl] for cl in cols], [v_ref[:, cl] for cl in cols],
                [a_ref[h] for h in heads], [b_ref[h] for h in heads], [al_ref[h] for h in heads],
                [dt_ref[h] for h in heads], [st_ref[h] for h in heads])
        _, vjp = jax.vjp(_dn_chunk, *args)
        dq, dk, dv, da, db, dal, ddt, ds0 = vjp(([do_ref[:, cl] for cl in cols], [dstate[h] for h in heads]))
        for h in heads:
            dq_ref[:, cols[h]] = dq[h]
            dk_ref[:, cols[h]] = dk[h]
            dv_ref[:, cols[h]] = dv[h]
            da_ref[h] = da[h]
            db_ref[h] = db[h]
            dal_ref[h] += dal[h]
            ddt_ref[h] += ddt[h]
            dstate[h] = ds0[h]

    w = nh * hd
    outs = pl.pallas_call(
        body, grid=(nh // hb, nc),
        in_specs=qkv_specs + [row, row, scal, scal, st, o_spec],
        out_specs=[o_spec, o_spec, o_spec, row, row, scal, scal],
        out_shape=[jax.ShapeDtypeStruct((s, w), F32)] * 3
        + [jax.ShapeDtypeStruct(a_rows.shape, F32)] * 2 + [jax.ShapeDtypeStruct((nh, 1, 1), F32)] * 2,
        scratch_shapes=[pltpu.VMEM((hb, hd, hd), F32)],
        compiler_params=_cparams(2), name=name,
    )(qkv, qkv, qkv, a_rows, b_rows, alog, dtb, states, do)
    return outs


def _dn_post_fwd(o, src, gate_col0, nw, *, name, tm=256):
    s, w = o.shape
    nh = w // DN_HEAD_DIM

    def body(o_ref, g_ref, w_ref, y_ref):
        ov = o_ref[...]
        r = lax.rsqrt(jnp.mean(ov * ov, axis=-1, keepdims=True) + EPS)
        y_ref[...] = (ov * r * w_ref[...] * _silu(g_ref[...])).astype(y_ref.dtype)

    blk = pl.BlockSpec((tm, DN_HEAD_DIM), lambda i, h: (i, h))
    return pl.pallas_call(
        body, grid=(s // tm, nh),
        in_specs=[blk, pl.BlockSpec((tm, DN_HEAD_DIM), lambda i, h: (i, gate_col0 + h)),
                  pl.BlockSpec((1, DN_HEAD_DIM), lambda i, h: (0, 0))],
        out_specs=blk, out_shape=jax.ShapeDtypeStruct((s, w), MXU_DTYPE),
        compiler_params=_cparams(2), name=name,
    )(o, src, nw.reshape(1, DN_HEAD_DIM))


def _dn_post_bwd(o, src, gate_col0, nw, dy, *, name, tm=256):
    s, w = o.shape
    nh = w // DN_HEAD_DIM

    def body(o_ref, g_ref, w_ref, dy_ref, do_ref, dg_ref, dw_ref):
        ov = o_ref[...]
        gv = g_ref[...]
        dyv = dy_ref[...]
        r = lax.rsqrt(jnp.mean(ov * ov, axis=-1, keepdims=True) + EPS)
        oh = ov * r
        dn = dyv * _silu(gv)
        dg_ref[...] = (dyv * (oh * w_ref[...]) * _silu_grad(gv)).astype(dg_ref.dtype)
        don = dn * w_ref[...]
        do_ref[...] = r * (don - oh * jnp.mean(don * oh, axis=-1, keepdims=True))

        @pl.when((pl.program_id(0) == 0) & (pl.program_id(1) == 0))
        def _():
            dw_ref[...] = jnp.zeros_like(dw_ref)

        dw_ref[...] += jnp.sum(dn * oh, axis=0, keepdims=True)

    blk = pl.BlockSpec((tm, DN_HEAD_DIM), lambda i, h: (i, h))
    wspec = pl.BlockSpec((1, DN_HEAD_DIM), lambda i, h: (0, 0))
    do, dg, dw = pl.pallas_call(
        body, grid=(s // tm, nh),
        in_specs=[blk, pl.BlockSpec((tm, DN_HEAD_DIM), lambda i, h: (i, gate_col0 + h)), wspec, blk],
        out_specs=[blk, blk, wspec],
        out_shape=[jax.ShapeDtypeStruct((s, w), F32), jax.ShapeDtypeStruct((s, w), MXU_DTYPE),
                   jax.ShapeDtypeStruct((1, DN_HEAD_DIM), F32)],
        compiler_params=_cparams(2), name=name,
    )(o, src, nw.reshape(1, DN_HEAD_DIM), dy)
    return do, dg, dw.reshape(DN_HEAD_DIM)


def _sb_consts():
    r2 = lax.broadcasted_iota(jnp.int32, (2 * SB_BLOCK, SB_BLOCK), 0)
    c2 = lax.broadcasted_iota(jnp.int32, (2 * SB_BLOCK, SB_BLOCK), 1)
    r = lax.broadcasted_iota(jnp.int32, (SB_BLOCK, SB_BLOCK), 0)
    c = lax.broadcasted_iota(jnp.int32, (SB_BLOCK, SB_BLOCK), 1)
    lm0 = c < SB_HEAD_DIM
    m_gt = jnp.where(r > c, 1.0, 0.0).astype(BF16)
    m_lt = jnp.where(r < c, 1.0, 0.0).astype(BF16)
    return r2, c2, lm0, m_gt, m_lt


def _sb_stack(x, lm0):
    return jnp.concatenate([jnp.where(lm0, x, 0.0), jnp.where(lm0, 0.0, x)], axis=0)


def _sb_unstack(x2, lm0):
    return jnp.where(lm0, x2[:SB_BLOCK], x2[SB_BLOCK:])


def _sb_fwd(src, col0, width, *, name):
    s = src.shape[0]
    nq = s // SB_BLOCK
    npair = width // LANES
    scale = SB_HEAD_DIM ** -0.5
    nu = math.gcd(SB_UNROLL, nq)

    def body(q_ref, k_ref, v_ref, o_ref, r_ref):
        i = pl.program_id(1)
        r2, c2, lm0, m_gt, _ = _sb_consts()
        t_glob = i * SB_BLOCK + (r2 & (SB_BLOCK - 1))
        q2 = (_sb_stack(q_ref[...], lm0) * scale).astype(MXU_DTYPE)

        def group(base, carry, masked):
            o2, rsum = carry
            js = [base + nu - 1 - u for u in range(nu)]
            offs = [pl.multiple_of(j * SB_BLOCK, SB_BLOCK) for j in js]
            zs = [_dot(q2, k_ref[pl.ds(off, SB_BLOCK), :].astype(MXU_DTYPE), NT) for off in offs]
            ts = [jnp.log(1.0 + jnp.exp(-jnp.abs(z))) for z in zs]
            lks = [-(jnp.maximum(z, 0.0) + t) for z, t in zip(zs, ts)]
            if masked:
                masks = [(j * SB_BLOCK + c2) < t_glob for j in js]
                lks = [jnp.where(mk, lk, 0.0) for mk, lk in zip(masks, lks)]
            sufs = [_split_dot(lk, m_gt, SB_SPLIT) for lk in lks]
            rs = [rsum]
            for lk in lks:
                rs.append(rs[-1] + jnp.sum(lk, axis=1, keepdims=True))
            wgts = [jnp.exp((jnp.minimum(z, 0.0) - t) + r_ + sf) for z, t, r_, sf in zip(zs, ts, rs, sufs)]
            if masked:
                wgts = [jnp.where(mk, wg, 0.0) for mk, wg in zip(masks, wgts)]
            for off, wg in zip(offs, wgts):
                o2 = o2 + _dot(wg.astype(MXU_DTYPE), v_ref[pl.ds(off, SB_BLOCK), :].astype(MXU_DTYPE), NN)
            return o2, rs[-1]

        top0 = (i // nu) * nu
        carry = group(top0, (jnp.zeros((2 * SB_BLOCK, LANES), F32), jnp.zeros((2 * SB_BLOCK, 1), F32)), True)
        o2, rsum = lax.fori_loop(1, i // nu + 1, lambda g, cr: group(top0 - nu * g, cr, False), carry)
        o_ref[...] = _sb_unstack(o2, lm0)
        r_ref[...] = _sb_unstack(jnp.broadcast_to(rsum, (2 * SB_BLOCK, LANES)), lm0)

    blk = pl.BlockSpec((SB_BLOCK, LANES), lambda p, i: (i, p))
    return pl.pallas_call(
        body, grid=(npair, nq),
        in_specs=[pl.BlockSpec((SB_BLOCK, LANES), lambda p, i: (i, col0 + p)),
                  pl.BlockSpec((s, LANES), lambda p, i: (0, col0 + npair + p)),
                  pl.BlockSpec((s, LANES), lambda p, i: (0, col0 + 2 * npair + p))],
        out_specs=[blk, blk],
        out_shape=[jax.ShapeDtypeStruct((s, width), F32), jax.ShapeDtypeStruct((s, width), F32)],
        compiler_params=_cparams(2), name=name,
    )(src, src, src)


def _sb_bwd(src, col0, width, rtot, do, *, name):
    s = src.shape[0]
    nq = s // SB_BLOCK
    npair = width // LANES
    scale = SB_HEAD_DIM ** -0.5
    nu = math.gcd(SB_UNROLL, nq)

    def body(q_ref, k_ref, v_ref, r_ref, do_ref, dq_ref, dk_ref, dv_ref):
        i = pl.program_id(1)

        @pl.when(i == 0)
        def _():
            dk_ref[...] = jnp.zeros_like(dk_ref)
            dv_ref[...] = jnp.zeros_like(dv_ref)

        r2, c2, lm0, m_gt, m_lt = _sb_consts()
        t_glob = i * SB_BLOCK + (r2 & (SB_BLOCK - 1))
        q2 = (_sb_stack(q_ref[...], lm0) * scale).astype(MXU_DTYPE)
        do2 = _sb_stack(do_ref[...], lm0).astype(MXU_DTYPE)
        rv = r_ref[...]
        rt = jnp.concatenate([jnp.max(jnp.where(lm0, rv, NEG_BIG), axis=1, keepdims=True),
                              jnp.max(jnp.where(lm0, NEG_BIG, rv), axis=1, keepdims=True)], axis=0)

        def group(g, carry, masked):
            dq2, psum, csum = carry
            js = [nu * g + u for u in range(nu)]
            offs = [pl.multiple_of(j * SB_BLOCK, SB_BLOCK) for j in js]
            kbs = [k_ref[pl.ds(off, SB_BLOCK), :].astype(MXU_DTYPE) for off in offs]
            zs = [_dot(q2, kb, NT) for kb in kbs]
            dws = [_dot(do2, v_ref[pl.ds(off, SB_BLOCK), :].astype(MXU_DTYPE), NT) for off in offs]
            ts = [jnp.log(1.0 + jnp.exp(-jnp.abs(z))) for z in zs]
            lks = [-(jnp.maximum(z, 0.0) + t) for z, t in zip(zs, ts)]
            if masked:
                masks = [(j * SB_BLOCK + c2) < t_glob for j in js]
                lks = [jnp.where(mk, lk, 0.0) for mk, lk in zip(masks, lks)]
            sufs = [_split_dot(lk, m_gt, SB_SPLIT) for lk in lks]
            lsums = [jnp.sum(lk, axis=1, keepdims=True) for lk in lks]
            logsigs = [jnp.minimum(z, 0.0) - t for z, t in zip(zs, ts)]
            wgts = []
            for lsg, lsum, sf in zip(logsigs, lsums, sufs):
                psum = psum + lsum
                wgts.append(jnp.exp(lsg + (rt - psum) + sf))
            if masked:
                wgts = [jnp.where(mk, wg, 0.0) for mk, wg in zip(masks, wgts)]
            dlogas = [wg * dw for wg, dw in zip(wgts, dws)]
            pres = [_split_dot(dl, m_lt, SB_SPLIT) for dl in dlogas]
            dlks = []
            for dl, pre in zip(dlogas, pres):
                dlks.append(csum + pre)
                csum = csum + jnp.sum(dl, axis=1, keepdims=True)
            if masked:
                dlks = [jnp.where(mk, dlk, 0.0) for mk, dlk in zip(masks, dlks)]
            sigs = [jnp.exp(lsg) for lsg in logsigs]
            dzbs = [(dl * (1.0 - sg) - dlk * sg).astype(MXU_DTYPE) for dl, sg, dlk in zip(dlogas, sigs, dlks)]
            for off, dzb, wg, kb in zip(offs, dzbs, wgts, kbs):
                dk_ref[pl.ds(off, SB_BLOCK), :] += _dot(dzb, q2, TN)
                dv_ref[pl.ds(off, SB_BLOCK), :] += _dot(wg.astype(MXU_DTYPE), do2, TN)
                dq2 = dq2 + _dot(dzb, kb, NN)
            return dq2, psum, csum

        zero_col = jnp.zeros((2 * SB_BLOCK, 1), F32)
        carry = lax.fori_loop(0, i // nu, lambda g, cr: group(g, cr, False),
                              (jnp.zeros((2 * SB_BLOCK, LANES), F32), zero_col, zero_col))
        dq2, _, _ = group(i // nu, carry, True)
        dq_ref[...] = _sb_unstack(dq2, lm0) * scale

    blk = pl.BlockSpec((SB_BLOCK, LANES), lambda p, i: (i, p))
    full = pl.BlockSpec((s, LANES), lambda p, i: (0, p))
    return pl.pallas_call(
        body, grid=(npair, nq),
        in_specs=[pl.BlockSpec((SB_BLOCK, LANES), lambda p, i: (i, col0 + p)),
                  pl.BlockSpec((s, LANES), lambda p, i: (0, col0 + npair + p)),
                  pl.BlockSpec((s, LANES), lambda p, i: (0, col0 + 2 * npair + p)),
                  blk, blk],
        out_specs=[blk, full, full],
        out_shape=[jax.ShapeDtypeStruct((s, width), F32)] * 3,
        compiler_params=_cparams(2), name=name,
    )(src, src, src, rtot, do)


def _ssd_group(xs, dt_rows, alogs, dtbs, bm, cm, h0s):
    c = bm.shape[0]
    ii, jj = _chunk_masks(c)
    causal, eye = ii >= jj, ii == jj
    scores = _hdot(cm, bm, NT)
    dt_r = _each(lambda dt, b: _softplus(dt + b), dt_rows, dtbs)
    a_r = _each(lambda al, dt: -jnp.exp(al) * dt, alogs, dt_r)
    dt_col = _each(lambda dt: _row_to_col(dt, eye), dt_r)
    a_col = _each(lambda a: _row_to_col(a, eye), a_r)
    ac_col = _each(lambda a: jnp.sum(jnp.where(causal, a, 0.0), axis=1, keepdims=True), a_r)
    ac_row = _each(lambda a: jnp.sum(jnp.where(jj >= ii, a, 0.0), axis=0, keepdims=True), a_col)
    lmat = _each(lambda c_, r_: jnp.exp(jnp.where(causal, c_ - r_, NEG_BIG)), ac_col, ac_row)
    xdt = _each(jnp.multiply, xs, dt_col)
    al = _each(lambda a: jnp.sum(a, axis=1, keepdims=True), a_r)
    ys = _each(lambda lm, xd, h0, ac: _hdot(scores * lm, xd) + _hdot(cm, h0, NT) * jnp.exp(ac), lmat, xdt, h0s, ac_col)
    h1s = _each(lambda h0, al_, xd, ac: h0 * jnp.exp(al_) + _hdot(xd * jnp.exp(al_ - ac), bm, TN), h0s, al, xdt, ac_col)
    return ys, h1s


def _ssd_specs(ng, nc, r, rev):
    n_of = (lambda n: nc - 1 - n) if rev else (lambda n: n)
    gw = r * SSM_HEAD_DIM
    x_spec = pl.BlockSpec((CHUNK, gw), lambda g, n: (n_of(n), g))
    b_spec = pl.BlockSpec((CHUNK, SSM_STATE), lambda g, n: (n_of(n), (ng * gw) // SSM_STATE + g))
    c_spec = pl.BlockSpec((CHUNK, SSM_STATE), lambda g, n: (n_of(n), (ng * gw) // SSM_STATE + ng + g))
    dt_spec = pl.BlockSpec((None, None, r, CHUNK), lambda g, n: (g, n_of(n), 0, 0))
    sc_spec = pl.BlockSpec((None, r, 1), lambda g, n: (g, 0, 0))
    st_spec = pl.BlockSpec((None, None, r, SSM_HEAD_DIM, SSM_STATE), lambda g, n: (g, n_of(n), 0, 0, 0))
    y_spec = pl.BlockSpec((CHUNK, gw), lambda g, n: (n_of(n), g))
    bc_out = pl.BlockSpec((CHUNK, SSM_STATE), lambda g, n: (n_of(n), g))
    return x_spec, b_spec, c_spec, dt_spec, sc_spec, st_spec, y_spec, bc_out


def _ssd_fwd(xbc, dt_rows, alog, dtb, *, name):
    s = xbc.shape[0]
    ng, nc, r = dt_rows.shape[0], dt_rows.shape[1], dt_rows.shape[2]
    w = ng * r * SSM_HEAD_DIM
    x_spec, b_spec, c_spec, dt_spec, sc_spec, st_spec, y_spec, _ = _ssd_specs(ng, nc, r, False)
    p = SSM_HEAD_DIM

    def body(x_ref, b_ref, c_ref, dt_ref, al_ref, db_ref, y_ref, st_ref, state):
        @pl.when(pl.program_id(1) == 0)
        def _():
            state[...] = jnp.zeros_like(state)

        st_ref[...] = state[...]
        xs = [x_ref[:, h * p:(h + 1) * p] for h in range(r)]
        dts = [dt_ref[h:h + 1, :] for h in range(r)]
        als = [al_ref[h:h + 1, :] for h in range(r)]
        dbs = [db_ref[h:h + 1, :] for h in range(r)]
        h0s = [state[h] for h in range(r)]
        ys, h1s = _ssd_group(xs, dts, als, dbs, b_ref[...], c_ref[...], h0s)
        for h in range(r):
            y_ref[:, h * p:(h + 1) * p] = ys[h]
            state[h] = h1s[h]

    return pl.pallas_call(
        body, grid=(ng, nc),
        in_specs=[x_spec, b_spec, c_spec, dt_spec, sc_spec, sc_spec],
        out_specs=[y_spec, st_spec],
        out_shape=[jax.ShapeDtypeStruct((s, w), F32), jax.ShapeDtypeStruct((ng, nc, r, p, SSM_STATE), F32)],
        scratch_shapes=[pltpu.VMEM((r, p, SSM_STATE), F32)],
        compiler_params=_cparams(2), name=name,
    )(xbc, xbc, xbc, dt_rows, alog, dtb)


def _ssd_bwd(xbc, dt_rows, alog, dtb, states, dy, *, name):
    s = xbc.shape[0]
    ng, nc, r = dt_rows.shape[0], dt_rows.shape[1], dt_rows.shape[2]
    w = ng * r * SSM_HEAD_DIM
    x_spec, b_spec, c_spec, dt_spec, sc_spec, st_spec, y_spec, bc_out = _ssd_specs(ng, nc, r, True)
    p = SSM_HEAD_DIM

    def body(x_ref, b_ref, c_ref, dt_ref, al_ref, db_ref, st_ref, dy_ref,
             dx_ref, dbm_ref, dcm_ref, ddt_ref, dal_ref, ddb_ref, dstate):
        @pl.when(pl.program_id(1) == 0)
        def _():
            dstate[...] = jnp.zeros_like(dstate)
            dal_ref[...] = jnp.zeros_like(dal_ref)
            ddb_ref[...] = jnp.zeros_like(ddb_ref)

        xs = [x_ref[:, h * p:(h + 1) * p] for h in range(r)]
        dts = [dt_ref[h:h + 1, :] for h in range(r)]
        als = [al_ref[h:h + 1, :] for h in range(r)]
        dbs = [db_ref[h:h + 1, :] for h in range(r)]
        h0s = [st_ref[h] for h in range(r)]
        _, vjp = jax.vjp(_ssd_group, xs, dts, als, dbs, b_ref[...], c_ref[...], h0s)
        dys = [dy_ref[:, h * p:(h + 1) * p] for h in range(r)]
        dh1s = [dstate[h] for h in range(r)]
        dxs, ddts, dals, ddbs, dbm, dcm, dh0s = vjp((dys, dh1s))
        dbm_ref[...] = dbm
        dcm_ref[...] = dcm
        for h in range(r):
            dx_ref[:, h * p:(h + 1) * p] = dxs[h]
            ddt_ref[h:h + 1, :] = ddts[h]
            dal_ref[h:h + 1, :] += dals[h]
            ddb_ref[h:h + 1, :] += ddbs[h]
            dstate[h] = dh0s[h]

    gn = ng * SSM_STATE
    return pl.pallas_call(
        body, grid=(ng, nc),
        in_specs=[x_spec, b_spec, c_spec, dt_spec, sc_spec, sc_spec, st_spec, y_spec],
        out_specs=[y_spec, bc_out, bc_out, dt_spec, sc_spec, sc_spec],
        out_shape=[jax.ShapeDtypeStruct((s, w), F32), jax.ShapeDtypeStruct((s, gn), F32), jax.ShapeDtypeStruct((s, gn), F32),
                   jax.ShapeDtypeStruct(dt_rows.shape, F32), jax.ShapeDtypeStruct((ng, r, 1), F32),
                   jax.ShapeDtypeStruct((ng, r, 1), F32)],
        scratch_shapes=[pltpu.VMEM((r, p, SSM_STATE), F32)],
        compiler_params=_cparams(2), name=name,
    )(xbc, xbc, xbc, dt_rows, alog, dtb, states, dy)


def _ssm_post_fwd(y, xbc, src, z_col0, dexp, nw, *, name, tm=256):
    s, w = y.shape
    gw = w // SSM_GROUPS
    zc = z_col0 * LANES // gw

    def body(y_ref, x_ref, z_ref, d_ref, w_ref, o_ref):
        yy = (y_ref[...] + x_ref[...] * d_ref[...]) * _silu(z_ref[...])
        r = lax.rsqrt(jnp.mean(yy * yy, axis=-1, keepdims=True) + EPS)
        o_ref[...] = (yy * r * w_ref[...]).astype(o_ref.dtype)

    blk = pl.BlockSpec((tm, gw), lambda g, i: (i, g))
    vec = pl.BlockSpec((1, gw), lambda g, i: (0, g))
    return pl.pallas_call(
        body, grid=(SSM_GROUPS, s // tm),
        in_specs=[blk, blk, pl.BlockSpec((tm, gw), lambda g, i: (i, zc + g)), vec, vec],
        out_specs=blk, out_shape=jax.ShapeDtypeStruct((s, w), MXU_DTYPE),
        compiler_params=_cparams(2), name=name,
    )(y, xbc, src, dexp.reshape(1, w), nw.reshape(1, w))


def _ssm_post_bwd(y, xbc, src, z_col0, dexp, nw, dout, *, name, tm=256):
    s, w = y.shape
    gw = w // SSM_GROUPS
    zc = z_col0 * LANES // gw

    def body(y_ref, x_ref, z_ref, d_ref, w_ref, do_ref, dy_ref, dx_ref, dz_ref, dd_ref, dw_ref):
        xv, zv, dv = x_ref[...], z_ref[...], d_ref[...]
        pre = y_ref[...] + xv * dv
        sz = _silu(zv)
        yy = pre * sz
        r = lax.rsqrt(jnp.mean(yy * yy, axis=-1, keepdims=True) + EPS)
        yh = yy * r
        dov = do_ref[...]
        dyn = dov * w_ref[...]
        dyy = r * (dyn - yh * jnp.mean(dyn * yh, axis=-1, keepdims=True))
        dpre = dyy * sz
        dy_ref[...] = dpre
        dx_ref[...] = dpre * dv
        dz_ref[...] = (dyy * pre * _silu_grad(zv)).astype(dz_ref.dtype)

        @pl.when(pl.program_id(1) == 0)
        def _():
            dd_ref[...] = jnp.zeros_like(dd_ref)
            dw_ref[...] = jnp.zeros_like(dw_ref)

        dd_ref[...] += jnp.sum(dpre * xv, axis=0, keepdims=True)
        dw_ref[...] += jnp.sum(dov * yh, axis=0, keepdims=True)

    blk = pl.BlockSpec((tm, gw), lambda g, i: (i, g))
    vec = pl.BlockSpec((1, gw), lambda g, i: (0, g))
    dy, dx, dz, dd, dw = pl.pallas_call(
        body, grid=(SSM_GROUPS, s // tm),
        in_specs=[blk, blk, pl.BlockSpec((tm, gw), lambda g, i: (i, zc + g)), vec, vec, blk],
        out_specs=[blk, blk, blk, vec, vec],
        out_shape=[jax.ShapeDtypeStruct((s, w), F32), jax.ShapeDtypeStruct((s, w), F32), jax.ShapeDtypeStruct((s, w), MXU_DTYPE),
                   jax.ShapeDtypeStruct((1, w), F32), jax.ShapeDtypeStruct((1, w), F32)],
        compiler_params=_cparams(2), name=name,
    )(y, xbc, src, dexp.reshape(1, w), nw.reshape(1, w), dout)
    return dy, dx, dz, dd.reshape(w), dw.reshape(w)


def _merge_fwd(proj3, src, gate_col0, d, *, name, tm=256):
    s = proj3.shape[0]
    nb = proj3.shape[1] // d
    gc = gate_col0 * LANES // d

    def body(*refs):
        p_refs, g_refs, o_ref = refs[:nb], refs[nb:2 * nb], refs[-1]
        acc = None
        for p_ref, g_ref in zip(p_refs, g_refs):
            term = _sigmoid(g_ref[...]) * p_ref[...]
            acc = term if acc is None else acc + term
        o_ref[...] = acc.astype(o_ref.dtype)

    p_specs = [pl.BlockSpec((tm, d), lambda i, b=b: (i, b)) for b in range(nb)]
    g_specs = [pl.BlockSpec((tm, d), lambda i, b=b: (i, gc + b)) for b in range(nb)]
    return pl.pallas_call(
        body, grid=(s // tm,), in_specs=p_specs + g_specs,
        out_specs=pl.BlockSpec((tm, d), lambda i: (i, 0)), out_shape=jax.ShapeDtypeStruct((s, d), MXU_DTYPE),
        compiler_params=_cparams(1), name=name,
    )(*([proj3] * nb), *([src] * nb))


def _merge_bwd(proj3, src, gate_col0, d, dmerged, *, name, tm=256):
    s = proj3.shape[0]
    nb = proj3.shape[1] // d
    gc = gate_col0 * LANES // d

    def body(p_ref, g_ref, dm_ref, dp_ref, dg_ref):
        sg = _sigmoid(g_ref[...])
        dm = dm_ref[...]
        dp_ref[...] = (dm * sg).astype(dp_ref.dtype)
        dg_ref[...] = (dm * p_ref[...] * sg * (1.0 - sg)).astype(dg_ref.dtype)

    blk = pl.BlockSpec((tm, d), lambda i, b: (i, b))
    return pl.pallas_call(
        body, grid=(s // tm, nb),
        in_specs=[blk, pl.BlockSpec((tm, d), lambda i, b: (i, gc + b)), pl.BlockSpec((tm, d), lambda i, b: (i, 0))],
        out_specs=[blk, blk],
        out_shape=[jax.ShapeDtypeStruct(proj3.shape, MXU_DTYPE), jax.ShapeDtypeStruct(proj3.shape, MXU_DTYPE)],
        compiler_params=_cparams(2), name=name,
    )(proj3, src, dmerged)


ANY = pl.BlockSpec(memory_space=pl.ANY)
MESH = pl.DeviceIdType.MESH


def _all_gather(shards, *, name):
    nt = len(shards)

    def body(*refs):
        x_refs, out_refs = refs[:nt], refs[nt:2 * nt]
        send_sems, recv_sems, local_sems = refs[2 * nt:]
        x, y, c = lax.axis_index("x"), lax.axis_index("y"), lax.axis_index("c")
        me, sibling = (x, y, c), (x, y, 1 - c)
        chips = [(1 - x, y), (x, 1 - y), (1 - x, 1 - y)]

        def slot(t, px, py, pc):
            return out_refs[t].at[4 * px + 2 * py + pc]

        def copy(t, k, block, to, from_input=False):
            return pltpu.make_async_remote_copy(
                src_ref=x_refs[t] if from_input else slot(t, *block), dst_ref=slot(t, *block),
                send_sem=send_sems.at[7 * t + k], recv_sem=recv_sems.at[7 * t + k], device_id=to, device_id_type=MESH)

        mine = [pltpu.make_async_copy(x_refs[t], slot(t, *me), local_sems.at[t]) for t in range(nt)]
        for cp in mine:
            cp.start()
        first = [copy(t, 0, me, sibling, True) for t in range(nt)]
        first += [copy(t, 1 + j, me, (*chip, c), True) for j, chip in enumerate(chips) for t in range(nt)]
        for cp in first:
            cp.start()
        passed = []
        for j, chip in enumerate(chips):
            for t in range(nt):
                copy(t, 1 + j, (*chip, c), me).wait_recv()
                fwd = copy(t, 4 + j, (*chip, c), sibling)
                fwd.start()
                passed.append(fwd)
        for t in range(nt):
            copy(t, 0, sibling, me).wait_recv()
            for j, chip in enumerate(chips):
                copy(t, 4 + j, (*chip, 1 - c), me).wait_recv()
        for cp in first + passed:
            cp.wait_send()
        for cp in mine:
            cp.wait()

    return pl.pallas_call(
        body, out_shape=[jax.ShapeDtypeStruct((N_DEV,) + a.shape, a.dtype) for a in shards],
        in_specs=[ANY] * nt, out_specs=[ANY] * nt,
        scratch_shapes=[pltpu.SemaphoreType.DMA((7 * nt,)), pltpu.SemaphoreType.DMA((7 * nt,)),
                        pltpu.SemaphoreType.DMA((nt,))],
        name=name,
    )(*shards)


def _grad_exchange(bigs, small, *, name):
    nl = len(bigs[0])
    flat = [a for per_layer in bigs for a in per_layer]
    nslot = len(flat)

    def body(*refs):
        in_refs, small_ref = refs[:nslot], refs[nslot]
        out_refs, smallr_ref = refs[nslot + 1:nslot + 1 + len(bigs)], refs[nslot + 1 + len(bigs)]
        send_sems, recv_sems, local_sems = refs[nslot + 2 + len(bigs):]
        x, y, c = lax.axis_index("x"), lax.axis_index("y"), lax.axis_index("c")
        me = 4 * x + 2 * y + c
        local = [pltpu.make_async_copy(in_refs[i].at[me], out_refs[i // nl].at[me, i % nl], local_sems.at[i])
                 for i in range(nslot)]
        local.append(pltpu.make_async_copy(small_ref, smallr_ref.at[me], local_sems.at[nslot]))
        for cp in local:
            cp.start()
        copies = []
        for k in range(1, N_DEV):
            px = x ^ ((k >> 2) & 1)
            py = y ^ ((k >> 1) & 1)
            pc = c ^ (k & 1)
            peer = 4 * px + 2 * py + pc
            for i in range(nslot + 1):
                sem = 7 * i + (k - 1)
                src = in_refs[i].at[peer] if i < nslot else small_ref
                dst = out_refs[i // nl].at[me, i % nl] if i < nslot else smallr_ref.at[me]
                copies.append(pltpu.make_async_remote_copy(
                    src_ref=src, dst_ref=dst, send_sem=send_sems.at[sem], recv_sem=recv_sems.at[sem],
                    device_id=(px, py, pc), device_id_type=MESH))
        for cp in copies:
            cp.start()
        for cp in copies:
            cp.wait_recv()
        for cp in copies:
            cp.wait_send()
        for cp in local:
            cp.wait()

    out_shape = [jax.ShapeDtypeStruct((N_DEV, nl) + per_layer[0].shape[1:], per_layer[0].dtype) for per_layer in bigs]
    out_shape.append(jax.ShapeDtypeStruct((N_DEV,) + small.shape, small.dtype))
    nsem = 7 * (nslot + 1)
    outs = pl.pallas_call(
        body, out_shape=out_shape,
        in_specs=[ANY] * (nslot + 1), out_specs=[ANY] * (len(bigs) + 1),
        scratch_shapes=[pltpu.SemaphoreType.DMA((nsem,)), pltpu.SemaphoreType.DMA((nsem,)),
                        pltpu.SemaphoreType.DMA((nslot + 1,))],
        name=name,
    )(*flat, small)
    return outs[:-1], outs[-1]


HBM = pl.BlockSpec(memory_space=pltpu.HBM)
SEM = pl.BlockSpec(memory_space=pltpu.SEMAPHORE)
EFFECT = pltpu.SideEffectType.DATAFLOW_SIDE_EFFECTING


def _peers():
    x, y, c = lax.axis_index("x"), lax.axis_index("y"), lax.axis_index("c")
    peers = []
    for k in range(1, N_DEV):
        px, py, pc = x ^ ((k >> 2) & 1), y ^ ((k >> 1) & 1), c ^ (k & 1)
        peers.append(((px, py, pc), 4 * px + 2 * py + pc))
    return 4 * x + 2 * y + c, peers


def _split_copies(slots, src_refs, land_refs, send_sems, recv_sems):
    me, peers = _peers()
    copies = []
    for t, (whole, layer) in enumerate(slots):
        dst = land_refs[t].at[me] if layer is None else land_refs[t].at[me, layer]
        for k, (dev, lin) in enumerate(peers):
            copies.append(pltpu.make_async_remote_copy(
                src_ref=src_refs[t] if whole else src_refs[t].at[lin], dst_ref=dst,
                send_sem=send_sems.at[7 * t + k], recv_sem=recv_sems.at[7 * t + k], device_id=dev, device_id_type=MESH))
    return copies


def _split_start(srcs, lands, slots, carry, *, name):
    n = len(srcs)

    def body(*refs):
        copies = _split_copies(slots, refs[:n], refs[n:2 * n], refs[2 * n + 1], refs[2 * n + 2])
        for cp in copies:
            cp.start()

    def hbm(a):
        return pltpu.HBM(a.shape, a.dtype)

    outs = pl.pallas_call(
        body, name=name,
        out_shape=[pltpu.SemaphoreType.DMA((7 * n,)), pltpu.SemaphoreType.DMA((7 * n,))]
        + [hbm(a) for a in srcs] + [hbm(a) for a in lands] + [hbm(carry)],
        in_specs=[HBM] * (2 * n + 1), out_specs=[SEM, SEM] + [HBM] * (2 * n + 1),
        input_output_aliases={i: 2 + i for i in range(2 * n + 1)},
        compiler_params=pltpu.CompilerParams(has_side_effects=EFFECT),
    )(*[pltpu.with_memory_space_constraint(a, pltpu.HBM) for a in list(srcs) + list(lands) + [carry]])
    return outs[0], outs[1], outs[2:2 + n], outs[2 + n:2 + 2 * n], outs[2 + 2 * n]


def _split_wait(send_sems, recv_sems, srcs, lands, slots, after, *, name):
    n = len(srcs)

    def body(*refs):
        copies = _split_copies(slots, refs[:n], refs[n:2 * n], refs[2 * n], refs[2 * n + 1])
        for cp in copies:
            cp.wait_send()
        for cp in copies:
            cp.wait_recv()

    outs = pl.pallas_call(
        body, name=name,
        out_shape=[pltpu.HBM(a.shape, a.dtype) for a in list(srcs) + list(lands)],
        in_specs=[HBM] * (2 * n) + [SEM, SEM, ANY], out_specs=[HBM] * (2 * n),
        input_output_aliases={i: i for i in range(2 * n)},
        compiler_params=pltpu.CompilerParams(has_side_effects=EFFECT),
    )(*srcs, *lands, send_sems, recv_sems, after)
    return outs[n:]


def _adam_math(w, g, m, v):
    m1 = ADAM_B1 * m + (1.0 - ADAM_B1) * g
    v1 = ADAM_B2 * v + (1.0 - ADAM_B2) * (g * g)
    m_hat = m1 / (1.0 - ADAM_B1 ** ADAM_STEP)
    v_hat = v1 / (1.0 - ADAM_B2 ** ADAM_STEP)
    delta = -ADAM_LR * (m_hat / (jnp.sqrt(v_hat) + ADAM_EPS) + ADAM_WD * w)
    return delta, m1, v1


def _sum_adamw(parts, w, m, v, layer, prev, *, name):
    shape = w.shape
    r, c = shape[-2], shape[-1]
    a_l = math.prod(shape[1:-2])
    a = shape[0] * a_l
    base = layer * a_l
    tr = _pick(r, (256,) if c <= 1024 else (128,))
    w3, m3, v3 = (t.reshape(a, r, c) for t in (w, m, v))
    n_prev = 0 if prev is None else 4

    def body(*refs):
        p_ref, w_ref, m_ref, v_ref = refs[:4]
        g_ref, d_ref, m1_ref, v1_ref = refs[4 + n_prev:]
        g = p_ref[0].astype(F32)
        for src in range(1, N_DEV):
            g = g + p_ref[src].astype(F32)
        delta, m1, v1 = _adam_math(w_ref[...], g, m_ref[...], v_ref[...])
        g_ref[...] = g
        d_ref[...] = delta
        m1_ref[...] = m1
        v1_ref[...] = v1

    blk = pl.BlockSpec((None, tr, c), lambda i, j: (base + i, j, 0))
    prev3 = [] if prev is None else [t.reshape(a, r, c) for t in prev]
    outs = pl.pallas_call(
        body, grid=(a_l, r // tr),
        in_specs=[pl.BlockSpec((N_DEV, None, tr, c), lambda i, j: (0, i, j, 0)), blk, blk, blk] + [ANY] * n_prev,
        out_specs=[blk] * 4, out_shape=[jax.ShapeDtypeStruct((a, r, c), F32)] * 4,
        input_output_aliases={4 + k: k for k in range(n_prev)},
        compiler_params=_cparams(2), name=name,
    )(parts.reshape(N_DEV, a_l, r, c), w3, m3, v3, *prev3)
    return [o.reshape(shape) for o in outs]


def _sum_parts(parts, *, name):
    rows = parts.shape[1]

    def body(p_ref, o_ref):
        g = p_ref[0]
        for src in range(1, N_DEV):
            g = g + p_ref[src]
        o_ref[...] = g

    return pl.pallas_call(
        body, grid=(1,), in_specs=[pl.BlockSpec((N_DEV, rows, LANES), lambda i: (0, 0, 0))],
        out_specs=pl.BlockSpec((rows, LANES), lambda i: (0, 0)), out_shape=jax.ShapeDtypeStruct((rows, LANES), F32),
        compiler_params=_cparams(1), name=name,
    )(parts)


def _adamw(w, g, m, v, *, name):
    rows = w.shape[0]

    def body(w_ref, g_ref, m_ref, v_ref, d_ref, m1_ref, v1_ref):
        delta, m1, v1 = _adam_math(w_ref[...], g_ref[...], m_ref[...], v_ref[...])
        d_ref[...] = delta
        m1_ref[...] = m1
        v1_ref[...] = v1

    blk = pl.BlockSpec((rows, LANES), lambda i: (0, 0))
    return pl.pallas_call(
        body, grid=(1,), in_specs=[blk] * 4, out_specs=[blk] * 3,
        out_shape=[jax.ShapeDtypeStruct((rows, LANES), F32)] * 3,
        compiler_params=_cparams(1), name=name,
    )(w, g, m, v)


def _pack(arrs, dtype, row_mult=16):
    flat = jnp.concatenate([a.reshape(-1).astype(dtype) for a in arrs])
    n = flat.shape[0]
    rows = -(-n // (LANES * row_mult)) * row_mult
    flat = jnp.pad(flat, (0, rows * LANES - n))
    return flat.reshape(rows, LANES)


def _unpack(packed, shapes):
    flat = packed.reshape(-1)
    out, off = [], 0
    for shp in shapes:
        n = math.prod(shp)
        out.append(flat[off:off + n].reshape(shp))
        off += n
    return out


class _Layout:
    def __init__(self, d):
        self.d = d
        w = d
        self.dn_heads = w // DN_HEAD_DIM
        self.ssm_heads = w // SSM_HEAD_DIM
        gn = SSM_GROUPS * SSM_STATE
        self.sizes = (3 * w, w, self.dn_heads, self.dn_heads, 3 * w, w, w + 2 * gn, self.ssm_heads, 3 * d)
        offs, o = [], 0
        for sz in self.sizes:
            offs.append(o)
            o += sz
        self.offs = offs
        self.in_dim = o
        self.big = (0, 1, 4, 5, 6, 8)
        self.small = (2, 3, 7)
        cols, o = {}, 0
        for idx in self.big:
            cols[idx] = o
            o += self.sizes[idx]
        self.small_col = o
        self.cols = cols
        self.padded = o + LANES
        self.n_small = sum(self.sizes[i] for i in self.small)

    def reorder_w(self, w_in):
        parts = [w_in[:, self.offs[i]:self.offs[i] + self.sizes[i]] for i in self.big + self.small]
        parts.append(jnp.zeros((w_in.shape[0], LANES - self.n_small), w_in.dtype))
        return jnp.concatenate(parts, axis=1)

    def from_shards(self, parts):
        cs = self.in_dim // N_DEV
        pieces = []
        for i in self.big + self.small:
            a, b = self.offs[i], self.offs[i] + self.sizes[i]
            while a < b:
                j = a // cs
                hi = min(b, (j + 1) * cs)
                pieces.append(parts[j][:, a - j * cs:hi - j * cs])
                a = hi
        pieces.append(jnp.zeros((parts.shape[1], LANES - self.n_small), parts.dtype))
        return jnp.concatenate(pieces, axis=1)

    def to_shards(self, wp):
        cs = self.in_dim // N_DEV
        pcol = dict(self.cols)
        o = self.small_col
        for i in self.small:
            pcol[i] = o
            o += self.sizes[i]
        shards = []
        for j in range(N_DEV):
            a, b = j * cs, (j + 1) * cs
            pieces = []
            for i in range(len(self.sizes)):
                lo, hi = max(a, self.offs[i]), min(b, self.offs[i] + self.sizes[i])
                if lo < hi:
                    pieces.append(wp[:, pcol[i] + lo - self.offs[i]:pcol[i] + hi - self.offs[i]])
            shards.append(jnp.concatenate(pieces, axis=1))
        return jnp.stack(shards)

    def restore_w(self, wp):
        pieces = {}
        for idx in self.big:
            pieces[idx] = wp[:, self.cols[idx]:self.cols[idx] + self.sizes[idx]]
        o = self.small_col
        for idx in self.small:
            pieces[idx] = wp[:, o:o + self.sizes[idx]]
            o += self.sizes[idx]
        return jnp.concatenate([pieces[i] for i in range(len(self.sizes))], axis=1)


def _rows_form(cols_t, nh, nc):
    return cols_t.T.reshape(nh, nc, 1, CHUNK)


def _layer_fwd(x, p, lay, tag, late=None):
    s, d = x.shape
    nc = s // CHUNK
    w = d
    dnh, smh = lay.dn_heads, lay.ssm_heads
    r = smh // SSM_GROUPS
    cb = {k: v // LANES for k, v in lay.cols.items()}
    sv = {}
    h1 = _rms_fwd(x, p["norm_mix"], name=f"rms_mix_{tag}")
    proj = _matmul(h1, p["w_in"], name=f"mm_in_{tag}")
    small = proj[:, lay.small_col:lay.small_col + LANES]
    a_rows = _rows_form(small[:, 0:dnh], dnh, nc)
    b_rows = _rows_form(small[:, dnh:2 * dnh], dnh, nc)
    dt_rows = small[:, 2 * dnh:2 * dnh + smh].T.reshape(SSM_GROUPS, r, nc, CHUNK).transpose(0, 2, 1, 3)
    zero_b = jnp.zeros((1, 3 * w), F32)
    dn_qkv = _conv_fwd(proj, cb[0], p["dn_conv_w"], zero_b, 2 * dnh, name=f"dn_conv_{tag}")
    dn_alog = p["dn_a_log"].reshape(dnh, 1, 1)
    dn_dtb = p["dn_dt_bias"].reshape(dnh, 1, 1)
    o_dn, dn_states = _dn_fwd(dn_qkv, a_rows, b_rows, dn_alog, dn_dtb, name=f"dn_chunk_{tag}")
    y_dn = _dn_post_fwd(o_dn, proj, cb[1], p["dn_norm_w"], name=f"dn_post_{tag}")
    o_sb, sb_r = _sb_fwd(proj, cb[4], w, name=f"sb_{tag}")
    xbc = _conv_fwd(proj, cb[6], p["ssm_conv_w"], p["ssm_conv_b"].reshape(1, -1), 0, name=f"ssm_conv_{tag}")
    ssm_alog = p["ssm_a_log"].reshape(SSM_GROUPS, r, 1)
    ssm_dtb = p["ssm_dt_bias"].reshape(SSM_GROUPS, r, 1)
    y_ssd, ssm_states = _ssd_fwd(xbc, dt_rows, ssm_alog, ssm_dtb, name=f"ssd_{tag}")
    dexp = jnp.repeat(p["ssm_d"], SSM_HEAD_DIM)
    y_ssm = _ssm_post_fwd(y_ssd, xbc, proj, cb[5], dexp, p["ssm_norm_w"], name=f"ssm_post_{tag}")
    if late is not None:
        p.update(late(y_ssm))
    branches = (y_dn, o_sb, y_ssm)
    proj3 = jnp.concatenate(
        [_matmul(br, p["w_branch"][i], name=f"mm_branch{i}_{tag}") for i, br in enumerate(branches)], axis=1)
    merged = _merge_fwd(proj3, proj, cb[8], d, name=f"merge_{tag}")
    x1 = _matmul(merged, p["w_out"], name=f"mm_out_{tag}", epilogue=lambda acc, res: (acc + res,), extras=(x,))
    h2 = _rms_fwd(x1, p["norm_mlp"], name=f"rms_mlp_{tag}")
    u, act = _matmul(h2, p["w_up"], name=f"mm_up_{tag}", out_dtypes=(F32, MXU_DTYPE),
                     epilogue=lambda acc: (acc, jnp.square(jnp.maximum(acc, 0.0))))
    x2 = _matmul(act, p["w_down"], name=f"mm_down_{tag}", epilogue=lambda acc, res: (acc + res,), extras=(x1,))
    sv.update(x=x, h1=h1, proj=proj, a_rows=a_rows, b_rows=b_rows, dt_rows=dt_rows, dn_qkv=dn_qkv, dn_alog=dn_alog,
              dn_dtb=dn_dtb, o_dn=o_dn, dn_states=dn_states, y_dn=y_dn, o_sb=o_sb, sb_r=sb_r, xbc=xbc, ssm_alog=ssm_alog,
              ssm_dtb=ssm_dtb, y_ssd=y_ssd, ssm_states=ssm_states, dexp=dexp, y_ssm=y_ssm, proj3=proj3, merged=merged,
              x1=x1, h2=h2, u=u, act=act)
    return x2, sv


def _layer_bwd(dx2, p, sv, lay, tag, early=None, late=None):
    x = sv["x"]
    s, d = x.shape
    nc = s // CHUNK
    w = d
    dnh, smh = lay.dn_heads, lay.ssm_heads
    r = smh // SSM_GROUPS
    gn = SSM_GROUPS * SSM_STATE
    cb = {k: v // LANES for k, v in lay.cols.items()}
    proj = sv["proj"]
    g = {}
    dx2_b = dx2.astype(MXU_DTYPE)
    du = _matmul(dx2_b, p["w_down"], tb=True, name=f"mm_down_dx_{tag}", out_dtypes=(MXU_DTYPE,),
                 epilogue=lambda acc, uu: (acc * (2.0 * jnp.maximum(uu, 0.0)),), extras=(sv["u"],))
    g["w_down"] = _matmul(sv["act"], dx2_b, ta=True, name=f"mm_down_dw_{tag}", out_dtypes=(BF16,)).reshape(N_DEV, -1, d)
    g["w_up"] = _matmul(sv["h2"], du, ta=True, name=f"mm_up_dw_{tag}", out_dtypes=(BF16,), col_shards=N_DEV)
    dh2 = _matmul(du, p["w_up"], tb=True, name=f"mm_up_dx_{tag}")
    dx1, g["norm_mlp"] = _rms_bwd(sv["x1"], p["norm_mlp"], dh2, dx2, name=f"rms_mlp_bwd_{tag}")
    dx1_b = dx1.astype(MXU_DTYPE)
    dmerged = _matmul(dx1_b, p["w_out"], tb=True, name=f"mm_out_dx_{tag}")
    g["w_out"] = _matmul(sv["merged"], dx1_b, ta=True, name=f"mm_out_dw_{tag}", out_dtypes=(BF16,)).reshape(N_DEV, -1, d)
    dproj3, dgates = _merge_bwd(sv["proj3"], proj, cb[8], d, dmerged, name=f"merge_bwd_{tag}")
    branches = (sv["y_dn"], sv["o_sb"], sv["y_ssm"])
    dwb, dbr = [], []
    for i, br in enumerate(branches):
        dp_i = dproj3[:, i * d:(i + 1) * d]
        dwb.append(_matmul(br, dp_i, ta=True, name=f"mm_branch{i}_dw_{tag}", out_dtypes=(BF16,)).reshape(N_DEV, -1, d))
        dbr.append(_matmul(dp_i, p["w_branch"][i], tb=True, name=f"mm_branch{i}_dx_{tag}"))
    g["w_branch"] = jnp.stack(dwb, axis=1)
    dy_dn, do_sb, dy_ssm = dbr
    if early is not None:
        dy_ssm = early(g, dy_ssm)
    dy_ssd, dxs_skip, dz, ddexp, g["ssm_norm_w"] = _ssm_post_bwd(
        sv["y_ssd"], sv["xbc"], proj, cb[5], sv["dexp"], p["ssm_norm_w"], dy_ssm, name=f"ssm_post_bwd_{tag}")
    g["ssm_d"] = ddexp.reshape(smh, SSM_HEAD_DIM).sum(axis=1)
    dxs, dbm, dcm, ddt_rows, dalog, ddtb = _ssd_bwd(
        sv["xbc"], sv["dt_rows"], sv["ssm_alog"], sv["ssm_dtb"], sv["ssm_states"], dy_ssd, name=f"ssd_bwd_{tag}")
    g["ssm_a_log"] = dalog.reshape(smh)
    g["ssm_dt_bias"] = ddtb.reshape(smh)
    dxbc_post = jnp.concatenate([dxs + dxs_skip, dbm, dcm], axis=1)
    dxbc, g["ssm_conv_w"], dcb = _conv_bwd(proj, cb[6], p["ssm_conv_w"], p["ssm_conv_b"].reshape(1, -1), 0, dxbc_post,
                                           name=f"ssm_conv_bwd_{tag}")
    g["ssm_conv_b"] = dcb.reshape(-1)
    ddt = ddt_rows.transpose(0, 2, 1, 3).reshape(smh, s).T
    dq_sb, dk_sb, dv_sb = _sb_bwd(proj, cb[4], w, sv["sb_r"], do_sb, name=f"sb_bwd_{tag}")
    do_dn, dgate_dn, g["dn_norm_w"] = _dn_post_bwd(sv["o_dn"], proj, cb[1], p["dn_norm_w"], dy_dn, name=f"dn_post_bwd_{tag}")
    dq, dk, dv, da_rows, db_rows, dal, ddtb_dn = _dn_bwd(
        sv["dn_qkv"], sv["a_rows"], sv["b_rows"], sv["dn_alog"], sv["dn_dtb"], sv["dn_states"], do_dn, name=f"dn_chunk_bwd_{tag}")
    g["dn_a_log"] = dal.reshape(dnh)
    g["dn_dt_bias"] = ddtb_dn.reshape(dnh)
    zero_b = jnp.zeros((1, 3 * w), F32)
    ddn_qkv, g["dn_conv_w"], _ = _conv_bwd(proj, cb[0], p["dn_conv_w"], zero_b, 2 * dnh,
                                           jnp.concatenate([dq, dk, dv], axis=1), name=f"dn_conv_bwd_{tag}")
    da = da_rows.reshape(dnh, s).T
    db = db_rows.reshape(dnh, s).T
    dsmall = jnp.concatenate([da, db, ddt, jnp.zeros((s, LANES - lay.n_small), F32)], axis=1).astype(MXU_DTYPE)
    dproj = jnp.concatenate(
        [ddn_qkv, dgate_dn, dq_sb.astype(MXU_DTYPE), dk_sb.astype(MXU_DTYPE), dv_sb.astype(MXU_DTYPE), dz, dxbc, dgates, dsmall],
        axis=1)
    g["w_in"] = lay.to_shards(_matmul(sv["h1"], dproj, ta=True, name=f"mm_in_dw_{tag}", out_dtypes=(BF16,)))
    if late is not None:
        dproj = late(g, dproj)
    dh1 = _matmul(dproj, p["w_in"], tb=True, name=f"mm_in_dx_{tag}")
    dx0, g["norm_mix"] = _rms_bwd(x, p["norm_mix"], dh1, dx1, name=f"rms_mix_bwd_{tag}")
    return dx0, g


BIG = ("w_in", "w_branch", "w_out", "w_up", "w_down")
CONV = ("dn_conv_w", "ssm_conv_w")
SMALL = ("norm_mix", "dn_conv_w", "dn_a_log", "dn_dt_bias", "dn_norm_w", "ssm_conv_w", "ssm_conv_b", "ssm_a_log",
         "ssm_dt_bias", "ssm_d", "ssm_norm_w", "norm_mlp", "norm_final")
WEIGHTS = ("norm_mix", "w_in", "dn_conv_w", "dn_a_log", "dn_dt_bias", "dn_norm_w", "ssm_conv_w", "ssm_conv_b", "ssm_a_log",
           "ssm_dt_bias", "ssm_d", "ssm_norm_w", "w_branch", "w_out", "norm_mlp", "w_up", "w_down", "norm_final")
SHARD_AXIS = {"w_in": 2, "dn_conv_w": 2, "ssm_conv_w": 2, "w_branch": 2, "w_out": 1, "w_up": 2, "w_down": 1}


def _to_shards(full, axis):
    shp = full.shape
    n = shp[axis] // N_DEV
    t = full.reshape(shp[:axis] + (N_DEV, n) + shp[axis + 1:])
    return jnp.moveaxis(t, axis, 0)


def _from_shards(parts, axis):
    t = jnp.moveaxis(parts, 0, axis)
    shp = t.shape
    return t.reshape(shp[:axis] + (shp[axis] * shp[axis + 1],) + shp[axis + 2:])


def _unshard(parts, axis, *, name):
    shard = parts.shape[1:]
    nd = len(shard)
    if axis == 0:
        return parts.reshape((N_DEV * shard[0],) + shard[1:])

    def copy_block(i_ref, o_ref):
        o_ref[...] = i_ref[...]

    if axis == nd - 1:
        rows, n = math.prod(shard[:-1]), shard[-1]
        out = pl.pallas_call(
            copy_block, grid=(N_DEV,),
            in_specs=[pl.BlockSpec((None, rows, n), lambda j: (j, 0, 0))],
            out_specs=pl.BlockSpec((rows, n), lambda j: (0, j)),
            out_shape=jax.ShapeDtypeStruct((rows, N_DEV * n), parts.dtype),
            compiler_params=_cparams(1), name=name,
        )(parts.reshape(N_DEV, rows, n))
        return out.reshape(shard[:-1] + (N_DEV * n,))
    assert axis == nd - 2, (parts.shape, axis)
    a, n, c = math.prod(shard[:-2]), shard[-2], shard[-1]
    out = pl.pallas_call(
        copy_block, grid=(N_DEV, a),
        in_specs=[pl.BlockSpec((None, None, n, c), lambda j, i: (j, i, 0, 0))],
        out_specs=pl.BlockSpec((None, n, c), lambda j, i: (i, j, 0)),
        out_shape=jax.ShapeDtypeStruct((a, N_DEV * n, c), parts.dtype),
        compiler_params=_cparams(2), name=name,
    )(parts.reshape(N_DEV, a, n, c))
    return out.reshape(shard[:-2] + (N_DEV * n, c))


def _step(w, m, v, x, target):
    s, d = x.shape
    lay = _Layout(d)
    me = 4 * lax.axis_index("x") + 2 * lax.axis_index("y") + lax.axis_index("c")

    def shard(n, l):
        return w[n][l].astype(BF16) if n in BIG else w[n][l]

    def empty_land(a):
        return lax.empty((N_DEV,) + a.shape, a.dtype)

    def with_own(land, own):
        return lax.dynamic_update_index_in_dim(land, own, me, 0)

    def assemble(n, parts, l):
        return lay.from_shards(parts) if n == "w_in" else _unshard(parts, SHARD_AXIS[n] - 1, name=f"unshard_{n}_l{l}")

    small_names = tuple(n for n in WEIGHTS if n not in BIG + CONV + ("norm_final",))

    first, rest = ("w_in",) + CONV, BIG[1:]
    got = _all_gather([shard(n, 0) for n in first], name="gather_l0_first")
    whole, sliced = (True, None), (False, None)
    names_a, names_b = rest, BIG + CONV
    srcs_a, srcs_b = [shard(n, 0) for n in names_a], [shard(n, 1) for n in names_b]
    sem_sa, sem_ra, srcs_a, lands_a, w_in0 = _split_start(
        srcs_a, [empty_land(a) for a in srcs_a], [whole] * len(srcs_a), got[0], name="gather_l0_rest_start")
    sem_sb, sem_rb, srcs_b, lands_b, w_in0 = _split_start(
        srcs_b, [empty_land(a) for a in srcs_b], [whole] * len(srcs_b), w_in0, name="gather_l1_start")
    p0 = {n: w[n][0] for n in small_names}
    p0.update({n: assemble(n, g, 0) for n, g in zip(first, [w_in0] + list(got[1:]))})

    def late_l0(after):
        lands = _split_wait(sem_sa, sem_ra, srcs_a, lands_a, [whole] * len(srcs_a), after, name="gather_l0_rest_wait")
        return {n: assemble(n, with_own(ld, s_), 0) for n, ld, s_ in zip(names_a, lands, srcs_a)}

    h, sv0 = _layer_fwd(x, p0, lay, "l0", late=late_l0)
    lands = _split_wait(sem_sb, sem_rb, srcs_b, lands_b, [whole] * len(srcs_b), h, name="gather_l1_wait")
    p1 = {n: w[n][1] for n in small_names}
    p1.update({n: assemble(n, with_own(ld, s_), 1) for n, ld, s_ in zip(names_b, lands, srcs_b)})
    h, sv1 = _layer_fwd(h, p1, lay, "l1")
    loss, dh, g_norm_final = _final_loss(h, w["norm_final"], target, name="final_loss")
    grads = [None] * DEPTH
    dh, grads[1] = _layer_bwd(dh, p1, sv1, lay, "l1")

    def exchange_start(names, g, carry, tag):
        srcs = [g[n] for n in names]
        return _split_start(srcs, [lax.empty(a.shape, a.dtype) for a in srcs], [sliced] * len(srcs), carry,
                            name=f"grad_{tag}_start")

    def exchange_wait(names, started, after, tag):
        sem_s, sem_r, srcs, lands_, _ = started
        lands_ = _split_wait(sem_s, sem_r, srcs, lands_, [sliced] * len(srcs), after, name=f"grad_{tag}_wait")
        return {n: with_own(ld, lax.dynamic_index_in_dim(s_, me, 0, keepdims=False)) for n, ld, s_ in zip(names, lands_, srcs)}

    x1_started = exchange_start(BIG, grads[1], dh, "l1")
    pending = {}

    def early_l0(g, carry):
        pending["rest"] = exchange_start(rest, g, carry, "l0_rest")
        return pending["rest"][4]

    def late_bwd_l0(g, carry):
        pending["w_in"] = exchange_start(("w_in",), g, carry, "l0_w_in")
        return pending["w_in"][4]

    grad_x, grads[0] = _layer_bwd(x1_started[4], p0, sv0, lay, "l0", early=early_l0, late=late_bwd_l0)

    out = {"grad": {}, "delta": {}, "new_m": {}, "new_v": {}}
    parts1 = exchange_wait(BIG, x1_started, grad_x, "l1")
    res1 = {n: _sum_adamw(parts1[n], w[n], m[n], v[n], 1, None, name=f"sum_adamw_{n}_l1") for n in BIG}
    parts0 = exchange_wait(rest, pending["rest"], res1["w_in"][0], "l0_rest")
    res0 = {n: _sum_adamw(parts0[n], w[n], m[n], v[n], 0, res1[n], name=f"sum_adamw_{n}_l0") for n in rest}
    parts0 = exchange_wait(("w_in",), pending["w_in"], res0["w_down"][0], "l0_w_in")
    res0["w_in"] = _sum_adamw(parts0["w_in"], w["w_in"], m["w_in"], v["w_in"], 0, res1["w_in"], name="sum_adamw_w_in_l0")
    for n in BIG:
        for key, a in zip(("grad", "delta", "new_m", "new_v"), res0[n]):
            out[key][n] = a

    gfull = {n: jnp.stack([grads[l][n] for l in range(DEPTH)]) for n in SMALL if n != "norm_final"}
    gfull["norm_final"] = g_norm_final
    small_send = _pack([gfull[n] for n in SMALL] + [loss.reshape(1)], F32)
    small_recv = _all_gather([small_send], name="gather_small_grads")[0]
    small_sum = _sum_parts(small_recv, name="sum_small")
    small_full = _unpack(small_sum, [gfull[n].shape for n in SMALL] + [(1,)])
    loss_total = small_full[-1][0]
    gsmall = {}
    for n, a in zip(SMALL, small_full[:-1]):
        if n in SHARD_AXIS:
            a = lax.dynamic_index_in_dim(_to_shards(a, SHARD_AXIS[n]), me, axis=0, keepdims=False)
        gsmall[n] = a
    small_shapes = [w[n].shape for n in SMALL]
    ws, gs, ms, vs = (_pack([t[n] for n in SMALL], F32) for t in (w, gsmall, m, v))
    ds, m1s, v1s = _adamw(ws, gs, ms, vs, name="adamw_small")
    for n in SMALL:
        out["grad"][n] = gsmall[n]
    for key, packed in (("delta", ds), ("new_m", m1s), ("new_v", v1s)):
        for n, a in zip(SMALL, _unpack(packed, small_shapes)):
            out[key][n] = a
    return loss_total, grad_x, out


def kernel(x, norm_mix, w_in, dn_conv_w, dn_a_log, dn_dt_bias, dn_norm_w, ssm_conv_w, ssm_conv_b, ssm_a_log, ssm_dt_bias, ssm_d, ssm_norm_w, w_branch, w_out, norm_mlp, w_up, w_down, norm_final, loss_target, m_norm_mix, m_w_in, m_dn_conv_w, m_dn_a_log, m_dn_dt_bias, m_dn_norm_w, m_ssm_conv_w, m_ssm_conv_b, m_ssm_a_log, m_ssm_dt_bias, m_ssm_d, m_ssm_norm_w, m_w_branch, m_w_out, m_norm_mlp, m_w_up, m_w_down, m_norm_final, v_norm_mix, v_w_in, v_dn_conv_w, v_dn_a_log, v_dn_dt_bias, v_dn_norm_w, v_ssm_conv_w, v_ssm_conv_b, v_ssm_a_log, v_ssm_dt_bias, v_ssm_d, v_ssm_norm_w, v_w_branch, v_w_out, v_norm_mlp, v_w_up, v_w_down, v_norm_final):
    w = dict(norm_mix=norm_mix, w_in=w_in, dn_conv_w=dn_conv_w, dn_a_log=dn_a_log, dn_dt_bias=dn_dt_bias, dn_norm_w=dn_norm_w,
             ssm_conv_w=ssm_conv_w, ssm_conv_b=ssm_conv_b, ssm_a_log=ssm_a_log, ssm_dt_bias=ssm_dt_bias, ssm_d=ssm_d,
             ssm_norm_w=ssm_norm_w, w_branch=w_branch, w_out=w_out, norm_mlp=norm_mlp, w_up=w_up, w_down=w_down,
             norm_final=norm_final)
    m = dict(norm_mix=m_norm_mix, w_in=m_w_in, dn_conv_w=m_dn_conv_w, dn_a_log=m_dn_a_log, dn_dt_bias=m_dn_dt_bias,
             dn_norm_w=m_dn_norm_w, ssm_conv_w=m_ssm_conv_w, ssm_conv_b=m_ssm_conv_b, ssm_a_log=m_ssm_a_log,
             ssm_dt_bias=m_ssm_dt_bias, ssm_d=m_ssm_d, ssm_norm_w=m_ssm_norm_w, w_branch=m_w_branch, w_out=m_w_out,
             norm_mlp=m_norm_mlp, w_up=m_w_up, w_down=m_w_down, norm_final=m_norm_final)
    v = dict(norm_mix=v_norm_mix, w_in=v_w_in, dn_conv_w=v_dn_conv_w, dn_a_log=v_dn_a_log, dn_dt_bias=v_dn_dt_bias,
             dn_norm_w=v_dn_norm_w, ssm_conv_w=v_ssm_conv_w, ssm_conv_b=v_ssm_conv_b, ssm_a_log=v_ssm_a_log,
             ssm_dt_bias=v_ssm_dt_bias, ssm_d=v_ssm_d, ssm_norm_w=v_ssm_norm_w, w_branch=v_w_branch, w_out=v_w_out,
             norm_mlp=v_norm_mlp, w_up=v_w_up, w_down=v_w_down, norm_final=v_norm_final)
    loss, grad_x, out = _step(w, m, v, x[0], loss_target[0])
    return (loss, grad_x[None], *[out["grad"][n] for n in WEIGHTS], *[out["delta"][n] for n in WEIGHTS],
            *[out["new_m"][n] for n in WEIGHTS], *[out["new_v"][n] for n in WEIGHTS])
```

```python
import functools
import math

import jax
import jax.numpy as jnp
from jax import lax
from jax.experimental import pallas as pl
from jax.experimental.pallas import tpu as pltpu

F32 = jnp.float32
BF16 = jnp.bfloat16
MXU_DTYPE = BF16
HIGHEST = lax.Precision.HIGHEST

N_DEV = 8
DEPTH = 2
EPS = 1e-6
CONV_K = 4
DN_HEAD_DIM = 128
SB_HEAD_DIM = 64
SSM_HEAD_DIM = 64
SSM_STATE = 128
SSM_GROUPS = 4
CHUNK = 64
SB_BLOCK = 128
LANES = 128
ADAM_LR, ADAM_B1, ADAM_B2, ADAM_EPS, ADAM_WD, ADAM_STEP = 0.001, 0.9, 0.999, 1e-08, 0.01, 10
NEG_BIG = -1e30
DN_HEADS_PER_STEP = 8
SSD_GROUPS_PER_STEP = 1
SB_UNROLL = 4
SB_SPLIT = 2
CHUNK_PREC = lax.Precision.HIGH

ARB = "arbitrary"


def _cparams(n_axes):
    return pltpu.CompilerParams(dimension_semantics=(ARB,) * n_axes)


def _softplus(x):
    return jnp.maximum(x, 0.0) + jnp.log1p(jnp.exp(-jnp.abs(x)))


def _sigmoid(x):
    return 1.0 / (1.0 + jnp.exp(-x))


def _silu(x):
    return x * _sigmoid(x)


def _silu_grad(x):
    s = _sigmoid(x)
    return s * (1.0 + x * (1.0 - s))


def _dot(a, b, dims, prec=None):
    return lax.dot_general(a, b, (dims, ((), ())), precision=prec, preferred_element_type=F32)


NN = ((1,), (0,))
NT = ((1,), (1,))
TN = ((0,), (0,))


def _hdot(a, b, dims=NN):
    return _dot(a, b, dims, CHUNK_PREC)


def _bdot(a, b, dims=NN):
    return _dot(a.astype(MXU_DTYPE), b.astype(MXU_DTYPE), dims)


def _split_dot(a, m_bf16, nsplit=3):
    out = None
    rem = a
    for _ in range(nsplit):
        piece = rem.astype(BF16)
        rem = rem - piece.astype(F32)
        term = _dot(piece, m_bf16, NN)
        out = term if out is None else out + term
    return out


def _pick(n, pref):
    for t in pref:
        if n % t == 0:
            return t
    return n


def _matmul(a, b, *, ta=False, tb=False, name, epilogue=None, extras=(), out_dtypes=(F32,), col_shards=1,
            tm=None, tn=None, tk=None):
    m, k = (a.shape[1], a.shape[0]) if ta else a.shape
    k2, n = (b.shape[1], b.shape[0]) if tb else b.shape
    assert k == k2, (a.shape, b.shape, ta, tb)
    ncs = n // col_shards
    tm = tm or _pick(m, (1920, 1024, 512, 256, 128))
    tn = tn or _pick(ncs, (1920, 1024, 640, 512, 384, 256, 128))
    tk = tk or _pick(k, (1920, 1024, 640, 512, 256, 128))
    nk = k // tk
    a_spec = pl.BlockSpec((tk, tm), lambda i, j, kk: (kk, i)) if ta else pl.BlockSpec((tm, tk), lambda i, j, kk: (i, kk))
    b_spec = pl.BlockSpec((tn, tk), lambda i, j, kk: (j, kk)) if tb else pl.BlockSpec((tk, tn), lambda i, j, kk: (kk, j))
    e_spec = pl.BlockSpec((tm, tn), lambda i, j, kk: (i, j))
    if col_shards == 1:
        o_spec, o_shape = e_spec, (m, n)
    else:
        per = ncs // tn
        o_spec, o_shape = pl.BlockSpec((None, tm, tn), lambda i, j, kk: (j // per, i, j % per)), (col_shards, m, ncs)
    dims = (((0,) if ta else (1,)), ((1,) if tb else (0,)))
    n_extra = len(extras)
    n_out = len(out_dtypes)

    def body(*refs):
        a_ref, b_ref = refs[0], refs[1]
        extra_refs = refs[2:2 + n_extra]
        out_refs = refs[2 + n_extra:2 + n_extra + n_out]
        acc_ref = refs[-1]
        kk = pl.program_id(2)

        @pl.when(kk == 0)
        def _():
            acc_ref[...] = jnp.zeros_like(acc_ref)

        acc_ref[...] += _dot(a_ref[...].astype(MXU_DTYPE), b_ref[...].astype(MXU_DTYPE), dims)

        @pl.when(kk == nk - 1)
        def _():
            acc = acc_ref[...]
            outs = (acc,) if epilogue is None else epilogue(acc, *[r[...] for r in extra_refs])
            for o_ref, o in zip(out_refs, outs):
                o_ref[...] = o.astype(o_ref.dtype)

    outs = pl.pallas_call(
        body,
        grid=(m // tm, n // tn, nk),
        in_specs=[a_spec, b_spec] + [e_spec] * n_extra,
        out_specs=[o_spec] * n_out,
        out_shape=[jax.ShapeDtypeStruct(o_shape, dt) for dt in out_dtypes],
        scratch_shapes=[pltpu.VMEM((tm, tn), F32)],
        compiler_params=pltpu.CompilerParams(dimension_semantics=("parallel", "parallel", ARB)),
        name=name,
    )(a, b, *extras)
    return outs[0] if n_out == 1 else tuple(outs)


def _rms_fwd(x, w, *, name, tm=256):
    s, d = x.shape
    out_dtype = MXU_DTYPE

    def body(x_ref, w_ref, o_ref):
        xv = x_ref[...]
        r = lax.rsqrt(jnp.mean(xv * xv, axis=-1, keepdims=True) + EPS)
        o_ref[...] = (xv * r * w_ref[...]).astype(o_ref.dtype)

    return pl.pallas_call(
        body, grid=(s // tm,),
        in_specs=[pl.BlockSpec((tm, d), lambda i: (i, 0)), pl.BlockSpec((1, d), lambda i: (0, 0))],
        out_specs=pl.BlockSpec((tm, d), lambda i: (i, 0)),
        out_shape=jax.ShapeDtypeStruct((s, d), out_dtype),
        compiler_params=_cparams(1), name=name,
    )(x, w.reshape(1, d))


def _rms_bwd(x, w, dh, dres, *, name, tm=256):
    s, d = x.shape

    def body(x_ref, w_ref, dh_ref, dres_ref, dx_ref, dw_ref):
        xv = x_ref[...]
        r = lax.rsqrt(jnp.mean(xv * xv, axis=-1, keepdims=True) + EPS)
        xh = xv * r
        dhv = dh_ref[...].astype(F32)
        dxn = dhv * w_ref[...]
        dx = r * (dxn - xh * jnp.mean(dxn * xh, axis=-1, keepdims=True))
        dx_ref[...] = dres_ref[...] + dx

        @pl.when(pl.program_id(0) == 0)
        def _():
            dw_ref[...] = jnp.zeros_like(dw_ref)

        dw_ref[...] += jnp.sum(dhv * xh, axis=0, keepdims=True)

    dx, dw = pl.pallas_call(
        body, grid=(s // tm,),
        in_specs=[pl.BlockSpec((tm, d), lambda i: (i, 0)), pl.BlockSpec((1, d), lambda i: (0, 0)),
                  pl.BlockSpec((tm, d), lambda i: (i, 0)), pl.BlockSpec((tm, d), lambda i: (i, 0))],
        out_specs=[pl.BlockSpec((tm, d), lambda i: (i, 0)), pl.BlockSpec((1, d), lambda i: (0, 0))],
        out_shape=[jax.ShapeDtypeStruct((s, d), F32), jax.ShapeDtypeStruct((1, d), F32)],
        compiler_params=_cparams(1), name=name,
    )(x, w.reshape(1, d), dh, dres)
    return dx, dw.reshape(d)


def _final_loss(x, w, target, *, name, tm=256):
    s, d = x.shape

    def body(x_ref, w_ref, t_ref, loss_ref, dx_ref, dw_ref):
        xv = x_ref[...]
        r = lax.rsqrt(jnp.mean(xv * xv, axis=-1, keepdims=True) + EPS)
        xh = xv * r
        err = xh * w_ref[...] - t_ref[...]
        dy = err * (1.0 / d)
        dxn = dy * w_ref[...]
        dx_ref[...] = r * (dxn - xh * jnp.mean(dxn * xh, axis=-1, keepdims=True))

        @pl.when(pl.program_id(0) == 0)
        def _():
            dw_ref[...] = jnp.zeros_like(dw_ref)
            loss_ref[...] = jnp.zeros_like(loss_ref)

        dw_ref[...] += jnp.sum(dy * xh, axis=0, keepdims=True)
        row = jnp.sum(err * err, axis=1, keepdims=True) * (0.5 / d)
        loss_ref[...] += jnp.sum(row, axis=0, keepdims=True)

    loss, dx, dw = pl.pallas_call(
        body, grid=(s // tm,),
        in_specs=[pl.BlockSpec((tm, d), lambda i: (i, 0)), pl.BlockSpec((1, d), lambda i: (0, 0)),
                  pl.BlockSpec((tm, d), lambda i: (i, 0))],
        out_specs=[pl.BlockSpec((1, 1), lambda i: (0, 0)), pl.BlockSpec((tm, d), lambda i: (i, 0)),
                   pl.BlockSpec((1, d), lambda i: (0, 0))],
        out_shape=[jax.ShapeDtypeStruct((1, 1), F32), jax.ShapeDtypeStruct((s, d), F32), jax.ShapeDtypeStruct((1, d), F32)],
        compiler_params=_cparams(1), name=name,
    )(x, w.reshape(1, d), target)
    return loss[0, 0], dx, dw.reshape(d)


def _shift_down(x, sh, t_idx):
    return jnp.where(t_idx >= sh, pltpu.roll(x, sh, 0), 0.0)


def _shift_up(x, sh, t_idx, s):
    return jnp.where(t_idx < s - sh, pltpu.roll(x, s - sh, 0), 0.0)


def _conv_pre(x, w_rows, b, t_idx):
    c = w_rows[CONV_K - 1] * x + b
    for sh in range(1, CONV_K):
        c = c + w_rows[CONV_K - 1 - sh] * _shift_down(x, sh, t_idx)
    return c


def _conv_fwd(src, col0, w, b, n_l2, *, name):
    s = src.shape[0]
    c_tot = w.shape[1]
    nblk = c_tot // LANES

    def body(x_ref, w_ref, b_ref, o_ref):
        j = pl.program_id(0)
        t_idx = lax.broadcasted_iota(jnp.int32, (s, LANES), 0)
        w_rows = [w_ref[kk:kk + 1, :] for kk in range(CONV_K)]
        y = _silu(_conv_pre(x_ref[...], w_rows, b_ref[...], t_idx))
        if n_l2 > 0:
            yn = y * lax.rsqrt(jnp.sum(y * y, axis=1, keepdims=True) + EPS)
            y = jnp.where(j < n_l2, yn, y)
        o_ref[...] = y

    return pl.pallas_call(
        body, grid=(nblk,),
        in_specs=[pl.BlockSpec((s, LANES), lambda j: (0, col0 + j)), pl.BlockSpec((CONV_K, LANES), lambda j: (0, j)),
                  pl.BlockSpec((1, LANES), lambda j: (0, j))],
        out_specs=pl.BlockSpec((s, LANES), lambda j: (0, j)),
        out_shape=jax.ShapeDtypeStruct((s, c_tot), F32),
        compiler_params=_cparams(1), name=name,
    )(src, w, b)


def _conv_bwd(src, col0, w, b, n_l2, dout, *, name):
    s = src.shape[0]
    c_tot = w.shape[1]
    nblk = c_tot // LANES

    def body(x_ref, w_ref, b_ref, do_ref, dx_ref, dw_ref, db_ref):
        j = pl.program_id(0)
        t_idx = lax.broadcasted_iota(jnp.int32, (s, LANES), 0)
        xv = x_ref[...]
        w_rows = [w_ref[kk:kk + 1, :] for kk in range(CONV_K)]
        c = _conv_pre(xv, w_rows, b_ref[...], t_idx)
        dy = do_ref[...]
        if n_l2 > 0:
            y = _silu(c)
            r = lax.rsqrt(jnp.sum(y * y, axis=1, keepdims=True) + EPS)
            dyn = r * dy - y * (r * r * r) * jnp.sum(dy * y, axis=1, keepdims=True)
            dy = jnp.where(j < n_l2, dyn, dy)
        dc = dy * _silu_grad(c)
        dx = w_rows[CONV_K - 1] * dc
        rows = [None] * CONV_K
        rows[CONV_K - 1] = jnp.sum(dc * xv, axis=0, keepdims=True)
        for sh in range(1, CONV_K):
            dx = dx + w_rows[CONV_K - 1 - sh] * _shift_up(dc, sh, t_idx, s)
            rows[CONV_K - 1 - sh] = jnp.sum(dc * _shift_down(xv, sh, t_idx), axis=0, keepdims=True)
        dx_ref[...] = dx.astype(dx_ref.dtype)
        for kk in range(CONV_K):
            dw_ref[kk:kk + 1, :] = rows[kk]
        db_ref[...] = jnp.sum(dc, axis=0, keepdims=True)

    return pl.pallas_call(
        body, grid=(nblk,),
        in_specs=[pl.BlockSpec((s, LANES), lambda j: (0, col0 + j)), pl.BlockSpec((CONV_K, LANES), lambda j: (0, j)),
                  pl.BlockSpec((1, LANES), lambda j: (0, j)), pl.BlockSpec((s, LANES), lambda j: (0, j))],
        out_specs=[pl.BlockSpec((s, LANES), lambda j: (0, j)), pl.BlockSpec((CONV_K, LANES), lambda j: (0, j)),
                   pl.BlockSpec((1, LANES), lambda j: (0, j))],
        out_shape=[jax.ShapeDtypeStruct((s, c_tot), MXU_DTYPE), jax.ShapeDtypeStruct((CONV_K, c_tot), F32),
                   jax.ShapeDtypeStruct((1, c_tot), F32)],
        compiler_params=_cparams(1), name=name,
    )(src, w, b, dout)


def _chunk_masks(c):
    ii = lax.broadcasted_iota(jnp.int32, (c, c), 0)
    jj = lax.broadcasted_iota(jnp.int32, (c, c), 1)
    return ii, jj


def _row_to_col(row, eye):
    return jnp.sum(jnp.where(eye, row, 0.0), axis=1, keepdims=True)


def _each(f, *lists):
    return [f(*xs) for xs in zip(*lists)]


def _dn_chunk(q, k, v, a_row, b_row, alog, dtb, s0):
    c = q[0].shape[0]
    ii, jj = _chunk_masks(c)
    causal, strict, eye = ii >= jj, ii > jj, ii == jj
    g_row = _each(lambda al, a, dt: -jnp.exp(al) * _softplus(a + dt), alog, a_row, dtb)
    beta_col = _each(lambda b: _row_to_col(_sigmoid(b), eye), b_row)
    g_col = _each(lambda g: _row_to_col(g, eye), g_row)
    gc_col = _each(lambda g: jnp.sum(jnp.where(causal, g, 0.0), axis=1, keepdims=True), g_row)
    gc_row = _each(lambda g: jnp.sum(jnp.where(jj >= ii, g, 0.0), axis=0, keepdims=True), g_col)
    decay = _each(lambda gc, gr: jnp.exp(jnp.where(causal, gc - gr, NEG_BIG)), gc_col, gc_row)
    kb = _each(jnp.multiply, k, beta_col)
    vb = _each(jnp.multiply, v, beta_col)
    nmat = _each(lambda kb_, k_, dc: -jnp.where(strict, _dot(kb_, k_, NT, HIGHEST) * dc, 0.0), kb, k, decay)
    xinv = _each(lambda n: jnp.where(eye, 1.0, 0.0) + n, nmat)
    pw = nmat
    for _ in range(int(math.log2(c)) - 1):
        pw = _each(lambda p: _dot(p, p, NN, HIGHEST), pw)
        xinv = _each(lambda x, p: x + _dot(x, p, NN, HIGHEST), xinv, pw)
    egc = _each(jnp.exp, gc_col)
    u = _each(lambda x, vb_: _dot(x, vb_, NN, HIGHEST), xinv, vb)
    w = _each(lambda x, kb_, e: _dot(x, kb_ * e, NN, HIGHEST), xinv, kb, egc)
    qs = _each(lambda q_: q_ * (q_.shape[1] ** -0.5), q)
    attn = _each(lambda q_, k_, dc: _hdot(q_, k_, NT) * dc, qs, k, decay)
    gl = _each(lambda g: jnp.sum(g, axis=1, keepdims=True), g_row)
    kd = _each(lambda k_, gl_, gc: k_ * jnp.exp(gl_ - gc), k, gl, gc_col)
    v_new = _each(lambda u_, w_, s: u_ - _hdot(w_, s), u, w, s0)
    o = _each(lambda q_, e, s, at, vn: _hdot(q_ * e, s) + _hdot(at, vn), qs, egc, s0, attn, v_new)
    s1 = _each(lambda s, gl_, kd_, vn: s * jnp.exp(gl_) + _hdot(kd_, vn, TN), s0, gl, kd, v_new)
    return o, s1


def _dn_specs(nh, nc, hb, rev):
    n_of = (lambda n: nc - 1 - n) if rev else (lambda n: n)
    ng = nh // hb
    qkv = [pl.BlockSpec((CHUNK, hb * DN_HEAD_DIM), (lambda h, n, o=o: (n_of(n), o * ng + h))) for o in range(3)]
    row = pl.BlockSpec((hb, None, 1, CHUNK), lambda h, n: (h, n_of(n), 0, 0))
    scal = pl.BlockSpec((hb, 1, 1), lambda h, n: (h, 0, 0))
    o_spec = pl.BlockSpec((CHUNK, hb * DN_HEAD_DIM), lambda h, n: (n_of(n), h))
    st = pl.BlockSpec((hb, None, DN_HEAD_DIM, DN_HEAD_DIM), lambda h, n: (h, n_of(n), 0, 0))
    return qkv, row, scal, o_spec, st


def _dn_fwd(qkv, a_rows, b_rows, alog, dtb, *, name):
    s = qkv.shape[0]
    nh, nc = a_rows.shape[0], a_rows.shape[1]
    hb = min(DN_HEADS_PER_STEP, nh)
    qkv_specs, row, scal, o_spec, st = _dn_specs(nh, nc, hb, False)
    hd = DN_HEAD_DIM

    def body(q_ref, k_ref, v_ref, a_ref, b_ref, al_ref, dt_ref, o_ref, st_ref, state):
        @pl.when(pl.program_id(1) == 0)
        def _():
            state[...] = jnp.zeros_like(state)

        cols = [slice(h * hd, (h + 1) * hd) for h in range(hb)]
        s0 = [state[h] for h in range(hb)]
        for h in range(hb):
            st_ref[h] = s0[h]
        o, s1 = _dn_chunk([q_ref[:, cl] for cl in cols], [k_ref[:, cl] for cl in cols], [v_ref[:, cl] for cl in cols],
                          [a_ref[h] for h in range(hb)], [b_ref[h] for h in range(hb)],
                          [al_ref[h] for h in range(hb)], [dt_ref[h] for h in range(hb)], s0)
        for h in range(hb):
            o_ref[:, cols[h]] = o[h]
            state[h] = s1[h]

    return pl.pallas_call(
        body, grid=(nh // hb, nc),
        in_specs=qkv_specs + [row, row, scal, scal],
        out_specs=[o_spec, st],
        out_shape=[jax.ShapeDtypeStruct((s, nh * hd), F32), jax.ShapeDtypeStruct((nh, nc, hd, hd), F32)],
        scratch_shapes=[pltpu.VMEM((hb, hd, hd), F32)],
        compiler_params=_cparams(2), name=name,
    )(qkv, qkv, qkv, a_rows, b_rows, alog, dtb)


def _dn_bwd(qkv, a_rows, b_rows, alog, dtb, states, do, *, name):
    s = qkv.shape[0]
    nh, nc = a_rows.shape[0], a_rows.shape[1]
    hb = min(DN_HEADS_PER_STEP, nh)
    qkv_specs, row, scal, o_spec, st = _dn_specs(nh, nc, hb, True)
    hd = DN_HEAD_DIM

    def body(q_ref, k_ref, v_ref, a_ref, b_ref, al_ref, dt_ref, st_ref, do_ref,
             dq_ref, dk_ref, dv_ref, da_ref, db_ref, dal_ref, ddt_ref, dstate):
        @pl.when(pl.program_id(1) == 0)
        def _():
            dstate[...] = jnp.zeros_like(dstate)
            dal_ref[...] = jnp.zeros_like(dal_ref)
            ddt_ref[...] = jnp.zeros_like(ddt_ref)

        cols = [slice(h * hd, (h + 1) * hd) for h in range(hb)]
        heads = range(hb)
        args = ([q_ref[:, cl] for cl in cols], [k_ref[:, cl] for cl in cols], [v_ref[:, cl] for cl in cols],
                [a_ref[h] for h in heads], [b_ref[h] for h in heads], [al_ref[h] for h in heads],
                [dt_ref[h] for h in heads], [st_ref[h] for h in heads])
        _, vjp = jax.vjp(_dn_chunk, *args)
        dq, dk, dv, da, db, dal, ddt, ds0 = vjp(([do_ref[:, cl] for cl in cols], [dstate[h] for h in heads]))
        for h in heads:
            dq_ref[:, cols[h]] = dq[h]
            dk_ref[:, cols[h]] = dk[h]
            dv_ref[:, cols[h]] = dv[h]
            da_ref[h] = da[h]
            db_ref[h] = db[h]
            dal_ref[h] += dal[h]
            ddt_ref[h] += ddt[h]
            dstate[h] = ds0[h]

    w = nh * hd
    outs = pl.pallas_call(
        body, grid=(nh // hb, nc),
        in_specs=qkv_specs + [row, row, scal, scal, st, o_spec],
        out_specs=[o_spec, o_spec, o_spec, row, row, scal, scal],
        out_shape=[jax.ShapeDtypeStruct((s, w), F32)] * 3
        + [jax.ShapeDtypeStruct(a_rows.shape, F32)] * 2 + [jax.ShapeDtypeStruct((nh, 1, 1), F32)] * 2,
        scratch_shapes=[pltpu.VMEM((hb, hd, hd), F32)],
        compiler_params=_cparams(2), name=name,
    )(qkv, qkv, qkv, a_rows, b_rows, alog, dtb, states, do)
    return outs


def _dn_post_fwd(o, src, gate_col0, nw, *, name, tm=256):
    s, w = o.shape
    nh = w // DN_HEAD_DIM

    def body(o_ref, g_ref, w_ref, y_ref):
        ov = o_ref[...]
        r = lax.rsqrt(jnp.mean(ov * ov, axis=-1, keepdims=True) + EPS)
        y_ref[...] = (ov * r * w_ref[...] * _silu(g_ref[...])).astype(y_ref.dtype)

    blk = pl.BlockSpec((tm, DN_HEAD_DIM), lambda i, h: (i, h))
    return pl.pallas_call(
        body, grid=(s // tm, nh),
        in_specs=[blk, pl.BlockSpec((tm, DN_HEAD_DIM), lambda i, h: (i, gate_col0 + h)),
                  pl.BlockSpec((1, DN_HEAD_DIM), lambda i, h: (0, 0))],
        out_specs=blk, out_shape=jax.ShapeDtypeStruct((s, w), MXU_DTYPE),
        compiler_params=_cparams(2), name=name,
    )(o, src, nw.reshape(1, DN_HEAD_DIM))


def _dn_post_bwd(o, src, gate_col0, nw, dy, *, name, tm=256):
    s, w = o.shape
    nh = w // DN_HEAD_DIM

    def body(o_ref, g_ref, w_ref, dy_ref, do_ref, dg_ref, dw_ref):
        ov = o_ref[...]
        gv = g_ref[...]
        dyv = dy_ref[...]
        r = lax.rsqrt(jnp.mean(ov * ov, axis=-1, keepdims=True) + EPS)
        oh = ov * r
        dn = dyv * _silu(gv)
        dg_ref[...] = (dyv * (oh * w_ref[...]) * _silu_grad(gv)).astype(dg_ref.dtype)
        don = dn * w_ref[...]
        do_ref[...] = r * (don - oh * jnp.mean(don * oh, axis=-1, keepdims=True))

        @pl.when((pl.program_id(0) == 0) & (pl.program_id(1) == 0))
        def _():
            dw_ref[...] = jnp.zeros_like(dw_ref)

        dw_ref[...] += jnp.sum(dn * oh, axis=0, keepdims=True)

    blk = pl.BlockSpec((tm, DN_HEAD_DIM), lambda i, h: (i, h))
    wspec = pl.BlockSpec((1, DN_HEAD_DIM), lambda i, h: (0, 0))
    do, dg, dw = pl.pallas_call(
        body, grid=(s // tm, nh),
        in_specs=[blk, pl.BlockSpec((tm, DN_HEAD_DIM), lambda i, h: (i, gate_col0 + h)), wspec, blk],
        out_specs=[blk, blk, wspec],
        out_shape=[jax.ShapeDtypeStruct((s, w), F32), jax.ShapeDtypeStruct((s, w), MXU_DTYPE),
                   jax.ShapeDtypeStruct((1, DN_HEAD_DIM), F32)],
        compiler_params=_cparams(2), name=name,
    )(o, src, nw.reshape(1, DN_HEAD_DIM), dy)
    return do, dg, dw.reshape(DN_HEAD_DIM)


def _sb_consts():
    r2 = lax.broadcasted_iota(jnp.int32, (2 * SB_BLOCK, SB_BLOCK), 0)
    c2 = lax.broadcasted_iota(jnp.int32, (2 * SB_BLOCK, SB_BLOCK), 1)
    r = lax.broadcasted_iota(jnp.int32, (SB_BLOCK, SB_BLOCK), 0)
    c = lax.broadcasted_iota(jnp.int32, (SB_BLOCK, SB_BLOCK), 1)
    lm0 = c < SB_HEAD_DIM
    m_gt = jnp.where(r > c, 1.0, 0.0).astype(BF16)
    m_lt = jnp.where(r < c, 1.0, 0.0).astype(BF16)
    return r2, c2, lm0, m_gt, m_lt


def _sb_stack(x, lm0):
    return jnp.concatenate([jnp.where(lm0, x, 0.0), jnp.where(lm0, 0.0, x)], axis=0)


def _sb_unstack(x2, lm0):
    return jnp.where(lm0, x2[:SB_BLOCK], x2[SB_BLOCK:])


def _sb_fwd(src, col0, width, *, name):
    s = src.shape[0]
    nq = s // SB_BLOCK
    npair = width // LANES
    scale = SB_HEAD_DIM ** -0.5
    nu = math.gcd(SB_UNROLL, nq)

    def body(q_ref, k_ref, v_ref, o_ref, r_ref):
        i = pl.program_id(1)
        r2, c2, lm0, m_gt, _ = _sb_consts()
        t_glob = i * SB_BLOCK + (r2 & (SB_BLOCK - 1))
        q2 = (_sb_stack(q_ref[...], lm0) * scale).astype(MXU_DTYPE)

        def group(base, carry, masked):
            o2, rsum = carry
            js = [base + nu - 1 - u for u in range(nu)]
            offs = [pl.multiple_of(j * SB_BLOCK, SB_BLOCK) for j in js]
            zs = [_dot(q2, k_ref[pl.ds(off, SB_BLOCK), :].astype(MXU_DTYPE), NT) for off in offs]
            ts = [jnp.log(1.0 + jnp.exp(-jnp.abs(z))) for z in zs]
            lks = [-(jnp.maximum(z, 0.0) + t) for z, t in zip(zs, ts)]
            if masked:
                masks = [(j * SB_BLOCK + c2) < t_glob for j in js]
                lks = [jnp.where(mk, lk, 0.0) for mk, lk in zip(masks, lks)]
            sufs = [_split_dot(lk, m_gt, SB_SPLIT) for lk in lks]
            rs = [rsum]
            for lk in lks:
                rs.append(rs[-1] + jnp.sum(lk, axis=1, keepdims=True))
            wgts = [jnp.exp((jnp.minimum(z, 0.0) - t) + r_ + sf) for z, t, r_, sf in zip(zs, ts, rs, sufs)]
            if masked:
                wgts = [jnp.where(mk, wg, 0.0) for mk, wg in zip(masks, wgts)]
            for off, wg in zip(offs, wgts):
                o2 = o2 + _dot(wg.astype(MXU_DTYPE), v_ref[pl.ds(off, SB_BLOCK), :].astype(MXU_DTYPE), NN)
            return o2, rs[-1]

        top0 = (i // nu) * nu
        carry = group(top0, (jnp.zeros((2 * SB_BLOCK, LANES), F32), jnp.zeros((2 * SB_BLOCK, 1), F32)), True)
        o2, rsum = lax.fori_loop(1, i // nu + 1, lambda g, cr: group(top0 - nu * g, cr, False), carry)
        o_ref[...] = _sb_unstack(o2, lm0)
        r_ref[...] = _sb_unstack(jnp.broadcast_to(rsum, (2 * SB_BLOCK, LANES)), lm0)

    blk = pl.BlockSpec((SB_BLOCK, LANES), lambda p, i: (i, p))
    return pl.pallas_call(
        body, grid=(npair, nq),
        in_specs=[pl.BlockSpec((SB_BLOCK, LANES), lambda p, i: (i, col0 + p)),
                  pl.BlockSpec((s, LANES), lambda p, i: (0, col0 + npair + p)),
                  pl.BlockSpec((s, LANES), lambda p, i: (0, col0 + 2 * npair + p))],
        out_specs=[blk, blk],
        out_shape=[jax.ShapeDtypeStruct((s, width), F32), jax.ShapeDtypeStruct((s, width), F32)],
        compiler_params=_cparams(2), name=name,
    )(src, src, src)


def _sb_bwd(src, col0, width, rtot, do, *, name):
    s = src.shape[0]
    nq = s // SB_BLOCK
    npair = width // LANES
    scale = SB_HEAD_DIM ** -0.5
    nu = math.gcd(SB_UNROLL, nq)

    def body(q_ref, k_ref, v_ref, r_ref, do_ref, dq_ref, dk_ref, dv_ref):
        i = pl.program_id(1)

        @pl.when(i == 0)
        def _():
            dk_ref[...] = jnp.zeros_like(dk_ref)
            dv_ref[...] = jnp.zeros_like(dv_ref)

        r2, c2, lm0, m_gt, m_lt = _sb_consts()
        t_glob = i * SB_BLOCK + (r2 & (SB_BLOCK - 1))
        q2 = (_sb_stack(q_ref[...], lm0) * scale).astype(MXU_DTYPE)
        do2 = _sb_stack(do_ref[...], lm0).astype(MXU_DTYPE)
        rv = r_ref[...]
        rt = jnp.concatenate([jnp.max(jnp.where(lm0, rv, NEG_BIG), axis=1, keepdims=True),
                              jnp.max(jnp.where(lm0, NEG_BIG, rv), axis=1, keepdims=True)], axis=0)

        def group(g, carry, masked):
            dq2, psum, csum = carry
            js = [nu * g + u for u in range(nu)]
            offs = [pl.multiple_of(j * SB_BLOCK, SB_BLOCK) for j in js]
            kbs = [k_ref[pl.ds(off, SB_BLOCK), :].astype(MXU_DTYPE) for off in offs]
            zs = [_dot(q2, kb, NT) for kb in kbs]
            dws = [_dot(do2, v_ref[pl.ds(off, SB_BLOCK), :].astype(MXU_DTYPE), NT) for off in offs]
            ts = [jnp.log(1.0 + jnp.exp(-jnp.abs(z))) for z in zs]
            lks = [-(jnp.maximum(z, 0.0) + t) for z, t in zip(zs, ts)]
            if masked:
                masks = [(j * SB_BLOCK + c2) < t_glob for j in js]
                lks = [jnp.where(mk, lk, 0.0) for mk, lk in zip(masks, lks)]
            sufs = [_split_dot(lk, m_gt, SB_SPLIT) for lk in lks]
            lsums = [jnp.sum(lk, axis=1, keepdims=True) for lk in lks]
            logsigs = [jnp.minimum(z, 0.0) - t for z, t in zip(zs, ts)]
            wgts = []
            for lsg, lsum, sf in zip(logsigs, lsums, sufs):
                psum = psum + lsum
                wgts.append(jnp.exp(lsg + (rt - psum) + sf))
            if masked:
                wgts = [jnp.where(mk, wg, 0.0) for mk, wg in zip(masks, wgts)]
            dlogas = [wg * dw for wg, dw in zip(wgts, dws)]
            pres = [_split_dot(dl, m_lt, SB_SPLIT) for dl in dlogas]
            dlks = []
            for dl, pre in zip(dlogas, pres):
                dlks.append(csum + pre)
                csum = csum + jnp.sum(dl, axis=1, keepdims=True)
            if masked:
                dlks = [jnp.where(mk, dlk, 0.0) for mk, dlk in zip(masks, dlks)]
            sigs = [jnp.exp(lsg) for lsg in logsigs]
            dzbs = [(dl * (1.0 - sg) - dlk * sg).astype(MXU_DTYPE) for dl, sg, dlk in zip(dlogas, sigs, dlks)]
            for off, dzb, wg, kb in zip(offs, dzbs, wgts, kbs):
                dk_ref[pl.ds(off, SB_BLOCK), :] += _dot(dzb, q2, TN)
                dv_ref[pl.ds(off, SB_BLOCK), :] += _dot(wg.astype(MXU_DTYPE), do2, TN)
                dq2 = dq2 + _dot(dzb, kb, NN)
            return dq2, psum, csum

        zero_col = jnp.zeros((2 * SB_BLOCK, 1), F32)
        carry = lax.fori_loop(0, i // nu, lambda g, cr: group(g, cr, False),
                              (jnp.zeros((2 * SB_BLOCK, LANES), F32), zero_col, zero_col))
        dq2, _, _ = group(i // nu, carry, True)
        dq_ref[...] = _sb_unstack(dq2, lm0) * scale

    blk = pl.BlockSpec((SB_BLOCK, LANES), lambda p, i: (i, p))
    full = pl.BlockSpec((s, LANES), lambda p, i: (0, p))
    return pl.pallas_call(
        body, grid=(npair, nq),
        in_specs=[pl.BlockSpec((SB_BLOCK, LANES), lambda p, i: (i, col0 + p)),
                  pl.BlockSpec((s, LANES), lambda p, i: (0, col0 + npair + p)),
                  pl.BlockSpec((s, LANES), lambda p, i: (0, col0 + 2 * npair + p)),
                  blk, blk],
        out_specs=[blk, full, full],
        out_shape=[jax.ShapeDtypeStruct((s, width), F32)] * 3,
        compiler_params=_cparams(2), name=name,
    )(src, src, src, rtot, do)


def _ssd_group(xs, dt_rows, alogs, dtbs, bms, cms, h0s):
    c = bms[0].shape[0]
    per = len(xs) // len(bms)
    ii, jj = _chunk_masks(c)
    causal, eye = ii >= jj, ii == jj
    grp = lambda per_group: [t for t in per_group for _ in range(per)]
    scores, bm, cm = grp(_each(lambda c_, b_: _hdot(c_, b_, NT), cms, bms)), grp(bms), grp(cms)
    dt_r = _each(lambda dt, b: _softplus(dt + b), dt_rows, dtbs)
    a_r = _each(lambda al, dt: -jnp.exp(al) * dt, alogs, dt_r)
    dt_col = _each(lambda dt: _row_to_col(dt, eye), dt_r)
    a_col = _each(lambda a: _row_to_col(a, eye), a_r)
    ac_col = _each(lambda a: jnp.sum(jnp.where(causal, a, 0.0), axis=1, keepdims=True), a_r)
    ac_row = _each(lambda a: jnp.sum(jnp.where(jj >= ii, a, 0.0), axis=0, keepdims=True), a_col)
    lmat = _each(lambda c_, r_: jnp.exp(jnp.where(causal, c_ - r_, NEG_BIG)), ac_col, ac_row)
    xdt = _each(jnp.multiply, xs, dt_col)
    al = _each(lambda a: jnp.sum(a, axis=1, keepdims=True), a_r)
    ys = _each(lambda sc, lm, xd, cm_, h0, ac: _hdot(sc * lm, xd) + _hdot(cm_, h0, NT) * jnp.exp(ac),
               scores, lmat, xdt, cm, h0s, ac_col)
    h1s = _each(lambda h0, al_, xd, ac, bm_: h0 * jnp.exp(al_) + _hdot(xd * jnp.exp(al_ - ac), bm_, TN),
                h0s, al, xdt, ac_col, bm)
    return ys, h1s


def _ssd_specs(ng, nc, r, gb, rev):
    n_of = (lambda n: nc - 1 - n) if rev else (lambda n: n)
    xw, bw = gb * r * SSM_HEAD_DIM, gb * SSM_STATE
    b0, c0 = (ng * r * SSM_HEAD_DIM) // bw, (ng * r * SSM_HEAD_DIM + ng * SSM_STATE) // bw
    x_spec = pl.BlockSpec((CHUNK, xw), lambda g, n: (n_of(n), g))
    b_spec = pl.BlockSpec((CHUNK, bw), lambda g, n: (n_of(n), b0 + g))
    c_spec = pl.BlockSpec((CHUNK, bw), lambda g, n: (n_of(n), c0 + g))
    dt_spec = pl.BlockSpec((gb, None, r, CHUNK), lambda g, n: (g, n_of(n), 0, 0))
    sc_spec = pl.BlockSpec((gb, r, 1), lambda g, n: (g, 0, 0))
    st_spec = pl.BlockSpec((gb, None, r, SSM_HEAD_DIM, SSM_STATE), lambda g, n: (g, n_of(n), 0, 0, 0))
    bc_out = pl.BlockSpec((CHUNK, bw), lambda g, n: (n_of(n), g))
    return x_spec, b_spec, c_spec, dt_spec, sc_spec, st_spec, x_spec, bc_out


def _ssd_refs(gb, r, x_ref, b_ref, c_ref, dt_ref, al_ref, db_ref):
    p, n = SSM_HEAD_DIM, SSM_STATE
    heads = [(g, h) for g in range(gb) for h in range(r)]
    xs = [x_ref[:, (g * r + h) * p:(g * r + h + 1) * p] for g, h in heads]
    dts = [dt_ref[g, h:h + 1, :] for g, h in heads]
    als = [al_ref[g, h:h + 1, :] for g, h in heads]
    dbs = [db_ref[g, h:h + 1, :] for g, h in heads]
    bms = [b_ref[:, g * n:(g + 1) * n] for g in range(gb)]
    cms = [c_ref[:, g * n:(g + 1) * n] for g in range(gb)]
    return heads, xs, dts, als, dbs, bms, cms


def _ssd_fwd(xbc, dt_rows, alog, dtb, *, name):
    s = xbc.shape[0]
    ng, nc, r = dt_rows.shape[0], dt_rows.shape[1], dt_rows.shape[2]
    w = ng * r * SSM_HEAD_DIM
    gb = math.gcd(SSD_GROUPS_PER_STEP, ng)
    x_spec, b_spec, c_spec, dt_spec, sc_spec, st_spec, y_spec, _ = _ssd_specs(ng, nc, r, gb, False)
    p = SSM_HEAD_DIM

    def body(x_ref, b_ref, c_ref, dt_ref, al_ref, db_ref, y_ref, st_ref, state):
        @pl.when(pl.program_id(1) == 0)
        def _():
            state[...] = jnp.zeros_like(state)

        st_ref[...] = state[...]
        heads, xs, dts, als, dbs, bms, cms = _ssd_refs(gb, r, x_ref, b_ref, c_ref, dt_ref, al_ref, db_ref)
        ys, h1s = _ssd_group(xs, dts, als, dbs, bms, cms, [state[g, h] for g, h in heads])
        for i, (g, h) in enumerate(heads):
            y_ref[:, (g * r + h) * p:(g * r + h + 1) * p] = ys[i]
            state[g, h] = h1s[i]

    return pl.pallas_call(
        body, grid=(ng // gb, nc),
        in_specs=[x_spec, b_spec, c_spec, dt_spec, sc_spec, sc_spec],
        out_specs=[y_spec, st_spec],
        out_shape=[jax.ShapeDtypeStruct((s, w), F32), jax.ShapeDtypeStruct((ng, nc, r, p, SSM_STATE), F32)],
        scratch_shapes=[pltpu.VMEM((gb, r, p, SSM_STATE), F32)],
        compiler_params=_cparams(2), name=name,
    )(xbc, xbc, xbc, dt_rows, alog, dtb)


def _ssd_bwd(xbc, dt_rows, alog, dtb, states, dy, *, name):
    s = xbc.shape[0]
    ng, nc, r = dt_rows.shape[0], dt_rows.shape[1], dt_rows.shape[2]
    w = ng * r * SSM_HEAD_DIM
    gb = math.gcd(SSD_GROUPS_PER_STEP, ng)
    x_spec, b_spec, c_spec, dt_spec, sc_spec, st_spec, y_spec, bc_out = _ssd_specs(ng, nc, r, gb, True)
    p = SSM_HEAD_DIM

    def body(x_ref, b_ref, c_ref, dt_ref, al_ref, db_ref, st_ref, dy_ref,
             dx_ref, dbm_ref, dcm_ref, ddt_ref, dal_ref, ddb_ref, dstate):
        @pl.when(pl.program_id(1) == 0)
        def _():
            dstate[...] = jnp.zeros_like(dstate)
            dal_ref[...] = jnp.zeros_like(dal_ref)
            ddb_ref[...] = jnp.zeros_like(ddb_ref)

        heads, xs, dts, als, dbs, bms, cms = _ssd_refs(gb, r, x_ref, b_ref, c_ref, dt_ref, al_ref, db_ref)
        _, vjp = jax.vjp(_ssd_group, xs, dts, als, dbs, bms, cms, [st_ref[g, h] for g, h in heads])
        dys = [dy_ref[:, (g * r + h) * p:(g * r + h + 1) * p] for g, h in heads]
        dxs, ddts, dals, ddbs, dbms, dcms, dh0s = vjp((dys, [dstate[g, h] for g, h in heads]))
        for g in range(gb):
            dbm_ref[:, g * SSM_STATE:(g + 1) * SSM_STATE] = dbms[g]
            dcm_ref[:, g * SSM_STATE:(g + 1) * SSM_STATE] = dcms[g]
        for i, (g, h) in enumerate(heads):
            dx_ref[:, (g * r + h) * p:(g * r + h + 1) * p] = dxs[i]
            ddt_ref[g, h:h + 1, :] = ddts[i]
            dal_ref[g, h:h + 1, :] += dals[i]
            ddb_ref[g, h:h + 1, :] += ddbs[i]
            dstate[g, h] = dh0s[i]

    gn = ng * SSM_STATE
    return pl.pallas_call(
        body, grid=(ng // gb, nc),
        in_specs=[x_spec, b_spec, c_spec, dt_spec, sc_spec, sc_spec, st_spec, y_spec],
        out_specs=[y_spec, bc_out, bc_out, dt_spec, sc_spec, sc_spec],
        out_shape=[jax.ShapeDtypeStruct((s, w), F32), jax.ShapeDtypeStruct((s, gn), F32), jax.ShapeDtypeStruct((s, gn), F32),
                   jax.ShapeDtypeStruct(dt_rows.shape, F32), jax.ShapeDtypeStruct((ng, r, 1), F32),
                   jax.ShapeDtypeStruct((ng, r, 1), F32)],
        scratch_shapes=[pltpu.VMEM((gb, r, p, SSM_STATE), F32)],
        compiler_params=_cparams(2), name=name,
    )(xbc, xbc, xbc, dt_rows, alog, dtb, states, dy)


def _ssm_post_fwd(y, xbc, src, z_col0, dexp, nw, *, name, tm=256):
    s, w = y.shape
    gw = w // SSM_GROUPS
    zc = z_col0 * LANES // gw

    def body(y_ref, x_ref, z_ref, d_ref, w_ref, o_ref):
        yy = (y_ref[...] + x_ref[...] * d_ref[...]) * _silu(z_ref[...])
        r = lax.rsqrt(jnp.mean(yy * yy, axis=-1, keepdims=True) + EPS)
        o_ref[...] = (yy * r * w_ref[...]).astype(o_ref.dtype)

    blk = pl.BlockSpec((tm, gw), lambda g, i: (i, g))
    vec = pl.BlockSpec((1, gw), lambda g, i: (0, g))
    return pl.pallas_call(
        body, grid=(SSM_GROUPS, s // tm),
        in_specs=[blk, blk, pl.BlockSpec((tm, gw), lambda g, i: (i, zc + g)), vec, vec],
        out_specs=blk, out_shape=jax.ShapeDtypeStruct((s, w), MXU_DTYPE),
        compiler_params=_cparams(2), name=name,
    )(y, xbc, src, dexp.reshape(1, w), nw.reshape(1, w))


def _ssm_post_bwd(y, xbc, src, z_col0, dexp, nw, dout, *, name, tm=256):
    s, w = y.shape
    gw = w // SSM_GROUPS
    zc = z_col0 * LANES // gw

    def body(y_ref, x_ref, z_ref, d_ref, w_ref, do_ref, dy_ref, dx_ref, dz_ref, dd_ref, dw_ref):
        xv, zv, dv = x_ref[...], z_ref[...], d_ref[...]
        pre = y_ref[...] + xv * dv
        sz = _silu(zv)
        yy = pre * sz
        r = lax.rsqrt(jnp.mean(yy * yy, axis=-1, keepdims=True) + EPS)
        yh = yy * r
        dov = do_ref[...]
        dyn = dov * w_ref[...]
        dyy = r * (dyn - yh * jnp.mean(dyn * yh, axis=-1, keepdims=True))
        dpre = dyy * sz
        dy_ref[...] = dpre
        dx_ref[...] = dpre * dv
        dz_ref[...] = (dyy * pre * _silu_grad(zv)).astype(dz_ref.dtype)

        @pl.when(pl.program_id(1) == 0)
        def _():
            dd_ref[...] = jnp.zeros_like(dd_ref)
            dw_ref[...] = jnp.zeros_like(dw_ref)

        dd_ref[...] += jnp.sum(dpre * xv, axis=0, keepdims=True)
        dw_ref[...] += jnp.sum(dov * yh, axis=0, keepdims=True)

    blk = pl.BlockSpec((tm, gw), lambda g, i: (i, g))
    vec = pl.BlockSpec((1, gw), lambda g, i: (0, g))
    dy, dx, dz, dd, dw = pl.pallas_call(
        body, grid=(SSM_GROUPS, s // tm),
        in_specs=[blk, blk, pl.BlockSpec((tm, gw), lambda g, i: (i, zc + g)), vec, vec, blk],
        out_specs=[blk, blk, blk, vec, vec],
        out_shape=[jax.ShapeDtypeStruct((s, w), F32), jax.ShapeDtypeStruct((s, w), F32), jax.ShapeDtypeStruct((s, w), MXU_DTYPE),
                   jax.ShapeDtypeStruct((1, w), F32), jax.ShapeDtypeStruct((1, w), F32)],
        compiler_params=_cparams(2), name=name,
    )(y, xbc, src, dexp.reshape(1, w), nw.reshape(1, w), dout)
    return dy, dx, dz, dd.reshape(w), dw.reshape(w)


def _merge_fwd(proj3, src, gate_col0, d, *, name, tm=256):
    s = proj3.shape[0]
    nb = proj3.shape[1] // d
    gc = gate_col0 * LANES // d

    def body(*refs):
        p_refs, g_refs, o_ref = refs[:nb], refs[nb:2 * nb], refs[-1]
        acc = None
        for p_ref, g_ref in zip(p_refs, g_refs):
            term = _sigmoid(g_ref[...]) * p_ref[...]
            acc = term if acc is None else acc + term
        o_ref[...] = acc.astype(o_ref.dtype)

    p_specs = [pl.BlockSpec((tm, d), lambda i, b=b: (i, b)) for b in range(nb)]
    g_specs = [pl.BlockSpec((tm, d), lambda i, b=b: (i, gc + b)) for b in range(nb)]
    return pl.pallas_call(
        body, grid=(s // tm,), in_specs=p_specs + g_specs,
        out_specs=pl.BlockSpec((tm, d), lambda i: (i, 0)), out_shape=jax.ShapeDtypeStruct((s, d), MXU_DTYPE),
        compiler_params=_cparams(1), name=name,
    )(*([proj3] * nb), *([src] * nb))


def _merge_bwd(proj3, src, gate_col0, d, dmerged, *, name, tm=256):
    s = proj3.shape[0]
    nb = proj3.shape[1] // d
    gc = gate_col0 * LANES // d

    def body(p_ref, g_ref, dm_ref, dp_ref, dg_ref):
        sg = _sigmoid(g_ref[...])
        dm = dm_ref[...]
        dp_ref[...] = (dm * sg).astype(dp_ref.dtype)
        dg_ref[...] = (dm * p_ref[...] * sg * (1.0 - sg)).astype(dg_ref.dtype)

    blk = pl.BlockSpec((tm, d), lambda i, b: (i, b))
    return pl.pallas_call(
        body, grid=(s // tm, nb),
        in_specs=[blk, pl.BlockSpec((tm, d), lambda i, b: (i, gc + b)), pl.BlockSpec((tm, d), lambda i, b: (i, 0))],
        out_specs=[blk, blk],
        out_shape=[jax.ShapeDtypeStruct(proj3.shape, MXU_DTYPE), jax.ShapeDtypeStruct(proj3.shape, MXU_DTYPE)],
        compiler_params=_cparams(2), name=name,
    )(proj3, src, dmerged)


ANY = pl.BlockSpec(memory_space=pl.ANY)
MESH = pl.DeviceIdType.MESH


def _all_gather(shards, *, name, after=None):
    nt = len(shards)
    n_after = 0 if after is None else 1

    def body(*refs):
        x_refs, out_refs = refs[:nt], refs[nt + n_after:2 * nt + n_after]
        send_sems, recv_sems, local_sems = refs[2 * nt + n_after:]
        x, y, c = lax.axis_index("x"), lax.axis_index("y"), lax.axis_index("c")
        me, sibling = (x, y, c), (x, y, 1 - c)
        chips = [(1 - x, y), (x, 1 - y), (1 - x, 1 - y)]

        def slot(t, px, py, pc):
            return out_refs[t].at[4 * px + 2 * py + pc]

        def copy(t, k, block, to, from_input=False):
            return pltpu.make_async_remote_copy(
                src_ref=x_refs[t] if from_input else slot(t, *block), dst_ref=slot(t, *block),
                send_sem=send_sems.at[7 * t + k], recv_sem=recv_sems.at[7 * t + k], device_id=to, device_id_type=MESH)

        mine = [pltpu.make_async_copy(x_refs[t], slot(t, *me), local_sems.at[t]) for t in range(nt)]
        for cp in mine:
            cp.start()
        first = [copy(t, 0, me, sibling, True) for t in range(nt)]
        first += [copy(t, 1 + j, me, (*chip, c), True) for j, chip in enumerate(chips) for t in range(nt)]
        for cp in first:
            cp.start()
        passed = []
        for j, chip in enumerate(chips):
            for t in range(nt):
                copy(t, 1 + j, (*chip, c), me).wait_recv()
                fwd = copy(t, 4 + j, (*chip, c), sibling)
                fwd.start()
                passed.append(fwd)
        for t in range(nt):
            copy(t, 0, sibling, me).wait_recv()
            for j, chip in enumerate(chips):
                copy(t, 4 + j, (*chip, 1 - c), me).wait_recv()
        for cp in first + passed:
            cp.wait_send()
        for cp in mine:
            cp.wait()

    return pl.pallas_call(
        body, out_shape=[jax.ShapeDtypeStruct((N_DEV,) + a.shape, a.dtype) for a in shards],
        in_specs=[ANY] * (nt + n_after), out_specs=[ANY] * nt,
        scratch_shapes=[pltpu.SemaphoreType.DMA((7 * nt,)), pltpu.SemaphoreType.DMA((7 * nt,)),
                        pltpu.SemaphoreType.DMA((nt,))],
        name=name,
    )(*shards, *([] if after is None else [after]))


def _grad_exchange(bigs, small, *, name):
    nl = len(bigs[0])
    flat = [a for per_layer in bigs for a in per_layer]
    nslot = len(flat)

    def body(*refs):
        in_refs, small_ref = refs[:nslot], refs[nslot]
        out_refs, smallr_ref = refs[nslot + 1:nslot + 1 + len(bigs)], refs[nslot + 1 + len(bigs)]
        send_sems, recv_sems, local_sems = refs[nslot + 2 + len(bigs):]
        x, y, c = lax.axis_index("x"), lax.axis_index("y"), lax.axis_index("c")
        me = 4 * x + 2 * y + c
        local = [pltpu.make_async_copy(in_refs[i].at[me], out_refs[i // nl].at[me, i % nl], local_sems.at[i])
                 for i in range(nslot)]
        local.append(pltpu.make_async_copy(small_ref, smallr_ref.at[me], local_sems.at[nslot]))
        for cp in local:
            cp.start()
        copies = []
        for k in range(1, N_DEV):
            px = x ^ ((k >> 2) & 1)
            py = y ^ ((k >> 1) & 1)
            pc = c ^ (k & 1)
            peer = 4 * px + 2 * py + pc
            for i in range(nslot + 1):
                sem = 7 * i + (k - 1)
                src = in_refs[i].at[peer] if i < nslot else small_ref
                dst = out_refs[i // nl].at[me, i % nl] if i < nslot else smallr_ref.at[me]
                copies.append(pltpu.make_async_remote_copy(
                    src_ref=src, dst_ref=dst, send_sem=send_sems.at[sem], recv_sem=recv_sems.at[sem],
                    device_id=(px, py, pc), device_id_type=MESH))
        for cp in copies:
            cp.start()
        for cp in copies:
            cp.wait_recv()
        for cp in copies:
            cp.wait_send()
        for cp in local:
            cp.wait()

    out_shape = [jax.ShapeDtypeStruct((N_DEV, nl) + per_layer[0].shape[1:], per_layer[0].dtype) for per_layer in bigs]
    out_shape.append(jax.ShapeDtypeStruct((N_DEV,) + small.shape, small.dtype))
    nsem = 7 * (nslot + 1)
    outs = pl.pallas_call(
        body, out_shape=out_shape,
        in_specs=[ANY] * (nslot + 1), out_specs=[ANY] * (len(bigs) + 1),
        scratch_shapes=[pltpu.SemaphoreType.DMA((nsem,)), pltpu.SemaphoreType.DMA((nsem,)),
                        pltpu.SemaphoreType.DMA((nslot + 1,))],
        name=name,
    )(*flat, small)
    return outs[:-1], outs[-1]


HBM = pl.BlockSpec(memory_space=pltpu.HBM)
SEM = pl.BlockSpec(memory_space=pltpu.SEMAPHORE)
EFFECT = pltpu.SideEffectType.DATAFLOW_SIDE_EFFECTING


def _peers():
    x, y, c = lax.axis_index("x"), lax.axis_index("y"), lax.axis_index("c")
    peers = []
    for k in range(1, N_DEV):
        px, py, pc = x ^ ((k >> 2) & 1), y ^ ((k >> 1) & 1), c ^ (k & 1)
        peers.append(((px, py, pc), 4 * px + 2 * py + pc))
    return 4 * x + 2 * y + c, peers


def _split_copies(slots, src_refs, land_refs, send_sems, recv_sems):
    me, peers = _peers()
    copies = []
    for t, (whole, layer) in enumerate(slots):
        dst = land_refs[t].at[me] if layer is None else land_refs[t].at[me, layer]
        for k, (dev, lin) in enumerate(peers):
            copies.append(pltpu.make_async_remote_copy(
                src_ref=src_refs[t] if whole else src_refs[t].at[lin], dst_ref=dst,
                send_sem=send_sems.at[7 * t + k], recv_sem=recv_sems.at[7 * t + k], device_id=dev, device_id_type=MESH))
    return copies


def _split_start(srcs, lands, slots, carry, *, name):
    n = len(srcs)

    def body(*refs):
        copies = _split_copies(slots, refs[:n], refs[n:2 * n], refs[2 * n + 1], refs[2 * n + 2])
        for cp in copies:
            cp.start()

    def hbm(a):
        return pltpu.HBM(a.shape, a.dtype)

    outs = pl.pallas_call(
        body, name=name,
        out_shape=[pltpu.SemaphoreType.DMA((7 * n,)), pltpu.SemaphoreType.DMA((7 * n,))]
        + [hbm(a) for a in srcs] + [hbm(a) for a in lands] + [hbm(carry)],
        in_specs=[HBM] * (2 * n + 1), out_specs=[SEM, SEM] + [HBM] * (2 * n + 1),
        input_output_aliases={i: 2 + i for i in range(2 * n + 1)},
        compiler_params=pltpu.CompilerParams(has_side_effects=EFFECT),
    )(*[pltpu.with_memory_space_constraint(a, pltpu.HBM) for a in list(srcs) + list(lands) + [carry]])
    return outs[0], outs[1], outs[2:2 + n], outs[2 + n:2 + 2 * n], outs[2 + 2 * n]


def _split_wait(send_sems, recv_sems, srcs, lands, slots, after, *, name):
    n = len(srcs)

    def body(*refs):
        copies = _split_copies(slots, refs[:n], refs[n:2 * n], refs[2 * n], refs[2 * n + 1])
        for cp in copies:
            cp.wait_send()
        for cp in copies:
            cp.wait_recv()

    outs = pl.pallas_call(
        body, name=name,
        out_shape=[pltpu.HBM(a.shape, a.dtype) for a in list(srcs) + list(lands)],
        in_specs=[HBM] * (2 * n) + [SEM, SEM, ANY], out_specs=[HBM] * (2 * n),
        input_output_aliases={i: i for i in range(2 * n)},
        compiler_params=pltpu.CompilerParams(has_side_effects=EFFECT),
    )(*srcs, *lands, send_sems, recv_sems, after)
    return outs[n:]


def _adam_math(w, g, m, v):
    m1 = ADAM_B1 * m + (1.0 - ADAM_B1) * g
    v1 = ADAM_B2 * v + (1.0 - ADAM_B2) * (g * g)
    m_hat = m1 / (1.0 - ADAM_B1 ** ADAM_STEP)
    v_hat = v1 / (1.0 - ADAM_B2 ** ADAM_STEP)
    delta = -ADAM_LR * (m_hat / (jnp.sqrt(v_hat) + ADAM_EPS) + ADAM_WD * w)
    return delta, m1, v1


def _sum_adamw(parts, w, m, v, layer, prev, *, name):
    shape = w.shape
    r, c = shape[-2], shape[-1]
    a_l = math.prod(shape[1:-2])
    a = shape[0] * a_l
    base = layer * a_l
    if r % 256 == 0:
        tr, tc = 256, c
    else:
        tr, tc = r, _pick(c, (256, 128))
    w3, m3, v3 = (t.reshape(a, r, c) for t in (w, m, v))
    n_prev = 0 if prev is None else 4

    def body(*refs):
        p_ref, w_ref, m_ref, v_ref = refs[:4]
        g_ref, d_ref, m1_ref, v1_ref = refs[4 + n_prev:]
        g = p_ref[0].astype(F32)
        for src in range(1, N_DEV):
            g = g + p_ref[src].astype(F32)
        delta, m1, v1 = _adam_math(w_ref[...], g, m_ref[...], v_ref[...])
        g_ref[...] = g
        d_ref[...] = delta
        m1_ref[...] = m1
        v1_ref[...] = v1

    nr, ncol = r // tr, c // tc
    blk = pl.BlockSpec((None, tr, tc), lambda i, j: (base + i, j // ncol, j % ncol))
    prev3 = [] if prev is None else [t.reshape(a, r, c) for t in prev]
    outs = pl.pallas_call(
        body, grid=(a_l, nr * ncol),
        in_specs=[pl.BlockSpec((N_DEV, None, tr, tc), lambda i, j: (0, i, j // ncol, j % ncol)), blk, blk, blk]
        + [ANY] * n_prev,
        out_specs=[blk] * 4, out_shape=[jax.ShapeDtypeStruct((a, r, c), F32)] * 4,
        input_output_aliases={4 + k: k for k in range(n_prev)},
        compiler_params=_cparams(2), name=name,
    )(parts.reshape(N_DEV, a_l, r, c), w3, m3, v3, *prev3)
    return [o.reshape(shape) for o in outs]


def _sum_parts(parts, *, name):
    rows = parts.shape[1]

    def body(p_ref, o_ref):
        g = p_ref[0]
        for src in range(1, N_DEV):
            g = g + p_ref[src]
        o_ref[...] = g

    return pl.pallas_call(
        body, grid=(1,), in_specs=[pl.BlockSpec((N_DEV, rows, LANES), lambda i: (0, 0, 0))],
        out_specs=pl.BlockSpec((rows, LANES), lambda i: (0, 0)), out_shape=jax.ShapeDtypeStruct((rows, LANES), F32),
        compiler_params=_cparams(1), name=name,
    )(parts)


def _adamw(w, g, m, v, *, name):
    rows = w.shape[0]

    def body(w_ref, g_ref, m_ref, v_ref, d_ref, m1_ref, v1_ref):
        delta, m1, v1 = _adam_math(w_ref[...], g_ref[...], m_ref[...], v_ref[...])
        d_ref[...] = delta
        m1_ref[...] = m1
        v1_ref[...] = v1

    blk = pl.BlockSpec((rows, LANES), lambda i: (0, 0))
    return pl.pallas_call(
        body, grid=(1,), in_specs=[blk] * 4, out_specs=[blk] * 3,
        out_shape=[jax.ShapeDtypeStruct((rows, LANES), F32)] * 3,
        compiler_params=_cparams(1), name=name,
    )(w, g, m, v)


def _pack(arrs, dtype, row_mult=16):
    flat = jnp.concatenate([a.reshape(-1).astype(dtype) for a in arrs])
    n = flat.shape[0]
    rows = -(-n // (LANES * row_mult)) * row_mult
    flat = jnp.pad(flat, (0, rows * LANES - n))
    return flat.reshape(rows, LANES)


def _unpack(packed, shapes):
    flat = packed.reshape(-1)
    out, off = [], 0
    for shp in shapes:
        n = math.prod(shp)
        out.append(flat[off:off + n].reshape(shp))
        off += n
    return out


class _Layout:
    def __init__(self, d):
        self.d = d
        w = d
        self.dn_heads = w // DN_HEAD_DIM
        self.ssm_heads = w // SSM_HEAD_DIM
        gn = SSM_GROUPS * SSM_STATE
        self.sizes = (3 * w, w, self.dn_heads, self.dn_heads, 3 * w, w, w + 2 * gn, self.ssm_heads, 3 * d)
        offs, o = [], 0
        for sz in self.sizes:
            offs.append(o)
            o += sz
        self.offs = offs
        self.in_dim = o
        self.big = (0, 1, 4, 5, 6, 8)
        self.small = (2, 3, 7)
        cols, o = {}, 0
        for idx in self.big:
            cols[idx] = o
            o += self.sizes[idx]
        self.small_col = o
        self.cols = cols
        self.padded = o + LANES
        self.n_small = sum(self.sizes[i] for i in self.small)

    def reorder_w(self, w_in):
        parts = [w_in[:, self.offs[i]:self.offs[i] + self.sizes[i]] for i in self.big + self.small]
        parts.append(jnp.zeros((w_in.shape[0], LANES - self.n_small), w_in.dtype))
        return jnp.concatenate(parts, axis=1)

    def from_shards(self, parts):
        cs = self.in_dim // N_DEV
        pieces = []
        for i in self.big + self.small:
            a, b = self.offs[i], self.offs[i] + self.sizes[i]
            while a < b:
                j = a // cs
                hi = min(b, (j + 1) * cs)
                pieces.append(parts[j][:, a - j * cs:hi - j * cs])
                a = hi
        pieces.append(jnp.zeros((parts.shape[1], LANES - self.n_small), parts.dtype))
        return jnp.concatenate(pieces, axis=1)

    def to_shards(self, wp):
        cs = self.in_dim // N_DEV
        pcol = dict(self.cols)
        o = self.small_col
        for i in self.small:
            pcol[i] = o
            o += self.sizes[i]
        shards = []
        for j in range(N_DEV):
            a, b = j * cs, (j + 1) * cs
            pieces = []
            for i in range(len(self.sizes)):
                lo, hi = max(a, self.offs[i]), min(b, self.offs[i] + self.sizes[i])
                if lo < hi:
                    pieces.append(wp[:, pcol[i] + lo - self.offs[i]:pcol[i] + hi - self.offs[i]])
            shards.append(jnp.concatenate(pieces, axis=1))
        return jnp.stack(shards)

    def restore_w(self, wp):
        pieces = {}
        for idx in self.big:
            pieces[idx] = wp[:, self.cols[idx]:self.cols[idx] + self.sizes[idx]]
        o = self.small_col
        for idx in self.small:
            pieces[idx] = wp[:, o:o + self.sizes[idx]]
            o += self.sizes[idx]
        return jnp.concatenate([pieces[i] for i in range(len(self.sizes))], axis=1)


def _rows_form(cols_t, nh, nc):
    return cols_t.T.reshape(nh, nc, 1, CHUNK)


def _layer_fwd(x, p, lay, tag, late=None):
    s, d = x.shape
    nc = s // CHUNK
    w = d
    dnh, smh = lay.dn_heads, lay.ssm_heads
    r = smh // SSM_GROUPS
    cb = {k: v // LANES for k, v in lay.cols.items()}
    sv = {}
    h1 = _rms_fwd(x, p["norm_mix"], name=f"rms_mix_{tag}")
    proj = _matmul(h1, p["w_in"], name=f"mm_in_{tag}")
    small = proj[:, lay.small_col:lay.small_col + LANES]
    a_rows = _rows_form(small[:, 0:dnh], dnh, nc)
    b_rows = _rows_form(small[:, dnh:2 * dnh], dnh, nc)
    dt_rows = small[:, 2 * dnh:2 * dnh + smh].T.reshape(SSM_GROUPS, r, nc, CHUNK).transpose(0, 2, 1, 3)
    zero_b = jnp.zeros((1, 3 * w), F32)
    dn_qkv = _conv_fwd(proj, cb[0], p["dn_conv_w"], zero_b, 2 * dnh, name=f"dn_conv_{tag}")
    dn_alog = p["dn_a_log"].reshape(dnh, 1, 1)
    dn_dtb = p["dn_dt_bias"].reshape(dnh, 1, 1)
    o_dn, dn_states = _dn_fwd(dn_qkv, a_rows, b_rows, dn_alog, dn_dtb, name=f"dn_chunk_{tag}")
    y_dn = _dn_post_fwd(o_dn, proj, cb[1], p["dn_norm_w"], name=f"dn_post_{tag}")
    o_sb, sb_r = _sb_fwd(proj, cb[4], w, name=f"sb_{tag}")
    xbc = _conv_fwd(proj, cb[6], p["ssm_conv_w"], p["ssm_conv_b"].reshape(1, -1), 0, name=f"ssm_conv_{tag}")
    ssm_alog = p["ssm_a_log"].reshape(SSM_GROUPS, r, 1)
    ssm_dtb = p["ssm_dt_bias"].reshape(SSM_GROUPS, r, 1)
    y_ssd, ssm_states = _ssd_fwd(xbc, dt_rows, ssm_alog, ssm_dtb, name=f"ssd_{tag}")
    dexp = jnp.repeat(p["ssm_d"], SSM_HEAD_DIM)
    y_ssm = _ssm_post_fwd(y_ssd, xbc, proj, cb[5], dexp, p["ssm_norm_w"], name=f"ssm_post_{tag}")
    if late is not None:
        p.update(late(y_ssm))
    branches = (y_dn, o_sb, y_ssm)
    proj3 = jnp.concatenate(
        [_matmul(br, p["w_branch"][i], name=f"mm_branch{i}_{tag}") for i, br in enumerate(branches)], axis=1)
    merged = _merge_fwd(proj3, proj, cb[8], d, name=f"merge_{tag}")
    x1 = _matmul(merged, p["w_out"], name=f"mm_out_{tag}", epilogue=lambda acc, res: (acc + res,), extras=(x,))
    h2 = _rms_fwd(x1, p["norm_mlp"], name=f"rms_mlp_{tag}")
    u, act = _matmul(h2, p["w_up"], name=f"mm_up_{tag}", out_dtypes=(F32, MXU_DTYPE),
                     epilogue=lambda acc: (acc, jnp.square(jnp.maximum(acc, 0.0))))
    x2 = _matmul(act, p["w_down"], name=f"mm_down_{tag}", epilogue=lambda acc, res: (acc + res,), extras=(x1,))
    sv.update(x=x, h1=h1, proj=proj, a_rows=a_rows, b_rows=b_rows, dt_rows=dt_rows, dn_qkv=dn_qkv, dn_alog=dn_alog,
              dn_dtb=dn_dtb, o_dn=o_dn, dn_states=dn_states, y_dn=y_dn, o_sb=o_sb, sb_r=sb_r, xbc=xbc, ssm_alog=ssm_alog,
              ssm_dtb=ssm_dtb, y_ssd=y_ssd, ssm_states=ssm_states, dexp=dexp, y_ssm=y_ssm, proj3=proj3, merged=merged,
              x1=x1, h2=h2, u=u, act=act)
    return x2, sv


def _layer_bwd(dx2, p, sv, lay, tag, early=None, late=None):
    x = sv["x"]
    s, d = x.shape
    nc = s // CHUNK
    w = d
    dnh, smh = lay.dn_heads, lay.ssm_heads
    r = smh // SSM_GROUPS
    gn = SSM_GROUPS * SSM_STATE
    cb = {k: v // LANES for k, v in lay.cols.items()}
    proj = sv["proj"]
    g = {}
    dx2_b = dx2.astype(MXU_DTYPE)
    du = _matmul(dx2_b, p["w_down"], tb=True, name=f"mm_down_dx_{tag}", out_dtypes=(MXU_DTYPE,),
                 epilogue=lambda acc, uu: (acc * (2.0 * jnp.maximum(uu, 0.0)),), extras=(sv["u"],))
    g["w_down"] = _matmul(sv["act"], dx2_b, ta=True, name=f"mm_down_dw_{tag}", out_dtypes=(BF16,)).reshape(N_DEV, -1, d)
    g["w_up"] = _matmul(sv["h2"], du, ta=True, name=f"mm_up_dw_{tag}", out_dtypes=(BF16,), col_shards=N_DEV)
    dh2 = _matmul(du, p["w_up"], tb=True, name=f"mm_up_dx_{tag}")
    dx1, g["norm_mlp"] = _rms_bwd(sv["x1"], p["norm_mlp"], dh2, dx2, name=f"rms_mlp_bwd_{tag}")
    dx1_b = dx1.astype(MXU_DTYPE)
    dmerged = _matmul(dx1_b, p["w_out"], tb=True, name=f"mm_out_dx_{tag}")
    g["w_out"] = _matmul(sv["merged"], dx1_b, ta=True, name=f"mm_out_dw_{tag}", out_dtypes=(BF16,)).reshape(N_DEV, -1, d)
    dproj3, dgates = _merge_bwd(sv["proj3"], proj, cb[8], d, dmerged, name=f"merge_bwd_{tag}")
    branches = (sv["y_dn"], sv["o_sb"], sv["y_ssm"])
    dwb, dbr = [], []
    for i, br in enumerate(branches):
        dp_i = dproj3[:, i * d:(i + 1) * d]
        dwb.append(_matmul(br, dp_i, ta=True, name=f"mm_branch{i}_dw_{tag}", out_dtypes=(BF16,)).reshape(N_DEV, -1, d))
        dbr.append(_matmul(dp_i, p["w_branch"][i], tb=True, name=f"mm_branch{i}_dx_{tag}"))
    g["w_branch"] = jnp.stack(dwb, axis=1)
    dy_dn, do_sb, dy_ssm = dbr
    if early is not None:
        dy_ssm = early(g, dy_ssm)
    dy_ssd, dxs_skip, dz, ddexp, g["ssm_norm_w"] = _ssm_post_bwd(
        sv["y_ssd"], sv["xbc"], proj, cb[5], sv["dexp"], p["ssm_norm_w"], dy_ssm, name=f"ssm_post_bwd_{tag}")
    g["ssm_d"] = ddexp.reshape(smh, SSM_HEAD_DIM).sum(axis=1)
    dxs, dbm, dcm, ddt_rows, dalog, ddtb = _ssd_bwd(
        sv["xbc"], sv["dt_rows"], sv["ssm_alog"], sv["ssm_dtb"], sv["ssm_states"], dy_ssd, name=f"ssd_bwd_{tag}")
    g["ssm_a_log"] = dalog.reshape(smh)
    g["ssm_dt_bias"] = ddtb.reshape(smh)
    dxbc_post = jnp.concatenate([dxs + dxs_skip, dbm, dcm], axis=1)
    dxbc, g["ssm_conv_w"], dcb = _conv_bwd(proj, cb[6], p["ssm_conv_w"], p["ssm_conv_b"].reshape(1, -1), 0, dxbc_post,
                                           name=f"ssm_conv_bwd_{tag}")
    g["ssm_conv_b"] = dcb.reshape(-1)
    ddt = ddt_rows.transpose(0, 2, 1, 3).reshape(smh, s).T
    dq_sb, dk_sb, dv_sb = _sb_bwd(proj, cb[4], w, sv["sb_r"], do_sb, name=f"sb_bwd_{tag}")
    do_dn, dgate_dn, g["dn_norm_w"] = _dn_post_bwd(sv["o_dn"], proj, cb[1], p["dn_norm_w"], dy_dn, name=f"dn_post_bwd_{tag}")
    dq, dk, dv, da_rows, db_rows, dal, ddtb_dn = _dn_bwd(
        sv["dn_qkv"], sv["a_rows"], sv["b_rows"], sv["dn_alog"], sv["dn_dtb"], sv["dn_states"], do_dn, name=f"dn_chunk_bwd_{tag}")
    g["dn_a_log"] = dal.reshape(dnh)
    g["dn_dt_bias"] = ddtb_dn.reshape(dnh)
    zero_b = jnp.zeros((1, 3 * w), F32)
    ddn_qkv, g["dn_conv_w"], _ = _conv_bwd(proj, cb[0], p["dn_conv_w"], zero_b, 2 * dnh,
                                           jnp.concatenate([dq, dk, dv], axis=1), name=f"dn_conv_bwd_{tag}")
    da = da_rows.reshape(dnh, s).T
    db = db_rows.reshape(dnh, s).T
    dsmall = jnp.concatenate([da, db, ddt, jnp.zeros((s, LANES - lay.n_small), F32)], axis=1).astype(MXU_DTYPE)
    dproj = jnp.concatenate(
        [ddn_qkv, dgate_dn, dq_sb.astype(MXU_DTYPE), dk_sb.astype(MXU_DTYPE), dv_sb.astype(MXU_DTYPE), dz, dxbc, dgates, dsmall],
        axis=1)
    g["w_in"] = lay.to_shards(_matmul(sv["h1"], dproj, ta=True, name=f"mm_in_dw_{tag}", out_dtypes=(BF16,)))
    if late is not None:
        dproj = late(g, dproj)
    dh1 = _matmul(dproj, p["w_in"], tb=True, name=f"mm_in_dx_{tag}")
    dx0, g["norm_mix"] = _rms_bwd(x, p["norm_mix"], dh1, dx1, name=f"rms_mix_bwd_{tag}")
    return dx0, g


BIG = ("w_in", "w_branch", "w_out", "w_up", "w_down")
CONV = ("dn_conv_w", "ssm_conv_w")
SMALL = ("norm_mix", "dn_conv_w", "dn_a_log", "dn_dt_bias", "dn_norm_w", "ssm_conv_w", "ssm_conv_b", "ssm_a_log",
         "ssm_dt_bias", "ssm_d", "ssm_norm_w", "norm_mlp", "norm_final")
WEIGHTS = ("norm_mix", "w_in", "dn_conv_w", "dn_a_log", "dn_dt_bias", "dn_norm_w", "ssm_conv_w", "ssm_conv_b", "ssm_a_log",
           "ssm_dt_bias", "ssm_d", "ssm_norm_w", "w_branch", "w_out", "norm_mlp", "w_up", "w_down", "norm_final")
SHARD_AXIS = {"w_in": 2, "dn_conv_w": 2, "ssm_conv_w": 2, "w_branch": 2, "w_out": 1, "w_up": 2, "w_down": 1}


def _to_shards(full, axis):
    shp = full.shape
    n = shp[axis] // N_DEV
    t = full.reshape(shp[:axis] + (N_DEV, n) + shp[axis + 1:])
    return jnp.moveaxis(t, axis, 0)


def _from_shards(parts, axis):
    t = jnp.moveaxis(parts, 0, axis)
    shp = t.shape
    return t.reshape(shp[:axis] + (shp[axis] * shp[axis + 1],) + shp[axis + 2:])


def _unshard(parts, axis, *, name):
    shard = parts.shape[1:]
    nd = len(shard)
    if axis == 0:
        return parts.reshape((N_DEV * shard[0],) + shard[1:])

    def copy_block(i_ref, o_ref):
        o_ref[...] = i_ref[...]

    if axis == nd - 1:
        rows, n = math.prod(shard[:-1]), shard[-1]
        out = pl.pallas_call(
            copy_block, grid=(N_DEV,),
            in_specs=[pl.BlockSpec((None, rows, n), lambda j: (j, 0, 0))],
            out_specs=pl.BlockSpec((rows, n), lambda j: (0, j)),
            out_shape=jax.ShapeDtypeStruct((rows, N_DEV * n), parts.dtype),
            compiler_params=_cparams(1), name=name,
        )(parts.reshape(N_DEV, rows, n))
        return out.reshape(shard[:-1] + (N_DEV * n,))
    assert axis == nd - 2, (parts.shape, axis)
    a, n, c = math.prod(shard[:-2]), shard[-2], shard[-1]
    out = pl.pallas_call(
        copy_block, grid=(N_DEV, a),
        in_specs=[pl.BlockSpec((None, None, n, c), lambda j, i: (j, i, 0, 0))],
        out_specs=pl.BlockSpec((None, n, c), lambda j, i: (i, j, 0)),
        out_shape=jax.ShapeDtypeStruct((a, N_DEV * n, c), parts.dtype),
        compiler_params=_cparams(2), name=name,
    )(parts.reshape(N_DEV, a, n, c))
    return out.reshape(shard[:-2] + (N_DEV * n, c))


def _step(w, m, v, x, target):
    s, d = x.shape
    lay = _Layout(d)
    me = 4 * lax.axis_index("x") + 2 * lax.axis_index("y") + lax.axis_index("c")

    def shard(n, l):
        return w[n][l].astype(BF16) if n in BIG else w[n][l]

    def empty_land(a):
        return lax.empty((N_DEV,) + a.shape, a.dtype)

    def with_own(land, own):
        return lax.dynamic_update_index_in_dim(land, own, me, 0)

    def assemble(n, parts, l):
        return lay.from_shards(parts) if n == "w_in" else _unshard(parts, SHARD_AXIS[n] - 1, name=f"unshard_{n}_l{l}")

    small_names = tuple(n for n in WEIGHTS if n not in BIG + CONV + ("norm_final",))

    first, rest = ("w_in",) + CONV, BIG[1:]
    got = _all_gather([shard(n, 0) for n in first], name="gather_l0_first")
    whole, sliced = (True, None), (False, None)
    names_a, names_b = rest, BIG + CONV
    srcs_a, srcs_b = [shard(n, 0) for n in names_a], [shard(n, 1) for n in names_b]
    sem_sa, sem_ra, srcs_a, lands_a, w_in0 = _split_start(
        srcs_a, [empty_land(a) for a in srcs_a], [whole] * len(srcs_a), got[0], name="gather_l0_rest_start")
    sem_sb, sem_rb, srcs_b, lands_b, w_in0 = _split_start(
        srcs_b, [empty_land(a) for a in srcs_b], [whole] * len(srcs_b), w_in0, name="gather_l1_start")
    p0 = {n: w[n][0] for n in small_names}
    p0.update({n: assemble(n, g, 0) for n, g in zip(first, [w_in0] + list(got[1:]))})

    def late_l0(after):
        lands = _split_wait(sem_sa, sem_ra, srcs_a, lands_a, [whole] * len(srcs_a), after, name="gather_l0_rest_wait")
        return {n: assemble(n, with_own(ld, s_), 0) for n, ld, s_ in zip(names_a, lands, srcs_a)}

    h, sv0 = _layer_fwd(x, p0, lay, "l0", late=late_l0)
    lands = _split_wait(sem_sb, sem_rb, srcs_b, lands_b, [whole] * len(srcs_b), h, name="gather_l1_wait")
    p1 = {n: w[n][1] for n in small_names}
    p1.update({n: assemble(n, with_own(ld, s_), 1) for n, ld, s_ in zip(names_b, lands, srcs_b)})
    h, sv1 = _layer_fwd(h, p1, lay, "l1")
    loss, dh, g_norm_final = _final_loss(h, w["norm_final"], target, name="final_loss")
    grads = [None] * DEPTH
    dh, grads[1] = _layer_bwd(dh, p1, sv1, lay, "l1")

    def exchange_start(names, g, carry, tag):
        srcs = [g[n] for n in names]
        return _split_start(srcs, [lax.empty(a.shape, a.dtype) for a in srcs], [sliced] * len(srcs), carry,
                            name=f"grad_{tag}_start")

    def exchange_wait(names, started, after, tag):
        sem_s, sem_r, srcs, lands_, _ = started
        lands_ = _split_wait(sem_s, sem_r, srcs, lands_, [sliced] * len(srcs), after, name=f"grad_{tag}_wait")
        return {n: with_own(ld, lax.dynamic_index_in_dim(s_, me, 0, keepdims=False)) for n, ld, s_ in zip(names, lands_, srcs)}

    x1_started = exchange_start(BIG, grads[1], dh, "l1")
    pending = {}

    def early_l0(g, carry):
        pending["rest"] = exchange_start(rest, g, carry, "l0_rest")
        return pending["rest"][4]

    def late_bwd_l0(g, carry):
        pending["w_in"] = exchange_start(("w_in",), g, carry, "l0_w_in")
        return pending["w_in"][4]

    grad_x, grads[0] = _layer_bwd(x1_started[4], p0, sv0, lay, "l0", early=early_l0, late=late_bwd_l0)

    out = {"grad": {}, "delta": {}, "new_m": {}, "new_v": {}}
    parts1 = exchange_wait(BIG, x1_started, grad_x, "l1")
    res1 = {n: _sum_adamw(parts1[n], w[n], m[n], v[n], 1, None, name=f"sum_adamw_{n}_l1") for n in BIG}
    parts0 = exchange_wait(rest, pending["rest"], res1["w_in"][0], "l0_rest")
    res0 = {n: _sum_adamw(parts0[n], w[n], m[n], v[n], 0, res1[n], name=f"sum_adamw_{n}_l0") for n in rest}
    parts0 = exchange_wait(("w_in",), pending["w_in"], res0["w_down"][0], "l0_w_in")
    res0["w_in"] = _sum_adamw(parts0["w_in"], w["w_in"], m["w_in"], v["w_in"], 0, res1["w_in"], name="sum_adamw_w_in_l0")
    for n in BIG:
        for key, a in zip(("grad", "delta", "new_m", "new_v"), res0[n]):
            out[key][n] = a

    gfull = {n: jnp.stack([grads[l][n] for l in range(DEPTH)]) for n in SMALL if n != "norm_final"}
    gfull["norm_final"] = g_norm_final
    small_send = _pack([gfull[n] for n in SMALL] + [loss.reshape(1)], F32)
    small_recv = _all_gather([small_send], name="gather_small_grads", after=res0["w_in"][0])[0]
    small_sum = _sum_parts(small_recv, name="sum_small")
    small_full = _unpack(small_sum, [gfull[n].shape for n in SMALL] + [(1,)])
    loss_total = small_full[-1][0]
    gsmall = {}
    for n, a in zip(SMALL, small_full[:-1]):
        if n in SHARD_AXIS:
            a = lax.dynamic_index_in_dim(_to_shards(a, SHARD_AXIS[n]), me, axis=0, keepdims=False)
        gsmall[n] = a
    small_shapes = [w[n].shape for n in SMALL]
    ws, gs, ms, vs = (_pack([t[n] for n in SMALL], F32) for t in (w, gsmall, m, v))
    ds, m1s, v1s = _adamw(ws, gs, ms, vs, name="adamw_small")
    for n in SMALL:
        out["grad"][n] = gsmall[n]
    for key, packed in (("delta", ds), ("new_m", m1s), ("new_v", v1s)):
        for n, a in zip(SMALL, _unpack(packed, small_shapes)):
            out[key][n] = a
    return loss_total, grad_x, out


def kernel(x, norm_mix, w_in, dn_conv_w, dn_a_log, dn_dt_bias, dn_norm_w, ssm_conv_w, ssm_conv_b, ssm_a_log, ssm_dt_bias, ssm_d, ssm_norm_w, w_branch, w_out, norm_mlp, w_up, w_down, norm_final, loss_target, m_norm_mix, m_w_in, m_dn_conv_w, m_dn_a_log, m_dn_dt_bias, m_dn_norm_w, m_ssm_conv_w, m_ssm_conv_b, m_ssm_a_log, m_ssm_dt_bias, m_ssm_d, m_ssm_norm_w, m_w_branch, m_w_out, m_norm_mlp, m_w_up, m_w_down, m_norm_final, v_norm_mix, v_w_in, v_dn_conv_w, v_dn_a_log, v_dn_dt_bias, v_dn_norm_w, v_ssm_conv_w, v_ssm_conv_b, v_ssm_a_log, v_ssm_dt_bias, v_ssm_d, v_ssm_norm_w, v_w_branch, v_w_out, v_norm_mlp, v_w_up, v_w_down, v_norm_final):
    w = dict(norm_mix=norm_mix, w_in=w_in, dn_conv_w=dn_conv_w, dn_a_log=dn_a_log, dn_dt_bias=dn_dt_bias, dn_norm_w=dn_norm_w,
             ssm_conv_w=ssm_conv_w, ssm_conv_b=ssm_conv_b, ssm_a_log=ssm_a_log, ssm_dt_bias=ssm_dt_bias, ssm_d=ssm_d,
             ssm_norm_w=ssm_norm_w, w_branch=w_branch, w_out=w_out, norm_mlp=norm_mlp, w_up=w_up, w_down=w_down,
             norm_final=norm_final)
    m = dict(norm_mix=m_norm_mix, w_in=m_w_in, dn_conv_w=m_dn_conv_w, dn_a_log=m_dn_a_log, dn_dt_bias=m_dn_dt_bias,
             dn_norm_w=m_dn_norm_w, ssm_conv_w=m_ssm_conv_w, ssm_conv_b=m_ssm_conv_b, ssm_a_log=m_ssm_a_log,
             ssm_dt_bias=m_ssm_dt_bias, ssm_d=m_ssm_d, ssm_norm_w=m_ssm_norm_w, w_branch=m_w_branch, w_out=m_w_out,
             norm_mlp=m_norm_mlp, w_up=m_w_up, w_down=m_w_down, norm_final=m_norm_final)
    v = dict(norm_mix=v_norm_mix, w_in=v_w_in, dn_conv_w=v_dn_conv_w, dn_a_log=v_dn_a_log, dn_dt_bias=v_dn_dt_bias,
             dn_norm_w=v_dn_norm_w, ssm_conv_w=v_ssm_conv_w, ssm_conv_b=v_ssm_conv_b, ssm_a_log=v_ssm_a_log,
             ssm_dt_bias=v_ssm_dt_bias, ssm_d=v_ssm_d, ssm_norm_w=v_ssm_norm_w, w_branch=v_w_branch, w_out=v_w_out,
             norm_mlp=v_norm_mlp, w_up=v_w_up, w_down=v_w_down, norm_final=v_norm_final)
    loss, grad_x, out = _step(w, m, v, x[0], loss_target[0])
    return (loss, grad_x[None], *[out["grad"][n] for n in WEIGHTS], *[out["delta"][n] for n in WEIGHTS],
            *[out["new_m"][n] for n in WEIGHTS], *[out["new_v"][n] for n in WEIGHTS])
```

```python
import functools
import math

import jax
import jax.numpy as jnp
from jax import lax
from jax.experimental import pallas as pl
from jax.experimental.pallas import tpu as pltpu

F32 = jnp.float32
BF16 = jnp.bfloat16
MXU_DTYPE = BF16
HIGHEST = lax.Precision.HIGHEST

N_DEV = 8
DEPTH = 2
EPS = 1e-6
CONV_K = 4
DN_HEAD_DIM = 128
SB_HEAD_DIM = 64
SSM_HEAD_DIM = 64
SSM_STATE = 128
SSM_GROUPS = 4
CHUNK = 64
SB_BLOCK = 128
LANES = 128
ADAM_LR, ADAM_B1, ADAM_B2, ADAM_EPS, ADAM_WD, ADAM_STEP = 0.001, 0.9, 0.999, 1e-08, 0.01, 10
NEG_BIG = -1e30
DN_HEADS_PER_STEP = 8
SSD_GROUPS_PER_STEP = 1
SB_UNROLL = 4
SB_SPLIT = 2
CHUNK_PREC = lax.Precision.HIGH

ARB = "arbitrary"


def _cparams(n_axes):
    return pltpu.CompilerParams(dimension_semantics=(ARB,) * n_axes)


def _softplus(x):
    return jnp.maximum(x, 0.0) + jnp.log1p(jnp.exp(-jnp.abs(x)))


def _sigmoid(x):
    return 1.0 / (1.0 + jnp.exp(-x))


def _silu(x):
    return x * _sigmoid(x)


def _silu_grad(x):
    s = _sigmoid(x)
    return s * (1.0 + x * (1.0 - s))


def _dot(a, b, dims, prec=None):
    return lax.dot_general(a, b, (dims, ((), ())), precision=prec, preferred_element_type=F32)


NN = ((1,), (0,))
NT = ((1,), (1,))
TN = ((0,), (0,))


def _hdot(a, b, dims=NN):
    return _dot(a, b, dims, CHUNK_PREC)


def _bdot(a, b, dims=NN):
    return _dot(a.astype(MXU_DTYPE), b.astype(MXU_DTYPE), dims)


def _split_dot(a, m_bf16, nsplit=3):
    out = None
    rem = a
    for _ in range(nsplit):
        piece = rem.astype(BF16)
        rem = rem - piece.astype(F32)
        term = _dot(piece, m_bf16, NN)
        out = term if out is None else out + term
    return out


def _pick(n, pref):
    for t in pref:
        if n % t == 0:
            return t
    return n


def _matmul(a, b, *, ta=False, tb=False, name, epilogue=None, extras=(), out_dtypes=(F32,), col_shards=1,
            tm=None, tn=None, tk=None):
    m, k = (a.shape[1], a.shape[0]) if ta else a.shape
    k2, n = (b.shape[1], b.shape[0]) if tb else b.shape
    assert k == k2, (a.shape, b.shape, ta, tb)
    ncs = n // col_shards
    tm = tm or _pick(m, (1920, 1024, 512, 256, 128))
    tn = tn or _pick(ncs, (1920, 1024, 640, 512, 384, 256, 128))
    tk = tk or _pick(k, (1920, 1024, 640, 512, 256, 128))
    nk = k // tk
    a_spec = pl.BlockSpec((tk, tm), lambda i, j, kk: (kk, i)) if ta else pl.BlockSpec((tm, tk), lambda i, j, kk: (i, kk))
    b_spec = pl.BlockSpec((tn, tk), lambda i, j, kk: (j, kk)) if tb else pl.BlockSpec((tk, tn), lambda i, j, kk: (kk, j))
    e_spec = pl.BlockSpec((tm, tn), lambda i, j, kk: (i, j))
    if col_shards == 1:
        o_spec, o_shape = e_spec, (m, n)
    else:
        per = ncs // tn
        o_spec, o_shape = pl.BlockSpec((None, tm, tn), lambda i, j, kk: (j // per, i, j % per)), (col_shards, m, ncs)
    dims = (((0,) if ta else (1,)), ((1,) if tb else (0,)))
    n_extra = len(extras)
    n_out = len(out_dtypes)

    def body(*refs):
        a_ref, b_ref = refs[0], refs[1]
        extra_refs = refs[2:2 + n_extra]
        out_refs = refs[2 + n_extra:2 + n_extra + n_out]
        acc_ref = refs[-1]
        kk = pl.program_id(2)

        @pl.when(kk == 0)
        def _():
            acc_ref[...] = jnp.zeros_like(acc_ref)

        acc_ref[...] += _dot(a_ref[...].astype(MXU_DTYPE), b_ref[...].astype(MXU_DTYPE), dims)

        @pl.when(kk == nk - 1)
        def _():
            acc = acc_ref[...]
            outs = (acc,) if epilogue is None else epilogue(acc, *[r[...] for r in extra_refs])
            for o_ref, o in zip(out_refs, outs):
                o_ref[...] = o.astype(o_ref.dtype)

    outs = pl.pallas_call(
        body,
        grid=(m // tm, n // tn, nk),
        in_specs=[a_spec, b_spec] + [e_spec] * n_extra,
        out_specs=[o_spec] * n_out,
        out_shape=[jax.ShapeDtypeStruct(o_shape, dt) for dt in out_dtypes],
        scratch_shapes=[pltpu.VMEM((tm, tn), F32)],
        compiler_params=pltpu.CompilerParams(dimension_semantics=("parallel", "parallel", ARB)),
        name=name,
    )(a, b, *extras)
    return outs[0] if n_out == 1 else tuple(outs)


def _rms_fwd(x, w, *, name, tm=256):
    s, d = x.shape
    out_dtype = MXU_DTYPE

    def body(x_ref, w_ref, o_ref):
        xv = x_ref[...]
        r = lax.rsqrt(jnp.mean(xv * xv, axis=-1, keepdims=True) + EPS)
        o_ref[...] = (xv * r * w_ref[...]).astype(o_ref.dtype)

    return pl.pallas_call(
        body, grid=(s // tm,),
        in_specs=[pl.BlockSpec((tm, d), lambda i: (i, 0)), pl.BlockSpec((1, d), lambda i: (0, 0))],
        out_specs=pl.BlockSpec((tm, d), lambda i: (i, 0)),
        out_shape=jax.ShapeDtypeStruct((s, d), out_dtype),
        compiler_params=_cparams(1), name=name,
    )(x, w.reshape(1, d))


def _rms_bwd(x, w, dh, dres, *, name, tm=256):
    s, d = x.shape

    def body(x_ref, w_ref, dh_ref, dres_ref, dx_ref, dw_ref):
        xv = x_ref[...]
        r = lax.rsqrt(jnp.mean(xv * xv, axis=-1, keepdims=True) + EPS)
        xh = xv * r
        dhv = dh_ref[...].astype(F32)
        dxn = dhv * w_ref[...]
        dx = r * (dxn - xh * jnp.mean(dxn * xh, axis=-1, keepdims=True))
        dx_ref[...] = dres_ref[...] + dx

        @pl.when(pl.program_id(0) == 0)
        def _():
            dw_ref[...] = jnp.zeros_like(dw_ref)

        dw_ref[...] += jnp.sum(dhv * xh, axis=0, keepdims=True)

    dx, dw = pl.pallas_call(
        body, grid=(s // tm,),
        in_specs=[pl.BlockSpec((tm, d), lambda i: (i, 0)), pl.BlockSpec((1, d), lambda i: (0, 0)),
                  pl.BlockSpec((tm, d), lambda i: (i, 0)), pl.BlockSpec((tm, d), lambda i: (i, 0))],
        out_specs=[pl.BlockSpec((tm, d), lambda i: (i, 0)), pl.BlockSpec((1, d), lambda i: (0, 0))],
        out_shape=[jax.ShapeDtypeStruct((s, d), F32), jax.ShapeDtypeStruct((1, d), F32)],
        compiler_params=_cparams(1), name=name,
    )(x, w.reshape(1, d), dh, dres)
    return dx, dw.reshape(d)


def _final_loss(x, w, target, *, name, tm=256):
    s, d = x.shape

    def body(x_ref, w_ref, t_ref, loss_ref, dx_ref, dw_ref):
        xv = x_ref[...]
        r = lax.rsqrt(jnp.mean(xv * xv, axis=-1, keepdims=True) + EPS)
        xh = xv * r
        err = xh * w_ref[...] - t_ref[...]
        dy = err * (1.0 / d)
        dxn = dy * w_ref[...]
        dx_ref[...] = r * (dxn - xh * jnp.mean(dxn * xh, axis=-1, keepdims=True))

        @pl.when(pl.program_id(0) == 0)
        def _():
            dw_ref[...] = jnp.zeros_like(dw_ref)
            loss_ref[...] = jnp.zeros_like(loss_ref)

        dw_ref[...] += jnp.sum(dy * xh, axis=0, keepdims=True)
        row = jnp.sum(err * err, axis=1, keepdims=True) * (0.5 / d)
        loss_ref[...] += jnp.sum(row, axis=0, keepdims=True)

    loss, dx, dw = pl.pallas_call(
        body, grid=(s // tm,),
        in_specs=[pl.BlockSpec((tm, d), lambda i: (i, 0)), pl.BlockSpec((1, d), lambda i: (0, 0)),
                  pl.BlockSpec((tm, d), lambda i: (i, 0))],
        out_specs=[pl.BlockSpec((1, 1), lambda i: (0, 0)), pl.BlockSpec((tm, d), lambda i: (i, 0)),
                   pl.BlockSpec((1, d), lambda i: (0, 0))],
        out_shape=[jax.ShapeDtypeStruct((1, 1), F32), jax.ShapeDtypeStruct((s, d), F32), jax.ShapeDtypeStruct((1, d), F32)],
        compiler_params=_cparams(1), name=name,
    )(x, w.reshape(1, d), target)
    return loss[0, 0], dx, dw.reshape(d)


def _shift_down(x, sh, t_idx):
    return jnp.where(t_idx >= sh, pltpu.roll(x, sh, 0), 0.0)


def _shift_up(x, sh, t_idx, s):
    return jnp.where(t_idx < s - sh, pltpu.roll(x, s - sh, 0), 0.0)


def _conv_pre(x, w_rows, b, t_idx):
    c = w_rows[CONV_K - 1] * x + b
    for sh in range(1, CONV_K):
        c = c + w_rows[CONV_K - 1 - sh] * _shift_down(x, sh, t_idx)
    return c


def _conv_fwd(src, col0, w, b, n_l2, *, name):
    s = src.shape[0]
    c_tot = w.shape[1]
    nblk = c_tot // LANES

    def body(x_ref, w_ref, b_ref, o_ref):
        j = pl.program_id(0)
        t_idx = lax.broadcasted_iota(jnp.int32, (s, LANES), 0)
        w_rows = [w_ref[kk:kk + 1, :] for kk in range(CONV_K)]
        y = _silu(_conv_pre(x_ref[...], w_rows, b_ref[...], t_idx))
        if n_l2 > 0:
            yn = y * lax.rsqrt(jnp.sum(y * y, axis=1, keepdims=True) + EPS)
            y = jnp.where(j < n_l2, yn, y)
        o_ref[...] = y

    return pl.pallas_call(
        body, grid=(nblk,),
        in_specs=[pl.BlockSpec((s, LANES), lambda j: (0, col0 + j)), pl.BlockSpec((CONV_K, LANES), lambda j: (0, j)),
                  pl.BlockSpec((1, LANES), lambda j: (0, j))],
        out_specs=pl.BlockSpec((s, LANES), lambda j: (0, j)),
        out_shape=jax.ShapeDtypeStruct((s, c_tot), F32),
        compiler_params=_cparams(1), name=name,
    )(src, w, b)


def _conv_bwd(src, col0, w, b, n_l2, dout, into, *, name):
    s = src.shape[0]
    c_tot = w.shape[1]
    nblk = c_tot // LANES

    def body(x_ref, w_ref, b_ref, do_ref, into_ref, dx_ref, dw_ref, db_ref):
        j = pl.program_id(0)
        t_idx = lax.broadcasted_iota(jnp.int32, (s, LANES), 0)
        xv = x_ref[...]
        w_rows = [w_ref[kk:kk + 1, :] for kk in range(CONV_K)]
        c = _conv_pre(xv, w_rows, b_ref[...], t_idx)
        dy = do_ref[...]
        if n_l2 > 0:
            y = _silu(c)
            r = lax.rsqrt(jnp.sum(y * y, axis=1, keepdims=True) + EPS)
            dyn = r * dy - y * (r * r * r) * jnp.sum(dy * y, axis=1, keepdims=True)
            dy = jnp.where(j < n_l2, dyn, dy)
        dc = dy * _silu_grad(c)
        dx = w_rows[CONV_K - 1] * dc
        rows = [None] * CONV_K
        rows[CONV_K - 1] = jnp.sum(dc * xv, axis=0, keepdims=True)
        for sh in range(1, CONV_K):
            dx = dx + w_rows[CONV_K - 1 - sh] * _shift_up(dc, sh, t_idx, s)
            rows[CONV_K - 1 - sh] = jnp.sum(dc * _shift_down(xv, sh, t_idx), axis=0, keepdims=True)
        dx_ref[...] = dx.astype(dx_ref.dtype)
        for kk in range(CONV_K):
            dw_ref[kk:kk + 1, :] = rows[kk]
        db_ref[...] = jnp.sum(dc, axis=0, keepdims=True)

    return pl.pallas_call(
        body, grid=(nblk,),
        in_specs=[pl.BlockSpec((s, LANES), lambda j: (0, col0 + j)), pl.BlockSpec((CONV_K, LANES), lambda j: (0, j)),
                  pl.BlockSpec((1, LANES), lambda j: (0, j)), pl.BlockSpec((s, LANES), lambda j: (0, j)), ANY],
        out_specs=[pl.BlockSpec((s, LANES), lambda j: (0, col0 + j)), pl.BlockSpec((CONV_K, LANES), lambda j: (0, j)),
                   pl.BlockSpec((1, LANES), lambda j: (0, j))],
        out_shape=[jax.ShapeDtypeStruct(into.shape, into.dtype), jax.ShapeDtypeStruct((CONV_K, c_tot), F32),
                   jax.ShapeDtypeStruct((1, c_tot), F32)],
        input_output_aliases={4: 0},
        compiler_params=_cparams(1), name=name,
    )(src, w, b, dout, into)


def _chunk_masks(c):
    ii = lax.broadcasted_iota(jnp.int32, (c, c), 0)
    jj = lax.broadcasted_iota(jnp.int32, (c, c), 1)
    return ii, jj


def _row_to_col(row, eye):
    return jnp.sum(jnp.where(eye, row, 0.0), axis=1, keepdims=True)


def _each(f, *lists):
    return [f(*xs) for xs in zip(*lists)]


@jax.custom_vjp
def _nilpotent_inverse(nmats):
    c = nmats[0].shape[0]
    ii, jj = _chunk_masks(c)
    xinv = _each(lambda n: jnp.where(ii == jj, 1.0, 0.0) + n, nmats)
    pw = nmats
    for _ in range(int(math.log2(c)) - 1):
        pw = _each(lambda p: _dot(p, p, NN, HIGHEST), pw)
        xinv = _each(lambda x, p: x + _dot(x, p, NN, HIGHEST), xinv, pw)
    return xinv


def _nilpotent_inverse_fwd(nmats):
    xinv = _nilpotent_inverse(nmats)
    return xinv, xinv


def _nilpotent_inverse_bwd(xinv, cts):
    left = _each(lambda x, ct: _dot(x, ct, TN, HIGHEST), xinv, cts)
    return (_each(lambda l_, x: _dot(l_, x, NT, HIGHEST), left, xinv),)


_nilpotent_inverse.defvjp(_nilpotent_inverse_fwd, _nilpotent_inverse_bwd)


@jax.custom_vjp
def _saved_inverse(nmats, saved):
    return saved


def _saved_inverse_fwd(nmats, saved):
    return saved, saved


def _saved_inverse_bwd(xinv, cts):
    return _nilpotent_inverse_bwd(xinv, cts) + (_each(jnp.zeros_like, xinv),)


_saved_inverse.defvjp(_saved_inverse_fwd, _saved_inverse_bwd)


def _dn_chunk(q, k, v, a_row, b_row, alog, dtb, s0, saved_inverse=None):
    c = q[0].shape[0]
    ii, jj = _chunk_masks(c)
    causal, strict, eye = ii >= jj, ii > jj, ii == jj
    g_row = _each(lambda al, a, dt: -jnp.exp(al) * _softplus(a + dt), alog, a_row, dtb)
    beta_col = _each(lambda b: _row_to_col(_sigmoid(b), eye), b_row)
    g_col = _each(lambda g: _row_to_col(g, eye), g_row)
    gc_col = _each(lambda g: jnp.sum(jnp.where(causal, g, 0.0), axis=1, keepdims=True), g_row)
    gc_row = _each(lambda g: jnp.sum(jnp.where(jj >= ii, g, 0.0), axis=0, keepdims=True), g_col)
    decay = _each(lambda gc, gr: jnp.exp(jnp.where(causal, gc - gr, NEG_BIG)), gc_col, gc_row)
    kb = _each(jnp.multiply, k, beta_col)
    vb = _each(jnp.multiply, v, beta_col)
    nmat = _each(lambda kb_, k_, dc: -jnp.where(strict, _dot(kb_, k_, NT, HIGHEST) * dc, 0.0), kb, k, decay)
    xinv = _nilpotent_inverse(nmat) if saved_inverse is None else _saved_inverse(nmat, saved_inverse)
    egc = _each(jnp.exp, gc_col)
    u = _each(lambda x, vb_: _dot(x, vb_, NN, HIGHEST), xinv, vb)
    w = _each(lambda x, kb_, e: _dot(x, kb_ * e, NN, HIGHEST), xinv, kb, egc)
    qs = _each(lambda q_: q_ * (q_.shape[1] ** -0.5), q)
    attn = _each(lambda q_, k_, dc: _hdot(q_, k_, NT) * dc, qs, k, decay)
    gl = _each(lambda g: jnp.sum(g, axis=1, keepdims=True), g_row)
    kd = _each(lambda k_, gl_, gc: k_ * jnp.exp(gl_ - gc), k, gl, gc_col)
    v_new = _each(lambda u_, w_, s: u_ - _hdot(w_, s), u, w, s0)
    o = _each(lambda q_, e, s, at, vn: _hdot(q_ * e, s) + _hdot(at, vn), qs, egc, s0, attn, v_new)
    s1 = _each(lambda s, gl_, kd_, vn: s * jnp.exp(gl_) + _hdot(kd_, vn, TN), s0, gl, kd, v_new)
    return (o, s1), xinv


def _dn_specs(nh, nc, hb, rev):
    n_of = (lambda n: nc - 1 - n) if rev else (lambda n: n)
    ng = nh // hb
    qkv = [pl.BlockSpec((CHUNK, hb * DN_HEAD_DIM), (lambda h, n, o=o: (n_of(n), o * ng + h))) for o in range(3)]
    row = pl.BlockSpec((hb, None, 1, CHUNK), lambda h, n: (h, n_of(n), 0, 0))
    scal = pl.BlockSpec((hb, 1, 1), lambda h, n: (h, 0, 0))
    o_spec = pl.BlockSpec((CHUNK, hb * DN_HEAD_DIM), lambda h, n: (n_of(n), h))
    st = pl.BlockSpec((hb, None, DN_HEAD_DIM, DN_HEAD_DIM), lambda h, n: (h, n_of(n), 0, 0))
    inv = pl.BlockSpec((hb, None, CHUNK, CHUNK), lambda h, n: (h, n_of(n), 0, 0))
    return qkv, row, scal, o_spec, st, inv


def _dn_fwd(qkv, a_rows, b_rows, alog, dtb, *, name):
    s = qkv.shape[0]
    nh, nc = a_rows.shape[0], a_rows.shape[1]
    hb = min(DN_HEADS_PER_STEP, nh)
    qkv_specs, row, scal, o_spec, st, inv = _dn_specs(nh, nc, hb, False)
    hd = DN_HEAD_DIM

    def body(q_ref, k_ref, v_ref, a_ref, b_ref, al_ref, dt_ref, o_ref, st_ref, inv_ref, state):
        @pl.when(pl.program_id(1) == 0)
        def _():
            state[...] = jnp.zeros_like(state)

        cols = [slice(h * hd, (h + 1) * hd) for h in range(hb)]
        s0 = [state[h] for h in range(hb)]
        for h in range(hb):
            st_ref[h] = s0[h]
        (o, s1), xinv = _dn_chunk(
            [q_ref[:, cl] for cl in cols], [k_ref[:, cl] for cl in cols], [v_ref[:, cl] for cl in cols],
            [a_ref[h] for h in range(hb)], [b_ref[h] for h in range(hb)],
            [al_ref[h] for h in range(hb)], [dt_ref[h] for h in range(hb)], s0)
        for h in range(hb):
            o_ref[:, cols[h]] = o[h]
            inv_ref[h] = xinv[h]
            state[h] = s1[h]

    return pl.pallas_call(
        body, grid=(nh // hb, nc),
        in_specs=qkv_specs + [row, row, scal, scal],
        out_specs=[o_spec, st, inv],
        out_shape=[jax.ShapeDtypeStruct((s, nh * hd), F32), jax.ShapeDtypeStruct((nh, nc, hd, hd), F32),
                   jax.ShapeDtypeStruct((nh, nc, CHUNK, CHUNK), F32)],
        scratch_shapes=[pltpu.VMEM((hb, hd, hd), F32)],
        compiler_params=_cparams(2), name=name,
    )(qkv, qkv, qkv, a_rows, b_rows, alog, dtb)


def _dn_bwd(qkv, a_rows, b_rows, alog, dtb, states, inverses, do, *, name):
    s = qkv.shape[0]
    nh, nc = a_rows.shape[0], a_rows.shape[1]
    hb = min(DN_HEADS_PER_STEP, nh)
    qkv_specs, row, scal, o_spec, st, inv = _dn_specs(nh, nc, hb, True)
    hd = DN_HEAD_DIM

    def body(q_ref, k_ref, v_ref, a_ref, b_ref, al_ref, dt_ref, st_ref, inv_ref, do_ref,
             dq_ref, dk_ref, dv_ref, da_ref, db_ref, dal_ref, ddt_ref, dstate):
        @pl.when(pl.program_id(1) == 0)
        def _():
            dstate[...] = jnp.zeros_like(dstate)
            dal_ref[...] = jnp.zeros_like(dal_ref)
            ddt_ref[...] = jnp.zeros_like(ddt_ref)

        cols = [slice(h * hd, (h + 1) * hd) for h in range(hb)]
        heads = range(hb)
        args = ([q_ref[:, cl] for cl in cols], [k_ref[:, cl] for cl in cols], [v_ref[:, cl] for cl in cols],
                [a_ref[h] for h in heads], [b_ref[h] for h in heads], [al_ref[h] for h in heads],
                [dt_ref[h] for h in heads], [st_ref[h] for h in heads])
        saved = [inv_ref[h] for h in heads]
        _, vjp, _ = jax.vjp(lambda *a: _dn_chunk(*a, saved_inverse=saved), *args, has_aux=True)
        dq, dk, dv, da, db, dal, ddt, ds0 = vjp(([do_ref[:, cl] for cl in cols], [dstate[h] for h in heads]))
        for h in heads:
            dq_ref[:, cols[h]] = dq[h]
            dk_ref[:, cols[h]] = dk[h]
            dv_ref[:, cols[h]] = dv[h]
            da_ref[h] = da[h]
            db_ref[h] = db[h]
            dal_ref[h] += dal[h]
            ddt_ref[h] += ddt[h]
            dstate[h] = ds0[h]

    w = nh * hd
    outs = pl.pallas_call(
        body, grid=(nh // hb, nc),
        in_specs=qkv_specs + [row, row, scal, scal, st, inv, o_spec],
        out_specs=[o_spec, o_spec, o_spec, row, row, scal, scal],
        out_shape=[jax.ShapeDtypeStruct((s, w), F32)] * 3
        + [jax.ShapeDtypeStruct(a_rows.shape, F32)] * 2 + [jax.ShapeDtypeStruct((nh, 1, 1), F32)] * 2,
        scratch_shapes=[pltpu.VMEM((hb, hd, hd), F32)],
        compiler_params=_cparams(2), name=name,
    )(qkv, qkv, qkv, a_rows, b_rows, alog, dtb, states, inverses, do)
    return outs


def _dn_post_fwd(o, src, gate_col0, nw, *, name, tm=256):
    s, w = o.shape
    nh = w // DN_HEAD_DIM

    def body(o_ref, g_ref, w_ref, y_ref):
        ov = o_ref[...]
        r = lax.rsqrt(jnp.mean(ov * ov, axis=-1, keepdims=True) + EPS)
        y_ref[...] = (ov * r * w_ref[...] * _silu(g_ref[...])).astype(y_ref.dtype)

    blk = pl.BlockSpec((tm, DN_HEAD_DIM), lambda i, h: (i, h))
    return pl.pallas_call(
        body, grid=(s // tm, nh),
        in_specs=[blk, pl.BlockSpec((tm, DN_HEAD_DIM), lambda i, h: (i, gate_col0 + h)),
                  pl.BlockSpec((1, DN_HEAD_DIM), lambda i, h: (0, 0))],
        out_specs=blk, out_shape=jax.ShapeDtypeStruct((s, w), MXU_DTYPE),
        compiler_params=_cparams(2), name=name,
    )(o, src, nw.reshape(1, DN_HEAD_DIM))


def _dn_post_bwd(o, src, gate_col0, nw, dy, into, *, name, tm=256):
    s, w = o.shape
    nh = w // DN_HEAD_DIM

    def body(o_ref, g_ref, w_ref, dy_ref, into_ref, do_ref, dg_ref, dw_ref):
        ov = o_ref[...]
        gv = g_ref[...]
        dyv = dy_ref[...]
        r = lax.rsqrt(jnp.mean(ov * ov, axis=-1, keepdims=True) + EPS)
        oh = ov * r
        dn = dyv * _silu(gv)
        dg_ref[...] = (dyv * (oh * w_ref[...]) * _silu_grad(gv)).astype(dg_ref.dtype)
        don = dn * w_ref[...]
        do_ref[...] = r * (don - oh * jnp.mean(don * oh, axis=-1, keepdims=True))

        @pl.when((pl.program_id(0) == 0) & (pl.program_id(1) == 0))
        def _():
            dw_ref[...] = jnp.zeros_like(dw_ref)

        dw_ref[...] += jnp.sum(dn * oh, axis=0, keepdims=True)

    blk = pl.BlockSpec((tm, DN_HEAD_DIM), lambda i, h: (i, h))
    wspec = pl.BlockSpec((1, DN_HEAD_DIM), lambda i, h: (0, 0))
    gate_blk = pl.BlockSpec((tm, DN_HEAD_DIM), lambda i, h: (i, gate_col0 + h))
    do, dg, dw = pl.pallas_call(
        body, grid=(s // tm, nh),
        in_specs=[blk, gate_blk, wspec, blk, ANY],
        out_specs=[blk, gate_blk, wspec],
        out_shape=[jax.ShapeDtypeStruct((s, w), F32), jax.ShapeDtypeStruct(into.shape, into.dtype),
                   jax.ShapeDtypeStruct((1, DN_HEAD_DIM), F32)],
        input_output_aliases={4: 1},
        compiler_params=_cparams(2), name=name,
    )(o, src, nw.reshape(1, DN_HEAD_DIM), dy, into)
    return do, dg, dw.reshape(DN_HEAD_DIM)


def _sb_consts():
    r2 = lax.broadcasted_iota(jnp.int32, (2 * SB_BLOCK, SB_BLOCK), 0)
    c2 = lax.broadcasted_iota(jnp.int32, (2 * SB_BLOCK, SB_BLOCK), 1)
    r = lax.broadcasted_iota(jnp.int32, (SB_BLOCK, SB_BLOCK), 0)
    c = lax.broadcasted_iota(jnp.int32, (SB_BLOCK, SB_BLOCK), 1)
    lm0 = c < SB_HEAD_DIM
    m_gt = jnp.where(r > c, 1.0, 0.0).astype(BF16)
    m_lt = jnp.where(r < c, 1.0, 0.0).astype(BF16)
    return r2, c2, lm0, m_gt, m_lt


def _sb_stack(x, lm0):
    return jnp.concatenate([jnp.where(lm0, x, 0.0), jnp.where(lm0, 0.0, x)], axis=0)


def _sb_unstack(x2, lm0):
    return jnp.where(lm0, x2[:SB_BLOCK], x2[SB_BLOCK:])


def _sb_fwd(src, col0, width, *, name):
    s = src.shape[0]
    nq = s // SB_BLOCK
    npair = width // LANES
    scale = SB_HEAD_DIM ** -0.5
    nu = math.gcd(SB_UNROLL, nq)

    def body(q_ref, k_ref, v_ref, o_ref, r_ref):
        i = pl.program_id(1)
        r2, c2, lm0, m_gt, _ = _sb_consts()
        t_glob = i * SB_BLOCK + (r2 & (SB_BLOCK - 1))
        q2 = (_sb_stack(q_ref[...], lm0) * scale).astype(MXU_DTYPE)

        def group(base, carry, masked):
            o2, rsum = carry
            js = [base + nu - 1 - u for u in range(nu)]
            offs = [pl.multiple_of(j * SB_BLOCK, SB_BLOCK) for j in js]
            zs = [_dot(q2, k_ref[pl.ds(off, SB_BLOCK), :].astype(MXU_DTYPE), NT) for off in offs]
            ts = [jnp.log(1.0 + jnp.exp(-jnp.abs(z))) for z in zs]
            lks = [-(jnp.maximum(z, 0.0) + t) for z, t in zip(zs, ts)]
            if masked:
                masks = [(j * SB_BLOCK + c2) < t_glob for j in js]
                lks = [jnp.where(mk, lk, 0.0) for mk, lk in zip(masks, lks)]
            sufs = [_split_dot(lk, m_gt, SB_SPLIT) for lk in lks]
            rs = [rsum]
            for lk in lks:
                rs.append(rs[-1] + jnp.sum(lk, axis=1, keepdims=True))
            wgts = [jnp.exp((jnp.minimum(z, 0.0) - t) + r_ + sf) for z, t, r_, sf in zip(zs, ts, rs, sufs)]
            if masked:
                wgts = [jnp.where(mk, wg, 0.0) for mk, wg in zip(masks, wgts)]
            for off, wg in zip(offs, wgts):
                o2 = o2 + _dot(wg.astype(MXU_DTYPE), v_ref[pl.ds(off, SB_BLOCK), :].astype(MXU_DTYPE), NN)
            return o2, rs[-1]

        top0 = (i // nu) * nu
        carry = group(top0, (jnp.zeros((2 * SB_BLOCK, LANES), F32), jnp.zeros((2 * SB_BLOCK, 1), F32)), True)
        o2, rsum = lax.fori_loop(1, i // nu + 1, lambda g, cr: group(top0 - nu * g, cr, False), carry)
        o_ref[...] = _sb_unstack(o2, lm0)
        r_ref[...] = _sb_unstack(jnp.broadcast_to(rsum, (2 * SB_BLOCK, LANES)), lm0)

    blk = pl.BlockSpec((SB_BLOCK, LANES), lambda p, i: (i, p))
    return pl.pallas_call(
        body, grid=(npair, nq),
        in_specs=[pl.BlockSpec((SB_BLOCK, LANES), lambda p, i: (i, col0 + p)),
                  pl.BlockSpec((s, LANES), lambda p, i: (0, col0 + npair + p)),
                  pl.BlockSpec((s, LANES), lambda p, i: (0, col0 + 2 * npair + p))],
        out_specs=[blk, blk],
        out_shape=[jax.ShapeDtypeStruct((s, width), F32), jax.ShapeDtypeStruct((s, width), F32)],
        compiler_params=_cparams(2), name=name,
    )(src, src, src)


def _sb_bwd(src, col0, width, rtot, do, *, name):
    s = src.shape[0]
    nq = s // SB_BLOCK
    npair = width // LANES
    scale = SB_HEAD_DIM ** -0.5
    nu = math.gcd(SB_UNROLL, nq)

    def body(q_ref, k_ref, v_ref, r_ref, do_ref, dq_ref, dk_ref, dv_ref):
        i = pl.program_id(1)

        @pl.when(i == 0)
        def _():
            dk_ref[...] = jnp.zeros_like(dk_ref)
            dv_ref[...] = jnp.zeros_like(dv_ref)

        r2, c2, lm0, m_gt, m_lt = _sb_consts()
        t_glob = i * SB_BLOCK + (r2 & (SB_BLOCK - 1))
        q2 = (_sb_stack(q_ref[...], lm0) * scale).astype(MXU_DTYPE)
        do2 = _sb_stack(do_ref[...], lm0).astype(MXU_DTYPE)
        rv = r_ref[...]
        rt = jnp.concatenate([jnp.max(jnp.where(lm0, rv, NEG_BIG), axis=1, keepdims=True),
                              jnp.max(jnp.where(lm0, NEG_BIG, rv), axis=1, keepdims=True)], axis=0)

        def group(g, carry, masked):
            dq2, psum, csum = carry
            js = [nu * g + u for u in range(nu)]
            offs = [pl.multiple_of(j * SB_BLOCK, SB_BLOCK) for j in js]
            kbs = [k_ref[pl.ds(off, SB_BLOCK), :].astype(MXU_DTYPE) for off in offs]
            zs = [_dot(q2, kb, NT) for kb in kbs]
            dws = [_dot(do2, v_ref[pl.ds(off, SB_BLOCK), :].astype(MXU_DTYPE), NT) for off in offs]
            ts = [jnp.log(1.0 + jnp.exp(-jnp.abs(z))) for z in zs]
            lks = [-(jnp.maximum(z, 0.0) + t) for z, t in zip(zs, ts)]
            if masked:
                masks = [(j * SB_BLOCK + c2) < t_glob for j in js]
                lks = [jnp.where(mk, lk, 0.0) for mk, lk in zip(masks, lks)]
            sufs = [_split_dot(lk, m_gt, SB_SPLIT) for lk in lks]
            lsums = [jnp.sum(lk, axis=1, keepdims=True) for lk in lks]
            logsigs = [jnp.minimum(z, 0.0) - t for z, t in zip(zs, ts)]
            wgts = []
            for lsg, lsum, sf in zip(logsigs, lsums, sufs):
                psum = psum + lsum
                wgts.append(jnp.exp(lsg + (rt - psum) + sf))
            if masked:
                wgts = [jnp.where(mk, wg, 0.0) for mk, wg in zip(masks, wgts)]
            dlogas = [wg * dw for wg, dw in zip(wgts, dws)]
            pres = [_split_dot(dl, m_lt, SB_SPLIT) for dl in dlogas]
            dlks = []
            for dl, pre in zip(dlogas, pres):
                dlks.append(csum + pre)
                csum = csum + jnp.sum(dl, axis=1, keepdims=True)
            if masked:
                dlks = [jnp.where(mk, dlk, 0.0) for mk, dlk in zip(masks, dlks)]
            sigs = [jnp.exp(lsg) for lsg in logsigs]
            dzbs = [(dl * (1.0 - sg) - dlk * sg).astype(MXU_DTYPE) for dl, sg, dlk in zip(dlogas, sigs, dlks)]
            for off, dzb, wg, kb in zip(offs, dzbs, wgts, kbs):
                dk_ref[pl.ds(off, SB_BLOCK), :] += _dot(dzb, q2, TN)
                dv_ref[pl.ds(off, SB_BLOCK), :] += _dot(wg.astype(MXU_DTYPE), do2, TN)
                dq2 = dq2 + _dot(dzb, kb, NN)
            return dq2, psum, csum

        zero_col = jnp.zeros((2 * SB_BLOCK, 1), F32)
        carry = lax.fori_loop(0, i // nu, lambda g, cr: group(g, cr, False),
                              (jnp.zeros((2 * SB_BLOCK, LANES), F32), zero_col, zero_col))
        dq2, _, _ = group(i // nu, carry, True)
        dq_ref[...] = _sb_unstack(dq2, lm0) * scale

    blk = pl.BlockSpec((SB_BLOCK, LANES), lambda p, i: (i, p))
    full = pl.BlockSpec((s, LANES), lambda p, i: (0, p))
    return pl.pallas_call(
        body, grid=(npair, nq),
        in_specs=[pl.BlockSpec((SB_BLOCK, LANES), lambda p, i: (i, col0 + p)),
                  pl.BlockSpec((s, LANES), lambda p, i: (0, col0 + npair + p)),
                  pl.BlockSpec((s, LANES), lambda p, i: (0, col0 + 2 * npair + p)),
                  blk, blk],
        out_specs=[blk, full, full],
        out_shape=[jax.ShapeDtypeStruct((s, width), F32)] * 3,
        compiler_params=_cparams(2), name=name,
    )(src, src, src, rtot, do)


def _ssd_group(xs, dt_rows, alogs, dtbs, bms, cms, h0s):
    c = bms[0].shape[0]
    per = len(xs) // len(bms)
    ii, jj = _chunk_masks(c)
    causal, eye = ii >= jj, ii == jj
    grp = lambda per_group: [t for t in per_group for _ in range(per)]
    scores, bm, cm = grp(_each(lambda c_, b_: _hdot(c_, b_, NT), cms, bms)), grp(bms), grp(cms)
    dt_r = _each(lambda dt, b: _softplus(dt + b), dt_rows, dtbs)
    a_r = _each(lambda al, dt: -jnp.exp(al) * dt, alogs, dt_r)
    dt_col = _each(lambda dt: _row_to_col(dt, eye), dt_r)
    a_col = _each(lambda a: _row_to_col(a, eye), a_r)
    ac_col = _each(lambda a: jnp.sum(jnp.where(causal, a, 0.0), axis=1, keepdims=True), a_r)
    ac_row = _each(lambda a: jnp.sum(jnp.where(jj >= ii, a, 0.0), axis=0, keepdims=True), a_col)
    lmat = _each(lambda c_, r_: jnp.exp(jnp.where(causal, c_ - r_, NEG_BIG)), ac_col, ac_row)
    xdt = _each(jnp.multiply, xs, dt_col)
    al = _each(lambda a: jnp.sum(a, axis=1, keepdims=True), a_r)
    ys = _each(lambda sc, lm, xd, cm_, h0, ac: _hdot(sc * lm, xd) + _hdot(cm_, h0, NT) * jnp.exp(ac),
               scores, lmat, xdt, cm, h0s, ac_col)
    h1s = _each(lambda h0, al_, xd, ac, bm_: h0 * jnp.exp(al_) + _hdot(xd * jnp.exp(al_ - ac), bm_, TN),
                h0s, al, xdt, ac_col, bm)
    return ys, h1s


def _ssd_specs(ng, nc, r, gb, rev):
    n_of = (lambda n: nc - 1 - n) if rev else (lambda n: n)
    xw, bw = gb * r * SSM_HEAD_DIM, gb * SSM_STATE
    b0, c0 = (ng * r * SSM_HEAD_DIM) // bw, (ng * r * SSM_HEAD_DIM + ng * SSM_STATE) // bw
    x_spec = pl.BlockSpec((CHUNK, xw), lambda g, n: (n_of(n), g))
    b_spec = pl.BlockSpec((CHUNK, bw), lambda g, n: (n_of(n), b0 + g))
    c_spec = pl.BlockSpec((CHUNK, bw), lambda g, n: (n_of(n), c0 + g))
    dt_spec = pl.BlockSpec((gb, None, r, CHUNK), lambda g, n: (g, n_of(n), 0, 0))
    sc_spec = pl.BlockSpec((gb, r, 1), lambda g, n: (g, 0, 0))
    st_spec = pl.BlockSpec((gb, None, r, SSM_HEAD_DIM, SSM_STATE), lambda g, n: (g, n_of(n), 0, 0, 0))
    bc_out = pl.BlockSpec((CHUNK, bw), lambda g, n: (n_of(n), g))
    return x_spec, b_spec, c_spec, dt_spec, sc_spec, st_spec, x_spec, bc_out


def _ssd_refs(gb, r, x_ref, b_ref, c_ref, dt_ref, al_ref, db_ref):
    p, n = SSM_HEAD_DIM, SSM_STATE
    heads = [(g, h) for g in range(gb) for h in range(r)]
    xs = [x_ref[:, (g * r + h) * p:(g * r + h + 1) * p] for g, h in heads]
    dts = [dt_ref[g, h:h + 1, :] for g, h in heads]
    als = [al_ref[g, h:h + 1, :] for g, h in heads]
    dbs = [db_ref[g, h:h + 1, :] for g, h in heads]
    bms = [b_ref[:, g * n:(g + 1) * n] for g in range(gb)]
    cms = [c_ref[:, g * n:(g + 1) * n] for g in range(gb)]
    return heads, xs, dts, als, dbs, bms, cms


def _ssd_fwd(xbc, dt_rows, alog, dtb, *, name):
    s = xbc.shape[0]
    ng, nc, r = dt_rows.shape[0], dt_rows.shape[1], dt_rows.shape[2]
    w = ng * r * SSM_HEAD_DIM
    gb = math.gcd(SSD_GROUPS_PER_STEP, ng)
    x_spec, b_spec, c_spec, dt_spec, sc_spec, st_spec, y_spec, _ = _ssd_specs(ng, nc, r, gb, False)
    p = SSM_HEAD_DIM

    def body(x_ref, b_ref, c_ref, dt_ref, al_ref, db_ref, y_ref, st_ref, state):
        @pl.when(pl.program_id(1) == 0)
        def _():
            state[...] = jnp.zeros_like(state)

        st_ref[...] = state[...]
        heads, xs, dts, als, dbs, bms, cms = _ssd_refs(gb, r, x_ref, b_ref, c_ref, dt_ref, al_ref, db_ref)
        ys, h1s = _ssd_group(xs, dts, als, dbs, bms, cms, [state[g, h] for g, h in heads])
        for i, (g, h) in enumerate(heads):
            y_ref[:, (g * r + h) * p:(g * r + h + 1) * p] = ys[i]
            state[g, h] = h1s[i]

    return pl.pallas_call(
        body, grid=(ng // gb, nc),
        in_specs=[x_spec, b_spec, c_spec, dt_spec, sc_spec, sc_spec],
        out_specs=[y_spec, st_spec],
        out_shape=[jax.ShapeDtypeStruct((s, w), F32), jax.ShapeDtypeStruct((ng, nc, r, p, SSM_STATE), F32)],
        scratch_shapes=[pltpu.VMEM((gb, r, p, SSM_STATE), F32)],
        compiler_params=_cparams(2), name=name,
    )(xbc, xbc, xbc, dt_rows, alog, dtb)


def _ssd_bwd(xbc, dt_rows, alog, dtb, states, dy, *, name):
    s = xbc.shape[0]
    ng, nc, r = dt_rows.shape[0], dt_rows.shape[1], dt_rows.shape[2]
    w = ng * r * SSM_HEAD_DIM
    gb = math.gcd(SSD_GROUPS_PER_STEP, ng)
    x_spec, b_spec, c_spec, dt_spec, sc_spec, st_spec, y_spec, bc_out = _ssd_specs(ng, nc, r, gb, True)
    p = SSM_HEAD_DIM

    def body(x_ref, b_ref, c_ref, dt_ref, al_ref, db_ref, st_ref, dy_ref,
             dx_ref, dbm_ref, dcm_ref, ddt_ref, dal_ref, ddb_ref, dstate):
        @pl.when(pl.program_id(1) == 0)
        def _():
            dstate[...] = jnp.zeros_like(dstate)
            dal_ref[...] = jnp.zeros_like(dal_ref)
            ddb_ref[...] = jnp.zeros_like(ddb_ref)

        heads, xs, dts, als, dbs, bms, cms = _ssd_refs(gb, r, x_ref, b_ref, c_ref, dt_ref, al_ref, db_ref)
        _, vjp = jax.vjp(_ssd_group, xs, dts, als, dbs, bms, cms, [st_ref[g, h] for g, h in heads])
        dys = [dy_ref[:, (g * r + h) * p:(g * r + h + 1) * p] for g, h in heads]
        dxs, ddts, dals, ddbs, dbms, dcms, dh0s = vjp((dys, [dstate[g, h] for g, h in heads]))
        for g in range(gb):
            dbm_ref[:, g * SSM_STATE:(g + 1) * SSM_STATE] = dbms[g]
            dcm_ref[:, g * SSM_STATE:(g + 1) * SSM_STATE] = dcms[g]
        for i, (g, h) in enumerate(heads):
            dx_ref[:, (g * r + h) * p:(g * r + h + 1) * p] = dxs[i]
            ddt_ref[g, h:h + 1, :] = ddts[i]
            dal_ref[g, h:h + 1, :] += dals[i]
            ddb_ref[g, h:h + 1, :] += ddbs[i]
            dstate[g, h] = dh0s[i]

    gn = ng * SSM_STATE
    return pl.pallas_call(
        body, grid=(ng // gb, nc),
        in_specs=[x_spec, b_spec, c_spec, dt_spec, sc_spec, sc_spec, st_spec, y_spec],
        out_specs=[y_spec, bc_out, bc_out, dt_spec, sc_spec, sc_spec],
        out_shape=[jax.ShapeDtypeStruct((s, w), F32), jax.ShapeDtypeStruct((s, gn), F32), jax.ShapeDtypeStruct((s, gn), F32),
                   jax.ShapeDtypeStruct(dt_rows.shape, F32), jax.ShapeDtypeStruct((ng, r, 1), F32),
                   jax.ShapeDtypeStruct((ng, r, 1), F32)],
        scratch_shapes=[pltpu.VMEM((gb, r, p, SSM_STATE), F32)],
        compiler_params=_cparams(2), name=name,
    )(xbc, xbc, xbc, dt_rows, alog, dtb, states, dy)


def _ssm_post_fwd(y, xbc, src, z_col0, dexp, nw, *, name, tm=256):
    s, w = y.shape
    gw = w // SSM_GROUPS
    zc = z_col0 * LANES // gw

    def body(y_ref, x_ref, z_ref, d_ref, w_ref, o_ref):
        yy = (y_ref[...] + x_ref[...] * d_ref[...]) * _silu(z_ref[...])
        r = lax.rsqrt(jnp.mean(yy * yy, axis=-1, keepdims=True) + EPS)
        o_ref[...] = (yy * r * w_ref[...]).astype(o_ref.dtype)

    blk = pl.BlockSpec((tm, gw), lambda g, i: (i, g))
    vec = pl.BlockSpec((1, gw), lambda g, i: (0, g))
    return pl.pallas_call(
        body, grid=(SSM_GROUPS, s // tm),
        in_specs=[blk, blk, pl.BlockSpec((tm, gw), lambda g, i: (i, zc + g)), vec, vec],
        out_specs=blk, out_shape=jax.ShapeDtypeStruct((s, w), MXU_DTYPE),
        compiler_params=_cparams(2), name=name,
    )(y, xbc, src, dexp.reshape(1, w), nw.reshape(1, w))


def _ssm_post_bwd(y, xbc, src, z_col0, dexp, nw, dout, into, *, name, tm=256):
    s, w = y.shape
    gw = w // SSM_GROUPS
    zc = z_col0 * LANES // gw

    def body(y_ref, x_ref, z_ref, d_ref, w_ref, do_ref, into_ref, dy_ref, dx_ref, dz_ref, dd_ref, dw_ref):
        xv, zv, dv = x_ref[...], z_ref[...], d_ref[...]
        pre = y_ref[...] + xv * dv
        sz = _silu(zv)
        yy = pre * sz
        r = lax.rsqrt(jnp.mean(yy * yy, axis=-1, keepdims=True) + EPS)
        yh = yy * r
        dov = do_ref[...]
        dyn = dov * w_ref[...]
        dyy = r * (dyn - yh * jnp.mean(dyn * yh, axis=-1, keepdims=True))
        dpre = dyy * sz
        dy_ref[...] = dpre
        dx_ref[...] = dpre * dv
        dz_ref[...] = (dyy * pre * _silu_grad(zv)).astype(dz_ref.dtype)

        @pl.when(pl.program_id(1) == 0)
        def _():
            dd_ref[...] = jnp.zeros_like(dd_ref)
            dw_ref[...] = jnp.zeros_like(dw_ref)

        dd_ref[...] += jnp.sum(dpre * xv, axis=0, keepdims=True)
        dw_ref[...] += jnp.sum(dov * yh, axis=0, keepdims=True)

    blk = pl.BlockSpec((tm, gw), lambda g, i: (i, g))
    vec = pl.BlockSpec((1, gw), lambda g, i: (0, g))
    z_blk = pl.BlockSpec((tm, gw), lambda g, i: (i, zc + g))
    dy, dx, dz, dd, dw = pl.pallas_call(
        body, grid=(SSM_GROUPS, s // tm),
        in_specs=[blk, blk, z_blk, vec, vec, blk, ANY],
        out_specs=[blk, blk, z_blk, vec, vec],
        out_shape=[jax.ShapeDtypeStruct((s, w), F32), jax.ShapeDtypeStruct((s, w), F32),
                   jax.ShapeDtypeStruct(into.shape, into.dtype), jax.ShapeDtypeStruct((1, w), F32),
                   jax.ShapeDtypeStruct((1, w), F32)],
        input_output_aliases={6: 2},
        compiler_params=_cparams(2), name=name,
    )(y, xbc, src, dexp.reshape(1, w), nw.reshape(1, w), dout, into)
    return dy, dx, dz, dd.reshape(w), dw.reshape(w)


def _merge_fwd(proj3, src, gate_col0, d, *, name, tm=256):
    s = proj3.shape[0]
    nb = proj3.shape[1] // d
    gc = gate_col0 * LANES // d

    def body(*refs):
        p_refs, g_refs, o_ref = refs[:nb], refs[nb:2 * nb], refs[-1]
        acc = None
        for p_ref, g_ref in zip(p_refs, g_refs):
            term = _sigmoid(g_ref[...]) * p_ref[...]
            acc = term if acc is None else acc + term
        o_ref[...] = acc.astype(o_ref.dtype)

    p_specs = [pl.BlockSpec((tm, d), lambda i, b=b: (i, b)) for b in range(nb)]
    g_specs = [pl.BlockSpec((tm, d), lambda i, b=b: (i, gc + b)) for b in range(nb)]
    return pl.pallas_call(
        body, grid=(s // tm,), in_specs=p_specs + g_specs,
        out_specs=pl.BlockSpec((tm, d), lambda i: (i, 0)), out_shape=jax.ShapeDtypeStruct((s, d), MXU_DTYPE),
        compiler_params=_cparams(1), name=name,
    )(*([proj3] * nb), *([src] * nb))


def _merge_bwd(proj3, src, gate_col0, d, dmerged, into, *, name, tm=256):
    s = proj3.shape[0]
    nb = proj3.shape[1] // d
    gc = gate_col0 * LANES // d

    def body(p_ref, g_ref, dm_ref, into_ref, dp_ref, dg_ref):
        sg = _sigmoid(g_ref[...])
        dm = dm_ref[...]
        dp_ref[...] = (dm * sg).astype(dp_ref.dtype)
        dg_ref[...] = (dm * p_ref[...] * sg * (1.0 - sg)).astype(dg_ref.dtype)

    blk = pl.BlockSpec((tm, d), lambda i, b: (i, b))
    gate_blk = pl.BlockSpec((tm, d), lambda i, b: (i, gc + b))
    return pl.pallas_call(
        body, grid=(s // tm, nb),
        in_specs=[blk, gate_blk, pl.BlockSpec((tm, d), lambda i, b: (i, 0)), ANY],
        out_specs=[blk, gate_blk],
        out_shape=[jax.ShapeDtypeStruct(proj3.shape, MXU_DTYPE), jax.ShapeDtypeStruct(into.shape, into.dtype)],
        input_output_aliases={3: 1},
        compiler_params=_cparams(2), name=name,
    )(proj3, src, dmerged, into)


ANY = pl.BlockSpec(memory_space=pl.ANY)
MESH = pl.DeviceIdType.MESH


def _all_gather(shards, *, name, after=None):
    nt = len(shards)
    n_after = 0 if after is None else 1

    def body(*refs):
        x_refs, out_refs = refs[:nt], refs[nt + n_after:2 * nt + n_after]
        send_sems, recv_sems, local_sems = refs[2 * nt + n_after:]
        x, y, c = lax.axis_index("x"), lax.axis_index("y"), lax.axis_index("c")
        me, sibling = (x, y, c), (x, y, 1 - c)
        chips = [(1 - x, y), (x, 1 - y), (1 - x, 1 - y)]

        def slot(t, px, py, pc):
            return out_refs[t].at[4 * px + 2 * py + pc]

        def copy(t, k, block, to, from_input=False):
            return pltpu.make_async_remote_copy(
                src_ref=x_refs[t] if from_input else slot(t, *block), dst_ref=slot(t, *block),
                send_sem=send_sems.at[7 * t + k], recv_sem=recv_sems.at[7 * t + k], device_id=to, device_id_type=MESH)

        mine = [pltpu.make_async_copy(x_refs[t], slot(t, *me), local_sems.at[t]) for t in range(nt)]
        for cp in mine:
            cp.start()
        first = [copy(t, 0, me, sibling, True) for t in range(nt)]
        first += [copy(t, 1 + j, me, (*chip, c), True) for j, chip in enumerate(chips) for t in range(nt)]
        for cp in first:
            cp.start()
        passed = []
        for j, chip in enumerate(chips):
            for t in range(nt):
                copy(t, 1 + j, (*chip, c), me).wait_recv()
                fwd = copy(t, 4 + j, (*chip, c), sibling)
                fwd.start()
                passed.append(fwd)
        for t in range(nt):
            copy(t, 0, sibling, me).wait_recv()
            for j, chip in enumerate(chips):
                copy(t, 4 + j, (*chip, 1 - c), me).wait_recv()
        for cp in first + passed:
            cp.wait_send()
        for cp in mine:
            cp.wait()

    return pl.pallas_call(
        body, out_shape=[jax.ShapeDtypeStruct((N_DEV,) + a.shape, a.dtype) for a in shards],
        in_specs=[ANY] * (nt + n_after), out_specs=[ANY] * nt,
        scratch_shapes=[pltpu.SemaphoreType.DMA((7 * nt,)), pltpu.SemaphoreType.DMA((7 * nt,)),
                        pltpu.SemaphoreType.DMA((nt,))],
        name=name,
    )(*shards, *([] if after is None else [after]))


def _grad_exchange(bigs, small, *, name):
    nl = len(bigs[0])
    flat = [a for per_layer in bigs for a in per_layer]
    nslot = len(flat)

    def body(*refs):
        in_refs, small_ref = refs[:nslot], refs[nslot]
        out_refs, smallr_ref = refs[nslot + 1:nslot + 1 + len(bigs)], refs[nslot + 1 + len(bigs)]
        send_sems, recv_sems, local_sems = refs[nslot + 2 + len(bigs):]
        x, y, c = lax.axis_index("x"), lax.axis_index("y"), lax.axis_index("c")
        me = 4 * x + 2 * y + c
        local = [pltpu.make_async_copy(in_refs[i].at[me], out_refs[i // nl].at[me, i % nl], local_sems.at[i])
                 for i in range(nslot)]
        local.append(pltpu.make_async_copy(small_ref, smallr_ref.at[me], local_sems.at[nslot]))
        for cp in local:
            cp.start()
        copies = []
        for k in range(1, N_DEV):
            px = x ^ ((k >> 2) & 1)
            py = y ^ ((k >> 1) & 1)
            pc = c ^ (k & 1)
            peer = 4 * px + 2 * py + pc
            for i in range(nslot + 1):
                sem = 7 * i + (k - 1)
                src = in_refs[i].at[peer] if i < nslot else small_ref
                dst = out_refs[i // nl].at[me, i % nl] if i < nslot else smallr_ref.at[me]
                copies.append(pltpu.make_async_remote_copy(
                    src_ref=src, dst_ref=dst, send_sem=send_sems.at[sem], recv_sem=recv_sems.at[sem],
                    device_id=(px, py, pc), device_id_type=MESH))
        for cp in copies:
            cp.start()
        for cp in copies:
            cp.wait_recv()
        for cp in copies:
            cp.wait_send()
        for cp in local:
            cp.wait()

    out_shape = [jax.ShapeDtypeStruct((N_DEV, nl) + per_layer[0].shape[1:], per_layer[0].dtype) for per_layer in bigs]
    out_shape.append(jax.ShapeDtypeStruct((N_DEV,) + small.shape, small.dtype))
    nsem = 7 * (nslot + 1)
    outs = pl.pallas_call(
        body, out_shape=out_shape,
        in_specs=[ANY] * (nslot + 1), out_specs=[ANY] * (len(bigs) + 1),
        scratch_shapes=[pltpu.SemaphoreType.DMA((nsem,)), pltpu.SemaphoreType.DMA((nsem,)),
                        pltpu.SemaphoreType.DMA((nslot + 1,))],
        name=name,
    )(*flat, small)
    return outs[:-1], outs[-1]


HBM = pl.BlockSpec(memory_space=pltpu.HBM)
SEM = pl.BlockSpec(memory_space=pltpu.SEMAPHORE)
EFFECT = pltpu.SideEffectType.DATAFLOW_SIDE_EFFECTING


def _peers():
    x, y, c = lax.axis_index("x"), lax.axis_index("y"), lax.axis_index("c")
    peers = []
    for k in range(1, N_DEV):
        px, py, pc = x ^ ((k >> 2) & 1), y ^ ((k >> 1) & 1), c ^ (k & 1)
        peers.append(((px, py, pc), 4 * px + 2 * py + pc))
    return 4 * x + 2 * y + c, peers


def _split_copies(slots, src_refs, land_refs, send_sems, recv_sems):
    me, peers = _peers()
    copies = []
    for t, (whole, layer) in enumerate(slots):
        dst = land_refs[t].at[me] if layer is None else land_refs[t].at[me, layer]
        for k, (dev, lin) in enumerate(peers):
            copies.append(pltpu.make_async_remote_copy(
                src_ref=src_refs[t] if whole else src_refs[t].at[lin], dst_ref=dst,
                send_sem=send_sems.at[7 * t + k], recv_sem=recv_sems.at[7 * t + k], device_id=dev, device_id_type=MESH))
    return copies


def _split_start(srcs, lands, slots, carry, *, name):
    n = len(srcs)

    def body(*refs):
        copies = _split_copies(slots, refs[:n], refs[n:2 * n], refs[2 * n + 1], refs[2 * n + 2])
        for cp in copies:
            cp.start()

    def hbm(a):
        return pltpu.HBM(a.shape, a.dtype)

    outs = pl.pallas_call(
        body, name=name,
        out_shape=[pltpu.SemaphoreType.DMA((7 * n,)), pltpu.SemaphoreType.DMA((7 * n,))]
        + [hbm(a) for a in srcs] + [hbm(a) for a in lands] + [hbm(carry)],
        in_specs=[HBM] * (2 * n + 1), out_specs=[SEM, SEM] + [HBM] * (2 * n + 1),
        input_output_aliases={i: 2 + i for i in range(2 * n + 1)},
        compiler_params=pltpu.CompilerParams(has_side_effects=EFFECT),
    )(*[pltpu.with_memory_space_constraint(a, pltpu.HBM) for a in list(srcs) + list(lands) + [carry]])
    return outs[0], outs[1], outs[2:2 + n], outs[2 + n:2 + 2 * n], outs[2 + 2 * n]


def _split_wait(send_sems, recv_sems, srcs, lands, slots, after, *, name):
    n = len(srcs)

    def body(*refs):
        copies = _split_copies(slots, refs[:n], refs[n:2 * n], refs[2 * n], refs[2 * n + 1])
        for cp in copies:
            cp.wait_send()
        for cp in copies:
            cp.wait_recv()

    outs = pl.pallas_call(
        body, name=name,
        out_shape=[pltpu.HBM(a.shape, a.dtype) for a in list(srcs) + list(lands)],
        in_specs=[HBM] * (2 * n) + [SEM, SEM, ANY], out_specs=[HBM] * (2 * n),
        input_output_aliases={i: i for i in range(2 * n)},
        compiler_params=pltpu.CompilerParams(has_side_effects=EFFECT),
    )(*srcs, *lands, send_sems, recv_sems, after)
    return outs[n:]


def _adam_math(w, g, m, v):
    m1 = ADAM_B1 * m + (1.0 - ADAM_B1) * g
    v1 = ADAM_B2 * v + (1.0 - ADAM_B2) * (g * g)
    m_hat = m1 / (1.0 - ADAM_B1 ** ADAM_STEP)
    v_hat = v1 / (1.0 - ADAM_B2 ** ADAM_STEP)
    delta = -ADAM_LR * (m_hat / (jnp.sqrt(v_hat) + ADAM_EPS) + ADAM_WD * w)
    return delta, m1, v1


def _sum_adamw(parts, w, m, v, layer, prev, *, name):
    shape = w.shape
    r, c = shape[-2], shape[-1]
    a_l = math.prod(shape[1:-2])
    a = shape[0] * a_l
    base = layer * a_l
    if r % 256 == 0:
        tr, tc = 256, c
    else:
        tr, tc = r, _pick(c, (256, 128))
    w3, m3, v3 = (t.reshape(a, r, c) for t in (w, m, v))
    n_prev = 0 if prev is None else 4

    def body(*refs):
        p_ref, w_ref, m_ref, v_ref = refs[:4]
        g_ref, d_ref, m1_ref, v1_ref = refs[4 + n_prev:]
        g = p_ref[0].astype(F32)
        for src in range(1, N_DEV):
            g = g + p_ref[src].astype(F32)
        delta, m1, v1 = _adam_math(w_ref[...], g, m_ref[...], v_ref[...])
        g_ref[...] = g
        d_ref[...] = delta
        m1_ref[...] = m1
        v1_ref[...] = v1

    nr, ncol = r // tr, c // tc
    blk = pl.BlockSpec((None, tr, tc), lambda i, j: (base + i, j // ncol, j % ncol))
    prev3 = [] if prev is None else [t.reshape(a, r, c) for t in prev]
    outs = pl.pallas_call(
        body, grid=(a_l, nr * ncol),
        in_specs=[pl.BlockSpec((N_DEV, None, tr, tc), lambda i, j: (0, i, j // ncol, j % ncol)), blk, blk, blk]
        + [ANY] * n_prev,
        out_specs=[blk] * 4, out_shape=[jax.ShapeDtypeStruct((a, r, c), F32)] * 4,
        input_output_aliases={4 + k: k for k in range(n_prev)},
        compiler_params=_cparams(2), name=name,
    )(parts.reshape(N_DEV, a_l, r, c), w3, m3, v3, *prev3)
    return [o.reshape(shape) for o in outs]


def _sum_parts(parts, *, name):
    rows = parts.shape[1]

    def body(p_ref, o_ref):
        g = p_ref[0]
        for src in range(1, N_DEV):
            g = g + p_ref[src]
        o_ref[...] = g

    return pl.pallas_call(
        body, grid=(1,), in_specs=[pl.BlockSpec((N_DEV, rows, LANES), lambda i: (0, 0, 0))],
        out_specs=pl.BlockSpec((rows, LANES), lambda i: (0, 0)), out_shape=jax.ShapeDtypeStruct((rows, LANES), F32),
        compiler_params=_cparams(1), name=name,
    )(parts)


def _adamw(w, g, m, v, *, name):
    rows = w.shape[0]

    def body(w_ref, g_ref, m_ref, v_ref, d_ref, m1_ref, v1_ref):
        delta, m1, v1 = _adam_math(w_ref[...], g_ref[...], m_ref[...], v_ref[...])
        d_ref[...] = delta
        m1_ref[...] = m1
        v1_ref[...] = v1

    blk = pl.BlockSpec((rows, LANES), lambda i: (0, 0))
    return pl.pallas_call(
        body, grid=(1,), in_specs=[blk] * 4, out_specs=[blk] * 3,
        out_shape=[jax.ShapeDtypeStruct((rows, LANES), F32)] * 3,
        compiler_params=_cparams(1), name=name,
    )(w, g, m, v)


def _pack(arrs, dtype, row_mult=16):
    flat = jnp.concatenate([a.reshape(-1).astype(dtype) for a in arrs])
    n = flat.shape[0]
    rows = -(-n // (LANES * row_mult)) * row_mult
    flat = jnp.pad(flat, (0, rows * LANES - n))
    return flat.reshape(rows, LANES)


def _unpack(packed, shapes):
    flat = packed.reshape(-1)
    out, off = [], 0
    for shp in shapes:
        n = math.prod(shp)
        out.append(flat[off:off + n].reshape(shp))
        off += n
    return out


class _Layout:
    def __init__(self, d):
        self.d = d
        w = d
        self.dn_heads = w // DN_HEAD_DIM
        self.ssm_heads = w // SSM_HEAD_DIM
        gn = SSM_GROUPS * SSM_STATE
        self.sizes = (3 * w, w, self.dn_heads, self.dn_heads, 3 * w, w, w + 2 * gn, self.ssm_heads, 3 * d)
        offs, o = [], 0
        for sz in self.sizes:
            offs.append(o)
            o += sz
        self.offs = offs
        self.in_dim = o
        self.big = (0, 1, 4, 5, 6, 8)
        self.small = (2, 3, 7)
        cols, o = {}, 0
        for idx in self.big:
            cols[idx] = o
            o += self.sizes[idx]
        self.small_col = o
        self.cols = cols
        self.padded = o + LANES
        self.n_small = sum(self.sizes[i] for i in self.small)

    def reorder_w(self, w_in):
        parts = [w_in[:, self.offs[i]:self.offs[i] + self.sizes[i]] for i in self.big + self.small]
        parts.append(jnp.zeros((w_in.shape[0], LANES - self.n_small), w_in.dtype))
        return jnp.concatenate(parts, axis=1)

    def from_shards(self, parts):
        cs = self.in_dim // N_DEV
        pieces = []
        for i in self.big + self.small:
            a, b = self.offs[i], self.offs[i] + self.sizes[i]
            while a < b:
                j = a // cs
                hi = min(b, (j + 1) * cs)
                pieces.append(parts[j][:, a - j * cs:hi - j * cs])
                a = hi
        pieces.append(jnp.zeros((parts.shape[1], LANES - self.n_small), parts.dtype))
        return jnp.concatenate(pieces, axis=1)

    def to_shards(self, wp):
        cs = self.in_dim // N_DEV
        pcol = dict(self.cols)
        o = self.small_col
        for i in self.small:
            pcol[i] = o
            o += self.sizes[i]
        shards = []
        for j in range(N_DEV):
            a, b = j * cs, (j + 1) * cs
            pieces = []
            for i in range(len(self.sizes)):
                lo, hi = max(a, self.offs[i]), min(b, self.offs[i] + self.sizes[i])
                if lo < hi:
                    pieces.append(wp[:, pcol[i] + lo - self.offs[i]:pcol[i] + hi - self.offs[i]])
            shards.append(jnp.concatenate(pieces, axis=1))
        return jnp.stack(shards)

    def restore_w(self, wp):
        pieces = {}
        for idx in self.big:
            pieces[idx] = wp[:, self.cols[idx]:self.cols[idx] + self.sizes[idx]]
        o = self.small_col
        for idx in self.small:
            pieces[idx] = wp[:, o:o + self.sizes[idx]]
            o += self.sizes[idx]
        return jnp.concatenate([pieces[i] for i in range(len(self.sizes))], axis=1)


def _rows_form(cols_t, nh, nc):
    return cols_t.T.reshape(nh, nc, 1, CHUNK)


def _layer_fwd(x, p, lay, tag, late=None):
    s, d = x.shape
    nc = s // CHUNK
    w = d
    dnh, smh = lay.dn_heads, lay.ssm_heads
    r = smh // SSM_GROUPS
    cb = {k: v // LANES for k, v in lay.cols.items()}
    sv = {}
    h1 = _rms_fwd(x, p["norm_mix"], name=f"rms_mix_{tag}")
    proj = _matmul(h1, p["w_in"], name=f"mm_in_{tag}")
    small = proj[:, lay.small_col:lay.small_col + LANES]
    a_rows = _rows_form(small[:, 0:dnh], dnh, nc)
    b_rows = _rows_form(small[:, dnh:2 * dnh], dnh, nc)
    dt_rows = small[:, 2 * dnh:2 * dnh + smh].T.reshape(SSM_GROUPS, r, nc, CHUNK).transpose(0, 2, 1, 3)
    zero_b = jnp.zeros((1, 3 * w), F32)
    dn_qkv = _conv_fwd(proj, cb[0], p["dn_conv_w"], zero_b, 2 * dnh, name=f"dn_conv_{tag}")
    dn_alog = p["dn_a_log"].reshape(dnh, 1, 1)
    dn_dtb = p["dn_dt_bias"].reshape(dnh, 1, 1)
    o_dn, dn_states, dn_inv = _dn_fwd(dn_qkv, a_rows, b_rows, dn_alog, dn_dtb, name=f"dn_chunk_{tag}")
    y_dn = _dn_post_fwd(o_dn, proj, cb[1], p["dn_norm_w"], name=f"dn_post_{tag}")
    o_sb, sb_r = _sb_fwd(proj, cb[4], w, name=f"sb_{tag}")
    xbc = _conv_fwd(proj, cb[6], p["ssm_conv_w"], p["ssm_conv_b"].reshape(1, -1), 0, name=f"ssm_conv_{tag}")
    ssm_alog = p["ssm_a_log"].reshape(SSM_GROUPS, r, 1)
    ssm_dtb = p["ssm_dt_bias"].reshape(SSM_GROUPS, r, 1)
    y_ssd, ssm_states = _ssd_fwd(xbc, dt_rows, ssm_alog, ssm_dtb, name=f"ssd_{tag}")
    dexp = jnp.repeat(p["ssm_d"], SSM_HEAD_DIM)
    y_ssm = _ssm_post_fwd(y_ssd, xbc, proj, cb[5], dexp, p["ssm_norm_w"], name=f"ssm_post_{tag}")
    if late is not None:
        p.update(late(y_ssm))
    branches = (y_dn, o_sb, y_ssm)
    proj3 = jnp.concatenate(
        [_matmul(br, p["w_branch"][i], name=f"mm_branch{i}_{tag}") for i, br in enumerate(branches)], axis=1)
    merged = _merge_fwd(proj3, proj, cb[8], d, name=f"merge_{tag}")
    x1 = _matmul(merged, p["w_out"], name=f"mm_out_{tag}", epilogue=lambda acc, res: (acc + res,), extras=(x,))
    h2 = _rms_fwd(x1, p["norm_mlp"], name=f"rms_mlp_{tag}")
    u, act = _matmul(h2, p["w_up"], name=f"mm_up_{tag}", out_dtypes=(F32, MXU_DTYPE),
                     epilogue=lambda acc: (acc, jnp.square(jnp.maximum(acc, 0.0))))
    x2 = _matmul(act, p["w_down"], name=f"mm_down_{tag}", epilogue=lambda acc, res: (acc + res,), extras=(x1,))
    sv.update(x=x, h1=h1, proj=proj, a_rows=a_rows, b_rows=b_rows, dt_rows=dt_rows, dn_qkv=dn_qkv, dn_alog=dn_alog,
              dn_dtb=dn_dtb, o_dn=o_dn, dn_states=dn_states, dn_inv=dn_inv, y_dn=y_dn, o_sb=o_sb, sb_r=sb_r, xbc=xbc, ssm_alog=ssm_alog,
              ssm_dtb=ssm_dtb, y_ssd=y_ssd, ssm_states=ssm_states, dexp=dexp, y_ssm=y_ssm, proj3=proj3, merged=merged,
              x1=x1, h2=h2, u=u, act=act)
    return x2, sv


def _layer_bwd(dx2, p, sv, lay, tag, early=None, late=None):
    x = sv["x"]
    s, d = x.shape
    nc = s // CHUNK
    w = d
    dnh, smh = lay.dn_heads, lay.ssm_heads
    r = smh // SSM_GROUPS
    gn = SSM_GROUPS * SSM_STATE
    cb = {k: v // LANES for k, v in lay.cols.items()}
    proj = sv["proj"]
    g = {}
    dx2_b = dx2.astype(MXU_DTYPE)
    du = _matmul(dx2_b, p["w_down"], tb=True, name=f"mm_down_dx_{tag}", out_dtypes=(MXU_DTYPE,),
                 epilogue=lambda acc, uu: (acc * (2.0 * jnp.maximum(uu, 0.0)),), extras=(sv["u"],))
    g["w_down"] = _matmul(sv["act"], dx2_b, ta=True, name=f"mm_down_dw_{tag}", out_dtypes=(BF16,)).reshape(N_DEV, -1, d)
    g["w_up"] = _matmul(sv["h2"], du, ta=True, name=f"mm_up_dw_{tag}", out_dtypes=(BF16,), col_shards=N_DEV)
    dh2 = _matmul(du, p["w_up"], tb=True, name=f"mm_up_dx_{tag}")
    dx1, g["norm_mlp"] = _rms_bwd(sv["x1"], p["norm_mlp"], dh2, dx2, name=f"rms_mlp_bwd_{tag}")
    dx1_b = dx1.astype(MXU_DTYPE)
    dmerged = _matmul(dx1_b, p["w_out"], tb=True, name=f"mm_out_dx_{tag}")
    g["w_out"] = _matmul(sv["merged"], dx1_b, ta=True, name=f"mm_out_dw_{tag}", out_dtypes=(BF16,)).reshape(N_DEV, -1, d)
    dproj = lax.empty((s, lay.padded), MXU_DTYPE)
    dproj3, dproj = _merge_bwd(sv["proj3"], proj, cb[8], d, dmerged, dproj, name=f"merge_bwd_{tag}")
    branches = (sv["y_dn"], sv["o_sb"], sv["y_ssm"])
    dwb, dbr = [], []
    for i, br in enumerate(branches):
        dp_i = dproj3[:, i * d:(i + 1) * d]
        dwb.append(_matmul(br, dp_i, ta=True, name=f"mm_branch{i}_dw_{tag}", out_dtypes=(BF16,)).reshape(N_DEV, -1, d))
        dbr.append(_matmul(dp_i, p["w_branch"][i], tb=True, name=f"mm_branch{i}_dx_{tag}"))
    g["w_branch"] = jnp.stack(dwb, axis=1)
    dy_dn, do_sb, dy_ssm = dbr
    if early is not None:
        dy_ssm = early(g, dy_ssm)
    dy_ssd, dxs_skip, dproj, ddexp, g["ssm_norm_w"] = _ssm_post_bwd(
        sv["y_ssd"], sv["xbc"], proj, cb[5], sv["dexp"], p["ssm_norm_w"], dy_ssm, dproj, name=f"ssm_post_bwd_{tag}")
    g["ssm_d"] = ddexp.reshape(smh, SSM_HEAD_DIM).sum(axis=1)
    dxs, dbm, dcm, ddt_rows, dalog, ddtb = _ssd_bwd(
        sv["xbc"], sv["dt_rows"], sv["ssm_alog"], sv["ssm_dtb"], sv["ssm_states"], dy_ssd, name=f"ssd_bwd_{tag}")
    g["ssm_a_log"] = dalog.reshape(smh)
    g["ssm_dt_bias"] = ddtb.reshape(smh)
    dxbc_post = jnp.concatenate([dxs + dxs_skip, dbm, dcm], axis=1)
    dproj, g["ssm_conv_w"], dcb = _conv_bwd(proj, cb[6], p["ssm_conv_w"], p["ssm_conv_b"].reshape(1, -1), 0, dxbc_post,
                                            dproj, name=f"ssm_conv_bwd_{tag}")
    g["ssm_conv_b"] = dcb.reshape(-1)
    ddt = ddt_rows.transpose(0, 2, 1, 3).reshape(smh, s).T
    dqkv_sb = _sb_bwd(proj, cb[4], w, sv["sb_r"], do_sb, name=f"sb_bwd_{tag}")
    dproj = lax.dynamic_update_slice(dproj, jnp.concatenate([t.astype(MXU_DTYPE) for t in dqkv_sb], axis=1), (0, lay.cols[4]))
    do_dn, dproj, g["dn_norm_w"] = _dn_post_bwd(sv["o_dn"], proj, cb[1], p["dn_norm_w"], dy_dn, dproj,
                                                name=f"dn_post_bwd_{tag}")
    dq, dk, dv, da_rows, db_rows, dal, ddtb_dn = _dn_bwd(
        sv["dn_qkv"], sv["a_rows"], sv["b_rows"], sv["dn_alog"], sv["dn_dtb"], sv["dn_states"], sv["dn_inv"], do_dn,
        name=f"dn_chunk_bwd_{tag}")
    g["dn_a_log"] = dal.reshape(dnh)
    g["dn_dt_bias"] = ddtb_dn.reshape(dnh)
    zero_b = jnp.zeros((1, 3 * w), F32)
    dproj, g["dn_conv_w"], _ = _conv_bwd(proj, cb[0], p["dn_conv_w"], zero_b, 2 * dnh,
                                         jnp.concatenate([dq, dk, dv], axis=1), dproj, name=f"dn_conv_bwd_{tag}")
    da = da_rows.reshape(dnh, s).T
    db = db_rows.reshape(dnh, s).T
    dsmall = jnp.concatenate([da, db, ddt, jnp.zeros((s, LANES - lay.n_small), F32)], axis=1).astype(MXU_DTYPE)
    dproj = lax.dynamic_update_slice(dproj, dsmall, (0, lay.small_col))
    g["w_in"] = lay.to_shards(_matmul(sv["h1"], dproj, ta=True, name=f"mm_in_dw_{tag}", out_dtypes=(BF16,)))
    if late is not None:
        dproj = late(g, dproj)
    dh1 = _matmul(dproj, p["w_in"], tb=True, name=f"mm_in_dx_{tag}")
    dx0, g["norm_mix"] = _rms_bwd(x, p["norm_mix"], dh1, dx1, name=f"rms_mix_bwd_{tag}")
    return dx0, g


BIG = ("w_in", "w_branch", "w_out", "w_up", "w_down")
CONV = ("dn_conv_w", "ssm_conv_w")
SMALL = ("norm_mix", "dn_conv_w", "dn_a_log", "dn_dt_bias", "dn_norm_w", "ssm_conv_w", "ssm_conv_b", "ssm_a_log",
         "ssm_dt_bias", "ssm_d", "ssm_norm_w", "norm_mlp", "norm_final")
WEIGHTS = ("norm_mix", "w_in", "dn_conv_w", "dn_a_log", "dn_dt_bias", "dn_norm_w", "ssm_conv_w", "ssm_conv_b", "ssm_a_log",
           "ssm_dt_bias", "ssm_d", "ssm_norm_w", "w_branch", "w_out", "norm_mlp", "w_up", "w_down", "norm_final")
SHARD_AXIS = {"w_in": 2, "dn_conv_w": 2, "ssm_conv_w": 2, "w_branch": 2, "w_out": 1, "w_up": 2, "w_down": 1}


def _to_shards(full, axis):
    shp = full.shape
    n = shp[axis] // N_DEV
    t = full.reshape(shp[:axis] + (N_DEV, n) + shp[axis + 1:])
    return jnp.moveaxis(t, axis, 0)


def _from_shards(parts, axis):
    t = jnp.moveaxis(parts, 0, axis)
    shp = t.shape
    return t.reshape(shp[:axis] + (shp[axis] * shp[axis + 1],) + shp[axis + 2:])


def _unshard(parts, axis, *, name):
    shard = parts.shape[1:]
    nd = len(shard)
    if axis == 0:
        return parts.reshape((N_DEV * shard[0],) + shard[1:])

    def copy_block(i_ref, o_ref):
        o_ref[...] = i_ref[...]

    if axis == nd - 1:
        rows, n = math.prod(shard[:-1]), shard[-1]
        out = pl.pallas_call(
            copy_block, grid=(N_DEV,),
            in_specs=[pl.BlockSpec((None, rows, n), lambda j: (j, 0, 0))],
            out_specs=pl.BlockSpec((rows, n), lambda j: (0, j)),
            out_shape=jax.ShapeDtypeStruct((rows, N_DEV * n), parts.dtype),
            compiler_params=_cparams(1), name=name,
        )(parts.reshape(N_DEV, rows, n))
        return out.reshape(shard[:-1] + (N_DEV * n,))
    assert axis == nd - 2, (parts.shape, axis)
    a, n, c = math.prod(shard[:-2]), shard[-2], shard[-1]
    out = pl.pallas_call(
        copy_block, grid=(N_DEV, a),
        in_specs=[pl.BlockSpec((None, None, n, c), lambda j, i: (j, i, 0, 0))],
        out_specs=pl.BlockSpec((None, n, c), lambda j, i: (i, j, 0)),
        out_shape=jax.ShapeDtypeStruct((a, N_DEV * n, c), parts.dtype),
        compiler_params=_cparams(2), name=name,
    )(parts.reshape(N_DEV, a, n, c))
    return out.reshape(shard[:-2] + (N_DEV * n, c))


def _step(w, m, v, x, target):
    s, d = x.shape
    lay = _Layout(d)
    me = 4 * lax.axis_index("x") + 2 * lax.axis_index("y") + lax.axis_index("c")

    def shard(n, l):
        return w[n][l].astype(BF16) if n in BIG else w[n][l]

    def empty_land(a):
        return lax.empty((N_DEV,) + a.shape, a.dtype)

    def with_own(land, own):
        return lax.dynamic_update_index_in_dim(land, own, me, 0)

    def assemble(n, parts, l):
        return lay.from_shards(parts) if n == "w_in" else _unshard(parts, SHARD_AXIS[n] - 1, name=f"unshard_{n}_l{l}")

    small_names = tuple(n for n in WEIGHTS if n not in BIG + CONV + ("norm_final",))

    first, rest = ("w_in",) + CONV, BIG[1:]
    got = _all_gather([shard(n, 0) for n in first], name="gather_l0_first")
    whole, sliced = (True, None), (False, None)
    names_a, names_b = rest, BIG + CONV
    srcs_a, srcs_b = [shard(n, 0) for n in names_a], [shard(n, 1) for n in names_b]
    sem_sa, sem_ra, srcs_a, lands_a, w_in0 = _split_start(
        srcs_a, [empty_land(a) for a in srcs_a], [whole] * len(srcs_a), got[0], name="gather_l0_rest_start")
    sem_sb, sem_rb, srcs_b, lands_b, w_in0 = _split_start(
        srcs_b, [empty_land(a) for a in srcs_b], [whole] * len(srcs_b), w_in0, name="gather_l1_start")
    p0 = {n: w[n][0] for n in small_names}
    p0.update({n: assemble(n, g, 0) for n, g in zip(first, [w_in0] + list(got[1:]))})

    def late_l0(after):
        lands = _split_wait(sem_sa, sem_ra, srcs_a, lands_a, [whole] * len(srcs_a), after, name="gather_l0_rest_wait")
        return {n: assemble(n, with_own(ld, s_), 0) for n, ld, s_ in zip(names_a, lands, srcs_a)}

    h, sv0 = _layer_fwd(x, p0, lay, "l0", late=late_l0)
    lands = _split_wait(sem_sb, sem_rb, srcs_b, lands_b, [whole] * len(srcs_b), h, name="gather_l1_wait")
    p1 = {n: w[n][1] for n in small_names}
    p1.update({n: assemble(n, with_own(ld, s_), 1) for n, ld, s_ in zip(names_b, lands, srcs_b)})
    h, sv1 = _layer_fwd(h, p1, lay, "l1")
    loss, dh, g_norm_final = _final_loss(h, w["norm_final"], target, name="final_loss")
    grads = [None] * DEPTH
    dh, grads[1] = _layer_bwd(dh, p1, sv1, lay, "l1")

    def exchange_start(names, g, carry, tag):
        srcs = [g[n] for n in names]
        return _split_start(srcs, [lax.empty(a.shape, a.dtype) for a in srcs], [sliced] * len(srcs), carry,
                            name=f"grad_{tag}_start")

    def exchange_wait(names, started, after, tag):
        sem_s, sem_r, srcs, lands_, _ = started
        lands_ = _split_wait(sem_s, sem_r, srcs, lands_, [sliced] * len(srcs), after, name=f"grad_{tag}_wait")
        return {n: with_own(ld, lax.dynamic_index_in_dim(s_, me, 0, keepdims=False)) for n, ld, s_ in zip(names, lands_, srcs)}

    x1_started = exchange_start(BIG, grads[1], dh, "l1")
    pending = {}

    def early_l0(g, carry):
        pending["rest"] = exchange_start(rest, g, carry, "l0_rest")
        return pending["rest"][4]

    def late_bwd_l0(g, carry):
        pending["w_in"] = exchange_start(("w_in",), g, carry, "l0_w_in")
        return pending["w_in"][4]

    grad_x, grads[0] = _layer_bwd(x1_started[4], p0, sv0, lay, "l0", early=early_l0, late=late_bwd_l0)

    out = {"grad": {}, "delta": {}, "new_m": {}, "new_v": {}}
    parts1 = exchange_wait(BIG, x1_started, grad_x, "l1")
    res1 = {n: _sum_adamw(parts1[n], w[n], m[n], v[n], 1, None, name=f"sum_adamw_{n}_l1") for n in BIG}
    parts0 = exchange_wait(rest, pending["rest"], res1["w_in"][0], "l0_rest")
    res0 = {n: _sum_adamw(parts0[n], w[n], m[n], v[n], 0, res1[n], name=f"sum_adamw_{n}_l0") for n in rest}
    parts0 = exchange_wait(("w_in",), pending["w_in"], res0["w_down"][0], "l0_w_in")
    res0["w_in"] = _sum_adamw(parts0["w_in"], w["w_in"], m["w_in"], v["w_in"], 0, res1["w_in"], name="sum_adamw_w_in_l0")
    for n in BIG:
        for key, a in zip(("grad", "delta", "new_m", "new_v"), res0[n]):
            out[key][n] = a

    gfull = {n: jnp.stack([grads[l][n] for l in range(DEPTH)]) for n in SMALL if n != "norm_final"}
    gfull["norm_final"] = g_norm_final
    small_send = _pack([gfull[n] for n in SMALL] + [loss.reshape(1)], F32)
    small_recv = _all_gather([small_send], name="gather_small_grads", after=res0["w_in"][0])[0]
    small_sum = _sum_parts(small_recv, name="sum_small")
    small_full = _unpack(small_sum, [gfull[n].shape for n in SMALL] + [(1,)])
    loss_total = small_full[-1][0]
    gsmall = {}
    for n, a in zip(SMALL, small_full[:-1]):
        if n in SHARD_AXIS:
            a = lax.dynamic_index_in_dim(_to_shards(a, SHARD_AXIS[n]), me, axis=0, keepdims=False)
        gsmall[n] = a
    small_shapes = [w[n].shape for n in SMALL]
    ws, gs, ms, vs = (_pack([t[n] for n in SMALL], F32) for t in (w, gsmall, m, v))
    ds, m1s, v1s = _adamw(ws, gs, ms, vs, name="adamw_small")
    for n in SMALL:
        out["grad"][n] = gsmall[n]
    for key, packed in (("delta", ds), ("new_m", m1s), ("new_v", v1s)):
        for n, a in zip(SMALL, _unpack(packed, small_shapes)):
            out[key][n] = a
    return loss_total, grad_x, out


def kernel(x, norm_mix, w_in, dn_conv_w, dn_a_log, dn_dt_bias, dn_norm_w, ssm_conv_w, ssm_conv_b, ssm_a_log, ssm_dt_bias, ssm_d, ssm_norm_w, w_branch, w_out, norm_mlp, w_up, w_down, norm_final, loss_target, m_norm_mix, m_w_in, m_dn_conv_w, m_dn_a_log, m_dn_dt_bias, m_dn_norm_w, m_ssm_conv_w, m_ssm_conv_b, m_ssm_a_log, m_ssm_dt_bias, m_ssm_d, m_ssm_norm_w, m_w_branch, m_w_out, m_norm_mlp, m_w_up, m_w_down, m_norm_final, v_norm_mix, v_w_in, v_dn_conv_w, v_dn_a_log, v_dn_dt_bias, v_dn_norm_w, v_ssm_conv_w, v_ssm_conv_b, v_ssm_a_log, v_ssm_dt_bias, v_ssm_d, v_ssm_norm_w, v_w_branch, v_w_out, v_norm_mlp, v_w_up, v_w_down, v_norm_final):
    w = dict(norm_mix=norm_mix, w_in=w_in, dn_conv_w=dn_conv_w, dn_a_log=dn_a_log, dn_dt_bias=dn_dt_bias, dn_norm_w=dn_norm_w,
             ssm_conv_w=ssm_conv_w, ssm_conv_b=ssm_conv_b, ssm_a_log=ssm_a_log, ssm_dt_bias=ssm_dt_bias, ssm_d=ssm_d,
             ssm_norm_w=ssm_norm_w, w_branch=w_branch, w_out=w_out, norm_mlp=norm_mlp, w_up=w_up, w_down=w_down,
             norm_final=norm_final)
    m = dict(norm_mix=m_norm_mix, w_in=m_w_in, dn_conv_w=m_dn_conv_w, dn_a_log=m_dn_a_log, dn_dt_bias=m_dn_dt_bias,
             dn_norm_w=m_dn_norm_w, ssm_conv_w=m_ssm_conv_w, ssm_conv_b=m_ssm_conv_b, ssm_a_log=m_ssm_a_log,
             ssm_dt_bias=m_ssm_dt_bias, ssm_d=m_ssm_d, ssm_norm_w=m_ssm_norm_w, w_branch=m_w_branch, w_out=m_w_out,
             norm_mlp=m_norm_mlp, w_up=m_w_up, w_down=m_w_down, norm_final=m_norm_final)
    v = dict(norm_mix=v_norm_mix, w_in=v_w_in, dn_conv_w=v_dn_conv_w, dn_a_log=v_dn_a_log, dn_dt_bias=v_dn_dt_bias,
             dn_norm_w=v_dn_norm_w, ssm_conv_w=v_ssm_conv_w, ssm_conv_b=v_ssm_conv_b, ssm_a_log=v_ssm_a_log,
             ssm_dt_bias=v_ssm_dt_bias, ssm_d=v_ssm_d, ssm_norm_w=v_ssm_norm_w, w_branch=v_w_branch, w_out=v_w_out,
             norm_mlp=v_norm_mlp, w_up=v_w_up, w_down=v_w_down, norm_final=v_norm_final)
    loss, grad_x, out = _step(w, m, v, x[0], loss_target[0])
    return (loss, grad_x[None], *[out["grad"][n] for n in WEIGHTS], *[out["delta"][n] for n in WEIGHTS],
            *[out["new_m"][n] for n in WEIGHTS], *[out["new_v"][n] for n in WEIGHTS])
```

```python
import functools
import math

import jax
import jax.numpy as jnp
from jax import lax
from jax.experimental import pallas as pl
from jax.experimental.pallas import tpu as pltpu

F32 = jnp.float32
BF16 = jnp.bfloat16
MXU_DTYPE = BF16
HIGHEST = lax.Precision.HIGHEST

N_DEV = 8
DEPTH = 2
EPS = 1e-6
CONV_K = 4
DN_HEAD_DIM = 128
SB_HEAD_DIM = 64
SSM_HEAD_DIM = 64
SSM_STATE = 128
SSM_GROUPS = 4
CHUNK = 64
SB_BLOCK = 128
LANES = 128
ADAM_LR, ADAM_B1, ADAM_B2, ADAM_EPS, ADAM_WD, ADAM_STEP = 0.001, 0.9, 0.999, 1e-08, 0.01, 10
NEG_BIG = -1e30
DN_HEADS_PER_STEP = 8
SSD_GROUPS_PER_STEP = 1
SB_UNROLL = 4
SB_SPLIT = 2
CHUNK_PREC = lax.Precision.HIGH

ARB = "arbitrary"


def _cparams(n_axes):
    return pltpu.CompilerParams(dimension_semantics=(ARB,) * n_axes)


def _softplus(x):
    return jnp.maximum(x, 0.0) + jnp.log1p(jnp.exp(-jnp.abs(x)))


def _sigmoid(x):
    return jax.nn.sigmoid(x)


def _silu(x):
    return x * _sigmoid(x)


def _silu_grad(x):
    s = _sigmoid(x)
    return s * (1.0 + x * (1.0 - s))


def _dot(a, b, dims, prec=None):
    return lax.dot_general(a, b, (dims, ((), ())), precision=prec, preferred_element_type=F32)


NN = ((1,), (0,))
NT = ((1,), (1,))
TN = ((0,), (0,))


def _hdot(a, b, dims=NN):
    return _dot(a, b, dims, CHUNK_PREC)


def _bdot(a, b, dims=NN):
    return _dot(a.astype(MXU_DTYPE), b.astype(MXU_DTYPE), dims)


def _split_dot(a, m_bf16, nsplit=3):
    out = None
    rem = a
    for _ in range(nsplit):
        piece = rem.astype(BF16)
        rem = rem - piece.astype(F32)
        term = _dot(piece, m_bf16, NN)
        out = term if out is None else out + term
    return out


def _pick(n, pref):
    for t in pref:
        if n % t == 0:
            return t
    return n


def _matmul(a, b, *, ta=False, tb=False, name, epilogue=None, extras=(), out_dtypes=(F32,), col_shards=1,
            tm=None, tn=None, tk=None):
    m, k = (a.shape[1], a.shape[0]) if ta else a.shape
    k2, n = (b.shape[1], b.shape[0]) if tb else b.shape
    assert k == k2, (a.shape, b.shape, ta, tb)
    ncs = n // col_shards
    tm = tm or _pick(m, (1920, 1024, 512, 256, 128))
    tn = tn or _pick(ncs, (1920, 1024, 640, 512, 384, 256, 128))
    tk = tk or _pick(k, (1920, 1024, 640, 512, 256, 128))
    nk = k // tk
    a_spec = pl.BlockSpec((tk, tm), lambda i, j, kk: (kk, i)) if ta else pl.BlockSpec((tm, tk), lambda i, j, kk: (i, kk))
    b_spec = pl.BlockSpec((tn, tk), lambda i, j, kk: (j, kk)) if tb else pl.BlockSpec((tk, tn), lambda i, j, kk: (kk, j))
    e_spec = pl.BlockSpec((tm, tn), lambda i, j, kk: (i, j))
    if col_shards == 1:
        o_spec, o_shape = e_spec, (m, n)
    else:
        per = ncs // tn
        o_spec, o_shape = pl.BlockSpec((None, tm, tn), lambda i, j, kk: (j // per, i, j % per)), (col_shards, m, ncs)
    dims = (((0,) if ta else (1,)), ((1,) if tb else (0,)))
    n_extra = len(extras)
    n_out = len(out_dtypes)

    def body(*refs):
        a_ref, b_ref = refs[0], refs[1]
        extra_refs = refs[2:2 + n_extra]
        out_refs = refs[2 + n_extra:2 + n_extra + n_out]
        acc_ref = refs[-1]
        kk = pl.program_id(2)

        @pl.when(kk == 0)
        def _():
            acc_ref[...] = jnp.zeros_like(acc_ref)

        acc_ref[...] += _dot(a_ref[...].astype(MXU_DTYPE), b_ref[...].astype(MXU_DTYPE), dims)

        @pl.when(kk == nk - 1)
        def _():
            acc = acc_ref[...]
            outs = (acc,) if epilogue is None else epilogue(acc, *[r[...] for r in extra_refs])
            for o_ref, o in zip(out_refs, outs):
                o_ref[...] = o.astype(o_ref.dtype)

    outs = pl.pallas_call(
        body,
        grid=(m // tm, n // tn, nk),
        in_specs=[a_spec, b_spec] + [e_spec] * n_extra,
        out_specs=[o_spec] * n_out,
        out_shape=[jax.ShapeDtypeStruct(o_shape, dt) for dt in out_dtypes],
        scratch_shapes=[pltpu.VMEM((tm, tn), F32)],
        compiler_params=pltpu.CompilerParams(dimension_semantics=("parallel", "parallel", ARB)),
        name=name,
    )(a, b, *extras)
    return outs[0] if n_out == 1 else tuple(outs)


def _rms_fwd(x, w, *, name, tm=256):
    s, d = x.shape
    out_dtype = MXU_DTYPE

    def body(x_ref, w_ref, o_ref):
        xv = x_ref[...]
        r = lax.rsqrt(jnp.mean(xv * xv, axis=-1, keepdims=True) + EPS)
        o_ref[...] = (xv * r * w_ref[...]).astype(o_ref.dtype)

    return pl.pallas_call(
        body, grid=(s // tm,),
        in_specs=[pl.BlockSpec((tm, d), lambda i: (i, 0)), pl.BlockSpec((1, d), lambda i: (0, 0))],
        out_specs=pl.BlockSpec((tm, d), lambda i: (i, 0)),
        out_shape=jax.ShapeDtypeStruct((s, d), out_dtype),
        compiler_params=_cparams(1), name=name,
    )(x, w.reshape(1, d))


def _rms_bwd(x, w, dh, dres, *, name, tm=256):
    s, d = x.shape

    def body(x_ref, w_ref, dh_ref, dres_ref, dx_ref, dw_ref):
        xv = x_ref[...]
        r = lax.rsqrt(jnp.mean(xv * xv, axis=-1, keepdims=True) + EPS)
        xh = xv * r
        dhv = dh_ref[...].astype(F32)
        dxn = dhv * w_ref[...]
        dx = r * (dxn - xh * jnp.mean(dxn * xh, axis=-1, keepdims=True))
        dx_ref[...] = dres_ref[...] + dx

        @pl.when(pl.program_id(0) == 0)
        def _():
            dw_ref[...] = jnp.zeros_like(dw_ref)

        dw_ref[...] += jnp.sum(dhv * xh, axis=0, keepdims=True)

    dx, dw = pl.pallas_call(
        body, grid=(s // tm,),
        in_specs=[pl.BlockSpec((tm, d), lambda i: (i, 0)), pl.BlockSpec((1, d), lambda i: (0, 0)),
                  pl.BlockSpec((tm, d), lambda i: (i, 0)), pl.BlockSpec((tm, d), lambda i: (i, 0))],
        out_specs=[pl.BlockSpec((tm, d), lambda i: (i, 0)), pl.BlockSpec((1, d), lambda i: (0, 0))],
        out_shape=[jax.ShapeDtypeStruct((s, d), F32), jax.ShapeDtypeStruct((1, d), F32)],
        compiler_params=_cparams(1), name=name,
    )(x, w.reshape(1, d), dh, dres)
    return dx, dw.reshape(d)


def _final_loss(x, w, target, *, name, tm=256):
    s, d = x.shape

    def body(x_ref, w_ref, t_ref, loss_ref, dx_ref, dw_ref):
        xv = x_ref[...]
        r = lax.rsqrt(jnp.mean(xv * xv, axis=-1, keepdims=True) + EPS)
        xh = xv * r
        err = xh * w_ref[...] - t_ref[...]
        dy = err * (1.0 / d)
        dxn = dy * w_ref[...]
        dx_ref[...] = r * (dxn - xh * jnp.mean(dxn * xh, axis=-1, keepdims=True))

        @pl.when(pl.program_id(0) == 0)
        def _():
            dw_ref[...] = jnp.zeros_like(dw_ref)
            loss_ref[...] = jnp.zeros_like(loss_ref)

        dw_ref[...] += jnp.sum(dy * xh, axis=0, keepdims=True)
        row = jnp.sum(err * err, axis=1, keepdims=True) * (0.5 / d)
        loss_ref[...] += jnp.sum(row, axis=0, keepdims=True)

    loss, dx, dw = pl.pallas_call(
        body, grid=(s // tm,),
        in_specs=[pl.BlockSpec((tm, d), lambda i: (i, 0)), pl.BlockSpec((1, d), lambda i: (0, 0)),
                  pl.BlockSpec((tm, d), lambda i: (i, 0))],
        out_specs=[pl.BlockSpec((1, 1), lambda i: (0, 0)), pl.BlockSpec((tm, d), lambda i: (i, 0)),
                   pl.BlockSpec((1, d), lambda i: (0, 0))],
        out_shape=[jax.ShapeDtypeStruct((1, 1), F32), jax.ShapeDtypeStruct((s, d), F32), jax.ShapeDtypeStruct((1, d), F32)],
        compiler_params=_cparams(1), name=name,
    )(x, w.reshape(1, d), target)
    return loss[0, 0], dx, dw.reshape(d)


def _shift_down(x, sh, t_idx):
    return jnp.where(t_idx >= sh, pltpu.roll(x, sh, 0), 0.0)


def _shift_up(x, sh, t_idx, s):
    return jnp.where(t_idx < s - sh, pltpu.roll(x, s - sh, 0), 0.0)


def _conv_pre(x, w_rows, b, t_idx):
    c = w_rows[CONV_K - 1] * x + b
    for sh in range(1, CONV_K):
        c = c + w_rows[CONV_K - 1 - sh] * _shift_down(x, sh, t_idx)
    return c


def _conv_fwd(src, col0, w, b, n_l2, *, name):
    s = src.shape[0]
    c_tot = w.shape[1]
    nblk = c_tot // LANES

    def body(x_ref, w_ref, b_ref, o_ref):
        j = pl.program_id(0)
        t_idx = lax.broadcasted_iota(jnp.int32, (s, LANES), 0)
        w_rows = [w_ref[kk:kk + 1, :] for kk in range(CONV_K)]
        y = _silu(_conv_pre(x_ref[...], w_rows, b_ref[...], t_idx))
        if n_l2 > 0:
            yn = y * lax.rsqrt(jnp.sum(y * y, axis=1, keepdims=True) + EPS)
            y = jnp.where(j < n_l2, yn, y)
        o_ref[...] = y

    return pl.pallas_call(
        body, grid=(nblk,),
        in_specs=[pl.BlockSpec((s, LANES), lambda j: (0, col0 + j)), pl.BlockSpec((CONV_K, LANES), lambda j: (0, j)),
                  pl.BlockSpec((1, LANES), lambda j: (0, j))],
        out_specs=pl.BlockSpec((s, LANES), lambda j: (0, j)),
        out_shape=jax.ShapeDtypeStruct((s, c_tot), F32),
        compiler_params=_cparams(1), name=name,
    )(src, w, b)


def _conv_bwd(src, col0, w, b, n_l2, dout, into, *, name):
    s = src.shape[0]
    c_tot = w.shape[1]
    nblk = c_tot // LANES

    def body(x_ref, w_ref, b_ref, do_ref, into_ref, dx_ref, dw_ref, db_ref):
        j = pl.program_id(0)
        t_idx = lax.broadcasted_iota(jnp.int32, (s, LANES), 0)
        xv = x_ref[...]
        w_rows = [w_ref[kk:kk + 1, :] for kk in range(CONV_K)]
        c = _conv_pre(xv, w_rows, b_ref[...], t_idx)
        dy = do_ref[...]
        if n_l2 > 0:
            y = _silu(c)
            r = lax.rsqrt(jnp.sum(y * y, axis=1, keepdims=True) + EPS)
            dyn = r * dy - y * (r * r * r) * jnp.sum(dy * y, axis=1, keepdims=True)
            dy = jnp.where(j < n_l2, dyn, dy)
        dc = dy * _silu_grad(c)
        dx = w_rows[CONV_K - 1] * dc
        rows = [None] * CONV_K
        rows[CONV_K - 1] = jnp.sum(dc * xv, axis=0, keepdims=True)
        for sh in range(1, CONV_K):
            dx = dx + w_rows[CONV_K - 1 - sh] * _shift_up(dc, sh, t_idx, s)
            rows[CONV_K - 1 - sh] = jnp.sum(dc * _shift_down(xv, sh, t_idx), axis=0, keepdims=True)
        dx_ref[...] = dx.astype(dx_ref.dtype)
        for kk in range(CONV_K):
            dw_ref[kk:kk + 1, :] = rows[kk]
        db_ref[...] = jnp.sum(dc, axis=0, keepdims=True)

    return pl.pallas_call(
        body, grid=(nblk,),
        in_specs=[pl.BlockSpec((s, LANES), lambda j: (0, col0 + j)), pl.BlockSpec((CONV_K, LANES), lambda j: (0, j)),
                  pl.BlockSpec((1, LANES), lambda j: (0, j)), pl.BlockSpec((s, LANES), lambda j: (0, j)), ANY],
        out_specs=[pl.BlockSpec((s, LANES), lambda j: (0, col0 + j)), pl.BlockSpec((CONV_K, LANES), lambda j: (0, j)),
                   pl.BlockSpec((1, LANES), lambda j: (0, j))],
        out_shape=[jax.ShapeDtypeStruct(into.shape, into.dtype), jax.ShapeDtypeStruct((CONV_K, c_tot), F32),
                   jax.ShapeDtypeStruct((1, c_tot), F32)],
        input_output_aliases={4: 0},
        compiler_params=_cparams(1), name=name,
    )(src, w, b, dout, into)


def _chunk_masks(c):
    ii = lax.broadcasted_iota(jnp.int32, (c, c), 0)
    jj = lax.broadcasted_iota(jnp.int32, (c, c), 1)
    return ii, jj


def _row_to_col(row, eye):
    return jnp.sum(jnp.where(eye, row, 0.0), axis=1, keepdims=True)


def _each(f, *lists):
    return [f(*xs) for xs in zip(*lists)]


@jax.custom_vjp
def _nilpotent_inverse(nmats):
    c = nmats[0].shape[0]
    ii, jj = _chunk_masks(c)
    xinv = _each(lambda n: jnp.where(ii == jj, 1.0, 0.0) + n, nmats)
    pw = nmats
    for _ in range(int(math.log2(c)) - 1):
        pw = _each(lambda p: _dot(p, p, NN, HIGHEST), pw)
        xinv = _each(lambda x, p: x + _dot(x, p, NN, HIGHEST), xinv, pw)
    return xinv


def _nilpotent_inverse_fwd(nmats):
    xinv = _nilpotent_inverse(nmats)
    return xinv, xinv


def _nilpotent_inverse_bwd(xinv, cts):
    left = _each(lambda x, ct: _dot(x, ct, TN, HIGHEST), xinv, cts)
    return (_each(lambda l_, x: _dot(l_, x, NT, HIGHEST), left, xinv),)


_nilpotent_inverse.defvjp(_nilpotent_inverse_fwd, _nilpotent_inverse_bwd)


@jax.custom_vjp
def _saved_inverse(nmats, saved):
    return saved


def _saved_inverse_fwd(nmats, saved):
    return saved, saved


def _saved_inverse_bwd(xinv, cts):
    return _nilpotent_inverse_bwd(xinv, cts) + (_each(jnp.zeros_like, xinv),)


_saved_inverse.defvjp(_saved_inverse_fwd, _saved_inverse_bwd)


def _dn_chunk(q, k, v, a_row, b_row, alog, dtb, s0, saved_inverse=None):
    c = q[0].shape[0]
    ii, jj = _chunk_masks(c)
    causal, strict, eye = ii >= jj, ii > jj, ii == jj
    g_row = _each(lambda al, a, dt: -jnp.exp(al) * _softplus(a + dt), alog, a_row, dtb)
    beta_col = _each(lambda b: _row_to_col(_sigmoid(b), eye), b_row)
    g_col = _each(lambda g: _row_to_col(g, eye), g_row)
    gc_col = _each(lambda g: jnp.sum(jnp.where(causal, g, 0.0), axis=1, keepdims=True), g_row)
    gc_row = _each(lambda g: jnp.sum(jnp.where(jj >= ii, g, 0.0), axis=0, keepdims=True), g_col)
    decay = _each(lambda gc, gr: jnp.exp(jnp.where(causal, gc - gr, NEG_BIG)), gc_col, gc_row)
    kb = _each(jnp.multiply, k, beta_col)
    vb = _each(jnp.multiply, v, beta_col)
    nmat = _each(lambda kb_, k_, dc: -jnp.where(strict, _dot(kb_, k_, NT, HIGHEST) * dc, 0.0), kb, k, decay)
    xinv = _nilpotent_inverse(nmat) if saved_inverse is None else _saved_inverse(nmat, saved_inverse)
    egc = _each(jnp.exp, gc_col)
    u = _each(lambda x, vb_: _dot(x, vb_, NN, HIGHEST), xinv, vb)
    w = _each(lambda x, kb_, e: _dot(x, kb_ * e, NN, HIGHEST), xinv, kb, egc)
    qs = _each(lambda q_: q_ * (q_.shape[1] ** -0.5), q)
    attn = _each(lambda q_, k_, dc: _hdot(q_, k_, NT) * dc, qs, k, decay)
    gl = _each(lambda g: jnp.sum(g, axis=1, keepdims=True), g_row)
    kd = _each(lambda k_, gl_, gc: k_ * jnp.exp(gl_ - gc), k, gl, gc_col)
    v_new = _each(lambda u_, w_, s: u_ - _hdot(w_, s), u, w, s0)
    o = _each(lambda q_, e, s, at, vn: _hdot(q_ * e, s) + _hdot(at, vn), qs, egc, s0, attn, v_new)
    s1 = _each(lambda s, gl_, kd_, vn: s * jnp.exp(gl_) + _hdot(kd_, vn, TN), s0, gl, kd, v_new)
    return (o, s1), xinv


def _dn_specs(nh, nc, hb, rev):
    n_of = (lambda n: nc - 1 - n) if rev else (lambda n: n)
    ng = nh // hb
    qkv = [pl.BlockSpec((CHUNK, hb * DN_HEAD_DIM), (lambda h, n, o=o: (n_of(n), o * ng + h))) for o in range(3)]
    row = pl.BlockSpec((hb, None, 1, CHUNK), lambda h, n: (h, n_of(n), 0, 0))
    scal = pl.BlockSpec((hb, 1, 1), lambda h, n: (h, 0, 0))
    o_spec = pl.BlockSpec((CHUNK, hb * DN_HEAD_DIM), lambda h, n: (n_of(n), h))
    st = pl.BlockSpec((hb, None, DN_HEAD_DIM, DN_HEAD_DIM), lambda h, n: (h, n_of(n), 0, 0))
    inv = pl.BlockSpec((hb, None, CHUNK, CHUNK), lambda h, n: (h, n_of(n), 0, 0))
    return qkv, row, scal, o_spec, st, inv


def _dn_fwd(qkv, a_rows, b_rows, alog, dtb, *, name):
    s = qkv.shape[0]
    nh, nc = a_rows.shape[0], a_rows.shape[1]
    hb = min(DN_HEADS_PER_STEP, nh)
    qkv_specs, row, scal, o_spec, st, inv = _dn_specs(nh, nc, hb, False)
    hd = DN_HEAD_DIM

    def body(q_ref, k_ref, v_ref, a_ref, b_ref, al_ref, dt_ref, o_ref, st_ref, inv_ref, state):
        @pl.when(pl.program_id(1) == 0)
        def _():
            state[...] = jnp.zeros_like(state)

        cols = [slice(h * hd, (h + 1) * hd) for h in range(hb)]
        s0 = [state[h] for h in range(hb)]
        for h in range(hb):
            st_ref[h] = s0[h]
        (o, s1), xinv = _dn_chunk(
            [q_ref[:, cl] for cl in cols], [k_ref[:, cl] for cl in cols], [v_ref[:, cl] for cl in cols],
            [a_ref[h] for h in range(hb)], [b_ref[h] for h in range(hb)],
            [al_ref[h] for h in range(hb)], [dt_ref[h] for h in range(hb)], s0)
        for h in range(hb):
            o_ref[:, cols[h]] = o[h]
            inv_ref[h] = xinv[h]
            state[h] = s1[h]

    return pl.pallas_call(
        body, grid=(nh // hb, nc),
        in_specs=qkv_specs + [row, row, scal, scal],
        out_specs=[o_spec, st, inv],
        out_shape=[jax.ShapeDtypeStruct((s, nh * hd), F32), jax.ShapeDtypeStruct((nh, nc, hd, hd), F32),
                   jax.ShapeDtypeStruct((nh, nc, CHUNK, CHUNK), F32)],
        scratch_shapes=[pltpu.VMEM((hb, hd, hd), F32)],
        compiler_params=_cparams(2), name=name,
    )(qkv, qkv, qkv, a_rows, b_rows, alog, dtb)


def _dn_bwd(qkv, a_rows, b_rows, alog, dtb, states, inverses, do, *, name):
    s = qkv.shape[0]
    nh, nc = a_rows.shape[0], a_rows.shape[1]
    hb = min(DN_HEADS_PER_STEP, nh)
    qkv_specs, row, scal, o_spec, st, inv = _dn_specs(nh, nc, hb, True)
    hd = DN_HEAD_DIM

    assert hb == nh, "dq | dk | dv are written as one [S, 3W] array: all heads in one grid step"
    w = nh * hd

    def body(q_ref, k_ref, v_ref, a_ref, b_ref, al_ref, dt_ref, st_ref, inv_ref, do_ref,
             dqkv_ref, da_ref, db_ref, dal_ref, ddt_ref, dstate):
        @pl.when(pl.program_id(1) == 0)
        def _():
            dstate[...] = jnp.zeros_like(dstate)
            dal_ref[...] = jnp.zeros_like(dal_ref)
            ddt_ref[...] = jnp.zeros_like(ddt_ref)

        cols = [slice(h * hd, (h + 1) * hd) for h in range(hb)]
        heads = range(hb)
        args = ([q_ref[:, cl] for cl in cols], [k_ref[:, cl] for cl in cols], [v_ref[:, cl] for cl in cols],
                [a_ref[h] for h in heads], [b_ref[h] for h in heads], [al_ref[h] for h in heads],
                [dt_ref[h] for h in heads], [st_ref[h] for h in heads])
        saved = [inv_ref[h] for h in heads]
        _, vjp, _ = jax.vjp(lambda *a: _dn_chunk(*a, saved_inverse=saved), *args, has_aux=True)
        dq, dk, dv, da, db, dal, ddt, ds0 = vjp(([do_ref[:, cl] for cl in cols], [dstate[h] for h in heads]))
        for h in heads:
            dqkv_ref[:, h * hd:(h + 1) * hd] = dq[h]
            dqkv_ref[:, w + h * hd:w + (h + 1) * hd] = dk[h]
            dqkv_ref[:, 2 * w + h * hd:2 * w + (h + 1) * hd] = dv[h]
            da_ref[h] = da[h]
            db_ref[h] = db[h]
            dal_ref[h] += dal[h]
            ddt_ref[h] += ddt[h]
            dstate[h] = ds0[h]

    n_of = lambda n: nc - 1 - n
    outs = pl.pallas_call(
        body, grid=(nh // hb, nc),
        in_specs=qkv_specs + [row, row, scal, scal, st, inv, o_spec],
        out_specs=[pl.BlockSpec((CHUNK, 3 * w), lambda h, n: (n_of(n), 0)), row, row, scal, scal],
        out_shape=[jax.ShapeDtypeStruct((s, 3 * w), F32)]
        + [jax.ShapeDtypeStruct(a_rows.shape, F32)] * 2 + [jax.ShapeDtypeStruct((nh, 1, 1), F32)] * 2,
        scratch_shapes=[pltpu.VMEM((hb, hd, hd), F32)],
        compiler_params=_cparams(2), name=name,
    )(qkv, qkv, qkv, a_rows, b_rows, alog, dtb, states, inverses, do)
    return outs


def _dn_post_fwd(o, src, gate_col0, nw, *, name, tm=256):
    s, w = o.shape
    hd = DN_HEAD_DIM
    gc = gate_col0 * LANES // w

    def body(o_ref, g_ref, w_ref, y_ref):
        for h in range(w // hd):
            cols = slice(h * hd, (h + 1) * hd)
            ov = o_ref[:, cols]
            r = lax.rsqrt(jnp.mean(ov * ov, axis=-1, keepdims=True) + EPS)
            y_ref[:, cols] = (ov * r * w_ref[...] * _silu(g_ref[:, cols])).astype(y_ref.dtype)

    blk = pl.BlockSpec((tm, w), lambda i: (i, 0))
    return pl.pallas_call(
        body, grid=(s // tm,),
        in_specs=[blk, pl.BlockSpec((tm, w), lambda i: (i, gc)), pl.BlockSpec((1, hd), lambda i: (0, 0))],
        out_specs=blk, out_shape=jax.ShapeDtypeStruct((s, w), MXU_DTYPE),
        compiler_params=_cparams(1), name=name,
    )(o, src, nw.reshape(1, hd))


def _dn_post_bwd(o, src, gate_col0, nw, dy, into, *, name, tm=256):
    s, w = o.shape
    hd = DN_HEAD_DIM
    gc = gate_col0 * LANES // w

    def body(o_ref, g_ref, w_ref, dy_ref, into_ref, do_ref, dg_ref, dw_ref):
        @pl.when(pl.program_id(0) == 0)
        def _():
            dw_ref[...] = jnp.zeros_like(dw_ref)

        dw = jnp.zeros((1, hd), F32)
        for h in range(w // hd):
            cols = slice(h * hd, (h + 1) * hd)
            ov, gv, dyv = o_ref[:, cols], g_ref[:, cols], dy_ref[:, cols]
            r = lax.rsqrt(jnp.mean(ov * ov, axis=-1, keepdims=True) + EPS)
            oh = ov * r
            dn = dyv * _silu(gv)
            dg_ref[:, cols] = (dyv * (oh * w_ref[...]) * _silu_grad(gv)).astype(dg_ref.dtype)
            don = dn * w_ref[...]
            do_ref[:, cols] = r * (don - oh * jnp.mean(don * oh, axis=-1, keepdims=True))
            dw = dw + jnp.sum(dn * oh, axis=0, keepdims=True)
        dw_ref[...] += dw

    blk = pl.BlockSpec((tm, w), lambda i: (i, 0))
    wspec = pl.BlockSpec((1, hd), lambda i: (0, 0))
    gate_blk = pl.BlockSpec((tm, w), lambda i: (i, gc))
    do, dg, dw = pl.pallas_call(
        body, grid=(s // tm,),
        in_specs=[blk, gate_blk, wspec, blk, ANY],
        out_specs=[blk, gate_blk, wspec],
        out_shape=[jax.ShapeDtypeStruct((s, w), F32), jax.ShapeDtypeStruct(into.shape, into.dtype),
                   jax.ShapeDtypeStruct((1, hd), F32)],
        input_output_aliases={4: 1},
        compiler_params=_cparams(1), name=name,
    )(o, src, nw.reshape(1, hd), dy, into)
    return do, dg, dw.reshape(hd)


def _sb_consts():
    r2 = lax.broadcasted_iota(jnp.int32, (2 * SB_BLOCK, SB_BLOCK), 0)
    c2 = lax.broadcasted_iota(jnp.int32, (2 * SB_BLOCK, SB_BLOCK), 1)
    r = lax.broadcasted_iota(jnp.int32, (SB_BLOCK, SB_BLOCK), 0)
    c = lax.broadcasted_iota(jnp.int32, (SB_BLOCK, SB_BLOCK), 1)
    lm0 = c < SB_HEAD_DIM
    m_gt = jnp.where(r > c, 1.0, 0.0).astype(BF16)
    m_lt = jnp.where(r < c, 1.0, 0.0).astype(BF16)
    return r2, c2, lm0, m_gt, m_lt


def _sb_stack(x, lm0):
    return jnp.concatenate([jnp.where(lm0, x, 0.0), jnp.where(lm0, 0.0, x)], axis=0)


def _sb_unstack(x2, lm0):
    return jnp.where(lm0, x2[:SB_BLOCK], x2[SB_BLOCK:])


def _sb_fwd(src, col0, width, *, name):
    s = src.shape[0]
    nq = s // SB_BLOCK
    npair = width // LANES
    scale = SB_HEAD_DIM ** -0.5
    nu = math.gcd(SB_UNROLL, nq)

    def body(q_ref, k_ref, v_ref, o_ref, r_ref):
        i = pl.program_id(1)
        r2, c2, lm0, m_gt, _ = _sb_consts()
        t_glob = i * SB_BLOCK + (r2 & (SB_BLOCK - 1))
        q2 = (_sb_stack(q_ref[...], lm0) * scale).astype(MXU_DTYPE)

        def group(base, carry, masked):
            o2, rsum = carry
            js = [base + nu - 1 - u for u in range(nu)]
            offs = [pl.multiple_of(j * SB_BLOCK, SB_BLOCK) for j in js]
            zs = [_dot(q2, k_ref[pl.ds(off, SB_BLOCK), :].astype(MXU_DTYPE), NT) for off in offs]
            ts = [jnp.log(1.0 + jnp.exp(-jnp.abs(z))) for z in zs]
            lks = [-(jnp.maximum(z, 0.0) + t) for z, t in zip(zs, ts)]
            if masked:
                masks = [(j * SB_BLOCK + c2) < t_glob for j in js]
                lks = [jnp.where(mk, lk, 0.0) for mk, lk in zip(masks, lks)]
            sufs = [_split_dot(lk, m_gt, SB_SPLIT) for lk in lks]
            rs = [rsum]
            for lk in lks:
                rs.append(rs[-1] + jnp.sum(lk, axis=1, keepdims=True))
            wgts = [jnp.exp((jnp.minimum(z, 0.0) - t) + r_ + sf) for z, t, r_, sf in zip(zs, ts, rs, sufs)]
            if masked:
                wgts = [jnp.where(mk, wg, 0.0) for mk, wg in zip(masks, wgts)]
            for off, wg in zip(offs, wgts):
                o2 = o2 + _dot(wg.astype(MXU_DTYPE), v_ref[pl.ds(off, SB_BLOCK), :].astype(MXU_DTYPE), NN)
            return o2, rs[-1]

        top0 = (i // nu) * nu
        carry = group(top0, (jnp.zeros((2 * SB_BLOCK, LANES), F32), jnp.zeros((2 * SB_BLOCK, 1), F32)), True)
        o2, rsum = lax.fori_loop(1, i // nu + 1, lambda g, cr: group(top0 - nu * g, cr, False), carry)
        o_ref[...] = _sb_unstack(o2, lm0)
        r_ref[...] = _sb_unstack(jnp.broadcast_to(rsum, (2 * SB_BLOCK, LANES)), lm0)

    blk = pl.BlockSpec((SB_BLOCK, LANES), lambda p, i: (i, p))
    return pl.pallas_call(
        body, grid=(npair, nq),
        in_specs=[pl.BlockSpec((SB_BLOCK, LANES), lambda p, i: (i, col0 + p)),
                  pl.BlockSpec((s, LANES), lambda p, i: (0, col0 + npair + p)),
                  pl.BlockSpec((s, LANES), lambda p, i: (0, col0 + 2 * npair + p))],
        out_specs=[blk, blk],
        out_shape=[jax.ShapeDtypeStruct((s, width), F32), jax.ShapeDtypeStruct((s, width), F32)],
        compiler_params=_cparams(2), name=name,
    )(src, src, src)


def _sb_bwd(src, col0, width, rtot, do, *, name):
    s = src.shape[0]
    nq = s // SB_BLOCK
    npair = width // LANES
    scale = SB_HEAD_DIM ** -0.5
    nu = math.gcd(SB_UNROLL, nq)

    def body(q_ref, k_ref, v_ref, r_ref, do_ref, dq_ref, dk_ref, dv_ref):
        i = pl.program_id(1)

        @pl.when(i == 0)
        def _():
            dk_ref[...] = jnp.zeros_like(dk_ref)
            dv_ref[...] = jnp.zeros_like(dv_ref)

        r2, c2, lm0, m_gt, m_lt = _sb_consts()
        t_glob = i * SB_BLOCK + (r2 & (SB_BLOCK - 1))
        q2 = (_sb_stack(q_ref[...], lm0) * scale).astype(MXU_DTYPE)
        do2 = _sb_stack(do_ref[...], lm0).astype(MXU_DTYPE)
        rv = r_ref[...]
        rt = jnp.concatenate([jnp.max(jnp.where(lm0, rv, NEG_BIG), axis=1, keepdims=True),
                              jnp.max(jnp.where(lm0, NEG_BIG, rv), axis=1, keepdims=True)], axis=0)

        def group(g, carry, masked):
            dq2, psum, csum = carry
            js = [nu * g + u for u in range(nu)]
            offs = [pl.multiple_of(j * SB_BLOCK, SB_BLOCK) for j in js]
            kbs = [k_ref[pl.ds(off, SB_BLOCK), :].astype(MXU_DTYPE) for off in offs]
            zs = [_dot(q2, kb, NT) for kb in kbs]
            dws = [_dot(do2, v_ref[pl.ds(off, SB_BLOCK), :].astype(MXU_DTYPE), NT) for off in offs]
            ts = [jnp.log(1.0 + jnp.exp(-jnp.abs(z))) for z in zs]
            lks = [-(jnp.maximum(z, 0.0) + t) for z, t in zip(zs, ts)]
            if masked:
                masks = [(j * SB_BLOCK + c2) < t_glob for j in js]
                lks = [jnp.where(mk, lk, 0.0) for mk, lk in zip(masks, lks)]
            sufs = [_split_dot(lk, m_gt, SB_SPLIT) for lk in lks]
            lsums = [jnp.sum(lk, axis=1, keepdims=True) for lk in lks]
            logsigs = [jnp.minimum(z, 0.0) - t for z, t in zip(zs, ts)]
            wgts = []
            for lsg, lsum, sf in zip(logsigs, lsums, sufs):
                psum = psum + lsum
                wgts.append(jnp.exp(lsg + (rt - psum) + sf))
            if masked:
                wgts = [jnp.where(mk, wg, 0.0) for mk, wg in zip(masks, wgts)]
            dlogas = [wg * dw for wg, dw in zip(wgts, dws)]
            pres = [_split_dot(dl, m_lt, SB_SPLIT) for dl in dlogas]
            dlks = []
            for dl, pre in zip(dlogas, pres):
                dlks.append(csum + pre)
                csum = csum + jnp.sum(dl, axis=1, keepdims=True)
            if masked:
                dlks = [jnp.where(mk, dlk, 0.0) for mk, dlk in zip(masks, dlks)]
            sigs = [jnp.exp(lsg) for lsg in logsigs]
            dzbs = [(dl * (1.0 - sg) - dlk * sg).astype(MXU_DTYPE) for dl, sg, dlk in zip(dlogas, sigs, dlks)]
            for off, dzb, wg, kb in zip(offs, dzbs, wgts, kbs):
                dk_ref[pl.ds(off, SB_BLOCK), :] += _dot(dzb, q2, TN)
                dv_ref[pl.ds(off, SB_BLOCK), :] += _dot(wg.astype(MXU_DTYPE), do2, TN)
                dq2 = dq2 + _dot(dzb, kb, NN)
            return dq2, psum, csum

        zero_col = jnp.zeros((2 * SB_BLOCK, 1), F32)
        carry = lax.fori_loop(0, i // nu, lambda g, cr: group(g, cr, False),
                              (jnp.zeros((2 * SB_BLOCK, LANES), F32), zero_col, zero_col))
        dq2, _, _ = group(i // nu, carry, True)
        dq_ref[...] = _sb_unstack(dq2, lm0) * scale

    blk = pl.BlockSpec((SB_BLOCK, LANES), lambda p, i: (i, p))
    full = pl.BlockSpec((s, LANES), lambda p, i: (0, p))
    return pl.pallas_call(
        body, grid=(npair, nq),
        in_specs=[pl.BlockSpec((SB_BLOCK, LANES), lambda p, i: (i, col0 + p)),
                  pl.BlockSpec((s, LANES), lambda p, i: (0, col0 + npair + p)),
                  pl.BlockSpec((s, LANES), lambda p, i: (0, col0 + 2 * npair + p)),
                  blk, blk],
        out_specs=[blk, full, full],
        out_shape=[jax.ShapeDtypeStruct((s, width), F32)] * 3,
        compiler_params=_cparams(2), name=name,
    )(src, src, src, rtot, do)


def _ssd_group(xs, dt_rows, alogs, dtbs, bms, cms, h0s):
    c = bms[0].shape[0]
    per = len(xs) // len(bms)
    ii, jj = _chunk_masks(c)
    causal, eye = ii >= jj, ii == jj
    grp = lambda per_group: [t for t in per_group for _ in range(per)]
    scores, bm, cm = grp(_each(lambda c_, b_: _hdot(c_, b_, NT), cms, bms)), grp(bms), grp(cms)
    dt_r = _each(lambda dt, b: _softplus(dt + b), dt_rows, dtbs)
    a_r = _each(lambda al, dt: -jnp.exp(al) * dt, alogs, dt_r)
    dt_col = _each(lambda dt: _row_to_col(dt, eye), dt_r)
    a_col = _each(lambda a: _row_to_col(a, eye), a_r)
    ac_col = _each(lambda a: jnp.sum(jnp.where(causal, a, 0.0), axis=1, keepdims=True), a_r)
    ac_row = _each(lambda a: jnp.sum(jnp.where(jj >= ii, a, 0.0), axis=0, keepdims=True), a_col)
    lmat = _each(lambda c_, r_: jnp.exp(jnp.where(causal, c_ - r_, NEG_BIG)), ac_col, ac_row)
    xdt = _each(jnp.multiply, xs, dt_col)
    al = _each(lambda a: jnp.sum(a, axis=1, keepdims=True), a_r)
    ys = _each(lambda sc, lm, xd, cm_, h0, ac: _hdot(sc * lm, xd) + _hdot(cm_, h0, NT) * jnp.exp(ac),
               scores, lmat, xdt, cm, h0s, ac_col)
    h1s = _each(lambda h0, al_, xd, ac, bm_: h0 * jnp.exp(al_) + _hdot(xd * jnp.exp(al_ - ac), bm_, TN),
                h0s, al, xdt, ac_col, bm)
    return ys, h1s


def _ssd_specs(ng, nc, r, gb, rev):
    n_of = (lambda n: nc - 1 - n) if rev else (lambda n: n)
    xw, bw = gb * r * SSM_HEAD_DIM, gb * SSM_STATE
    b0, c0 = (ng * r * SSM_HEAD_DIM) // bw, (ng * r * SSM_HEAD_DIM + ng * SSM_STATE) // bw
    x_spec = pl.BlockSpec((CHUNK, xw), lambda g, n: (n_of(n), g))
    b_spec = pl.BlockSpec((CHUNK, bw), lambda g, n: (n_of(n), b0 + g))
    c_spec = pl.BlockSpec((CHUNK, bw), lambda g, n: (n_of(n), c0 + g))
    dt_spec = pl.BlockSpec((gb, None, r, CHUNK), lambda g, n: (g, n_of(n), 0, 0))
    sc_spec = pl.BlockSpec((gb, r, 1), lambda g, n: (g, 0, 0))
    st_spec = pl.BlockSpec((gb, None, r, SSM_HEAD_DIM, SSM_STATE), lambda g, n: (g, n_of(n), 0, 0, 0))
    bc_out = pl.BlockSpec((CHUNK, bw), lambda g, n: (n_of(n), g))
    return x_spec, b_spec, c_spec, dt_spec, sc_spec, st_spec, x_spec, bc_out


def _ssd_refs(gb, r, x_ref, b_ref, c_ref, dt_ref, al_ref, db_ref):
    p, n = SSM_HEAD_DIM, SSM_STATE
    heads = [(g, h) for g in range(gb) for h in range(r)]
    xs = [x_ref[:, (g * r + h) * p:(g * r + h + 1) * p] for g, h in heads]
    dts = [dt_ref[g, h:h + 1, :] for g, h in heads]
    als = [al_ref[g, h:h + 1, :] for g, h in heads]
    dbs = [db_ref[g, h:h + 1, :] for g, h in heads]
    bms = [b_ref[:, g * n:(g + 1) * n] for g in range(gb)]
    cms = [c_ref[:, g * n:(g + 1) * n] for g in range(gb)]
    return heads, xs, dts, als, dbs, bms, cms


def _ssd_fwd(xbc, dt_rows, alog, dtb, *, name):
    s = xbc.shape[0]
    ng, nc, r = dt_rows.shape[0], dt_rows.shape[1], dt_rows.shape[2]
    w = ng * r * SSM_HEAD_DIM
    gb = math.gcd(SSD_GROUPS_PER_STEP, ng)
    x_spec, b_spec, c_spec, dt_spec, sc_spec, st_spec, y_spec, _ = _ssd_specs(ng, nc, r, gb, False)
    p = SSM_HEAD_DIM

    def body(x_ref, b_ref, c_ref, dt_ref, al_ref, db_ref, y_ref, st_ref, state):
        @pl.when(pl.program_id(1) == 0)
        def _():
            state[...] = jnp.zeros_like(state)

        st_ref[...] = state[...]
        heads, xs, dts, als, dbs, bms, cms = _ssd_refs(gb, r, x_ref, b_ref, c_ref, dt_ref, al_ref, db_ref)
        ys, h1s = _ssd_group(xs, dts, als, dbs, bms, cms, [state[g, h] for g, h in heads])
        for i, (g, h) in enumerate(heads):
            y_ref[:, (g * r + h) * p:(g * r + h + 1) * p] = ys[i]
            state[g, h] = h1s[i]

    return pl.pallas_call(
        body, grid=(ng // gb, nc),
        in_specs=[x_spec, b_spec, c_spec, dt_spec, sc_spec, sc_spec],
        out_specs=[y_spec, st_spec],
        out_shape=[jax.ShapeDtypeStruct((s, w), F32), jax.ShapeDtypeStruct((ng, nc, r, p, SSM_STATE), F32)],
        scratch_shapes=[pltpu.VMEM((gb, r, p, SSM_STATE), F32)],
        compiler_params=_cparams(2), name=name,
    )(xbc, xbc, xbc, dt_rows, alog, dtb)


def _ssd_bwd(xbc, dt_rows, alog, dtb, states, dy, *, name):
    s = xbc.shape[0]
    ng, nc, r = dt_rows.shape[0], dt_rows.shape[1], dt_rows.shape[2]
    w = ng * r * SSM_HEAD_DIM
    gb = math.gcd(SSD_GROUPS_PER_STEP, ng)
    x_spec, b_spec, c_spec, dt_spec, sc_spec, st_spec, y_spec, bc_out = _ssd_specs(ng, nc, r, gb, True)
    p = SSM_HEAD_DIM

    def body(x_ref, b_ref, c_ref, dt_ref, al_ref, db_ref, st_ref, dy_ref,
             dx_ref, dbm_ref, dcm_ref, ddt_ref, dal_ref, ddb_ref, dstate):
        @pl.when(pl.program_id(1) == 0)
        def _():
            dstate[...] = jnp.zeros_like(dstate)
            dal_ref[...] = jnp.zeros_like(dal_ref)
            ddb_ref[...] = jnp.zeros_like(ddb_ref)

        heads, xs, dts, als, dbs, bms, cms = _ssd_refs(gb, r, x_ref, b_ref, c_ref, dt_ref, al_ref, db_ref)
        _, vjp = jax.vjp(_ssd_group, xs, dts, als, dbs, bms, cms, [st_ref[g, h] for g, h in heads])
        dys = [dy_ref[:, (g * r + h) * p:(g * r + h + 1) * p] for g, h in heads]
        dxs, ddts, dals, ddbs, dbms, dcms, dh0s = vjp((dys, [dstate[g, h] for g, h in heads]))
        for g in range(gb):
            dbm_ref[:, g * SSM_STATE:(g + 1) * SSM_STATE] = dbms[g]
            dcm_ref[:, g * SSM_STATE:(g + 1) * SSM_STATE] = dcms[g]
        for i, (g, h) in enumerate(heads):
            dx_ref[:, (g * r + h) * p:(g * r + h + 1) * p] = dxs[i]
            ddt_ref[g, h:h + 1, :] = ddts[i]
            dal_ref[g, h:h + 1, :] += dals[i]
            ddb_ref[g, h:h + 1, :] += ddbs[i]
            dstate[g, h] = dh0s[i]

    gn = ng * SSM_STATE
    return pl.pallas_call(
        body, grid=(ng // gb, nc),
        in_specs=[x_spec, b_spec, c_spec, dt_spec, sc_spec, sc_spec, st_spec, y_spec],
        out_specs=[y_spec, bc_out, bc_out, dt_spec, sc_spec, sc_spec],
        out_shape=[jax.ShapeDtypeStruct((s, w), F32), jax.ShapeDtypeStruct((s, gn), F32), jax.ShapeDtypeStruct((s, gn), F32),
                   jax.ShapeDtypeStruct(dt_rows.shape, F32), jax.ShapeDtypeStruct((ng, r, 1), F32),
                   jax.ShapeDtypeStruct((ng, r, 1), F32)],
        scratch_shapes=[pltpu.VMEM((gb, r, p, SSM_STATE), F32)],
        compiler_params=_cparams(2), name=name,
    )(xbc, xbc, xbc, dt_rows, alog, dtb, states, dy)


def _ssm_post_fwd(y, xbc, src, z_col0, dexp, nw, *, name, tm=256):
    s, w = y.shape
    gw = w // SSM_GROUPS
    zc = z_col0 * LANES // gw

    def body(y_ref, x_ref, z_ref, d_ref, w_ref, o_ref):
        yy = (y_ref[...] + x_ref[...] * d_ref[...]) * _silu(z_ref[...])
        r = lax.rsqrt(jnp.mean(yy * yy, axis=-1, keepdims=True) + EPS)
        o_ref[...] = (yy * r * w_ref[...]).astype(o_ref.dtype)

    blk = pl.BlockSpec((tm, gw), lambda g, i: (i, g))
    vec = pl.BlockSpec((1, gw), lambda g, i: (0, g))
    return pl.pallas_call(
        body, grid=(SSM_GROUPS, s // tm),
        in_specs=[blk, blk, pl.BlockSpec((tm, gw), lambda g, i: (i, zc + g)), vec, vec],
        out_specs=blk, out_shape=jax.ShapeDtypeStruct((s, w), MXU_DTYPE),
        compiler_params=_cparams(2), name=name,
    )(y, xbc, src, dexp.reshape(1, w), nw.reshape(1, w))


def _ssm_post_bwd(y, xbc, src, z_col0, dexp, nw, dout, into, *, name, tm=256):
    s, w = y.shape
    gw = w // SSM_GROUPS
    zc = z_col0 * LANES // gw

    def body(y_ref, x_ref, z_ref, d_ref, w_ref, do_ref, into_ref, dy_ref, dx_ref, dz_ref, dd_ref, dw_ref):
        xv, zv, dv = x_ref[...], z_ref[...], d_ref[...]
        pre = y_ref[...] + xv * dv
        sz = _silu(zv)
        yy = pre * sz
        r = lax.rsqrt(jnp.mean(yy * yy, axis=-1, keepdims=True) + EPS)
        yh = yy * r
        dov = do_ref[...]
        dyn = dov * w_ref[...]
        dyy = r * (dyn - yh * jnp.mean(dyn * yh, axis=-1, keepdims=True))
        dpre = dyy * sz
        dy_ref[...] = dpre
        dx_ref[...] = dpre * dv
        dz_ref[...] = (dyy * pre * _silu_grad(zv)).astype(dz_ref.dtype)

        @pl.when(pl.program_id(1) == 0)
        def _():
            dd_ref[...] = jnp.zeros_like(dd_ref)
            dw_ref[...] = jnp.zeros_like(dw_ref)

        dd_ref[...] += jnp.sum(dpre * xv, axis=0, keepdims=True)
        dw_ref[...] += jnp.sum(dov * yh, axis=0, keepdims=True)

    blk = pl.BlockSpec((tm, gw), lambda g, i: (i, g))
    vec = pl.BlockSpec((1, gw), lambda g, i: (0, g))
    z_blk = pl.BlockSpec((tm, gw), lambda g, i: (i, zc + g))
    dy, dx, dz, dd, dw = pl.pallas_call(
        body, grid=(SSM_GROUPS, s // tm),
        in_specs=[blk, blk, z_blk, vec, vec, blk, ANY],
        out_specs=[blk, blk, z_blk, vec, vec],
        out_shape=[jax.ShapeDtypeStruct((s, w), F32), jax.ShapeDtypeStruct((s, w), F32),
                   jax.ShapeDtypeStruct(into.shape, into.dtype), jax.ShapeDtypeStruct((1, w), F32),
                   jax.ShapeDtypeStruct((1, w), F32)],
        input_output_aliases={6: 2},
        compiler_params=_cparams(2), name=name,
    )(y, xbc, src, dexp.reshape(1, w), nw.reshape(1, w), dout, into)
    return dy, dx, dz, dd.reshape(w), dw.reshape(w)


def _merge_fwd(proj3, src, gate_col0, d, *, name, tm=256):
    s = proj3.shape[0]
    nb = proj3.shape[1] // d
    gc = gate_col0 * LANES // d

    def body(*refs):
        p_refs, g_refs, o_ref = refs[:nb], refs[nb:2 * nb], refs[-1]
        acc = None
        for p_ref, g_ref in zip(p_refs, g_refs):
            term = _sigmoid(g_ref[...]) * p_ref[...]
            acc = term if acc is None else acc + term
        o_ref[...] = acc.astype(o_ref.dtype)

    p_specs = [pl.BlockSpec((tm, d), lambda i, b=b: (i, b)) for b in range(nb)]
    g_specs = [pl.BlockSpec((tm, d), lambda i, b=b: (i, gc + b)) for b in range(nb)]
    return pl.pallas_call(
        body, grid=(s // tm,), in_specs=p_specs + g_specs,
        out_specs=pl.BlockSpec((tm, d), lambda i: (i, 0)), out_shape=jax.ShapeDtypeStruct((s, d), MXU_DTYPE),
        compiler_params=_cparams(1), name=name,
    )(*([proj3] * nb), *([src] * nb))


def _merge_bwd(proj3, src, gate_col0, d, dmerged, into, *, name, tm=256):
    s = proj3.shape[0]
    nb = proj3.shape[1] // d
    gc = gate_col0 * LANES // d

    def body(p_ref, g_ref, dm_ref, into_ref, dp_ref, dg_ref):
        sg = _sigmoid(g_ref[...])
        dm = dm_ref[...]
        dp_ref[...] = (dm * sg).astype(dp_ref.dtype)
        dg_ref[...] = (dm * p_ref[...] * sg * (1.0 - sg)).astype(dg_ref.dtype)

    blk = pl.BlockSpec((tm, d), lambda i, b: (i, b))
    gate_blk = pl.BlockSpec((tm, d), lambda i, b: (i, gc + b))
    return pl.pallas_call(
        body, grid=(s // tm, nb),
        in_specs=[blk, gate_blk, pl.BlockSpec((tm, d), lambda i, b: (i, 0)), ANY],
        out_specs=[blk, gate_blk],
        out_shape=[jax.ShapeDtypeStruct(proj3.shape, MXU_DTYPE), jax.ShapeDtypeStruct(into.shape, into.dtype)],
        input_output_aliases={3: 1},
        compiler_params=_cparams(2), name=name,
    )(proj3, src, dmerged, into)


ANY = pl.BlockSpec(memory_space=pl.ANY)
MESH = pl.DeviceIdType.MESH


def _all_gather(shards, *, name, after=None):
    nt = len(shards)
    n_after = 0 if after is None else 1

    def body(*refs):
        x_refs, out_refs = refs[:nt], refs[nt + n_after:2 * nt + n_after]
        send_sems, recv_sems, local_sems = refs[2 * nt + n_after:]
        x, y, c = lax.axis_index("x"), lax.axis_index("y"), lax.axis_index("c")
        me, sibling = (x, y, c), (x, y, 1 - c)
        chips = [(1 - x, y), (x, 1 - y), (1 - x, 1 - y)]

        def slot(t, px, py, pc):
            return out_refs[t].at[4 * px + 2 * py + pc]

        def copy(t, k, block, to, from_input=False):
            return pltpu.make_async_remote_copy(
                src_ref=x_refs[t] if from_input else slot(t, *block), dst_ref=slot(t, *block),
                send_sem=send_sems.at[7 * t + k], recv_sem=recv_sems.at[7 * t + k], device_id=to, device_id_type=MESH)

        mine = [pltpu.make_async_copy(x_refs[t], slot(t, *me), local_sems.at[t]) for t in range(nt)]
        for cp in mine:
            cp.start()
        first = [copy(t, 0, me, sibling, True) for t in range(nt)]
        first += [copy(t, 1 + j, me, (*chip, c), True) for j, chip in enumerate(chips) for t in range(nt)]
        for cp in first:
            cp.start()
        passed = []
        for j, chip in enumerate(chips):
            for t in range(nt):
                copy(t, 1 + j, (*chip, c), me).wait_recv()
                fwd = copy(t, 4 + j, (*chip, c), sibling)
                fwd.start()
                passed.append(fwd)
        for t in range(nt):
            copy(t, 0, sibling, me).wait_recv()
            for j, chip in enumerate(chips):
                copy(t, 4 + j, (*chip, 1 - c), me).wait_recv()
        for cp in first + passed:
            cp.wait_send()
        for cp in mine:
            cp.wait()

    return pl.pallas_call(
        body, out_shape=[jax.ShapeDtypeStruct((N_DEV,) + a.shape, a.dtype) for a in shards],
        in_specs=[ANY] * (nt + n_after), out_specs=[ANY] * nt,
        scratch_shapes=[pltpu.SemaphoreType.DMA((7 * nt,)), pltpu.SemaphoreType.DMA((7 * nt,)),
                        pltpu.SemaphoreType.DMA((nt,))],
        name=name,
    )(*shards, *([] if after is None else [after]))


def _grad_exchange(bigs, small, *, name):
    nl = len(bigs[0])
    flat = [a for per_layer in bigs for a in per_layer]
    nslot = len(flat)

    def body(*refs):
        in_refs, small_ref = refs[:nslot], refs[nslot]
        out_refs, smallr_ref = refs[nslot + 1:nslot + 1 + len(bigs)], refs[nslot + 1 + len(bigs)]
        send_sems, recv_sems, local_sems = refs[nslot + 2 + len(bigs):]
        x, y, c = lax.axis_index("x"), lax.axis_index("y"), lax.axis_index("c")
        me = 4 * x + 2 * y + c
        local = [pltpu.make_async_copy(in_refs[i].at[me], out_refs[i // nl].at[me, i % nl], local_sems.at[i])
                 for i in range(nslot)]
        local.append(pltpu.make_async_copy(small_ref, smallr_ref.at[me], local_sems.at[nslot]))
        for cp in local:
            cp.start()
        copies = []
        for k in range(1, N_DEV):
            px = x ^ ((k >> 2) & 1)
            py = y ^ ((k >> 1) & 1)
            pc = c ^ (k & 1)
            peer = 4 * px + 2 * py + pc
            for i in range(nslot + 1):
                sem = 7 * i + (k - 1)
                src = in_refs[i].at[peer] if i < nslot else small_ref
                dst = out_refs[i // nl].at[me, i % nl] if i < nslot else smallr_ref.at[me]
                copies.append(pltpu.make_async_remote_copy(
                    src_ref=src, dst_ref=dst, send_sem=send_sems.at[sem], recv_sem=recv_sems.at[sem],
                    device_id=(px, py, pc), device_id_type=MESH))
        for cp in copies:
            cp.start()
        for cp in copies:
            cp.wait_recv()
        for cp in copies:
            cp.wait_send()
        for cp in local:
            cp.wait()

    out_shape = [jax.ShapeDtypeStruct((N_DEV, nl) + per_layer[0].shape[1:], per_layer[0].dtype) for per_layer in bigs]
    out_shape.append(jax.ShapeDtypeStruct((N_DEV,) + small.shape, small.dtype))
    nsem = 7 * (nslot + 1)
    outs = pl.pallas_call(
        body, out_shape=out_shape,
        in_specs=[ANY] * (nslot + 1), out_specs=[ANY] * (len(bigs) + 1),
        scratch_shapes=[pltpu.SemaphoreType.DMA((nsem,)), pltpu.SemaphoreType.DMA((nsem,)),
                        pltpu.SemaphoreType.DMA((nslot + 1,))],
        name=name,
    )(*flat, small)
    return outs[:-1], outs[-1]


HBM = pl.BlockSpec(memory_space=pltpu.HBM)
SEM = pl.BlockSpec(memory_space=pltpu.SEMAPHORE)
EFFECT = pltpu.SideEffectType.DATAFLOW_SIDE_EFFECTING


def _peers():
    x, y, c = lax.axis_index("x"), lax.axis_index("y"), lax.axis_index("c")
    peers = []
    for k in range(1, N_DEV):
        px, py, pc = x ^ ((k >> 2) & 1), y ^ ((k >> 1) & 1), c ^ (k & 1)
        peers.append(((px, py, pc), 4 * px + 2 * py + pc))
    return 4 * x + 2 * y + c, peers


def _split_copies(slots, src_refs, land_refs, send_sems, recv_sems):
    me, peers = _peers()
    copies = []
    for t, (whole, layer) in enumerate(slots):
        dst = land_refs[t].at[me] if layer is None else land_refs[t].at[me, layer]
        for k, (dev, lin) in enumerate(peers):
            copies.append(pltpu.make_async_remote_copy(
                src_ref=src_refs[t] if whole else src_refs[t].at[lin], dst_ref=dst,
                send_sem=send_sems.at[7 * t + k], recv_sem=recv_sems.at[7 * t + k], device_id=dev, device_id_type=MESH))
    return copies


def _split_start(srcs, lands, slots, carry, *, name):
    n = len(srcs)

    def body(*refs):
        copies = _split_copies(slots, refs[:n], refs[n:2 * n], refs[2 * n + 1], refs[2 * n + 2])
        for cp in copies:
            cp.start()

    def hbm(a):
        return pltpu.HBM(a.shape, a.dtype)

    outs = pl.pallas_call(
        body, name=name,
        out_shape=[pltpu.SemaphoreType.DMA((7 * n,)), pltpu.SemaphoreType.DMA((7 * n,))]
        + [hbm(a) for a in srcs] + [hbm(a) for a in lands] + [hbm(carry)],
        in_specs=[HBM] * (2 * n + 1), out_specs=[SEM, SEM] + [HBM] * (2 * n + 1),
        input_output_aliases={i: 2 + i for i in range(2 * n + 1)},
        compiler_params=pltpu.CompilerParams(has_side_effects=EFFECT),
    )(*[pltpu.with_memory_space_constraint(a, pltpu.HBM) for a in list(srcs) + list(lands) + [carry]])
    return outs[0], outs[1], outs[2:2 + n], outs[2 + n:2 + 2 * n], outs[2 + 2 * n]


def _split_wait(send_sems, recv_sems, srcs, lands, slots, after, *, name):
    n = len(srcs)

    def body(*refs):
        copies = _split_copies(slots, refs[:n], refs[n:2 * n], refs[2 * n], refs[2 * n + 1])
        for cp in copies:
            cp.wait_send()
        for cp in copies:
            cp.wait_recv()

    outs = pl.pallas_call(
        body, name=name,
        out_shape=[pltpu.HBM(a.shape, a.dtype) for a in list(srcs) + list(lands)],
        in_specs=[HBM] * (2 * n) + [SEM, SEM, ANY], out_specs=[HBM] * (2 * n),
        input_output_aliases={i: i for i in range(2 * n)},
        compiler_params=pltpu.CompilerParams(has_side_effects=EFFECT),
    )(*srcs, *lands, send_sems, recv_sems, after)
    return outs[n:]


def _adam_math(w, g, m, v):
    m1 = ADAM_B1 * m + (1.0 - ADAM_B1) * g
    v1 = ADAM_B2 * v + (1.0 - ADAM_B2) * (g * g)
    m_hat = m1 / (1.0 - ADAM_B1 ** ADAM_STEP)
    v_hat = v1 / (1.0 - ADAM_B2 ** ADAM_STEP)
    delta = -ADAM_LR * (m_hat / (jnp.sqrt(v_hat) + ADAM_EPS) + ADAM_WD * w)
    return delta, m1, v1


def _sum_adamw(parts, w, m, v, layer, prev, *, name):
    shape = w.shape
    r, c = shape[-2], shape[-1]
    a_l = math.prod(shape[1:-2])
    a = shape[0] * a_l
    base = layer * a_l
    if r % 256 == 0:
        tr, tc = 256, c
    else:
        tr, tc = r, _pick(c, (256, 128))
    w3, m3, v3 = (t.reshape(a, r, c) for t in (w, m, v))
    n_prev = 0 if prev is None else 4

    def body(*refs):
        p_ref, w_ref, m_ref, v_ref = refs[:4]
        g_ref, d_ref, m1_ref, v1_ref = refs[4 + n_prev:]
        g = p_ref[0].astype(F32)
        for src in range(1, N_DEV):
            g = g + p_ref[src].astype(F32)
        delta, m1, v1 = _adam_math(w_ref[...], g, m_ref[...], v_ref[...])
        g_ref[...] = g
        d_ref[...] = delta
        m1_ref[...] = m1
        v1_ref[...] = v1

    nr, ncol = r // tr, c // tc
    blk = pl.BlockSpec((None, tr, tc), lambda i, j: (base + i, j // ncol, j % ncol))
    prev3 = [] if prev is None else [t.reshape(a, r, c) for t in prev]
    outs = pl.pallas_call(
        body, grid=(a_l, nr * ncol),
        in_specs=[pl.BlockSpec((N_DEV, None, tr, tc), lambda i, j: (0, i, j // ncol, j % ncol)), blk, blk, blk]
        + [ANY] * n_prev,
        out_specs=[blk] * 4, out_shape=[jax.ShapeDtypeStruct((a, r, c), F32)] * 4,
        input_output_aliases={4 + k: k for k in range(n_prev)},
        compiler_params=_cparams(2), name=name,
    )(parts.reshape(N_DEV, a_l, r, c), w3, m3, v3, *prev3)
    return [o.reshape(shape) for o in outs]


def _sum_parts(parts, *, name):
    rows = parts.shape[1]

    def body(p_ref, o_ref):
        g = p_ref[0]
        for src in range(1, N_DEV):
            g = g + p_ref[src]
        o_ref[...] = g

    return pl.pallas_call(
        body, grid=(1,), in_specs=[pl.BlockSpec((N_DEV, rows, LANES), lambda i: (0, 0, 0))],
        out_specs=pl.BlockSpec((rows, LANES), lambda i: (0, 0)), out_shape=jax.ShapeDtypeStruct((rows, LANES), F32),
        compiler_params=_cparams(1), name=name,
    )(parts)


def _adamw(w, g, m, v, *, name):
    rows = w.shape[0]

    def body(w_ref, g_ref, m_ref, v_ref, d_ref, m1_ref, v1_ref):
        delta, m1, v1 = _adam_math(w_ref[...], g_ref[...], m_ref[...], v_ref[...])
        d_ref[...] = delta
        m1_ref[...] = m1
        v1_ref[...] = v1

    blk = pl.BlockSpec((rows, LANES), lambda i: (0, 0))
    return pl.pallas_call(
        body, grid=(1,), in_specs=[blk] * 4, out_specs=[blk] * 3,
        out_shape=[jax.ShapeDtypeStruct((rows, LANES), F32)] * 3,
        compiler_params=_cparams(1), name=name,
    )(w, g, m, v)


def _pack(arrs, dtype, row_mult=16):
    flat = jnp.concatenate([a.reshape(-1).astype(dtype) for a in arrs])
    n = flat.shape[0]
    rows = -(-n // (LANES * row_mult)) * row_mult
    flat = jnp.pad(flat, (0, rows * LANES - n))
    return flat.reshape(rows, LANES)


def _unpack(packed, shapes):
    flat = packed.reshape(-1)
    out, off = [], 0
    for shp in shapes:
        n = math.prod(shp)
        out.append(flat[off:off + n].reshape(shp))
        off += n
    return out


class _Layout:
    def __init__(self, d):
        self.d = d
        w = d
        self.dn_heads = w // DN_HEAD_DIM
        self.ssm_heads = w // SSM_HEAD_DIM
        gn = SSM_GROUPS * SSM_STATE
        self.sizes = (3 * w, w, self.dn_heads, self.dn_heads, 3 * w, w, w + 2 * gn, self.ssm_heads, 3 * d)
        offs, o = [], 0
        for sz in self.sizes:
            offs.append(o)
            o += sz
        self.offs = offs
        self.in_dim = o
        self.big = (0, 1, 4, 5, 6, 8)
        self.small = (2, 3, 7)
        cols, o = {}, 0
        for idx in self.big:
            cols[idx] = o
            o += self.sizes[idx]
        self.small_col = o
        self.cols = cols
        self.padded = o + LANES
        self.n_small = sum(self.sizes[i] for i in self.small)

    def reorder_w(self, w_in):
        parts = [w_in[:, self.offs[i]:self.offs[i] + self.sizes[i]] for i in self.big + self.small]
        parts.append(jnp.zeros((w_in.shape[0], LANES - self.n_small), w_in.dtype))
        return jnp.concatenate(parts, axis=1)

    def from_shards(self, parts):
        cs = self.in_dim // N_DEV
        pieces = []
        for i in self.big + self.small:
            a, b = self.offs[i], self.offs[i] + self.sizes[i]
            while a < b:
                j = a // cs
                hi = min(b, (j + 1) * cs)
                pieces.append(parts[j][:, a - j * cs:hi - j * cs])
                a = hi
        pieces.append(jnp.zeros((parts.shape[1], LANES - self.n_small), parts.dtype))
        return jnp.concatenate(pieces, axis=1)

    def to_shards(self, wp):
        cs = self.in_dim // N_DEV
        pcol = dict(self.cols)
        o = self.small_col
        for i in self.small:
            pcol[i] = o
            o += self.sizes[i]
        shards = []
        for j in range(N_DEV):
            a, b = j * cs, (j + 1) * cs
            pieces = []
            for i in range(len(self.sizes)):
                lo, hi = max(a, self.offs[i]), min(b, self.offs[i] + self.sizes[i])
                if lo < hi:
                    pieces.append(wp[:, pcol[i] + lo - self.offs[i]:pcol[i] + hi - self.offs[i]])
            shards.append(jnp.concatenate(pieces, axis=1))
        return jnp.stack(shards)

    def restore_w(self, wp):
        pieces = {}
        for idx in self.big:
            pieces[idx] = wp[:, self.cols[idx]:self.cols[idx] + self.sizes[idx]]
        o = self.small_col
        for idx in self.small:
            pieces[idx] = wp[:, o:o + self.sizes[idx]]
            o += self.sizes[idx]
        return jnp.concatenate([pieces[i] for i in range(len(self.sizes))], axis=1)


def _rows_form(cols_t, nh, nc):
    return cols_t.T.reshape(nh, nc, 1, CHUNK)


def _layer_fwd(x, p, lay, tag, late=None):
    s, d = x.shape
    nc = s // CHUNK
    w = d
    dnh, smh = lay.dn_heads, lay.ssm_heads
    r = smh // SSM_GROUPS
    cb = {k: v // LANES for k, v in lay.cols.items()}
    sv = {}
    h1 = _rms_fwd(x, p["norm_mix"], name=f"rms_mix_{tag}")
    proj = _matmul(h1, p["w_in"], name=f"mm_in_{tag}")
    small = proj[:, lay.small_col:lay.small_col + LANES]
    a_rows = _rows_form(small[:, 0:dnh], dnh, nc)
    b_rows = _rows_form(small[:, dnh:2 * dnh], dnh, nc)
    dt_rows = small[:, 2 * dnh:2 * dnh + smh].T.reshape(SSM_GROUPS, r, nc, CHUNK).transpose(0, 2, 1, 3)
    zero_b = jnp.zeros((1, 3 * w), F32)
    dn_qkv = _conv_fwd(proj, cb[0], p["dn_conv_w"], zero_b, 2 * dnh, name=f"dn_conv_{tag}")
    dn_alog = p["dn_a_log"].reshape(dnh, 1, 1)
    dn_dtb = p["dn_dt_bias"].reshape(dnh, 1, 1)
    o_dn, dn_states, dn_inv = _dn_fwd(dn_qkv, a_rows, b_rows, dn_alog, dn_dtb, name=f"dn_chunk_{tag}")
    y_dn = _dn_post_fwd(o_dn, proj, cb[1], p["dn_norm_w"], name=f"dn_post_{tag}")
    o_sb, sb_r = _sb_fwd(proj, cb[4], w, name=f"sb_{tag}")
    xbc = _conv_fwd(proj, cb[6], p["ssm_conv_w"], p["ssm_conv_b"].reshape(1, -1), 0, name=f"ssm_conv_{tag}")
    ssm_alog = p["ssm_a_log"].reshape(SSM_GROUPS, r, 1)
    ssm_dtb = p["ssm_dt_bias"].reshape(SSM_GROUPS, r, 1)
    y_ssd, ssm_states = _ssd_fwd(xbc, dt_rows, ssm_alog, ssm_dtb, name=f"ssd_{tag}")
    dexp = jnp.repeat(p["ssm_d"], SSM_HEAD_DIM)
    y_ssm = _ssm_post_fwd(y_ssd, xbc, proj, cb[5], dexp, p["ssm_norm_w"], name=f"ssm_post_{tag}")
    if late is not None:
        p.update(late(y_ssm))
    branches = (y_dn, o_sb, y_ssm)
    proj3 = jnp.concatenate(
        [_matmul(br, p["w_branch"][i], name=f"mm_branch{i}_{tag}") for i, br in enumerate(branches)], axis=1)
    merged = _merge_fwd(proj3, proj, cb[8], d, name=f"merge_{tag}")
    x1 = _matmul(merged, p["w_out"], name=f"mm_out_{tag}", epilogue=lambda acc, res: (acc + res,), extras=(x,))
    h2 = _rms_fwd(x1, p["norm_mlp"], name=f"rms_mlp_{tag}")
    u, act = _matmul(h2, p["w_up"], name=f"mm_up_{tag}", out_dtypes=(F32, MXU_DTYPE),
                     epilogue=lambda acc: (acc, jnp.square(jnp.maximum(acc, 0.0))))
    x2 = _matmul(act, p["w_down"], name=f"mm_down_{tag}", epilogue=lambda acc, res: (acc + res,), extras=(x1,))
    sv.update(x=x, h1=h1, proj=proj, a_rows=a_rows, b_rows=b_rows, dt_rows=dt_rows, dn_qkv=dn_qkv, dn_alog=dn_alog,
              dn_dtb=dn_dtb, o_dn=o_dn, dn_states=dn_states, dn_inv=dn_inv, y_dn=y_dn, o_sb=o_sb, sb_r=sb_r, xbc=xbc, ssm_alog=ssm_alog,
              ssm_dtb=ssm_dtb, y_ssd=y_ssd, ssm_states=ssm_states, dexp=dexp, y_ssm=y_ssm, proj3=proj3, merged=merged,
              x1=x1, h2=h2, u=u, act=act)
    return x2, sv


def _layer_bwd(dx2, p, sv, lay, tag, early=None, late=None):
    x = sv["x"]
    s, d = x.shape
    nc = s // CHUNK
    w = d
    dnh, smh = lay.dn_heads, lay.ssm_heads
    r = smh // SSM_GROUPS
    gn = SSM_GROUPS * SSM_STATE
    cb = {k: v // LANES for k, v in lay.cols.items()}
    proj = sv["proj"]
    g = {}
    dx2_b = dx2.astype(MXU_DTYPE)
    du = _matmul(dx2_b, p["w_down"], tb=True, name=f"mm_down_dx_{tag}", out_dtypes=(MXU_DTYPE,),
                 epilogue=lambda acc, uu: (acc * (2.0 * jnp.maximum(uu, 0.0)),), extras=(sv["u"],))
    g["w_down"] = _matmul(sv["act"], dx2_b, ta=True, name=f"mm_down_dw_{tag}", out_dtypes=(BF16,)).reshape(N_DEV, -1, d)
    g["w_up"] = _matmul(sv["h2"], du, ta=True, name=f"mm_up_dw_{tag}", out_dtypes=(BF16,), col_shards=N_DEV)
    dh2 = _matmul(du, p["w_up"], tb=True, name=f"mm_up_dx_{tag}")
    dx1, g["norm_mlp"] = _rms_bwd(sv["x1"], p["norm_mlp"], dh2, dx2, name=f"rms_mlp_bwd_{tag}")
    dx1_b = dx1.astype(MXU_DTYPE)
    dmerged = _matmul(dx1_b, p["w_out"], tb=True, name=f"mm_out_dx_{tag}")
    g["w_out"] = _matmul(sv["merged"], dx1_b, ta=True, name=f"mm_out_dw_{tag}", out_dtypes=(BF16,)).reshape(N_DEV, -1, d)
    dproj = lax.empty((s, lay.padded), MXU_DTYPE)
    dproj3, dproj = _merge_bwd(sv["proj3"], proj, cb[8], d, dmerged, dproj, name=f"merge_bwd_{tag}")
    branches = (sv["y_dn"], sv["o_sb"], sv["y_ssm"])
    dwb, dbr = [], []
    for i, br in enumerate(branches):
        dp_i = dproj3[:, i * d:(i + 1) * d]
        dwb.append(_matmul(br, dp_i, ta=True, name=f"mm_branch{i}_dw_{tag}", out_dtypes=(BF16,)).reshape(N_DEV, -1, d))
        dbr.append(_matmul(dp_i, p["w_branch"][i], tb=True, name=f"mm_branch{i}_dx_{tag}"))
    g["w_branch"] = jnp.stack(dwb, axis=1)
    dy_dn, do_sb, dy_ssm = dbr
    if early is not None:
        dy_ssm = early(g, dy_ssm)
    dy_ssd, dxs_skip, dproj, ddexp, g["ssm_norm_w"] = _ssm_post_bwd(
        sv["y_ssd"], sv["xbc"], proj, cb[5], sv["dexp"], p["ssm_norm_w"], dy_ssm, dproj, name=f"ssm_post_bwd_{tag}")
    g["ssm_d"] = ddexp.reshape(smh, SSM_HEAD_DIM).sum(axis=1)
    dxs, dbm, dcm, ddt_rows, dalog, ddtb = _ssd_bwd(
        sv["xbc"], sv["dt_rows"], sv["ssm_alog"], sv["ssm_dtb"], sv["ssm_states"], dy_ssd, name=f"ssd_bwd_{tag}")
    g["ssm_a_log"] = dalog.reshape(smh)
    g["ssm_dt_bias"] = ddtb.reshape(smh)
    dxbc_post = jnp.concatenate([dxs + dxs_skip, dbm, dcm], axis=1)
    dproj, g["ssm_conv_w"], dcb = _conv_bwd(proj, cb[6], p["ssm_conv_w"], p["ssm_conv_b"].reshape(1, -1), 0, dxbc_post,
                                            dproj, name=f"ssm_conv_bwd_{tag}")
    g["ssm_conv_b"] = dcb.reshape(-1)
    ddt = ddt_rows.transpose(0, 2, 1, 3).reshape(smh, s).T
    dqkv_sb = _sb_bwd(proj, cb[4], w, sv["sb_r"], do_sb, name=f"sb_bwd_{tag}")
    dproj = lax.dynamic_update_slice(dproj, jnp.concatenate([t.astype(MXU_DTYPE) for t in dqkv_sb], axis=1), (0, lay.cols[4]))
    do_dn, dproj, g["dn_norm_w"] = _dn_post_bwd(sv["o_dn"], proj, cb[1], p["dn_norm_w"], dy_dn, dproj,
                                                name=f"dn_post_bwd_{tag}")
    dqkv_dn, da_rows, db_rows, dal, ddtb_dn = _dn_bwd(
        sv["dn_qkv"], sv["a_rows"], sv["b_rows"], sv["dn_alog"], sv["dn_dtb"], sv["dn_states"], sv["dn_inv"], do_dn,
        name=f"dn_chunk_bwd_{tag}")
    g["dn_a_log"] = dal.reshape(dnh)
    g["dn_dt_bias"] = ddtb_dn.reshape(dnh)
    zero_b = jnp.zeros((1, 3 * w), F32)
    dproj, g["dn_conv_w"], _ = _conv_bwd(proj, cb[0], p["dn_conv_w"], zero_b, 2 * dnh, dqkv_dn, dproj,
                                         name=f"dn_conv_bwd_{tag}")
    da = da_rows.reshape(dnh, s).T
    db = db_rows.reshape(dnh, s).T
    dsmall = jnp.concatenate([da, db, ddt, jnp.zeros((s, LANES - lay.n_small), F32)], axis=1).astype(MXU_DTYPE)
    dproj = lax.dynamic_update_slice(dproj, dsmall, (0, lay.small_col))
    g["w_in"] = lay.to_shards(_matmul(sv["h1"], dproj, ta=True, name=f"mm_in_dw_{tag}", out_dtypes=(BF16,)))
    if late is not None:
        dproj = late(g, dproj)
    dh1 = _matmul(dproj, p["w_in"], tb=True, name=f"mm_in_dx_{tag}")
    dx0, g["norm_mix"] = _rms_bwd(x, p["norm_mix"], dh1, dx1, name=f"rms_mix_bwd_{tag}")
    return dx0, g


BIG = ("w_in", "w_branch", "w_out", "w_up", "w_down")
CONV = ("dn_conv_w", "ssm_conv_w")
SMALL = ("norm_mix", "dn_conv_w", "dn_a_log", "dn_dt_bias", "dn_norm_w", "ssm_conv_w", "ssm_conv_b", "ssm_a_log",
         "ssm_dt_bias", "ssm_d", "ssm_norm_w", "norm_mlp", "norm_final")
WEIGHTS = ("norm_mix", "w_in", "dn_conv_w", "dn_a_log", "dn_dt_bias", "dn_norm_w", "ssm_conv_w", "ssm_conv_b", "ssm_a_log",
           "ssm_dt_bias", "ssm_d", "ssm_norm_w", "w_branch", "w_out", "norm_mlp", "w_up", "w_down", "norm_final")
SHARD_AXIS = {"w_in": 2, "dn_conv_w": 2, "ssm_conv_w": 2, "w_branch": 2, "w_out": 1, "w_up": 2, "w_down": 1}


def _to_shards(full, axis):
    shp = full.shape
    n = shp[axis] // N_DEV
    t = full.reshape(shp[:axis] + (N_DEV, n) + shp[axis + 1:])
    return jnp.moveaxis(t, axis, 0)


def _from_shards(parts, axis):
    t = jnp.moveaxis(parts, 0, axis)
    shp = t.shape
    return t.reshape(shp[:axis] + (shp[axis] * shp[axis + 1],) + shp[axis + 2:])


def _unshard(parts, axis, *, name):
    shard = parts.shape[1:]
    nd = len(shard)
    if axis == 0:
        return parts.reshape((N_DEV * shard[0],) + shard[1:])

    def copy_block(i_ref, o_ref):
        o_ref[...] = i_ref[...]

    if axis == nd - 1:
        rows, n = math.prod(shard[:-1]), shard[-1]
        out = pl.pallas_call(
            copy_block, grid=(N_DEV,),
            in_specs=[pl.BlockSpec((None, rows, n), lambda j: (j, 0, 0))],
            out_specs=pl.BlockSpec((rows, n), lambda j: (0, j)),
            out_shape=jax.ShapeDtypeStruct((rows, N_DEV * n), parts.dtype),
            compiler_params=_cparams(1), name=name,
        )(parts.reshape(N_DEV, rows, n))
        return out.reshape(shard[:-1] + (N_DEV * n,))
    assert axis == nd - 2, (parts.shape, axis)
    a, n, c = math.prod(shard[:-2]), shard[-2], shard[-1]
    out = pl.pallas_call(
        copy_block, grid=(N_DEV, a),
        in_specs=[pl.BlockSpec((None, None, n, c), lambda j, i: (j, i, 0, 0))],
        out_specs=pl.BlockSpec((None, n, c), lambda j, i: (i, j, 0)),
        out_shape=jax.ShapeDtypeStruct((a, N_DEV * n, c), parts.dtype),
        compiler_params=_cparams(2), name=name,
    )(parts.reshape(N_DEV, a, n, c))
    return out.reshape(shard[:-2] + (N_DEV * n, c))


def _step(w, m, v, x, target):
    s, d = x.shape
    lay = _Layout(d)
    me = 4 * lax.axis_index("x") + 2 * lax.axis_index("y") + lax.axis_index("c")

    def shard(n, l):
        return w[n][l].astype(BF16) if n in BIG else w[n][l]

    def empty_land(a):
        return lax.empty((N_DEV,) + a.shape, a.dtype)

    def with_own(land, own):
        return lax.dynamic_update_index_in_dim(land, own, me, 0)

    def assemble(n, parts, l):
        return lay.from_shards(parts) if n == "w_in" else _unshard(parts, SHARD_AXIS[n] - 1, name=f"unshard_{n}_l{l}")

    small_names = tuple(n for n in WEIGHTS if n not in BIG + CONV + ("norm_final",))

    first, rest = ("w_in",) + CONV, BIG[1:]
    got = _all_gather([shard(n, 0) for n in first], name="gather_l0_first")
    whole, sliced = (True, None), (False, None)
    names_a, names_b = rest, BIG + CONV
    srcs_a, srcs_b = [shard(n, 0) for n in names_a], [shard(n, 1) for n in names_b]
    sem_sa, sem_ra, srcs_a, lands_a, w_in0 = _split_start(
        srcs_a, [empty_land(a) for a in srcs_a], [whole] * len(srcs_a), got[0], name="gather_l0_rest_start")
    sem_sb, sem_rb, srcs_b, lands_b, w_in0 = _split_start(
        srcs_b, [empty_land(a) for a in srcs_b], [whole] * len(srcs_b), w_in0, name="gather_l1_start")
    p0 = {n: w[n][0] for n in small_names}
    p0.update({n: assemble(n, g, 0) for n, g in zip(first, [w_in0] + list(got[1:]))})

    def late_l0(after):
        lands = _split_wait(sem_sa, sem_ra, srcs_a, lands_a, [whole] * len(srcs_a), after, name="gather_l0_rest_wait")
        return {n: assemble(n, with_own(ld, s_), 0) for n, ld, s_ in zip(names_a, lands, srcs_a)}

    h, sv0 = _layer_fwd(x, p0, lay, "l0", late=late_l0)
    lands = _split_wait(sem_sb, sem_rb, srcs_b, lands_b, [whole] * len(srcs_b), h, name="gather_l1_wait")
    p1 = {n: w[n][1] for n in small_names}
    p1.update({n: assemble(n, with_own(ld, s_), 1) for n, ld, s_ in zip(names_b, lands, srcs_b)})
    h, sv1 = _layer_fwd(h, p1, lay, "l1")
    loss, dh, g_norm_final = _final_loss(h, w["norm_final"], target, name="final_loss")
    grads = [None] * DEPTH
    dh, grads[1] = _layer_bwd(dh, p1, sv1, lay, "l1")

    def exchange_start(names, g, carry, tag):
        srcs = [g[n] for n in names]
        return _split_start(srcs, [lax.empty(a.shape, a.dtype) for a in srcs], [sliced] * len(srcs), carry,
                            name=f"grad_{tag}_start")

    def exchange_wait(names, started, after, tag):
        sem_s, sem_r, srcs, lands_, _ = started
        lands_ = _split_wait(sem_s, sem_r, srcs, lands_, [sliced] * len(srcs), after, name=f"grad_{tag}_wait")
        return {n: with_own(ld, lax.dynamic_index_in_dim(s_, me, 0, keepdims=False)) for n, ld, s_ in zip(names, lands_, srcs)}

    x1_started = exchange_start(BIG, grads[1], dh, "l1")
    pending = {}

    def early_l0(g, carry):
        pending["rest"] = exchange_start(rest, g, carry, "l0_rest")
        return pending["rest"][4]

    def late_bwd_l0(g, carry):
        pending["w_in"] = exchange_start(("w_in",), g, carry, "l0_w_in")
        return pending["w_in"][4]

    grad_x, grads[0] = _layer_bwd(x1_started[4], p0, sv0, lay, "l0", early=early_l0, late=late_bwd_l0)

    out = {"grad": {}, "delta": {}, "new_m": {}, "new_v": {}}
    parts1 = exchange_wait(BIG, x1_started, grad_x, "l1")
    res1 = {n: _sum_adamw(parts1[n], w[n], m[n], v[n], 1, None, name=f"sum_adamw_{n}_l1") for n in BIG}
    parts0 = exchange_wait(rest, pending["rest"], res1["w_in"][0], "l0_rest")
    res0 = {n: _sum_adamw(parts0[n], w[n], m[n], v[n], 0, res1[n], name=f"sum_adamw_{n}_l0") for n in rest}
    parts0 = exchange_wait(("w_in",), pending["w_in"], res0["w_down"][0], "l0_w_in")
    res0["w_in"] = _sum_adamw(parts0["w_in"], w["w_in"], m["w_in"], v["w_in"], 0, res1["w_in"], name="sum_adamw_w_in_l0")
    for n in BIG:
        for key, a in zip(("grad", "delta", "new_m", "new_v"), res0[n]):
            out[key][n] = a

    gfull = {n: jnp.stack([grads[l][n] for l in range(DEPTH)]) for n in SMALL if n != "norm_final"}
    gfull["norm_final"] = g_norm_final
    small_send = _pack([gfull[n] for n in SMALL] + [loss.reshape(1)], F32)
    small_recv = _all_gather([small_send], name="gather_small_grads", after=res0["w_in"][0])[0]
    small_sum = _sum_parts(small_recv, name="sum_small")
    small_full = _unpack(small_sum, [gfull[n].shape for n in SMALL] + [(1,)])
    loss_total = small_full[-1][0]
    gsmall = {}
    for n, a in zip(SMALL, small_full[:-1]):
        if n in SHARD_AXIS:
            a = lax.dynamic_index_in_dim(_to_shards(a, SHARD_AXIS[n]), me, axis=0, keepdims=False)
        gsmall[n] = a
    small_shapes = [w[n].shape for n in SMALL]
    ws, gs, ms, vs = (_pack([t[n] for n in SMALL], F32) for t in (w, gsmall, m, v))
    ds, m1s, v1s = _adamw(ws, gs, ms, vs, name="adamw_small")
    for n in SMALL:
        out["grad"][n] = gsmall[n]
    for key, packed in (("delta", ds), ("new_m", m1s), ("new_v", v1s)):
        for n, a in zip(SMALL, _unpack(packed, small_shapes)):
            out[key][n] = a
    return loss_total, grad_x, out


def kernel(x, norm_mix, w_in, dn_conv_w, dn_a_log, dn_dt_bias, dn_norm_w, ssm_conv_w, ssm_conv_b, ssm_a_log, ssm_dt_bias, ssm_d, ssm_norm_w, w_branch, w_out, norm_mlp, w_up, w_down, norm_final, loss_target, m_norm_mix, m_w_in, m_dn_conv_w, m_dn_a_log, m_dn_dt_bias, m_dn_norm_w, m_ssm_conv_w, m_ssm_conv_b, m_ssm_a_log, m_ssm_dt_bias, m_ssm_d, m_ssm_norm_w, m_w_branch, m_w_out, m_norm_mlp, m_w_up, m_w_down, m_norm_final, v_norm_mix, v_w_in, v_dn_conv_w, v_dn_a_log, v_dn_dt_bias, v_dn_norm_w, v_ssm_conv_w, v_ssm_conv_b, v_ssm_a_log, v_ssm_dt_bias, v_ssm_d, v_ssm_norm_w, v_w_branch, v_w_out, v_norm_mlp, v_w_up, v_w_down, v_norm_final):
    w = dict(norm_mix=norm_mix, w_in=w_in, dn_conv_w=dn_conv_w, dn_a_log=dn_a_log, dn_dt_bias=dn_dt_bias, dn_norm_w=dn_norm_w,
             ssm_conv_w=ssm_conv_w, ssm_conv_b=ssm_conv_b, ssm_a_log=ssm_a_log, ssm_dt_bias=ssm_dt_bias, ssm_d=ssm_d,
             ssm_norm_w=ssm_norm_w, w_branch=w_branch, w_out=w_out, norm_mlp=norm_mlp, w_up=w_up, w_down=w_down,
             norm_final=norm_final)
    m = dict(norm_mix=m_norm_mix, w_in=m_w_in, dn_conv_w=m_dn_conv_w, dn_a_log=m_dn_a_log, dn_dt_bias=m_dn_dt_bias,
             dn_norm_w=m_dn_norm_w, ssm_conv_w=m_ssm_conv_w, ssm_conv_b=m_ssm_conv_b, ssm_a_log=m_ssm_a_log,
             ssm_dt_bias=m_ssm_dt_bias, ssm_d=m_ssm_d, ssm_norm_w=m_ssm_norm_w, w_branch=m_w_branch, w_out=m_w_out,
             norm_mlp=m_norm_mlp, w_up=m_w_up, w_down=m_w_down, norm_final=m_norm_final)
    v = dict(norm_mix=v_norm_mix, w_in=v_w_in, dn_conv_w=v_dn_conv_w, dn_a_log=v_dn_a_log, dn_dt_bias=v_dn_dt_bias,
             dn_norm_w=v_dn_norm_w, ssm_conv_w=v_ssm_conv_w, ssm_conv_b=v_ssm_conv_b, ssm_a_log=v_ssm_a_log,
             ssm_dt_bias=v_ssm_dt_bias, ssm_d=v_ssm_d, ssm_norm_w=v_ssm_norm_w, w_branch=v_w_branch, w_out=v_w_out,
             norm_mlp=v_norm_mlp, w_up=v_w_up, w_down=v_w_down, norm_final=v_norm_final)
    loss, grad_x, out = _step(w, m, v, x[0], loss_target[0])
    return (loss, grad_x[None], *[out["grad"][n] for n in WEIGHTS], *[out["delta"][n] for n in WEIGHTS],
            *[out["new_m"][n] for n in WEIGHTS], *[out["new_v"][n] for n in WEIGHTS])
```

```python
import functools
import math

import jax
import jax.numpy as jnp
from jax import lax
from jax.experimental import pallas as pl
from jax.experimental.pallas import tpu as pltpu

F32 = jnp.float32
BF16 = jnp.bfloat16
MXU_DTYPE = BF16
HIGHEST = lax.Precision.HIGHEST

N_DEV = 8
DEPTH = 2
EPS = 1e-6
CONV_K = 4
DN_HEAD_DIM = 128
SB_HEAD_DIM = 64
SSM_HEAD_DIM = 64
SSM_STATE = 128
SSM_GROUPS = 4
CHUNK = 64
SB_BLOCK = 128
LANES = 128
ADAM_LR, ADAM_B1, ADAM_B2, ADAM_EPS, ADAM_WD, ADAM_STEP = 0.001, 0.9, 0.999, 1e-08, 0.01, 10
NEG_BIG = -1e30
DN_HEADS_PER_STEP = 8
SSD_GROUPS_PER_STEP = 1
SB_UNROLL = 4
SB_SPLIT = 2
CHUNK_PREC = lax.Precision.HIGH

ARB = "arbitrary"


def _cparams(n_axes):
    return pltpu.CompilerParams(dimension_semantics=(ARB,) * n_axes)


def _softplus(x):
    return jnp.maximum(x, 0.0) + jnp.log1p(jnp.exp(-jnp.abs(x)))


def _sigmoid(x):
    return jax.nn.sigmoid(x)


def _silu(x):
    return x * _sigmoid(x)


def _silu_and_grad(x):
    s = _sigmoid(x)
    return x * s, s * (1.0 + x * (1.0 - s))


def _dot(a, b, dims, prec=None):
    return lax.dot_general(a, b, (dims, ((), ())), precision=prec, preferred_element_type=F32)


NN = ((1,), (0,))
NT = ((1,), (1,))
TN = ((0,), (0,))


def _hdot(a, b, dims=NN):
    return _dot(a, b, dims, CHUNK_PREC)


def _bdot(a, b, dims=NN):
    return _dot(a.astype(MXU_DTYPE), b.astype(MXU_DTYPE), dims)


def _split_dot(a, m_bf16, nsplit=3):
    out = None
    rem = a
    for _ in range(nsplit):
        piece = rem.astype(BF16)
        rem = rem - piece.astype(F32)
        term = _dot(piece, m_bf16, NN)
        out = term if out is None else out + term
    return out


def _pick(n, pref):
    for t in pref:
        if n % t == 0:
            return t
    return n


def _matmul(a, b, *, ta=False, tb=False, name, epilogue=None, extras=(), out_dtypes=(F32,), col_shards=1,
            tm=None, tn=None, tk=None):
    m, k = (a.shape[1], a.shape[0]) if ta else a.shape
    k2, n = (b.shape[1], b.shape[0]) if tb else b.shape
    assert k == k2, (a.shape, b.shape, ta, tb)
    ncs = n // col_shards
    tm = tm or _pick(m, (1920, 1024, 512, 256, 128))
    tn = tn or _pick(ncs, (1920, 1024, 640, 512, 384, 256, 128))
    tk = tk or _pick(k, (1920, 1024, 640, 512, 256, 128))
    nk = k // tk
    a_spec = pl.BlockSpec((tk, tm), lambda i, j, kk: (kk, i)) if ta else pl.BlockSpec((tm, tk), lambda i, j, kk: (i, kk))
    b_spec = pl.BlockSpec((tn, tk), lambda i, j, kk: (j, kk)) if tb else pl.BlockSpec((tk, tn), lambda i, j, kk: (kk, j))
    e_spec = pl.BlockSpec((tm, tn), lambda i, j, kk: (i, j))
    if col_shards == 1:
        o_spec, o_shape = e_spec, (m, n)
    else:
        per = ncs // tn
        o_spec, o_shape = pl.BlockSpec((None, tm, tn), lambda i, j, kk: (j // per, i, j % per)), (col_shards, m, ncs)
    dims = (((0,) if ta else (1,)), ((1,) if tb else (0,)))
    n_extra = len(extras)
    n_out = len(out_dtypes)

    def body(*refs):
        a_ref, b_ref = refs[0], refs[1]
        extra_refs = refs[2:2 + n_extra]
        out_refs = refs[2 + n_extra:2 + n_extra + n_out]
        acc_ref = refs[-1]
        kk = pl.program_id(2)

        @pl.when(kk == 0)
        def _():
            acc_ref[...] = jnp.zeros_like(acc_ref)

        acc_ref[...] += _dot(a_ref[...].astype(MXU_DTYPE), b_ref[...].astype(MXU_DTYPE), dims)

        @pl.when(kk == nk - 1)
        def _():
            acc = acc_ref[...]
            outs = (acc,) if epilogue is None else epilogue(acc, *[r[...] for r in extra_refs])
            for o_ref, o in zip(out_refs, outs):
                o_ref[...] = o.astype(o_ref.dtype)

    outs = pl.pallas_call(
        body,
        grid=(m // tm, n // tn, nk),
        in_specs=[a_spec, b_spec] + [e_spec] * n_extra,
        out_specs=[o_spec] * n_out,
        out_shape=[jax.ShapeDtypeStruct(o_shape, dt) for dt in out_dtypes],
        scratch_shapes=[pltpu.VMEM((tm, tn), F32)],
        compiler_params=pltpu.CompilerParams(dimension_semantics=("parallel", "parallel", ARB)),
        name=name,
    )(a, b, *extras)
    return outs[0] if n_out == 1 else tuple(outs)


def _rms_fwd(x, w, *, name, tm=256):
    s, d = x.shape
    out_dtype = MXU_DTYPE

    def body(x_ref, w_ref, o_ref):
        xv = x_ref[...]
        r = lax.rsqrt(jnp.mean(xv * xv, axis=-1, keepdims=True) + EPS)
        o_ref[...] = (xv * r * w_ref[...]).astype(o_ref.dtype)

    return pl.pallas_call(
        body, grid=(s // tm,),
        in_specs=[pl.BlockSpec((tm, d), lambda i: (i, 0)), pl.BlockSpec((1, d), lambda i: (0, 0))],
        out_specs=pl.BlockSpec((tm, d), lambda i: (i, 0)),
        out_shape=jax.ShapeDtypeStruct((s, d), out_dtype),
        compiler_params=_cparams(1), name=name,
    )(x, w.reshape(1, d))


def _rms_bwd(x, w, dh, dres, *, name, tm=256):
    s, d = x.shape

    def body(x_ref, w_ref, dh_ref, dres_ref, dx_ref, dw_ref):
        xv = x_ref[...]
        r = lax.rsqrt(jnp.mean(xv * xv, axis=-1, keepdims=True) + EPS)
        xh = xv * r
        dhv = dh_ref[...].astype(F32)
        dxn = dhv * w_ref[...]
        dx = r * (dxn - xh * jnp.mean(dxn * xh, axis=-1, keepdims=True))
        dx_ref[...] = dres_ref[...] + dx

        @pl.when(pl.program_id(0) == 0)
        def _():
            dw_ref[...] = jnp.zeros_like(dw_ref)

        dw_ref[...] += jnp.sum(dhv * xh, axis=0, keepdims=True)

    dx, dw = pl.pallas_call(
        body, grid=(s // tm,),
        in_specs=[pl.BlockSpec((tm, d), lambda i: (i, 0)), pl.BlockSpec((1, d), lambda i: (0, 0)),
                  pl.BlockSpec((tm, d), lambda i: (i, 0)), pl.BlockSpec((tm, d), lambda i: (i, 0))],
        out_specs=[pl.BlockSpec((tm, d), lambda i: (i, 0)), pl.BlockSpec((1, d), lambda i: (0, 0))],
        out_shape=[jax.ShapeDtypeStruct((s, d), F32), jax.ShapeDtypeStruct((1, d), F32)],
        compiler_params=_cparams(1), name=name,
    )(x, w.reshape(1, d), dh, dres)
    return dx, dw.reshape(d)


def _final_loss(x, w, target, *, name, tm=256):
    s, d = x.shape

    def body(x_ref, w_ref, t_ref, loss_ref, dx_ref, dw_ref):
        xv = x_ref[...]
        r = lax.rsqrt(jnp.mean(xv * xv, axis=-1, keepdims=True) + EPS)
        xh = xv * r
        err = xh * w_ref[...] - t_ref[...]
        dy = err * (1.0 / d)
        dxn = dy * w_ref[...]
        dx_ref[...] = r * (dxn - xh * jnp.mean(dxn * xh, axis=-1, keepdims=True))

        @pl.when(pl.program_id(0) == 0)
        def _():
            dw_ref[...] = jnp.zeros_like(dw_ref)
            loss_ref[...] = jnp.zeros_like(loss_ref)

        dw_ref[...] += jnp.sum(dy * xh, axis=0, keepdims=True)
        row = jnp.sum(err * err, axis=1, keepdims=True) * (0.5 / d)
        loss_ref[...] += jnp.sum(row, axis=0, keepdims=True)

    loss, dx, dw = pl.pallas_call(
        body, grid=(s // tm,),
        in_specs=[pl.BlockSpec((tm, d), lambda i: (i, 0)), pl.BlockSpec((1, d), lambda i: (0, 0)),
                  pl.BlockSpec((tm, d), lambda i: (i, 0))],
        out_specs=[pl.BlockSpec((1, 1), lambda i: (0, 0)), pl.BlockSpec((tm, d), lambda i: (i, 0)),
                   pl.BlockSpec((1, d), lambda i: (0, 0))],
        out_shape=[jax.ShapeDtypeStruct((1, 1), F32), jax.ShapeDtypeStruct((s, d), F32), jax.ShapeDtypeStruct((1, d), F32)],
        compiler_params=_cparams(1), name=name,
    )(x, w.reshape(1, d), target)
    return loss[0, 0], dx, dw.reshape(d)


def _shift_down(x, sh, t_idx):
    return jnp.where(t_idx >= sh, pltpu.roll(x, sh, 0), 0.0)


def _shift_up(x, sh, t_idx, s):
    return jnp.where(t_idx < s - sh, pltpu.roll(x, s - sh, 0), 0.0)


def _conv_pre(x, w_rows, b, t_idx):
    c = w_rows[CONV_K - 1] * x + b
    for sh in range(1, CONV_K):
        c = c + w_rows[CONV_K - 1 - sh] * _shift_down(x, sh, t_idx)
    return c


def _conv_fwd(src, col0, w, b, n_l2, *, name):
    s = src.shape[0]
    c_tot = w.shape[1]
    nblk = c_tot // LANES

    def body(x_ref, w_ref, b_ref, o_ref):
        j = pl.program_id(0)
        t_idx = lax.broadcasted_iota(jnp.int32, (s, LANES), 0)
        w_rows = [w_ref[kk:kk + 1, :] for kk in range(CONV_K)]
        y = _silu(_conv_pre(x_ref[...], w_rows, b_ref[...], t_idx))
        if n_l2 > 0:
            yn = y * lax.rsqrt(jnp.sum(y * y, axis=1, keepdims=True) + EPS)
            y = jnp.where(j < n_l2, yn, y)
        o_ref[...] = y

    return pl.pallas_call(
        body, grid=(nblk,),
        in_specs=[pl.BlockSpec((s, LANES), lambda j: (0, col0 + j)), pl.BlockSpec((CONV_K, LANES), lambda j: (0, j)),
                  pl.BlockSpec((1, LANES), lambda j: (0, j))],
        out_specs=pl.BlockSpec((s, LANES), lambda j: (0, j)),
        out_shape=jax.ShapeDtypeStruct((s, c_tot), F32),
        compiler_params=_cparams(1), name=name,
    )(src, w, b)


def _conv_bwd(src, col0, w, b, n_l2, dout, into, *, name):
    s = src.shape[0]
    c_tot = w.shape[1]
    nblk = c_tot // LANES

    def body(x_ref, w_ref, b_ref, do_ref, into_ref, dx_ref, dw_ref, db_ref):
        j = pl.program_id(0)
        t_idx = lax.broadcasted_iota(jnp.int32, (s, LANES), 0)
        xv = x_ref[...]
        w_rows = [w_ref[kk:kk + 1, :] for kk in range(CONV_K)]
        c = _conv_pre(xv, w_rows, b_ref[...], t_idx)
        dy = do_ref[...]
        y, y_grad = _silu_and_grad(c)
        if n_l2 > 0:
            r = lax.rsqrt(jnp.sum(y * y, axis=1, keepdims=True) + EPS)
            dyn = r * dy - y * (r * r * r) * jnp.sum(dy * y, axis=1, keepdims=True)
            dy = jnp.where(j < n_l2, dyn, dy)
        dc = dy * y_grad
        dx = w_rows[CONV_K - 1] * dc
        rows = [None] * CONV_K
        rows[CONV_K - 1] = jnp.sum(dc * xv, axis=0, keepdims=True)
        for sh in range(1, CONV_K):
            dx = dx + w_rows[CONV_K - 1 - sh] * _shift_up(dc, sh, t_idx, s)
            rows[CONV_K - 1 - sh] = jnp.sum(dc * _shift_down(xv, sh, t_idx), axis=0, keepdims=True)
        dx_ref[...] = dx.astype(dx_ref.dtype)
        for kk in range(CONV_K):
            dw_ref[kk:kk + 1, :] = rows[kk]
        db_ref[...] = jnp.sum(dc, axis=0, keepdims=True)

    return pl.pallas_call(
        body, grid=(nblk,),
        in_specs=[pl.BlockSpec((s, LANES), lambda j: (0, col0 + j)), pl.BlockSpec((CONV_K, LANES), lambda j: (0, j)),
                  pl.BlockSpec((1, LANES), lambda j: (0, j)), pl.BlockSpec((s, LANES), lambda j: (0, j)), ANY],
        out_specs=[pl.BlockSpec((s, LANES), lambda j: (0, col0 + j)), pl.BlockSpec((CONV_K, LANES), lambda j: (0, j)),
                   pl.BlockSpec((1, LANES), lambda j: (0, j))],
        out_shape=[jax.ShapeDtypeStruct(into.shape, into.dtype), jax.ShapeDtypeStruct((CONV_K, c_tot), F32),
                   jax.ShapeDtypeStruct((1, c_tot), F32)],
        input_output_aliases={4: 0},
        compiler_params=_cparams(1), name=name,
    )(src, w, b, dout, into)


def _chunk_masks(c):
    ii = lax.broadcasted_iota(jnp.int32, (c, c), 0)
    jj = lax.broadcasted_iota(jnp.int32, (c, c), 1)
    return ii, jj


def _row_to_col(row, eye):
    return jnp.sum(jnp.where(eye, row, 0.0), axis=1, keepdims=True)


def _each(f, *lists):
    return [f(*xs) for xs in zip(*lists)]


@jax.custom_vjp
def _nilpotent_inverse(nmats):
    c = nmats[0].shape[0]
    ii, jj = _chunk_masks(c)
    xinv = _each(lambda n: jnp.where(ii == jj, 1.0, 0.0) + n, nmats)
    pw = nmats
    for _ in range(int(math.log2(c)) - 1):
        pw = _each(lambda p: _dot(p, p, NN, HIGHEST), pw)
        xinv = _each(lambda x, p: x + _dot(x, p, NN, HIGHEST), xinv, pw)
    return xinv


def _nilpotent_inverse_fwd(nmats):
    xinv = _nilpotent_inverse(nmats)
    return xinv, xinv


def _nilpotent_inverse_bwd(xinv, cts):
    left = _each(lambda x, ct: _dot(x, ct, TN, HIGHEST), xinv, cts)
    return (_each(lambda l_, x: _dot(l_, x, NT, HIGHEST), left, xinv),)


_nilpotent_inverse.defvjp(_nilpotent_inverse_fwd, _nilpotent_inverse_bwd)


@jax.custom_vjp
def _saved_inverse(nmats, saved):
    return saved


def _saved_inverse_fwd(nmats, saved):
    return saved, saved


def _saved_inverse_bwd(xinv, cts):
    return _nilpotent_inverse_bwd(xinv, cts) + (_each(jnp.zeros_like, xinv),)


_saved_inverse.defvjp(_saved_inverse_fwd, _saved_inverse_bwd)


def _dn_chunk(q, k, v, a_row, b_row, alog, dtb, s0, saved_inverse=None):
    c = q[0].shape[0]
    ii, jj = _chunk_masks(c)
    causal, strict, eye = ii >= jj, ii > jj, ii == jj
    g_row = _each(lambda al, a, dt: -jnp.exp(al) * _softplus(a + dt), alog, a_row, dtb)
    beta_col = _each(lambda b: _row_to_col(_sigmoid(b), eye), b_row)
    g_col = _each(lambda g: _row_to_col(g, eye), g_row)
    gc_col = _each(lambda g: jnp.sum(jnp.where(causal, g, 0.0), axis=1, keepdims=True), g_row)
    gc_row = _each(lambda g: jnp.sum(jnp.where(jj >= ii, g, 0.0), axis=0, keepdims=True), g_col)
    decay = _each(lambda gc, gr: jnp.exp(jnp.where(causal, gc - gr, NEG_BIG)), gc_col, gc_row)
    kb = _each(jnp.multiply, k, beta_col)
    vb = _each(jnp.multiply, v, beta_col)
    nmat = _each(lambda kb_, k_, dc: -jnp.where(strict, _dot(kb_, k_, NT, HIGHEST) * dc, 0.0), kb, k, decay)
    xinv = _nilpotent_inverse(nmat) if saved_inverse is None else _saved_inverse(nmat, saved_inverse)
    egc = _each(jnp.exp, gc_col)
    u = _each(lambda x, vb_: _dot(x, vb_, NN, HIGHEST), xinv, vb)
    w = _each(lambda x, kb_, e: _dot(x, kb_ * e, NN, HIGHEST), xinv, kb, egc)
    qs = _each(lambda q_: q_ * (q_.shape[1] ** -0.5), q)
    attn = _each(lambda q_, k_, dc: _hdot(q_, k_, NT) * dc, qs, k, decay)
    gl = _each(lambda g: jnp.sum(g, axis=1, keepdims=True), g_row)
    kd = _each(lambda k_, gl_, gc: k_ * jnp.exp(gl_ - gc), k, gl, gc_col)
    v_new = _each(lambda u_, w_, s: u_ - _hdot(w_, s), u, w, s0)
    o = _each(lambda q_, e, s, at, vn: _hdot(q_ * e, s) + _hdot(at, vn), qs, egc, s0, attn, v_new)
    s1 = _each(lambda s, gl_, kd_, vn: s * jnp.exp(gl_) + _hdot(kd_, vn, TN), s0, gl, kd, v_new)
    return (o, s1), xinv


def _dn_specs(nh, nc, hb, rev):
    n_of = (lambda n: nc - 1 - n) if rev else (lambda n: n)
    ng = nh // hb
    qkv = [pl.BlockSpec((CHUNK, hb * DN_HEAD_DIM), (lambda h, n, o=o: (n_of(n), o * ng + h))) for o in range(3)]
    row = pl.BlockSpec((hb, None, 1, CHUNK), lambda h, n: (h, n_of(n), 0, 0))
    scal = pl.BlockSpec((hb, 1, 1), lambda h, n: (h, 0, 0))
    o_spec = pl.BlockSpec((CHUNK, hb * DN_HEAD_DIM), lambda h, n: (n_of(n), h))
    st = pl.BlockSpec((hb, None, DN_HEAD_DIM, DN_HEAD_DIM), lambda h, n: (h, n_of(n), 0, 0))
    inv = pl.BlockSpec((hb, None, CHUNK, CHUNK), lambda h, n: (h, n_of(n), 0, 0))
    return qkv, row, scal, o_spec, st, inv


def _dn_fwd(qkv, a_rows, b_rows, alog, dtb, *, name):
    s = qkv.shape[0]
    nh, nc = a_rows.shape[0], a_rows.shape[1]
    hb = min(DN_HEADS_PER_STEP, nh)
    qkv_specs, row, scal, o_spec, st, inv = _dn_specs(nh, nc, hb, False)
    hd = DN_HEAD_DIM

    def body(q_ref, k_ref, v_ref, a_ref, b_ref, al_ref, dt_ref, o_ref, st_ref, inv_ref, state):
        @pl.when(pl.program_id(1) == 0)
        def _():
            state[...] = jnp.zeros_like(state)

        cols = [slice(h * hd, (h + 1) * hd) for h in range(hb)]
        s0 = [state[h] for h in range(hb)]
        for h in range(hb):
            st_ref[h] = s0[h]
        (o, s1), xinv = _dn_chunk(
            [q_ref[:, cl] for cl in cols], [k_ref[:, cl] for cl in cols], [v_ref[:, cl] for cl in cols],
            [a_ref[h] for h in range(hb)], [b_ref[h] for h in range(hb)],
            [al_ref[h] for h in range(hb)], [dt_ref[h] for h in range(hb)], s0)
        for h in range(hb):
            o_ref[:, cols[h]] = o[h]
            inv_ref[h] = xinv[h]
            state[h] = s1[h]

    return pl.pallas_call(
        body, grid=(nh // hb, nc),
        in_specs=qkv_specs + [row, row, scal, scal],
        out_specs=[o_spec, st, inv],
        out_shape=[jax.ShapeDtypeStruct((s, nh * hd), F32), jax.ShapeDtypeStruct((nh, nc, hd, hd), F32),
                   jax.ShapeDtypeStruct((nh, nc, CHUNK, CHUNK), F32)],
        scratch_shapes=[pltpu.VMEM((hb, hd, hd), F32)],
        compiler_params=_cparams(2), name=name,
    )(qkv, qkv, qkv, a_rows, b_rows, alog, dtb)


def _dn_bwd(qkv, a_rows, b_rows, alog, dtb, states, inverses, do, *, name):
    s = qkv.shape[0]
    nh, nc = a_rows.shape[0], a_rows.shape[1]
    hb = min(DN_HEADS_PER_STEP, nh)
    qkv_specs, row, scal, o_spec, st, inv = _dn_specs(nh, nc, hb, True)
    hd = DN_HEAD_DIM

    assert hb == nh, "dq | dk | dv are written as one [S, 3W] array: all heads in one grid step"
    w = nh * hd

    def body(q_ref, k_ref, v_ref, a_ref, b_ref, al_ref, dt_ref, st_ref, inv_ref, do_ref,
             dqkv_ref, da_ref, db_ref, dal_ref, ddt_ref, dstate):
        @pl.when(pl.program_id(1) == 0)
        def _():
            dstate[...] = jnp.zeros_like(dstate)
            dal_ref[...] = jnp.zeros_like(dal_ref)
            ddt_ref[...] = jnp.zeros_like(ddt_ref)

        cols = [slice(h * hd, (h + 1) * hd) for h in range(hb)]
        heads = range(hb)
        args = ([q_ref[:, cl] for cl in cols], [k_ref[:, cl] for cl in cols], [v_ref[:, cl] for cl in cols],
                [a_ref[h] for h in heads], [b_ref[h] for h in heads], [al_ref[h] for h in heads],
                [dt_ref[h] for h in heads], [st_ref[h] for h in heads])
        saved = [inv_ref[h] for h in heads]
        _, vjp, _ = jax.vjp(lambda *a: _dn_chunk(*a, saved_inverse=saved), *args, has_aux=True)
        dq, dk, dv, da, db, dal, ddt, ds0 = vjp(([do_ref[:, cl] for cl in cols], [dstate[h] for h in heads]))
        for h in heads:
            dqkv_ref[:, h * hd:(h + 1) * hd] = dq[h]
            dqkv_ref[:, w + h * hd:w + (h + 1) * hd] = dk[h]
            dqkv_ref[:, 2 * w + h * hd:2 * w + (h + 1) * hd] = dv[h]
            da_ref[h] = da[h]
            db_ref[h] = db[h]
            dal_ref[h] += dal[h]
            ddt_ref[h] += ddt[h]
            dstate[h] = ds0[h]

    n_of = lambda n: nc - 1 - n
    outs = pl.pallas_call(
        body, grid=(nh // hb, nc),
        in_specs=qkv_specs + [row, row, scal, scal, st, inv, o_spec],
        out_specs=[pl.BlockSpec((CHUNK, 3 * w), lambda h, n: (n_of(n), 0)), row, row, scal, scal],
        out_shape=[jax.ShapeDtypeStruct((s, 3 * w), F32)]
        + [jax.ShapeDtypeStruct(a_rows.shape, F32)] * 2 + [jax.ShapeDtypeStruct((nh, 1, 1), F32)] * 2,
        scratch_shapes=[pltpu.VMEM((hb, hd, hd), F32)],
        compiler_params=_cparams(2), name=name,
    )(qkv, qkv, qkv, a_rows, b_rows, alog, dtb, states, inverses, do)
    return outs


def _dn_post_fwd(o, src, gate_col0, nw, *, name, tm=256):
    s, w = o.shape
    hd = DN_HEAD_DIM
    gc = gate_col0 * LANES // w

    def body(o_ref, g_ref, w_ref, y_ref):
        for h in range(w // hd):
            cols = slice(h * hd, (h + 1) * hd)
            ov = o_ref[:, cols]
            r = lax.rsqrt(jnp.mean(ov * ov, axis=-1, keepdims=True) + EPS)
            y_ref[:, cols] = (ov * r * w_ref[...] * _silu(g_ref[:, cols])).astype(y_ref.dtype)

    blk = pl.BlockSpec((tm, w), lambda i: (i, 0))
    return pl.pallas_call(
        body, grid=(s // tm,),
        in_specs=[blk, pl.BlockSpec((tm, w), lambda i: (i, gc)), pl.BlockSpec((1, hd), lambda i: (0, 0))],
        out_specs=blk, out_shape=jax.ShapeDtypeStruct((s, w), MXU_DTYPE),
        compiler_params=_cparams(1), name=name,
    )(o, src, nw.reshape(1, hd))


def _dn_post_bwd(o, src, gate_col0, nw, dy, into, *, name, tm=256):
    s, w = o.shape
    hd = DN_HEAD_DIM
    gc = gate_col0 * LANES // w

    def body(o_ref, g_ref, w_ref, dy_ref, into_ref, do_ref, dg_ref, dw_ref):
        @pl.when(pl.program_id(0) == 0)
        def _():
            dw_ref[...] = jnp.zeros_like(dw_ref)

        dw = jnp.zeros((1, hd), F32)
        for h in range(w // hd):
            cols = slice(h * hd, (h + 1) * hd)
            ov, gv, dyv = o_ref[:, cols], g_ref[:, cols], dy_ref[:, cols]
            r = lax.rsqrt(jnp.mean(ov * ov, axis=-1, keepdims=True) + EPS)
            oh = ov * r
            sg, sg_grad = _silu_and_grad(gv)
            dn = dyv * sg
            dg_ref[:, cols] = (dyv * (oh * w_ref[...]) * sg_grad).astype(dg_ref.dtype)
            don = dn * w_ref[...]
            do_ref[:, cols] = r * (don - oh * jnp.mean(don * oh, axis=-1, keepdims=True))
            dw = dw + jnp.sum(dn * oh, axis=0, keepdims=True)
        dw_ref[...] += dw

    blk = pl.BlockSpec((tm, w), lambda i: (i, 0))
    wspec = pl.BlockSpec((1, hd), lambda i: (0, 0))
    gate_blk = pl.BlockSpec((tm, w), lambda i: (i, gc))
    do, dg, dw = pl.pallas_call(
        body, grid=(s // tm,),
        in_specs=[blk, gate_blk, wspec, blk, ANY],
        out_specs=[blk, gate_blk, wspec],
        out_shape=[jax.ShapeDtypeStruct((s, w), F32), jax.ShapeDtypeStruct(into.shape, into.dtype),
                   jax.ShapeDtypeStruct((1, hd), F32)],
        input_output_aliases={4: 1},
        compiler_params=_cparams(1), name=name,
    )(o, src, nw.reshape(1, hd), dy, into)
    return do, dg, dw.reshape(hd)


def _sb_consts():
    r2 = lax.broadcasted_iota(jnp.int32, (2 * SB_BLOCK, SB_BLOCK), 0)
    c2 = lax.broadcasted_iota(jnp.int32, (2 * SB_BLOCK, SB_BLOCK), 1)
    r = lax.broadcasted_iota(jnp.int32, (SB_BLOCK, SB_BLOCK), 0)
    c = lax.broadcasted_iota(jnp.int32, (SB_BLOCK, SB_BLOCK), 1)
    lm0 = c < SB_HEAD_DIM
    m_gt = jnp.where(r > c, 1.0, 0.0).astype(BF16)
    m_lt = jnp.where(r < c, 1.0, 0.0).astype(BF16)
    return r2, c2, lm0, m_gt, m_lt


def _sb_stack(x, lm0):
    return jnp.concatenate([jnp.where(lm0, x, 0.0), jnp.where(lm0, 0.0, x)], axis=0)


def _sb_unstack(x2, lm0):
    return jnp.where(lm0, x2[:SB_BLOCK], x2[SB_BLOCK:])


def _sb_fwd(src, col0, width, *, name):
    s = src.shape[0]
    nq = s // SB_BLOCK
    npair = width // LANES
    scale = SB_HEAD_DIM ** -0.5
    nu = math.gcd(SB_UNROLL, nq)

    def body(q_ref, k_ref, v_ref, o_ref, w_hbm, stage, sems):
        p, i = pl.program_id(0), pl.program_id(1)
        r2, c2, lm0, m_gt, _ = _sb_consts()
        t_glob = i * SB_BLOCK + (r2 & (SB_BLOCK - 1))
        q2 = (_sb_stack(q_ref[...], lm0) * scale).astype(MXU_DTYPE)

        def save(slot, u, j):
            return pltpu.make_async_copy(stage.at[slot, u], w_hbm.at[p, i, j], sems.at[slot, u])

        def group(base, carry, masked, slot, reused):
            o2, rsum = carry
            js = [base + nu - 1 - u for u in range(nu)]
            offs = [pl.multiple_of(j * SB_BLOCK, SB_BLOCK) for j in js]
            zs = [_dot(q2, k_ref[pl.ds(off, SB_BLOCK), :].astype(MXU_DTYPE), NT) for off in offs]
            ts = [jnp.log(1.0 + jnp.exp(-jnp.abs(z))) for z in zs]
            lks = [-(jnp.maximum(z, 0.0) + t) for z, t in zip(zs, ts)]
            if masked:
                masks = [(j * SB_BLOCK + c2) < t_glob for j in js]
                lks = [jnp.where(mk, lk, 0.0) for mk, lk in zip(masks, lks)]
            sufs = [_split_dot(lk, m_gt, SB_SPLIT) for lk in lks]
            rs = [rsum]
            for lk in lks:
                rs.append(rs[-1] + jnp.sum(lk, axis=1, keepdims=True))
            wgts = [jnp.exp((jnp.minimum(z, 0.0) - t) + r_ + sf) for z, t, r_, sf in zip(zs, ts, rs, sufs)]
            if masked:
                wgts = [jnp.where(mk, wg, 0.0) for mk, wg in zip(masks, wgts)]
            wbs = [wg.astype(MXU_DTYPE) for wg in wgts]
            if reused is not None:
                @pl.when(reused)
                def _():
                    for u in range(nu):
                        save(slot, u, 0).wait()
            for u, (j, wb) in enumerate(zip(js, wbs)):
                stage[slot, u] = wb
                save(slot, u, j).start()
            for off, wb in zip(offs, wbs):
                o2 = o2 + _dot(wb, v_ref[pl.ds(off, SB_BLOCK), :].astype(MXU_DTYPE), NN)
            return o2, rs[-1]

        top0 = (i // nu) * nu
        last = i // nu
        carry = group(top0, (jnp.zeros((2 * SB_BLOCK, LANES), F32), jnp.zeros((2 * SB_BLOCK, 1), F32)), True, 0, None)
        o2, _ = lax.fori_loop(1, last + 1, lambda g, cr: group(top0 - nu * g, cr, False, g % 2, g >= 2), carry)
        o_ref[...] = _sb_unstack(o2, lm0)
        for u in range(nu):
            save(last % 2, u, 0).wait()

        @pl.when(last >= 1)
        def _():
            for u in range(nu):
                save((last - 1) % 2, u, 0).wait()

    blk = pl.BlockSpec((SB_BLOCK, LANES), lambda p, i: (i, p))
    return pl.pallas_call(
        body, grid=(npair, nq),
        in_specs=[pl.BlockSpec((SB_BLOCK, LANES), lambda p, i: (i, col0 + p)),
                  pl.BlockSpec((s, LANES), lambda p, i: (0, col0 + npair + p)),
                  pl.BlockSpec((s, LANES), lambda p, i: (0, col0 + 2 * npair + p))],
        out_specs=[blk, ANY],
        out_shape=[jax.ShapeDtypeStruct((s, width), F32),
                   jax.ShapeDtypeStruct((npair, nq, nq, 2 * SB_BLOCK, LANES), MXU_DTYPE)],
        scratch_shapes=[pltpu.VMEM((2, nu, 2 * SB_BLOCK, LANES), MXU_DTYPE), pltpu.SemaphoreType.DMA((2, nu))],
        compiler_params=_cparams(2), name=name,
    )(src, src, src)


def _sb_bwd(src, col0, width, weights, do, *, name):
    s = src.shape[0]
    nq = s // SB_BLOCK
    npair = width // LANES
    scale = SB_HEAD_DIM ** -0.5
    nu = math.gcd(SB_UNROLL, nq)

    def body(q_ref, k_ref, v_ref, w_hbm, do_ref, dq_ref, dk_ref, dv_ref, stage, sems):
        p, i = pl.program_id(0), pl.program_id(1)

        @pl.when(i == 0)
        def _():
            dk_ref[...] = jnp.zeros_like(dk_ref)
            dv_ref[...] = jnp.zeros_like(dv_ref)

        r2, c2, lm0, _, m_lt = _sb_consts()
        t_glob = i * SB_BLOCK + (r2 & (SB_BLOCK - 1))
        q2 = (_sb_stack(q_ref[...], lm0) * scale).astype(MXU_DTYPE)
        do2 = _sb_stack(do_ref[...], lm0).astype(MXU_DTYPE)

        def load(slot, u, j):
            return pltpu.make_async_copy(w_hbm.at[p, i, j], stage.at[slot, u], sems.at[slot, u])

        def fetch(g, slot):
            for u in range(nu):
                load(slot, u, nu * g + u).start()

        def arrived(slot):
            for u in range(nu):
                load(slot, u, 0).wait()

        def group(g, carry, masked, slot):
            dq2, csum = carry
            js = [nu * g + u for u in range(nu)]
            offs = [pl.multiple_of(j * SB_BLOCK, SB_BLOCK) for j in js]
            kbs = [k_ref[pl.ds(off, SB_BLOCK), :].astype(MXU_DTYPE) for off in offs]
            zs = [_dot(q2, kb, NT) for kb in kbs]
            dws = [_dot(do2, v_ref[pl.ds(off, SB_BLOCK), :].astype(MXU_DTYPE), NT) for off in offs]
            wbs = [stage[slot, u] for u in range(nu)]
            sigs = [_sigmoid(z) for z in zs]
            dlogas = [wb.astype(F32) * dw for wb, dw in zip(wbs, dws)]
            pres = [_split_dot(dl, m_lt, SB_SPLIT) for dl in dlogas]
            dlks = []
            for dl, pre in zip(dlogas, pres):
                dlks.append(csum + pre)
                csum = csum + jnp.sum(dl, axis=1, keepdims=True)
            if masked:
                dlks = [jnp.where((j * SB_BLOCK + c2) < t_glob, dlk, 0.0) for j, dlk in zip(js, dlks)]
            dzbs = [(dl * (1.0 - sg) - dlk * sg).astype(MXU_DTYPE) for dl, sg, dlk in zip(dlogas, sigs, dlks)]
            for off, dzb, wb, kb in zip(offs, dzbs, wbs, kbs):
                dk_ref[pl.ds(off, SB_BLOCK), :] += _dot(dzb, q2, TN)
                dv_ref[pl.ds(off, SB_BLOCK), :] += _dot(wb, do2, TN)
                dq2 = dq2 + _dot(dzb, kb, NN)
            return dq2, csum

        def step(g, carry):
            fetch(g + 1, (g + 1) % 2)
            arrived(g % 2)
            return group(g, carry, False, g % 2)

        last = i // nu
        fetch(0, 0)
        carry = lax.fori_loop(0, last, step, (jnp.zeros((2 * SB_BLOCK, LANES), F32), jnp.zeros((2 * SB_BLOCK, 1), F32)))
        arrived(last % 2)
        dq2, _ = group(last, carry, True, last % 2)
        dq_ref[...] = _sb_unstack(dq2, lm0) * scale

    blk = pl.BlockSpec((SB_BLOCK, LANES), lambda p, i: (i, p))
    full = pl.BlockSpec((s, LANES), lambda p, i: (0, p))
    return pl.pallas_call(
        body, grid=(npair, nq),
        in_specs=[pl.BlockSpec((SB_BLOCK, LANES), lambda p, i: (i, col0 + p)),
                  pl.BlockSpec((s, LANES), lambda p, i: (0, col0 + npair + p)),
                  pl.BlockSpec((s, LANES), lambda p, i: (0, col0 + 2 * npair + p)),
                  ANY, blk],
        out_specs=[blk, full, full],
        out_shape=[jax.ShapeDtypeStruct((s, width), F32)] * 3,
        scratch_shapes=[pltpu.VMEM((2, nu, 2 * SB_BLOCK, LANES), MXU_DTYPE), pltpu.SemaphoreType.DMA((2, nu))],
        compiler_params=_cparams(2), name=name,
    )(src, src, src, weights, do)


def _ssd_group(xs, dt_rows, alogs, dtbs, bms, cms, h0s):
    c = bms[0].shape[0]
    per = len(xs) // len(bms)
    ii, jj = _chunk_masks(c)
    causal, eye = ii >= jj, ii == jj
    grp = lambda per_group: [t for t in per_group for _ in range(per)]
    scores, bm, cm = grp(_each(lambda c_, b_: _hdot(c_, b_, NT), cms, bms)), grp(bms), grp(cms)
    dt_r = _each(lambda dt, b: _softplus(dt + b), dt_rows, dtbs)
    a_r = _each(lambda al, dt: -jnp.exp(al) * dt, alogs, dt_r)
    dt_col = _each(lambda dt: _row_to_col(dt, eye), dt_r)
    a_col = _each(lambda a: _row_to_col(a, eye), a_r)
    ac_col = _each(lambda a: jnp.sum(jnp.where(causal, a, 0.0), axis=1, keepdims=True), a_r)
    ac_row = _each(lambda a: jnp.sum(jnp.where(jj >= ii, a, 0.0), axis=0, keepdims=True), a_col)
    lmat = _each(lambda c_, r_: jnp.exp(jnp.where(causal, c_ - r_, NEG_BIG)), ac_col, ac_row)
    xdt = _each(jnp.multiply, xs, dt_col)
    al = _each(lambda a: jnp.sum(a, axis=1, keepdims=True), a_r)
    ys = _each(lambda sc, lm, xd, cm_, h0, ac: _hdot(sc * lm, xd) + _hdot(cm_, h0, NT) * jnp.exp(ac),
               scores, lmat, xdt, cm, h0s, ac_col)
    h1s = _each(lambda h0, al_, xd, ac, bm_: h0 * jnp.exp(al_) + _hdot(xd * jnp.exp(al_ - ac), bm_, TN),
                h0s, al, xdt, ac_col, bm)
    return ys, h1s


def _ssd_specs(ng, nc, r, gb, rev):
    n_of = (lambda n: nc - 1 - n) if rev else (lambda n: n)
    xw, bw = gb * r * SSM_HEAD_DIM, gb * SSM_STATE
    b0, c0 = (ng * r * SSM_HEAD_DIM) // bw, (ng * r * SSM_HEAD_DIM + ng * SSM_STATE) // bw
    x_spec = pl.BlockSpec((CHUNK, xw), lambda g, n: (n_of(n), g))
    b_spec = pl.BlockSpec((CHUNK, bw), lambda g, n: (n_of(n), b0 + g))
    c_spec = pl.BlockSpec((CHUNK, bw), lambda g, n: (n_of(n), c0 + g))
    dt_spec = pl.BlockSpec((gb, None, r, CHUNK), lambda g, n: (g, n_of(n), 0, 0))
    sc_spec = pl.BlockSpec((gb, r, 1), lambda g, n: (g, 0, 0))
    st_spec = pl.BlockSpec((gb, None, r, SSM_HEAD_DIM, SSM_STATE), lambda g, n: (g, n_of(n), 0, 0, 0))
    bc_out = pl.BlockSpec((CHUNK, bw), lambda g, n: (n_of(n), g))
    return x_spec, b_spec, c_spec, dt_spec, sc_spec, st_spec, x_spec, bc_out


def _ssd_refs(gb, r, x_ref, b_ref, c_ref, dt_ref, al_ref, db_ref):
    p, n = SSM_HEAD_DIM, SSM_STATE
    heads = [(g, h) for g in range(gb) for h in range(r)]
    xs = [x_ref[:, (g * r + h) * p:(g * r + h + 1) * p] for g, h in heads]
    dts = [dt_ref[g, h:h + 1, :] for g, h in heads]
    als = [al_ref[g, h:h + 1, :] for g, h in heads]
    dbs = [db_ref[g, h:h + 1, :] for g, h in heads]
    bms = [b_ref[:, g * n:(g + 1) * n] for g in range(gb)]
    cms = [c_ref[:, g * n:(g + 1) * n] for g in range(gb)]
    return heads, xs, dts, als, dbs, bms, cms


def _ssd_fwd(xbc, dt_rows, alog, dtb, *, name):
    s = xbc.shape[0]
    ng, nc, r = dt_rows.shape[0], dt_rows.shape[1], dt_rows.shape[2]
    w = ng * r * SSM_HEAD_DIM
    gb = math.gcd(SSD_GROUPS_PER_STEP, ng)
    x_spec, b_spec, c_spec, dt_spec, sc_spec, st_spec, y_spec, _ = _ssd_specs(ng, nc, r, gb, False)
    p = SSM_HEAD_DIM

    def body(x_ref, b_ref, c_ref, dt_ref, al_ref, db_ref, y_ref, st_ref, state):
        @pl.when(pl.program_id(1) == 0)
        def _():
            state[...] = jnp.zeros_like(state)

        st_ref[...] = state[...]
        heads, xs, dts, als, dbs, bms, cms = _ssd_refs(gb, r, x_ref, b_ref, c_ref, dt_ref, al_ref, db_ref)
        ys, h1s = _ssd_group(xs, dts, als, dbs, bms, cms, [state[g, h] for g, h in heads])
        for i, (g, h) in enumerate(heads):
            y_ref[:, (g * r + h) * p:(g * r + h + 1) * p] = ys[i]
            state[g, h] = h1s[i]

    return pl.pallas_call(
        body, grid=(ng // gb, nc),
        in_specs=[x_spec, b_spec, c_spec, dt_spec, sc_spec, sc_spec],
        out_specs=[y_spec, st_spec],
        out_shape=[jax.ShapeDtypeStruct((s, w), F32), jax.ShapeDtypeStruct((ng, nc, r, p, SSM_STATE), F32)],
        scratch_shapes=[pltpu.VMEM((gb, r, p, SSM_STATE), F32)],
        compiler_params=_cparams(2), name=name,
    )(xbc, xbc, xbc, dt_rows, alog, dtb)


def _ssd_bwd(xbc, dt_rows, alog, dtb, states, dy, *, name):
    s = xbc.shape[0]
    ng, nc, r = dt_rows.shape[0], dt_rows.shape[1], dt_rows.shape[2]
    w = ng * r * SSM_HEAD_DIM
    gb = math.gcd(SSD_GROUPS_PER_STEP, ng)
    x_spec, b_spec, c_spec, dt_spec, sc_spec, st_spec, y_spec, bc_out = _ssd_specs(ng, nc, r, gb, True)
    p = SSM_HEAD_DIM

    def body(x_ref, b_ref, c_ref, dt_ref, al_ref, db_ref, st_ref, dy_ref,
             dx_ref, dbm_ref, dcm_ref, ddt_ref, dal_ref, ddb_ref, dstate):
        @pl.when(pl.program_id(1) == 0)
        def _():
            dstate[...] = jnp.zeros_like(dstate)
            dal_ref[...] = jnp.zeros_like(dal_ref)
            ddb_ref[...] = jnp.zeros_like(ddb_ref)

        heads, xs, dts, als, dbs, bms, cms = _ssd_refs(gb, r, x_ref, b_ref, c_ref, dt_ref, al_ref, db_ref)
        _, vjp = jax.vjp(_ssd_group, xs, dts, als, dbs, bms, cms, [st_ref[g, h] for g, h in heads])
        dys = [dy_ref[:, (g * r + h) * p:(g * r + h + 1) * p] for g, h in heads]
        dxs, ddts, dals, ddbs, dbms, dcms, dh0s = vjp((dys, [dstate[g, h] for g, h in heads]))
        for g in range(gb):
            dbm_ref[:, g * SSM_STATE:(g + 1) * SSM_STATE] = dbms[g]
            dcm_ref[:, g * SSM_STATE:(g + 1) * SSM_STATE] = dcms[g]
        for i, (g, h) in enumerate(heads):
            dx_ref[:, (g * r + h) * p:(g * r + h + 1) * p] = dxs[i]
            ddt_ref[g, h:h + 1, :] = ddts[i]
            dal_ref[g, h:h + 1, :] += dals[i]
            ddb_ref[g, h:h + 1, :] += ddbs[i]
            dstate[g, h] = dh0s[i]

    gn = ng * SSM_STATE
    return pl.pallas_call(
        body, grid=(ng // gb, nc),
        in_specs=[x_spec, b_spec, c_spec, dt_spec, sc_spec, sc_spec, st_spec, y_spec],
        out_specs=[y_spec, bc_out, bc_out, dt_spec, sc_spec, sc_spec],
        out_shape=[jax.ShapeDtypeStruct((s, w), F32), jax.ShapeDtypeStruct((s, gn), F32), jax.ShapeDtypeStruct((s, gn), F32),
                   jax.ShapeDtypeStruct(dt_rows.shape, F32), jax.ShapeDtypeStruct((ng, r, 1), F32),
                   jax.ShapeDtypeStruct((ng, r, 1), F32)],
        scratch_shapes=[pltpu.VMEM((gb, r, p, SSM_STATE), F32)],
        compiler_params=_cparams(2), name=name,
    )(xbc, xbc, xbc, dt_rows, alog, dtb, states, dy)


def _ssm_post_fwd(y, xbc, src, z_col0, dexp, nw, *, name, tm=256):
    s, w = y.shape
    gw = w // SSM_GROUPS
    zc = z_col0 * LANES // gw

    def body(y_ref, x_ref, z_ref, d_ref, w_ref, o_ref):
        yy = (y_ref[...] + x_ref[...] * d_ref[...]) * _silu(z_ref[...])
        r = lax.rsqrt(jnp.mean(yy * yy, axis=-1, keepdims=True) + EPS)
        o_ref[...] = (yy * r * w_ref[...]).astype(o_ref.dtype)

    blk = pl.BlockSpec((tm, gw), lambda g, i: (i, g))
    vec = pl.BlockSpec((1, gw), lambda g, i: (0, g))
    return pl.pallas_call(
        body, grid=(SSM_GROUPS, s // tm),
        in_specs=[blk, blk, pl.BlockSpec((tm, gw), lambda g, i: (i, zc + g)), vec, vec],
        out_specs=blk, out_shape=jax.ShapeDtypeStruct((s, w), MXU_DTYPE),
        compiler_params=_cparams(2), name=name,
    )(y, xbc, src, dexp.reshape(1, w), nw.reshape(1, w))


def _ssm_post_bwd(y, xbc, src, z_col0, dexp, nw, dout, into, *, name, tm=256):
    s, w = y.shape
    gw = w // SSM_GROUPS
    zc = z_col0 * LANES // gw

    def body(y_ref, x_ref, z_ref, d_ref, w_ref, do_ref, into_ref, dy_ref, dx_ref, dz_ref, dd_ref, dw_ref):
        xv, zv, dv = x_ref[...], z_ref[...], d_ref[...]
        pre = y_ref[...] + xv * dv
        sz, sz_grad = _silu_and_grad(zv)
        yy = pre * sz
        r = lax.rsqrt(jnp.mean(yy * yy, axis=-1, keepdims=True) + EPS)
        yh = yy * r
        dov = do_ref[...]
        dyn = dov * w_ref[...]
        dyy = r * (dyn - yh * jnp.mean(dyn * yh, axis=-1, keepdims=True))
        dpre = dyy * sz
        dy_ref[...] = dpre
        dx_ref[...] = dpre * dv
        dz_ref[...] = (dyy * pre * sz_grad).astype(dz_ref.dtype)

        @pl.when(pl.program_id(1) == 0)
        def _():
            dd_ref[...] = jnp.zeros_like(dd_ref)
            dw_ref[...] = jnp.zeros_like(dw_ref)

        dd_ref[...] += jnp.sum(dpre * xv, axis=0, keepdims=True)
        dw_ref[...] += jnp.sum(dov * yh, axis=0, keepdims=True)

    blk = pl.BlockSpec((tm, gw), lambda g, i: (i, g))
    vec = pl.BlockSpec((1, gw), lambda g, i: (0, g))
    z_blk = pl.BlockSpec((tm, gw), lambda g, i: (i, zc + g))
    dy, dx, dz, dd, dw = pl.pallas_call(
        body, grid=(SSM_GROUPS, s // tm),
        in_specs=[blk, blk, z_blk, vec, vec, blk, ANY],
        out_specs=[blk, blk, z_blk, vec, vec],
        out_shape=[jax.ShapeDtypeStruct((s, w), F32), jax.ShapeDtypeStruct((s, w), F32),
                   jax.ShapeDtypeStruct(into.shape, into.dtype), jax.ShapeDtypeStruct((1, w), F32),
                   jax.ShapeDtypeStruct((1, w), F32)],
        input_output_aliases={6: 2},
        compiler_params=_cparams(2), name=name,
    )(y, xbc, src, dexp.reshape(1, w), nw.reshape(1, w), dout, into)
    return dy, dx, dz, dd.reshape(w), dw.reshape(w)


def _merge_fwd(proj3, src, gate_col0, d, *, name, tm=256):
    s = proj3.shape[0]
    nb = proj3.shape[1] // d
    gc = gate_col0 * LANES // d

    def body(*refs):
        p_refs, g_refs, o_ref = refs[:nb], refs[nb:2 * nb], refs[-1]
        acc = None
        for p_ref, g_ref in zip(p_refs, g_refs):
            term = _sigmoid(g_ref[...]) * p_ref[...]
            acc = term if acc is None else acc + term
        o_ref[...] = acc.astype(o_ref.dtype)

    p_specs = [pl.BlockSpec((tm, d), lambda i, b=b: (i, b)) for b in range(nb)]
    g_specs = [pl.BlockSpec((tm, d), lambda i, b=b: (i, gc + b)) for b in range(nb)]
    return pl.pallas_call(
        body, grid=(s // tm,), in_specs=p_specs + g_specs,
        out_specs=pl.BlockSpec((tm, d), lambda i: (i, 0)), out_shape=jax.ShapeDtypeStruct((s, d), MXU_DTYPE),
        compiler_params=_cparams(1), name=name,
    )(*([proj3] * nb), *([src] * nb))


def _merge_bwd(proj3, src, gate_col0, d, dmerged, into, *, name, tm=256):
    s = proj3.shape[0]
    nb = proj3.shape[1] // d
    gc = gate_col0 * LANES // d

    def body(p_ref, g_ref, dm_ref, into_ref, dp_ref, dg_ref):
        sg = _sigmoid(g_ref[...])
        dm = dm_ref[...]
        dp_ref[...] = (dm * sg).astype(dp_ref.dtype)
        dg_ref[...] = (dm * p_ref[...] * sg * (1.0 - sg)).astype(dg_ref.dtype)

    blk = pl.BlockSpec((tm, d), lambda i, b: (i, b))
    gate_blk = pl.BlockSpec((tm, d), lambda i, b: (i, gc + b))
    return pl.pallas_call(
        body, grid=(s // tm, nb),
        in_specs=[blk, gate_blk, pl.BlockSpec((tm, d), lambda i, b: (i, 0)), ANY],
        out_specs=[blk, gate_blk],
        out_shape=[jax.ShapeDtypeStruct(proj3.shape, MXU_DTYPE), jax.ShapeDtypeStruct(into.shape, into.dtype)],
        input_output_aliases={3: 1},
        compiler_params=_cparams(2), name=name,
    )(proj3, src, dmerged, into)


ANY = pl.BlockSpec(memory_space=pl.ANY)
MESH = pl.DeviceIdType.MESH


def _all_gather(shards, *, name, after=None):
    nt = len(shards)
    n_after = 0 if after is None else 1

    def body(*refs):
        x_refs, out_refs = refs[:nt], refs[nt + n_after:2 * nt + n_after]
        send_sems, recv_sems, local_sems = refs[2 * nt + n_after:]
        x, y, c = lax.axis_index("x"), lax.axis_index("y"), lax.axis_index("c")
        me, sibling = (x, y, c), (x, y, 1 - c)
        chips = [(1 - x, y), (x, 1 - y), (1 - x, 1 - y)]

        def slot(t, px, py, pc):
            return out_refs[t].at[4 * px + 2 * py + pc]

        def copy(t, k, block, to, from_input=False):
            return pltpu.make_async_remote_copy(
                src_ref=x_refs[t] if from_input else slot(t, *block), dst_ref=slot(t, *block),
                send_sem=send_sems.at[7 * t + k], recv_sem=recv_sems.at[7 * t + k], device_id=to, device_id_type=MESH)

        mine = [pltpu.make_async_copy(x_refs[t], slot(t, *me), local_sems.at[t]) for t in range(nt)]
        for cp in mine:
            cp.start()
        first = [copy(t, 0, me, sibling, True) for t in range(nt)]
        first += [copy(t, 1 + j, me, (*chip, c), True) for j, chip in enumerate(chips) for t in range(nt)]
        for cp in first:
            cp.start()
        passed = []
        for j, chip in enumerate(chips):
            for t in range(nt):
                copy(t, 1 + j, (*chip, c), me).wait_recv()
                fwd = copy(t, 4 + j, (*chip, c), sibling)
                fwd.start()
                passed.append(fwd)
        for t in range(nt):
            copy(t, 0, sibling, me).wait_recv()
            for j, chip in enumerate(chips):
                copy(t, 4 + j, (*chip, 1 - c), me).wait_recv()
        for cp in first + passed:
            cp.wait_send()
        for cp in mine:
            cp.wait()

    return pl.pallas_call(
        body, out_shape=[jax.ShapeDtypeStruct((N_DEV,) + a.shape, a.dtype) for a in shards],
        in_specs=[ANY] * (nt + n_after), out_specs=[ANY] * nt,
        scratch_shapes=[pltpu.SemaphoreType.DMA((7 * nt,)), pltpu.SemaphoreType.DMA((7 * nt,)),
                        pltpu.SemaphoreType.DMA((nt,))],
        name=name,
    )(*shards, *([] if after is None else [after]))


def _grad_exchange(bigs, small, *, name):
    nl = len(bigs[0])
    flat = [a for per_layer in bigs for a in per_layer]
    nslot = len(flat)

    def body(*refs):
        in_refs, small_ref = refs[:nslot], refs[nslot]
        out_refs, smallr_ref = refs[nslot + 1:nslot + 1 + len(bigs)], refs[nslot + 1 + len(bigs)]
        send_sems, recv_sems, local_sems = refs[nslot + 2 + len(bigs):]
        x, y, c = lax.axis_index("x"), lax.axis_index("y"), lax.axis_index("c")
        me = 4 * x + 2 * y + c
        local = [pltpu.make_async_copy(in_refs[i].at[me], out_refs[i // nl].at[me, i % nl], local_sems.at[i])
                 for i in range(nslot)]
        local.append(pltpu.make_async_copy(small_ref, smallr_ref.at[me], local_sems.at[nslot]))
        for cp in local:
            cp.start()
        copies = []
        for k in range(1, N_DEV):
            px = x ^ ((k >> 2) & 1)
            py = y ^ ((k >> 1) & 1)
            pc = c ^ (k & 1)
            peer = 4 * px + 2 * py + pc
            for i in range(nslot + 1):
                sem = 7 * i + (k - 1)
                src = in_refs[i].at[peer] if i < nslot else small_ref
                dst = out_refs[i // nl].at[me, i % nl] if i < nslot else smallr_ref.at[me]
                copies.append(pltpu.make_async_remote_copy(
                    src_ref=src, dst_ref=dst, send_sem=send_sems.at[sem], recv_sem=recv_sems.at[sem],
                    device_id=(px, py, pc), device_id_type=MESH))
        for cp in copies:
            cp.start()
        for cp in copies:
            cp.wait_recv()
        for cp in copies:
            cp.wait_send()
        for cp in local:
            cp.wait()

    out_shape = [jax.ShapeDtypeStruct((N_DEV, nl) + per_layer[0].shape[1:], per_layer[0].dtype) for per_layer in bigs]
    out_shape.append(jax.ShapeDtypeStruct((N_DEV,) + small.shape, small.dtype))
    nsem = 7 * (nslot + 1)
    outs = pl.pallas_call(
        body, out_shape=out_shape,
        in_specs=[ANY] * (nslot + 1), out_specs=[ANY] * (len(bigs) + 1),
        scratch_shapes=[pltpu.SemaphoreType.DMA((nsem,)), pltpu.SemaphoreType.DMA((nsem,)),
                        pltpu.SemaphoreType.DMA((nslot + 1,))],
        name=name,
    )(*flat, small)
    return outs[:-1], outs[-1]


HBM = pl.BlockSpec(memory_space=pltpu.HBM)
SEM = pl.BlockSpec(memory_space=pltpu.SEMAPHORE)
EFFECT = pltpu.SideEffectType.DATAFLOW_SIDE_EFFECTING


def _peers():
    x, y, c = lax.axis_index("x"), lax.axis_index("y"), lax.axis_index("c")
    peers = []
    for k in range(1, N_DEV):
        px, py, pc = x ^ ((k >> 2) & 1), y ^ ((k >> 1) & 1), c ^ (k & 1)
        peers.append(((px, py, pc), 4 * px + 2 * py + pc))
    return 4 * x + 2 * y + c, peers


def _split_copies(slots, src_refs, land_refs, send_sems, recv_sems):
    me, peers = _peers()
    copies = []
    for t, (whole, layer) in enumerate(slots):
        dst = land_refs[t].at[me] if layer is None else land_refs[t].at[me, layer]
        for k, (dev, lin) in enumerate(peers):
            copies.append(pltpu.make_async_remote_copy(
                src_ref=src_refs[t] if whole else src_refs[t].at[lin], dst_ref=dst,
                send_sem=send_sems.at[7 * t + k], recv_sem=recv_sems.at[7 * t + k], device_id=dev, device_id_type=MESH))
    return copies


def _split_start(srcs, lands, slots, carry, *, name):
    n = len(srcs)

    def body(*refs):
        copies = _split_copies(slots, refs[:n], refs[n:2 * n], refs[2 * n + 1], refs[2 * n + 2])
        for cp in copies:
            cp.start()

    def hbm(a):
        return pltpu.HBM(a.shape, a.dtype)

    outs = pl.pallas_call(
        body, name=name,
        out_shape=[pltpu.SemaphoreType.DMA((7 * n,)), pltpu.SemaphoreType.DMA((7 * n,))]
        + [hbm(a) for a in srcs] + [hbm(a) for a in lands] + [hbm(carry)],
        in_specs=[HBM] * (2 * n + 1), out_specs=[SEM, SEM] + [HBM] * (2 * n + 1),
        input_output_aliases={i: 2 + i for i in range(2 * n + 1)},
        compiler_params=pltpu.CompilerParams(has_side_effects=EFFECT),
    )(*[pltpu.with_memory_space_constraint(a, pltpu.HBM) for a in list(srcs) + list(lands) + [carry]])
    return outs[0], outs[1], outs[2:2 + n], outs[2 + n:2 + 2 * n], outs[2 + 2 * n]


def _split_wait(send_sems, recv_sems, srcs, lands, slots, after, *, name):
    n = len(srcs)

    def body(*refs):
        copies = _split_copies(slots, refs[:n], refs[n:2 * n], refs[2 * n], refs[2 * n + 1])
        for cp in copies:
            cp.wait_send()
        for cp in copies:
            cp.wait_recv()

    outs = pl.pallas_call(
        body, name=name,
        out_shape=[pltpu.HBM(a.shape, a.dtype) for a in list(srcs) + list(lands)],
        in_specs=[HBM] * (2 * n) + [SEM, SEM, ANY], out_specs=[HBM] * (2 * n),
        input_output_aliases={i: i for i in range(2 * n)},
        compiler_params=pltpu.CompilerParams(has_side_effects=EFFECT),
    )(*srcs, *lands, send_sems, recv_sems, after)
    return outs[n:]


def _adam_math(w, g, m, v):
    m1 = ADAM_B1 * m + (1.0 - ADAM_B1) * g
    v1 = ADAM_B2 * v + (1.0 - ADAM_B2) * (g * g)
    m_hat = m1 / (1.0 - ADAM_B1 ** ADAM_STEP)
    v_hat = v1 / (1.0 - ADAM_B2 ** ADAM_STEP)
    delta = -ADAM_LR * (m_hat / (jnp.sqrt(v_hat) + ADAM_EPS) + ADAM_WD * w)
    return delta, m1, v1


def _sum_adamw(parts, w, m, v, layer, prev, *, name):
    shape = w.shape
    r, c = shape[-2], shape[-1]
    a_l = math.prod(shape[1:-2])
    a = shape[0] * a_l
    base = layer * a_l
    if r % 256 == 0:
        tr, tc = 256, c
    else:
        tr, tc = r, _pick(c, (256, 128))
    w3, m3, v3 = (t.reshape(a, r, c) for t in (w, m, v))
    n_prev = 0 if prev is None else 4

    def body(*refs):
        p_ref, w_ref, m_ref, v_ref = refs[:4]
        g_ref, d_ref, m1_ref, v1_ref = refs[4 + n_prev:]
        g = p_ref[0].astype(F32)
        for src in range(1, N_DEV):
            g = g + p_ref[src].astype(F32)
        delta, m1, v1 = _adam_math(w_ref[...], g, m_ref[...], v_ref[...])
        g_ref[...] = g
        d_ref[...] = delta
        m1_ref[...] = m1
        v1_ref[...] = v1

    nr, ncol = r // tr, c // tc
    blk = pl.BlockSpec((None, tr, tc), lambda i, j: (base + i, j // ncol, j % ncol))
    prev3 = [] if prev is None else [t.reshape(a, r, c) for t in prev]
    outs = pl.pallas_call(
        body, grid=(a_l, nr * ncol),
        in_specs=[pl.BlockSpec((N_DEV, None, tr, tc), lambda i, j: (0, i, j // ncol, j % ncol)), blk, blk, blk]
        + [ANY] * n_prev,
        out_specs=[blk] * 4, out_shape=[jax.ShapeDtypeStruct((a, r, c), F32)] * 4,
        input_output_aliases={4 + k: k for k in range(n_prev)},
        compiler_params=_cparams(2), name=name,
    )(parts.reshape(N_DEV, a_l, r, c), w3, m3, v3, *prev3)
    return [o.reshape(shape) for o in outs]


def _sum_parts(parts, *, name):
    rows = parts.shape[1]

    def body(p_ref, o_ref):
        g = p_ref[0]
        for src in range(1, N_DEV):
            g = g + p_ref[src]
        o_ref[...] = g

    return pl.pallas_call(
        body, grid=(1,), in_specs=[pl.BlockSpec((N_DEV, rows, LANES), lambda i: (0, 0, 0))],
        out_specs=pl.BlockSpec((rows, LANES), lambda i: (0, 0)), out_shape=jax.ShapeDtypeStruct((rows, LANES), F32),
        compiler_params=_cparams(1), name=name,
    )(parts)


def _adamw(w, g, m, v, *, name):
    rows = w.shape[0]

    def body(w_ref, g_ref, m_ref, v_ref, d_ref, m1_ref, v1_ref):
        delta, m1, v1 = _adam_math(w_ref[...], g_ref[...], m_ref[...], v_ref[...])
        d_ref[...] = delta
        m1_ref[...] = m1
        v1_ref[...] = v1

    blk = pl.BlockSpec((rows, LANES), lambda i: (0, 0))
    return pl.pallas_call(
        body, grid=(1,), in_specs=[blk] * 4, out_specs=[blk] * 3,
        out_shape=[jax.ShapeDtypeStruct((rows, LANES), F32)] * 3,
        compiler_params=_cparams(1), name=name,
    )(w, g, m, v)


def _pack(arrs, dtype, row_mult=16):
    flat = jnp.concatenate([a.reshape(-1).astype(dtype) for a in arrs])
    n = flat.shape[0]
    rows = -(-n // (LANES * row_mult)) * row_mult
    flat = jnp.pad(flat, (0, rows * LANES - n))
    return flat.reshape(rows, LANES)


def _unpack(packed, shapes):
    flat = packed.reshape(-1)
    out, off = [], 0
    for shp in shapes:
        n = math.prod(shp)
        out.append(flat[off:off + n].reshape(shp))
        off += n
    return out


class _Layout:
    def __init__(self, d):
        self.d = d
        w = d
        self.dn_heads = w // DN_HEAD_DIM
        self.ssm_heads = w // SSM_HEAD_DIM
        gn = SSM_GROUPS * SSM_STATE
        self.sizes = (3 * w, w, self.dn_heads, self.dn_heads, 3 * w, w, w + 2 * gn, self.ssm_heads, 3 * d)
        offs, o = [], 0
        for sz in self.sizes:
            offs.append(o)
            o += sz
        self.offs = offs
        self.in_dim = o
        self.big = (0, 1, 4, 5, 6, 8)
        self.small = (2, 3, 7)
        cols, o = {}, 0
        for idx in self.big:
            cols[idx] = o
            o += self.sizes[idx]
        self.small_col = o
        self.cols = cols
        self.padded = o + LANES
        self.n_small = sum(self.sizes[i] for i in self.small)

    def reorder_w(self, w_in):
        parts = [w_in[:, self.offs[i]:self.offs[i] + self.sizes[i]] for i in self.big + self.small]
        parts.append(jnp.zeros((w_in.shape[0], LANES - self.n_small), w_in.dtype))
        return jnp.concatenate(parts, axis=1)

    def from_shards(self, parts):
        cs = self.in_dim // N_DEV
        pieces = []
        for i in self.big + self.small:
            a, b = self.offs[i], self.offs[i] + self.sizes[i]
            while a < b:
                j = a // cs
                hi = min(b, (j + 1) * cs)
                pieces.append(parts[j][:, a - j * cs:hi - j * cs])
                a = hi
        pieces.append(jnp.zeros((parts.shape[1], LANES - self.n_small), parts.dtype))
        return jnp.concatenate(pieces, axis=1)

    def to_shards(self, wp):
        cs = self.in_dim // N_DEV
        pcol = dict(self.cols)
        o = self.small_col
        for i in self.small:
            pcol[i] = o
            o += self.sizes[i]
        shards = []
        for j in range(N_DEV):
            a, b = j * cs, (j + 1) * cs
            pieces = []
            for i in range(len(self.sizes)):
                lo, hi = max(a, self.offs[i]), min(b, self.offs[i] + self.sizes[i])
                if lo < hi:
                    pieces.append(wp[:, pcol[i] + lo - self.offs[i]:pcol[i] + hi - self.offs[i]])
            shards.append(jnp.concatenate(pieces, axis=1))
        return jnp.stack(shards)

    def restore_w(self, wp):
        pieces = {}
        for idx in self.big:
            pieces[idx] = wp[:, self.cols[idx]:self.cols[idx] + self.sizes[idx]]
        o = self.small_col
        for idx in self.small:
            pieces[idx] = wp[:, o:o + self.sizes[idx]]
            o += self.sizes[idx]
        return jnp.concatenate([pieces[i] for i in range(len(self.sizes))], axis=1)


def _rows_form(cols_t, nh, nc):
    return cols_t.T.reshape(nh, nc, 1, CHUNK)


def _layer_fwd(x, p, lay, tag, late=None):
    s, d = x.shape
    nc = s // CHUNK
    w = d
    dnh, smh = lay.dn_heads, lay.ssm_heads
    r = smh // SSM_GROUPS
    cb = {k: v // LANES for k, v in lay.cols.items()}
    sv = {}
    h1 = _rms_fwd(x, p["norm_mix"], name=f"rms_mix_{tag}")
    proj = _matmul(h1, p["w_in"], name=f"mm_in_{tag}")
    small = proj[:, lay.small_col:lay.small_col + LANES]
    a_rows = _rows_form(small[:, 0:dnh], dnh, nc)
    b_rows = _rows_form(small[:, dnh:2 * dnh], dnh, nc)
    dt_rows = small[:, 2 * dnh:2 * dnh + smh].T.reshape(SSM_GROUPS, r, nc, CHUNK).transpose(0, 2, 1, 3)
    zero_b = jnp.zeros((1, 3 * w), F32)
    dn_qkv = _conv_fwd(proj, cb[0], p["dn_conv_w"], zero_b, 2 * dnh, name=f"dn_conv_{tag}")
    dn_alog = p["dn_a_log"].reshape(dnh, 1, 1)
    dn_dtb = p["dn_dt_bias"].reshape(dnh, 1, 1)
    o_dn, dn_states, dn_inv = _dn_fwd(dn_qkv, a_rows, b_rows, dn_alog, dn_dtb, name=f"dn_chunk_{tag}")
    y_dn = _dn_post_fwd(o_dn, proj, cb[1], p["dn_norm_w"], name=f"dn_post_{tag}")
    o_sb, sb_r = _sb_fwd(proj, cb[4], w, name=f"sb_{tag}")
    xbc = _conv_fwd(proj, cb[6], p["ssm_conv_w"], p["ssm_conv_b"].reshape(1, -1), 0, name=f"ssm_conv_{tag}")
    ssm_alog = p["ssm_a_log"].reshape(SSM_GROUPS, r, 1)
    ssm_dtb = p["ssm_dt_bias"].reshape(SSM_GROUPS, r, 1)
    y_ssd, ssm_states = _ssd_fwd(xbc, dt_rows, ssm_alog, ssm_dtb, name=f"ssd_{tag}")
    dexp = jnp.repeat(p["ssm_d"], SSM_HEAD_DIM)
    y_ssm = _ssm_post_fwd(y_ssd, xbc, proj, cb[5], dexp, p["ssm_norm_w"], name=f"ssm_post_{tag}")
    if late is not None:
        p.update(late(y_ssm))
    branches = (y_dn, o_sb, y_ssm)
    proj3 = jnp.concatenate(
        [_matmul(br, p["w_branch"][i], name=f"mm_branch{i}_{tag}") for i, br in enumerate(branches)], axis=1)
    merged = _merge_fwd(proj3, proj, cb[8], d, name=f"merge_{tag}")
    x1 = _matmul(merged, p["w_out"], name=f"mm_out_{tag}", epilogue=lambda acc, res: (acc + res,), extras=(x,))
    h2 = _rms_fwd(x1, p["norm_mlp"], name=f"rms_mlp_{tag}")
    u, act = _matmul(h2, p["w_up"], name=f"mm_up_{tag}", out_dtypes=(F32, MXU_DTYPE),
                     epilogue=lambda acc: (acc, jnp.square(jnp.maximum(acc, 0.0))))
    x2 = _matmul(act, p["w_down"], name=f"mm_down_{tag}", epilogue=lambda acc, res: (acc + res,), extras=(x1,))
    sv.update(x=x, h1=h1, proj=proj, a_rows=a_rows, b_rows=b_rows, dt_rows=dt_rows, dn_qkv=dn_qkv, dn_alog=dn_alog,
              dn_dtb=dn_dtb, o_dn=o_dn, dn_states=dn_states, dn_inv=dn_inv, y_dn=y_dn, o_sb=o_sb, sb_r=sb_r, xbc=xbc, ssm_alog=ssm_alog,
              ssm_dtb=ssm_dtb, y_ssd=y_ssd, ssm_states=ssm_states, dexp=dexp, y_ssm=y_ssm, proj3=proj3, merged=merged,
              x1=x1, h2=h2, u=u, act=act)
    return x2, sv


def _layer_bwd(dx2, p, sv, lay, tag, early=None, late=None):
    x = sv["x"]
    s, d = x.shape
    nc = s // CHUNK
    w = d
    dnh, smh = lay.dn_heads, lay.ssm_heads
    r = smh // SSM_GROUPS
    gn = SSM_GROUPS * SSM_STATE
    cb = {k: v // LANES for k, v in lay.cols.items()}
    proj = sv["proj"]
    g = {}
    dx2_b = dx2.astype(MXU_DTYPE)
    du = _matmul(dx2_b, p["w_down"], tb=True, name=f"mm_down_dx_{tag}", out_dtypes=(MXU_DTYPE,),
                 epilogue=lambda acc, uu: (acc * (2.0 * jnp.maximum(uu, 0.0)),), extras=(sv["u"],))
    g["w_down"] = _matmul(sv["act"], dx2_b, ta=True, name=f"mm_down_dw_{tag}", out_dtypes=(BF16,)).reshape(N_DEV, -1, d)
    g["w_up"] = _matmul(sv["h2"], du, ta=True, name=f"mm_up_dw_{tag}", out_dtypes=(BF16,), col_shards=N_DEV)
    dh2 = _matmul(du, p["w_up"], tb=True, name=f"mm_up_dx_{tag}")
    dx1, g["norm_mlp"] = _rms_bwd(sv["x1"], p["norm_mlp"], dh2, dx2, name=f"rms_mlp_bwd_{tag}")
    dx1_b = dx1.astype(MXU_DTYPE)
    dmerged = _matmul(dx1_b, p["w_out"], tb=True, name=f"mm_out_dx_{tag}")
    g["w_out"] = _matmul(sv["merged"], dx1_b, ta=True, name=f"mm_out_dw_{tag}", out_dtypes=(BF16,)).reshape(N_DEV, -1, d)
    dproj = lax.empty((s, lay.padded), MXU_DTYPE)
    dproj3, dproj = _merge_bwd(sv["proj3"], proj, cb[8], d, dmerged, dproj, name=f"merge_bwd_{tag}")
    branches = (sv["y_dn"], sv["o_sb"], sv["y_ssm"])
    dwb, dbr = [], []
    for i, br in enumerate(branches):
        dp_i = dproj3[:, i * d:(i + 1) * d]
        dwb.append(_matmul(br, dp_i, ta=True, name=f"mm_branch{i}_dw_{tag}", out_dtypes=(BF16,)).reshape(N_DEV, -1, d))
        dbr.append(_matmul(dp_i, p["w_branch"][i], tb=True, name=f"mm_branch{i}_dx_{tag}"))
    g["w_branch"] = jnp.stack(dwb, axis=1)
    dy_dn, do_sb, dy_ssm = dbr
    if early is not None:
        dy_ssm = early(g, dy_ssm)
    dy_ssd, dxs_skip, dproj, ddexp, g["ssm_norm_w"] = _ssm_post_bwd(
        sv["y_ssd"], sv["xbc"], proj, cb[5], sv["dexp"], p["ssm_norm_w"], dy_ssm, dproj, name=f"ssm_post_bwd_{tag}")
    g["ssm_d"] = ddexp.reshape(smh, SSM_HEAD_DIM).sum(axis=1)
    dxs, dbm, dcm, ddt_rows, dalog, ddtb = _ssd_bwd(
        sv["xbc"], sv["dt_rows"], sv["ssm_alog"], sv["ssm_dtb"], sv["ssm_states"], dy_ssd, name=f"ssd_bwd_{tag}")
    g["ssm_a_log"] = dalog.reshape(smh)
    g["ssm_dt_bias"] = ddtb.reshape(smh)
    dxbc_post = jnp.concatenate([dxs + dxs_skip, dbm, dcm], axis=1)
    dproj, g["ssm_conv_w"], dcb = _conv_bwd(proj, cb[6], p["ssm_conv_w"], p["ssm_conv_b"].reshape(1, -1), 0, dxbc_post,
                                            dproj, name=f"ssm_conv_bwd_{tag}")
    g["ssm_conv_b"] = dcb.reshape(-1)
    ddt = ddt_rows.transpose(0, 2, 1, 3).reshape(smh, s).T
    dqkv_sb = _sb_bwd(proj, cb[4], w, sv["sb_r"], do_sb, name=f"sb_bwd_{tag}")
    dproj = lax.dynamic_update_slice(dproj, jnp.concatenate([t.astype(MXU_DTYPE) for t in dqkv_sb], axis=1), (0, lay.cols[4]))
    do_dn, dproj, g["dn_norm_w"] = _dn_post_bwd(sv["o_dn"], proj, cb[1], p["dn_norm_w"], dy_dn, dproj,
                                                name=f"dn_post_bwd_{tag}")
    dqkv_dn, da_rows, db_rows, dal, ddtb_dn = _dn_bwd(
        sv["dn_qkv"], sv["a_rows"], sv["b_rows"], sv["dn_alog"], sv["dn_dtb"], sv["dn_states"], sv["dn_inv"], do_dn,
        name=f"dn_chunk_bwd_{tag}")
    g["dn_a_log"] = dal.reshape(dnh)
    g["dn_dt_bias"] = ddtb_dn.reshape(dnh)
    zero_b = jnp.zeros((1, 3 * w), F32)
    dproj, g["dn_conv_w"], _ = _conv_bwd(proj, cb[0], p["dn_conv_w"], zero_b, 2 * dnh, dqkv_dn, dproj,
                                         name=f"dn_conv_bwd_{tag}")
    da = da_rows.reshape(dnh, s).T
    db = db_rows.reshape(dnh, s).T
    dsmall = jnp.concatenate([da, db, ddt, jnp.zeros((s, LANES - lay.n_small), F32)], axis=1).astype(MXU_DTYPE)
    dproj = lax.dynamic_update_slice(dproj, dsmall, (0, lay.small_col))
    g["w_in"] = lay.to_shards(_matmul(sv["h1"], dproj, ta=True, name=f"mm_in_dw_{tag}", out_dtypes=(BF16,)))
    if late is not None:
        dproj = late(g, dproj)
    dh1 = _matmul(dproj, p["w_in"], tb=True, name=f"mm_in_dx_{tag}")
    dx0, g["norm_mix"] = _rms_bwd(x, p["norm_mix"], dh1, dx1, name=f"rms_mix_bwd_{tag}")
    return dx0, g


BIG = ("w_in", "w_branch", "w_out", "w_up", "w_down")
CONV = ("dn_conv_w", "ssm_conv_w")
SMALL = ("norm_mix", "dn_conv_w", "dn_a_log", "dn_dt_bias", "dn_norm_w", "ssm_conv_w", "ssm_conv_b", "ssm_a_log",
         "ssm_dt_bias", "ssm_d", "ssm_norm_w", "norm_mlp", "norm_final")
WEIGHTS = ("norm_mix", "w_in", "dn_conv_w", "dn_a_log", "dn_dt_bias", "dn_norm_w", "ssm_conv_w", "ssm_conv_b", "ssm_a_log",
           "ssm_dt_bias", "ssm_d", "ssm_norm_w", "w_branch", "w_out", "norm_mlp", "w_up", "w_down", "norm_final")
SHARD_AXIS = {"w_in": 2, "dn_conv_w": 2, "ssm_conv_w": 2, "w_branch": 2, "w_out": 1, "w_up": 2, "w_down": 1}


def _to_shards(full, axis):
    shp = full.shape
    n = shp[axis] // N_DEV
    t = full.reshape(shp[:axis] + (N_DEV, n) + shp[axis + 1:])
    return jnp.moveaxis(t, axis, 0)


def _from_shards(parts, axis):
    t = jnp.moveaxis(parts, 0, axis)
    shp = t.shape
    return t.reshape(shp[:axis] + (shp[axis] * shp[axis + 1],) + shp[axis + 2:])


def _unshard(parts, axis, *, name):
    shard = parts.shape[1:]
    nd = len(shard)
    if axis == 0:
        return parts.reshape((N_DEV * shard[0],) + shard[1:])

    def copy_block(i_ref, o_ref):
        o_ref[...] = i_ref[...]

    if axis == nd - 1:
        rows, n = math.prod(shard[:-1]), shard[-1]
        out = pl.pallas_call(
            copy_block, grid=(N_DEV,),
            in_specs=[pl.BlockSpec((None, rows, n), lambda j: (j, 0, 0))],
            out_specs=pl.BlockSpec((rows, n), lambda j: (0, j)),
            out_shape=jax.ShapeDtypeStruct((rows, N_DEV * n), parts.dtype),
            compiler_params=_cparams(1), name=name,
        )(parts.reshape(N_DEV, rows, n))
        return out.reshape(shard[:-1] + (N_DEV * n,))
    assert axis == nd - 2, (parts.shape, axis)
    a, n, c = math.prod(shard[:-2]), shard[-2], shard[-1]
    out = pl.pallas_call(
        copy_block, grid=(N_DEV, a),
        in_specs=[pl.BlockSpec((None, None, n, c), lambda j, i: (j, i, 0, 0))],
        out_specs=pl.BlockSpec((None, n, c), lambda j, i: (i, j, 0)),
        out_shape=jax.ShapeDtypeStruct((a, N_DEV * n, c), parts.dtype),
        compiler_params=_cparams(2), name=name,
    )(parts.reshape(N_DEV, a, n, c))
    return out.reshape(shard[:-2] + (N_DEV * n, c))


def _step(w, m, v, x, target):
    s, d = x.shape
    lay = _Layout(d)
    me = 4 * lax.axis_index("x") + 2 * lax.axis_index("y") + lax.axis_index("c")

    def shard(n, l):
        return w[n][l].astype(BF16) if n in BIG else w[n][l]

    def empty_land(a):
        return lax.empty((N_DEV,) + a.shape, a.dtype)

    def with_own(land, own):
        return lax.dynamic_update_index_in_dim(land, own, me, 0)

    def assemble(n, parts, l):
        return lay.from_shards(parts) if n == "w_in" else _unshard(parts, SHARD_AXIS[n] - 1, name=f"unshard_{n}_l{l}")

    small_names = tuple(n for n in WEIGHTS if n not in BIG + CONV + ("norm_final",))

    first, rest = ("w_in",) + CONV, BIG[1:]
    got = _all_gather([shard(n, 0) for n in first], name="gather_l0_first")
    whole, sliced = (True, None), (False, None)
    names_a, names_b = rest, BIG + CONV
    srcs_a, srcs_b = [shard(n, 0) for n in names_a], [shard(n, 1) for n in names_b]
    sem_sa, sem_ra, srcs_a, lands_a, w_in0 = _split_start(
        srcs_a, [empty_land(a) for a in srcs_a], [whole] * len(srcs_a), got[0], name="gather_l0_rest_start")
    sem_sb, sem_rb, srcs_b, lands_b, w_in0 = _split_start(
        srcs_b, [empty_land(a) for a in srcs_b], [whole] * len(srcs_b), w_in0, name="gather_l1_start")
    p0 = {n: w[n][0] for n in small_names}
    p0.update({n: assemble(n, g, 0) for n, g in zip(first, [w_in0] + list(got[1:]))})

    def late_l0(after):
        lands = _split_wait(sem_sa, sem_ra, srcs_a, lands_a, [whole] * len(srcs_a), after, name="gather_l0_rest_wait")
        return {n: assemble(n, with_own(ld, s_), 0) for n, ld, s_ in zip(names_a, lands, srcs_a)}

    h, sv0 = _layer_fwd(x, p0, lay, "l0", late=late_l0)
    lands = _split_wait(sem_sb, sem_rb, srcs_b, lands_b, [whole] * len(srcs_b), h, name="gather_l1_wait")
    p1 = {n: w[n][1] for n in small_names}
    p1.update({n: assemble(n, with_own(ld, s_), 1) for n, ld, s_ in zip(names_b, lands, srcs_b)})
    h, sv1 = _layer_fwd(h, p1, lay, "l1")
    loss, dh, g_norm_final = _final_loss(h, w["norm_final"], target, name="final_loss")
    grads = [None] * DEPTH
    dh, grads[1] = _layer_bwd(dh, p1, sv1, lay, "l1")

    def exchange_start(names, g, carry, tag):
        srcs = [g[n] for n in names]
        return _split_start(srcs, [lax.empty(a.shape, a.dtype) for a in srcs], [sliced] * len(srcs), carry,
                            name=f"grad_{tag}_start")

    def exchange_wait(names, started, after, tag):
        sem_s, sem_r, srcs, lands_, _ = started
        lands_ = _split_wait(sem_s, sem_r, srcs, lands_, [sliced] * len(srcs), after, name=f"grad_{tag}_wait")
        return {n: with_own(ld, lax.dynamic_index_in_dim(s_, me, 0, keepdims=False)) for n, ld, s_ in zip(names, lands_, srcs)}

    x1_started = exchange_start(BIG, grads[1], dh, "l1")
    pending = {}

    def early_l0(g, carry):
        pending["rest"] = exchange_start(rest, g, carry, "l0_rest")
        return pending["rest"][4]

    def late_bwd_l0(g, carry):
        pending["w_in"] = exchange_start(("w_in",), g, carry, "l0_w_in")
        return pending["w_in"][4]

    grad_x, grads[0] = _layer_bwd(x1_started[4], p0, sv0, lay, "l0", early=early_l0, late=late_bwd_l0)

    out = {"grad": {}, "delta": {}, "new_m": {}, "new_v": {}}
    parts1 = exchange_wait(BIG, x1_started, grad_x, "l1")
    res1 = {n: _sum_adamw(parts1[n], w[n], m[n], v[n], 1, None, name=f"sum_adamw_{n}_l1") for n in BIG}
    parts0 = exchange_wait(rest, pending["rest"], res1["w_in"][0], "l0_rest")
    res0 = {n: _sum_adamw(parts0[n], w[n], m[n], v[n], 0, res1[n], name=f"sum_adamw_{n}_l0") for n in rest}
    parts0 = exchange_wait(("w_in",), pending["w_in"], res0["w_down"][0], "l0_w_in")
    res0["w_in"] = _sum_adamw(parts0["w_in"], w["w_in"], m["w_in"], v["w_in"], 0, res1["w_in"], name="sum_adamw_w_in_l0")
    for n in BIG:
        for key, a in zip(("grad", "delta", "new_m", "new_v"), res0[n]):
            out[key][n] = a

    gfull = {n: jnp.stack([grads[l][n] for l in range(DEPTH)]) for n in SMALL if n != "norm_final"}
    gfull["norm_final"] = g_norm_final
    small_send = _pack([gfull[n] for n in SMALL] + [loss.reshape(1)], F32)
    small_recv = _all_gather([small_send], name="gather_small_grads", after=res0["w_in"][0])[0]
    small_sum = _sum_parts(small_recv, name="sum_small")
    small_full = _unpack(small_sum, [gfull[n].shape for n in SMALL] + [(1,)])
    loss_total = small_full[-1][0]
    gsmall = {}
    for n, a in zip(SMALL, small_full[:-1]):
        if n in SHARD_AXIS:
            a = lax.dynamic_index_in_dim(_to_shards(a, SHARD_AXIS[n]), me, axis=0, keepdims=False)
        gsmall[n] = a
    small_shapes = [w[n].shape for n in SMALL]
    ws, gs, ms, vs = (_pack([t[n] for n in SMALL], F32) for t in (w, gsmall, m, v))
    ds, m1s, v1s = _adamw(ws, gs, ms, vs, name="adamw_small")
    for n in SMALL:
        out["grad"][n] = gsmall[n]
    for key, packed in (("delta", ds), ("new_m", m1s), ("new_v", v1s)):
        for n, a in zip(SMALL, _unpack(packed, small_shapes)):
            out[key][n] = a
    return loss_total, grad_x, out


def kernel(x, norm_mix, w_in, dn_conv_w, dn_a_log, dn_dt_bias, dn_norm_w, ssm_conv_w, ssm_conv_b, ssm_a_log, ssm_dt_bias, ssm_d, ssm_norm_w, w_branch, w_out, norm_mlp, w_up, w_down, norm_final, loss_target, m_norm_mix, m_w_in, m_dn_conv_w, m_dn_a_log, m_dn_dt_bias, m_dn_norm_w, m_ssm_conv_w, m_ssm_conv_b, m_ssm_a_log, m_ssm_dt_bias, m_ssm_d, m_ssm_norm_w, m_w_branch, m_w_out, m_norm_mlp, m_w_up, m_w_down, m_norm_final, v_norm_mix, v_w_in, v_dn_conv_w, v_dn_a_log, v_dn_dt_bias, v_dn_norm_w, v_ssm_conv_w, v_ssm_conv_b, v_ssm_a_log, v_ssm_dt_bias, v_ssm_d, v_ssm_norm_w, v_w_branch, v_w_out, v_norm_mlp, v_w_up, v_w_down, v_norm_final):
    w = dict(norm_mix=norm_mix, w_in=w_in, dn_conv_w=dn_conv_w, dn_a_log=dn_a_log, dn_dt_bias=dn_dt_bias, dn_norm_w=dn_norm_w,
             ssm_conv_w=ssm_conv_w, ssm_conv_b=ssm_conv_b, ssm_a_log=ssm_a_log, ssm_dt_bias=ssm_dt_bias, ssm_d=ssm_d,
             ssm_norm_w=ssm_norm_w, w_branch=w_branch, w_out=w_out, norm_mlp=norm_mlp, w_up=w_up, w_down=w_down,
             norm_final=norm_final)
    m = dict(norm_mix=m_norm_mix, w_in=m_w_in, dn_conv_w=m_dn_conv_w, dn_a_log=m_dn_a_log, dn_dt_bias=m_dn_dt_bias,
             dn_norm_w=m_dn_norm_w, ssm_conv_w=m_ssm_conv_w, ssm_conv_b=m_ssm_conv_b, ssm_a_log=m_ssm_a_log,
             ssm_dt_bias=m_ssm_dt_bias, ssm_d=m_ssm_d, ssm_norm_w=m_ssm_norm_w, w_branch=m_w_branch, w_out=m_w_out,
             norm_mlp=m_norm_mlp, w_up=m_w_up, w_down=m_w_down, norm_final=m_norm_final)
    v = dict(norm_mix=v_norm_mix, w_in=v_w_in, dn_conv_w=v_dn_conv_w, dn_a_log=v_dn_a_log, dn_dt_bias=v_dn_dt_bias,
             dn_norm_w=v_dn_norm_w, ssm_conv_w=v_ssm_conv_w, ssm_conv_b=v_ssm_conv_b, ssm_a_log=v_ssm_a_log,
             ssm_dt_bias=v_ssm_dt_bias, ssm_d=v_ssm_d, ssm_norm_w=v_ssm_norm_w, w_branch=v_w_branch, w_out=v_w_out,
             norm_mlp=v_norm_mlp, w_up=v_w_up, w_down=v_w_down, norm_final=v_norm_final)
    loss, grad_x, out = _step(w, m, v, x[0], loss_target[0])
    return (loss, grad_x[None], *[out["grad"][n] for n in WEIGHTS], *[out["delta"][n] for n in WEIGHTS],
            *[out["new_m"][n] for n in WEIGHTS], *[out["new_v"][n] for n in WEIGHTS])
```

```python
import functools
import math

import jax
import jax.numpy as jnp
from jax import lax
from jax.experimental import pallas as pl
from jax.experimental.pallas import tpu as pltpu

F32 = jnp.float32
BF16 = jnp.bfloat16
MXU_DTYPE = BF16
HIGHEST = lax.Precision.HIGHEST

N_DEV = 8
DEPTH = 2
EPS = 1e-6
CONV_K = 4
DN_HEAD_DIM = 128
SB_HEAD_DIM = 64
SSM_HEAD_DIM = 64
SSM_STATE = 128
SSM_GROUPS = 4
CHUNK = 64
SB_BLOCK = 128
LANES = 128
ADAM_LR, ADAM_B1, ADAM_B2, ADAM_EPS, ADAM_WD, ADAM_STEP = 0.001, 0.9, 0.999, 1e-08, 0.01, 10
NEG_BIG = -1e30
DN_HEADS_PER_STEP = 8
SSD_GROUPS_PER_STEP = 1
SB_UNROLL = 4
SB_SPLIT = 2
CHUNK_PREC = lax.Precision.HIGH

ARB = "arbitrary"


def _cparams(n_axes):
    return pltpu.CompilerParams(dimension_semantics=(ARB,) * n_axes)


def _softplus(x):
    return jnp.maximum(x, 0.0) + jnp.log1p(jnp.exp(-jnp.abs(x)))


def _sigmoid(x):
    return jax.nn.sigmoid(x)


def _silu(x):
    return x * _sigmoid(x)


def _silu_and_grad(x):
    s = _sigmoid(x)
    return x * s, s * (1.0 + x * (1.0 - s))


def _dot(a, b, dims, prec=None):
    return lax.dot_general(a, b, (dims, ((), ())), precision=prec, preferred_element_type=F32)


NN = ((1,), (0,))
NT = ((1,), (1,))
TN = ((0,), (0,))


def _hdot(a, b, dims=NN):
    return _dot(a, b, dims, CHUNK_PREC)


def _bdot(a, b, dims=NN):
    return _dot(a.astype(MXU_DTYPE), b.astype(MXU_DTYPE), dims)


def _split_dot(a, m_bf16, nsplit=3):
    out = None
    rem = a
    for _ in range(nsplit):
        piece = rem.astype(BF16)
        rem = rem - piece.astype(F32)
        term = _dot(piece, m_bf16, NN)
        out = term if out is None else out + term
    return out


def _pick(n, pref):
    for t in pref:
        if n % t == 0:
            return t
    return n


def _matmul(a, b, *, ta=False, tb=False, name, epilogue=None, extras=(), out_dtypes=(F32,), col_shards=1,
            tm=None, tn=None, tk=None):
    m, k = (a.shape[1], a.shape[0]) if ta else a.shape
    k2, n = (b.shape[1], b.shape[0]) if tb else b.shape
    assert k == k2, (a.shape, b.shape, ta, tb)
    ncs = n // col_shards
    tm = tm or _pick(m, (1920, 1024, 512, 256, 128))
    tn = tn or _pick(ncs, (1920, 1024, 640, 512, 384, 256, 128))
    tk = tk or _pick(k, (1920, 1024, 640, 512, 256, 128))
    nk = k // tk
    a_spec = pl.BlockSpec((tk, tm), lambda i, j, kk: (kk, i)) if ta else pl.BlockSpec((tm, tk), lambda i, j, kk: (i, kk))
    b_spec = pl.BlockSpec((tn, tk), lambda i, j, kk: (j, kk)) if tb else pl.BlockSpec((tk, tn), lambda i, j, kk: (kk, j))
    e_spec = pl.BlockSpec((tm, tn), lambda i, j, kk: (i, j))
    if col_shards == 1:
        o_spec, o_shape = e_spec, (m, n)
    else:
        per = ncs // tn
        o_spec, o_shape = pl.BlockSpec((None, tm, tn), lambda i, j, kk: (j // per, i, j % per)), (col_shards, m, ncs)
    dims = (((0,) if ta else (1,)), ((1,) if tb else (0,)))
    n_extra = len(extras)
    n_out = len(out_dtypes)

    def body(*refs):
        a_ref, b_ref = refs[0], refs[1]
        extra_refs = refs[2:2 + n_extra]
        out_refs = refs[2 + n_extra:2 + n_extra + n_out]
        acc_ref = refs[-1]
        kk = pl.program_id(2)

        @pl.when(kk == 0)
        def _():
            acc_ref[...] = jnp.zeros_like(acc_ref)

        acc_ref[...] += _dot(a_ref[...].astype(MXU_DTYPE), b_ref[...].astype(MXU_DTYPE), dims)

        @pl.when(kk == nk - 1)
        def _():
            acc = acc_ref[...]
            outs = (acc,) if epilogue is None else epilogue(acc, *[r[...] for r in extra_refs])
            for o_ref, o in zip(out_refs, outs):
                o_ref[...] = o.astype(o_ref.dtype)

    outs = pl.pallas_call(
        body,
        grid=(m // tm, n // tn, nk),
        in_specs=[a_spec, b_spec] + [e_spec] * n_extra,
        out_specs=[o_spec] * n_out,
        out_shape=[jax.ShapeDtypeStruct(o_shape, dt) for dt in out_dtypes],
        scratch_shapes=[pltpu.VMEM((tm, tn), F32)],
        compiler_params=pltpu.CompilerParams(dimension_semantics=("parallel", "parallel", ARB)),
        name=name,
    )(a, b, *extras)
    return outs[0] if n_out == 1 else tuple(outs)


def _rms_fwd(x, w, *, name, tm=256):
    s, d = x.shape
    out_dtype = MXU_DTYPE

    def body(x_ref, w_ref, o_ref):
        xv = x_ref[...]
        r = lax.rsqrt(jnp.mean(xv * xv, axis=-1, keepdims=True) + EPS)
        o_ref[...] = (xv * r * w_ref[...]).astype(o_ref.dtype)

    return pl.pallas_call(
        body, grid=(s // tm,),
        in_specs=[pl.BlockSpec((tm, d), lambda i: (i, 0)), pl.BlockSpec((1, d), lambda i: (0, 0))],
        out_specs=pl.BlockSpec((tm, d), lambda i: (i, 0)),
        out_shape=jax.ShapeDtypeStruct((s, d), out_dtype),
        compiler_params=_cparams(1), name=name,
    )(x, w.reshape(1, d))


def _rms_bwd(x, w, dh, dres, *, name, tm=256):
    s, d = x.shape

    def body(x_ref, w_ref, dh_ref, dres_ref, dx_ref, dw_ref):
        xv = x_ref[...]
        r = lax.rsqrt(jnp.mean(xv * xv, axis=-1, keepdims=True) + EPS)
        xh = xv * r
        dhv = dh_ref[...].astype(F32)
        dxn = dhv * w_ref[...]
        dx = r * (dxn - xh * jnp.mean(dxn * xh, axis=-1, keepdims=True))
        dx_ref[...] = dres_ref[...] + dx

        @pl.when(pl.program_id(0) == 0)
        def _():
            dw_ref[...] = jnp.zeros_like(dw_ref)

        dw_ref[...] += jnp.sum(dhv * xh, axis=0, keepdims=True)

    dx, dw = pl.pallas_call(
        body, grid=(s // tm,),
        in_specs=[pl.BlockSpec((tm, d), lambda i: (i, 0)), pl.BlockSpec((1, d), lambda i: (0, 0)),
                  pl.BlockSpec((tm, d), lambda i: (i, 0)), pl.BlockSpec((tm, d), lambda i: (i, 0))],
        out_specs=[pl.BlockSpec((tm, d), lambda i: (i, 0)), pl.BlockSpec((1, d), lambda i: (0, 0))],
        out_shape=[jax.ShapeDtypeStruct((s, d), F32), jax.ShapeDtypeStruct((1, d), F32)],
        compiler_params=_cparams(1), name=name,
    )(x, w.reshape(1, d), dh, dres)
    return dx, dw.reshape(d)


def _final_loss(x, w, target, *, name, tm=256):
    s, d = x.shape

    def body(x_ref, w_ref, t_ref, loss_ref, dx_ref, dw_ref):
        xv = x_ref[...]
        r = lax.rsqrt(jnp.mean(xv * xv, axis=-1, keepdims=True) + EPS)
        xh = xv * r
        err = xh * w_ref[...] - t_ref[...]
        dy = err * (1.0 / d)
        dxn = dy * w_ref[...]
        dx_ref[...] = r * (dxn - xh * jnp.mean(dxn * xh, axis=-1, keepdims=True))

        @pl.when(pl.program_id(0) == 0)
        def _():
            dw_ref[...] = jnp.zeros_like(dw_ref)
            loss_ref[...] = jnp.zeros_like(loss_ref)

        dw_ref[...] += jnp.sum(dy * xh, axis=0, keepdims=True)
        row = jnp.sum(err * err, axis=1, keepdims=True) * (0.5 / d)
        loss_ref[...] += jnp.sum(row, axis=0, keepdims=True)

    loss, dx, dw = pl.pallas_call(
        body, grid=(s // tm,),
        in_specs=[pl.BlockSpec((tm, d), lambda i: (i, 0)), pl.BlockSpec((1, d), lambda i: (0, 0)),
                  pl.BlockSpec((tm, d), lambda i: (i, 0))],
        out_specs=[pl.BlockSpec((1, 1), lambda i: (0, 0)), pl.BlockSpec((tm, d), lambda i: (i, 0)),
                   pl.BlockSpec((1, d), lambda i: (0, 0))],
        out_shape=[jax.ShapeDtypeStruct((1, 1), F32), jax.ShapeDtypeStruct((s, d), F32), jax.ShapeDtypeStruct((1, d), F32)],
        compiler_params=_cparams(1), name=name,
    )(x, w.reshape(1, d), target)
    return loss[0, 0], dx, dw.reshape(d)


def _shift_down(x, sh, t_idx):
    return jnp.where(t_idx >= sh, pltpu.roll(x, sh, 0), 0.0)


def _shift_up(x, sh, t_idx, s):
    return jnp.where(t_idx < s - sh, pltpu.roll(x, s - sh, 0), 0.0)


def _conv_pre(x, w_rows, b, t_idx):
    c = w_rows[CONV_K - 1] * x + b
    for sh in range(1, CONV_K):
        c = c + w_rows[CONV_K - 1 - sh] * _shift_down(x, sh, t_idx)
    return c


def _conv_fwd(src, col0, w, b, n_l2, *, name):
    s = src.shape[0]
    c_tot = w.shape[1]
    nblk = c_tot // LANES

    def body(x_ref, w_ref, b_ref, o_ref):
        j = pl.program_id(0)
        t_idx = lax.broadcasted_iota(jnp.int32, (s, LANES), 0)
        w_rows = [w_ref[kk:kk + 1, :] for kk in range(CONV_K)]
        y = _silu(_conv_pre(x_ref[...], w_rows, b_ref[...], t_idx))
        if n_l2 > 0:
            yn = y * lax.rsqrt(jnp.sum(y * y, axis=1, keepdims=True) + EPS)
            y = jnp.where(j < n_l2, yn, y)
        o_ref[...] = y

    return pl.pallas_call(
        body, grid=(nblk,),
        in_specs=[pl.BlockSpec((s, LANES), lambda j: (0, col0 + j)), pl.BlockSpec((CONV_K, LANES), lambda j: (0, j)),
                  pl.BlockSpec((1, LANES), lambda j: (0, j))],
        out_specs=pl.BlockSpec((s, LANES), lambda j: (0, j)),
        out_shape=jax.ShapeDtypeStruct((s, c_tot), F32),
        compiler_params=_cparams(1), name=name,
    )(src, w, b)


def _conv_bwd(src, col0, w, b, n_l2, dout, into, *, name):
    s = src.shape[0]
    c_tot = w.shape[1]
    nblk = c_tot // LANES

    def body(x_ref, w_ref, b_ref, do_ref, into_ref, dx_ref, dw_ref, db_ref):
        j = pl.program_id(0)
        t_idx = lax.broadcasted_iota(jnp.int32, (s, LANES), 0)
        xv = x_ref[...]
        w_rows = [w_ref[kk:kk + 1, :] for kk in range(CONV_K)]
        c = _conv_pre(xv, w_rows, b_ref[...], t_idx)
        dy = do_ref[...]
        y, y_grad = _silu_and_grad(c)
        if n_l2 > 0:
            r = lax.rsqrt(jnp.sum(y * y, axis=1, keepdims=True) + EPS)
            dyn = r * dy - y * (r * r * r) * jnp.sum(dy * y, axis=1, keepdims=True)
            dy = jnp.where(j < n_l2, dyn, dy)
        dc = dy * y_grad
        dx = w_rows[CONV_K - 1] * dc
        rows = [None] * CONV_K
        rows[CONV_K - 1] = jnp.sum(dc * xv, axis=0, keepdims=True)
        for sh in range(1, CONV_K):
            dx = dx + w_rows[CONV_K - 1 - sh] * _shift_up(dc, sh, t_idx, s)
            rows[CONV_K - 1 - sh] = jnp.sum(dc * _shift_down(xv, sh, t_idx), axis=0, keepdims=True)
        dx_ref[...] = dx.astype(dx_ref.dtype)
        for kk in range(CONV_K):
            dw_ref[kk:kk + 1, :] = rows[kk]
        db_ref[...] = jnp.sum(dc, axis=0, keepdims=True)

    return pl.pallas_call(
        body, grid=(nblk,),
        in_specs=[pl.BlockSpec((s, LANES), lambda j: (0, col0 + j)), pl.BlockSpec((CONV_K, LANES), lambda j: (0, j)),
                  pl.BlockSpec((1, LANES), lambda j: (0, j)), pl.BlockSpec((s, LANES), lambda j: (0, j)), ANY],
        out_specs=[pl.BlockSpec((s, LANES), lambda j: (0, col0 + j)), pl.BlockSpec((CONV_K, LANES), lambda j: (0, j)),
                   pl.BlockSpec((1, LANES), lambda j: (0, j))],
        out_shape=[jax.ShapeDtypeStruct(into.shape, into.dtype), jax.ShapeDtypeStruct((CONV_K, c_tot), F32),
                   jax.ShapeDtypeStruct((1, c_tot), F32)],
        input_output_aliases={4: 0},
        compiler_params=_cparams(1), name=name,
    )(src, w, b, dout, into)


def _chunk_masks(c):
    ii = lax.broadcasted_iota(jnp.int32, (c, c), 0)
    jj = lax.broadcasted_iota(jnp.int32, (c, c), 1)
    return ii, jj


def _row_to_col(row, eye):
    return jnp.sum(jnp.where(eye, row, 0.0), axis=1, keepdims=True)


def _each(f, *lists):
    return [f(*xs) for xs in zip(*lists)]


@jax.custom_vjp
def _nilpotent_inverse(nmats):
    c = nmats[0].shape[0]
    ii, jj = _chunk_masks(c)
    xinv = _each(lambda n: jnp.where(ii == jj, 1.0, 0.0) + n, nmats)
    pw = nmats
    for _ in range(int(math.log2(c)) - 1):
        pw = _each(lambda p: _dot(p, p, NN, HIGHEST), pw)
        xinv = _each(lambda x, p: x + _dot(x, p, NN, HIGHEST), xinv, pw)
    return xinv


def _nilpotent_inverse_fwd(nmats):
    xinv = _nilpotent_inverse(nmats)
    return xinv, xinv


def _nilpotent_inverse_bwd(xinv, cts):
    left = _each(lambda x, ct: _dot(x, ct, TN, HIGHEST), xinv, cts)
    return (_each(lambda l_, x: _dot(l_, x, NT, HIGHEST), left, xinv),)


_nilpotent_inverse.defvjp(_nilpotent_inverse_fwd, _nilpotent_inverse_bwd)


@jax.custom_vjp
def _saved_inverse(nmats, saved):
    return saved


def _saved_inverse_fwd(nmats, saved):
    return saved, saved


def _saved_inverse_bwd(xinv, cts):
    return _nilpotent_inverse_bwd(xinv, cts) + (_each(jnp.zeros_like, xinv),)


_saved_inverse.defvjp(_saved_inverse_fwd, _saved_inverse_bwd)


def _dn_chunk(q, k, v, a_row, b_row, alog, dtb, s0, saved_inverse=None):
    c = q[0].shape[0]
    ii, jj = _chunk_masks(c)
    causal, strict, eye = ii >= jj, ii > jj, ii == jj
    g_row = _each(lambda al, a, dt: -jnp.exp(al) * _softplus(a + dt), alog, a_row, dtb)
    beta_col = _each(lambda b: _row_to_col(_sigmoid(b), eye), b_row)
    g_col = _each(lambda g: _row_to_col(g, eye), g_row)
    gc_col = _each(lambda g: jnp.sum(jnp.where(causal, g, 0.0), axis=1, keepdims=True), g_row)
    gc_row = _each(lambda g: jnp.sum(jnp.where(jj >= ii, g, 0.0), axis=0, keepdims=True), g_col)
    decay = _each(lambda gc, gr: jnp.exp(jnp.where(causal, gc - gr, NEG_BIG)), gc_col, gc_row)
    kb = _each(jnp.multiply, k, beta_col)
    vb = _each(jnp.multiply, v, beta_col)
    nmat = _each(lambda kb_, k_, dc: -jnp.where(strict, _dot(kb_, k_, NT, HIGHEST) * dc, 0.0), kb, k, decay)
    xinv = _nilpotent_inverse(nmat) if saved_inverse is None else _saved_inverse(nmat, saved_inverse)
    egc = _each(jnp.exp, gc_col)
    u = _each(lambda x, vb_: _dot(x, vb_, NN, HIGHEST), xinv, vb)
    w = _each(lambda x, kb_, e: _dot(x, kb_ * e, NN, HIGHEST), xinv, kb, egc)
    qs = _each(lambda q_: q_ * (q_.shape[1] ** -0.5), q)
    attn = _each(lambda q_, k_, dc: _hdot(q_, k_, NT) * dc, qs, k, decay)
    gl = _each(lambda g: jnp.sum(g, axis=1, keepdims=True), g_row)
    kd = _each(lambda k_, gl_, gc: k_ * jnp.exp(gl_ - gc), k, gl, gc_col)
    v_new = _each(lambda u_, w_, s: u_ - _hdot(w_, s), u, w, s0)
    o = _each(lambda q_, e, s, at, vn: _hdot(q_ * e, s) + _hdot(at, vn), qs, egc, s0, attn, v_new)
    s1 = _each(lambda s, gl_, kd_, vn: s * jnp.exp(gl_) + _hdot(kd_, vn, TN), s0, gl, kd, v_new)
    return (o, s1), xinv


def _dn_specs(nh, nc, hb, rev):
    n_of = (lambda n: nc - 1 - n) if rev else (lambda n: n)
    ng = nh // hb
    qkv = [pl.BlockSpec((CHUNK, hb * DN_HEAD_DIM), (lambda h, n, o=o: (n_of(n), o * ng + h))) for o in range(3)]
    row = pl.BlockSpec((hb, None, 1, CHUNK), lambda h, n: (h, n_of(n), 0, 0))
    scal = pl.BlockSpec((hb, 1, 1), lambda h, n: (h, 0, 0))
    o_spec = pl.BlockSpec((CHUNK, hb * DN_HEAD_DIM), lambda h, n: (n_of(n), h))
    st = pl.BlockSpec((hb, None, DN_HEAD_DIM, DN_HEAD_DIM), lambda h, n: (h, n_of(n), 0, 0))
    inv = pl.BlockSpec((hb, None, CHUNK, CHUNK), lambda h, n: (h, n_of(n), 0, 0))
    return qkv, row, scal, o_spec, st, inv


def _dn_fwd(qkv, a_rows, b_rows, alog, dtb, *, name):
    s = qkv.shape[0]
    nh, nc = a_rows.shape[0], a_rows.shape[1]
    hb = min(DN_HEADS_PER_STEP, nh)
    qkv_specs, row, scal, o_spec, st, inv = _dn_specs(nh, nc, hb, False)
    hd = DN_HEAD_DIM

    def body(q_ref, k_ref, v_ref, a_ref, b_ref, al_ref, dt_ref, o_ref, st_ref, inv_ref, state):
        @pl.when(pl.program_id(1) == 0)
        def _():
            state[...] = jnp.zeros_like(state)

        cols = [slice(h * hd, (h + 1) * hd) for h in range(hb)]
        s0 = [state[h] for h in range(hb)]
        for h in range(hb):
            st_ref[h] = s0[h]
        (o, s1), xinv = _dn_chunk(
            [q_ref[:, cl] for cl in cols], [k_ref[:, cl] for cl in cols], [v_ref[:, cl] for cl in cols],
            [a_ref[h] for h in range(hb)], [b_ref[h] for h in range(hb)],
            [al_ref[h] for h in range(hb)], [dt_ref[h] for h in range(hb)], s0)
        for h in range(hb):
            o_ref[:, cols[h]] = o[h]
            inv_ref[h] = xinv[h]
            state[h] = s1[h]

    return pl.pallas_call(
        body, grid=(nh // hb, nc),
        in_specs=qkv_specs + [row, row, scal, scal],
        out_specs=[o_spec, st, inv],
        out_shape=[jax.ShapeDtypeStruct((s, nh * hd), F32), jax.ShapeDtypeStruct((nh, nc, hd, hd), F32),
                   jax.ShapeDtypeStruct((nh, nc, CHUNK, CHUNK), F32)],
        scratch_shapes=[pltpu.VMEM((hb, hd, hd), F32)],
        compiler_params=_cparams(2), name=name,
    )(qkv, qkv, qkv, a_rows, b_rows, alog, dtb)


def _dn_bwd(qkv, a_rows, b_rows, alog, dtb, states, inverses, do, *, name):
    s = qkv.shape[0]
    nh, nc = a_rows.shape[0], a_rows.shape[1]
    hb = min(DN_HEADS_PER_STEP, nh)
    qkv_specs, row, scal, o_spec, st, inv = _dn_specs(nh, nc, hb, True)
    hd = DN_HEAD_DIM

    assert hb == nh, "dq | dk | dv are written as one [S, 3W] array: all heads in one grid step"
    w = nh * hd

    def body(q_ref, k_ref, v_ref, a_ref, b_ref, al_ref, dt_ref, st_ref, inv_ref, do_ref,
             dqkv_ref, da_ref, db_ref, dal_ref, ddt_ref, dstate):
        @pl.when(pl.program_id(1) == 0)
        def _():
            dstate[...] = jnp.zeros_like(dstate)
            dal_ref[...] = jnp.zeros_like(dal_ref)
            ddt_ref[...] = jnp.zeros_like(ddt_ref)

        cols = [slice(h * hd, (h + 1) * hd) for h in range(hb)]
        heads = range(hb)
        args = ([q_ref[:, cl] for cl in cols], [k_ref[:, cl] for cl in cols], [v_ref[:, cl] for cl in cols],
                [a_ref[h] for h in heads], [b_ref[h] for h in heads], [al_ref[h] for h in heads],
                [dt_ref[h] for h in heads], [st_ref[h] for h in heads])
        saved = [inv_ref[h] for h in heads]
        _, vjp, _ = jax.vjp(lambda *a: _dn_chunk(*a, saved_inverse=saved), *args, has_aux=True)
        dq, dk, dv, da, db, dal, ddt, ds0 = vjp(([do_ref[:, cl] for cl in cols], [dstate[h] for h in heads]))
        for h in heads:
            dqkv_ref[:, h * hd:(h + 1) * hd] = dq[h]
            dqkv_ref[:, w + h * hd:w + (h + 1) * hd] = dk[h]
            dqkv_ref[:, 2 * w + h * hd:2 * w + (h + 1) * hd] = dv[h]
            da_ref[h] = da[h]
            db_ref[h] = db[h]
            dal_ref[h] += dal[h]
            ddt_ref[h] += ddt[h]
            dstate[h] = ds0[h]

    n_of = lambda n: nc - 1 - n
    outs = pl.pallas_call(
        body, grid=(nh // hb, nc),
        in_specs=qkv_specs + [row, row, scal, scal, st, inv, o_spec],
        out_specs=[pl.BlockSpec((CHUNK, 3 * w), lambda h, n: (n_of(n), 0)), row, row, scal, scal],
        out_shape=[jax.ShapeDtypeStruct((s, 3 * w), F32)]
        + [jax.ShapeDtypeStruct(a_rows.shape, F32)] * 2 + [jax.ShapeDtypeStruct((nh, 1, 1), F32)] * 2,
        scratch_shapes=[pltpu.VMEM((hb, hd, hd), F32)],
        compiler_params=_cparams(2), name=name,
    )(qkv, qkv, qkv, a_rows, b_rows, alog, dtb, states, inverses, do)
    return outs


def _dn_post_fwd(o, src, gate_col0, nw, *, name, tm=256):
    s, w = o.shape
    hd = DN_HEAD_DIM
    gc = gate_col0 * LANES // w

    def body(o_ref, g_ref, w_ref, y_ref):
        for h in range(w // hd):
            cols = slice(h * hd, (h + 1) * hd)
            ov = o_ref[:, cols]
            r = lax.rsqrt(jnp.mean(ov * ov, axis=-1, keepdims=True) + EPS)
            y_ref[:, cols] = (ov * r * w_ref[...] * _silu(g_ref[:, cols])).astype(y_ref.dtype)

    blk = pl.BlockSpec((tm, w), lambda i: (i, 0))
    return pl.pallas_call(
        body, grid=(s // tm,),
        in_specs=[blk, pl.BlockSpec((tm, w), lambda i: (i, gc)), pl.BlockSpec((1, hd), lambda i: (0, 0))],
        out_specs=blk, out_shape=jax.ShapeDtypeStruct((s, w), MXU_DTYPE),
        compiler_params=_cparams(1), name=name,
    )(o, src, nw.reshape(1, hd))


def _dn_post_bwd(o, src, gate_col0, nw, dy, into, *, name, tm=256):
    s, w = o.shape
    hd = DN_HEAD_DIM
    gc = gate_col0 * LANES // w

    def body(o_ref, g_ref, w_ref, dy_ref, into_ref, do_ref, dg_ref, dw_ref):
        @pl.when(pl.program_id(0) == 0)
        def _():
            dw_ref[...] = jnp.zeros_like(dw_ref)

        dw = jnp.zeros((1, hd), F32)
        for h in range(w // hd):
            cols = slice(h * hd, (h + 1) * hd)
            ov, gv, dyv = o_ref[:, cols], g_ref[:, cols], dy_ref[:, cols]
            r = lax.rsqrt(jnp.mean(ov * ov, axis=-1, keepdims=True) + EPS)
            oh = ov * r
            sg, sg_grad = _silu_and_grad(gv)
            dn = dyv * sg
            dg_ref[:, cols] = (dyv * (oh * w_ref[...]) * sg_grad).astype(dg_ref.dtype)
            don = dn * w_ref[...]
            do_ref[:, cols] = r * (don - oh * jnp.mean(don * oh, axis=-1, keepdims=True))
            dw = dw + jnp.sum(dn * oh, axis=0, keepdims=True)
        dw_ref[...] += dw

    blk = pl.BlockSpec((tm, w), lambda i: (i, 0))
    wspec = pl.BlockSpec((1, hd), lambda i: (0, 0))
    gate_blk = pl.BlockSpec((tm, w), lambda i: (i, gc))
    do, dg, dw = pl.pallas_call(
        body, grid=(s // tm,),
        in_specs=[blk, gate_blk, wspec, blk, ANY],
        out_specs=[blk, gate_blk, wspec],
        out_shape=[jax.ShapeDtypeStruct((s, w), F32), jax.ShapeDtypeStruct(into.shape, into.dtype),
                   jax.ShapeDtypeStruct((1, hd), F32)],
        input_output_aliases={4: 1},
        compiler_params=_cparams(1), name=name,
    )(o, src, nw.reshape(1, hd), dy, into)
    return do, dg, dw.reshape(hd)


def _sb_consts():
    r2 = lax.broadcasted_iota(jnp.int32, (2 * SB_BLOCK, SB_BLOCK), 0)
    c2 = lax.broadcasted_iota(jnp.int32, (2 * SB_BLOCK, SB_BLOCK), 1)
    r = lax.broadcasted_iota(jnp.int32, (SB_BLOCK, SB_BLOCK), 0)
    c = lax.broadcasted_iota(jnp.int32, (SB_BLOCK, SB_BLOCK), 1)
    lm0 = c < SB_HEAD_DIM
    m_gt = jnp.where(r > c, 1.0, 0.0).astype(BF16)
    m_lt = jnp.where(r < c, 1.0, 0.0).astype(BF16)
    return r2, c2, lm0, m_gt, m_lt


def _sb_stack(x, lm0):
    return jnp.concatenate([jnp.where(lm0, x, 0.0), jnp.where(lm0, 0.0, x)], axis=0)


def _sb_unstack(x2, lm0):
    return jnp.where(lm0, x2[:SB_BLOCK], x2[SB_BLOCK:])


def _sb_fwd(src, col0, width, *, name):
    s = src.shape[0]
    nq = s // SB_BLOCK
    npair = width // LANES
    scale = SB_HEAD_DIM ** -0.5
    nu = math.gcd(SB_UNROLL, nq)

    def body(q_ref, k_ref, v_ref, o_ref, w_hbm, stage, sems):
        p, i = pl.program_id(0), pl.program_id(1)
        r2, c2, lm0, m_gt, _ = _sb_consts()
        t_glob = i * SB_BLOCK + (r2 & (SB_BLOCK - 1))
        q2 = (_sb_stack(q_ref[...], lm0) * scale).astype(MXU_DTYPE)

        t = p * nq + i
        half = t % 2
        ngrp = nq // nu

        def save(half_, grp, pp, ii):
            return pltpu.make_async_copy(stage.at[half_, grp], w_hbm.at[pp, ii, grp], sems.at[half_, grp])

        def drain(half_, pp, ii):
            for grp in range(ngrp):
                @pl.when(grp <= ii // nu)
                def _():
                    save(half_, grp, pp, ii).wait()

        def group(base, carry, masked):
            o2, rsum = carry
            js = [base + nu - 1 - u for u in range(nu)]
            offs = [pl.multiple_of(j * SB_BLOCK, SB_BLOCK) for j in js]
            zs = [_dot(q2, k_ref[pl.ds(off, SB_BLOCK), :].astype(MXU_DTYPE), NT) for off in offs]
            ts = [jnp.log(1.0 + jnp.exp(-jnp.abs(z))) for z in zs]
            lks = [-(jnp.maximum(z, 0.0) + t) for z, t in zip(zs, ts)]
            if masked:
                masks = [(j * SB_BLOCK + c2) < t_glob for j in js]
                lks = [jnp.where(mk, lk, 0.0) for mk, lk in zip(masks, lks)]
            sufs = [_split_dot(lk, m_gt, SB_SPLIT) for lk in lks]
            rs = [rsum]
            for lk in lks:
                rs.append(rs[-1] + jnp.sum(lk, axis=1, keepdims=True))
            wgts = [jnp.exp((jnp.minimum(z, 0.0) - t) + r_ + sf) for z, t, r_, sf in zip(zs, ts, rs, sufs)]
            if masked:
                wgts = [jnp.where(mk, wg, 0.0) for mk, wg in zip(masks, wgts)]
            wbs = [wg.astype(MXU_DTYPE) for wg in wgts]
            grp = base // nu
            for u, wb in enumerate(wbs):
                stage[half, grp, nu - 1 - u] = wb
            save(half, grp, p, i).start()
            for off, wb in zip(offs, wbs):
                o2 = o2 + _dot(wb, v_ref[pl.ds(off, SB_BLOCK), :].astype(MXU_DTYPE), NN)
            return o2, rs[-1]

        top0 = (i // nu) * nu
        last = i // nu
        carry = group(top0, (jnp.zeros((2 * SB_BLOCK, LANES), F32), jnp.zeros((2 * SB_BLOCK, 1), F32)), True)
        o2, _ = lax.fori_loop(1, last + 1, lambda g, cr: group(top0 - nu * g, cr, False), carry)
        o_ref[...] = _sb_unstack(o2, lm0)

        @pl.when(t >= 1)
        def _():
            drain(1 - half, (t - 1) // nq, (t - 1) % nq)

        @pl.when(t == npair * nq - 1)
        def _():
            drain(half, p, i)

    blk = pl.BlockSpec((SB_BLOCK, LANES), lambda p, i: (i, p))
    return pl.pallas_call(
        body, grid=(npair, nq),
        in_specs=[pl.BlockSpec((SB_BLOCK, LANES), lambda p, i: (i, col0 + p)),
                  pl.BlockSpec((s, LANES), lambda p, i: (0, col0 + npair + p)),
                  pl.BlockSpec((s, LANES), lambda p, i: (0, col0 + 2 * npair + p))],
        out_specs=[blk, ANY],
        out_shape=[jax.ShapeDtypeStruct((s, width), F32),
                   jax.ShapeDtypeStruct((npair, nq, nq // nu, nu, 2 * SB_BLOCK, LANES), MXU_DTYPE)],
        scratch_shapes=[pltpu.VMEM((2, nq // nu, nu, 2 * SB_BLOCK, LANES), MXU_DTYPE),
                        pltpu.SemaphoreType.DMA((2, nq // nu))],
        compiler_params=_cparams(2), name=name,
    )(src, src, src)


def _sb_bwd(src, col0, width, weights, do, *, name):
    s = src.shape[0]
    nq = s // SB_BLOCK
    npair = width // LANES
    scale = SB_HEAD_DIM ** -0.5
    nu = math.gcd(SB_UNROLL, nq)

    def body(q_ref, k_ref, v_ref, w_hbm, do_ref, dq_ref, dk_ref, dv_ref, stage, sems):
        p, i = pl.program_id(0), pl.program_id(1)

        @pl.when(i == 0)
        def _():
            dk_ref[...] = jnp.zeros_like(dk_ref)
            dv_ref[...] = jnp.zeros_like(dv_ref)

        r2, c2, lm0, _, m_lt = _sb_consts()
        t_glob = i * SB_BLOCK + (r2 & (SB_BLOCK - 1))
        q2 = (_sb_stack(q_ref[...], lm0) * scale).astype(MXU_DTYPE)
        do2 = _sb_stack(do_ref[...], lm0).astype(MXU_DTYPE)

        ngrp = nq // nu

        def load(half_, grp, pp, ii):
            return pltpu.make_async_copy(w_hbm.at[pp, ii, grp], stage.at[half_, grp], sems.at[half_, grp])

        def fetch_step(half_, pp, ii):
            for grp in range(ngrp):
                @pl.when(grp <= ii // nu)
                def _():
                    load(half_, grp, pp, ii).start()

        def group(g, carry, masked, slot):
            dq2, csum = carry
            js = [nu * g + u for u in range(nu)]
            offs = [pl.multiple_of(j * SB_BLOCK, SB_BLOCK) for j in js]
            kbs = [k_ref[pl.ds(off, SB_BLOCK), :].astype(MXU_DTYPE) for off in offs]
            zs = [_dot(q2, kb, NT) for kb in kbs]
            dws = [_dot(do2, v_ref[pl.ds(off, SB_BLOCK), :].astype(MXU_DTYPE), NT) for off in offs]
            wbs = [stage[slot[0], slot[1], u] for u in range(nu)]
            sigs = [_sigmoid(z) for z in zs]
            dlogas = [wb.astype(F32) * dw for wb, dw in zip(wbs, dws)]
            pres = [_split_dot(dl, m_lt, SB_SPLIT) for dl in dlogas]
            dlks = []
            for dl, pre in zip(dlogas, pres):
                dlks.append(csum + pre)
                csum = csum + jnp.sum(dl, axis=1, keepdims=True)
            if masked:
                dlks = [jnp.where((j * SB_BLOCK + c2) < t_glob, dlk, 0.0) for j, dlk in zip(js, dlks)]
            dzbs = [(dl * (1.0 - sg) - dlk * sg).astype(MXU_DTYPE) for dl, sg, dlk in zip(dlogas, sigs, dlks)]
            for off, dzb, wb, kb in zip(offs, dzbs, wbs, kbs):
                dk_ref[pl.ds(off, SB_BLOCK), :] += _dot(dzb, q2, TN)
                dv_ref[pl.ds(off, SB_BLOCK), :] += _dot(wb, do2, TN)
                dq2 = dq2 + _dot(dzb, kb, NN)
            return dq2, csum

        t = p * nq + i
        half = t % 2

        @pl.when(t == 0)
        def _():
            fetch_step(0, p, i)

        @pl.when(t + 1 < npair * nq)
        def _():
            fetch_step(1 - half, (t + 1) // nq, (t + 1) % nq)

        def step(g, carry):
            load(half, g, p, i).wait()
            return group(g, carry, False, (half, g))

        last = i // nu
        carry = lax.fori_loop(0, last, step, (jnp.zeros((2 * SB_BLOCK, LANES), F32), jnp.zeros((2 * SB_BLOCK, 1), F32)))
        load(half, last, p, i).wait()
        dq2, _ = group(last, carry, True, (half, last))
        dq_ref[...] = _sb_unstack(dq2, lm0) * scale

    blk = pl.BlockSpec((SB_BLOCK, LANES), lambda p, i: (i, p))
    full = pl.BlockSpec((s, LANES), lambda p, i: (0, p))
    return pl.pallas_call(
        body, grid=(npair, nq),
        in_specs=[pl.BlockSpec((SB_BLOCK, LANES), lambda p, i: (i, col0 + p)),
                  pl.BlockSpec((s, LANES), lambda p, i: (0, col0 + npair + p)),
                  pl.BlockSpec((s, LANES), lambda p, i: (0, col0 + 2 * npair + p)),
                  ANY, blk],
        out_specs=[blk, full, full],
        out_shape=[jax.ShapeDtypeStruct((s, width), F32)] * 3,
        scratch_shapes=[pltpu.VMEM((2, nq // nu, nu, 2 * SB_BLOCK, LANES), MXU_DTYPE),
                        pltpu.SemaphoreType.DMA((2, nq // nu))],
        compiler_params=_cparams(2), name=name,
    )(src, src, src, weights, do)


def _ssd_group(xs, dt_rows, alogs, dtbs, bms, cms, h0s):
    c = bms[0].shape[0]
    per = len(xs) // len(bms)
    ii, jj = _chunk_masks(c)
    causal, eye = ii >= jj, ii == jj
    grp = lambda per_group: [t for t in per_group for _ in range(per)]
    scores, bm, cm = grp(_each(lambda c_, b_: _hdot(c_, b_, NT), cms, bms)), grp(bms), grp(cms)
    dt_r = _each(lambda dt, b: _softplus(dt + b), dt_rows, dtbs)
    a_r = _each(lambda al, dt: -jnp.exp(al) * dt, alogs, dt_r)
    dt_col = _each(lambda dt: _row_to_col(dt, eye), dt_r)
    a_col = _each(lambda a: _row_to_col(a, eye), a_r)
    ac_col = _each(lambda a: jnp.sum(jnp.where(causal, a, 0.0), axis=1, keepdims=True), a_r)
    ac_row = _each(lambda a: jnp.sum(jnp.where(jj >= ii, a, 0.0), axis=0, keepdims=True), a_col)
    lmat = _each(lambda c_, r_: jnp.exp(jnp.where(causal, c_ - r_, NEG_BIG)), ac_col, ac_row)
    xdt = _each(jnp.multiply, xs, dt_col)
    al = _each(lambda a: jnp.sum(a, axis=1, keepdims=True), a_r)
    ys = _each(lambda sc, lm, xd, cm_, h0, ac: _hdot(sc * lm, xd) + _hdot(cm_, h0, NT) * jnp.exp(ac),
               scores, lmat, xdt, cm, h0s, ac_col)
    h1s = _each(lambda h0, al_, xd, ac, bm_: h0 * jnp.exp(al_) + _hdot(xd * jnp.exp(al_ - ac), bm_, TN),
                h0s, al, xdt, ac_col, bm)
    return ys, h1s


def _ssd_specs(ng, nc, r, gb, rev):
    n_of = (lambda n: nc - 1 - n) if rev else (lambda n: n)
    xw, bw = gb * r * SSM_HEAD_DIM, gb * SSM_STATE
    b0, c0 = (ng * r * SSM_HEAD_DIM) // bw, (ng * r * SSM_HEAD_DIM + ng * SSM_STATE) // bw
    x_spec = pl.BlockSpec((CHUNK, xw), lambda g, n: (n_of(n), g))
    b_spec = pl.BlockSpec((CHUNK, bw), lambda g, n: (n_of(n), b0 + g))
    c_spec = pl.BlockSpec((CHUNK, bw), lambda g, n: (n_of(n), c0 + g))
    dt_spec = pl.BlockSpec((gb, None, r, CHUNK), lambda g, n: (g, n_of(n), 0, 0))
    sc_spec = pl.BlockSpec((gb, r, 1), lambda g, n: (g, 0, 0))
    st_spec = pl.BlockSpec((gb, None, r, SSM_HEAD_DIM, SSM_STATE), lambda g, n: (g, n_of(n), 0, 0, 0))
    bc_out = pl.BlockSpec((CHUNK, bw), lambda g, n: (n_of(n), g))
    return x_spec, b_spec, c_spec, dt_spec, sc_spec, st_spec, x_spec, bc_out


def _ssd_refs(gb, r, x_ref, b_ref, c_ref, dt_ref, al_ref, db_ref):
    p, n = SSM_HEAD_DIM, SSM_STATE
    heads = [(g, h) for g in range(gb) for h in range(r)]
    xs = [x_ref[:, (g * r + h) * p:(g * r + h + 1) * p] for g, h in heads]
    dts = [dt_ref[g, h:h + 1, :] for g, h in heads]
    als = [al_ref[g, h:h + 1, :] for g, h in heads]
    dbs = [db_ref[g, h:h + 1, :] for g, h in heads]
    bms = [b_ref[:, g * n:(g + 1) * n] for g in range(gb)]
    cms = [c_ref[:, g * n:(g + 1) * n] for g in range(gb)]
    return heads, xs, dts, als, dbs, bms, cms


def _ssd_fwd(xbc, dt_rows, alog, dtb, *, name):
    s = xbc.shape[0]
    ng, nc, r = dt_rows.shape[0], dt_rows.shape[1], dt_rows.shape[2]
    w = ng * r * SSM_HEAD_DIM
    gb = math.gcd(SSD_GROUPS_PER_STEP, ng)
    x_spec, b_spec, c_spec, dt_spec, sc_spec, st_spec, y_spec, _ = _ssd_specs(ng, nc, r, gb, False)
    p = SSM_HEAD_DIM

    def body(x_ref, b_ref, c_ref, dt_ref, al_ref, db_ref, y_ref, st_ref, state):
        @pl.when(pl.program_id(1) == 0)
        def _():
            state[...] = jnp.zeros_like(state)

        st_ref[...] = state[...]
        heads, xs, dts, als, dbs, bms, cms = _ssd_refs(gb, r, x_ref, b_ref, c_ref, dt_ref, al_ref, db_ref)
        ys, h1s = _ssd_group(xs, dts, als, dbs, bms, cms, [state[g, h] for g, h in heads])
        for i, (g, h) in enumerate(heads):
            y_ref[:, (g * r + h) * p:(g * r + h + 1) * p] = ys[i]
            state[g, h] = h1s[i]

    return pl.pallas_call(
        body, grid=(ng // gb, nc),
        in_specs=[x_spec, b_spec, c_spec, dt_spec, sc_spec, sc_spec],
        out_specs=[y_spec, st_spec],
        out_shape=[jax.ShapeDtypeStruct((s, w), F32), jax.ShapeDtypeStruct((ng, nc, r, p, SSM_STATE), F32)],
        scratch_shapes=[pltpu.VMEM((gb, r, p, SSM_STATE), F32)],
        compiler_params=_cparams(2), name=name,
    )(xbc, xbc, xbc, dt_rows, alog, dtb)


def _ssd_bwd(xbc, dt_rows, alog, dtb, states, dy, *, name):
    s = xbc.shape[0]
    ng, nc, r = dt_rows.shape[0], dt_rows.shape[1], dt_rows.shape[2]
    w = ng * r * SSM_HEAD_DIM
    gb = math.gcd(SSD_GROUPS_PER_STEP, ng)
    x_spec, b_spec, c_spec, dt_spec, sc_spec, st_spec, y_spec, bc_out = _ssd_specs(ng, nc, r, gb, True)
    p = SSM_HEAD_DIM

    def body(x_ref, b_ref, c_ref, dt_ref, al_ref, db_ref, st_ref, dy_ref,
             dx_ref, dbm_ref, dcm_ref, ddt_ref, dal_ref, ddb_ref, dstate):
        @pl.when(pl.program_id(1) == 0)
        def _():
            dstate[...] = jnp.zeros_like(dstate)
            dal_ref[...] = jnp.zeros_like(dal_ref)
            ddb_ref[...] = jnp.zeros_like(ddb_ref)

        heads, xs, dts, als, dbs, bms, cms = _ssd_refs(gb, r, x_ref, b_ref, c_ref, dt_ref, al_ref, db_ref)
        _, vjp = jax.vjp(_ssd_group, xs, dts, als, dbs, bms, cms, [st_ref[g, h] for g, h in heads])
        dys = [dy_ref[:, (g * r + h) * p:(g * r + h + 1) * p] for g, h in heads]
        dxs, ddts, dals, ddbs, dbms, dcms, dh0s = vjp((dys, [dstate[g, h] for g, h in heads]))
        for g in range(gb):
            dbm_ref[:, g * SSM_STATE:(g + 1) * SSM_STATE] = dbms[g]
            dcm_ref[:, g * SSM_STATE:(g + 1) * SSM_STATE] = dcms[g]
        for i, (g, h) in enumerate(heads):
            dx_ref[:, (g * r + h) * p:(g * r + h + 1) * p] = dxs[i]
            ddt_ref[g, h:h + 1, :] = ddts[i]
            dal_ref[g, h:h + 1, :] += dals[i]
            ddb_ref[g, h:h + 1, :] += ddbs[i]
            dstate[g, h] = dh0s[i]

    gn = ng * SSM_STATE
    return pl.pallas_call(
        body, grid=(ng // gb, nc),
        in_specs=[x_spec, b_spec, c_spec, dt_spec, sc_spec, sc_spec, st_spec, y_spec],
        out_specs=[y_spec, bc_out, bc_out, dt_spec, sc_spec, sc_spec],
        out_shape=[jax.ShapeDtypeStruct((s, w), F32), jax.ShapeDtypeStruct((s, gn), F32), jax.ShapeDtypeStruct((s, gn), F32),
                   jax.ShapeDtypeStruct(dt_rows.shape, F32), jax.ShapeDtypeStruct((ng, r, 1), F32),
                   jax.ShapeDtypeStruct((ng, r, 1), F32)],
        scratch_shapes=[pltpu.VMEM((gb, r, p, SSM_STATE), F32)],
        compiler_params=_cparams(2), name=name,
    )(xbc, xbc, xbc, dt_rows, alog, dtb, states, dy)


def _ssm_post_fwd(y, xbc, src, z_col0, dexp, nw, *, name, tm=256):
    s, w = y.shape
    gw = w // SSM_GROUPS
    zc = z_col0 * LANES // gw

    def body(y_ref, x_ref, z_ref, d_ref, w_ref, o_ref):
        yy = (y_ref[...] + x_ref[...] * d_ref[...]) * _silu(z_ref[...])
        r = lax.rsqrt(jnp.mean(yy * yy, axis=-1, keepdims=True) + EPS)
        o_ref[...] = (yy * r * w_ref[...]).astype(o_ref.dtype)

    blk = pl.BlockSpec((tm, gw), lambda g, i: (i, g))
    vec = pl.BlockSpec((1, gw), lambda g, i: (0, g))
    return pl.pallas_call(
        body, grid=(SSM_GROUPS, s // tm),
        in_specs=[blk, blk, pl.BlockSpec((tm, gw), lambda g, i: (i, zc + g)), vec, vec],
        out_specs=blk, out_shape=jax.ShapeDtypeStruct((s, w), MXU_DTYPE),
        compiler_params=_cparams(2), name=name,
    )(y, xbc, src, dexp.reshape(1, w), nw.reshape(1, w))


def _ssm_post_bwd(y, xbc, src, z_col0, dexp, nw, dout, into, *, name, tm=256):
    s, w = y.shape
    gw = w // SSM_GROUPS
    zc = z_col0 * LANES // gw

    def body(y_ref, x_ref, z_ref, d_ref, w_ref, do_ref, into_ref, dy_ref, dx_ref, dz_ref, dd_ref, dw_ref):
        xv, zv, dv = x_ref[...], z_ref[...], d_ref[...]
        pre = y_ref[...] + xv * dv
        sz, sz_grad = _silu_and_grad(zv)
        yy = pre * sz
        r = lax.rsqrt(jnp.mean(yy * yy, axis=-1, keepdims=True) + EPS)
        yh = yy * r
        dov = do_ref[...]
        dyn = dov * w_ref[...]
        dyy = r * (dyn - yh * jnp.mean(dyn * yh, axis=-1, keepdims=True))
        dpre = dyy * sz
        dy_ref[...] = dpre
        dx_ref[...] = dpre * dv
        dz_ref[...] = (dyy * pre * sz_grad).astype(dz_ref.dtype)

        @pl.when(pl.program_id(1) == 0)
        def _():
            dd_ref[...] = jnp.zeros_like(dd_ref)
            dw_ref[...] = jnp.zeros_like(dw_ref)

        dd_ref[...] += jnp.sum(dpre * xv, axis=0, keepdims=True)
        dw_ref[...] += jnp.sum(dov * yh, axis=0, keepdims=True)

    blk = pl.BlockSpec((tm, gw), lambda g, i: (i, g))
    vec = pl.BlockSpec((1, gw), lambda g, i: (0, g))
    z_blk = pl.BlockSpec((tm, gw), lambda g, i: (i, zc + g))
    dy, dx, dz, dd, dw = pl.pallas_call(
        body, grid=(SSM_GROUPS, s // tm),
        in_specs=[blk, blk, z_blk, vec, vec, blk, ANY],
        out_specs=[blk, blk, z_blk, vec, vec],
        out_shape=[jax.ShapeDtypeStruct((s, w), F32), jax.ShapeDtypeStruct((s, w), F32),
                   jax.ShapeDtypeStruct(into.shape, into.dtype), jax.ShapeDtypeStruct((1, w), F32),
                   jax.ShapeDtypeStruct((1, w), F32)],
        input_output_aliases={6: 2},
        compiler_params=_cparams(2), name=name,
    )(y, xbc, src, dexp.reshape(1, w), nw.reshape(1, w), dout, into)
    return dy, dx, dz, dd.reshape(w), dw.reshape(w)


def _merge_fwd(proj3, src, gate_col0, d, *, name, tm=256):
    s = proj3.shape[0]
    nb = proj3.shape[1] // d
    gc = gate_col0 * LANES // d

    def body(*refs):
        p_refs, g_refs, o_ref = refs[:nb], refs[nb:2 * nb], refs[-1]
        acc = None
        for p_ref, g_ref in zip(p_refs, g_refs):
            term = _sigmoid(g_ref[...]) * p_ref[...]
            acc = term if acc is None else acc + term
        o_ref[...] = acc.astype(o_ref.dtype)

    p_specs = [pl.BlockSpec((tm, d), lambda i, b=b: (i, b)) for b in range(nb)]
    g_specs = [pl.BlockSpec((tm, d), lambda i, b=b: (i, gc + b)) for b in range(nb)]
    return pl.pallas_call(
        body, grid=(s // tm,), in_specs=p_specs + g_specs,
        out_specs=pl.BlockSpec((tm, d), lambda i: (i, 0)), out_shape=jax.ShapeDtypeStruct((s, d), MXU_DTYPE),
        compiler_params=_cparams(1), name=name,
    )(*([proj3] * nb), *([src] * nb))


def _merge_bwd(proj3, src, gate_col0, d, dmerged, into, *, name, tm=256):
    s = proj3.shape[0]
    nb = proj3.shape[1] // d
    gc = gate_col0 * LANES // d

    def body(p_ref, g_ref, dm_ref, into_ref, dp_ref, dg_ref):
        sg = _sigmoid(g_ref[...])
        dm = dm_ref[...]
        dp_ref[...] = (dm * sg).astype(dp_ref.dtype)
        dg_ref[...] = (dm * p_ref[...] * sg * (1.0 - sg)).astype(dg_ref.dtype)

    blk = pl.BlockSpec((tm, d), lambda i, b: (i, b))
    gate_blk = pl.BlockSpec((tm, d), lambda i, b: (i, gc + b))
    return pl.pallas_call(
        body, grid=(s // tm, nb),
        in_specs=[blk, gate_blk, pl.BlockSpec((tm, d), lambda i, b: (i, 0)), ANY],
        out_specs=[blk, gate_blk],
        out_shape=[jax.ShapeDtypeStruct(proj3.shape, MXU_DTYPE), jax.ShapeDtypeStruct(into.shape, into.dtype)],
        input_output_aliases={3: 1},
        compiler_params=_cparams(2), name=name,
    )(proj3, src, dmerged, into)


ANY = pl.BlockSpec(memory_space=pl.ANY)
MESH = pl.DeviceIdType.MESH


def _all_gather(shards, *, name, after=None):
    nt = len(shards)
    n_after = 0 if after is None else 1

    def body(*refs):
        x_refs, out_refs = refs[:nt], refs[nt + n_after:2 * nt + n_after]
        send_sems, recv_sems, local_sems = refs[2 * nt + n_after:]
        x, y, c = lax.axis_index("x"), lax.axis_index("y"), lax.axis_index("c")
        me, sibling = (x, y, c), (x, y, 1 - c)
        chips = [(1 - x, y), (x, 1 - y), (1 - x, 1 - y)]

        def slot(t, px, py, pc):
            return out_refs[t].at[4 * px + 2 * py + pc]

        def copy(t, k, block, to, from_input=False):
            return pltpu.make_async_remote_copy(
                src_ref=x_refs[t] if from_input else slot(t, *block), dst_ref=slot(t, *block),
                send_sem=send_sems.at[7 * t + k], recv_sem=recv_sems.at[7 * t + k], device_id=to, device_id_type=MESH)

        mine = [pltpu.make_async_copy(x_refs[t], slot(t, *me), local_sems.at[t]) for t in range(nt)]
        for cp in mine:
            cp.start()
        first = [copy(t, 0, me, sibling, True) for t in range(nt)]
        first += [copy(t, 1 + j, me, (*chip, c), True) for j, chip in enumerate(chips) for t in range(nt)]
        for cp in first:
            cp.start()
        passed = []
        for j, chip in enumerate(chips):
            for t in range(nt):
                copy(t, 1 + j, (*chip, c), me).wait_recv()
                fwd = copy(t, 4 + j, (*chip, c), sibling)
                fwd.start()
                passed.append(fwd)
        for t in range(nt):
            copy(t, 0, sibling, me).wait_recv()
            for j, chip in enumerate(chips):
                copy(t, 4 + j, (*chip, 1 - c), me).wait_recv()
        for cp in first + passed:
            cp.wait_send()
        for cp in mine:
            cp.wait()

    return pl.pallas_call(
        body, out_shape=[jax.ShapeDtypeStruct((N_DEV,) + a.shape, a.dtype) for a in shards],
        in_specs=[ANY] * (nt + n_after), out_specs=[ANY] * nt,
        scratch_shapes=[pltpu.SemaphoreType.DMA((7 * nt,)), pltpu.SemaphoreType.DMA((7 * nt,)),
                        pltpu.SemaphoreType.DMA((nt,))],
        name=name,
    )(*shards, *([] if after is None else [after]))


def _grad_exchange(bigs, small, *, name):
    nl = len(bigs[0])
    flat = [a for per_layer in bigs for a in per_layer]
    nslot = len(flat)

    def body(*refs):
        in_refs, small_ref = refs[:nslot], refs[nslot]
        out_refs, smallr_ref = refs[nslot + 1:nslot + 1 + len(bigs)], refs[nslot + 1 + len(bigs)]
        send_sems, recv_sems, local_sems = refs[nslot + 2 + len(bigs):]
        x, y, c = lax.axis_index("x"), lax.axis_index("y"), lax.axis_index("c")
        me = 4 * x + 2 * y + c
        local = [pltpu.make_async_copy(in_refs[i].at[me], out_refs[i // nl].at[me, i % nl], local_sems.at[i])
                 for i in range(nslot)]
        local.append(pltpu.make_async_copy(small_ref, smallr_ref.at[me], local_sems.at[nslot]))
        for cp in local:
            cp.start()
        copies = []
        for k in range(1, N_DEV):
            px = x ^ ((k >> 2) & 1)
            py = y ^ ((k >> 1) & 1)
            pc = c ^ (k & 1)
            peer = 4 * px + 2 * py + pc
            for i in range(nslot + 1):
                sem = 7 * i + (k - 1)
                src = in_refs[i].at[peer] if i < nslot else small_ref
                dst = out_refs[i // nl].at[me, i % nl] if i < nslot else smallr_ref.at[me]
                copies.append(pltpu.make_async_remote_copy(
                    src_ref=src, dst_ref=dst, send_sem=send_sems.at[sem], recv_sem=recv_sems.at[sem],
                    device_id=(px, py, pc), device_id_type=MESH))
        for cp in copies:
            cp.start()
        for cp in copies:
            cp.wait_recv()
        for cp in copies:
            cp.wait_send()
        for cp in local:
            cp.wait()

    out_shape = [jax.ShapeDtypeStruct((N_DEV, nl) + per_layer[0].shape[1:], per_layer[0].dtype) for per_layer in bigs]
    out_shape.append(jax.ShapeDtypeStruct((N_DEV,) + small.shape, small.dtype))
    nsem = 7 * (nslot + 1)
    outs = pl.pallas_call(
        body, out_shape=out_shape,
        in_specs=[ANY] * (nslot + 1), out_specs=[ANY] * (len(bigs) + 1),
        scratch_shapes=[pltpu.SemaphoreType.DMA((nsem,)), pltpu.SemaphoreType.DMA((nsem,)),
                        pltpu.SemaphoreType.DMA((nslot + 1,))],
        name=name,
    )(*flat, small)
    return outs[:-1], outs[-1]


HBM = pl.BlockSpec(memory_space=pltpu.HBM)
SEM = pl.BlockSpec(memory_space=pltpu.SEMAPHORE)
EFFECT = pltpu.SideEffectType.DATAFLOW_SIDE_EFFECTING


def _peers():
    x, y, c = lax.axis_index("x"), lax.axis_index("y"), lax.axis_index("c")
    peers = []
    for k in range(1, N_DEV):
        px, py, pc = x ^ ((k >> 2) & 1), y ^ ((k >> 1) & 1), c ^ (k & 1)
        peers.append(((px, py, pc), 4 * px + 2 * py + pc))
    return 4 * x + 2 * y + c, peers


def _split_copies(slots, src_refs, land_refs, send_sems, recv_sems):
    me, peers = _peers()
    copies = []
    for t, (whole, layer) in enumerate(slots):
        dst = land_refs[t].at[me] if layer is None else land_refs[t].at[me, layer]
        for k, (dev, lin) in enumerate(peers):
            copies.append(pltpu.make_async_remote_copy(
                src_ref=src_refs[t] if whole else src_refs[t].at[lin], dst_ref=dst,
                send_sem=send_sems.at[7 * t + k], recv_sem=recv_sems.at[7 * t + k], device_id=dev, device_id_type=MESH))
    return copies


def _split_start(srcs, lands, slots, carry, *, name):
    n = len(srcs)

    def body(*refs):
        copies = _split_copies(slots, refs[:n], refs[n:2 * n], refs[2 * n + 1], refs[2 * n + 2])
        for cp in copies:
            cp.start()

    def hbm(a):
        return pltpu.HBM(a.shape, a.dtype)

    outs = pl.pallas_call(
        body, name=name,
        out_shape=[pltpu.SemaphoreType.DMA((7 * n,)), pltpu.SemaphoreType.DMA((7 * n,))]
        + [hbm(a) for a in srcs] + [hbm(a) for a in lands] + [hbm(carry)],
        in_specs=[HBM] * (2 * n + 1), out_specs=[SEM, SEM] + [HBM] * (2 * n + 1),
        input_output_aliases={i: 2 + i for i in range(2 * n + 1)},
        compiler_params=pltpu.CompilerParams(has_side_effects=EFFECT),
    )(*[pltpu.with_memory_space_constraint(a, pltpu.HBM) for a in list(srcs) + list(lands) + [carry]])
    return outs[0], outs[1], outs[2:2 + n], outs[2 + n:2 + 2 * n], outs[2 + 2 * n]


def _split_wait(send_sems, recv_sems, srcs, lands, slots, after, *, name):
    n = len(srcs)

    def body(*refs):
        copies = _split_copies(slots, refs[:n], refs[n:2 * n], refs[2 * n], refs[2 * n + 1])
        for cp in copies:
            cp.wait_send()
        for cp in copies:
            cp.wait_recv()

    outs = pl.pallas_call(
        body, name=name,
        out_shape=[pltpu.HBM(a.shape, a.dtype) for a in list(srcs) + list(lands)],
        in_specs=[HBM] * (2 * n) + [SEM, SEM, ANY], out_specs=[HBM] * (2 * n),
        input_output_aliases={i: i for i in range(2 * n)},
        compiler_params=pltpu.CompilerParams(has_side_effects=EFFECT),
    )(*srcs, *lands, send_sems, recv_sems, after)
    return outs[n:]


def _adam_math(w, g, m, v):
    m1 = ADAM_B1 * m + (1.0 - ADAM_B1) * g
    v1 = ADAM_B2 * v + (1.0 - ADAM_B2) * (g * g)
    m_hat = m1 / (1.0 - ADAM_B1 ** ADAM_STEP)
    v_hat = v1 / (1.0 - ADAM_B2 ** ADAM_STEP)
    delta = -ADAM_LR * (m_hat / (jnp.sqrt(v_hat) + ADAM_EPS) + ADAM_WD * w)
    return delta, m1, v1


def _sum_adamw(parts, w, m, v, layer, prev, *, name):
    shape = w.shape
    r, c = shape[-2], shape[-1]
    a_l = math.prod(shape[1:-2])
    a = shape[0] * a_l
    base = layer * a_l
    if r % 256 == 0:
        tr, tc = 256, c
    else:
        tr, tc = r, _pick(c, (256, 128))
    w3, m3, v3 = (t.reshape(a, r, c) for t in (w, m, v))
    n_prev = 0 if prev is None else 4

    def body(*refs):
        p_ref, w_ref, m_ref, v_ref = refs[:4]
        g_ref, d_ref, m1_ref, v1_ref = refs[4 + n_prev:]
        g = p_ref[0].astype(F32)
        for src in range(1, N_DEV):
            g = g + p_ref[src].astype(F32)
        delta, m1, v1 = _adam_math(w_ref[...], g, m_ref[...], v_ref[...])
        g_ref[...] = g
        d_ref[...] = delta
        m1_ref[...] = m1
        v1_ref[...] = v1

    nr, ncol = r // tr, c // tc
    blk = pl.BlockSpec((None, tr, tc), lambda i, j: (base + i, j // ncol, j % ncol))
    prev3 = [] if prev is None else [t.reshape(a, r, c) for t in prev]
    outs = pl.pallas_call(
        body, grid=(a_l, nr * ncol),
        in_specs=[pl.BlockSpec((N_DEV, None, tr, tc), lambda i, j: (0, i, j // ncol, j % ncol)), blk, blk, blk]
        + [ANY] * n_prev,
        out_specs=[blk] * 4, out_shape=[jax.ShapeDtypeStruct((a, r, c), F32)] * 4,
        input_output_aliases={4 + k: k for k in range(n_prev)},
        compiler_params=_cparams(2), name=name,
    )(parts.reshape(N_DEV, a_l, r, c), w3, m3, v3, *prev3)
    return [o.reshape(shape) for o in outs]


def _sum_parts(parts, *, name):
    rows = parts.shape[1]

    def body(p_ref, o_ref):
        g = p_ref[0]
        for src in range(1, N_DEV):
            g = g + p_ref[src]
        o_ref[...] = g

    return pl.pallas_call(
        body, grid=(1,), in_specs=[pl.BlockSpec((N_DEV, rows, LANES), lambda i: (0, 0, 0))],
        out_specs=pl.BlockSpec((rows, LANES), lambda i: (0, 0)), out_shape=jax.ShapeDtypeStruct((rows, LANES), F32),
        compiler_params=_cparams(1), name=name,
    )(parts)


def _adamw(w, g, m, v, *, name):
    rows = w.shape[0]

    def body(w_ref, g_ref, m_ref, v_ref, d_ref, m1_ref, v1_ref):
        delta, m1, v1 = _adam_math(w_ref[...], g_ref[...], m_ref[...], v_ref[...])
        d_ref[...] = delta
        m1_ref[...] = m1
        v1_ref[...] = v1

    blk = pl.BlockSpec((rows, LANES), lambda i: (0, 0))
    return pl.pallas_call(
        body, grid=(1,), in_specs=[blk] * 4, out_specs=[blk] * 3,
        out_shape=[jax.ShapeDtypeStruct((rows, LANES), F32)] * 3,
        compiler_params=_cparams(1), name=name,
    )(w, g, m, v)


def _pack(arrs, dtype, row_mult=16):
    flat = jnp.concatenate([a.reshape(-1).astype(dtype) for a in arrs])
    n = flat.shape[0]
    rows = -(-n // (LANES * row_mult)) * row_mult
    flat = jnp.pad(flat, (0, rows * LANES - n))
    return flat.reshape(rows, LANES)


def _unpack(packed, shapes):
    flat = packed.reshape(-1)
    out, off = [], 0
    for shp in shapes:
        n = math.prod(shp)
        out.append(flat[off:off + n].reshape(shp))
        off += n
    return out


class _Layout:
    def __init__(self, d):
        self.d = d
        w = d
        self.dn_heads = w // DN_HEAD_DIM
        self.ssm_heads = w // SSM_HEAD_DIM
        gn = SSM_GROUPS * SSM_STATE
        self.sizes = (3 * w, w, self.dn_heads, self.dn_heads, 3 * w, w, w + 2 * gn, self.ssm_heads, 3 * d)
        offs, o = [], 0
        for sz in self.sizes:
            offs.append(o)
            o += sz
        self.offs = offs
        self.in_dim = o
        self.big = (0, 1, 4, 5, 6, 8)
        self.small = (2, 3, 7)
        cols, o = {}, 0
        for idx in self.big:
            cols[idx] = o
            o += self.sizes[idx]
        self.small_col = o
        self.cols = cols
        self.padded = o + LANES
        self.n_small = sum(self.sizes[i] for i in self.small)

    def reorder_w(self, w_in):
        parts = [w_in[:, self.offs[i]:self.offs[i] + self.sizes[i]] for i in self.big + self.small]
        parts.append(jnp.zeros((w_in.shape[0], LANES - self.n_small), w_in.dtype))
        return jnp.concatenate(parts, axis=1)

    def from_shards(self, parts):
        cs = self.in_dim // N_DEV
        pieces = []
        for i in self.big + self.small:
            a, b = self.offs[i], self.offs[i] + self.sizes[i]
            while a < b:
                j = a // cs
                hi = min(b, (j + 1) * cs)
                pieces.append(parts[j][:, a - j * cs:hi - j * cs])
                a = hi
        pieces.append(jnp.zeros((parts.shape[1], LANES - self.n_small), parts.dtype))
        return jnp.concatenate(pieces, axis=1)

    def to_shards(self, wp):
        cs = self.in_dim // N_DEV
        pcol = dict(self.cols)
        o = self.small_col
        for i in self.small:
            pcol[i] = o
            o += self.sizes[i]
        shards = []
        for j in range(N_DEV):
            a, b = j * cs, (j + 1) * cs
            pieces = []
            for i in range(len(self.sizes)):
                lo, hi = max(a, self.offs[i]), min(b, self.offs[i] + self.sizes[i])
                if lo < hi:
                    pieces.append(wp[:, pcol[i] + lo - self.offs[i]:pcol[i] + hi - self.offs[i]])
            shards.append(jnp.concatenate(pieces, axis=1))
        return jnp.stack(shards)

    def restore_w(self, wp):
        pieces = {}
        for idx in self.big:
            pieces[idx] = wp[:, self.cols[idx]:self.cols[idx] + self.sizes[idx]]
        o = self.small_col
        for idx in self.small:
            pieces[idx] = wp[:, o:o + self.sizes[idx]]
            o += self.sizes[idx]
        return jnp.concatenate([pieces[i] for i in range(len(self.sizes))], axis=1)


def _rows_form(cols_t, nh, nc):
    return cols_t.T.reshape(nh, nc, 1, CHUNK)


def _layer_fwd(x, p, lay, tag, late=None):
    s, d = x.shape
    nc = s // CHUNK
    w = d
    dnh, smh = lay.dn_heads, lay.ssm_heads
    r = smh // SSM_GROUPS
    cb = {k: v // LANES for k, v in lay.cols.items()}
    sv = {}
    h1 = _rms_fwd(x, p["norm_mix"], name=f"rms_mix_{tag}")
    proj = _matmul(h1, p["w_in"], name=f"mm_in_{tag}")
    small = proj[:, lay.small_col:lay.small_col + LANES]
    a_rows = _rows_form(small[:, 0:dnh], dnh, nc)
    b_rows = _rows_form(small[:, dnh:2 * dnh], dnh, nc)
    dt_rows = small[:, 2 * dnh:2 * dnh + smh].T.reshape(SSM_GROUPS, r, nc, CHUNK).transpose(0, 2, 1, 3)
    zero_b = jnp.zeros((1, 3 * w), F32)
    dn_qkv = _conv_fwd(proj, cb[0], p["dn_conv_w"], zero_b, 2 * dnh, name=f"dn_conv_{tag}")
    dn_alog = p["dn_a_log"].reshape(dnh, 1, 1)
    dn_dtb = p["dn_dt_bias"].reshape(dnh, 1, 1)
    o_dn, dn_states, dn_inv = _dn_fwd(dn_qkv, a_rows, b_rows, dn_alog, dn_dtb, name=f"dn_chunk_{tag}")
    y_dn = _dn_post_fwd(o_dn, proj, cb[1], p["dn_norm_w"], name=f"dn_post_{tag}")
    o_sb, sb_r = _sb_fwd(proj, cb[4], w, name=f"sb_{tag}")
    xbc = _conv_fwd(proj, cb[6], p["ssm_conv_w"], p["ssm_conv_b"].reshape(1, -1), 0, name=f"ssm_conv_{tag}")
    ssm_alog = p["ssm_a_log"].reshape(SSM_GROUPS, r, 1)
    ssm_dtb = p["ssm_dt_bias"].reshape(SSM_GROUPS, r, 1)
    y_ssd, ssm_states = _ssd_fwd(xbc, dt_rows, ssm_alog, ssm_dtb, name=f"ssd_{tag}")
    dexp = jnp.repeat(p["ssm_d"], SSM_HEAD_DIM)
    y_ssm = _ssm_post_fwd(y_ssd, xbc, proj, cb[5], dexp, p["ssm_norm_w"], name=f"ssm_post_{tag}")
    if late is not None:
        p.update(late(y_ssm))
    branches = (y_dn, o_sb, y_ssm)
    proj3 = jnp.concatenate(
        [_matmul(br, p["w_branch"][i], name=f"mm_branch{i}_{tag}") for i, br in enumerate(branches)], axis=1)
    merged = _merge_fwd(proj3, proj, cb[8], d, name=f"merge_{tag}")
    x1 = _matmul(merged, p["w_out"], name=f"mm_out_{tag}", epilogue=lambda acc, res: (acc + res,), extras=(x,))
    h2 = _rms_fwd(x1, p["norm_mlp"], name=f"rms_mlp_{tag}")
    u, act = _matmul(h2, p["w_up"], name=f"mm_up_{tag}", out_dtypes=(F32, MXU_DTYPE),
                     epilogue=lambda acc: (acc, jnp.square(jnp.maximum(acc, 0.0))))
    x2 = _matmul(act, p["w_down"], name=f"mm_down_{tag}", epilogue=lambda acc, res: (acc + res,), extras=(x1,))
    sv.update(x=x, h1=h1, proj=proj, a_rows=a_rows, b_rows=b_rows, dt_rows=dt_rows, dn_qkv=dn_qkv, dn_alog=dn_alog,
              dn_dtb=dn_dtb, o_dn=o_dn, dn_states=dn_states, dn_inv=dn_inv, y_dn=y_dn, o_sb=o_sb, sb_r=sb_r, xbc=xbc, ssm_alog=ssm_alog,
              ssm_dtb=ssm_dtb, y_ssd=y_ssd, ssm_states=ssm_states, dexp=dexp, y_ssm=y_ssm, proj3=proj3, merged=merged,
              x1=x1, h2=h2, u=u, act=act)
    return x2, sv


def _layer_bwd(dx2, p, sv, lay, tag, early=None, late=None):
    x = sv["x"]
    s, d = x.shape
    nc = s // CHUNK
    w = d
    dnh, smh = lay.dn_heads, lay.ssm_heads
    r = smh // SSM_GROUPS
    gn = SSM_GROUPS * SSM_STATE
    cb = {k: v // LANES for k, v in lay.cols.items()}
    proj = sv["proj"]
    g = {}
    dx2_b = dx2.astype(MXU_DTYPE)
    du = _matmul(dx2_b, p["w_down"], tb=True, name=f"mm_down_dx_{tag}", out_dtypes=(MXU_DTYPE,),
                 epilogue=lambda acc, uu: (acc * (2.0 * jnp.maximum(uu, 0.0)),), extras=(sv["u"],))
    g["w_down"] = _matmul(sv["act"], dx2_b, ta=True, name=f"mm_down_dw_{tag}", out_dtypes=(BF16,)).reshape(N_DEV, -1, d)
    g["w_up"] = _matmul(sv["h2"], du, ta=True, name=f"mm_up_dw_{tag}", out_dtypes=(BF16,), col_shards=N_DEV)
    dh2 = _matmul(du, p["w_up"], tb=True, name=f"mm_up_dx_{tag}")
    dx1, g["norm_mlp"] = _rms_bwd(sv["x1"], p["norm_mlp"], dh2, dx2, name=f"rms_mlp_bwd_{tag}")
    dx1_b = dx1.astype(MXU_DTYPE)
    dmerged = _matmul(dx1_b, p["w_out"], tb=True, name=f"mm_out_dx_{tag}")
    g["w_out"] = _matmul(sv["merged"], dx1_b, ta=True, name=f"mm_out_dw_{tag}", out_dtypes=(BF16,)).reshape(N_DEV, -1, d)
    dproj = lax.empty((s, lay.padded), MXU_DTYPE)
    dproj3, dproj = _merge_bwd(sv["proj3"], proj, cb[8], d, dmerged, dproj, name=f"merge_bwd_{tag}")
    branches = (sv["y_dn"], sv["o_sb"], sv["y_ssm"])
    dwb, dbr = [], []
    for i, br in enumerate(branches):
        dp_i = dproj3[:, i * d:(i + 1) * d]
        dwb.append(_matmul(br, dp_i, ta=True, name=f"mm_branch{i}_dw_{tag}", out_dtypes=(BF16,)).reshape(N_DEV, -1, d))
        dbr.append(_matmul(dp_i, p["w_branch"][i], tb=True, name=f"mm_branch{i}_dx_{tag}"))
    g["w_branch"] = jnp.stack(dwb, axis=1)
    dy_dn, do_sb, dy_ssm = dbr
    if early is not None:
        dy_ssm = early(g, dy_ssm)
    dy_ssd, dxs_skip, dproj, ddexp, g["ssm_norm_w"] = _ssm_post_bwd(
        sv["y_ssd"], sv["xbc"], proj, cb[5], sv["dexp"], p["ssm_norm_w"], dy_ssm, dproj, name=f"ssm_post_bwd_{tag}")
    g["ssm_d"] = ddexp.reshape(smh, SSM_HEAD_DIM).sum(axis=1)
    dxs, dbm, dcm, ddt_rows, dalog, ddtb = _ssd_bwd(
        sv["xbc"], sv["dt_rows"], sv["ssm_alog"], sv["ssm_dtb"], sv["ssm_states"], dy_ssd, name=f"ssd_bwd_{tag}")
    g["ssm_a_log"] = dalog.reshape(smh)
    g["ssm_dt_bias"] = ddtb.reshape(smh)
    dxbc_post = jnp.concatenate([dxs + dxs_skip, dbm, dcm], axis=1)
    dproj, g["ssm_conv_w"], dcb = _conv_bwd(proj, cb[6], p["ssm_conv_w"], p["ssm_conv_b"].reshape(1, -1), 0, dxbc_post,
                                            dproj, name=f"ssm_conv_bwd_{tag}")
    g["ssm_conv_b"] = dcb.reshape(-1)
    ddt = ddt_rows.transpose(0, 2, 1, 3).reshape(smh, s).T
    dqkv_sb = _sb_bwd(proj, cb[4], w, sv["sb_r"], do_sb, name=f"sb_bwd_{tag}")
    dproj = lax.dynamic_update_slice(dproj, jnp.concatenate([t.astype(MXU_DTYPE) for t in dqkv_sb], axis=1), (0, lay.cols[4]))
    do_dn, dproj, g["dn_norm_w"] = _dn_post_bwd(sv["o_dn"], proj, cb[1], p["dn_norm_w"], dy_dn, dproj,
                                                name=f"dn_post_bwd_{tag}")
    dqkv_dn, da_rows, db_rows, dal, ddtb_dn = _dn_bwd(
        sv["dn_qkv"], sv["a_rows"], sv["b_rows"], sv["dn_alog"], sv["dn_dtb"], sv["dn_states"], sv["dn_inv"], do_dn,
        name=f"dn_chunk_bwd_{tag}")
    g["dn_a_log"] = dal.reshape(dnh)
    g["dn_dt_bias"] = ddtb_dn.reshape(dnh)
    zero_b = jnp.zeros((1, 3 * w), F32)
    dproj, g["dn_conv_w"], _ = _conv_bwd(proj, cb[0], p["dn_conv_w"], zero_b, 2 * dnh, dqkv_dn, dproj,
                                         name=f"dn_conv_bwd_{tag}")
    da = da_rows.reshape(dnh, s).T
    db = db_rows.reshape(dnh, s).T
    dsmall = jnp.concatenate([da, db, ddt, jnp.zeros((s, LANES - lay.n_small), F32)], axis=1).astype(MXU_DTYPE)
    dproj = lax.dynamic_update_slice(dproj, dsmall, (0, lay.small_col))
    g["w_in"] = lay.to_shards(_matmul(sv["h1"], dproj, ta=True, name=f"mm_in_dw_{tag}", out_dtypes=(BF16,)))
    if late is not None:
        dproj = late(g, dproj)
    dh1 = _matmul(dproj, p["w_in"], tb=True, name=f"mm_in_dx_{tag}")
    dx0, g["norm_mix"] = _rms_bwd(x, p["norm_mix"], dh1, dx1, name=f"rms_mix_bwd_{tag}")
    return dx0, g


BIG = ("w_in", "w_branch", "w_out", "w_up", "w_down")
CONV = ("dn_conv_w", "ssm_conv_w")
SMALL = ("norm_mix", "dn_conv_w", "dn_a_log", "dn_dt_bias", "dn_norm_w", "ssm_conv_w", "ssm_conv_b", "ssm_a_log",
         "ssm_dt_bias", "ssm_d", "ssm_norm_w", "norm_mlp", "norm_final")
WEIGHTS = ("norm_mix", "w_in", "dn_conv_w", "dn_a_log", "dn_dt_bias", "dn_norm_w", "ssm_conv_w", "ssm_conv_b", "ssm_a_log",
           "ssm_dt_bias", "ssm_d", "ssm_norm_w", "w_branch", "w_out", "norm_mlp", "w_up", "w_down", "norm_final")
SHARD_AXIS = {"w_in": 2, "dn_conv_w": 2, "ssm_conv_w": 2, "w_branch": 2, "w_out": 1, "w_up": 2, "w_down": 1}


def _to_shards(full, axis):
    shp = full.shape
    n = shp[axis] // N_DEV
    t = full.reshape(shp[:axis] + (N_DEV, n) + shp[axis + 1:])
    return jnp.moveaxis(t, axis, 0)


def _from_shards(parts, axis):
    t = jnp.moveaxis(parts, 0, axis)
    shp = t.shape
    return t.reshape(shp[:axis] + (shp[axis] * shp[axis + 1],) + shp[axis + 2:])


def _unshard(parts, axis, *, name):
    shard = parts.shape[1:]
    nd = len(shard)
    if axis == 0:
        return parts.reshape((N_DEV * shard[0],) + shard[1:])

    def copy_block(i_ref, o_ref):
        o_ref[...] = i_ref[...]

    if axis == nd - 1:
        rows, n = math.prod(shard[:-1]), shard[-1]
        out = pl.pallas_call(
            copy_block, grid=(N_DEV,),
            in_specs=[pl.BlockSpec((None, rows, n), lambda j: (j, 0, 0))],
            out_specs=pl.BlockSpec((rows, n), lambda j: (0, j)),
            out_shape=jax.ShapeDtypeStruct((rows, N_DEV * n), parts.dtype),
            compiler_params=_cparams(1), name=name,
        )(parts.reshape(N_DEV, rows, n))
        return out.reshape(shard[:-1] + (N_DEV * n,))
    assert axis == nd - 2, (parts.shape, axis)
    a, n, c = math.prod(shard[:-2]), shard[-2], shard[-1]
    out = pl.pallas_call(
        copy_block, grid=(N_DEV, a),
        in_specs=[pl.BlockSpec((None, None, n, c), lambda j, i: (j, i, 0, 0))],
        out_specs=pl.BlockSpec((None, n, c), lambda j, i: (i, j, 0)),
        out_shape=jax.ShapeDtypeStruct((a, N_DEV * n, c), parts.dtype),
        compiler_params=_cparams(2), name=name,
    )(parts.reshape(N_DEV, a, n, c))
    return out.reshape(shard[:-2] + (N_DEV * n, c))


def _step(w, m, v, x, target):
    s, d = x.shape
    lay = _Layout(d)
    me = 4 * lax.axis_index("x") + 2 * lax.axis_index("y") + lax.axis_index("c")

    def shard(n, l):
        return w[n][l].astype(BF16) if n in BIG else w[n][l]

    def empty_land(a):
        return lax.empty((N_DEV,) + a.shape, a.dtype)

    def with_own(land, own):
        return lax.dynamic_update_index_in_dim(land, own, me, 0)

    def assemble(n, parts, l):
        return lay.from_shards(parts) if n == "w_in" else _unshard(parts, SHARD_AXIS[n] - 1, name=f"unshard_{n}_l{l}")

    small_names = tuple(n for n in WEIGHTS if n not in BIG + CONV + ("norm_final",))

    first, rest = ("w_in",) + CONV, BIG[1:]
    got = _all_gather([shard(n, 0) for n in first], name="gather_l0_first")
    whole, sliced = (True, None), (False, None)
    names_a, names_b = rest, BIG + CONV
    srcs_a, srcs_b = [shard(n, 0) for n in names_a], [shard(n, 1) for n in names_b]
    sem_sa, sem_ra, srcs_a, lands_a, w_in0 = _split_start(
        srcs_a, [empty_land(a) for a in srcs_a], [whole] * len(srcs_a), got[0], name="gather_l0_rest_start")
    sem_sb, sem_rb, srcs_b, lands_b, w_in0 = _split_start(
        srcs_b, [empty_land(a) for a in srcs_b], [whole] * len(srcs_b), w_in0, name="gather_l1_start")
    p0 = {n: w[n][0] for n in small_names}
    p0.update({n: assemble(n, g, 0) for n, g in zip(first, [w_in0] + list(got[1:]))})

    def late_l0(after):
        lands = _split_wait(sem_sa, sem_ra, srcs_a, lands_a, [whole] * len(srcs_a), after, name="gather_l0_rest_wait")
        return {n: assemble(n, with_own(ld, s_), 0) for n, ld, s_ in zip(names_a, lands, srcs_a)}

    h, sv0 = _layer_fwd(x, p0, lay, "l0", late=late_l0)
    lands = _split_wait(sem_sb, sem_rb, srcs_b, lands_b, [whole] * len(srcs_b), h, name="gather_l1_wait")
    p1 = {n: w[n][1] for n in small_names}
    p1.update({n: assemble(n, with_own(ld, s_), 1) for n, ld, s_ in zip(names_b, lands, srcs_b)})
    h, sv1 = _layer_fwd(h, p1, lay, "l1")
    loss, dh, g_norm_final = _final_loss(h, w["norm_final"], target, name="final_loss")
    grads = [None] * DEPTH
    dh, grads[1] = _layer_bwd(dh, p1, sv1, lay, "l1")

    def exchange_start(names, g, carry, tag):
        srcs = [g[n] for n in names]
        return _split_start(srcs, [lax.empty(a.shape, a.dtype) for a in srcs], [sliced] * len(srcs), carry,
                            name=f"grad_{tag}_start")

    def exchange_wait(names, started, after, tag):
        sem_s, sem_r, srcs, lands_, _ = started
        lands_ = _split_wait(sem_s, sem_r, srcs, lands_, [sliced] * len(srcs), after, name=f"grad_{tag}_wait")
        return {n: with_own(ld, lax.dynamic_index_in_dim(s_, me, 0, keepdims=False)) for n, ld, s_ in zip(names, lands_, srcs)}

    x1_started = exchange_start(BIG, grads[1], dh, "l1")
    pending = {}

    def early_l0(g, carry):
        pending["rest"] = exchange_start(rest, g, carry, "l0_rest")
        return pending["rest"][4]

    def late_bwd_l0(g, carry):
        pending["w_in"] = exchange_start(("w_in",), g, carry, "l0_w_in")
        return pending["w_in"][4]

    grad_x, grads[0] = _layer_bwd(x1_started[4], p0, sv0, lay, "l0", early=early_l0, late=late_bwd_l0)

    out = {"grad": {}, "delta": {}, "new_m": {}, "new_v": {}}
    parts1 = exchange_wait(BIG, x1_started, grad_x, "l1")
    res1 = {n: _sum_adamw(parts1[n], w[n], m[n], v[n], 1, None, name=f"sum_adamw_{n}_l1") for n in BIG}
    parts0 = exchange_wait(rest, pending["rest"], res1["w_in"][0], "l0_rest")
    res0 = {n: _sum_adamw(parts0[n], w[n], m[n], v[n], 0, res1[n], name=f"sum_adamw_{n}_l0") for n in rest}
    parts0 = exchange_wait(("w_in",), pending["w_in"], res0["w_down"][0], "l0_w_in")
    res0["w_in"] = _sum_adamw(parts0["w_in"], w["w_in"], m["w_in"], v["w_in"], 0, res1["w_in"], name="sum_adamw_w_in_l0")
    for n in BIG:
        for key, a in zip(("grad", "delta", "new_m", "new_v"), res0[n]):
            out[key][n] = a

    gfull = {n: jnp.stack([grads[l][n] for l in range(DEPTH)]) for n in SMALL if n != "norm_final"}
    gfull["norm_final"] = g_norm_final
    small_send = _pack([gfull[n] for n in SMALL] + [loss.reshape(1)], F32)
    small_recv = _all_gather([small_send], name="gather_small_grads", after=res0["w_in"][0])[0]
    small_sum = _sum_parts(small_recv, name="sum_small")
    small_full = _unpack(small_sum, [gfull[n].shape for n in SMALL] + [(1,)])
    loss_total = small_full[-1][0]
    gsmall = {}
    for n, a in zip(SMALL, small_full[:-1]):
        if n in SHARD_AXIS:
            a = lax.dynamic_index_in_dim(_to_shards(a, SHARD_AXIS[n]), me, axis=0, keepdims=False)
        gsmall[n] = a
    small_shapes = [w[n].shape for n in SMALL]
    ws, gs, ms, vs = (_pack([t[n] for n in SMALL], F32) for t in (w, gsmall, m, v))
    ds, m1s, v1s = _adamw(ws, gs, ms, vs, name="adamw_small")
    for n in SMALL:
        out["grad"][n] = gsmall[n]
    for key, packed in (("delta", ds), ("new_m", m1s), ("new_v", v1s)):
        for n, a in zip(SMALL, _unpack(packed, small_shapes)):
            out[key][n] = a
    return loss_total, grad_x, out


def kernel(x, norm_mix, w_in, dn_conv_w, dn_a_log, dn_dt_bias, dn_norm_w, ssm_conv_w, ssm_conv_b, ssm_a_log, ssm_dt_bias, ssm_d, ssm_norm_w, w_branch, w_out, norm_mlp, w_up, w_down, norm_final, loss_target, m_norm_mix, m_w_in, m_dn_conv_w, m_dn_a_log, m_dn_dt_bias, m_dn_norm_w, m_ssm_conv_w, m_ssm_conv_b, m_ssm_a_log, m_ssm_dt_bias, m_ssm_d, m_ssm_norm_w, m_w_branch, m_w_out, m_norm_mlp, m_w_up, m_w_down, m_norm_final, v_norm_mix, v_w_in, v_dn_conv_w, v_dn_a_log, v_dn_dt_bias, v_dn_norm_w, v_ssm_conv_w, v_ssm_conv_b, v_ssm_a_log, v_ssm_dt_bias, v_ssm_d, v_ssm_norm_w, v_w_branch, v_w_out, v_norm_mlp, v_w_up, v_w_down, v_norm_final):
    w = dict(norm_mix=norm_mix, w_in=w_in, dn_conv_w=dn_conv_w, dn_a_log=dn_a_log, dn_dt_bias=dn_dt_bias, dn_norm_w=dn_norm_w,
             ssm_conv_w=ssm_conv_w, ssm_conv_b=ssm_conv_b, ssm_a_log=ssm_a_log, ssm_dt_bias=ssm_dt_bias, ssm_d=ssm_d,
             ssm_norm_w=ssm_norm_w, w_branch=w_branch, w_out=w_out, norm_mlp=norm_mlp, w_up=w_up, w_down=w_down,
             norm_final=norm_final)
    m = dict(norm_mix=m_norm_mix, w_in=m_w_in, dn_conv_w=m_dn_conv_w, dn_a_log=m_dn_a_log, dn_dt_bias=m_dn_dt_bias,
             dn_norm_w=m_dn_norm_w, ssm_conv_w=m_ssm_conv_w, ssm_conv_b=m_ssm_conv_b, ssm_a_log=m_ssm_a_log,
             ssm_dt_bias=m_ssm_dt_bias, ssm_d=m_ssm_d, ssm_norm_w=m_ssm_norm_w, w_branch=m_w_branch, w_out=m_w_out,
             norm_mlp=m_norm_mlp, w_up=m_w_up, w_down=m_w_down, norm_final=m_norm_final)
    v = dict(norm_mix=v_norm_mix, w_in=v_w_in, dn_conv_w=v_dn_conv_w, dn_a_log=v_dn_a_log, dn_dt_bias=v_dn_dt_bias,
             dn_norm_w=v_dn_norm_w, ssm_conv_w=v_ssm_conv_w, ssm_conv_b=v_ssm_conv_b, ssm_a_log=v_ssm_a_log,
             ssm_dt_bias=v_ssm_dt_bias, ssm_d=v_ssm_d, ssm_norm_w=v_ssm_norm_w, w_branch=v_w_branch, w_out=v_w_out,
             norm_mlp=v_norm_mlp, w_up=v_w_up, w_down=v_w_down, norm_final=v_norm_final)
    loss, grad_x, out = _step(w, m, v, x[0], loss_target[0])
    return (loss, grad_x[None], *[out["grad"][n] for n in WEIGHTS], *[out["delta"][n] for n in WEIGHTS],
            *[out["new_m"][n] for n in WEIGHTS], *[out["new_v"][n] for n in WEIGHTS])
```

```python
import functools
import math

import jax
import jax.numpy as jnp
from jax import lax
from jax.experimental import pallas as pl
from jax.experimental.pallas import tpu as pltpu

F32 = jnp.float32
BF16 = jnp.bfloat16
MXU_DTYPE = BF16
HIGHEST = lax.Precision.HIGHEST

N_DEV = 8
DEPTH = 2
EPS = 1e-6
CONV_K = 4
DN_HEAD_DIM = 128
SB_HEAD_DIM = 64
SSM_HEAD_DIM = 64
SSM_STATE = 128
SSM_GROUPS = 4
CHUNK = 64
SB_BLOCK = 128
LANES = 128
ADAM_LR, ADAM_B1, ADAM_B2, ADAM_EPS, ADAM_WD, ADAM_STEP = 0.001, 0.9, 0.999, 1e-08, 0.01, 10
NEG_BIG = -1e30
DN_HEADS_PER_STEP = 8
SSD_GROUPS_PER_STEP = 1
SB_UNROLL = 4
SB_SPLIT = 2
CHUNK_PREC = lax.Precision.HIGH

ARB = "arbitrary"


def _cparams(n_axes):
    return pltpu.CompilerParams(dimension_semantics=(ARB,) * n_axes)


def _softplus(x):
    return jnp.maximum(x, 0.0) + jnp.log1p(jnp.exp(-jnp.abs(x)))


def _sigmoid(x):
    return jax.nn.sigmoid(x)


def _silu(x):
    return x * _sigmoid(x)


def _silu_and_grad(x):
    s = _sigmoid(x)
    return x * s, s * (1.0 + x * (1.0 - s))


def _dot(a, b, dims, prec=None):
    return lax.dot_general(a, b, (dims, ((), ())), precision=prec, preferred_element_type=F32)


NN = ((1,), (0,))
NT = ((1,), (1,))
TN = ((0,), (0,))


def _hdot(a, b, dims=NN):
    return _dot(a, b, dims, CHUNK_PREC)


def _bdot(a, b, dims=NN):
    return _dot(a.astype(MXU_DTYPE), b.astype(MXU_DTYPE), dims)


def _split_dot(a, m_bf16, nsplit=3):
    out = None
    rem = a
    for _ in range(nsplit):
        piece = rem.astype(BF16)
        rem = rem - piece.astype(F32)
        term = _dot(piece, m_bf16, NN)
        out = term if out is None else out + term
    return out


def _pick(n, pref):
    for t in pref:
        if n % t == 0:
            return t
    return n


def _matmul(a, b, *, ta=False, tb=False, name, epilogue=None, extras=(), out_dtypes=(F32,), col_shards=1, into=None,
            tm=None, tn=None, tk=None):
    m, k = (a.shape[1], a.shape[0]) if ta else a.shape
    k2, n = (b.shape[1], b.shape[0]) if tb else b.shape
    assert k == k2, (a.shape, b.shape, ta, tb)
    ncs = n // col_shards
    tm = tm or _pick(m, (1920, 1024, 512, 256, 128))
    tn = tn or _pick(ncs, (1920, 1024, 640, 512, 384, 256, 128))
    tk = tk or _pick(k, (1920, 1024, 640, 512, 256, 128))
    nk = k // tk
    a_spec = pl.BlockSpec((tk, tm), lambda i, j, kk: (kk, i)) if ta else pl.BlockSpec((tm, tk), lambda i, j, kk: (i, kk))
    b_spec = pl.BlockSpec((tn, tk), lambda i, j, kk: (j, kk)) if tb else pl.BlockSpec((tk, tn), lambda i, j, kk: (kk, j))
    e_spec = pl.BlockSpec((tm, tn), lambda i, j, kk: (i, j))
    if into is not None:
        buf, col_off = into
        off = col_off // tn
        assert col_shards == 1 and len(out_dtypes) == 1 and col_off % tn == 0 and out_dtypes[0] == buf.dtype
        o_spec, o_shape = pl.BlockSpec((tm, tn), lambda i, j, kk: (i, off + j)), buf.shape
    elif col_shards == 1:
        o_spec, o_shape = e_spec, (m, n)
    else:
        per = ncs // tn
        o_spec, o_shape = pl.BlockSpec((None, tm, tn), lambda i, j, kk: (j // per, i, j % per)), (col_shards, m, ncs)
    dims = (((0,) if ta else (1,)), ((1,) if tb else (0,)))
    n_extra = len(extras)
    n_out = len(out_dtypes)
    n_into = 0 if into is None else 1

    def body(*refs):
        a_ref, b_ref = refs[0], refs[1]
        extra_refs = refs[2:2 + n_extra]
        out_refs = refs[2 + n_extra + n_into:2 + n_extra + n_into + n_out]
        acc_ref = refs[-1]
        kk = pl.program_id(2)

        @pl.when(kk == 0)
        def _():
            acc_ref[...] = jnp.zeros_like(acc_ref)

        acc_ref[...] += _dot(a_ref[...].astype(MXU_DTYPE), b_ref[...].astype(MXU_DTYPE), dims)

        @pl.when(kk == nk - 1)
        def _():
            acc = acc_ref[...]
            outs = (acc,) if epilogue is None else epilogue(acc, *[r[...] for r in extra_refs])
            for o_ref, o in zip(out_refs, outs):
                o_ref[...] = o.astype(o_ref.dtype)

    outs = pl.pallas_call(
        body,
        grid=(m // tm, n // tn, nk),
        in_specs=[a_spec, b_spec] + [e_spec] * n_extra + [ANY] * n_into,
        out_specs=[o_spec] * n_out,
        out_shape=[jax.ShapeDtypeStruct(o_shape, dt) for dt in out_dtypes],
        input_output_aliases={2 + n_extra: 0} if n_into else {},
        scratch_shapes=[pltpu.VMEM((tm, tn), F32)],
        compiler_params=pltpu.CompilerParams(dimension_semantics=("parallel", "parallel", ARB)),
        name=name,
    )(a, b, *extras, *([] if into is None else [into[0]]))
    return outs[0] if n_out == 1 else tuple(outs)


def _rms_fwd(x, w, *, name, tm=512):
    s, d = x.shape
    out_dtype = MXU_DTYPE

    def body(x_ref, w_ref, o_ref):
        xv = x_ref[...]
        r = lax.rsqrt(jnp.mean(xv * xv, axis=-1, keepdims=True) + EPS)
        o_ref[...] = (xv * r * w_ref[...]).astype(o_ref.dtype)

    return pl.pallas_call(
        body, grid=(s // tm,),
        in_specs=[pl.BlockSpec((tm, d), lambda i: (i, 0)), pl.BlockSpec((1, d), lambda i: (0, 0))],
        out_specs=pl.BlockSpec((tm, d), lambda i: (i, 0)),
        out_shape=jax.ShapeDtypeStruct((s, d), out_dtype),
        compiler_params=_cparams(1), name=name,
    )(x, w.reshape(1, d))


def _rms_bwd(x, w, dh, dres, *, name, tm=512):
    s, d = x.shape

    def body(x_ref, w_ref, dh_ref, dres_ref, dx_ref, dw_ref):
        xv = x_ref[...]
        r = lax.rsqrt(jnp.mean(xv * xv, axis=-1, keepdims=True) + EPS)
        xh = xv * r
        dhv = dh_ref[...].astype(F32)
        dxn = dhv * w_ref[...]
        dx = r * (dxn - xh * jnp.mean(dxn * xh, axis=-1, keepdims=True))
        dx_ref[...] = dres_ref[...] + dx

        @pl.when(pl.program_id(0) == 0)
        def _():
            dw_ref[...] = jnp.zeros_like(dw_ref)

        dw_ref[...] += jnp.sum(dhv * xh, axis=0, keepdims=True)

    dx, dw = pl.pallas_call(
        body, grid=(s // tm,),
        in_specs=[pl.BlockSpec((tm, d), lambda i: (i, 0)), pl.BlockSpec((1, d), lambda i: (0, 0)),
                  pl.BlockSpec((tm, d), lambda i: (i, 0)), pl.BlockSpec((tm, d), lambda i: (i, 0))],
        out_specs=[pl.BlockSpec((tm, d), lambda i: (i, 0)), pl.BlockSpec((1, d), lambda i: (0, 0))],
        out_shape=[jax.ShapeDtypeStruct((s, d), F32), jax.ShapeDtypeStruct((1, d), F32)],
        compiler_params=_cparams(1), name=name,
    )(x, w.reshape(1, d), dh, dres)
    return dx, dw.reshape(d)


def _final_loss(x, w, target, *, name, tm=512):
    s, d = x.shape

    def body(x_ref, w_ref, t_ref, loss_ref, dx_ref, dw_ref):
        xv = x_ref[...]
        r = lax.rsqrt(jnp.mean(xv * xv, axis=-1, keepdims=True) + EPS)
        xh = xv * r
        err = xh * w_ref[...] - t_ref[...]
        dy = err * (1.0 / d)
        dxn = dy * w_ref[...]
        dx_ref[...] = r * (dxn - xh * jnp.mean(dxn * xh, axis=-1, keepdims=True))

        @pl.when(pl.program_id(0) == 0)
        def _():
            dw_ref[...] = jnp.zeros_like(dw_ref)
            loss_ref[...] = jnp.zeros_like(loss_ref)

        dw_ref[...] += jnp.sum(dy * xh, axis=0, keepdims=True)
        row = jnp.sum(err * err, axis=1, keepdims=True) * (0.5 / d)
        loss_ref[...] += jnp.sum(row, axis=0, keepdims=True)

    loss, dx, dw = pl.pallas_call(
        body, grid=(s // tm,),
        in_specs=[pl.BlockSpec((tm, d), lambda i: (i, 0)), pl.BlockSpec((1, d), lambda i: (0, 0)),
                  pl.BlockSpec((tm, d), lambda i: (i, 0))],
        out_specs=[pl.BlockSpec((1, 1), lambda i: (0, 0)), pl.BlockSpec((tm, d), lambda i: (i, 0)),
                   pl.BlockSpec((1, d), lambda i: (0, 0))],
        out_shape=[jax.ShapeDtypeStruct((1, 1), F32), jax.ShapeDtypeStruct((s, d), F32), jax.ShapeDtypeStruct((1, d), F32)],
        compiler_params=_cparams(1), name=name,
    )(x, w.reshape(1, d), target)
    return loss[0, 0], dx, dw.reshape(d)


def _shift_down(x, sh, t_idx):
    return jnp.where(t_idx >= sh, pltpu.roll(x, sh, 0), 0.0)


def _shift_up(x, sh, t_idx, s):
    return jnp.where(t_idx < s - sh, pltpu.roll(x, s - sh, 0), 0.0)


def _conv_pre(x, w_rows, b, t_idx):
    c = w_rows[CONV_K - 1] * x + b
    for sh in range(1, CONV_K):
        c = c + w_rows[CONV_K - 1 - sh] * _shift_down(x, sh, t_idx)
    return c


def _conv_fwd(src, col0, w, b, n_l2, *, name):
    s = src.shape[0]
    c_tot = w.shape[1]
    nblk = c_tot // LANES

    def body(x_ref, w_ref, b_ref, o_ref):
        j = pl.program_id(0)
        t_idx = lax.broadcasted_iota(jnp.int32, (s, LANES), 0)
        w_rows = [w_ref[kk:kk + 1, :] for kk in range(CONV_K)]
        y = _silu(_conv_pre(x_ref[...], w_rows, b_ref[...], t_idx))
        if n_l2 > 0:
            yn = y * lax.rsqrt(jnp.sum(y * y, axis=1, keepdims=True) + EPS)
            y = jnp.where(j < n_l2, yn, y)
        o_ref[...] = y

    return pl.pallas_call(
        body, grid=(nblk,),
        in_specs=[pl.BlockSpec((s, LANES), lambda j: (0, col0 + j)), pl.BlockSpec((CONV_K, LANES), lambda j: (0, j)),
                  pl.BlockSpec((1, LANES), lambda j: (0, j))],
        out_specs=pl.BlockSpec((s, LANES), lambda j: (0, j)),
        out_shape=jax.ShapeDtypeStruct((s, c_tot), F32),
        compiler_params=_cparams(1), name=name,
    )(src, w, b)


def _conv_bwd(src, col0, w, b, n_l2, dout, into, *, name):
    s = src.shape[0]
    c_tot = w.shape[1]
    nblk = c_tot // LANES

    def body(x_ref, w_ref, b_ref, do_ref, into_ref, dx_ref, dw_ref, db_ref):
        j = pl.program_id(0)
        t_idx = lax.broadcasted_iota(jnp.int32, (s, LANES), 0)
        xv = x_ref[...]
        w_rows = [w_ref[kk:kk + 1, :] for kk in range(CONV_K)]
        c = _conv_pre(xv, w_rows, b_ref[...], t_idx)
        dy = do_ref[...]
        y, y_grad = _silu_and_grad(c)
        if n_l2 > 0:
            r = lax.rsqrt(jnp.sum(y * y, axis=1, keepdims=True) + EPS)
            dyn = r * dy - y * (r * r * r) * jnp.sum(dy * y, axis=1, keepdims=True)
            dy = jnp.where(j < n_l2, dyn, dy)
        dc = dy * y_grad
        dx = w_rows[CONV_K - 1] * dc
        rows = [None] * CONV_K
        rows[CONV_K - 1] = jnp.sum(dc * xv, axis=0, keepdims=True)
        for sh in range(1, CONV_K):
            dx = dx + w_rows[CONV_K - 1 - sh] * _shift_up(dc, sh, t_idx, s)
            rows[CONV_K - 1 - sh] = jnp.sum(dc * _shift_down(xv, sh, t_idx), axis=0, keepdims=True)
        dx_ref[...] = dx.astype(dx_ref.dtype)
        for kk in range(CONV_K):
            dw_ref[kk:kk + 1, :] = rows[kk]
        db_ref[...] = jnp.sum(dc, axis=0, keepdims=True)

    return pl.pallas_call(
        body, grid=(nblk,),
        in_specs=[pl.BlockSpec((s, LANES), lambda j: (0, col0 + j)), pl.BlockSpec((CONV_K, LANES), lambda j: (0, j)),
                  pl.BlockSpec((1, LANES), lambda j: (0, j)), pl.BlockSpec((s, LANES), lambda j: (0, j)), ANY],
        out_specs=[pl.BlockSpec((s, LANES), lambda j: (0, col0 + j)), pl.BlockSpec((CONV_K, LANES), lambda j: (0, j)),
                   pl.BlockSpec((1, LANES), lambda j: (0, j))],
        out_shape=[jax.ShapeDtypeStruct(into.shape, into.dtype), jax.ShapeDtypeStruct((CONV_K, c_tot), F32),
                   jax.ShapeDtypeStruct((1, c_tot), F32)],
        input_output_aliases={4: 0},
        compiler_params=_cparams(1), name=name,
    )(src, w, b, dout, into)


def _chunk_masks(c):
    ii = lax.broadcasted_iota(jnp.int32, (c, c), 0)
    jj = lax.broadcasted_iota(jnp.int32, (c, c), 1)
    return ii, jj


def _row_to_col(row, eye):
    return jnp.sum(jnp.where(eye, row, 0.0), axis=1, keepdims=True)


def _each(f, *lists):
    return [f(*xs) for xs in zip(*lists)]


@jax.custom_vjp
def _nilpotent_inverse(nmats):
    c = nmats[0].shape[0]
    ii, jj = _chunk_masks(c)
    xinv = _each(lambda n: jnp.where(ii == jj, 1.0, 0.0) + n, nmats)
    pw = nmats
    for _ in range(int(math.log2(c)) - 1):
        pw = _each(lambda p: _dot(p, p, NN, HIGHEST), pw)
        xinv = _each(lambda x, p: x + _dot(x, p, NN, HIGHEST), xinv, pw)
    return xinv


def _nilpotent_inverse_fwd(nmats):
    xinv = _nilpotent_inverse(nmats)
    return xinv, xinv


def _nilpotent_inverse_bwd(xinv, cts):
    left = _each(lambda x, ct: _dot(x, ct, TN, HIGHEST), xinv, cts)
    return (_each(lambda l_, x: _dot(l_, x, NT, HIGHEST), left, xinv),)


_nilpotent_inverse.defvjp(_nilpotent_inverse_fwd, _nilpotent_inverse_bwd)


@jax.custom_vjp
def _saved_inverse(nmats, saved):
    return saved


def _saved_inverse_fwd(nmats, saved):
    return saved, saved


def _saved_inverse_bwd(xinv, cts):
    return _nilpotent_inverse_bwd(xinv, cts) + (_each(jnp.zeros_like, xinv),)


_saved_inverse.defvjp(_saved_inverse_fwd, _saved_inverse_bwd)


def _dn_chunk(q, k, v, a_row, b_row, alog, dtb, s0, saved_inverse=None):
    c = q[0].shape[0]
    ii, jj = _chunk_masks(c)
    causal, strict, eye = ii >= jj, ii > jj, ii == jj
    g_row = _each(lambda al, a, dt: -jnp.exp(al) * _softplus(a + dt), alog, a_row, dtb)
    beta_col = _each(lambda b: _row_to_col(_sigmoid(b), eye), b_row)
    g_col = _each(lambda g: _row_to_col(g, eye), g_row)
    gc_col = _each(lambda g: jnp.sum(jnp.where(causal, g, 0.0), axis=1, keepdims=True), g_row)
    gc_row = _each(lambda g: jnp.sum(jnp.where(jj >= ii, g, 0.0), axis=0, keepdims=True), g_col)
    decay = _each(lambda gc, gr: jnp.exp(jnp.where(causal, gc - gr, NEG_BIG)), gc_col, gc_row)
    kb = _each(jnp.multiply, k, beta_col)
    vb = _each(jnp.multiply, v, beta_col)
    nmat = _each(lambda kb_, k_, dc: -jnp.where(strict, _dot(kb_, k_, NT, HIGHEST) * dc, 0.0), kb, k, decay)
    xinv = _nilpotent_inverse(nmat) if saved_inverse is None else _saved_inverse(nmat, saved_inverse)
    egc = _each(jnp.exp, gc_col)
    u = _each(lambda x, vb_: _dot(x, vb_, NN, HIGHEST), xinv, vb)
    w = _each(lambda x, kb_, e: _dot(x, kb_ * e, NN, HIGHEST), xinv, kb, egc)
    qs = _each(lambda q_: q_ * (q_.shape[1] ** -0.5), q)
    attn = _each(lambda q_, k_, dc: _hdot(q_, k_, NT) * dc, qs, k, decay)
    gl = _each(lambda g: jnp.sum(g, axis=1, keepdims=True), g_row)
    kd = _each(lambda k_, gl_, gc: k_ * jnp.exp(gl_ - gc), k, gl, gc_col)
    v_new = _each(lambda u_, w_, s: u_ - _hdot(w_, s), u, w, s0)
    o = _each(lambda q_, e, s, at, vn: _hdot(q_ * e, s) + _hdot(at, vn), qs, egc, s0, attn, v_new)
    s1 = _each(lambda s, gl_, kd_, vn: s * jnp.exp(gl_) + _hdot(kd_, vn, TN), s0, gl, kd, v_new)
    return (o, s1), xinv


def _dn_specs(nh, nc, hb, rev):
    n_of = (lambda n: nc - 1 - n) if rev else (lambda n: n)
    ng = nh // hb
    qkv = [pl.BlockSpec((CHUNK, hb * DN_HEAD_DIM), (lambda h, n, o=o: (n_of(n), o * ng + h))) for o in range(3)]
    row = pl.BlockSpec((hb, None, 1, CHUNK), lambda h, n: (h, n_of(n), 0, 0))
    scal = pl.BlockSpec((hb, 1, 1), lambda h, n: (h, 0, 0))
    o_spec = pl.BlockSpec((CHUNK, hb * DN_HEAD_DIM), lambda h, n: (n_of(n), h))
    st = pl.BlockSpec((hb, None, DN_HEAD_DIM, DN_HEAD_DIM), lambda h, n: (h, n_of(n), 0, 0))
    inv = pl.BlockSpec((hb, None, CHUNK, CHUNK), lambda h, n: (h, n_of(n), 0, 0))
    return qkv, row, scal, o_spec, st, inv


def _dn_fwd(qkv, a_rows, b_rows, alog, dtb, *, name):
    s = qkv.shape[0]
    nh, nc = a_rows.shape[0], a_rows.shape[1]
    hb = min(DN_HEADS_PER_STEP, nh)
    qkv_specs, row, scal, o_spec, st, inv = _dn_specs(nh, nc, hb, False)
    hd = DN_HEAD_DIM

    def body(q_ref, k_ref, v_ref, a_ref, b_ref, al_ref, dt_ref, o_ref, st_ref, inv_ref, state):
        @pl.when(pl.program_id(1) == 0)
        def _():
            state[...] = jnp.zeros_like(state)

        cols = [slice(h * hd, (h + 1) * hd) for h in range(hb)]
        s0 = [state[h] for h in range(hb)]
        for h in range(hb):
            st_ref[h] = s0[h]
        (o, s1), xinv = _dn_chunk(
            [q_ref[:, cl] for cl in cols], [k_ref[:, cl] for cl in cols], [v_ref[:, cl] for cl in cols],
            [a_ref[h] for h in range(hb)], [b_ref[h] for h in range(hb)],
            [al_ref[h] for h in range(hb)], [dt_ref[h] for h in range(hb)], s0)
        for h in range(hb):
            o_ref[:, cols[h]] = o[h]
            inv_ref[h] = xinv[h]
            state[h] = s1[h]

    return pl.pallas_call(
        body, grid=(nh // hb, nc),
        in_specs=qkv_specs + [row, row, scal, scal],
        out_specs=[o_spec, st, inv],
        out_shape=[jax.ShapeDtypeStruct((s, nh * hd), F32), jax.ShapeDtypeStruct((nh, nc, hd, hd), F32),
                   jax.ShapeDtypeStruct((nh, nc, CHUNK, CHUNK), F32)],
        scratch_shapes=[pltpu.VMEM((hb, hd, hd), F32)],
        compiler_params=_cparams(2), name=name,
    )(qkv, qkv, qkv, a_rows, b_rows, alog, dtb)


def _dn_bwd(qkv, a_rows, b_rows, alog, dtb, states, inverses, do, *, name):
    s = qkv.shape[0]
    nh, nc = a_rows.shape[0], a_rows.shape[1]
    hb = min(DN_HEADS_PER_STEP, nh)
    qkv_specs, row, scal, o_spec, st, inv = _dn_specs(nh, nc, hb, True)
    hd = DN_HEAD_DIM

    assert hb == nh, "dq | dk | dv are written as one [S, 3W] array: all heads in one grid step"
    w = nh * hd

    def body(q_ref, k_ref, v_ref, a_ref, b_ref, al_ref, dt_ref, st_ref, inv_ref, do_ref,
             dqkv_ref, da_ref, db_ref, dal_ref, ddt_ref, dstate):
        @pl.when(pl.program_id(1) == 0)
        def _():
            dstate[...] = jnp.zeros_like(dstate)
            dal_ref[...] = jnp.zeros_like(dal_ref)
            ddt_ref[...] = jnp.zeros_like(ddt_ref)

        cols = [slice(h * hd, (h + 1) * hd) for h in range(hb)]
        heads = range(hb)
        args = ([q_ref[:, cl] for cl in cols], [k_ref[:, cl] for cl in cols], [v_ref[:, cl] for cl in cols],
                [a_ref[h] for h in heads], [b_ref[h] for h in heads], [al_ref[h] for h in heads],
                [dt_ref[h] for h in heads], [st_ref[h] for h in heads])
        saved = [inv_ref[h] for h in heads]
        _, vjp, _ = jax.vjp(lambda *a: _dn_chunk(*a, saved_inverse=saved), *args, has_aux=True)
        dq, dk, dv, da, db, dal, ddt, ds0 = vjp(([do_ref[:, cl] for cl in cols], [dstate[h] for h in heads]))
        for h in heads:
            dqkv_ref[:, h * hd:(h + 1) * hd] = dq[h]
            dqkv_ref[:, w + h * hd:w + (h + 1) * hd] = dk[h]
            dqkv_ref[:, 2 * w + h * hd:2 * w + (h + 1) * hd] = dv[h]
            da_ref[h] = da[h]
            db_ref[h] = db[h]
            dal_ref[h] += dal[h]
            ddt_ref[h] += ddt[h]
            dstate[h] = ds0[h]

    n_of = lambda n: nc - 1 - n
    outs = pl.pallas_call(
        body, grid=(nh // hb, nc),
        in_specs=qkv_specs + [row, row, scal, scal, st, inv, o_spec],
        out_specs=[pl.BlockSpec((CHUNK, 3 * w), lambda h, n: (n_of(n), 0)), row, row, scal, scal],
        out_shape=[jax.ShapeDtypeStruct((s, 3 * w), F32)]
        + [jax.ShapeDtypeStruct(a_rows.shape, F32)] * 2 + [jax.ShapeDtypeStruct((nh, 1, 1), F32)] * 2,
        scratch_shapes=[pltpu.VMEM((hb, hd, hd), F32)],
        compiler_params=_cparams(2), name=name,
    )(qkv, qkv, qkv, a_rows, b_rows, alog, dtb, states, inverses, do)
    return outs


def _dn_post_fwd(o, src, gate_col0, nw, *, name, tm=512):
    s, w = o.shape
    hd = DN_HEAD_DIM
    gc = gate_col0 * LANES // w

    def body(o_ref, g_ref, w_ref, y_ref):
        for h in range(w // hd):
            cols = slice(h * hd, (h + 1) * hd)
            ov = o_ref[:, cols]
            r = lax.rsqrt(jnp.mean(ov * ov, axis=-1, keepdims=True) + EPS)
            y_ref[:, cols] = (ov * r * w_ref[...] * _silu(g_ref[:, cols])).astype(y_ref.dtype)

    blk = pl.BlockSpec((tm, w), lambda i: (i, 0))
    return pl.pallas_call(
        body, grid=(s // tm,),
        in_specs=[blk, pl.BlockSpec((tm, w), lambda i: (i, gc)), pl.BlockSpec((1, hd), lambda i: (0, 0))],
        out_specs=blk, out_shape=jax.ShapeDtypeStruct((s, w), MXU_DTYPE),
        compiler_params=_cparams(1), name=name,
    )(o, src, nw.reshape(1, hd))


def _dn_post_bwd(o, src, gate_col0, nw, dy, into, *, name, tm=512):
    s, w = o.shape
    hd = DN_HEAD_DIM
    gc = gate_col0 * LANES // w

    def body(o_ref, g_ref, w_ref, dy_ref, into_ref, do_ref, dg_ref, dw_ref):
        @pl.when(pl.program_id(0) == 0)
        def _():
            dw_ref[...] = jnp.zeros_like(dw_ref)

        dw = jnp.zeros((1, hd), F32)
        for h in range(w // hd):
            cols = slice(h * hd, (h + 1) * hd)
            ov, gv, dyv = o_ref[:, cols], g_ref[:, cols], dy_ref[:, cols]
            r = lax.rsqrt(jnp.mean(ov * ov, axis=-1, keepdims=True) + EPS)
            oh = ov * r
            sg, sg_grad = _silu_and_grad(gv)
            dn = dyv * sg
            dg_ref[:, cols] = (dyv * (oh * w_ref[...]) * sg_grad).astype(dg_ref.dtype)
            don = dn * w_ref[...]
            do_ref[:, cols] = r * (don - oh * jnp.mean(don * oh, axis=-1, keepdims=True))
            dw = dw + jnp.sum(dn * oh, axis=0, keepdims=True)
        dw_ref[...] += dw

    blk = pl.BlockSpec((tm, w), lambda i: (i, 0))
    wspec = pl.BlockSpec((1, hd), lambda i: (0, 0))
    gate_blk = pl.BlockSpec((tm, w), lambda i: (i, gc))
    do, dg, dw = pl.pallas_call(
        body, grid=(s // tm,),
        in_specs=[blk, gate_blk, wspec, blk, ANY],
        out_specs=[blk, gate_blk, wspec],
        out_shape=[jax.ShapeDtypeStruct((s, w), F32), jax.ShapeDtypeStruct(into.shape, into.dtype),
                   jax.ShapeDtypeStruct((1, hd), F32)],
        input_output_aliases={4: 1},
        compiler_params=_cparams(1), name=name,
    )(o, src, nw.reshape(1, hd), dy, into)
    return do, dg, dw.reshape(hd)


def _sb_consts():
    r2 = lax.broadcasted_iota(jnp.int32, (2 * SB_BLOCK, SB_BLOCK), 0)
    c2 = lax.broadcasted_iota(jnp.int32, (2 * SB_BLOCK, SB_BLOCK), 1)
    r = lax.broadcasted_iota(jnp.int32, (SB_BLOCK, SB_BLOCK), 0)
    c = lax.broadcasted_iota(jnp.int32, (SB_BLOCK, SB_BLOCK), 1)
    lm0 = c < SB_HEAD_DIM
    m_gt = jnp.where(r > c, 1.0, 0.0).astype(BF16)
    m_lt = jnp.where(r < c, 1.0, 0.0).astype(BF16)
    return r2, c2, lm0, m_gt, m_lt


def _sb_stack(x, lm0):
    return jnp.concatenate([jnp.where(lm0, x, 0.0), jnp.where(lm0, 0.0, x)], axis=0)


def _sb_unstack(x2, lm0):
    return jnp.where(lm0, x2[:SB_BLOCK], x2[SB_BLOCK:])


def _sb_fwd(src, col0, width, *, name):
    s = src.shape[0]
    nq = s // SB_BLOCK
    npair = width // LANES
    scale = SB_HEAD_DIM ** -0.5
    nu = math.gcd(SB_UNROLL, nq)

    def body(q_ref, k_ref, v_ref, o_ref, w_hbm, stage, sems):
        p, i = pl.program_id(0), pl.program_id(1)
        r2, c2, lm0, m_gt, _ = _sb_consts()
        t_glob = i * SB_BLOCK + (r2 & (SB_BLOCK - 1))
        q2 = (_sb_stack(q_ref[...], lm0) * scale).astype(MXU_DTYPE)

        t = p * nq + i
        half = t % 2
        ngrp = nq // nu

        def save(half_, grp, pp, ii):
            return pltpu.make_async_copy(stage.at[half_, grp], w_hbm.at[pp, ii, grp], sems.at[half_, grp])

        def drain(half_, pp, ii):
            for grp in range(ngrp):
                @pl.when(grp <= ii // nu)
                def _():
                    save(half_, grp, pp, ii).wait()

        def group(base, carry, masked):
            o2, rsum = carry
            js = [base + nu - 1 - u for u in range(nu)]
            offs = [pl.multiple_of(j * SB_BLOCK, SB_BLOCK) for j in js]
            zs = [_dot(q2, k_ref[pl.ds(off, SB_BLOCK), :].astype(MXU_DTYPE), NT) for off in offs]
            ts = [jnp.log(1.0 + jnp.exp(-jnp.abs(z))) for z in zs]
            lks = [-(jnp.maximum(z, 0.0) + t) for z, t in zip(zs, ts)]
            if masked:
                masks = [(j * SB_BLOCK + c2) < t_glob for j in js]
                lks = [jnp.where(mk, lk, 0.0) for mk, lk in zip(masks, lks)]
            sufs = [_split_dot(lk, m_gt, SB_SPLIT) for lk in lks]
            rs = [rsum]
            for lk in lks:
                rs.append(rs[-1] + jnp.sum(lk, axis=1, keepdims=True))
            wgts = [jnp.exp((jnp.minimum(z, 0.0) - t) + r_ + sf) for z, t, r_, sf in zip(zs, ts, rs, sufs)]
            if masked:
                wgts = [jnp.where(mk, wg, 0.0) for mk, wg in zip(masks, wgts)]
            wbs = [wg.astype(MXU_DTYPE) for wg in wgts]
            grp = base // nu
            for u, wb in enumerate(wbs):
                stage[half, grp, nu - 1 - u] = wb
            save(half, grp, p, i).start()
            for off, wb in zip(offs, wbs):
                o2 = o2 + _dot(wb, v_ref[pl.ds(off, SB_BLOCK), :].astype(MXU_DTYPE), NN)
            return o2, rs[-1]

        top0 = (i // nu) * nu
        last = i // nu
        carry = group(top0, (jnp.zeros((2 * SB_BLOCK, LANES), F32), jnp.zeros((2 * SB_BLOCK, 1), F32)), True)
        o2, _ = lax.fori_loop(1, last + 1, lambda g, cr: group(top0 - nu * g, cr, False), carry)
        o_ref[...] = _sb_unstack(o2, lm0)

        @pl.when(t >= 1)
        def _():
            drain(1 - half, (t - 1) // nq, (t - 1) % nq)

        @pl.when(t == npair * nq - 1)
        def _():
            drain(half, p, i)

    blk = pl.BlockSpec((SB_BLOCK, LANES), lambda p, i: (i, p))
    return pl.pallas_call(
        body, grid=(npair, nq),
        in_specs=[pl.BlockSpec((SB_BLOCK, LANES), lambda p, i: (i, col0 + p)),
                  pl.BlockSpec((s, LANES), lambda p, i: (0, col0 + npair + p)),
                  pl.BlockSpec((s, LANES), lambda p, i: (0, col0 + 2 * npair + p))],
        out_specs=[blk, ANY],
        out_shape=[jax.ShapeDtypeStruct((s, width), F32),
                   jax.ShapeDtypeStruct((npair, nq, nq // nu, nu, 2 * SB_BLOCK, LANES), MXU_DTYPE)],
        scratch_shapes=[pltpu.VMEM((2, nq // nu, nu, 2 * SB_BLOCK, LANES), MXU_DTYPE),
                        pltpu.SemaphoreType.DMA((2, nq // nu))],
        compiler_params=_cparams(2), name=name,
    )(src, src, src)


def _sb_bwd(src, col0, width, weights, do, *, name):
    s = src.shape[0]
    nq = s // SB_BLOCK
    npair = width // LANES
    scale = SB_HEAD_DIM ** -0.5
    nu = math.gcd(SB_UNROLL, nq)

    def body(q_ref, k_ref, v_ref, w_hbm, do_ref, dq_ref, dk_ref, dv_ref, stage, sems):
        p, i = pl.program_id(0), pl.program_id(1)

        @pl.when(i == 0)
        def _():
            dk_ref[...] = jnp.zeros_like(dk_ref)
            dv_ref[...] = jnp.zeros_like(dv_ref)

        r2, c2, lm0, _, m_lt = _sb_consts()
        t_glob = i * SB_BLOCK + (r2 & (SB_BLOCK - 1))
        q2 = (_sb_stack(q_ref[...], lm0) * scale).astype(MXU_DTYPE)
        do2 = _sb_stack(do_ref[...], lm0).astype(MXU_DTYPE)

        ngrp = nq // nu

        def load(half_, grp, pp, ii):
            return pltpu.make_async_copy(w_hbm.at[pp, ii, grp], stage.at[half_, grp], sems.at[half_, grp])

        def fetch_step(half_, pp, ii):
            for grp in range(ngrp):
                @pl.when(grp <= ii // nu)
                def _():
                    load(half_, grp, pp, ii).start()

        def group(g, carry, masked, slot):
            dq2, csum = carry
            js = [nu * g + u for u in range(nu)]
            offs = [pl.multiple_of(j * SB_BLOCK, SB_BLOCK) for j in js]
            kbs = [k_ref[pl.ds(off, SB_BLOCK), :].astype(MXU_DTYPE) for off in offs]
            zs = [_dot(q2, kb, NT) for kb in kbs]
            dws = [_dot(do2, v_ref[pl.ds(off, SB_BLOCK), :].astype(MXU_DTYPE), NT) for off in offs]
            wbs = [stage[slot[0], slot[1], u] for u in range(nu)]
            sigs = [_sigmoid(z) for z in zs]
            dlogas = [wb.astype(F32) * dw for wb, dw in zip(wbs, dws)]
            pres = [_split_dot(dl, m_lt, SB_SPLIT) for dl in dlogas]
            dlks = []
            for dl, pre in zip(dlogas, pres):
                dlks.append(csum + pre)
                csum = csum + jnp.sum(dl, axis=1, keepdims=True)
            if masked:
                dlks = [jnp.where((j * SB_BLOCK + c2) < t_glob, dlk, 0.0) for j, dlk in zip(js, dlks)]
            dzbs = [(dl * (1.0 - sg) - dlk * sg).astype(MXU_DTYPE) for dl, sg, dlk in zip(dlogas, sigs, dlks)]
            for off, dzb, wb, kb in zip(offs, dzbs, wbs, kbs):
                dk_ref[pl.ds(off, SB_BLOCK), :] += _dot(dzb, q2, TN)
                dv_ref[pl.ds(off, SB_BLOCK), :] += _dot(wb, do2, TN)
                dq2 = dq2 + _dot(dzb, kb, NN)
            return dq2, csum

        t = p * nq + i
        half = t % 2

        @pl.when(t == 0)
        def _():
            fetch_step(0, p, i)

        @pl.when(t + 1 < npair * nq)
        def _():
            fetch_step(1 - half, (t + 1) // nq, (t + 1) % nq)

        def step(g, carry):
            load(half, g, p, i).wait()
            return group(g, carry, False, (half, g))

        last = i // nu
        carry = lax.fori_loop(0, last, step, (jnp.zeros((2 * SB_BLOCK, LANES), F32), jnp.zeros((2 * SB_BLOCK, 1), F32)))
        load(half, last, p, i).wait()
        dq2, _ = group(last, carry, True, (half, last))
        dq_ref[...] = _sb_unstack(dq2, lm0) * scale

    blk = pl.BlockSpec((SB_BLOCK, LANES), lambda p, i: (i, p))
    full = pl.BlockSpec((s, LANES), lambda p, i: (0, p))
    return pl.pallas_call(
        body, grid=(npair, nq),
        in_specs=[pl.BlockSpec((SB_BLOCK, LANES), lambda p, i: (i, col0 + p)),
                  pl.BlockSpec((s, LANES), lambda p, i: (0, col0 + npair + p)),
                  pl.BlockSpec((s, LANES), lambda p, i: (0, col0 + 2 * npair + p)),
                  ANY, blk],
        out_specs=[blk, full, full],
        out_shape=[jax.ShapeDtypeStruct((s, width), F32)] * 3,
        scratch_shapes=[pltpu.VMEM((2, nq // nu, nu, 2 * SB_BLOCK, LANES), MXU_DTYPE),
                        pltpu.SemaphoreType.DMA((2, nq // nu))],
        compiler_params=_cparams(2), name=name,
    )(src, src, src, weights, do)


def _ssd_group(xs, dt_rows, alogs, dtbs, bms, cms, h0s):
    c = bms[0].shape[0]
    per = len(xs) // len(bms)
    ii, jj = _chunk_masks(c)
    causal, eye = ii >= jj, ii == jj
    grp = lambda per_group: [t for t in per_group for _ in range(per)]
    scores, bm, cm = grp(_each(lambda c_, b_: _hdot(c_, b_, NT), cms, bms)), grp(bms), grp(cms)
    dt_r = _each(lambda dt, b: _softplus(dt + b), dt_rows, dtbs)
    a_r = _each(lambda al, dt: -jnp.exp(al) * dt, alogs, dt_r)
    dt_col = _each(lambda dt: _row_to_col(dt, eye), dt_r)
    a_col = _each(lambda a: _row_to_col(a, eye), a_r)
    ac_col = _each(lambda a: jnp.sum(jnp.where(causal, a, 0.0), axis=1, keepdims=True), a_r)
    ac_row = _each(lambda a: jnp.sum(jnp.where(jj >= ii, a, 0.0), axis=0, keepdims=True), a_col)
    lmat = _each(lambda c_, r_: jnp.exp(jnp.where(causal, c_ - r_, NEG_BIG)), ac_col, ac_row)
    xdt = _each(jnp.multiply, xs, dt_col)
    al = _each(lambda a: jnp.sum(a, axis=1, keepdims=True), a_r)
    ys = _each(lambda sc, lm, xd, cm_, h0, ac: _hdot(sc * lm, xd) + _hdot(cm_, h0, NT) * jnp.exp(ac),
               scores, lmat, xdt, cm, h0s, ac_col)
    h1s = _each(lambda h0, al_, xd, ac, bm_: h0 * jnp.exp(al_) + _hdot(xd * jnp.exp(al_ - ac), bm_, TN),
                h0s, al, xdt, ac_col, bm)
    return ys, h1s


def _ssd_specs(ng, nc, r, gb, rev):
    n_of = (lambda n: nc - 1 - n) if rev else (lambda n: n)
    xw, bw = gb * r * SSM_HEAD_DIM, gb * SSM_STATE
    b0, c0 = (ng * r * SSM_HEAD_DIM) // bw, (ng * r * SSM_HEAD_DIM + ng * SSM_STATE) // bw
    x_spec = pl.BlockSpec((CHUNK, xw), lambda g, n: (n_of(n), g))
    b_spec = pl.BlockSpec((CHUNK, bw), lambda g, n: (n_of(n), b0 + g))
    c_spec = pl.BlockSpec((CHUNK, bw), lambda g, n: (n_of(n), c0 + g))
    dt_spec = pl.BlockSpec((gb, None, r, CHUNK), lambda g, n: (g, n_of(n), 0, 0))
    sc_spec = pl.BlockSpec((gb, r, 1), lambda g, n: (g, 0, 0))
    st_spec = pl.BlockSpec((gb, None, r, SSM_HEAD_DIM, SSM_STATE), lambda g, n: (g, n_of(n), 0, 0, 0))
    bc_out = pl.BlockSpec((CHUNK, bw), lambda g, n: (n_of(n), g))
    return x_spec, b_spec, c_spec, dt_spec, sc_spec, st_spec, x_spec, bc_out


def _ssd_refs(gb, r, x_ref, b_ref, c_ref, dt_ref, al_ref, db_ref):
    p, n = SSM_HEAD_DIM, SSM_STATE
    heads = [(g, h) for g in range(gb) for h in range(r)]
    xs = [x_ref[:, (g * r + h) * p:(g * r + h + 1) * p] for g, h in heads]
    dts = [dt_ref[g, h:h + 1, :] for g, h in heads]
    als = [al_ref[g, h:h + 1, :] for g, h in heads]
    dbs = [db_ref[g, h:h + 1, :] for g, h in heads]
    bms = [b_ref[:, g * n:(g + 1) * n] for g in range(gb)]
    cms = [c_ref[:, g * n:(g + 1) * n] for g in range(gb)]
    return heads, xs, dts, als, dbs, bms, cms


def _ssd_fwd(xbc, dt_rows, alog, dtb, *, name):
    s = xbc.shape[0]
    ng, nc, r = dt_rows.shape[0], dt_rows.shape[1], dt_rows.shape[2]
    w = ng * r * SSM_HEAD_DIM
    gb = math.gcd(SSD_GROUPS_PER_STEP, ng)
    x_spec, b_spec, c_spec, dt_spec, sc_spec, st_spec, y_spec, _ = _ssd_specs(ng, nc, r, gb, False)
    p = SSM_HEAD_DIM

    def body(x_ref, b_ref, c_ref, dt_ref, al_ref, db_ref, y_ref, st_ref, state):
        @pl.when(pl.program_id(1) == 0)
        def _():
            state[...] = jnp.zeros_like(state)

        st_ref[...] = state[...]
        heads, xs, dts, als, dbs, bms, cms = _ssd_refs(gb, r, x_ref, b_ref, c_ref, dt_ref, al_ref, db_ref)
        ys, h1s = _ssd_group(xs, dts, als, dbs, bms, cms, [state[g, h] for g, h in heads])
        for i, (g, h) in enumerate(heads):
            y_ref[:, (g * r + h) * p:(g * r + h + 1) * p] = ys[i]
            state[g, h] = h1s[i]

    return pl.pallas_call(
        body, grid=(ng // gb, nc),
        in_specs=[x_spec, b_spec, c_spec, dt_spec, sc_spec, sc_spec],
        out_specs=[y_spec, st_spec],
        out_shape=[jax.ShapeDtypeStruct((s, w), F32), jax.ShapeDtypeStruct((ng, nc, r, p, SSM_STATE), F32)],
        scratch_shapes=[pltpu.VMEM((gb, r, p, SSM_STATE), F32)],
        compiler_params=_cparams(2), name=name,
    )(xbc, xbc, xbc, dt_rows, alog, dtb)


def _ssd_bwd(xbc, dt_rows, alog, dtb, states, dy, *, name):
    s = xbc.shape[0]
    ng, nc, r = dt_rows.shape[0], dt_rows.shape[1], dt_rows.shape[2]
    w = ng * r * SSM_HEAD_DIM
    gb = math.gcd(SSD_GROUPS_PER_STEP, ng)
    x_spec, b_spec, c_spec, dt_spec, sc_spec, st_spec, y_spec, bc_out = _ssd_specs(ng, nc, r, gb, True)
    p = SSM_HEAD_DIM

    def body(x_ref, b_ref, c_ref, dt_ref, al_ref, db_ref, st_ref, dy_ref,
             dx_ref, dbm_ref, dcm_ref, ddt_ref, dal_ref, ddb_ref, dstate):
        @pl.when(pl.program_id(1) == 0)
        def _():
            dstate[...] = jnp.zeros_like(dstate)
            dal_ref[...] = jnp.zeros_like(dal_ref)
            ddb_ref[...] = jnp.zeros_like(ddb_ref)

        heads, xs, dts, als, dbs, bms, cms = _ssd_refs(gb, r, x_ref, b_ref, c_ref, dt_ref, al_ref, db_ref)
        _, vjp = jax.vjp(_ssd_group, xs, dts, als, dbs, bms, cms, [st_ref[g, h] for g, h in heads])
        dys = [dy_ref[:, (g * r + h) * p:(g * r + h + 1) * p] for g, h in heads]
        dxs, ddts, dals, ddbs, dbms, dcms, dh0s = vjp((dys, [dstate[g, h] for g, h in heads]))
        for g in range(gb):
            dbm_ref[:, g * SSM_STATE:(g + 1) * SSM_STATE] = dbms[g]
            dcm_ref[:, g * SSM_STATE:(g + 1) * SSM_STATE] = dcms[g]
        for i, (g, h) in enumerate(heads):
            dx_ref[:, (g * r + h) * p:(g * r + h + 1) * p] = dxs[i]
            ddt_ref[g, h:h + 1, :] = ddts[i]
            dal_ref[g, h:h + 1, :] += dals[i]
            ddb_ref[g, h:h + 1, :] += ddbs[i]
            dstate[g, h] = dh0s[i]

    gn = ng * SSM_STATE
    return pl.pallas_call(
        body, grid=(ng // gb, nc),
        in_specs=[x_spec, b_spec, c_spec, dt_spec, sc_spec, sc_spec, st_spec, y_spec],
        out_specs=[y_spec, bc_out, bc_out, dt_spec, sc_spec, sc_spec],
        out_shape=[jax.ShapeDtypeStruct((s, w), F32), jax.ShapeDtypeStruct((s, gn), F32), jax.ShapeDtypeStruct((s, gn), F32),
                   jax.ShapeDtypeStruct(dt_rows.shape, F32), jax.ShapeDtypeStruct((ng, r, 1), F32),
                   jax.ShapeDtypeStruct((ng, r, 1), F32)],
        scratch_shapes=[pltpu.VMEM((gb, r, p, SSM_STATE), F32)],
        compiler_params=_cparams(2), name=name,
    )(xbc, xbc, xbc, dt_rows, alog, dtb, states, dy)


def _ssm_post_fwd(y, xbc, src, z_col0, dexp, nw, *, name, tm=512):
    s, w = y.shape
    gw = w // SSM_GROUPS
    zc = z_col0 * LANES // gw

    def body(y_ref, x_ref, z_ref, d_ref, w_ref, o_ref):
        yy = (y_ref[...] + x_ref[...] * d_ref[...]) * _silu(z_ref[...])
        r = lax.rsqrt(jnp.mean(yy * yy, axis=-1, keepdims=True) + EPS)
        o_ref[...] = (yy * r * w_ref[...]).astype(o_ref.dtype)

    blk = pl.BlockSpec((tm, gw), lambda g, i: (i, g))
    vec = pl.BlockSpec((1, gw), lambda g, i: (0, g))
    return pl.pallas_call(
        body, grid=(SSM_GROUPS, s // tm),
        in_specs=[blk, blk, pl.BlockSpec((tm, gw), lambda g, i: (i, zc + g)), vec, vec],
        out_specs=blk, out_shape=jax.ShapeDtypeStruct((s, w), MXU_DTYPE),
        compiler_params=_cparams(2), name=name,
    )(y, xbc, src, dexp.reshape(1, w), nw.reshape(1, w))


def _ssm_post_bwd(y, xbc, src, z_col0, dexp, nw, dout, into, *, name, tm=512):
    s, w = y.shape
    gw = w // SSM_GROUPS
    zc = z_col0 * LANES // gw

    def body(y_ref, x_ref, z_ref, d_ref, w_ref, do_ref, into_ref, dy_ref, dx_ref, dz_ref, dd_ref, dw_ref):
        xv, zv, dv = x_ref[...], z_ref[...], d_ref[...]
        pre = y_ref[...] + xv * dv
        sz, sz_grad = _silu_and_grad(zv)
        yy = pre * sz
        r = lax.rsqrt(jnp.mean(yy * yy, axis=-1, keepdims=True) + EPS)
        yh = yy * r
        dov = do_ref[...]
        dyn = dov * w_ref[...]
        dyy = r * (dyn - yh * jnp.mean(dyn * yh, axis=-1, keepdims=True))
        dpre = dyy * sz
        dy_ref[...] = dpre
        dx_ref[...] = dpre * dv
        dz_ref[...] = (dyy * pre * sz_grad).astype(dz_ref.dtype)

        @pl.when(pl.program_id(1) == 0)
        def _():
            dd_ref[...] = jnp.zeros_like(dd_ref)
            dw_ref[...] = jnp.zeros_like(dw_ref)

        dd_ref[...] += jnp.sum(dpre * xv, axis=0, keepdims=True)
        dw_ref[...] += jnp.sum(dov * yh, axis=0, keepdims=True)

    blk = pl.BlockSpec((tm, gw), lambda g, i: (i, g))
    vec = pl.BlockSpec((1, gw), lambda g, i: (0, g))
    z_blk = pl.BlockSpec((tm, gw), lambda g, i: (i, zc + g))
    dy, dx, dz, dd, dw = pl.pallas_call(
        body, grid=(SSM_GROUPS, s // tm),
        in_specs=[blk, blk, z_blk, vec, vec, blk, ANY],
        out_specs=[blk, blk, z_blk, vec, vec],
        out_shape=[jax.ShapeDtypeStruct((s, w), F32), jax.ShapeDtypeStruct((s, w), F32),
                   jax.ShapeDtypeStruct(into.shape, into.dtype), jax.ShapeDtypeStruct((1, w), F32),
                   jax.ShapeDtypeStruct((1, w), F32)],
        input_output_aliases={6: 2},
        compiler_params=_cparams(2), name=name,
    )(y, xbc, src, dexp.reshape(1, w), nw.reshape(1, w), dout, into)
    return dy, dx, dz, dd.reshape(w), dw.reshape(w)


def _merge_fwd(proj3, src, gate_col0, d, *, name, tm=512):
    s = proj3.shape[0]
    nb = proj3.shape[1] // d
    gc = gate_col0 * LANES // d

    def body(*refs):
        p_refs, g_refs, o_ref = refs[:nb], refs[nb:2 * nb], refs[-1]
        acc = None
        for p_ref, g_ref in zip(p_refs, g_refs):
            term = _sigmoid(g_ref[...]) * p_ref[...]
            acc = term if acc is None else acc + term
        o_ref[...] = acc.astype(o_ref.dtype)

    p_specs = [pl.BlockSpec((tm, d), lambda i, b=b: (i, b)) for b in range(nb)]
    g_specs = [pl.BlockSpec((tm, d), lambda i, b=b: (i, gc + b)) for b in range(nb)]
    return pl.pallas_call(
        body, grid=(s // tm,), in_specs=p_specs + g_specs,
        out_specs=pl.BlockSpec((tm, d), lambda i: (i, 0)), out_shape=jax.ShapeDtypeStruct((s, d), MXU_DTYPE),
        compiler_params=_cparams(1), name=name,
    )(*([proj3] * nb), *([src] * nb))


def _merge_bwd(proj3, src, gate_col0, d, dmerged, into, *, name, tm=512):
    s = proj3.shape[0]
    nb = proj3.shape[1] // d
    gc = gate_col0 * LANES // d

    def body(p_ref, g_ref, dm_ref, into_ref, dp_ref, dg_ref):
        sg = _sigmoid(g_ref[...])
        dm = dm_ref[...]
        dp_ref[...] = (dm * sg).astype(dp_ref.dtype)
        dg_ref[...] = (dm * p_ref[...] * sg * (1.0 - sg)).astype(dg_ref.dtype)

    blk = pl.BlockSpec((tm, d), lambda i, b: (i, b))
    gate_blk = pl.BlockSpec((tm, d), lambda i, b: (i, gc + b))
    return pl.pallas_call(
        body, grid=(s // tm, nb),
        in_specs=[blk, gate_blk, pl.BlockSpec((tm, d), lambda i, b: (i, 0)), ANY],
        out_specs=[blk, gate_blk],
        out_shape=[jax.ShapeDtypeStruct(proj3.shape, MXU_DTYPE), jax.ShapeDtypeStruct(into.shape, into.dtype)],
        input_output_aliases={3: 1},
        compiler_params=_cparams(2), name=name,
    )(proj3, src, dmerged, into)


ANY = pl.BlockSpec(memory_space=pl.ANY)
MESH = pl.DeviceIdType.MESH


def _all_gather(shards, *, name, after=None):
    nt = len(shards)
    n_after = 0 if after is None else 1

    def body(*refs):
        x_refs, out_refs = refs[:nt], refs[nt + n_after:2 * nt + n_after]
        send_sems, recv_sems, local_sems = refs[2 * nt + n_after:]
        x, y, c = lax.axis_index("x"), lax.axis_index("y"), lax.axis_index("c")
        me, sibling = (x, y, c), (x, y, 1 - c)
        chips = [(1 - x, y), (x, 1 - y), (1 - x, 1 - y)]

        def slot(t, px, py, pc):
            return out_refs[t].at[4 * px + 2 * py + pc]

        def copy(t, k, block, to, from_input=False):
            return pltpu.make_async_remote_copy(
                src_ref=x_refs[t] if from_input else slot(t, *block), dst_ref=slot(t, *block),
                send_sem=send_sems.at[7 * t + k], recv_sem=recv_sems.at[7 * t + k], device_id=to, device_id_type=MESH)

        mine = [pltpu.make_async_copy(x_refs[t], slot(t, *me), local_sems.at[t]) for t in range(nt)]
        for cp in mine:
            cp.start()
        first = [copy(t, 0, me, sibling, True) for t in range(nt)]
        first += [copy(t, 1 + j, me, (*chip, c), True) for j, chip in enumerate(chips) for t in range(nt)]
        for cp in first:
            cp.start()
        passed = []
        for j, chip in enumerate(chips):
            for t in range(nt):
                copy(t, 1 + j, (*chip, c), me).wait_recv()
                fwd = copy(t, 4 + j, (*chip, c), sibling)
                fwd.start()
                passed.append(fwd)
        for t in range(nt):
            copy(t, 0, sibling, me).wait_recv()
            for j, chip in enumerate(chips):
                copy(t, 4 + j, (*chip, 1 - c), me).wait_recv()
        for cp in first + passed:
            cp.wait_send()
        for cp in mine:
            cp.wait()

    return pl.pallas_call(
        body, out_shape=[jax.ShapeDtypeStruct((N_DEV,) + a.shape, a.dtype) for a in shards],
        in_specs=[ANY] * (nt + n_after), out_specs=[ANY] * nt,
        scratch_shapes=[pltpu.SemaphoreType.DMA((7 * nt,)), pltpu.SemaphoreType.DMA((7 * nt,)),
                        pltpu.SemaphoreType.DMA((nt,))],
        name=name,
    )(*shards, *([] if after is None else [after]))


def _grad_exchange(bigs, small, *, name):
    nl = len(bigs[0])
    flat = [a for per_layer in bigs for a in per_layer]
    nslot = len(flat)

    def body(*refs):
        in_refs, small_ref = refs[:nslot], refs[nslot]
        out_refs, smallr_ref = refs[nslot + 1:nslot + 1 + len(bigs)], refs[nslot + 1 + len(bigs)]
        send_sems, recv_sems, local_sems = refs[nslot + 2 + len(bigs):]
        x, y, c = lax.axis_index("x"), lax.axis_index("y"), lax.axis_index("c")
        me = 4 * x + 2 * y + c
        local = [pltpu.make_async_copy(in_refs[i].at[me], out_refs[i // nl].at[me, i % nl], local_sems.at[i])
                 for i in range(nslot)]
        local.append(pltpu.make_async_copy(small_ref, smallr_ref.at[me], local_sems.at[nslot]))
        for cp in local:
            cp.start()
        copies = []
        for k in range(1, N_DEV):
            px = x ^ ((k >> 2) & 1)
            py = y ^ ((k >> 1) & 1)
            pc = c ^ (k & 1)
            peer = 4 * px + 2 * py + pc
            for i in range(nslot + 1):
                sem = 7 * i + (k - 1)
                src = in_refs[i].at[peer] if i < nslot else small_ref
                dst = out_refs[i // nl].at[me, i % nl] if i < nslot else smallr_ref.at[me]
                copies.append(pltpu.make_async_remote_copy(
                    src_ref=src, dst_ref=dst, send_sem=send_sems.at[sem], recv_sem=recv_sems.at[sem],
                    device_id=(px, py, pc), device_id_type=MESH))
        for cp in copies:
            cp.start()
        for cp in copies:
            cp.wait_recv()
        for cp in copies:
            cp.wait_send()
        for cp in local:
            cp.wait()

    out_shape = [jax.ShapeDtypeStruct((N_DEV, nl) + per_layer[0].shape[1:], per_layer[0].dtype) for per_layer in bigs]
    out_shape.append(jax.ShapeDtypeStruct((N_DEV,) + small.shape, small.dtype))
    nsem = 7 * (nslot + 1)
    outs = pl.pallas_call(
        body, out_shape=out_shape,
        in_specs=[ANY] * (nslot + 1), out_specs=[ANY] * (len(bigs) + 1),
        scratch_shapes=[pltpu.SemaphoreType.DMA((nsem,)), pltpu.SemaphoreType.DMA((nsem,)),
                        pltpu.SemaphoreType.DMA((nslot + 1,))],
        name=name,
    )(*flat, small)
    return outs[:-1], outs[-1]


HBM = pl.BlockSpec(memory_space=pltpu.HBM)
SEM = pl.BlockSpec(memory_space=pltpu.SEMAPHORE)
EFFECT = pltpu.SideEffectType.DATAFLOW_SIDE_EFFECTING


def _peers():
    x, y, c = lax.axis_index("x"), lax.axis_index("y"), lax.axis_index("c")
    peers = []
    for k in range(1, N_DEV):
        px, py, pc = x ^ ((k >> 2) & 1), y ^ ((k >> 1) & 1), c ^ (k & 1)
        peers.append(((px, py, pc), 4 * px + 2 * py + pc))
    return 4 * x + 2 * y + c, peers


def _split_copies(slots, src_refs, land_refs, send_sems, recv_sems):
    me, peers = _peers()
    copies = []
    for t, (whole, layer) in enumerate(slots):
        dst = land_refs[t].at[me] if layer is None else land_refs[t].at[me, layer]
        for k, (dev, lin) in enumerate(peers):
            copies.append(pltpu.make_async_remote_copy(
                src_ref=src_refs[t] if whole else src_refs[t].at[lin], dst_ref=dst,
                send_sem=send_sems.at[7 * t + k], recv_sem=recv_sems.at[7 * t + k], device_id=dev, device_id_type=MESH))
    return copies


def _split_start(srcs, lands, slots, carry, *, name):
    n = len(srcs)

    def body(*refs):
        copies = _split_copies(slots, refs[:n], refs[n:2 * n], refs[2 * n + 1], refs[2 * n + 2])
        for cp in copies:
            cp.start()

    def hbm(a):
        return pltpu.HBM(a.shape, a.dtype)

    outs = pl.pallas_call(
        body, name=name,
        out_shape=[pltpu.SemaphoreType.DMA((7 * n,)), pltpu.SemaphoreType.DMA((7 * n,))]
        + [hbm(a) for a in srcs] + [hbm(a) for a in lands] + [hbm(carry)],
        in_specs=[HBM] * (2 * n + 1), out_specs=[SEM, SEM] + [HBM] * (2 * n + 1),
        input_output_aliases={i: 2 + i for i in range(2 * n + 1)},
        compiler_params=pltpu.CompilerParams(has_side_effects=EFFECT),
    )(*[pltpu.with_memory_space_constraint(a, pltpu.HBM) for a in list(srcs) + list(lands) + [carry]])
    return outs[0], outs[1], outs[2:2 + n], outs[2 + n:2 + 2 * n], outs[2 + 2 * n]


def _split_wait(send_sems, recv_sems, srcs, lands, slots, after, *, name):
    n = len(srcs)

    def body(*refs):
        copies = _split_copies(slots, refs[:n], refs[n:2 * n], refs[2 * n], refs[2 * n + 1])
        for cp in copies:
            cp.wait_send()
        for cp in copies:
            cp.wait_recv()

    outs = pl.pallas_call(
        body, name=name,
        out_shape=[pltpu.HBM(a.shape, a.dtype) for a in list(srcs) + list(lands)],
        in_specs=[HBM] * (2 * n) + [SEM, SEM, ANY], out_specs=[HBM] * (2 * n),
        input_output_aliases={i: i for i in range(2 * n)},
        compiler_params=pltpu.CompilerParams(has_side_effects=EFFECT),
    )(*srcs, *lands, send_sems, recv_sems, after)
    return outs[n:]


def _adam_math(w, g, m, v):
    m1 = ADAM_B1 * m + (1.0 - ADAM_B1) * g
    v1 = ADAM_B2 * v + (1.0 - ADAM_B2) * (g * g)
    m_hat = m1 / (1.0 - ADAM_B1 ** ADAM_STEP)
    v_hat = v1 / (1.0 - ADAM_B2 ** ADAM_STEP)
    delta = -ADAM_LR * (m_hat / (jnp.sqrt(v_hat) + ADAM_EPS) + ADAM_WD * w)
    return delta, m1, v1


def _sum_adamw(parts, w, m, v, layer, prev, *, name):
    shape = w.shape
    r, c = shape[-2], shape[-1]
    a_l = math.prod(shape[1:-2])
    a = shape[0] * a_l
    base = layer * a_l
    if r % 256 == 0:
        tr, tc = 256, c
    else:
        tr, tc = r, _pick(c, (256, 128))
    w3, m3, v3 = (t.reshape(a, r, c) for t in (w, m, v))
    n_prev = 0 if prev is None else 4

    def body(*refs):
        p_ref, w_ref, m_ref, v_ref = refs[:4]
        g_ref, d_ref, m1_ref, v1_ref = refs[4 + n_prev:]
        g = p_ref[0].astype(F32)
        for src in range(1, N_DEV):
            g = g + p_ref[src].astype(F32)
        delta, m1, v1 = _adam_math(w_ref[...], g, m_ref[...], v_ref[...])
        g_ref[...] = g
        d_ref[...] = delta
        m1_ref[...] = m1
        v1_ref[...] = v1

    nr, ncol = r // tr, c // tc
    blk = pl.BlockSpec((None, tr, tc), lambda i, j: (base + i, j // ncol, j % ncol))
    prev3 = [] if prev is None else [t.reshape(a, r, c) for t in prev]
    outs = pl.pallas_call(
        body, grid=(a_l, nr * ncol),
        in_specs=[pl.BlockSpec((N_DEV, None, tr, tc), lambda i, j: (0, i, j // ncol, j % ncol)), blk, blk, blk]
        + [ANY] * n_prev,
        out_specs=[blk] * 4, out_shape=[jax.ShapeDtypeStruct((a, r, c), F32)] * 4,
        input_output_aliases={4 + k: k for k in range(n_prev)},
        compiler_params=_cparams(2), name=name,
    )(parts.reshape(N_DEV, a_l, r, c), w3, m3, v3, *prev3)
    return [o.reshape(shape) for o in outs]


def _sum_parts(parts, *, name):
    rows = parts.shape[1]

    def body(p_ref, o_ref):
        g = p_ref[0]
        for src in range(1, N_DEV):
            g = g + p_ref[src]
        o_ref[...] = g

    return pl.pallas_call(
        body, grid=(1,), in_specs=[pl.BlockSpec((N_DEV, rows, LANES), lambda i: (0, 0, 0))],
        out_specs=pl.BlockSpec((rows, LANES), lambda i: (0, 0)), out_shape=jax.ShapeDtypeStruct((rows, LANES), F32),
        compiler_params=_cparams(1), name=name,
    )(parts)


def _adamw(w, g, m, v, *, name):
    rows = w.shape[0]

    def body(w_ref, g_ref, m_ref, v_ref, d_ref, m1_ref, v1_ref):
        delta, m1, v1 = _adam_math(w_ref[...], g_ref[...], m_ref[...], v_ref[...])
        d_ref[...] = delta
        m1_ref[...] = m1
        v1_ref[...] = v1

    blk = pl.BlockSpec((rows, LANES), lambda i: (0, 0))
    return pl.pallas_call(
        body, grid=(1,), in_specs=[blk] * 4, out_specs=[blk] * 3,
        out_shape=[jax.ShapeDtypeStruct((rows, LANES), F32)] * 3,
        compiler_params=_cparams(1), name=name,
    )(w, g, m, v)


def _pack(arrs, dtype, row_mult=16):
    flat = jnp.concatenate([a.reshape(-1).astype(dtype) for a in arrs])
    n = flat.shape[0]
    rows = -(-n // (LANES * row_mult)) * row_mult
    flat = jnp.pad(flat, (0, rows * LANES - n))
    return flat.reshape(rows, LANES)


def _unpack(packed, shapes):
    flat = packed.reshape(-1)
    out, off = [], 0
    for shp in shapes:
        n = math.prod(shp)
        out.append(flat[off:off + n].reshape(shp))
        off += n
    return out


class _Layout:
    def __init__(self, d):
        self.d = d
        w = d
        self.dn_heads = w // DN_HEAD_DIM
        self.ssm_heads = w // SSM_HEAD_DIM
        gn = SSM_GROUPS * SSM_STATE
        self.sizes = (3 * w, w, self.dn_heads, self.dn_heads, 3 * w, w, w + 2 * gn, self.ssm_heads, 3 * d)
        offs, o = [], 0
        for sz in self.sizes:
            offs.append(o)
            o += sz
        self.offs = offs
        self.in_dim = o
        self.big = (0, 1, 4, 5, 6, 8)
        self.small = (2, 3, 7)
        cols, o = {}, 0
        for idx in self.big:
            cols[idx] = o
            o += self.sizes[idx]
        self.small_col = o
        self.cols = cols
        self.padded = o + LANES
        self.n_small = sum(self.sizes[i] for i in self.small)

    def reorder_w(self, w_in):
        parts = [w_in[:, self.offs[i]:self.offs[i] + self.sizes[i]] for i in self.big + self.small]
        parts.append(jnp.zeros((w_in.shape[0], LANES - self.n_small), w_in.dtype))
        return jnp.concatenate(parts, axis=1)

    def from_shards(self, parts):
        cs = self.in_dim // N_DEV
        pieces = []
        for i in self.big + self.small:
            a, b = self.offs[i], self.offs[i] + self.sizes[i]
            while a < b:
                j = a // cs
                hi = min(b, (j + 1) * cs)
                pieces.append(parts[j][:, a - j * cs:hi - j * cs])
                a = hi
        pieces.append(jnp.zeros((parts.shape[1], LANES - self.n_small), parts.dtype))
        return jnp.concatenate(pieces, axis=1)

    def to_shards(self, wp):
        cs = self.in_dim // N_DEV
        pcol = dict(self.cols)
        o = self.small_col
        for i in self.small:
            pcol[i] = o
            o += self.sizes[i]
        shards = []
        for j in range(N_DEV):
            a, b = j * cs, (j + 1) * cs
            pieces = []
            for i in range(len(self.sizes)):
                lo, hi = max(a, self.offs[i]), min(b, self.offs[i] + self.sizes[i])
                if lo < hi:
                    pieces.append(wp[:, pcol[i] + lo - self.offs[i]:pcol[i] + hi - self.offs[i]])
            shards.append(jnp.concatenate(pieces, axis=1))
        return jnp.stack(shards)

    def restore_w(self, wp):
        pieces = {}
        for idx in self.big:
            pieces[idx] = wp[:, self.cols[idx]:self.cols[idx] + self.sizes[idx]]
        o = self.small_col
        for idx in self.small:
            pieces[idx] = wp[:, o:o + self.sizes[idx]]
            o += self.sizes[idx]
        return jnp.concatenate([pieces[i] for i in range(len(self.sizes))], axis=1)


def _rows_form(cols_t, nh, nc):
    return cols_t.T.reshape(nh, nc, 1, CHUNK)


def _layer_fwd(x, p, lay, tag, late=None):
    s, d = x.shape
    nc = s // CHUNK
    w = d
    dnh, smh = lay.dn_heads, lay.ssm_heads
    r = smh // SSM_GROUPS
    cb = {k: v // LANES for k, v in lay.cols.items()}
    sv = {}
    h1 = _rms_fwd(x, p["norm_mix"], name=f"rms_mix_{tag}")
    proj = _matmul(h1, p["w_in"], name=f"mm_in_{tag}")
    small = proj[:, lay.small_col:lay.small_col + LANES]
    a_rows = _rows_form(small[:, 0:dnh], dnh, nc)
    b_rows = _rows_form(small[:, dnh:2 * dnh], dnh, nc)
    dt_rows = small[:, 2 * dnh:2 * dnh + smh].T.reshape(SSM_GROUPS, r, nc, CHUNK).transpose(0, 2, 1, 3)
    zero_b = jnp.zeros((1, 3 * w), F32)
    dn_qkv = _conv_fwd(proj, cb[0], p["dn_conv_w"], zero_b, 2 * dnh, name=f"dn_conv_{tag}")
    dn_alog = p["dn_a_log"].reshape(dnh, 1, 1)
    dn_dtb = p["dn_dt_bias"].reshape(dnh, 1, 1)
    o_dn, dn_states, dn_inv = _dn_fwd(dn_qkv, a_rows, b_rows, dn_alog, dn_dtb, name=f"dn_chunk_{tag}")
    y_dn = _dn_post_fwd(o_dn, proj, cb[1], p["dn_norm_w"], name=f"dn_post_{tag}")
    o_sb, sb_r = _sb_fwd(proj, cb[4], w, name=f"sb_{tag}")
    xbc = _conv_fwd(proj, cb[6], p["ssm_conv_w"], p["ssm_conv_b"].reshape(1, -1), 0, name=f"ssm_conv_{tag}")
    ssm_alog = p["ssm_a_log"].reshape(SSM_GROUPS, r, 1)
    ssm_dtb = p["ssm_dt_bias"].reshape(SSM_GROUPS, r, 1)
    y_ssd, ssm_states = _ssd_fwd(xbc, dt_rows, ssm_alog, ssm_dtb, name=f"ssd_{tag}")
    dexp = jnp.repeat(p["ssm_d"], SSM_HEAD_DIM)
    y_ssm = _ssm_post_fwd(y_ssd, xbc, proj, cb[5], dexp, p["ssm_norm_w"], name=f"ssm_post_{tag}")
    if late is not None:
        p.update(late(y_ssm))
    branches = (y_dn, o_sb, y_ssm)
    proj3 = lax.empty((s, 3 * d), F32)
    for i, br in enumerate(branches):
        proj3 = _matmul(br, p["w_branch"][i], into=(proj3, i * d), name=f"mm_branch{i}_{tag}")
    merged = _merge_fwd(proj3, proj, cb[8], d, name=f"merge_{tag}")
    x1 = _matmul(merged, p["w_out"], name=f"mm_out_{tag}", epilogue=lambda acc, res: (acc + res,), extras=(x,))
    h2 = _rms_fwd(x1, p["norm_mlp"], name=f"rms_mlp_{tag}")
    u, act = _matmul(h2, p["w_up"], name=f"mm_up_{tag}", out_dtypes=(F32, MXU_DTYPE),
                     epilogue=lambda acc: (acc, jnp.square(jnp.maximum(acc, 0.0))))
    x2 = _matmul(act, p["w_down"], name=f"mm_down_{tag}", epilogue=lambda acc, res: (acc + res,), extras=(x1,))
    sv.update(x=x, h1=h1, proj=proj, a_rows=a_rows, b_rows=b_rows, dt_rows=dt_rows, dn_qkv=dn_qkv, dn_alog=dn_alog,
              dn_dtb=dn_dtb, o_dn=o_dn, dn_states=dn_states, dn_inv=dn_inv, y_dn=y_dn, o_sb=o_sb, sb_r=sb_r, xbc=xbc, ssm_alog=ssm_alog,
              ssm_dtb=ssm_dtb, y_ssd=y_ssd, ssm_states=ssm_states, dexp=dexp, y_ssm=y_ssm, proj3=proj3, merged=merged,
              x1=x1, h2=h2, u=u, act=act)
    return x2, sv


def _layer_bwd(dx2, p, sv, lay, tag, early=None, late=None):
    x = sv["x"]
    s, d = x.shape
    nc = s // CHUNK
    w = d
    dnh, smh = lay.dn_heads, lay.ssm_heads
    r = smh // SSM_GROUPS
    gn = SSM_GROUPS * SSM_STATE
    cb = {k: v // LANES for k, v in lay.cols.items()}
    proj = sv["proj"]
    g = {}
    dx2_b = dx2.astype(MXU_DTYPE)
    du = _matmul(dx2_b, p["w_down"], tb=True, name=f"mm_down_dx_{tag}", out_dtypes=(MXU_DTYPE,),
                 epilogue=lambda acc, uu: (acc * (2.0 * jnp.maximum(uu, 0.0)),), extras=(sv["u"],))
    g["w_down"] = _matmul(sv["act"], dx2_b, ta=True, name=f"mm_down_dw_{tag}", out_dtypes=(BF16,)).reshape(N_DEV, -1, d)
    g["w_up"] = _matmul(sv["h2"], du, ta=True, name=f"mm_up_dw_{tag}", out_dtypes=(BF16,), col_shards=N_DEV)
    dh2 = _matmul(du, p["w_up"], tb=True, name=f"mm_up_dx_{tag}")
    dx1, g["norm_mlp"] = _rms_bwd(sv["x1"], p["norm_mlp"], dh2, dx2, name=f"rms_mlp_bwd_{tag}")
    dx1_b = dx1.astype(MXU_DTYPE)
    dmerged = _matmul(dx1_b, p["w_out"], tb=True, name=f"mm_out_dx_{tag}")
    g["w_out"] = _matmul(sv["merged"], dx1_b, ta=True, name=f"mm_out_dw_{tag}", out_dtypes=(BF16,)).reshape(N_DEV, -1, d)
    dproj = lax.empty((s, lay.padded), MXU_DTYPE)
    dproj3, dproj = _merge_bwd(sv["proj3"], proj, cb[8], d, dmerged, dproj, name=f"merge_bwd_{tag}")
    branches = (sv["y_dn"], sv["o_sb"], sv["y_ssm"])
    dwb, dbr = [], []
    for i, br in enumerate(branches):
        dp_i = dproj3[:, i * d:(i + 1) * d]
        dwb.append(_matmul(br, dp_i, ta=True, name=f"mm_branch{i}_dw_{tag}", out_dtypes=(BF16,)).reshape(N_DEV, -1, d))
        dbr.append(_matmul(dp_i, p["w_branch"][i], tb=True, name=f"mm_branch{i}_dx_{tag}"))
    g["w_branch"] = jnp.stack(dwb, axis=1)
    dy_dn, do_sb, dy_ssm = dbr
    if early is not None:
        dy_ssm = early(g, dy_ssm)
    dy_ssd, dxs_skip, dproj, ddexp, g["ssm_norm_w"] = _ssm_post_bwd(
        sv["y_ssd"], sv["xbc"], proj, cb[5], sv["dexp"], p["ssm_norm_w"], dy_ssm, dproj, name=f"ssm_post_bwd_{tag}")
    g["ssm_d"] = ddexp.reshape(smh, SSM_HEAD_DIM).sum(axis=1)
    dxs, dbm, dcm, ddt_rows, dalog, ddtb = _ssd_bwd(
        sv["xbc"], sv["dt_rows"], sv["ssm_alog"], sv["ssm_dtb"], sv["ssm_states"], dy_ssd, name=f"ssd_bwd_{tag}")
    g["ssm_a_log"] = dalog.reshape(smh)
    g["ssm_dt_bias"] = ddtb.reshape(smh)
    dxbc_post = jnp.concatenate([dxs + dxs_skip, dbm, dcm], axis=1)
    dproj, g["ssm_conv_w"], dcb = _conv_bwd(proj, cb[6], p["ssm_conv_w"], p["ssm_conv_b"].reshape(1, -1), 0, dxbc_post,
                                            dproj, name=f"ssm_conv_bwd_{tag}")
    g["ssm_conv_b"] = dcb.reshape(-1)
    ddt = ddt_rows.transpose(0, 2, 1, 3).reshape(smh, s).T
    dqkv_sb = _sb_bwd(proj, cb[4], w, sv["sb_r"], do_sb, name=f"sb_bwd_{tag}")
    dproj = lax.dynamic_update_slice(dproj, jnp.concatenate([t.astype(MXU_DTYPE) for t in dqkv_sb], axis=1), (0, lay.cols[4]))
    do_dn, dproj, g["dn_norm_w"] = _dn_post_bwd(sv["o_dn"], proj, cb[1], p["dn_norm_w"], dy_dn, dproj,
                                                name=f"dn_post_bwd_{tag}")
    dqkv_dn, da_rows, db_rows, dal, ddtb_dn = _dn_bwd(
        sv["dn_qkv"], sv["a_rows"], sv["b_rows"], sv["dn_alog"], sv["dn_dtb"], sv["dn_states"], sv["dn_inv"], do_dn,
        name=f"dn_chunk_bwd_{tag}")
    g["dn_a_log"] = dal.reshape(dnh)
    g["dn_dt_bias"] = ddtb_dn.reshape(dnh)
    zero_b = jnp.zeros((1, 3 * w), F32)
    dproj, g["dn_conv_w"], _ = _conv_bwd(proj, cb[0], p["dn_conv_w"], zero_b, 2 * dnh, dqkv_dn, dproj,
                                         name=f"dn_conv_bwd_{tag}")
    da = da_rows.reshape(dnh, s).T
    db = db_rows.reshape(dnh, s).T
    dsmall = jnp.concatenate([da, db, ddt, jnp.zeros((s, LANES - lay.n_small), F32)], axis=1).astype(MXU_DTYPE)
    dproj = lax.dynamic_update_slice(dproj, dsmall, (0, lay.small_col))
    g["w_in"] = lay.to_shards(_matmul(sv["h1"], dproj, ta=True, name=f"mm_in_dw_{tag}", out_dtypes=(BF16,)))
    if late is not None:
        dproj = late(g, dproj)
    dh1 = _matmul(dproj, p["w_in"], tb=True, name=f"mm_in_dx_{tag}")
    dx0, g["norm_mix"] = _rms_bwd(x, p["norm_mix"], dh1, dx1, name=f"rms_mix_bwd_{tag}")
    return dx0, g


BIG = ("w_in", "w_branch", "w_out", "w_up", "w_down")
CONV = ("dn_conv_w", "ssm_conv_w")
SMALL = ("norm_mix", "dn_conv_w", "dn_a_log", "dn_dt_bias", "dn_norm_w", "ssm_conv_w", "ssm_conv_b", "ssm_a_log",
         "ssm_dt_bias", "ssm_d", "ssm_norm_w", "norm_mlp", "norm_final")
WEIGHTS = ("norm_mix", "w_in", "dn_conv_w", "dn_a_log", "dn_dt_bias", "dn_norm_w", "ssm_conv_w", "ssm_conv_b", "ssm_a_log",
           "ssm_dt_bias", "ssm_d", "ssm_norm_w", "w_branch", "w_out", "norm_mlp", "w_up", "w_down", "norm_final")
SHARD_AXIS = {"w_in": 2, "dn_conv_w": 2, "ssm_conv_w": 2, "w_branch": 2, "w_out": 1, "w_up": 2, "w_down": 1}


def _to_shards(full, axis):
    shp = full.shape
    n = shp[axis] // N_DEV
    t = full.reshape(shp[:axis] + (N_DEV, n) + shp[axis + 1:])
    return jnp.moveaxis(t, axis, 0)


def _from_shards(parts, axis):
    t = jnp.moveaxis(parts, 0, axis)
    shp = t.shape
    return t.reshape(shp[:axis] + (shp[axis] * shp[axis + 1],) + shp[axis + 2:])


def _unshard(parts, axis, *, name):
    shard = parts.shape[1:]
    nd = len(shard)
    if axis == 0:
        return parts.reshape((N_DEV * shard[0],) + shard[1:])

    def copy_block(i_ref, o_ref):
        o_ref[...] = i_ref[...]

    if axis == nd - 1:
        rows, n = math.prod(shard[:-1]), shard[-1]
        out = pl.pallas_call(
            copy_block, grid=(N_DEV,),
            in_specs=[pl.BlockSpec((None, rows, n), lambda j: (j, 0, 0))],
            out_specs=pl.BlockSpec((rows, n), lambda j: (0, j)),
            out_shape=jax.ShapeDtypeStruct((rows, N_DEV * n), parts.dtype),
            compiler_params=_cparams(1), name=name,
        )(parts.reshape(N_DEV, rows, n))
        return out.reshape(shard[:-1] + (N_DEV * n,))
    assert axis == nd - 2, (parts.shape, axis)
    a, n, c = math.prod(shard[:-2]), shard[-2], shard[-1]
    out = pl.pallas_call(
        copy_block, grid=(N_DEV, a),
        in_specs=[pl.BlockSpec((None, None, n, c), lambda j, i: (j, i, 0, 0))],
        out_specs=pl.BlockSpec((None, n, c), lambda j, i: (i, j, 0)),
        out_shape=jax.ShapeDtypeStruct((a, N_DEV * n, c), parts.dtype),
        compiler_params=_cparams(2), name=name,
    )(parts.reshape(N_DEV, a, n, c))
    return out.reshape(shard[:-2] + (N_DEV * n, c))


def _step(w, m, v, x, target):
    s, d = x.shape
    lay = _Layout(d)
    me = 4 * lax.axis_index("x") + 2 * lax.axis_index("y") + lax.axis_index("c")

    def shard(n, l):
        return w[n][l].astype(BF16) if n in BIG else w[n][l]

    def empty_land(a):
        return lax.empty((N_DEV,) + a.shape, a.dtype)

    def with_own(land, own):
        return lax.dynamic_update_index_in_dim(land, own, me, 0)

    def assemble(n, parts, l):
        return lay.from_shards(parts) if n == "w_in" else _unshard(parts, SHARD_AXIS[n] - 1, name=f"unshard_{n}_l{l}")

    small_names = tuple(n for n in WEIGHTS if n not in BIG + CONV + ("norm_final",))

    first, rest = ("w_in",) + CONV, BIG[1:]
    got = _all_gather([shard(n, 0) for n in first], name="gather_l0_first")
    whole, sliced = (True, None), (False, None)
    names_a, names_b = rest, BIG + CONV
    srcs_a, srcs_b = [shard(n, 0) for n in names_a], [shard(n, 1) for n in names_b]
    sem_sa, sem_ra, srcs_a, lands_a, w_in0 = _split_start(
        srcs_a, [empty_land(a) for a in srcs_a], [whole] * len(srcs_a), got[0], name="gather_l0_rest_start")
    sem_sb, sem_rb, srcs_b, lands_b, w_in0 = _split_start(
        srcs_b, [empty_land(a) for a in srcs_b], [whole] * len(srcs_b), w_in0, name="gather_l1_start")
    p0 = {n: w[n][0] for n in small_names}
    p0.update({n: assemble(n, g, 0) for n, g in zip(first, [w_in0] + list(got[1:]))})

    def late_l0(after):
        lands = _split_wait(sem_sa, sem_ra, srcs_a, lands_a, [whole] * len(srcs_a), after, name="gather_l0_rest_wait")
        return {n: assemble(n, with_own(ld, s_), 0) for n, ld, s_ in zip(names_a, lands, srcs_a)}

    h, sv0 = _layer_fwd(x, p0, lay, "l0", late=late_l0)
    lands = _split_wait(sem_sb, sem_rb, srcs_b, lands_b, [whole] * len(srcs_b), h, name="gather_l1_wait")
    p1 = {n: w[n][1] for n in small_names}
    p1.update({n: assemble(n, with_own(ld, s_), 1) for n, ld, s_ in zip(names_b, lands, srcs_b)})
    h, sv1 = _layer_fwd(h, p1, lay, "l1")
    loss, dh, g_norm_final = _final_loss(h, w["norm_final"], target, name="final_loss")
    grads = [None] * DEPTH
    dh, grads[1] = _layer_bwd(dh, p1, sv1, lay, "l1")

    def exchange_start(names, g, carry, tag):
        srcs = [g[n] for n in names]
        return _split_start(srcs, [lax.empty(a.shape, a.dtype) for a in srcs], [sliced] * len(srcs), carry,
                            name=f"grad_{tag}_start")

    def exchange_wait(names, started, after, tag):
        sem_s, sem_r, srcs, lands_, _ = started
        lands_ = _split_wait(sem_s, sem_r, srcs, lands_, [sliced] * len(srcs), after, name=f"grad_{tag}_wait")
        return {n: with_own(ld, lax.dynamic_index_in_dim(s_, me, 0, keepdims=False)) for n, ld, s_ in zip(names, lands_, srcs)}

    x1_started = exchange_start(BIG, grads[1], dh, "l1")
    pending = {}

    def early_l0(g, carry):
        pending["rest"] = exchange_start(rest, g, carry, "l0_rest")
        return pending["rest"][4]

    def late_bwd_l0(g, carry):
        pending["w_in"] = exchange_start(("w_in",), g, carry, "l0_w_in")
        return pending["w_in"][4]

    grad_x, grads[0] = _layer_bwd(x1_started[4], p0, sv0, lay, "l0", early=early_l0, late=late_bwd_l0)

    out = {"grad": {}, "delta": {}, "new_m": {}, "new_v": {}}
    parts1 = exchange_wait(BIG, x1_started, grad_x, "l1")
    res1 = {n: _sum_adamw(parts1[n], w[n], m[n], v[n], 1, None, name=f"sum_adamw_{n}_l1") for n in BIG}
    parts0 = exchange_wait(rest, pending["rest"], res1["w_in"][0], "l0_rest")
    res0 = {n: _sum_adamw(parts0[n], w[n], m[n], v[n], 0, res1[n], name=f"sum_adamw_{n}_l0") for n in rest}
    parts0 = exchange_wait(("w_in",), pending["w_in"], res0["w_down"][0], "l0_w_in")
    res0["w_in"] = _sum_adamw(parts0["w_in"], w["w_in"], m["w_in"], v["w_in"], 0, res1["w_in"], name="sum_adamw_w_in_l0")
    for n in BIG:
        for key, a in zip(("grad", "delta", "new_m", "new_v"), res0[n]):
            out[key][n] = a

    gfull = {n: jnp.stack([grads[l][n] for l in range(DEPTH)]) for n in SMALL if n != "norm_final"}
    gfull["norm_final"] = g_norm_final
    small_send = _pack([gfull[n] for n in SMALL] + [loss.reshape(1)], F32)
    small_recv = _all_gather([small_send], name="gather_small_grads", after=res0["w_in"][0])[0]
    small_sum = _sum_parts(small_recv, name="sum_small")
    small_full = _unpack(small_sum, [gfull[n].shape for n in SMALL] + [(1,)])
    loss_total = small_full[-1][0]
    gsmall = {}
    for n, a in zip(SMALL, small_full[:-1]):
        if n in SHARD_AXIS:
            a = lax.dynamic_index_in_dim(_to_shards(a, SHARD_AXIS[n]), me, axis=0, keepdims=False)
        gsmall[n] = a
    small_shapes = [w[n].shape for n in SMALL]
    ws, gs, ms, vs = (_pack([t[n] for n in SMALL], F32) for t in (w, gsmall, m, v))
    ds, m1s, v1s = _adamw(ws, gs, ms, vs, name="adamw_small")
    for n in SMALL:
        out["grad"][n] = gsmall[n]
    for key, packed in (("delta", ds), ("new_m", m1s), ("new_v", v1s)):
        for n, a in zip(SMALL, _unpack(packed, small_shapes)):
            out[key][n] = a
    return loss_total, grad_x, out


def kernel(x, norm_mix, w_in, dn_conv_w, dn_a_log, dn_dt_bias, dn_norm_w, ssm_conv_w, ssm_conv_b, ssm_a_log, ssm_dt_bias, ssm_d, ssm_norm_w, w_branch, w_out, norm_mlp, w_up, w_down, norm_final, loss_target, m_norm_mix, m_w_in, m_dn_conv_w, m_dn_a_log, m_dn_dt_bias, m_dn_norm_w, m_ssm_conv_w, m_ssm_conv_b, m_ssm_a_log, m_ssm_dt_bias, m_ssm_d, m_ssm_norm_w, m_w_branch, m_w_out, m_norm_mlp, m_w_up, m_w_down, m_norm_final, v_norm_mix, v_w_in, v_dn_conv_w, v_dn_a_log, v_dn_dt_bias, v_dn_norm_w, v_ssm_conv_w, v_ssm_conv_b, v_ssm_a_log, v_ssm_dt_bias, v_ssm_d, v_ssm_norm_w, v_w_branch, v_w_out, v_norm_mlp, v_w_up, v_w_down, v_norm_final):
    w = dict(norm_mix=norm_mix, w_in=w_in, dn_conv_w=dn_conv_w, dn_a_log=dn_a_log, dn_dt_bias=dn_dt_bias, dn_norm_w=dn_norm_w,
             ssm_conv_w=ssm_conv_w, ssm_conv_b=ssm_conv_b, ssm_a_log=ssm_a_log, ssm_dt_bias=ssm_dt_bias, ssm_d=ssm_d,
             ssm_norm_w=ssm_norm_w, w_branch=w_branch, w_out=w_out, norm_mlp=norm_mlp, w_up=w_up, w_down=w_down,
             norm_final=norm_final)
    m = dict(norm_mix=m_norm_mix, w_in=m_w_in, dn_conv_w=m_dn_conv_w, dn_a_log=m_dn_a_log, dn_dt_bias=m_dn_dt_bias,
             dn_norm_w=m_dn_norm_w, ssm_conv_w=m_ssm_conv_w, ssm_conv_b=m_ssm_conv_b, ssm_a_log=m_ssm_a_log,
             ssm_dt_bias=m_ssm_dt_bias, ssm_d=m_ssm_d, ssm_norm_w=m_ssm_norm_w, w_branch=m_w_branch, w_out=m_w_out,
             norm_mlp=m_norm_mlp, w_up=m_w_up, w_down=m_w_down, norm_final=m_norm_final)
    v = dict(norm_mix=v_norm_mix, w_in=v_w_in, dn_conv_w=v_dn_conv_w, dn_a_log=v_dn_a_log, dn_dt_bias=v_dn_dt_bias,
             dn_norm_w=v_dn_norm_w, ssm_conv_w=v_ssm_conv_w, ssm_conv_b=v_ssm_conv_b, ssm_a_log=v_ssm_a_log,
             ssm_dt_bias=v_ssm_dt_bias, ssm_d=v_ssm_d, ssm_norm_w=v_ssm_norm_w, w_branch=v_w_branch, w_out=v_w_out,
             norm_mlp=v_norm_mlp, w_up=v_w_up, w_down=v_w_down, norm_final=v_norm_final)
    loss, grad_x, out = _step(w, m, v, x[0], loss_target[0])
    return (loss, grad_x[None], *[out["grad"][n] for n in WEIGHTS], *[out["delta"][n] for n in WEIGHTS],
            *[out["new_m"][n] for n in WEIGHTS], *[out["new_v"][n] for n in WEIGHTS])
```

```python
import math

import jax
import jax.numpy as jnp
from jax import lax
from jax.experimental import pallas as pl
from jax.experimental.pallas import tpu as pltpu

F32 = jnp.float32
BF16 = jnp.bfloat16
MXU_DTYPE = BF16
HIGHEST = lax.Precision.HIGHEST

N_DEV = 8
DEPTH = 2
EPS = 1e-6
CONV_K = 4
DN_HEAD_DIM = 128
SB_HEAD_DIM = 64
SSM_HEAD_DIM = 64
SSM_STATE = 128
SSM_GROUPS = 4
CHUNK = 64
SB_BLOCK = 128
LANES = 128
ADAM_LR, ADAM_B1, ADAM_B2, ADAM_EPS, ADAM_WD, ADAM_STEP = 0.001, 0.9, 0.999, 1e-08, 0.01, 10
NEG_BIG = -1e30
DN_HEADS_PER_STEP = 8
SSD_GROUPS_PER_STEP = 1
SB_UNROLL = 4
SB_SPLIT = 2
CHUNK_PREC = lax.Precision.HIGH

ARB = "arbitrary"


def _cparams(n_axes):
    return pltpu.CompilerParams(dimension_semantics=(ARB,) * n_axes)


def _softplus(x):
    return jnp.maximum(x, 0.0) + jnp.log1p(jnp.exp(-jnp.abs(x)))


def _sigmoid(x):
    return jax.nn.sigmoid(x)


def _silu(x):
    return x * _sigmoid(x)


def _silu_and_grad(x):
    s = _sigmoid(x)
    return x * s, s * (1.0 + x * (1.0 - s))


def _dot(a, b, dims, prec=None):
    return lax.dot_general(a, b, (dims, ((), ())), precision=prec, preferred_element_type=F32)


NN = ((1,), (0,))
NT = ((1,), (1,))
TN = ((0,), (0,))


def _hdot(a, b, dims=NN):
    return _dot(a, b, dims, CHUNK_PREC)


def _split_dot(a, m_bf16, nsplit=3):
    out = None
    rem = a
    for _ in range(nsplit):
        piece = rem.astype(BF16)
        rem = rem - piece.astype(F32)
        term = _dot(piece, m_bf16, NN)
        out = term if out is None else out + term
    return out


def _pick(n, pref):
    for t in pref:
        if n % t == 0:
            return t
    return n


def _matmul(a, b, *, ta=False, tb=False, name, epilogue=None, extras=(), out_dtypes=(F32,), col_shards=1, into=None,
            tm=None, tn=None, tk=None):
    m, k = (a.shape[1], a.shape[0]) if ta else a.shape
    k2, n = (b.shape[1], b.shape[0]) if tb else b.shape
    assert k == k2, (a.shape, b.shape, ta, tb)
    ncs = n // col_shards
    tm = tm or _pick(m, (1920, 1024, 512, 256, 128))
    tn = tn or _pick(ncs, (1920, 1024, 640, 512, 384, 256, 128))
    tk = tk or _pick(k, (1920, 1024, 640, 512, 256, 128))
    nk = k // tk
    a_spec = pl.BlockSpec((tk, tm), lambda i, j, kk: (kk, i)) if ta else pl.BlockSpec((tm, tk), lambda i, j, kk: (i, kk))
    b_spec = pl.BlockSpec((tn, tk), lambda i, j, kk: (j, kk)) if tb else pl.BlockSpec((tk, tn), lambda i, j, kk: (kk, j))
    e_spec = pl.BlockSpec((tm, tn), lambda i, j, kk: (i, j))
    if into is not None:
        buf, col_off = into
        off = col_off // tn
        assert col_shards == 1 and len(out_dtypes) == 1 and col_off % tn == 0 and out_dtypes[0] == buf.dtype
        o_spec, o_shape = pl.BlockSpec((tm, tn), lambda i, j, kk: (i, off + j)), buf.shape
    elif col_shards == 1:
        o_spec, o_shape = e_spec, (m, n)
    else:
        per = ncs // tn
        o_spec, o_shape = pl.BlockSpec((None, tm, tn), lambda i, j, kk: (j // per, i, j % per)), (col_shards, m, ncs)
    dims = (((0,) if ta else (1,)), ((1,) if tb else (0,)))
    n_extra = len(extras)
    n_out = len(out_dtypes)
    n_into = 0 if into is None else 1

    def body(*refs):
        a_ref, b_ref = refs[0], refs[1]
        extra_refs = refs[2:2 + n_extra]
        out_refs = refs[2 + n_extra + n_into:2 + n_extra + n_into + n_out]
        acc_ref = refs[-1]
        kk = pl.program_id(2)

        @pl.when(kk == 0)
        def _():
            acc_ref[...] = jnp.zeros_like(acc_ref)

        acc_ref[...] += _dot(a_ref[...].astype(MXU_DTYPE), b_ref[...].astype(MXU_DTYPE), dims)

        @pl.when(kk == nk - 1)
        def _():
            acc = acc_ref[...]
            outs = (acc,) if epilogue is None else epilogue(acc, *[r[...] for r in extra_refs])
            for o_ref, o in zip(out_refs, outs):
                o_ref[...] = o.astype(o_ref.dtype)

    outs = pl.pallas_call(
        body,
        grid=(m // tm, n // tn, nk),
        in_specs=[a_spec, b_spec] + [e_spec] * n_extra + [ANY] * n_into,
        out_specs=[o_spec] * n_out,
        out_shape=[jax.ShapeDtypeStruct(o_shape, dt) for dt in out_dtypes],
        input_output_aliases={2 + n_extra: 0} if n_into else {},
        scratch_shapes=[pltpu.VMEM((tm, tn), F32)],
        compiler_params=pltpu.CompilerParams(dimension_semantics=("parallel", "parallel", ARB)),
        name=name,
    )(a, b, *extras, *([] if into is None else [into[0]]))
    return outs[0] if n_out == 1 else tuple(outs)


def _rms_fwd(x, w, *, name, tm=512):
    s, d = x.shape
    out_dtype = MXU_DTYPE

    def body(x_ref, w_ref, o_ref):
        xv = x_ref[...]
        r = lax.rsqrt(jnp.mean(xv * xv, axis=-1, keepdims=True) + EPS)
        o_ref[...] = (xv * r * w_ref[...]).astype(o_ref.dtype)

    return pl.pallas_call(
        body, grid=(s // tm,),
        in_specs=[pl.BlockSpec((tm, d), lambda i: (i, 0)), pl.BlockSpec((1, d), lambda i: (0, 0))],
        out_specs=pl.BlockSpec((tm, d), lambda i: (i, 0)),
        out_shape=jax.ShapeDtypeStruct((s, d), out_dtype),
        compiler_params=_cparams(1), name=name,
    )(x, w.reshape(1, d))


def _rms_bwd(x, w, dh, dres, *, name, tm=512):
    s, d = x.shape

    def body(x_ref, w_ref, dh_ref, dres_ref, dx_ref, dw_ref):
        xv = x_ref[...]
        r = lax.rsqrt(jnp.mean(xv * xv, axis=-1, keepdims=True) + EPS)
        xh = xv * r
        dhv = dh_ref[...].astype(F32)
        dxn = dhv * w_ref[...]
        dx = r * (dxn - xh * jnp.mean(dxn * xh, axis=-1, keepdims=True))
        dx_ref[...] = dres_ref[...] + dx

        @pl.when(pl.program_id(0) == 0)
        def _():
            dw_ref[...] = jnp.zeros_like(dw_ref)

        dw_ref[...] += jnp.sum(dhv * xh, axis=0, keepdims=True)

    dx, dw = pl.pallas_call(
        body, grid=(s // tm,),
        in_specs=[pl.BlockSpec((tm, d), lambda i: (i, 0)), pl.BlockSpec((1, d), lambda i: (0, 0)),
                  pl.BlockSpec((tm, d), lambda i: (i, 0)), pl.BlockSpec((tm, d), lambda i: (i, 0))],
        out_specs=[pl.BlockSpec((tm, d), lambda i: (i, 0)), pl.BlockSpec((1, d), lambda i: (0, 0))],
        out_shape=[jax.ShapeDtypeStruct((s, d), F32), jax.ShapeDtypeStruct((1, d), F32)],
        compiler_params=_cparams(1), name=name,
    )(x, w.reshape(1, d), dh, dres)
    return dx, dw.reshape(d)


def _final_loss(x, w, target, *, name, tm=512):
    s, d = x.shape

    def body(x_ref, w_ref, t_ref, loss_ref, dx_ref, dw_ref):
        xv = x_ref[...]
        r = lax.rsqrt(jnp.mean(xv * xv, axis=-1, keepdims=True) + EPS)
        xh = xv * r
        err = xh * w_ref[...] - t_ref[...]
        dy = err * (1.0 / d)
        dxn = dy * w_ref[...]
        dx_ref[...] = r * (dxn - xh * jnp.mean(dxn * xh, axis=-1, keepdims=True))

        @pl.when(pl.program_id(0) == 0)
        def _():
            dw_ref[...] = jnp.zeros_like(dw_ref)
            loss_ref[...] = jnp.zeros_like(loss_ref)

        dw_ref[...] += jnp.sum(dy * xh, axis=0, keepdims=True)
        row = jnp.sum(err * err, axis=1, keepdims=True) * (0.5 / d)
        loss_ref[...] += jnp.sum(row, axis=0, keepdims=True)

    loss, dx, dw = pl.pallas_call(
        body, grid=(s // tm,),
        in_specs=[pl.BlockSpec((tm, d), lambda i: (i, 0)), pl.BlockSpec((1, d), lambda i: (0, 0)),
                  pl.BlockSpec((tm, d), lambda i: (i, 0))],
        out_specs=[pl.BlockSpec((1, 1), lambda i: (0, 0)), pl.BlockSpec((tm, d), lambda i: (i, 0)),
                   pl.BlockSpec((1, d), lambda i: (0, 0))],
        out_shape=[jax.ShapeDtypeStruct((1, 1), F32), jax.ShapeDtypeStruct((s, d), F32), jax.ShapeDtypeStruct((1, d), F32)],
        compiler_params=_cparams(1), name=name,
    )(x, w.reshape(1, d), target)
    return loss[0, 0], dx, dw.reshape(d)


def _shift_down(x, sh, t_idx):
    return jnp.where(t_idx >= sh, pltpu.roll(x, sh, 0), 0.0)


def _shift_up(x, sh, t_idx, s):
    return jnp.where(t_idx < s - sh, pltpu.roll(x, s - sh, 0), 0.0)


def _conv_pre(x, w_rows, b, t_idx):
    c = w_rows[CONV_K - 1] * x + b
    for sh in range(1, CONV_K):
        c = c + w_rows[CONV_K - 1 - sh] * _shift_down(x, sh, t_idx)
    return c


def _conv_fwd(src, col0, w, b, n_l2, *, name):
    s = src.shape[0]
    c_tot = w.shape[1]
    nblk = c_tot // LANES

    def body(x_ref, w_ref, b_ref, o_ref):
        j = pl.program_id(0)
        t_idx = lax.broadcasted_iota(jnp.int32, (s, LANES), 0)
        w_rows = [w_ref[kk:kk + 1, :] for kk in range(CONV_K)]
        y = _silu(_conv_pre(x_ref[...], w_rows, b_ref[...], t_idx))
        if n_l2 > 0:
            yn = y * lax.rsqrt(jnp.sum(y * y, axis=1, keepdims=True) + EPS)
            y = jnp.where(j < n_l2, yn, y)
        o_ref[...] = y

    return pl.pallas_call(
        body, grid=(nblk,),
        in_specs=[pl.BlockSpec((s, LANES), lambda j: (0, col0 + j)), pl.BlockSpec((CONV_K, LANES), lambda j: (0, j)),
                  pl.BlockSpec((1, LANES), lambda j: (0, j))],
        out_specs=pl.BlockSpec((s, LANES), lambda j: (0, j)),
        out_shape=jax.ShapeDtypeStruct((s, c_tot), F32),
        compiler_params=_cparams(1), name=name,
    )(src, w, b)


def _conv_bwd(src, col0, w, b, n_l2, dout, into, *, name):
    s = src.shape[0]
    c_tot = w.shape[1]
    nblk = c_tot // LANES

    def body(x_ref, w_ref, b_ref, do_ref, into_ref, dx_ref, dw_ref, db_ref):
        j = pl.program_id(0)
        t_idx = lax.broadcasted_iota(jnp.int32, (s, LANES), 0)
        xv = x_ref[...]
        w_rows = [w_ref[kk:kk + 1, :] for kk in range(CONV_K)]
        c = _conv_pre(xv, w_rows, b_ref[...], t_idx)
        dy = do_ref[...]
        y, y_grad = _silu_and_grad(c)
        if n_l2 > 0:
            r = lax.rsqrt(jnp.sum(y * y, axis=1, keepdims=True) + EPS)
            dyn = r * dy - y * (r * r * r) * jnp.sum(dy * y, axis=1, keepdims=True)
            dy = jnp.where(j < n_l2, dyn, dy)
        dc = dy * y_grad
        dx = w_rows[CONV_K - 1] * dc
        rows = [None] * CONV_K
        rows[CONV_K - 1] = jnp.sum(dc * xv, axis=0, keepdims=True)
        for sh in range(1, CONV_K):
            dx = dx + w_rows[CONV_K - 1 - sh] * _shift_up(dc, sh, t_idx, s)
            rows[CONV_K - 1 - sh] = jnp.sum(dc * _shift_down(xv, sh, t_idx), axis=0, keepdims=True)
        dx_ref[...] = dx.astype(dx_ref.dtype)
        for kk in range(CONV_K):
            dw_ref[kk:kk + 1, :] = rows[kk]
        db_ref[...] = jnp.sum(dc, axis=0, keepdims=True)

    return pl.pallas_call(
        body, grid=(nblk,),
        in_specs=[pl.BlockSpec((s, LANES), lambda j: (0, col0 + j)), pl.BlockSpec((CONV_K, LANES), lambda j: (0, j)),
                  pl.BlockSpec((1, LANES), lambda j: (0, j)), pl.BlockSpec((s, LANES), lambda j: (0, j)), ANY],
        out_specs=[pl.BlockSpec((s, LANES), lambda j: (0, col0 + j)), pl.BlockSpec((CONV_K, LANES), lambda j: (0, j)),
                   pl.BlockSpec((1, LANES), lambda j: (0, j))],
        out_shape=[jax.ShapeDtypeStruct(into.shape, into.dtype), jax.ShapeDtypeStruct((CONV_K, c_tot), F32),
                   jax.ShapeDtypeStruct((1, c_tot), F32)],
        input_output_aliases={4: 0},
        compiler_params=_cparams(1), name=name,
    )(src, w, b, dout, into)


def _chunk_masks(c):
    ii = lax.broadcasted_iota(jnp.int32, (c, c), 0)
    jj = lax.broadcasted_iota(jnp.int32, (c, c), 1)
    return ii, jj


def _row_to_col(row, eye):
    return jnp.sum(jnp.where(eye, row, 0.0), axis=1, keepdims=True)


def _each(f, *lists):
    return [f(*xs) for xs in zip(*lists)]


@jax.custom_vjp
def _nilpotent_inverse(nmats):
    c = nmats[0].shape[0]
    ii, jj = _chunk_masks(c)
    xinv = _each(lambda n: jnp.where(ii == jj, 1.0, 0.0) + n, nmats)
    pw = nmats
    for _ in range(int(math.log2(c)) - 1):
        pw = _each(lambda p: _dot(p, p, NN, HIGHEST), pw)
        xinv = _each(lambda x, p: x + _dot(x, p, NN, HIGHEST), xinv, pw)
    return xinv


def _nilpotent_inverse_fwd(nmats):
    xinv = _nilpotent_inverse(nmats)
    return xinv, xinv


def _nilpotent_inverse_bwd(xinv, cts):
    left = _each(lambda x, ct: _dot(x, ct, TN, HIGHEST), xinv, cts)
    return (_each(lambda l_, x: _dot(l_, x, NT, HIGHEST), left, xinv),)


_nilpotent_inverse.defvjp(_nilpotent_inverse_fwd, _nilpotent_inverse_bwd)


@jax.custom_vjp
def _saved_inverse(nmats, saved):
    return saved


def _saved_inverse_fwd(nmats, saved):
    return saved, saved


def _saved_inverse_bwd(xinv, cts):
    return _nilpotent_inverse_bwd(xinv, cts) + (_each(jnp.zeros_like, xinv),)


_saved_inverse.defvjp(_saved_inverse_fwd, _saved_inverse_bwd)


def _dn_chunk(q, k, v, a_row, b_row, alog, dtb, s0, saved_inverse=None):
    c = q[0].shape[0]
    ii, jj = _chunk_masks(c)
    causal, strict, eye = ii >= jj, ii > jj, ii == jj
    g_row = _each(lambda al, a, dt: -jnp.exp(al) * _softplus(a + dt), alog, a_row, dtb)
    beta_col = _each(lambda b: _row_to_col(_sigmoid(b), eye), b_row)
    g_col = _each(lambda g: _row_to_col(g, eye), g_row)
    gc_col = _each(lambda g: jnp.sum(jnp.where(causal, g, 0.0), axis=1, keepdims=True), g_row)
    gc_row = _each(lambda g: jnp.sum(jnp.where(jj >= ii, g, 0.0), axis=0, keepdims=True), g_col)
    decay = _each(lambda gc, gr: jnp.exp(jnp.where(causal, gc - gr, NEG_BIG)), gc_col, gc_row)
    kb = _each(jnp.multiply, k, beta_col)
    vb = _each(jnp.multiply, v, beta_col)
    nmat = _each(lambda kb_, k_, dc: -jnp.where(strict, _dot(kb_, k_, NT, HIGHEST) * dc, 0.0), kb, k, decay)
    xinv = _nilpotent_inverse(nmat) if saved_inverse is None else _saved_inverse(nmat, saved_inverse)
    egc = _each(jnp.exp, gc_col)
    dv = v[0].shape[1]
    uw = _each(lambda x, vb_, kb_, e: _dot(x, jnp.concatenate([vb_, kb_ * e], axis=1), NN, HIGHEST), xinv, vb, kb, egc)
    u = _each(lambda t: t[:, :dv], uw)
    w = _each(lambda t: t[:, dv:], uw)
    qs = _each(lambda q_: q_ * (q_.shape[1] ** -0.5), q)
    attn = _each(lambda q_, k_, dc: _hdot(q_, k_, NT) * dc, qs, k, decay)
    gl = _each(lambda g: jnp.sum(g, axis=1, keepdims=True), g_row)
    kd = _each(lambda k_, gl_, gc: k_ * jnp.exp(gl_ - gc), k, gl, gc_col)
    v_new = _each(lambda u_, w_, s: u_ - _hdot(w_, s), u, w, s0)
    o = _each(lambda q_, e, s, at, vn: _hdot(q_ * e, s) + _hdot(at, vn), qs, egc, s0, attn, v_new)
    s1 = _each(lambda s, gl_, kd_, vn: s * jnp.exp(gl_) + _hdot(kd_, vn, TN), s0, gl, kd, v_new)
    return (o, s1), xinv


def _dn_specs(nh, nc, hb, rev):
    n_of = (lambda n: nc - 1 - n) if rev else (lambda n: n)
    ng = nh // hb
    qkv = [pl.BlockSpec((CHUNK, hb * DN_HEAD_DIM), (lambda h, n, o=o: (n_of(n), o * ng + h))) for o in range(3)]
    row = pl.BlockSpec((hb, None, 1, CHUNK), lambda h, n: (h, n_of(n), 0, 0))
    scal = pl.BlockSpec((hb, 1, 1), lambda h, n: (h, 0, 0))
    o_spec = pl.BlockSpec((CHUNK, hb * DN_HEAD_DIM), lambda h, n: (n_of(n), h))
    st = pl.BlockSpec((hb, None, DN_HEAD_DIM, DN_HEAD_DIM), lambda h, n: (h, n_of(n), 0, 0))
    inv = pl.BlockSpec((hb, None, CHUNK, CHUNK), lambda h, n: (h, n_of(n), 0, 0))
    return qkv, row, scal, o_spec, st, inv


def _dn_fwd(qkv, a_rows, b_rows, alog, dtb, *, name):
    s = qkv.shape[0]
    nh, nc = a_rows.shape[0], a_rows.shape[1]
    hb = min(DN_HEADS_PER_STEP, nh)
    qkv_specs, row, scal, o_spec, st, inv = _dn_specs(nh, nc, hb, False)
    hd = DN_HEAD_DIM

    def body(q_ref, k_ref, v_ref, a_ref, b_ref, al_ref, dt_ref, o_ref, st_ref, inv_ref, state):
        @pl.when(pl.program_id(1) == 0)
        def _():
            state[...] = jnp.zeros_like(state)

        cols = [slice(h * hd, (h + 1) * hd) for h in range(hb)]
        s0 = [state[h] for h in range(hb)]
        for h in range(hb):
            st_ref[h] = s0[h]
        (o, s1), xinv = _dn_chunk(
            [q_ref[:, cl] for cl in cols], [k_ref[:, cl] for cl in cols], [v_ref[:, cl] for cl in cols],
            [a_ref[h] for h in range(hb)], [b_ref[h] for h in range(hb)],
            [al_ref[h] for h in range(hb)], [dt_ref[h] for h in range(hb)], s0)
        for h in range(hb):
            o_ref[:, cols[h]] = o[h]
            inv_ref[h] = xinv[h]
            state[h] = s1[h]

    return pl.pallas_call(
        body, grid=(nh // hb, nc),
        in_specs=qkv_specs + [row, row, scal, scal],
        out_specs=[o_spec, st, inv],
        out_shape=[jax.ShapeDtypeStruct((s, nh * hd), F32), jax.ShapeDtypeStruct((nh, nc, hd, hd), F32),
                   jax.ShapeDtypeStruct((nh, nc, CHUNK, CHUNK), F32)],
        scratch_shapes=[pltpu.VMEM((hb, hd, hd), F32)],
        compiler_params=_cparams(2), name=name,
    )(qkv, qkv, qkv, a_rows, b_rows, alog, dtb)


def _dn_bwd(qkv, a_rows, b_rows, alog, dtb, states, inverses, do, *, name):
    s = qkv.shape[0]
    nh, nc = a_rows.shape[0], a_rows.shape[1]
    hb = min(DN_HEADS_PER_STEP, nh)
    qkv_specs, row, scal, o_spec, st, inv = _dn_specs(nh, nc, hb, True)
    hd = DN_HEAD_DIM

    assert hb == nh, "dq | dk | dv are written as one [S, 3W] array: all heads in one grid step"
    w = nh * hd

    def body(q_ref, k_ref, v_ref, a_ref, b_ref, al_ref, dt_ref, st_ref, inv_ref, do_ref,
             dqkv_ref, da_ref, db_ref, dal_ref, ddt_ref, dstate):
        @pl.when(pl.program_id(1) == 0)
        def _():
            dstate[...] = jnp.zeros_like(dstate)
            dal_ref[...] = jnp.zeros_like(dal_ref)
            ddt_ref[...] = jnp.zeros_like(ddt_ref)

        cols = [slice(h * hd, (h + 1) * hd) for h in range(hb)]
        heads = range(hb)
        args = ([q_ref[:, cl] for cl in cols], [k_ref[:, cl] for cl in cols], [v_ref[:, cl] for cl in cols],
                [a_ref[h] for h in heads], [b_ref[h] for h in heads], [al_ref[h] for h in heads],
                [dt_ref[h] for h in heads], [st_ref[h] for h in heads])
        saved = [inv_ref[h] for h in heads]
        _, vjp, _ = jax.vjp(lambda *a: _dn_chunk(*a, saved_inverse=saved), *args, has_aux=True)
        dq, dk, dv, da, db, dal, ddt, ds0 = vjp(([do_ref[:, cl] for cl in cols], [dstate[h] for h in heads]))
        for h in heads:
            dqkv_ref[:, h * hd:(h + 1) * hd] = dq[h]
            dqkv_ref[:, w + h * hd:w + (h + 1) * hd] = dk[h]
            dqkv_ref[:, 2 * w + h * hd:2 * w + (h + 1) * hd] = dv[h]
            da_ref[h] = da[h]
            db_ref[h] = db[h]
            dal_ref[h] += dal[h]
            ddt_ref[h] += ddt[h]
            dstate[h] = ds0[h]

    n_of = lambda n: nc - 1 - n
    outs = pl.pallas_call(
        body, grid=(nh // hb, nc),
        in_specs=qkv_specs + [row, row, scal, scal, st, inv, o_spec],
        out_specs=[pl.BlockSpec((CHUNK, 3 * w), lambda h, n: (n_of(n), 0)), row, row, scal, scal],
        out_shape=[jax.ShapeDtypeStruct((s, 3 * w), F32)]
        + [jax.ShapeDtypeStruct(a_rows.shape, F32)] * 2 + [jax.ShapeDtypeStruct((nh, 1, 1), F32)] * 2,
        scratch_shapes=[pltpu.VMEM((hb, hd, hd), F32)],
        compiler_params=_cparams(2), name=name,
    )(qkv, qkv, qkv, a_rows, b_rows, alog, dtb, states, inverses, do)
    return outs


def _dn_post_fwd(o, src, gate_col0, nw, *, name, tm=512):
    s, w = o.shape
    hd = DN_HEAD_DIM
    gc = gate_col0 * LANES // w

    def body(o_ref, g_ref, w_ref, y_ref):
        for h in range(w // hd):
            cols = slice(h * hd, (h + 1) * hd)
            ov = o_ref[:, cols]
            r = lax.rsqrt(jnp.mean(ov * ov, axis=-1, keepdims=True) + EPS)
            y_ref[:, cols] = (ov * r * w_ref[...] * _silu(g_ref[:, cols])).astype(y_ref.dtype)

    blk = pl.BlockSpec((tm, w), lambda i: (i, 0))
    return pl.pallas_call(
        body, grid=(s // tm,),
        in_specs=[blk, pl.BlockSpec((tm, w), lambda i: (i, gc)), pl.BlockSpec((1, hd), lambda i: (0, 0))],
        out_specs=blk, out_shape=jax.ShapeDtypeStruct((s, w), MXU_DTYPE),
        compiler_params=_cparams(1), name=name,
    )(o, src, nw.reshape(1, hd))


def _dn_post_bwd(o, src, gate_col0, nw, dy, into, *, name, tm=512):
    s, w = o.shape
    hd = DN_HEAD_DIM
    gc = gate_col0 * LANES // w

    def body(o_ref, g_ref, w_ref, dy_ref, into_ref, do_ref, dg_ref, dw_ref):
        @pl.when(pl.program_id(0) == 0)
        def _():
            dw_ref[...] = jnp.zeros_like(dw_ref)

        dw = jnp.zeros((1, hd), F32)
        for h in range(w // hd):
            cols = slice(h * hd, (h + 1) * hd)
            ov, gv, dyv = o_ref[:, cols], g_ref[:, cols], dy_ref[:, cols]
            r = lax.rsqrt(jnp.mean(ov * ov, axis=-1, keepdims=True) + EPS)
            oh = ov * r
            sg, sg_grad = _silu_and_grad(gv)
            dn = dyv * sg
            dg_ref[:, cols] = (dyv * (oh * w_ref[...]) * sg_grad).astype(dg_ref.dtype)
            don = dn * w_ref[...]
            do_ref[:, cols] = r * (don - oh * jnp.mean(don * oh, axis=-1, keepdims=True))
            dw = dw + jnp.sum(dn * oh, axis=0, keepdims=True)
        dw_ref[...] += dw

    blk = pl.BlockSpec((tm, w), lambda i: (i, 0))
    wspec = pl.BlockSpec((1, hd), lambda i: (0, 0))
    gate_blk = pl.BlockSpec((tm, w), lambda i: (i, gc))
    do, dg, dw = pl.pallas_call(
        body, grid=(s // tm,),
        in_specs=[blk, gate_blk, wspec, blk, ANY],
        out_specs=[blk, gate_blk, wspec],
        out_shape=[jax.ShapeDtypeStruct((s, w), F32), jax.ShapeDtypeStruct(into.shape, into.dtype),
                   jax.ShapeDtypeStruct((1, hd), F32)],
        input_output_aliases={4: 1},
        compiler_params=_cparams(1), name=name,
    )(o, src, nw.reshape(1, hd), dy, into)
    return do, dg, dw.reshape(hd)


def _sb_consts():
    r2 = lax.broadcasted_iota(jnp.int32, (2 * SB_BLOCK, SB_BLOCK), 0)
    c2 = lax.broadcasted_iota(jnp.int32, (2 * SB_BLOCK, SB_BLOCK), 1)
    r = lax.broadcasted_iota(jnp.int32, (SB_BLOCK, SB_BLOCK), 0)
    c = lax.broadcasted_iota(jnp.int32, (SB_BLOCK, SB_BLOCK), 1)
    lm0 = c < SB_HEAD_DIM
    m_gt = jnp.where(r > c, 1.0, 0.0).astype(BF16)
    m_lt = jnp.where(r < c, 1.0, 0.0).astype(BF16)
    return r2, c2, lm0, m_gt, m_lt


def _sb_stack(x, lm0):
    return jnp.concatenate([jnp.where(lm0, x, 0.0), jnp.where(lm0, 0.0, x)], axis=0)


def _sb_unstack(x2, lm0):
    return jnp.where(lm0, x2[:SB_BLOCK], x2[SB_BLOCK:])


def _sb_fwd(src, col0, width, *, name):
    s = src.shape[0]
    nq = s // SB_BLOCK
    npair = width // LANES
    scale = SB_HEAD_DIM ** -0.5
    nu = math.gcd(SB_UNROLL, nq)

    def body(q_ref, k_ref, v_ref, o_ref, w_hbm, stage, sems):
        p, i = pl.program_id(0), pl.program_id(1)
        r2, c2, lm0, m_gt, _ = _sb_consts()
        t_glob = i * SB_BLOCK + (r2 & (SB_BLOCK - 1))
        q2 = (_sb_stack(q_ref[...], lm0) * scale).astype(MXU_DTYPE)

        t = p * nq + i
        half = t % 2
        ngrp = nq // nu

        def save(half_, grp, pp, ii):
            return pltpu.make_async_copy(stage.at[half_, grp], w_hbm.at[pp, ii, grp], sems.at[half_, grp])

        def drain(half_, pp, ii):
            for grp in range(ngrp):
                @pl.when(grp <= ii // nu)
                def _():
                    save(half_, grp, pp, ii).wait()

        def group(base, carry, masked):
            o2, rsum = carry
            js = [base + nu - 1 - u for u in range(nu)]
            offs = [pl.multiple_of(j * SB_BLOCK, SB_BLOCK) for j in js]
            zs = [_dot(q2, k_ref[pl.ds(off, SB_BLOCK), :].astype(MXU_DTYPE), NT) for off in offs]
            ts = [jnp.log(1.0 + jnp.exp(-jnp.abs(z))) for z in zs]
            lks = [-(jnp.maximum(z, 0.0) + t) for z, t in zip(zs, ts)]
            if masked:
                masks = [(j * SB_BLOCK + c2) < t_glob for j in js]
                lks = [jnp.where(mk, lk, 0.0) for mk, lk in zip(masks, lks)]
            sufs = [_split_dot(lk, m_gt, SB_SPLIT) for lk in lks]
            rs = [rsum]
            for lk in lks:
                rs.append(rs[-1] + jnp.sum(lk, axis=1, keepdims=True))
            wgts = [jnp.exp((jnp.minimum(z, 0.0) - t) + r_ + sf) for z, t, r_, sf in zip(zs, ts, rs, sufs)]
            if masked:
                wgts = [jnp.where(mk, wg, 0.0) for mk, wg in zip(masks, wgts)]
            wbs = [wg.astype(MXU_DTYPE) for wg in wgts]
            grp = base // nu
            for u, wb in enumerate(wbs):
                stage[half, grp, nu - 1 - u] = wb
            save(half, grp, p, i).start()
            for off, wb in zip(offs, wbs):
                o2 = o2 + _dot(wb, v_ref[pl.ds(off, SB_BLOCK), :].astype(MXU_DTYPE), NN)
            return o2, rs[-1]

        top0 = (i // nu) * nu
        last = i // nu
        carry = group(top0, (jnp.zeros((2 * SB_BLOCK, LANES), F32), jnp.zeros((2 * SB_BLOCK, 1), F32)), True)
        o2, _ = lax.fori_loop(1, last + 1, lambda g, cr: group(top0 - nu * g, cr, False), carry)
        o_ref[...] = _sb_unstack(o2, lm0)

        @pl.when(t >= 1)
        def _():
            drain(1 - half, (t - 1) // nq, (t - 1) % nq)

        @pl.when(t == npair * nq - 1)
        def _():
            drain(half, p, i)

    blk = pl.BlockSpec((SB_BLOCK, LANES), lambda p, i: (i, p))
    return pl.pallas_call(
        body, grid=(npair, nq),
        in_specs=[pl.BlockSpec((SB_BLOCK, LANES), lambda p, i: (i, col0 + p)),
                  pl.BlockSpec((s, LANES), lambda p, i: (0, col0 + npair + p)),
                  pl.BlockSpec((s, LANES), lambda p, i: (0, col0 + 2 * npair + p))],
        out_specs=[blk, ANY],
        out_shape=[jax.ShapeDtypeStruct((s, width), F32),
                   jax.ShapeDtypeStruct((npair, nq, nq // nu, nu, 2 * SB_BLOCK, LANES), MXU_DTYPE)],
        scratch_shapes=[pltpu.VMEM((2, nq // nu, nu, 2 * SB_BLOCK, LANES), MXU_DTYPE),
                        pltpu.SemaphoreType.DMA((2, nq // nu))],
        compiler_params=_cparams(2), name=name,
    )(src, src, src)


def _sb_bwd(src, col0, width, weights, do, *, name):
    s = src.shape[0]
    nq = s // SB_BLOCK
    npair = width // LANES
    scale = SB_HEAD_DIM ** -0.5
    nu = math.gcd(SB_UNROLL, nq)

    def body(q_ref, k_ref, v_ref, w_hbm, do_ref, dq_ref, dk_ref, dv_ref, stage, sems):
        p, i = pl.program_id(0), pl.program_id(1)

        @pl.when(i == 0)
        def _():
            dk_ref[...] = jnp.zeros_like(dk_ref)
            dv_ref[...] = jnp.zeros_like(dv_ref)

        r2, c2, lm0, _, m_lt = _sb_consts()
        t_glob = i * SB_BLOCK + (r2 & (SB_BLOCK - 1))
        q2 = (_sb_stack(q_ref[...], lm0) * scale).astype(MXU_DTYPE)
        do2 = _sb_stack(do_ref[...], lm0).astype(MXU_DTYPE)

        ngrp = nq // nu

        def load(half_, grp, pp, ii):
            return pltpu.make_async_copy(w_hbm.at[pp, ii, grp], stage.at[half_, grp], sems.at[half_, grp])

        def fetch_step(half_, pp, ii):
            for grp in range(ngrp):
                @pl.when(grp <= ii // nu)
                def _():
                    load(half_, grp, pp, ii).start()

        def group(g, carry, masked, slot):
            dq2, csum = carry
            js = [nu * g + u for u in range(nu)]
            offs = [pl.multiple_of(j * SB_BLOCK, SB_BLOCK) for j in js]
            kbs = [k_ref[pl.ds(off, SB_BLOCK), :].astype(MXU_DTYPE) for off in offs]
            zs = [_dot(q2, kb, NT) for kb in kbs]
            dws = [_dot(do2, v_ref[pl.ds(off, SB_BLOCK), :].astype(MXU_DTYPE), NT) for off in offs]
            wbs = [stage[slot[0], slot[1], u] for u in range(nu)]
            sigs = [_sigmoid(z) for z in zs]
            dlogas = [wb.astype(F32) * dw for wb, dw in zip(wbs, dws)]
            pres = [_split_dot(dl, m_lt, SB_SPLIT) for dl in dlogas]
            dlks = []
            for dl, pre in zip(dlogas, pres):
                dlks.append(csum + pre)
                csum = csum + jnp.sum(dl, axis=1, keepdims=True)
            if masked:
                dlks = [jnp.where((j * SB_BLOCK + c2) < t_glob, dlk, 0.0) for j, dlk in zip(js, dlks)]
            dzbs = [(dl * (1.0 - sg) - dlk * sg).astype(MXU_DTYPE) for dl, sg, dlk in zip(dlogas, sigs, dlks)]
            for off, dzb, wb, kb in zip(offs, dzbs, wbs, kbs):
                dk_ref[pl.ds(off, SB_BLOCK), :] += _dot(dzb, q2, TN)
                dv_ref[pl.ds(off, SB_BLOCK), :] += _dot(wb, do2, TN)
                dq2 = dq2 + _dot(dzb, kb, NN)
            return dq2, csum

        t = p * nq + i
        half = t % 2

        @pl.when(t == 0)
        def _():
            fetch_step(0, p, i)

        @pl.when(t + 1 < npair * nq)
        def _():
            fetch_step(1 - half, (t + 1) // nq, (t + 1) % nq)

        def step(g, carry):
            load(half, g, p, i).wait()
            return group(g, carry, False, (half, g))

        last = i // nu
        carry = lax.fori_loop(0, last, step, (jnp.zeros((2 * SB_BLOCK, LANES), F32), jnp.zeros((2 * SB_BLOCK, 1), F32)))
        load(half, last, p, i).wait()
        dq2, _ = group(last, carry, True, (half, last))
        dq_ref[...] = _sb_unstack(dq2, lm0) * scale

    blk = pl.BlockSpec((SB_BLOCK, LANES), lambda p, i: (i, p))
    full = pl.BlockSpec((s, LANES), lambda p, i: (0, p))
    return pl.pallas_call(
        body, grid=(npair, nq),
        in_specs=[pl.BlockSpec((SB_BLOCK, LANES), lambda p, i: (i, col0 + p)),
                  pl.BlockSpec((s, LANES), lambda p, i: (0, col0 + npair + p)),
                  pl.BlockSpec((s, LANES), lambda p, i: (0, col0 + 2 * npair + p)),
                  ANY, blk],
        out_specs=[blk, full, full],
        out_shape=[jax.ShapeDtypeStruct((s, width), F32)] * 3,
        scratch_shapes=[pltpu.VMEM((2, nq // nu, nu, 2 * SB_BLOCK, LANES), MXU_DTYPE),
                        pltpu.SemaphoreType.DMA((2, nq // nu))],
        compiler_params=_cparams(2), name=name,
    )(src, src, src, weights, do)


def _ssd_group(xs, dt_rows, alogs, dtbs, bms, cms, h0s):
    c = bms[0].shape[0]
    per = len(xs) // len(bms)
    ii, jj = _chunk_masks(c)
    causal, eye = ii >= jj, ii == jj
    scores = [t for t in _each(lambda c_, b_: _hdot(c_, b_, NT), cms, bms) for _ in range(per)]
    dt_r = _each(lambda dt, b: _softplus(dt + b), dt_rows, dtbs)
    a_r = _each(lambda al, dt: -jnp.exp(al) * dt, alogs, dt_r)
    dt_col = _each(lambda dt: _row_to_col(dt, eye), dt_r)
    a_col = _each(lambda a: _row_to_col(a, eye), a_r)
    ac_col = _each(lambda a: jnp.sum(jnp.where(causal, a, 0.0), axis=1, keepdims=True), a_r)
    ac_row = _each(lambda a: jnp.sum(jnp.where(jj >= ii, a, 0.0), axis=0, keepdims=True), a_col)
    lmat = _each(lambda c_, r_: jnp.exp(jnp.where(causal, c_ - r_, NEG_BIG)), ac_col, ac_row)
    xdt = _each(jnp.multiply, xs, dt_col)
    al = _each(lambda a: jnp.sum(a, axis=1, keepdims=True), a_r)
    bm = [t for t in bms for _ in range(per)]
    cm = [t for t in cms for _ in range(per)]
    ys = _each(lambda sc, lm, xd, cm_, h0, ac: _hdot(sc * lm, xd) + _hdot(cm_, h0, NT) * jnp.exp(ac),
               scores, lmat, xdt, cm, h0s, ac_col)
    h1s = _each(lambda h0, al_, xd, ac, bm_: h0 * jnp.exp(al_) + _hdot(xd * jnp.exp(al_ - ac), bm_, TN),
                h0s, al, xdt, ac_col, bm)
    return ys, h1s


def _ssd_specs(ng, nc, r, gb, rev):
    n_of = (lambda n: nc - 1 - n) if rev else (lambda n: n)
    xw, bw = gb * r * SSM_HEAD_DIM, gb * SSM_STATE
    b0, c0 = (ng * r * SSM_HEAD_DIM) // bw, (ng * r * SSM_HEAD_DIM + ng * SSM_STATE) // bw
    x_spec = pl.BlockSpec((CHUNK, xw), lambda g, n: (n_of(n), g))
    b_spec = pl.BlockSpec((CHUNK, bw), lambda g, n: (n_of(n), b0 + g))
    c_spec = pl.BlockSpec((CHUNK, bw), lambda g, n: (n_of(n), c0 + g))
    dt_spec = pl.BlockSpec((gb, None, r, CHUNK), lambda g, n: (g, n_of(n), 0, 0))
    sc_spec = pl.BlockSpec((gb, r, 1), lambda g, n: (g, 0, 0))
    st_spec = pl.BlockSpec((gb, None, r, SSM_HEAD_DIM, SSM_STATE), lambda g, n: (g, n_of(n), 0, 0, 0))
    bc_out = pl.BlockSpec((CHUNK, bw), lambda g, n: (n_of(n), g))
    return x_spec, b_spec, c_spec, dt_spec, sc_spec, st_spec, x_spec, bc_out


def _ssd_refs(gb, r, x_ref, b_ref, c_ref, dt_ref, al_ref, db_ref):
    p, n = SSM_HEAD_DIM, SSM_STATE
    heads = [(g, h) for g in range(gb) for h in range(r)]
    xs = [x_ref[:, (g * r + h) * p:(g * r + h + 1) * p] for g, h in heads]
    dts = [dt_ref[g, h:h + 1, :] for g, h in heads]
    als = [al_ref[g, h:h + 1, :] for g, h in heads]
    dbs = [db_ref[g, h:h + 1, :] for g, h in heads]
    bms = [b_ref[:, g * n:(g + 1) * n] for g in range(gb)]
    cms = [c_ref[:, g * n:(g + 1) * n] for g in range(gb)]
    return heads, xs, dts, als, dbs, bms, cms


def _ssd_fwd(xbc, dt_rows, alog, dtb, *, name):
    s = xbc.shape[0]
    ng, nc, r = dt_rows.shape[0], dt_rows.shape[1], dt_rows.shape[2]
    w = ng * r * SSM_HEAD_DIM
    gb = math.gcd(SSD_GROUPS_PER_STEP, ng)
    x_spec, b_spec, c_spec, dt_spec, sc_spec, st_spec, y_spec, _ = _ssd_specs(ng, nc, r, gb, False)
    p = SSM_HEAD_DIM

    def body(x_ref, b_ref, c_ref, dt_ref, al_ref, db_ref, y_ref, st_ref, state):
        @pl.when(pl.program_id(1) == 0)
        def _():
            state[...] = jnp.zeros_like(state)

        st_ref[...] = state[...]
        heads, xs, dts, als, dbs, bms, cms = _ssd_refs(gb, r, x_ref, b_ref, c_ref, dt_ref, al_ref, db_ref)
        ys, h1s = _ssd_group(xs, dts, als, dbs, bms, cms, [state[g, h] for g, h in heads])
        for i, (g, h) in enumerate(heads):
            y_ref[:, (g * r + h) * p:(g * r + h + 1) * p] = ys[i]
            state[g, h] = h1s[i]

    return pl.pallas_call(
        body, grid=(ng // gb, nc),
        in_specs=[x_spec, b_spec, c_spec, dt_spec, sc_spec, sc_spec],
        out_specs=[y_spec, st_spec],
        out_shape=[jax.ShapeDtypeStruct((s, w), F32), jax.ShapeDtypeStruct((ng, nc, r, p, SSM_STATE), F32)],
        scratch_shapes=[pltpu.VMEM((gb, r, p, SSM_STATE), F32)],
        compiler_params=_cparams(2), name=name,
    )(xbc, xbc, xbc, dt_rows, alog, dtb)


def _ssd_bwd(xbc, dt_rows, alog, dtb, states, dy, *, name):
    s = xbc.shape[0]
    ng, nc, r = dt_rows.shape[0], dt_rows.shape[1], dt_rows.shape[2]
    w = ng * r * SSM_HEAD_DIM
    gb = math.gcd(SSD_GROUPS_PER_STEP, ng)
    x_spec, b_spec, c_spec, dt_spec, sc_spec, st_spec, y_spec, bc_out = _ssd_specs(ng, nc, r, gb, True)
    p = SSM_HEAD_DIM

    def body(x_ref, b_ref, c_ref, dt_ref, al_ref, db_ref, st_ref, dy_ref,
             dx_ref, dbm_ref, dcm_ref, ddt_ref, dal_ref, ddb_ref, dstate):
        @pl.when(pl.program_id(1) == 0)
        def _():
            dstate[...] = jnp.zeros_like(dstate)
            dal_ref[...] = jnp.zeros_like(dal_ref)
            ddb_ref[...] = jnp.zeros_like(ddb_ref)

        heads, xs, dts, als, dbs, bms, cms = _ssd_refs(gb, r, x_ref, b_ref, c_ref, dt_ref, al_ref, db_ref)
        _, vjp = jax.vjp(_ssd_group, xs, dts, als, dbs, bms, cms, [st_ref[g, h] for g, h in heads])
        dys = [dy_ref[:, (g * r + h) * p:(g * r + h + 1) * p] for g, h in heads]
        dxs, ddts, dals, ddbs, dbms, dcms, dh0s = vjp((dys, [dstate[g, h] for g, h in heads]))
        for g in range(gb):
            dbm_ref[:, g * SSM_STATE:(g + 1) * SSM_STATE] = dbms[g]
            dcm_ref[:, g * SSM_STATE:(g + 1) * SSM_STATE] = dcms[g]
        for i, (g, h) in enumerate(heads):
            dx_ref[:, (g * r + h) * p:(g * r + h + 1) * p] = dxs[i]
            ddt_ref[g, h:h + 1, :] = ddts[i]
            dal_ref[g, h:h + 1, :] += dals[i]
            ddb_ref[g, h:h + 1, :] += ddbs[i]
            dstate[g, h] = dh0s[i]

    gn = ng * SSM_STATE
    return pl.pallas_call(
        body, grid=(ng // gb, nc),
        in_specs=[x_spec, b_spec, c_spec, dt_spec, sc_spec, sc_spec, st_spec, y_spec],
        out_specs=[y_spec, bc_out, bc_out, dt_spec, sc_spec, sc_spec],
        out_shape=[jax.ShapeDtypeStruct((s, w), F32), jax.ShapeDtypeStruct((s, gn), F32), jax.ShapeDtypeStruct((s, gn), F32),
                   jax.ShapeDtypeStruct(dt_rows.shape, F32), jax.ShapeDtypeStruct((ng, r, 1), F32),
                   jax.ShapeDtypeStruct((ng, r, 1), F32)],
        scratch_shapes=[pltpu.VMEM((gb, r, p, SSM_STATE), F32)],
        compiler_params=_cparams(2), name=name,
    )(xbc, xbc, xbc, dt_rows, alog, dtb, states, dy)


def _ssm_post_fwd(y, xbc, src, z_col0, dexp, nw, *, name, tm=512):
    s, w = y.shape
    gw = w // SSM_GROUPS
    zc = z_col0 * LANES // gw

    def body(y_ref, x_ref, z_ref, d_ref, w_ref, o_ref):
        yy = (y_ref[...] + x_ref[...] * d_ref[...]) * _silu(z_ref[...])
        r = lax.rsqrt(jnp.mean(yy * yy, axis=-1, keepdims=True) + EPS)
        o_ref[...] = (yy * r * w_ref[...]).astype(o_ref.dtype)

    blk = pl.BlockSpec((tm, gw), lambda g, i: (i, g))
    vec = pl.BlockSpec((1, gw), lambda g, i: (0, g))
    return pl.pallas_call(
        body, grid=(SSM_GROUPS, s // tm),
        in_specs=[blk, blk, pl.BlockSpec((tm, gw), lambda g, i: (i, zc + g)), vec, vec],
        out_specs=blk, out_shape=jax.ShapeDtypeStruct((s, w), MXU_DTYPE),
        compiler_params=_cparams(2), name=name,
    )(y, xbc, src, dexp.reshape(1, w), nw.reshape(1, w))


def _ssm_post_bwd(y, xbc, src, z_col0, dexp, nw, dout, into, *, name, tm=512):
    s, w = y.shape
    gw = w // SSM_GROUPS
    zc = z_col0 * LANES // gw

    def body(y_ref, x_ref, z_ref, d_ref, w_ref, do_ref, into_ref, dy_ref, dx_ref, dz_ref, dd_ref, dw_ref):
        xv, zv, dv = x_ref[...], z_ref[...], d_ref[...]
        pre = y_ref[...] + xv * dv
        sz, sz_grad = _silu_and_grad(zv)
        yy = pre * sz
        r = lax.rsqrt(jnp.mean(yy * yy, axis=-1, keepdims=True) + EPS)
        yh = yy * r
        dov = do_ref[...]
        dyn = dov * w_ref[...]
        dyy = r * (dyn - yh * jnp.mean(dyn * yh, axis=-1, keepdims=True))
        dpre = dyy * sz
        dy_ref[...] = dpre
        dx_ref[...] = dpre * dv
        dz_ref[...] = (dyy * pre * sz_grad).astype(dz_ref.dtype)

        @pl.when(pl.program_id(1) == 0)
        def _():
            dd_ref[...] = jnp.zeros_like(dd_ref)
            dw_ref[...] = jnp.zeros_like(dw_ref)

        dd_ref[...] += jnp.sum(dpre * xv, axis=0, keepdims=True)
        dw_ref[...] += jnp.sum(dov * yh, axis=0, keepdims=True)

    blk = pl.BlockSpec((tm, gw), lambda g, i: (i, g))
    vec = pl.BlockSpec((1, gw), lambda g, i: (0, g))
    z_blk = pl.BlockSpec((tm, gw), lambda g, i: (i, zc + g))
    dy, dx, dz, dd, dw = pl.pallas_call(
        body, grid=(SSM_GROUPS, s // tm),
        in_specs=[blk, blk, z_blk, vec, vec, blk, ANY],
        out_specs=[blk, blk, z_blk, vec, vec],
        out_shape=[jax.ShapeDtypeStruct((s, w), F32), jax.ShapeDtypeStruct((s, w), F32),
                   jax.ShapeDtypeStruct(into.shape, into.dtype), jax.ShapeDtypeStruct((1, w), F32),
                   jax.ShapeDtypeStruct((1, w), F32)],
        input_output_aliases={6: 2},
        compiler_params=_cparams(2), name=name,
    )(y, xbc, src, dexp.reshape(1, w), nw.reshape(1, w), dout, into)
    return dy, dx, dz, dd.reshape(w), dw.reshape(w)


def _merge_fwd(proj3, src, gate_col0, d, *, name, tm=512):
    s = proj3.shape[0]
    nb = proj3.shape[1] // d
    gc = gate_col0 * LANES // d

    def body(*refs):
        p_refs, g_refs, o_ref = refs[:nb], refs[nb:2 * nb], refs[-1]
        acc = None
        for p_ref, g_ref in zip(p_refs, g_refs):
            term = _sigmoid(g_ref[...]) * p_ref[...]
            acc = term if acc is None else acc + term
        o_ref[...] = acc.astype(o_ref.dtype)

    p_specs = [pl.BlockSpec((tm, d), lambda i, b=b: (i, b)) for b in range(nb)]
    g_specs = [pl.BlockSpec((tm, d), lambda i, b=b: (i, gc + b)) for b in range(nb)]
    return pl.pallas_call(
        body, grid=(s // tm,), in_specs=p_specs + g_specs,
        out_specs=pl.BlockSpec((tm, d), lambda i: (i, 0)), out_shape=jax.ShapeDtypeStruct((s, d), MXU_DTYPE),
        compiler_params=_cparams(1), name=name,
    )(*([proj3] * nb), *([src] * nb))


def _merge_bwd(proj3, src, gate_col0, d, dmerged, into, *, name, tm=512):
    s = proj3.shape[0]
    nb = proj3.shape[1] // d
    gc = gate_col0 * LANES // d

    def body(p_ref, g_ref, dm_ref, into_ref, dp_ref, dg_ref):
        sg = _sigmoid(g_ref[...])
        dm = dm_ref[...]
        dp_ref[...] = (dm * sg).astype(dp_ref.dtype)
        dg_ref[...] = (dm * p_ref[...] * sg * (1.0 - sg)).astype(dg_ref.dtype)

    blk = pl.BlockSpec((tm, d), lambda i, b: (i, b))
    gate_blk = pl.BlockSpec((tm, d), lambda i, b: (i, gc + b))
    return pl.pallas_call(
        body, grid=(s // tm, nb),
        in_specs=[blk, gate_blk, pl.BlockSpec((tm, d), lambda i, b: (i, 0)), ANY],
        out_specs=[blk, gate_blk],
        out_shape=[jax.ShapeDtypeStruct(proj3.shape, MXU_DTYPE), jax.ShapeDtypeStruct(into.shape, into.dtype)],
        input_output_aliases={3: 1},
        compiler_params=_cparams(2), name=name,
    )(proj3, src, dmerged, into)


ANY = pl.BlockSpec(memory_space=pl.ANY)
MESH = pl.DeviceIdType.MESH


def _all_gather(shards, *, name, after=None):
    nt = len(shards)
    n_after = 0 if after is None else 1

    def body(*refs):
        x_refs, out_refs = refs[:nt], refs[nt + n_after:2 * nt + n_after]
        send_sems, recv_sems, local_sems = refs[2 * nt + n_after:]
        x, y, c = lax.axis_index("x"), lax.axis_index("y"), lax.axis_index("c")
        me, sibling = (x, y, c), (x, y, 1 - c)
        chips = [(1 - x, y), (x, 1 - y), (1 - x, 1 - y)]

        def slot(t, px, py, pc):
            return out_refs[t].at[4 * px + 2 * py + pc]

        def copy(t, k, block, to, from_input=False):
            return pltpu.make_async_remote_copy(
                src_ref=x_refs[t] if from_input else slot(t, *block), dst_ref=slot(t, *block),
                send_sem=send_sems.at[7 * t + k], recv_sem=recv_sems.at[7 * t + k], device_id=to, device_id_type=MESH)

        mine = [pltpu.make_async_copy(x_refs[t], slot(t, *me), local_sems.at[t]) for t in range(nt)]
        for cp in mine:
            cp.start()
        first = [copy(t, 0, me, sibling, True) for t in range(nt)]
        first += [copy(t, 1 + j, me, (*chip, c), True) for j, chip in enumerate(chips) for t in range(nt)]
        for cp in first:
            cp.start()
        passed = []
        for j, chip in enumerate(chips):
            for t in range(nt):
                copy(t, 1 + j, (*chip, c), me).wait_recv()
                fwd = copy(t, 4 + j, (*chip, c), sibling)
                fwd.start()
                passed.append(fwd)
        for t in range(nt):
            copy(t, 0, sibling, me).wait_recv()
            for j, chip in enumerate(chips):
                copy(t, 4 + j, (*chip, 1 - c), me).wait_recv()
        for cp in first + passed:
            cp.wait_send()
        for cp in mine:
            cp.wait()

    return pl.pallas_call(
        body, out_shape=[jax.ShapeDtypeStruct((N_DEV,) + a.shape, a.dtype) for a in shards],
        in_specs=[ANY] * (nt + n_after), out_specs=[ANY] * nt,
        scratch_shapes=[pltpu.SemaphoreType.DMA((7 * nt,)), pltpu.SemaphoreType.DMA((7 * nt,)),
                        pltpu.SemaphoreType.DMA((nt,))],
        name=name,
    )(*shards, *([] if after is None else [after]))


HBM = pl.BlockSpec(memory_space=pltpu.HBM)
SEM = pl.BlockSpec(memory_space=pltpu.SEMAPHORE)
EFFECT = pltpu.SideEffectType.DATAFLOW_SIDE_EFFECTING


def _peers():
    x, y, c = lax.axis_index("x"), lax.axis_index("y"), lax.axis_index("c")
    peers = []
    for k in range(1, N_DEV):
        px, py, pc = x ^ ((k >> 2) & 1), y ^ ((k >> 1) & 1), c ^ (k & 1)
        peers.append(((px, py, pc), 4 * px + 2 * py + pc))
    return 4 * x + 2 * y + c, peers


def _split_copies(slots, src_refs, land_refs, send_sems, recv_sems):
    me, peers = _peers()
    copies = []
    for t, (whole, layer) in enumerate(slots):
        dst = land_refs[t].at[me] if layer is None else land_refs[t].at[me, layer]
        for k, (dev, lin) in enumerate(peers):
            copies.append(pltpu.make_async_remote_copy(
                src_ref=src_refs[t] if whole else src_refs[t].at[lin], dst_ref=dst,
                send_sem=send_sems.at[7 * t + k], recv_sem=recv_sems.at[7 * t + k], device_id=dev, device_id_type=MESH))
    return copies


def _split_start(srcs, lands, slots, carry, *, name):
    n = len(srcs)

    def body(*refs):
        copies = _split_copies(slots, refs[:n], refs[n:2 * n], refs[2 * n + 1], refs[2 * n + 2])
        for cp in copies:
            cp.start()

    def hbm(a):
        return pltpu.HBM(a.shape, a.dtype)

    outs = pl.pallas_call(
        body, name=name,
        out_shape=[pltpu.SemaphoreType.DMA((7 * n,)), pltpu.SemaphoreType.DMA((7 * n,))]
        + [hbm(a) for a in srcs] + [hbm(a) for a in lands] + [hbm(carry)],
        in_specs=[HBM] * (2 * n + 1), out_specs=[SEM, SEM] + [HBM] * (2 * n + 1),
        input_output_aliases={i: 2 + i for i in range(2 * n + 1)},
        compiler_params=pltpu.CompilerParams(has_side_effects=EFFECT),
    )(*[pltpu.with_memory_space_constraint(a, pltpu.HBM) for a in list(srcs) + list(lands) + [carry]])
    return outs[0], outs[1], outs[2:2 + n], outs[2 + n:2 + 2 * n], outs[2 + 2 * n]


def _split_wait(send_sems, recv_sems, srcs, lands, slots, after, *, name):
    n = len(srcs)

    def body(*refs):
        copies = _split_copies(slots, refs[:n], refs[n:2 * n], refs[2 * n], refs[2 * n + 1])
        for cp in copies:
            cp.wait_send()
        for cp in copies:
            cp.wait_recv()

    outs = pl.pallas_call(
        body, name=name,
        out_shape=[pltpu.HBM(a.shape, a.dtype) for a in list(srcs) + list(lands)],
        in_specs=[HBM] * (2 * n) + [SEM, SEM, ANY], out_specs=[HBM] * (2 * n),
        input_output_aliases={i: i for i in range(2 * n)},
        compiler_params=pltpu.CompilerParams(has_side_effects=EFFECT),
    )(*srcs, *lands, send_sems, recv_sems, after)
    return outs[n:]


def _adam_math(w, g, m, v):
    m1 = ADAM_B1 * m + (1.0 - ADAM_B1) * g
    v1 = ADAM_B2 * v + (1.0 - ADAM_B2) * (g * g)
    m_hat = m1 / (1.0 - ADAM_B1 ** ADAM_STEP)
    v_hat = v1 / (1.0 - ADAM_B2 ** ADAM_STEP)
    delta = -ADAM_LR * (m_hat / (jnp.sqrt(v_hat) + ADAM_EPS) + ADAM_WD * w)
    return delta, m1, v1


def _sum_adamw(parts, w, m, v, layer, prev, *, name):
    shape = w.shape
    r, c = shape[-2], shape[-1]
    a_l = math.prod(shape[1:-2])
    a = shape[0] * a_l
    base = layer * a_l
    if r % 256 == 0:
        tr, tc = 256, c
    else:
        tr, tc = r, _pick(c, (256, 128))
    w3, m3, v3 = (t.reshape(a, r, c) for t in (w, m, v))
    n_prev = 0 if prev is None else 4

    def body(*refs):
        p_ref, w_ref, m_ref, v_ref = refs[:4]
        g_ref, d_ref, m1_ref, v1_ref = refs[4 + n_prev:]
        g = p_ref[0].astype(F32)
        for src in range(1, N_DEV):
            g = g + p_ref[src].astype(F32)
        delta, m1, v1 = _adam_math(w_ref[...], g, m_ref[...], v_ref[...])
        g_ref[...] = g
        d_ref[...] = delta
        m1_ref[...] = m1
        v1_ref[...] = v1

    nr, ncol = r // tr, c // tc
    blk = pl.BlockSpec((None, tr, tc), lambda i, j: (base + i, j // ncol, j % ncol))
    prev3 = [] if prev is None else [t.reshape(a, r, c) for t in prev]
    outs = pl.pallas_call(
        body, grid=(a_l, nr * ncol),
        in_specs=[pl.BlockSpec((N_DEV, None, tr, tc), lambda i, j: (0, i, j // ncol, j % ncol)), blk, blk, blk]
        + [ANY] * n_prev,
        out_specs=[blk] * 4, out_shape=[jax.ShapeDtypeStruct((a, r, c), F32)] * 4,
        input_output_aliases={4 + k: k for k in range(n_prev)},
        compiler_params=_cparams(2), name=name,
    )(parts.reshape(N_DEV, a_l, r, c), w3, m3, v3, *prev3)
    return [o.reshape(shape) for o in outs]


def _sum_parts(parts, *, name):
    rows = parts.shape[1]

    def body(p_ref, o_ref):
        g = p_ref[0]
        for src in range(1, N_DEV):
            g = g + p_ref[src]
        o_ref[...] = g

    return pl.pallas_call(
        body, grid=(1,), in_specs=[pl.BlockSpec((N_DEV, rows, LANES), lambda i: (0, 0, 0))],
        out_specs=pl.BlockSpec((rows, LANES), lambda i: (0, 0)), out_shape=jax.ShapeDtypeStruct((rows, LANES), F32),
        compiler_params=_cparams(1), name=name,
    )(parts)


def _adamw(w, g, m, v, *, name):
    rows = w.shape[0]

    def body(w_ref, g_ref, m_ref, v_ref, d_ref, m1_ref, v1_ref):
        delta, m1, v1 = _adam_math(w_ref[...], g_ref[...], m_ref[...], v_ref[...])
        d_ref[...] = delta
        m1_ref[...] = m1
        v1_ref[...] = v1

    blk = pl.BlockSpec((rows, LANES), lambda i: (0, 0))
    return pl.pallas_call(
        body, grid=(1,), in_specs=[blk] * 4, out_specs=[blk] * 3,
        out_shape=[jax.ShapeDtypeStruct((rows, LANES), F32)] * 3,
        compiler_params=_cparams(1), name=name,
    )(w, g, m, v)


def _pack(arrs, dtype, row_mult=16):
    flat = jnp.concatenate([a.reshape(-1).astype(dtype) for a in arrs])
    n = flat.shape[0]
    rows = -(-n // (LANES * row_mult)) * row_mult
    flat = jnp.pad(flat, (0, rows * LANES - n))
    return flat.reshape(rows, LANES)


def _unpack(packed, shapes):
    flat = packed.reshape(-1)
    out, off = [], 0
    for shp in shapes:
        n = math.prod(shp)
        out.append(flat[off:off + n].reshape(shp))
        off += n
    return out


class _Layout:
    def __init__(self, d):
        self.d = d
        w = d
        self.dn_heads = w // DN_HEAD_DIM
        self.ssm_heads = w // SSM_HEAD_DIM
        gn = SSM_GROUPS * SSM_STATE
        self.sizes = (3 * w, w, self.dn_heads, self.dn_heads, 3 * w, w, w + 2 * gn, self.ssm_heads, 3 * d)
        offs, o = [], 0
        for sz in self.sizes:
            offs.append(o)
            o += sz
        self.offs = offs
        self.in_dim = o
        self.big = (0, 1, 4, 5, 6, 8)
        self.small = (2, 3, 7)
        cols, o = {}, 0
        for idx in self.big:
            cols[idx] = o
            o += self.sizes[idx]
        self.small_col = o
        self.cols = cols
        self.padded = o + LANES
        self.n_small = sum(self.sizes[i] for i in self.small)

    def from_shards(self, parts):
        cs = self.in_dim // N_DEV
        pieces = []
        for i in self.big + self.small:
            a, b = self.offs[i], self.offs[i] + self.sizes[i]
            while a < b:
                j = a // cs
                hi = min(b, (j + 1) * cs)
                pieces.append(parts[j][:, a - j * cs:hi - j * cs])
                a = hi
        pieces.append(jnp.zeros((parts.shape[1], LANES - self.n_small), parts.dtype))
        return jnp.concatenate(pieces, axis=1)

    def to_shards(self, wp):
        cs = self.in_dim // N_DEV
        pcol = dict(self.cols)
        o = self.small_col
        for i in self.small:
            pcol[i] = o
            o += self.sizes[i]
        shards = []
        for j in range(N_DEV):
            a, b = j * cs, (j + 1) * cs
            pieces = []
            for i in range(len(self.sizes)):
                lo, hi = max(a, self.offs[i]), min(b, self.offs[i] + self.sizes[i])
                if lo < hi:
                    pieces.append(wp[:, pcol[i] + lo - self.offs[i]:pcol[i] + hi - self.offs[i]])
            shards.append(jnp.concatenate(pieces, axis=1))
        return jnp.stack(shards)

def _rows_form(cols_t, nh, nc):
    return cols_t.T.reshape(nh, nc, 1, CHUNK)


def _layer_fwd(x, p, lay, tag, late=None):
    s, d = x.shape
    nc = s // CHUNK
    w = d
    dnh, smh = lay.dn_heads, lay.ssm_heads
    r = smh // SSM_GROUPS
    cb = {k: v // LANES for k, v in lay.cols.items()}
    sv = {}
    h1 = _rms_fwd(x, p["norm_mix"], name=f"rms_mix_{tag}")
    proj = _matmul(h1, p["w_in"], name=f"mm_in_{tag}")
    small = proj[:, lay.small_col:lay.small_col + LANES]
    a_rows = _rows_form(small[:, 0:dnh], dnh, nc)
    b_rows = _rows_form(small[:, dnh:2 * dnh], dnh, nc)
    dt_rows = small[:, 2 * dnh:2 * dnh + smh].T.reshape(SSM_GROUPS, r, nc, CHUNK).transpose(0, 2, 1, 3)
    zero_b = jnp.zeros((1, 3 * w), F32)
    dn_qkv = _conv_fwd(proj, cb[0], p["dn_conv_w"], zero_b, 2 * dnh, name=f"dn_conv_{tag}")
    dn_alog = p["dn_a_log"].reshape(dnh, 1, 1)
    dn_dtb = p["dn_dt_bias"].reshape(dnh, 1, 1)
    o_dn, dn_states, dn_inv = _dn_fwd(dn_qkv, a_rows, b_rows, dn_alog, dn_dtb, name=f"dn_chunk_{tag}")
    y_dn = _dn_post_fwd(o_dn, proj, cb[1], p["dn_norm_w"], name=f"dn_post_{tag}")
    o_sb, sb_r = _sb_fwd(proj, cb[4], w, name=f"sb_{tag}")
    xbc = _conv_fwd(proj, cb[6], p["ssm_conv_w"], p["ssm_conv_b"].reshape(1, -1), 0, name=f"ssm_conv_{tag}")
    ssm_alog = p["ssm_a_log"].reshape(SSM_GROUPS, r, 1)
    ssm_dtb = p["ssm_dt_bias"].reshape(SSM_GROUPS, r, 1)
    y_ssd, ssm_states = _ssd_fwd(xbc, dt_rows, ssm_alog, ssm_dtb, name=f"ssd_{tag}")
    dexp = jnp.repeat(p["ssm_d"], SSM_HEAD_DIM)
    y_ssm = _ssm_post_fwd(y_ssd, xbc, proj, cb[5], dexp, p["ssm_norm_w"], name=f"ssm_post_{tag}")
    if late is not None:
        p.update(late(y_ssm))
    branches = (y_dn, o_sb, y_ssm)
    proj3 = lax.empty((s, 3 * d), F32)
    for i, br in enumerate(branches):
        proj3 = _matmul(br, p["w_branch"][i], into=(proj3, i * d), name=f"mm_branch{i}_{tag}")
    merged = _merge_fwd(proj3, proj, cb[8], d, name=f"merge_{tag}")
    x1 = _matmul(merged, p["w_out"], name=f"mm_out_{tag}", epilogue=lambda acc, res: (acc + res,), extras=(x,))
    h2 = _rms_fwd(x1, p["norm_mlp"], name=f"rms_mlp_{tag}")
    u, act = _matmul(h2, p["w_up"], name=f"mm_up_{tag}", out_dtypes=(F32, MXU_DTYPE),
                     epilogue=lambda acc: (acc, jnp.square(jnp.maximum(acc, 0.0))))
    x2 = _matmul(act, p["w_down"], name=f"mm_down_{tag}", epilogue=lambda acc, res: (acc + res,), extras=(x1,))
    sv.update(x=x, h1=h1, proj=proj, a_rows=a_rows, b_rows=b_rows, dt_rows=dt_rows, dn_qkv=dn_qkv, dn_alog=dn_alog,
              dn_dtb=dn_dtb, o_dn=o_dn, dn_states=dn_states, dn_inv=dn_inv, y_dn=y_dn, o_sb=o_sb, sb_r=sb_r, xbc=xbc, ssm_alog=ssm_alog,
              ssm_dtb=ssm_dtb, y_ssd=y_ssd, ssm_states=ssm_states, dexp=dexp, y_ssm=y_ssm, proj3=proj3, merged=merged,
              x1=x1, h2=h2, u=u, act=act)
    return x2, sv


def _layer_bwd(dx2, p, sv, lay, tag, early=None, late=None):
    x = sv["x"]
    s, d = x.shape
    nc = s // CHUNK
    w = d
    dnh, smh = lay.dn_heads, lay.ssm_heads
    r = smh // SSM_GROUPS
    gn = SSM_GROUPS * SSM_STATE
    cb = {k: v // LANES for k, v in lay.cols.items()}
    proj = sv["proj"]
    g = {}
    dx2_b = dx2.astype(MXU_DTYPE)
    du = _matmul(dx2_b, p["w_down"], tb=True, name=f"mm_down_dx_{tag}", out_dtypes=(MXU_DTYPE,),
                 epilogue=lambda acc, uu: (acc * (2.0 * jnp.maximum(uu, 0.0)),), extras=(sv["u"],))
    g["w_down"] = _matmul(sv["act"], dx2_b, ta=True, name=f"mm_down_dw_{tag}", out_dtypes=(BF16,)).reshape(N_DEV, -1, d)
    g["w_up"] = _matmul(sv["h2"], du, ta=True, name=f"mm_up_dw_{tag}", out_dtypes=(BF16,), col_shards=N_DEV)
    dh2 = _matmul(du, p["w_up"], tb=True, name=f"mm_up_dx_{tag}")
    dx1, g["norm_mlp"] = _rms_bwd(sv["x1"], p["norm_mlp"], dh2, dx2, name=f"rms_mlp_bwd_{tag}")
    dx1_b = dx1.astype(MXU_DTYPE)
    dmerged = _matmul(dx1_b, p["w_out"], tb=True, name=f"mm_out_dx_{tag}")
    g["w_out"] = _matmul(sv["merged"], dx1_b, ta=True, name=f"mm_out_dw_{tag}", out_dtypes=(BF16,)).reshape(N_DEV, -1, d)
    dproj = lax.empty((s, lay.padded), MXU_DTYPE)
    dproj3, dproj = _merge_bwd(sv["proj3"], proj, cb[8], d, dmerged, dproj, name=f"merge_bwd_{tag}")
    branches = (sv["y_dn"], sv["o_sb"], sv["y_ssm"])
    dwb, dbr = [], []
    for i, br in enumerate(branches):
        dp_i = dproj3[:, i * d:(i + 1) * d]
        dwb.append(_matmul(br, dp_i, ta=True, name=f"mm_branch{i}_dw_{tag}", out_dtypes=(BF16,)).reshape(N_DEV, -1, d))
        dbr.append(_matmul(dp_i, p["w_branch"][i], tb=True, name=f"mm_branch{i}_dx_{tag}"))
    g["w_branch"] = jnp.stack(dwb, axis=1)
    dy_dn, do_sb, dy_ssm = dbr
    if early is not None:
        dy_ssm = early(g, dy_ssm)
    dy_ssd, dxs_skip, dproj, ddexp, g["ssm_norm_w"] = _ssm_post_bwd(
        sv["y_ssd"], sv["xbc"], proj, cb[5], sv["dexp"], p["ssm_norm_w"], dy_ssm, dproj, name=f"ssm_post_bwd_{tag}")
    g["ssm_d"] = ddexp.reshape(smh, SSM_HEAD_DIM).sum(axis=1)
    dxs, dbm, dcm, ddt_rows, dalog, ddtb = _ssd_bwd(
        sv["xbc"], sv["dt_rows"], sv["ssm_alog"], sv["ssm_dtb"], sv["ssm_states"], dy_ssd, name=f"ssd_bwd_{tag}")
    g["ssm_a_log"] = dalog.reshape(smh)
    g["ssm_dt_bias"] = ddtb.reshape(smh)
    dxbc_post = jnp.concatenate([dxs + dxs_skip, dbm, dcm], axis=1)
    dproj, g["ssm_conv_w"], dcb = _conv_bwd(proj, cb[6], p["ssm_conv_w"], p["ssm_conv_b"].reshape(1, -1), 0, dxbc_post,
                                            dproj, name=f"ssm_conv_bwd_{tag}")
    g["ssm_conv_b"] = dcb.reshape(-1)
    ddt = ddt_rows.transpose(0, 2, 1, 3).reshape(smh, s).T
    dqkv_sb = _sb_bwd(proj, cb[4], w, sv["sb_r"], do_sb, name=f"sb_bwd_{tag}")
    dproj = lax.dynamic_update_slice(dproj, jnp.concatenate([t.astype(MXU_DTYPE) for t in dqkv_sb], axis=1), (0, lay.cols[4]))
    do_dn, dproj, g["dn_norm_w"] = _dn_post_bwd(sv["o_dn"], proj, cb[1], p["dn_norm_w"], dy_dn, dproj,
                                                name=f"dn_post_bwd_{tag}")
    dqkv_dn, da_rows, db_rows, dal, ddtb_dn = _dn_bwd(
        sv["dn_qkv"], sv["a_rows"], sv["b_rows"], sv["dn_alog"], sv["dn_dtb"], sv["dn_states"], sv["dn_inv"], do_dn,
        name=f"dn_chunk_bwd_{tag}")
    g["dn_a_log"] = dal.reshape(dnh)
    g["dn_dt_bias"] = ddtb_dn.reshape(dnh)
    zero_b = jnp.zeros((1, 3 * w), F32)
    dproj, g["dn_conv_w"], _ = _conv_bwd(proj, cb[0], p["dn_conv_w"], zero_b, 2 * dnh, dqkv_dn, dproj,
                                         name=f"dn_conv_bwd_{tag}")
    da = da_rows.reshape(dnh, s).T
    db = db_rows.reshape(dnh, s).T
    dsmall = jnp.concatenate([da, db, ddt, jnp.zeros((s, LANES - lay.n_small), F32)], axis=1).astype(MXU_DTYPE)
    dproj = lax.dynamic_update_slice(dproj, dsmall, (0, lay.small_col))
    g["w_in"] = lay.to_shards(_matmul(sv["h1"], dproj, ta=True, name=f"mm_in_dw_{tag}", out_dtypes=(BF16,)))
    if late is not None:
        dproj = late(g, dproj)
    dh1 = _matmul(dproj, p["w_in"], tb=True, name=f"mm_in_dx_{tag}")
    dx0, g["norm_mix"] = _rms_bwd(x, p["norm_mix"], dh1, dx1, name=f"rms_mix_bwd_{tag}")
    return dx0, g


BIG = ("w_in", "w_branch", "w_out", "w_up", "w_down")
CONV = ("dn_conv_w", "ssm_conv_w")
SMALL = ("norm_mix", "dn_conv_w", "dn_a_log", "dn_dt_bias", "dn_norm_w", "ssm_conv_w", "ssm_conv_b", "ssm_a_log",
         "ssm_dt_bias", "ssm_d", "ssm_norm_w", "norm_mlp", "norm_final")
WEIGHTS = ("norm_mix", "w_in", "dn_conv_w", "dn_a_log", "dn_dt_bias", "dn_norm_w", "ssm_conv_w", "ssm_conv_b", "ssm_a_log",
           "ssm_dt_bias", "ssm_d", "ssm_norm_w", "w_branch", "w_out", "norm_mlp", "w_up", "w_down", "norm_final")
SHARD_AXIS = {"w_in": 2, "dn_conv_w": 2, "ssm_conv_w": 2, "w_branch": 2, "w_out": 1, "w_up": 2, "w_down": 1}


def _to_shards(full, axis):
    shp = full.shape
    n = shp[axis] // N_DEV
    t = full.reshape(shp[:axis] + (N_DEV, n) + shp[axis + 1:])
    return jnp.moveaxis(t, axis, 0)


def _unshard(parts, axis, *, name):
    shard = parts.shape[1:]
    nd = len(shard)
    if axis == 0:
        return parts.reshape((N_DEV * shard[0],) + shard[1:])

    def copy_block(i_ref, o_ref):
        o_ref[...] = i_ref[...]

    if axis == nd - 1:
        rows, n = math.prod(shard[:-1]), shard[-1]
        out = pl.pallas_call(
            copy_block, grid=(N_DEV,),
            in_specs=[pl.BlockSpec((None, rows, n), lambda j: (j, 0, 0))],
            out_specs=pl.BlockSpec((rows, n), lambda j: (0, j)),
            out_shape=jax.ShapeDtypeStruct((rows, N_DEV * n), parts.dtype),
            compiler_params=_cparams(1), name=name,
        )(parts.reshape(N_DEV, rows, n))
        return out.reshape(shard[:-1] + (N_DEV * n,))
    assert axis == nd - 2, (parts.shape, axis)
    a, n, c = math.prod(shard[:-2]), shard[-2], shard[-1]
    out = pl.pallas_call(
        copy_block, grid=(N_DEV, a),
        in_specs=[pl.BlockSpec((None, None, n, c), lambda j, i: (j, i, 0, 0))],
        out_specs=pl.BlockSpec((None, n, c), lambda j, i: (i, j, 0)),
        out_shape=jax.ShapeDtypeStruct((a, N_DEV * n, c), parts.dtype),
        compiler_params=_cparams(2), name=name,
    )(parts.reshape(N_DEV, a, n, c))
    return out.reshape(shard[:-2] + (N_DEV * n, c))


def _step(w, m, v, x, target):
    s, d = x.shape
    lay = _Layout(d)
    me = 4 * lax.axis_index("x") + 2 * lax.axis_index("y") + lax.axis_index("c")

    def shard(n, l):
        return w[n][l].astype(BF16) if n in BIG else w[n][l]

    def empty_land(a):
        return lax.empty((N_DEV,) + a.shape, a.dtype)

    def with_own(land, own):
        return lax.dynamic_update_index_in_dim(land, own, me, 0)

    def assemble(n, parts, l):
        return lay.from_shards(parts) if n == "w_in" else _unshard(parts, SHARD_AXIS[n] - 1, name=f"unshard_{n}_l{l}")

    small_names = tuple(n for n in WEIGHTS if n not in BIG + CONV + ("norm_final",))

    first, rest = ("w_in",) + CONV, BIG[1:]
    got = _all_gather([shard(n, 0) for n in first], name="gather_l0_first")
    whole, sliced = (True, None), (False, None)
    names_a, names_b = rest, BIG + CONV
    srcs_a, srcs_b = [shard(n, 0) for n in names_a], [shard(n, 1) for n in names_b]
    sem_sa, sem_ra, srcs_a, lands_a, w_in0 = _split_start(
        srcs_a, [empty_land(a) for a in srcs_a], [whole] * len(srcs_a), got[0], name="gather_l0_rest_start")
    sem_sb, sem_rb, srcs_b, lands_b, w_in0 = _split_start(
        srcs_b, [empty_land(a) for a in srcs_b], [whole] * len(srcs_b), w_in0, name="gather_l1_start")
    p0 = {n: w[n][0] for n in small_names}
    p0.update({n: assemble(n, g, 0) for n, g in zip(first, [w_in0] + list(got[1:]))})

    def late_l0(after):
        lands = _split_wait(sem_sa, sem_ra, srcs_a, lands_a, [whole] * len(srcs_a), after, name="gather_l0_rest_wait")
        return {n: assemble(n, with_own(ld, s_), 0) for n, ld, s_ in zip(names_a, lands, srcs_a)}

    h, sv0 = _layer_fwd(x, p0, lay, "l0", late=late_l0)
    lands = _split_wait(sem_sb, sem_rb, srcs_b, lands_b, [whole] * len(srcs_b), h, name="gather_l1_wait")
    p1 = {n: w[n][1] for n in small_names}
    p1.update({n: assemble(n, with_own(ld, s_), 1) for n, ld, s_ in zip(names_b, lands, srcs_b)})
    h, sv1 = _layer_fwd(h, p1, lay, "l1")
    loss, dh, g_norm_final = _final_loss(h, w["norm_final"], target, name="final_loss")
    grads = [None] * DEPTH
    dh, grads[1] = _layer_bwd(dh, p1, sv1, lay, "l1")

    def exchange_start(names, g, carry, tag):
        srcs = [g[n] for n in names]
        return _split_start(srcs, [lax.empty(a.shape, a.dtype) for a in srcs], [sliced] * len(srcs), carry,
                            name=f"grad_{tag}_start")

    def exchange_wait(names, started, after, tag):
        sem_s, sem_r, srcs, lands_, _ = started
        lands_ = _split_wait(sem_s, sem_r, srcs, lands_, [sliced] * len(srcs), after, name=f"grad_{tag}_wait")
        return {n: with_own(ld, lax.dynamic_index_in_dim(s_, me, 0, keepdims=False)) for n, ld, s_ in zip(names, lands_, srcs)}

    x1_started = exchange_start(BIG, grads[1], dh, "l1")
    pending = {}

    def early_l0(g, carry):
        pending["rest"] = exchange_start(rest, g, carry, "l0_rest")
        return pending["rest"][4]

    def late_bwd_l0(g, carry):
        pending["w_in"] = exchange_start(("w_in",), g, carry, "l0_w_in")
        return pending["w_in"][4]

    grad_x, grads[0] = _layer_bwd(x1_started[4], p0, sv0, lay, "l0", early=early_l0, late=late_bwd_l0)

    out = {"grad": {}, "delta": {}, "new_m": {}, "new_v": {}}
    parts1 = exchange_wait(BIG, x1_started, grad_x, "l1")
    res1 = {n: _sum_adamw(parts1[n], w[n], m[n], v[n], 1, None, name=f"sum_adamw_{n}_l1") for n in BIG}
    parts0 = exchange_wait(rest, pending["rest"], res1["w_in"][0], "l0_rest")
    res0 = {n: _sum_adamw(parts0[n], w[n], m[n], v[n], 0, res1[n], name=f"sum_adamw_{n}_l0") for n in rest}
    parts0 = exchange_wait(("w_in",), pending["w_in"], res0["w_down"][0], "l0_w_in")
    res0["w_in"] = _sum_adamw(parts0["w_in"], w["w_in"], m["w_in"], v["w_in"], 0, res1["w_in"], name="sum_adamw_w_in_l0")
    for n in BIG:
        for key, a in zip(("grad", "delta", "new_m", "new_v"), res0[n]):
            out[key][n] = a

    gfull = {n: jnp.stack([grads[l][n] for l in range(DEPTH)]) for n in SMALL if n != "norm_final"}
    gfull["norm_final"] = g_norm_final
    small_send = _pack([gfull[n] for n in SMALL] + [loss.reshape(1)], F32)
    small_recv = _all_gather([small_send], name="gather_small_grads", after=res0["w_in"][0])[0]
    small_sum = _sum_parts(small_recv, name="sum_small")
    small_full = _unpack(small_sum, [gfull[n].shape for n in SMALL] + [(1,)])
    loss_total = small_full[-1][0]
    gsmall = {}
    for n, a in zip(SMALL, small_full[:-1]):
        if n in SHARD_AXIS:
            a = lax.dynamic_index_in_dim(_to_shards(a, SHARD_AXIS[n]), me, axis=0, keepdims=False)
        gsmall[n] = a
    small_shapes = [w[n].shape for n in SMALL]
    ws, gs, ms, vs = (_pack([t[n] for n in SMALL], F32) for t in (w, gsmall, m, v))
    ds, m1s, v1s = _adamw(ws, gs, ms, vs, name="adamw_small")
    for n in SMALL:
        out["grad"][n] = gsmall[n]
    for key, packed in (("delta", ds), ("new_m", m1s), ("new_v", v1s)):
        for n, a in zip(SMALL, _unpack(packed, small_shapes)):
            out[key][n] = a
    return loss_total, grad_x, out


def kernel(x, norm_mix, w_in, dn_conv_w, dn_a_log, dn_dt_bias, dn_norm_w, ssm_conv_w, ssm_conv_b, ssm_a_log, ssm_dt_bias, ssm_d, ssm_norm_w, w_branch, w_out, norm_mlp, w_up, w_down, norm_final, loss_target, m_norm_mix, m_w_in, m_dn_conv_w, m_dn_a_log, m_dn_dt_bias, m_dn_norm_w, m_ssm_conv_w, m_ssm_conv_b, m_ssm_a_log, m_ssm_dt_bias, m_ssm_d, m_ssm_norm_w, m_w_branch, m_w_out, m_norm_mlp, m_w_up, m_w_down, m_norm_final, v_norm_mix, v_w_in, v_dn_conv_w, v_dn_a_log, v_dn_dt_bias, v_dn_norm_w, v_ssm_conv_w, v_ssm_conv_b, v_ssm_a_log, v_ssm_dt_bias, v_ssm_d, v_ssm_norm_w, v_w_branch, v_w_out, v_norm_mlp, v_w_up, v_w_down, v_norm_final):
    w = dict(norm_mix=norm_mix, w_in=w_in, dn_conv_w=dn_conv_w, dn_a_log=dn_a_log, dn_dt_bias=dn_dt_bias, dn_norm_w=dn_norm_w,
             ssm_conv_w=ssm_conv_w, ssm_conv_b=ssm_conv_b, ssm_a_log=ssm_a_log, ssm_dt_bias=ssm_dt_bias, ssm_d=ssm_d,
             ssm_norm_w=ssm_norm_w, w_branch=w_branch, w_out=w_out, norm_mlp=norm_mlp, w_up=w_up, w_down=w_down,
             norm_final=norm_final)
    m = dict(norm_mix=m_norm_mix, w_in=m_w_in, dn_conv_w=m_dn_conv_w, dn_a_log=m_dn_a_log, dn_dt_bias=m_dn_dt_bias,
             dn_norm_w=m_dn_norm_w, ssm_conv_w=m_ssm_conv_w, ssm_conv_b=m_ssm_conv_b, ssm_a_log=m_ssm_a_log,
             ssm_dt_bias=m_ssm_dt_bias, ssm_d=m_ssm_d, ssm_norm_w=m_ssm_norm_w, w_branch=m_w_branch, w_out=m_w_out,
             norm_mlp=m_norm_mlp, w_up=m_w_up, w_down=m_w_down, norm_final=m_norm_final)
    v = dict(norm_mix=v_norm_mix, w_in=v_w_in, dn_conv_w=v_dn_conv_w, dn_a_log=v_dn_a_log, dn_dt_bias=v_dn_dt_bias,
             dn_norm_w=v_dn_norm_w, ssm_conv_w=v_ssm_conv_w, ssm_conv_b=v_ssm_conv_b, ssm_a_log=v_ssm_a_log,
             ssm_dt_bias=v_ssm_dt_bias, ssm_d=v_ssm_d, ssm_norm_w=v_ssm_norm_w, w_branch=v_w_branch, w_out=v_w_out,
             norm_mlp=v_norm_mlp, w_up=v_w_up, w_down=v_w_down, norm_final=v_norm_final)
    loss, grad_x, out = _step(w, m, v, x[0], loss_target[0])
    return (loss, grad_x[None], *[out["grad"][n] for n in WEIGHTS], *[out["delta"][n] for n in WEIGHTS],
            *[out["new_m"][n] for n in WEIGHTS], *[out["new_v"][n] for n in WEIGHTS])
```

```python
import math

import jax
import jax.numpy as jnp
from jax import lax
from jax.experimental import pallas as pl
from jax.experimental.pallas import tpu as pltpu

F32 = jnp.float32
BF16 = jnp.bfloat16
MXU_DTYPE = BF16
HIGHEST = lax.Precision.HIGHEST

N_DEV = 8
DEPTH = 2
EPS = 1e-6
CONV_K = 4
DN_HEAD_DIM = 128
SB_HEAD_DIM = 64
SSM_HEAD_DIM = 64
SSM_STATE = 128
SSM_GROUPS = 4
CHUNK = 64
SB_BLOCK = 128
LANES = 128
ADAM_LR, ADAM_B1, ADAM_B2, ADAM_EPS, ADAM_WD, ADAM_STEP = 0.001, 0.9, 0.999, 1e-08, 0.01, 10
NEG_BIG = -1e30
DN_HEADS_PER_STEP = 8
SSD_GROUPS_PER_STEP = 1
SB_UNROLL = 4
SB_SPLIT = 2
CHUNK_PREC = lax.Precision.HIGH

ARB = "arbitrary"


def _cparams(n_axes):
    return pltpu.CompilerParams(dimension_semantics=(ARB,) * n_axes)


def _softplus(x):
    return jnp.maximum(x, 0.0) + jnp.log1p(jnp.exp(-jnp.abs(x)))


def _sigmoid(x):
    return jax.nn.sigmoid(x)


def _silu(x):
    return x * _sigmoid(x)


def _silu_and_grad(x):
    s = _sigmoid(x)
    return x * s, s * (1.0 + x * (1.0 - s))


def _dot(a, b, dims, prec=None):
    return lax.dot_general(a, b, (dims, ((), ())), precision=prec, preferred_element_type=F32)


NN = ((1,), (0,))
NT = ((1,), (1,))
TN = ((0,), (0,))


def _hdot(a, b, dims=NN):
    return _dot(a, b, dims, CHUNK_PREC)


def _mxu_dot(a, b, dims):
    return _dot(a.astype(MXU_DTYPE), b.astype(MXU_DTYPE), dims)


def _single_pass_dot(dims):
    grads = {NN: (lambda a, b, ct: (_mxu_dot(ct, b, NT), _mxu_dot(a, ct, TN))),
             NT: (lambda a, b, ct: (_mxu_dot(ct, b, NN), _mxu_dot(ct, a, TN))),
             TN: (lambda a, b, ct: (_mxu_dot(b, ct, NT), _mxu_dot(a, ct, NN)))}[dims]

    @jax.custom_vjp
    def f(a, b):
        return _mxu_dot(a, b, dims)

    f.defvjp(lambda a, b: (_mxu_dot(a, b, dims), (a, b)), lambda res, ct: grads(*res, ct))
    return f


_SDOT = {dims: _single_pass_dot(dims) for dims in (NN, NT, TN)}


def _sdot(a, b, dims=NN):
    return _SDOT[dims](a, b)


def _split_dot(a, m_bf16, nsplit=3):
    out = None
    rem = a
    for _ in range(nsplit):
        piece = rem.astype(BF16)
        rem = rem - piece.astype(F32)
        term = _dot(piece, m_bf16, NN)
        out = term if out is None else out + term
    return out


def _pick(n, pref):
    for t in pref:
        if n % t == 0:
            return t
    return n


def _matmul(a, b, *, ta=False, tb=False, name, epilogue=None, extras=(), out_dtypes=(F32,), col_shards=1, into=None,
            tm=None, tn=None, tk=None):
    m, k = (a.shape[1], a.shape[0]) if ta else a.shape
    k2, n = (b.shape[1], b.shape[0]) if tb else b.shape
    assert k == k2, (a.shape, b.shape, ta, tb)
    ncs = n // col_shards
    tm = tm or _pick(m, (1920, 1024, 512, 256, 128))
    tn = tn or _pick(ncs, (1920, 1024, 640, 512, 384, 256, 128))
    tk = tk or _pick(k, (1920, 1024, 640, 512, 256, 128))
    nk = k // tk
    a_spec = pl.BlockSpec((tk, tm), lambda i, j, kk: (kk, i)) if ta else pl.BlockSpec((tm, tk), lambda i, j, kk: (i, kk))
    b_spec = pl.BlockSpec((tn, tk), lambda i, j, kk: (j, kk)) if tb else pl.BlockSpec((tk, tn), lambda i, j, kk: (kk, j))
    e_spec = pl.BlockSpec((tm, tn), lambda i, j, kk: (i, j))
    if into is not None:
        buf, col_off = into
        off = col_off // tn
        assert col_shards == 1 and len(out_dtypes) == 1 and col_off % tn == 0 and out_dtypes[0] == buf.dtype
        o_spec, o_shape = pl.BlockSpec((tm, tn), lambda i, j, kk: (i, off + j)), buf.shape
    elif col_shards == 1:
        o_spec, o_shape = e_spec, (m, n)
    else:
        per = ncs // tn
        o_spec, o_shape = pl.BlockSpec((None, tm, tn), lambda i, j, kk: (j // per, i, j % per)), (col_shards, m, ncs)
    dims = (((0,) if ta else (1,)), ((1,) if tb else (0,)))
    n_extra = len(extras)
    n_out = len(out_dtypes)
    n_into = 0 if into is None else 1

    def body(*refs):
        a_ref, b_ref = refs[0], refs[1]
        extra_refs = refs[2:2 + n_extra]
        out_refs = refs[2 + n_extra + n_into:2 + n_extra + n_into + n_out]
        acc_ref = refs[-1]
        kk = pl.program_id(2)

        @pl.when(kk == 0)
        def _():
            acc_ref[...] = jnp.zeros_like(acc_ref)

        acc_ref[...] += _dot(a_ref[...].astype(MXU_DTYPE), b_ref[...].astype(MXU_DTYPE), dims)

        @pl.when(kk == nk - 1)
        def _():
            acc = acc_ref[...]
            outs = (acc,) if epilogue is None else epilogue(acc, *[r[...] for r in extra_refs])
            for o_ref, o in zip(out_refs, outs):
                o_ref[...] = o.astype(o_ref.dtype)

    outs = pl.pallas_call(
        body,
        grid=(m // tm, n // tn, nk),
        in_specs=[a_spec, b_spec] + [e_spec] * n_extra + [ANY] * n_into,
        out_specs=[o_spec] * n_out,
        out_shape=[jax.ShapeDtypeStruct(o_shape, dt) for dt in out_dtypes],
        input_output_aliases={2 + n_extra: 0} if n_into else {},
        scratch_shapes=[pltpu.VMEM((tm, tn), F32)],
        compiler_params=pltpu.CompilerParams(dimension_semantics=("parallel", "parallel", ARB)),
        name=name,
    )(a, b, *extras, *([] if into is None else [into[0]]))
    return outs[0] if n_out == 1 else tuple(outs)


def _rms_fwd(x, w, *, name, tm=512):
    s, d = x.shape
    out_dtype = MXU_DTYPE

    def body(x_ref, w_ref, o_ref):
        xv = x_ref[...]
        r = lax.rsqrt(jnp.mean(xv * xv, axis=-1, keepdims=True) + EPS)
        o_ref[...] = (xv * r * w_ref[...]).astype(o_ref.dtype)

    return pl.pallas_call(
        body, grid=(s // tm,),
        in_specs=[pl.BlockSpec((tm, d), lambda i: (i, 0)), pl.BlockSpec((1, d), lambda i: (0, 0))],
        out_specs=pl.BlockSpec((tm, d), lambda i: (i, 0)),
        out_shape=jax.ShapeDtypeStruct((s, d), out_dtype),
        compiler_params=_cparams(1), name=name,
    )(x, w.reshape(1, d))


def _rms_bwd(x, w, dh, dres, *, name, tm=512):
    s, d = x.shape

    def body(x_ref, w_ref, dh_ref, dres_ref, dx_ref, dw_ref):
        xv = x_ref[...]
        r = lax.rsqrt(jnp.mean(xv * xv, axis=-1, keepdims=True) + EPS)
        xh = xv * r
        dhv = dh_ref[...].astype(F32)
        dxn = dhv * w_ref[...]
        dx = r * (dxn - xh * jnp.mean(dxn * xh, axis=-1, keepdims=True))
        dx_ref[...] = dres_ref[...] + dx

        @pl.when(pl.program_id(0) == 0)
        def _():
            dw_ref[...] = jnp.zeros_like(dw_ref)

        dw_ref[...] += jnp.sum(dhv * xh, axis=0, keepdims=True)

    dx, dw = pl.pallas_call(
        body, grid=(s // tm,),
        in_specs=[pl.BlockSpec((tm, d), lambda i: (i, 0)), pl.BlockSpec((1, d), lambda i: (0, 0)),
                  pl.BlockSpec((tm, d), lambda i: (i, 0)), pl.BlockSpec((tm, d), lambda i: (i, 0))],
        out_specs=[pl.BlockSpec((tm, d), lambda i: (i, 0)), pl.BlockSpec((1, d), lambda i: (0, 0))],
        out_shape=[jax.ShapeDtypeStruct((s, d), F32), jax.ShapeDtypeStruct((1, d), F32)],
        compiler_params=_cparams(1), name=name,
    )(x, w.reshape(1, d), dh, dres)
    return dx, dw.reshape(d)


def _final_loss(x, w, target, *, name, tm=512):
    s, d = x.shape

    def body(x_ref, w_ref, t_ref, loss_ref, dx_ref, dw_ref):
        xv = x_ref[...]
        r = lax.rsqrt(jnp.mean(xv * xv, axis=-1, keepdims=True) + EPS)
        xh = xv * r
        err = xh * w_ref[...] - t_ref[...]
        dy = err * (1.0 / d)
        dxn = dy * w_ref[...]
        dx_ref[...] = r * (dxn - xh * jnp.mean(dxn * xh, axis=-1, keepdims=True))

        @pl.when(pl.program_id(0) == 0)
        def _():
            dw_ref[...] = jnp.zeros_like(dw_ref)
            loss_ref[...] = jnp.zeros_like(loss_ref)

        dw_ref[...] += jnp.sum(dy * xh, axis=0, keepdims=True)
        row = jnp.sum(err * err, axis=1, keepdims=True) * (0.5 / d)
        loss_ref[...] += jnp.sum(row, axis=0, keepdims=True)

    loss, dx, dw = pl.pallas_call(
        body, grid=(s // tm,),
        in_specs=[pl.BlockSpec((tm, d), lambda i: (i, 0)), pl.BlockSpec((1, d), lambda i: (0, 0)),
                  pl.BlockSpec((tm, d), lambda i: (i, 0))],
        out_specs=[pl.BlockSpec((1, 1), lambda i: (0, 0)), pl.BlockSpec((tm, d), lambda i: (i, 0)),
                   pl.BlockSpec((1, d), lambda i: (0, 0))],
        out_shape=[jax.ShapeDtypeStruct((1, 1), F32), jax.ShapeDtypeStruct((s, d), F32), jax.ShapeDtypeStruct((1, d), F32)],
        compiler_params=_cparams(1), name=name,
    )(x, w.reshape(1, d), target)
    return loss[0, 0], dx, dw.reshape(d)


def _shift_down(x, sh, t_idx):
    return jnp.where(t_idx >= sh, pltpu.roll(x, sh, 0), 0.0)


def _shift_up(x, sh, t_idx, s):
    return jnp.where(t_idx < s - sh, pltpu.roll(x, s - sh, 0), 0.0)


def _conv_pre(x, w_rows, b, t_idx):
    c = w_rows[CONV_K - 1] * x + b
    for sh in range(1, CONV_K):
        c = c + w_rows[CONV_K - 1 - sh] * _shift_down(x, sh, t_idx)
    return c


def _conv_fwd(src, col0, w, b, n_l2, *, name):
    s = src.shape[0]
    c_tot = w.shape[1]
    nblk = c_tot // LANES

    def body(x_ref, w_ref, b_ref, o_ref):
        j = pl.program_id(0)
        t_idx = lax.broadcasted_iota(jnp.int32, (s, LANES), 0)
        w_rows = [w_ref[kk:kk + 1, :] for kk in range(CONV_K)]
        y = _silu(_conv_pre(x_ref[...], w_rows, b_ref[...], t_idx))
        if n_l2 > 0:
            yn = y * lax.rsqrt(jnp.sum(y * y, axis=1, keepdims=True) + EPS)
            y = jnp.where(j < n_l2, yn, y)
        o_ref[...] = y

    return pl.pallas_call(
        body, grid=(nblk,),
        in_specs=[pl.BlockSpec((s, LANES), lambda j: (0, col0 + j)), pl.BlockSpec((CONV_K, LANES), lambda j: (0, j)),
                  pl.BlockSpec((1, LANES), lambda j: (0, j))],
        out_specs=pl.BlockSpec((s, LANES), lambda j: (0, j)),
        out_shape=jax.ShapeDtypeStruct((s, c_tot), F32),
        compiler_params=_cparams(1), name=name,
    )(src, w, b)


def _conv_bwd(src, col0, w, b, n_l2, dout, into, *, name):
    s = src.shape[0]
    c_tot = w.shape[1]
    nblk = c_tot // LANES

    def body(x_ref, w_ref, b_ref, do_ref, into_ref, dx_ref, dw_ref, db_ref):
        j = pl.program_id(0)
        t_idx = lax.broadcasted_iota(jnp.int32, (s, LANES), 0)
        xv = x_ref[...]
        w_rows = [w_ref[kk:kk + 1, :] for kk in range(CONV_K)]
        c = _conv_pre(xv, w_rows, b_ref[...], t_idx)
        dy = do_ref[...]
        y, y_grad = _silu_and_grad(c)
        if n_l2 > 0:
            r = lax.rsqrt(jnp.sum(y * y, axis=1, keepdims=True) + EPS)
            dyn = r * dy - y * (r * r * r) * jnp.sum(dy * y, axis=1, keepdims=True)
            dy = jnp.where(j < n_l2, dyn, dy)
        dc = dy * y_grad
        dx = w_rows[CONV_K - 1] * dc
        rows = [None] * CONV_K
        rows[CONV_K - 1] = jnp.sum(dc * xv, axis=0, keepdims=True)
        for sh in range(1, CONV_K):
            dx = dx + w_rows[CONV_K - 1 - sh] * _shift_up(dc, sh, t_idx, s)
            rows[CONV_K - 1 - sh] = jnp.sum(dc * _shift_down(xv, sh, t_idx), axis=0, keepdims=True)
        dx_ref[...] = dx.astype(dx_ref.dtype)
        for kk in range(CONV_K):
            dw_ref[kk:kk + 1, :] = rows[kk]
        db_ref[...] = jnp.sum(dc, axis=0, keepdims=True)

    return pl.pallas_call(
        body, grid=(nblk,),
        in_specs=[pl.BlockSpec((s, LANES), lambda j: (0, col0 + j)), pl.BlockSpec((CONV_K, LANES), lambda j: (0, j)),
                  pl.BlockSpec((1, LANES), lambda j: (0, j)), pl.BlockSpec((s, LANES), lambda j: (0, j)), ANY],
        out_specs=[pl.BlockSpec((s, LANES), lambda j: (0, col0 + j)), pl.BlockSpec((CONV_K, LANES), lambda j: (0, j)),
                   pl.BlockSpec((1, LANES), lambda j: (0, j))],
        out_shape=[jax.ShapeDtypeStruct(into.shape, into.dtype), jax.ShapeDtypeStruct((CONV_K, c_tot), F32),
                   jax.ShapeDtypeStruct((1, c_tot), F32)],
        input_output_aliases={4: 0},
        compiler_params=_cparams(1), name=name,
    )(src, w, b, dout, into)


def _chunk_masks(c):
    ii = lax.broadcasted_iota(jnp.int32, (c, c), 0)
    jj = lax.broadcasted_iota(jnp.int32, (c, c), 1)
    return ii, jj


def _row_to_col(row, eye):
    return jnp.sum(jnp.where(eye, row, 0.0), axis=1, keepdims=True)


def _each(f, *lists):
    return [f(*xs) for xs in zip(*lists)]


@jax.custom_vjp
def _nilpotent_inverse(nmats):
    c = nmats[0].shape[0]
    ii, jj = _chunk_masks(c)
    xinv = _each(lambda n: jnp.where(ii == jj, 1.0, 0.0) + n, nmats)
    pw = nmats
    for _ in range(int(math.log2(c)) - 1):
        pw = _each(lambda p: _dot(p, p, NN, HIGHEST), pw)
        xinv = _each(lambda x, p: x + _dot(x, p, NN, HIGHEST), xinv, pw)
    return xinv


def _nilpotent_inverse_fwd(nmats):
    xinv = _nilpotent_inverse(nmats)
    return xinv, xinv


def _nilpotent_inverse_bwd(xinv, cts):
    left = _each(lambda x, ct: _dot(x, ct, TN, HIGHEST), xinv, cts)
    return (_each(lambda l_, x: _dot(l_, x, NT, HIGHEST), left, xinv),)


_nilpotent_inverse.defvjp(_nilpotent_inverse_fwd, _nilpotent_inverse_bwd)


@jax.custom_vjp
def _saved_inverse(nmats, saved):
    return saved


def _saved_inverse_fwd(nmats, saved):
    return saved, saved


def _saved_inverse_bwd(xinv, cts):
    return _nilpotent_inverse_bwd(xinv, cts) + (_each(jnp.zeros_like, xinv),)


_saved_inverse.defvjp(_saved_inverse_fwd, _saved_inverse_bwd)


def _dn_chunk(q, k, v, a_row, b_row, alog, dtb, s0, saved_inverse=None):
    c = q[0].shape[0]
    ii, jj = _chunk_masks(c)
    causal, strict, eye = ii >= jj, ii > jj, ii == jj
    g_row = _each(lambda al, a, dt: -jnp.exp(al) * _softplus(a + dt), alog, a_row, dtb)
    beta_col = _each(lambda b: _row_to_col(_sigmoid(b), eye), b_row)
    g_col = _each(lambda g: _row_to_col(g, eye), g_row)
    gc_col = _each(lambda g: jnp.sum(jnp.where(causal, g, 0.0), axis=1, keepdims=True), g_row)
    gc_row = _each(lambda g: jnp.sum(jnp.where(jj >= ii, g, 0.0), axis=0, keepdims=True), g_col)
    decay = _each(lambda gc, gr: jnp.exp(jnp.where(causal, gc - gr, NEG_BIG)), gc_col, gc_row)
    kb = _each(jnp.multiply, k, beta_col)
    vb = _each(jnp.multiply, v, beta_col)
    nmat = _each(lambda kb_, k_, dc: -jnp.where(strict, _dot(kb_, k_, NT, HIGHEST) * dc, 0.0), kb, k, decay)
    xinv = _nilpotent_inverse(nmat) if saved_inverse is None else _saved_inverse(nmat, saved_inverse)
    egc = _each(jnp.exp, gc_col)
    dv = v[0].shape[1]
    uw = _each(lambda x, vb_, kb_, e: _dot(x, jnp.concatenate([vb_, kb_ * e], axis=1), NN, HIGHEST), xinv, vb, kb, egc)
    u = _each(lambda t: t[:, :dv], uw)
    w = _each(lambda t: t[:, dv:], uw)
    qs = _each(lambda q_: q_ * (q_.shape[1] ** -0.5), q)
    attn = _each(lambda q_, k_, dc: _sdot(q_, k_, NT) * dc, qs, k, decay)
    gl = _each(lambda g: jnp.sum(g, axis=1, keepdims=True), g_row)
    kd = _each(lambda k_, gl_, gc: k_ * jnp.exp(gl_ - gc), k, gl, gc_col)
    v_new = _each(lambda u_, w_, s: u_ - _sdot(w_, s), u, w, s0)
    o = _each(lambda q_, e, s, at, vn: _sdot(q_ * e, s) + _sdot(at, vn), qs, egc, s0, attn, v_new)
    s1 = _each(lambda s, gl_, kd_, vn: s * jnp.exp(gl_) + _sdot(kd_, vn, TN), s0, gl, kd, v_new)
    return (o, s1), xinv


def _dn_specs(nh, nc, hb, rev):
    n_of = (lambda n: nc - 1 - n) if rev else (lambda n: n)
    ng = nh // hb
    qkv = [pl.BlockSpec((CHUNK, hb * DN_HEAD_DIM), (lambda h, n, o=o: (n_of(n), o * ng + h))) for o in range(3)]
    row = pl.BlockSpec((hb, None, 1, CHUNK), lambda h, n: (h, n_of(n), 0, 0))
    scal = pl.BlockSpec((hb, 1, 1), lambda h, n: (h, 0, 0))
    o_spec = pl.BlockSpec((CHUNK, hb * DN_HEAD_DIM), lambda h, n: (n_of(n), h))
    st = pl.BlockSpec((hb, None, DN_HEAD_DIM, DN_HEAD_DIM), lambda h, n: (h, n_of(n), 0, 0))
    inv = pl.BlockSpec((hb, None, CHUNK, CHUNK), lambda h, n: (h, n_of(n), 0, 0))
    return qkv, row, scal, o_spec, st, inv


def _dn_fwd(qkv, a_rows, b_rows, alog, dtb, *, name):
    s = qkv.shape[0]
    nh, nc = a_rows.shape[0], a_rows.shape[1]
    hb = min(DN_HEADS_PER_STEP, nh)
    qkv_specs, row, scal, o_spec, st, inv = _dn_specs(nh, nc, hb, False)
    hd = DN_HEAD_DIM

    def body(q_ref, k_ref, v_ref, a_ref, b_ref, al_ref, dt_ref, o_ref, st_ref, inv_ref, state):
        @pl.when(pl.program_id(1) == 0)
        def _():
            state[...] = jnp.zeros_like(state)

        cols = [slice(h * hd, (h + 1) * hd) for h in range(hb)]
        s0 = [state[h] for h in range(hb)]
        for h in range(hb):
            st_ref[h] = s0[h]
        (o, s1), xinv = _dn_chunk(
            [q_ref[:, cl] for cl in cols], [k_ref[:, cl] for cl in cols], [v_ref[:, cl] for cl in cols],
            [a_ref[h] for h in range(hb)], [b_ref[h] for h in range(hb)],
            [al_ref[h] for h in range(hb)], [dt_ref[h] for h in range(hb)], s0)
        for h in range(hb):
            o_ref[:, cols[h]] = o[h]
            inv_ref[h] = xinv[h]
            state[h] = s1[h]

    return pl.pallas_call(
        body, grid=(nh // hb, nc),
        in_specs=qkv_specs + [row, row, scal, scal],
        out_specs=[o_spec, st, inv],
        out_shape=[jax.ShapeDtypeStruct((s, nh * hd), F32), jax.ShapeDtypeStruct((nh, nc, hd, hd), F32),
                   jax.ShapeDtypeStruct((nh, nc, CHUNK, CHUNK), F32)],
        scratch_shapes=[pltpu.VMEM((hb, hd, hd), F32)],
        compiler_params=_cparams(2), name=name,
    )(qkv, qkv, qkv, a_rows, b_rows, alog, dtb)


def _dn_bwd(qkv, a_rows, b_rows, alog, dtb, states, inverses, do, *, name):
    s = qkv.shape[0]
    nh, nc = a_rows.shape[0], a_rows.shape[1]
    hb = min(DN_HEADS_PER_STEP, nh)
    qkv_specs, row, scal, o_spec, st, inv = _dn_specs(nh, nc, hb, True)
    hd = DN_HEAD_DIM

    assert hb == nh, "dq | dk | dv are written as one [S, 3W] array: all heads in one grid step"
    w = nh * hd

    def body(q_ref, k_ref, v_ref, a_ref, b_ref, al_ref, dt_ref, st_ref, inv_ref, do_ref,
             dqkv_ref, da_ref, db_ref, dal_ref, ddt_ref, dstate):
        @pl.when(pl.program_id(1) == 0)
        def _():
            dstate[...] = jnp.zeros_like(dstate)
            dal_ref[...] = jnp.zeros_like(dal_ref)
            ddt_ref[...] = jnp.zeros_like(ddt_ref)

        cols = [slice(h * hd, (h + 1) * hd) for h in range(hb)]
        heads = range(hb)
        args = ([q_ref[:, cl] for cl in cols], [k_ref[:, cl] for cl in cols], [v_ref[:, cl] for cl in cols],
                [a_ref[h] for h in heads], [b_ref[h] for h in heads], [al_ref[h] for h in heads],
                [dt_ref[h] for h in heads], [st_ref[h] for h in heads])
        saved = [inv_ref[h] for h in heads]
        _, vjp, _ = jax.vjp(lambda *a: _dn_chunk(*a, saved_inverse=saved), *args, has_aux=True)
        dq, dk, dv, da, db, dal, ddt, ds0 = vjp(([do_ref[:, cl] for cl in cols], [dstate[h] for h in heads]))
        for h in heads:
            dqkv_ref[:, h * hd:(h + 1) * hd] = dq[h]
            dqkv_ref[:, w + h * hd:w + (h + 1) * hd] = dk[h]
            dqkv_ref[:, 2 * w + h * hd:2 * w + (h + 1) * hd] = dv[h]
            da_ref[h] = da[h]
            db_ref[h] = db[h]
            dal_ref[h] += dal[h]
            ddt_ref[h] += ddt[h]
            dstate[h] = ds0[h]

    n_of = lambda n: nc - 1 - n
    outs = pl.pallas_call(
        body, grid=(nh // hb, nc),
        in_specs=qkv_specs + [row, row, scal, scal, st, inv, o_spec],
        out_specs=[pl.BlockSpec((CHUNK, 3 * w), lambda h, n: (n_of(n), 0)), row, row, scal, scal],
        out_shape=[jax.ShapeDtypeStruct((s, 3 * w), F32)]
        + [jax.ShapeDtypeStruct(a_rows.shape, F32)] * 2 + [jax.ShapeDtypeStruct((nh, 1, 1), F32)] * 2,
        scratch_shapes=[pltpu.VMEM((hb, hd, hd), F32)],
        compiler_params=_cparams(2), name=name,
    )(qkv, qkv, qkv, a_rows, b_rows, alog, dtb, states, inverses, do)
    return outs


def _dn_post_fwd(o, src, gate_col0, nw, *, name, tm=512):
    s, w = o.shape
    hd = DN_HEAD_DIM
    gc = gate_col0 * LANES // w

    def body(o_ref, g_ref, w_ref, y_ref):
        for h in range(w // hd):
            cols = slice(h * hd, (h + 1) * hd)
            ov = o_ref[:, cols]
            r = lax.rsqrt(jnp.mean(ov * ov, axis=-1, keepdims=True) + EPS)
            y_ref[:, cols] = (ov * r * w_ref[...] * _silu(g_ref[:, cols])).astype(y_ref.dtype)

    blk = pl.BlockSpec((tm, w), lambda i: (i, 0))
    return pl.pallas_call(
        body, grid=(s // tm,),
        in_specs=[blk, pl.BlockSpec((tm, w), lambda i: (i, gc)), pl.BlockSpec((1, hd), lambda i: (0, 0))],
        out_specs=blk, out_shape=jax.ShapeDtypeStruct((s, w), MXU_DTYPE),
        compiler_params=_cparams(1), name=name,
    )(o, src, nw.reshape(1, hd))


def _dn_post_bwd(o, src, gate_col0, nw, dy, into, *, name, tm=512):
    s, w = o.shape
    hd = DN_HEAD_DIM
    gc = gate_col0 * LANES // w

    def body(o_ref, g_ref, w_ref, dy_ref, into_ref, do_ref, dg_ref, dw_ref):
        @pl.when(pl.program_id(0) == 0)
        def _():
            dw_ref[...] = jnp.zeros_like(dw_ref)

        dw = jnp.zeros((1, hd), F32)
        for h in range(w // hd):
            cols = slice(h * hd, (h + 1) * hd)
            ov, gv, dyv = o_ref[:, cols], g_ref[:, cols], dy_ref[:, cols]
            r = lax.rsqrt(jnp.mean(ov * ov, axis=-1, keepdims=True) + EPS)
            oh = ov * r
            sg, sg_grad = _silu_and_grad(gv)
            dn = dyv * sg
            dg_ref[:, cols] = (dyv * (oh * w_ref[...]) * sg_grad).astype(dg_ref.dtype)
            don = dn * w_ref[...]
            do_ref[:, cols] = r * (don - oh * jnp.mean(don * oh, axis=-1, keepdims=True))
            dw = dw + jnp.sum(dn * oh, axis=0, keepdims=True)
        dw_ref[...] += dw

    blk = pl.BlockSpec((tm, w), lambda i: (i, 0))
    wspec = pl.BlockSpec((1, hd), lambda i: (0, 0))
    gate_blk = pl.BlockSpec((tm, w), lambda i: (i, gc))
    do, dg, dw = pl.pallas_call(
        body, grid=(s // tm,),
        in_specs=[blk, gate_blk, wspec, blk, ANY],
        out_specs=[blk, gate_blk, wspec],
        out_shape=[jax.ShapeDtypeStruct((s, w), F32), jax.ShapeDtypeStruct(into.shape, into.dtype),
                   jax.ShapeDtypeStruct((1, hd), F32)],
        input_output_aliases={4: 1},
        compiler_params=_cparams(1), name=name,
    )(o, src, nw.reshape(1, hd), dy, into)
    return do, dg, dw.reshape(hd)


def _sb_consts():
    r2 = lax.broadcasted_iota(jnp.int32, (2 * SB_BLOCK, SB_BLOCK), 0)
    c2 = lax.broadcasted_iota(jnp.int32, (2 * SB_BLOCK, SB_BLOCK), 1)
    r = lax.broadcasted_iota(jnp.int32, (SB_BLOCK, SB_BLOCK), 0)
    c = lax.broadcasted_iota(jnp.int32, (SB_BLOCK, SB_BLOCK), 1)
    lm0 = c < SB_HEAD_DIM
    m_gt = jnp.where(r > c, 1.0, 0.0).astype(BF16)
    m_lt = jnp.where(r < c, 1.0, 0.0).astype(BF16)
    return r2, c2, lm0, m_gt, m_lt


def _sb_stack(x, lm0):
    return jnp.concatenate([jnp.where(lm0, x, 0.0), jnp.where(lm0, 0.0, x)], axis=0)


def _sb_unstack(x2, lm0):
    return jnp.where(lm0, x2[:SB_BLOCK], x2[SB_BLOCK:])


def _sb_fwd(src, col0, width, *, name):
    s = src.shape[0]
    nq = s // SB_BLOCK
    npair = width // LANES
    scale = SB_HEAD_DIM ** -0.5
    nu = math.gcd(SB_UNROLL, nq)

    def body(q_ref, k_ref, v_ref, o_ref, w_hbm, stage, sems):
        p, i = pl.program_id(0), pl.program_id(1)
        r2, c2, lm0, m_gt, _ = _sb_consts()
        t_glob = i * SB_BLOCK + (r2 & (SB_BLOCK - 1))
        q2 = (_sb_stack(q_ref[...], lm0) * scale).astype(MXU_DTYPE)

        t = p * nq + i
        half = t % 2
        ngrp = nq // nu

        def save(half_, grp, pp, ii):
            return pltpu.make_async_copy(stage.at[half_, grp], w_hbm.at[pp, ii, grp], sems.at[half_, grp])

        def drain(half_, pp, ii):
            for grp in range(ngrp):
                @pl.when(grp <= ii // nu)
                def _():
                    save(half_, grp, pp, ii).wait()

        def group(base, carry, masked):
            o2, rsum = carry
            js = [base + nu - 1 - u for u in range(nu)]
            offs = [pl.multiple_of(j * SB_BLOCK, SB_BLOCK) for j in js]
            zs = [_dot(q2, k_ref[pl.ds(off, SB_BLOCK), :].astype(MXU_DTYPE), NT) for off in offs]
            ts = [jnp.log(1.0 + jnp.exp(-jnp.abs(z))) for z in zs]
            lks = [-(jnp.maximum(z, 0.0) + t) for z, t in zip(zs, ts)]
            if masked:
                masks = [(j * SB_BLOCK + c2) < t_glob for j in js]
                lks = [jnp.where(mk, lk, 0.0) for mk, lk in zip(masks, lks)]
            sufs = [_split_dot(lk, m_gt, SB_SPLIT) for lk in lks]
            rs = [rsum]
            for lk in lks:
                rs.append(rs[-1] + jnp.sum(lk, axis=1, keepdims=True))
            wgts = [jnp.exp((jnp.minimum(z, 0.0) - t) + r_ + sf) for z, t, r_, sf in zip(zs, ts, rs, sufs)]
            if masked:
                wgts = [jnp.where(mk, wg, 0.0) for mk, wg in zip(masks, wgts)]
            wbs = [wg.astype(MXU_DTYPE) for wg in wgts]
            grp = base // nu
            for u, wb in enumerate(wbs):
                stage[half, grp, nu - 1 - u] = wb
            save(half, grp, p, i).start()
            for off, wb in zip(offs, wbs):
                o2 = o2 + _dot(wb, v_ref[pl.ds(off, SB_BLOCK), :].astype(MXU_DTYPE), NN)
            return o2, rs[-1]

        top0 = (i // nu) * nu
        last = i // nu
        carry = group(top0, (jnp.zeros((2 * SB_BLOCK, LANES), F32), jnp.zeros((2 * SB_BLOCK, 1), F32)), True)
        o2, _ = lax.fori_loop(1, last + 1, lambda g, cr: group(top0 - nu * g, cr, False), carry)
        o_ref[...] = _sb_unstack(o2, lm0)

        @pl.when(t >= 1)
        def _():
            drain(1 - half, (t - 1) // nq, (t - 1) % nq)

        @pl.when(t == npair * nq - 1)
        def _():
            drain(half, p, i)

    blk = pl.BlockSpec((SB_BLOCK, LANES), lambda p, i: (i, p))
    return pl.pallas_call(
        body, grid=(npair, nq),
        in_specs=[pl.BlockSpec((SB_BLOCK, LANES), lambda p, i: (i, col0 + p)),
                  pl.BlockSpec((s, LANES), lambda p, i: (0, col0 + npair + p)),
                  pl.BlockSpec((s, LANES), lambda p, i: (0, col0 + 2 * npair + p))],
        out_specs=[blk, ANY],
        out_shape=[jax.ShapeDtypeStruct((s, width), F32),
                   jax.ShapeDtypeStruct((npair, nq, nq // nu, nu, 2 * SB_BLOCK, LANES), MXU_DTYPE)],
        scratch_shapes=[pltpu.VMEM((2, nq // nu, nu, 2 * SB_BLOCK, LANES), MXU_DTYPE),
                        pltpu.SemaphoreType.DMA((2, nq // nu))],
        compiler_params=_cparams(2), name=name,
    )(src, src, src)


def _sb_bwd(src, col0, width, weights, do, *, name):
    s = src.shape[0]
    nq = s // SB_BLOCK
    npair = width // LANES
    scale = SB_HEAD_DIM ** -0.5
    nu = math.gcd(SB_UNROLL, nq)

    def body(q_ref, k_ref, v_ref, w_hbm, do_ref, dq_ref, dk_ref, dv_ref, stage, sems):
        p, i = pl.program_id(0), pl.program_id(1)

        @pl.when(i == 0)
        def _():
            dk_ref[...] = jnp.zeros_like(dk_ref)
            dv_ref[...] = jnp.zeros_like(dv_ref)

        r2, c2, lm0, _, m_lt = _sb_consts()
        t_glob = i * SB_BLOCK + (r2 & (SB_BLOCK - 1))
        q2 = (_sb_stack(q_ref[...], lm0) * scale).astype(MXU_DTYPE)
        do2 = _sb_stack(do_ref[...], lm0).astype(MXU_DTYPE)

        ngrp = nq // nu

        def load(half_, grp, pp, ii):
            return pltpu.make_async_copy(w_hbm.at[pp, ii, grp], stage.at[half_, grp], sems.at[half_, grp])

        def fetch_step(half_, pp, ii):
            for grp in range(ngrp):
                @pl.when(grp <= ii // nu)
                def _():
                    load(half_, grp, pp, ii).start()

        def group(g, carry, masked, slot):
            dq2, csum = carry
            js = [nu * g + u for u in range(nu)]
            offs = [pl.multiple_of(j * SB_BLOCK, SB_BLOCK) for j in js]
            kbs = [k_ref[pl.ds(off, SB_BLOCK), :].astype(MXU_DTYPE) for off in offs]
            zs = [_dot(q2, kb, NT) for kb in kbs]
            dws = [_dot(do2, v_ref[pl.ds(off, SB_BLOCK), :].astype(MXU_DTYPE), NT) for off in offs]
            wbs = [stage[slot[0], slot[1], u] for u in range(nu)]
            sigs = [_sigmoid(z) for z in zs]
            dlogas = [wb.astype(F32) * dw for wb, dw in zip(wbs, dws)]
            pres = [_split_dot(dl, m_lt, SB_SPLIT) for dl in dlogas]
            dlks = []
            for dl, pre in zip(dlogas, pres):
                dlks.append(csum + pre)
                csum = csum + jnp.sum(dl, axis=1, keepdims=True)
            if masked:
                dlks = [jnp.where((j * SB_BLOCK + c2) < t_glob, dlk, 0.0) for j, dlk in zip(js, dlks)]
            dzbs = [(dl * (1.0 - sg) - dlk * sg).astype(MXU_DTYPE) for dl, sg, dlk in zip(dlogas, sigs, dlks)]
            for off, dzb, wb, kb in zip(offs, dzbs, wbs, kbs):
                dk_ref[pl.ds(off, SB_BLOCK), :] += _dot(dzb, q2, TN)
                dv_ref[pl.ds(off, SB_BLOCK), :] += _dot(wb, do2, TN)
                dq2 = dq2 + _dot(dzb, kb, NN)
            return dq2, csum

        t = p * nq + i
        half = t % 2

        @pl.when(t == 0)
        def _():
            fetch_step(0, p, i)

        @pl.when(t + 1 < npair * nq)
        def _():
            fetch_step(1 - half, (t + 1) // nq, (t + 1) % nq)

        def step(g, carry):
            load(half, g, p, i).wait()
            return group(g, carry, False, (half, g))

        last = i // nu
        carry = lax.fori_loop(0, last, step, (jnp.zeros((2 * SB_BLOCK, LANES), F32), jnp.zeros((2 * SB_BLOCK, 1), F32)))
        load(half, last, p, i).wait()
        dq2, _ = group(last, carry, True, (half, last))
        dq_ref[...] = _sb_unstack(dq2, lm0) * scale

    blk = pl.BlockSpec((SB_BLOCK, LANES), lambda p, i: (i, p))
    full = pl.BlockSpec((s, LANES), lambda p, i: (0, p))
    return pl.pallas_call(
        body, grid=(npair, nq),
        in_specs=[pl.BlockSpec((SB_BLOCK, LANES), lambda p, i: (i, col0 + p)),
                  pl.BlockSpec((s, LANES), lambda p, i: (0, col0 + npair + p)),
                  pl.BlockSpec((s, LANES), lambda p, i: (0, col0 + 2 * npair + p)),
                  ANY, blk],
        out_specs=[blk, full, full],
        out_shape=[jax.ShapeDtypeStruct((s, width), F32)] * 3,
        scratch_shapes=[pltpu.VMEM((2, nq // nu, nu, 2 * SB_BLOCK, LANES), MXU_DTYPE),
                        pltpu.SemaphoreType.DMA((2, nq // nu))],
        compiler_params=_cparams(2), name=name,
    )(src, src, src, weights, do)


def _ssd_group(xs, dt_rows, alogs, dtbs, bms, cms, h0s):
    c = bms[0].shape[0]
    per = len(xs) // len(bms)
    ii, jj = _chunk_masks(c)
    causal, eye = ii >= jj, ii == jj
    scores = [t for t in _each(lambda c_, b_: _hdot(c_, b_, NT), cms, bms) for _ in range(per)]
    dt_r = _each(lambda dt, b: _softplus(dt + b), dt_rows, dtbs)
    a_r = _each(lambda al, dt: -jnp.exp(al) * dt, alogs, dt_r)
    dt_col = _each(lambda dt: _row_to_col(dt, eye), dt_r)
    a_col = _each(lambda a: _row_to_col(a, eye), a_r)
    ac_col = _each(lambda a: jnp.sum(jnp.where(causal, a, 0.0), axis=1, keepdims=True), a_r)
    ac_row = _each(lambda a: jnp.sum(jnp.where(jj >= ii, a, 0.0), axis=0, keepdims=True), a_col)
    lmat = _each(lambda c_, r_: jnp.exp(jnp.where(causal, c_ - r_, NEG_BIG)), ac_col, ac_row)
    xdt = _each(jnp.multiply, xs, dt_col)
    al = _each(lambda a: jnp.sum(a, axis=1, keepdims=True), a_r)
    bm = [t for t in bms for _ in range(per)]
    cm = [t for t in cms for _ in range(per)]
    ys = _each(lambda sc, lm, xd, cm_, h0, ac: _hdot(sc * lm, xd) + _hdot(cm_, h0, NT) * jnp.exp(ac),
               scores, lmat, xdt, cm, h0s, ac_col)
    h1s = _each(lambda h0, al_, xd, ac, bm_: h0 * jnp.exp(al_) + _hdot(xd * jnp.exp(al_ - ac), bm_, TN),
                h0s, al, xdt, ac_col, bm)
    return ys, h1s


def _ssd_specs(ng, nc, r, gb, rev):
    n_of = (lambda n: nc - 1 - n) if rev else (lambda n: n)
    xw, bw = gb * r * SSM_HEAD_DIM, gb * SSM_STATE
    b0, c0 = (ng * r * SSM_HEAD_DIM) // bw, (ng * r * SSM_HEAD_DIM + ng * SSM_STATE) // bw
    x_spec = pl.BlockSpec((CHUNK, xw), lambda g, n: (n_of(n), g))
    b_spec = pl.BlockSpec((CHUNK, bw), lambda g, n: (n_of(n), b0 + g))
    c_spec = pl.BlockSpec((CHUNK, bw), lambda g, n: (n_of(n), c0 + g))
    dt_spec = pl.BlockSpec((gb, None, r, CHUNK), lambda g, n: (g, n_of(n), 0, 0))
    sc_spec = pl.BlockSpec((gb, r, 1), lambda g, n: (g, 0, 0))
    st_spec = pl.BlockSpec((gb, None, r, SSM_HEAD_DIM, SSM_STATE), lambda g, n: (g, n_of(n), 0, 0, 0))
    bc_out = pl.BlockSpec((CHUNK, bw), lambda g, n: (n_of(n), g))
    return x_spec, b_spec, c_spec, dt_spec, sc_spec, st_spec, x_spec, bc_out


def _ssd_refs(gb, r, x_ref, b_ref, c_ref, dt_ref, al_ref, db_ref):
    p, n = SSM_HEAD_DIM, SSM_STATE
    heads = [(g, h) for g in range(gb) for h in range(r)]
    xs = [x_ref[:, (g * r + h) * p:(g * r + h + 1) * p] for g, h in heads]
    dts = [dt_ref[g, h:h + 1, :] for g, h in heads]
    als = [al_ref[g, h:h + 1, :] for g, h in heads]
    dbs = [db_ref[g, h:h + 1, :] for g, h in heads]
    bms = [b_ref[:, g * n:(g + 1) * n] for g in range(gb)]
    cms = [c_ref[:, g * n:(g + 1) * n] for g in range(gb)]
    return heads, xs, dts, als, dbs, bms, cms


def _ssd_fwd(xbc, dt_rows, alog, dtb, *, name):
    s = xbc.shape[0]
    ng, nc, r = dt_rows.shape[0], dt_rows.shape[1], dt_rows.shape[2]
    w = ng * r * SSM_HEAD_DIM
    gb = math.gcd(SSD_GROUPS_PER_STEP, ng)
    x_spec, b_spec, c_spec, dt_spec, sc_spec, st_spec, y_spec, _ = _ssd_specs(ng, nc, r, gb, False)
    p = SSM_HEAD_DIM

    def body(x_ref, b_ref, c_ref, dt_ref, al_ref, db_ref, y_ref, st_ref, state):
        @pl.when(pl.program_id(1) == 0)
        def _():
            state[...] = jnp.zeros_like(state)

        st_ref[...] = state[...]
        heads, xs, dts, als, dbs, bms, cms = _ssd_refs(gb, r, x_ref, b_ref, c_ref, dt_ref, al_ref, db_ref)
        ys, h1s = _ssd_group(xs, dts, als, dbs, bms, cms, [state[g, h] for g, h in heads])
        for i, (g, h) in enumerate(heads):
            y_ref[:, (g * r + h) * p:(g * r + h + 1) * p] = ys[i]
            state[g, h] = h1s[i]

    return pl.pallas_call(
        body, grid=(ng // gb, nc),
        in_specs=[x_spec, b_spec, c_spec, dt_spec, sc_spec, sc_spec],
        out_specs=[y_spec, st_spec],
        out_shape=[jax.ShapeDtypeStruct((s, w), F32), jax.ShapeDtypeStruct((ng, nc, r, p, SSM_STATE), F32)],
        scratch_shapes=[pltpu.VMEM((gb, r, p, SSM_STATE), F32)],
        compiler_params=_cparams(2), name=name,
    )(xbc, xbc, xbc, dt_rows, alog, dtb)


def _ssd_bwd(xbc, dt_rows, alog, dtb, states, dy, *, name):
    s = xbc.shape[0]
    ng, nc, r = dt_rows.shape[0], dt_rows.shape[1], dt_rows.shape[2]
    w = ng * r * SSM_HEAD_DIM
    gb = math.gcd(SSD_GROUPS_PER_STEP, ng)
    x_spec, b_spec, c_spec, dt_spec, sc_spec, st_spec, y_spec, bc_out = _ssd_specs(ng, nc, r, gb, True)
    p = SSM_HEAD_DIM

    def body(x_ref, b_ref, c_ref, dt_ref, al_ref, db_ref, st_ref, dy_ref,
             dx_ref, dbm_ref, dcm_ref, ddt_ref, dal_ref, ddb_ref, dstate):
        @pl.when(pl.program_id(1) == 0)
        def _():
            dstate[...] = jnp.zeros_like(dstate)
            dal_ref[...] = jnp.zeros_like(dal_ref)
            ddb_ref[...] = jnp.zeros_like(ddb_ref)

        heads, xs, dts, als, dbs, bms, cms = _ssd_refs(gb, r, x_ref, b_ref, c_ref, dt_ref, al_ref, db_ref)
        _, vjp = jax.vjp(_ssd_group, xs, dts, als, dbs, bms, cms, [st_ref[g, h] for g, h in heads])
        dys = [dy_ref[:, (g * r + h) * p:(g * r + h + 1) * p] for g, h in heads]
        dxs, ddts, dals, ddbs, dbms, dcms, dh0s = vjp((dys, [dstate[g, h] for g, h in heads]))
        for g in range(gb):
            dbm_ref[:, g * SSM_STATE:(g + 1) * SSM_STATE] = dbms[g]
            dcm_ref[:, g * SSM_STATE:(g + 1) * SSM_STATE] = dcms[g]
        for i, (g, h) in enumerate(heads):
            dx_ref[:, (g * r + h) * p:(g * r + h + 1) * p] = dxs[i]
            ddt_ref[g, h:h + 1, :] = ddts[i]
            dal_ref[g, h:h + 1, :] += dals[i]
            ddb_ref[g, h:h + 1, :] += ddbs[i]
            dstate[g, h] = dh0s[i]

    gn = ng * SSM_STATE
    return pl.pallas_call(
        body, grid=(ng // gb, nc),
        in_specs=[x_spec, b_spec, c_spec, dt_spec, sc_spec, sc_spec, st_spec, y_spec],
        out_specs=[y_spec, bc_out, bc_out, dt_spec, sc_spec, sc_spec],
        out_shape=[jax.ShapeDtypeStruct((s, w), F32), jax.ShapeDtypeStruct((s, gn), F32), jax.ShapeDtypeStruct((s, gn), F32),
                   jax.ShapeDtypeStruct(dt_rows.shape, F32), jax.ShapeDtypeStruct((ng, r, 1), F32),
                   jax.ShapeDtypeStruct((ng, r, 1), F32)],
        scratch_shapes=[pltpu.VMEM((gb, r, p, SSM_STATE), F32)],
        compiler_params=_cparams(2), name=name,
    )(xbc, xbc, xbc, dt_rows, alog, dtb, states, dy)


def _ssm_post_fwd(y, xbc, src, z_col0, dexp, nw, *, name, tm=512):
    s, w = y.shape
    gw = w // SSM_GROUPS
    zc = z_col0 * LANES // gw

    def body(y_ref, x_ref, z_ref, d_ref, w_ref, o_ref):
        yy = (y_ref[...] + x_ref[...] * d_ref[...]) * _silu(z_ref[...])
        r = lax.rsqrt(jnp.mean(yy * yy, axis=-1, keepdims=True) + EPS)
        o_ref[...] = (yy * r * w_ref[...]).astype(o_ref.dtype)

    blk = pl.BlockSpec((tm, gw), lambda g, i: (i, g))
    vec = pl.BlockSpec((1, gw), lambda g, i: (0, g))
    return pl.pallas_call(
        body, grid=(SSM_GROUPS, s // tm),
        in_specs=[blk, blk, pl.BlockSpec((tm, gw), lambda g, i: (i, zc + g)), vec, vec],
        out_specs=blk, out_shape=jax.ShapeDtypeStruct((s, w), MXU_DTYPE),
        compiler_params=_cparams(2), name=name,
    )(y, xbc, src, dexp.reshape(1, w), nw.reshape(1, w))


def _ssm_post_bwd(y, xbc, src, z_col0, dexp, nw, dout, into, *, name, tm=512):
    s, w = y.shape
    gw = w // SSM_GROUPS
    zc = z_col0 * LANES // gw

    def body(y_ref, x_ref, z_ref, d_ref, w_ref, do_ref, into_ref, dy_ref, dx_ref, dz_ref, dd_ref, dw_ref):
        xv, zv, dv = x_ref[...], z_ref[...], d_ref[...]
        pre = y_ref[...] + xv * dv
        sz, sz_grad = _silu_and_grad(zv)
        yy = pre * sz
        r = lax.rsqrt(jnp.mean(yy * yy, axis=-1, keepdims=True) + EPS)
        yh = yy * r
        dov = do_ref[...]
        dyn = dov * w_ref[...]
        dyy = r * (dyn - yh * jnp.mean(dyn * yh, axis=-1, keepdims=True))
        dpre = dyy * sz
        dy_ref[...] = dpre
        dx_ref[...] = dpre * dv
        dz_ref[...] = (dyy * pre * sz_grad).astype(dz_ref.dtype)

        @pl.when(pl.program_id(1) == 0)
        def _():
            dd_ref[...] = jnp.zeros_like(dd_ref)
            dw_ref[...] = jnp.zeros_like(dw_ref)

        dd_ref[...] += jnp.sum(dpre * xv, axis=0, keepdims=True)
        dw_ref[...] += jnp.sum(dov * yh, axis=0, keepdims=True)

    blk = pl.BlockSpec((tm, gw), lambda g, i: (i, g))
    vec = pl.BlockSpec((1, gw), lambda g, i: (0, g))
    z_blk = pl.BlockSpec((tm, gw), lambda g, i: (i, zc + g))
    dy, dx, dz, dd, dw = pl.pallas_call(
        body, grid=(SSM_GROUPS, s // tm),
        in_specs=[blk, blk, z_blk, vec, vec, blk, ANY],
        out_specs=[blk, blk, z_blk, vec, vec],
        out_shape=[jax.ShapeDtypeStruct((s, w), F32), jax.ShapeDtypeStruct((s, w), F32),
                   jax.ShapeDtypeStruct(into.shape, into.dtype), jax.ShapeDtypeStruct((1, w), F32),
                   jax.ShapeDtypeStruct((1, w), F32)],
        input_output_aliases={6: 2},
        compiler_params=_cparams(2), name=name,
    )(y, xbc, src, dexp.reshape(1, w), nw.reshape(1, w), dout, into)
    return dy, dx, dz, dd.reshape(w), dw.reshape(w)


def _merge_fwd(proj3, src, gate_col0, d, *, name, tm=512):
    s = proj3.shape[0]
    nb = proj3.shape[1] // d
    gc = gate_col0 * LANES // d

    def body(*refs):
        p_refs, g_refs, o_ref = refs[:nb], refs[nb:2 * nb], refs[-1]
        acc = None
        for p_ref, g_ref in zip(p_refs, g_refs):
            term = _sigmoid(g_ref[...]) * p_ref[...]
            acc = term if acc is None else acc + term
        o_ref[...] = acc.astype(o_ref.dtype)

    p_specs = [pl.BlockSpec((tm, d), lambda i, b=b: (i, b)) for b in range(nb)]
    g_specs = [pl.BlockSpec((tm, d), lambda i, b=b: (i, gc + b)) for b in range(nb)]
    return pl.pallas_call(
        body, grid=(s // tm,), in_specs=p_specs + g_specs,
        out_specs=pl.BlockSpec((tm, d), lambda i: (i, 0)), out_shape=jax.ShapeDtypeStruct((s, d), MXU_DTYPE),
        compiler_params=_cparams(1), name=name,
    )(*([proj3] * nb), *([src] * nb))


def _merge_bwd(proj3, src, gate_col0, d, dmerged, into, *, name, tm=512):
    s = proj3.shape[0]
    nb = proj3.shape[1] // d
    gc = gate_col0 * LANES // d

    def body(p_ref, g_ref, dm_ref, into_ref, dp_ref, dg_ref):
        sg = _sigmoid(g_ref[...])
        dm = dm_ref[...]
        dp_ref[...] = (dm * sg).astype(dp_ref.dtype)
        dg_ref[...] = (dm * p_ref[...] * sg * (1.0 - sg)).astype(dg_ref.dtype)

    blk = pl.BlockSpec((tm, d), lambda i, b: (i, b))
    gate_blk = pl.BlockSpec((tm, d), lambda i, b: (i, gc + b))
    return pl.pallas_call(
        body, grid=(s // tm, nb),
        in_specs=[blk, gate_blk, pl.BlockSpec((tm, d), lambda i, b: (i, 0)), ANY],
        out_specs=[blk, gate_blk],
        out_shape=[jax.ShapeDtypeStruct(proj3.shape, MXU_DTYPE), jax.ShapeDtypeStruct(into.shape, into.dtype)],
        input_output_aliases={3: 1},
        compiler_params=_cparams(2), name=name,
    )(proj3, src, dmerged, into)


ANY = pl.BlockSpec(memory_space=pl.ANY)
MESH = pl.DeviceIdType.MESH


def _all_gather(shards, *, name, after=None):
    nt = len(shards)
    n_after = 0 if after is None else 1

    def body(*refs):
        x_refs, out_refs = refs[:nt], refs[nt + n_after:2 * nt + n_after]
        send_sems, recv_sems, local_sems = refs[2 * nt + n_after:]
        x, y, c = lax.axis_index("x"), lax.axis_index("y"), lax.axis_index("c")
        me, sibling = (x, y, c), (x, y, 1 - c)
        chips = [(1 - x, y), (x, 1 - y), (1 - x, 1 - y)]

        def slot(t, px, py, pc):
            return out_refs[t].at[4 * px + 2 * py + pc]

        def copy(t, k, block, to, from_input=False):
            return pltpu.make_async_remote_copy(
                src_ref=x_refs[t] if from_input else slot(t, *block), dst_ref=slot(t, *block),
                send_sem=send_sems.at[7 * t + k], recv_sem=recv_sems.at[7 * t + k], device_id=to, device_id_type=MESH)

        mine = [pltpu.make_async_copy(x_refs[t], slot(t, *me), local_sems.at[t]) for t in range(nt)]
        for cp in mine:
            cp.start()
        first = [copy(t, 0, me, sibling, True) for t in range(nt)]
        first += [copy(t, 1 + j, me, (*chip, c), True) for j, chip in enumerate(chips) for t in range(nt)]
        for cp in first:
            cp.start()
        passed = []
        for j, chip in enumerate(chips):
            for t in range(nt):
                copy(t, 1 + j, (*chip, c), me).wait_recv()
                fwd = copy(t, 4 + j, (*chip, c), sibling)
                fwd.start()
                passed.append(fwd)
        for t in range(nt):
            copy(t, 0, sibling, me).wait_recv()
            for j, chip in enumerate(chips):
                copy(t, 4 + j, (*chip, 1 - c), me).wait_recv()
        for cp in first + passed:
            cp.wait_send()
        for cp in mine:
            cp.wait()

    return pl.pallas_call(
        body, out_shape=[jax.ShapeDtypeStruct((N_DEV,) + a.shape, a.dtype) for a in shards],
        in_specs=[ANY] * (nt + n_after), out_specs=[ANY] * nt,
        scratch_shapes=[pltpu.SemaphoreType.DMA((7 * nt,)), pltpu.SemaphoreType.DMA((7 * nt,)),
                        pltpu.SemaphoreType.DMA((nt,))],
        name=name,
    )(*shards, *([] if after is None else [after]))


HBM = pl.BlockSpec(memory_space=pltpu.HBM)
SEM = pl.BlockSpec(memory_space=pltpu.SEMAPHORE)
EFFECT = pltpu.SideEffectType.DATAFLOW_SIDE_EFFECTING


def _peers():
    x, y, c = lax.axis_index("x"), lax.axis_index("y"), lax.axis_index("c")
    peers = []
    for k in range(1, N_DEV):
        px, py, pc = x ^ ((k >> 2) & 1), y ^ ((k >> 1) & 1), c ^ (k & 1)
        peers.append(((px, py, pc), 4 * px + 2 * py + pc))
    return 4 * x + 2 * y + c, peers


def _split_copies(slots, src_refs, land_refs, send_sems, recv_sems):
    me, peers = _peers()
    copies = []
    for t, (whole, layer) in enumerate(slots):
        dst = land_refs[t].at[me] if layer is None else land_refs[t].at[me, layer]
        for k, (dev, lin) in enumerate(peers):
            copies.append(pltpu.make_async_remote_copy(
                src_ref=src_refs[t] if whole else src_refs[t].at[lin], dst_ref=dst,
                send_sem=send_sems.at[7 * t + k], recv_sem=recv_sems.at[7 * t + k], device_id=dev, device_id_type=MESH))
    return copies


def _split_start(srcs, lands, slots, carry, *, name):
    n = len(srcs)

    def body(*refs):
        copies = _split_copies(slots, refs[:n], refs[n:2 * n], refs[2 * n + 1], refs[2 * n + 2])
        for cp in copies:
            cp.start()

    def hbm(a):
        return pltpu.HBM(a.shape, a.dtype)

    outs = pl.pallas_call(
        body, name=name,
        out_shape=[pltpu.SemaphoreType.DMA((7 * n,)), pltpu.SemaphoreType.DMA((7 * n,))]
        + [hbm(a) for a in srcs] + [hbm(a) for a in lands] + [hbm(carry)],
        in_specs=[HBM] * (2 * n + 1), out_specs=[SEM, SEM] + [HBM] * (2 * n + 1),
        input_output_aliases={i: 2 + i for i in range(2 * n + 1)},
        compiler_params=pltpu.CompilerParams(has_side_effects=EFFECT),
    )(*[pltpu.with_memory_space_constraint(a, pltpu.HBM) for a in list(srcs) + list(lands) + [carry]])
    return outs[0], outs[1], outs[2:2 + n], outs[2 + n:2 + 2 * n], outs[2 + 2 * n]


def _split_wait(send_sems, recv_sems, srcs, lands, slots, after, *, name):
    n = len(srcs)

    def body(*refs):
        copies = _split_copies(slots, refs[:n], refs[n:2 * n], refs[2 * n], refs[2 * n + 1])
        for cp in copies:
            cp.wait_send()
        for cp in copies:
            cp.wait_recv()

    outs = pl.pallas_call(
        body, name=name,
        out_shape=[pltpu.HBM(a.shape, a.dtype) for a in list(srcs) + list(lands)],
        in_specs=[HBM] * (2 * n) + [SEM, SEM, ANY], out_specs=[HBM] * (2 * n),
        input_output_aliases={i: i for i in range(2 * n)},
        compiler_params=pltpu.CompilerParams(has_side_effects=EFFECT),
    )(*srcs, *lands, send_sems, recv_sems, after)
    return outs[n:]


def _adam_math(w, g, m, v):
    m1 = ADAM_B1 * m + (1.0 - ADAM_B1) * g
    v1 = ADAM_B2 * v + (1.0 - ADAM_B2) * (g * g)
    m_hat = m1 / (1.0 - ADAM_B1 ** ADAM_STEP)
    v_hat = v1 / (1.0 - ADAM_B2 ** ADAM_STEP)
    delta = -ADAM_LR * (m_hat / (jnp.sqrt(v_hat) + ADAM_EPS) + ADAM_WD * w)
    return delta, m1, v1


def _sum_adamw(parts, w, m, v, layer, prev, *, name):
    shape = w.shape
    r, c = shape[-2], shape[-1]
    a_l = math.prod(shape[1:-2])
    a = shape[0] * a_l
    base = layer * a_l
    if r % 256 == 0:
        tr, tc = 256, c
    else:
        tr, tc = r, _pick(c, (256, 128))
    w3, m3, v3 = (t.reshape(a, r, c) for t in (w, m, v))
    n_prev = 0 if prev is None else 4

    def body(*refs):
        p_ref, w_ref, m_ref, v_ref = refs[:4]
        g_ref, d_ref, m1_ref, v1_ref = refs[4 + n_prev:]
        g = p_ref[0].astype(F32)
        for src in range(1, N_DEV):
            g = g + p_ref[src].astype(F32)
        delta, m1, v1 = _adam_math(w_ref[...], g, m_ref[...], v_ref[...])
        g_ref[...] = g
        d_ref[...] = delta
        m1_ref[...] = m1
        v1_ref[...] = v1

    nr, ncol = r // tr, c // tc
    blk = pl.BlockSpec((None, tr, tc), lambda i, j: (base + i, j // ncol, j % ncol))
    prev3 = [] if prev is None else [t.reshape(a, r, c) for t in prev]
    outs = pl.pallas_call(
        body, grid=(a_l, nr * ncol),
        in_specs=[pl.BlockSpec((N_DEV, None, tr, tc), lambda i, j: (0, i, j // ncol, j % ncol)), blk, blk, blk]
        + [ANY] * n_prev,
        out_specs=[blk] * 4, out_shape=[jax.ShapeDtypeStruct((a, r, c), F32)] * 4,
        input_output_aliases={4 + k: k for k in range(n_prev)},
        compiler_params=_cparams(2), name=name,
    )(parts.reshape(N_DEV, a_l, r, c), w3, m3, v3, *prev3)
    return [o.reshape(shape) for o in outs]


def _sum_parts(parts, *, name):
    rows = parts.shape[1]

    def body(p_ref, o_ref):
        g = p_ref[0]
        for src in range(1, N_DEV):
            g = g + p_ref[src]
        o_ref[...] = g

    return pl.pallas_call(
        body, grid=(1,), in_specs=[pl.BlockSpec((N_DEV, rows, LANES), lambda i: (0, 0, 0))],
        out_specs=pl.BlockSpec((rows, LANES), lambda i: (0, 0)), out_shape=jax.ShapeDtypeStruct((rows, LANES), F32),
        compiler_params=_cparams(1), name=name,
    )(parts)


def _adamw(w, g, m, v, *, name):
    rows = w.shape[0]

    def body(w_ref, g_ref, m_ref, v_ref, d_ref, m1_ref, v1_ref):
        delta, m1, v1 = _adam_math(w_ref[...], g_ref[...], m_ref[...], v_ref[...])
        d_ref[...] = delta
        m1_ref[...] = m1
        v1_ref[...] = v1

    blk = pl.BlockSpec((rows, LANES), lambda i: (0, 0))
    return pl.pallas_call(
        body, grid=(1,), in_specs=[blk] * 4, out_specs=[blk] * 3,
        out_shape=[jax.ShapeDtypeStruct((rows, LANES), F32)] * 3,
        compiler_params=_cparams(1), name=name,
    )(w, g, m, v)


def _pack(arrs, dtype, row_mult=16):
    flat = jnp.concatenate([a.reshape(-1).astype(dtype) for a in arrs])
    n = flat.shape[0]
    rows = -(-n // (LANES * row_mult)) * row_mult
    flat = jnp.pad(flat, (0, rows * LANES - n))
    return flat.reshape(rows, LANES)


def _unpack(packed, shapes):
    flat = packed.reshape(-1)
    out, off = [], 0
    for shp in shapes:
        n = math.prod(shp)
        out.append(flat[off:off + n].reshape(shp))
        off += n
    return out


class _Layout:
    def __init__(self, d):
        self.d = d
        w = d
        self.dn_heads = w // DN_HEAD_DIM
        self.ssm_heads = w // SSM_HEAD_DIM
        gn = SSM_GROUPS * SSM_STATE
        self.sizes = (3 * w, w, self.dn_heads, self.dn_heads, 3 * w, w, w + 2 * gn, self.ssm_heads, 3 * d)
        offs, o = [], 0
        for sz in self.sizes:
            offs.append(o)
            o += sz
        self.offs = offs
        self.in_dim = o
        self.big = (0, 1, 4, 5, 6, 8)
        self.small = (2, 3, 7)
        cols, o = {}, 0
        for idx in self.big:
            cols[idx] = o
            o += self.sizes[idx]
        self.small_col = o
        self.cols = cols
        self.padded = o + LANES
        self.n_small = sum(self.sizes[i] for i in self.small)

    def from_shards(self, parts):
        cs = self.in_dim // N_DEV
        pieces = []
        for i in self.big + self.small:
            a, b = self.offs[i], self.offs[i] + self.sizes[i]
            while a < b:
                j = a // cs
                hi = min(b, (j + 1) * cs)
                pieces.append(parts[j][:, a - j * cs:hi - j * cs])
                a = hi
        pieces.append(jnp.zeros((parts.shape[1], LANES - self.n_small), parts.dtype))
        return jnp.concatenate(pieces, axis=1)

    def to_shards(self, wp):
        cs = self.in_dim // N_DEV
        pcol = dict(self.cols)
        o = self.small_col
        for i in self.small:
            pcol[i] = o
            o += self.sizes[i]
        shards = []
        for j in range(N_DEV):
            a, b = j * cs, (j + 1) * cs
            pieces = []
            for i in range(len(self.sizes)):
                lo, hi = max(a, self.offs[i]), min(b, self.offs[i] + self.sizes[i])
                if lo < hi:
                    pieces.append(wp[:, pcol[i] + lo - self.offs[i]:pcol[i] + hi - self.offs[i]])
            shards.append(jnp.concatenate(pieces, axis=1))
        return jnp.stack(shards)

def _rows_form(cols_t, nh, nc):
    return cols_t.T.reshape(nh, nc, 1, CHUNK)


def _layer_fwd(x, p, lay, tag, late=None):
    s, d = x.shape
    nc = s // CHUNK
    w = d
    dnh, smh = lay.dn_heads, lay.ssm_heads
    r = smh // SSM_GROUPS
    cb = {k: v // LANES for k, v in lay.cols.items()}
    sv = {}
    h1 = _rms_fwd(x, p["norm_mix"], name=f"rms_mix_{tag}")
    proj = _matmul(h1, p["w_in"], name=f"mm_in_{tag}")
    small = proj[:, lay.small_col:lay.small_col + LANES]
    a_rows = _rows_form(small[:, 0:dnh], dnh, nc)
    b_rows = _rows_form(small[:, dnh:2 * dnh], dnh, nc)
    dt_rows = small[:, 2 * dnh:2 * dnh + smh].T.reshape(SSM_GROUPS, r, nc, CHUNK).transpose(0, 2, 1, 3)
    zero_b = jnp.zeros((1, 3 * w), F32)
    dn_qkv = _conv_fwd(proj, cb[0], p["dn_conv_w"], zero_b, 2 * dnh, name=f"dn_conv_{tag}")
    dn_alog = p["dn_a_log"].reshape(dnh, 1, 1)
    dn_dtb = p["dn_dt_bias"].reshape(dnh, 1, 1)
    o_dn, dn_states, dn_inv = _dn_fwd(dn_qkv, a_rows, b_rows, dn_alog, dn_dtb, name=f"dn_chunk_{tag}")
    y_dn = _dn_post_fwd(o_dn, proj, cb[1], p["dn_norm_w"], name=f"dn_post_{tag}")
    o_sb, sb_r = _sb_fwd(proj, cb[4], w, name=f"sb_{tag}")
    xbc = _conv_fwd(proj, cb[6], p["ssm_conv_w"], p["ssm_conv_b"].reshape(1, -1), 0, name=f"ssm_conv_{tag}")
    ssm_alog = p["ssm_a_log"].reshape(SSM_GROUPS, r, 1)
    ssm_dtb = p["ssm_dt_bias"].reshape(SSM_GROUPS, r, 1)
    y_ssd, ssm_states = _ssd_fwd(xbc, dt_rows, ssm_alog, ssm_dtb, name=f"ssd_{tag}")
    dexp = jnp.repeat(p["ssm_d"], SSM_HEAD_DIM)
    y_ssm = _ssm_post_fwd(y_ssd, xbc, proj, cb[5], dexp, p["ssm_norm_w"], name=f"ssm_post_{tag}")
    if late is not None:
        p.update(late(y_ssm))
    branches = (y_dn, o_sb, y_ssm)
    proj3 = lax.empty((s, 3 * d), F32)
    for i, br in enumerate(branches):
        proj3 = _matmul(br, p["w_branch"][i], into=(proj3, i * d), name=f"mm_branch{i}_{tag}")
    merged = _merge_fwd(proj3, proj, cb[8], d, name=f"merge_{tag}")
    x1 = _matmul(merged, p["w_out"], name=f"mm_out_{tag}", epilogue=lambda acc, res: (acc + res,), extras=(x,))
    h2 = _rms_fwd(x1, p["norm_mlp"], name=f"rms_mlp_{tag}")
    u, act = _matmul(h2, p["w_up"], name=f"mm_up_{tag}", out_dtypes=(F32, MXU_DTYPE),
                     epilogue=lambda acc: (acc, jnp.square(jnp.maximum(acc, 0.0))))
    x2 = _matmul(act, p["w_down"], name=f"mm_down_{tag}", epilogue=lambda acc, res: (acc + res,), extras=(x1,))
    sv.update(x=x, h1=h1, proj=proj, a_rows=a_rows, b_rows=b_rows, dt_rows=dt_rows, dn_qkv=dn_qkv, dn_alog=dn_alog,
              dn_dtb=dn_dtb, o_dn=o_dn, dn_states=dn_states, dn_inv=dn_inv, y_dn=y_dn, o_sb=o_sb, sb_r=sb_r, xbc=xbc, ssm_alog=ssm_alog,
              ssm_dtb=ssm_dtb, y_ssd=y_ssd, ssm_states=ssm_states, dexp=dexp, y_ssm=y_ssm, proj3=proj3, merged=merged,
              x1=x1, h2=h2, u=u, act=act)
    return x2, sv


def _layer_bwd(dx2, p, sv, lay, tag, early=None, late=None):
    x = sv["x"]
    s, d = x.shape
    nc = s // CHUNK
    w = d
    dnh, smh = lay.dn_heads, lay.ssm_heads
    r = smh // SSM_GROUPS
    gn = SSM_GROUPS * SSM_STATE
    cb = {k: v // LANES for k, v in lay.cols.items()}
    proj = sv["proj"]
    g = {}
    dx2_b = dx2.astype(MXU_DTYPE)
    du = _matmul(dx2_b, p["w_down"], tb=True, name=f"mm_down_dx_{tag}", out_dtypes=(MXU_DTYPE,),
                 epilogue=lambda acc, uu: (acc * (2.0 * jnp.maximum(uu, 0.0)),), extras=(sv["u"],))
    g["w_down"] = _matmul(sv["act"], dx2_b, ta=True, name=f"mm_down_dw_{tag}", out_dtypes=(BF16,)).reshape(N_DEV, -1, d)
    g["w_up"] = _matmul(sv["h2"], du, ta=True, name=f"mm_up_dw_{tag}", out_dtypes=(BF16,), col_shards=N_DEV)
    dh2 = _matmul(du, p["w_up"], tb=True, name=f"mm_up_dx_{tag}")
    dx1, g["norm_mlp"] = _rms_bwd(sv["x1"], p["norm_mlp"], dh2, dx2, name=f"rms_mlp_bwd_{tag}")
    dx1_b = dx1.astype(MXU_DTYPE)
    dmerged = _matmul(dx1_b, p["w_out"], tb=True, name=f"mm_out_dx_{tag}")
    g["w_out"] = _matmul(sv["merged"], dx1_b, ta=True, name=f"mm_out_dw_{tag}", out_dtypes=(BF16,)).reshape(N_DEV, -1, d)
    dproj = lax.empty((s, lay.padded), MXU_DTYPE)
    dproj3, dproj = _merge_bwd(sv["proj3"], proj, cb[8], d, dmerged, dproj, name=f"merge_bwd_{tag}")
    branches = (sv["y_dn"], sv["o_sb"], sv["y_ssm"])
    dwb, dbr = [], []
    for i, br in enumerate(branches):
        dp_i = dproj3[:, i * d:(i + 1) * d]
        dwb.append(_matmul(br, dp_i, ta=True, name=f"mm_branch{i}_dw_{tag}", out_dtypes=(BF16,)).reshape(N_DEV, -1, d))
        dbr.append(_matmul(dp_i, p["w_branch"][i], tb=True, name=f"mm_branch{i}_dx_{tag}"))
    g["w_branch"] = jnp.stack(dwb, axis=1)
    dy_dn, do_sb, dy_ssm = dbr
    if early is not None:
        dy_ssm = early(g, dy_ssm)
    dy_ssd, dxs_skip, dproj, ddexp, g["ssm_norm_w"] = _ssm_post_bwd(
        sv["y_ssd"], sv["xbc"], proj, cb[5], sv["dexp"], p["ssm_norm_w"], dy_ssm, dproj, name=f"ssm_post_bwd_{tag}")
    g["ssm_d"] = ddexp.reshape(smh, SSM_HEAD_DIM).sum(axis=1)
    dxs, dbm, dcm, ddt_rows, dalog, ddtb = _ssd_bwd(
        sv["xbc"], sv["dt_rows"], sv["ssm_alog"], sv["ssm_dtb"], sv["ssm_states"], dy_ssd, name=f"ssd_bwd_{tag}")
    g["ssm_a_log"] = dalog.reshape(smh)
    g["ssm_dt_bias"] = ddtb.reshape(smh)
    dxbc_post = jnp.concatenate([dxs + dxs_skip, dbm, dcm], axis=1)
    dproj, g["ssm_conv_w"], dcb = _conv_bwd(proj, cb[6], p["ssm_conv_w"], p["ssm_conv_b"].reshape(1, -1), 0, dxbc_post,
                                            dproj, name=f"ssm_conv_bwd_{tag}")
    g["ssm_conv_b"] = dcb.reshape(-1)
    ddt = ddt_rows.transpose(0, 2, 1, 3).reshape(smh, s).T
    dqkv_sb = _sb_bwd(proj, cb[4], w, sv["sb_r"], do_sb, name=f"sb_bwd_{tag}")
    dproj = lax.dynamic_update_slice(dproj, jnp.concatenate([t.astype(MXU_DTYPE) for t in dqkv_sb], axis=1), (0, lay.cols[4]))
    do_dn, dproj, g["dn_norm_w"] = _dn_post_bwd(sv["o_dn"], proj, cb[1], p["dn_norm_w"], dy_dn, dproj,
                                                name=f"dn_post_bwd_{tag}")
    dqkv_dn, da_rows, db_rows, dal, ddtb_dn = _dn_bwd(
        sv["dn_qkv"], sv["a_rows"], sv["b_rows"], sv["dn_alog"], sv["dn_dtb"], sv["dn_states"], sv["dn_inv"], do_dn,
        name=f"dn_chunk_bwd_{tag}")
    g["dn_a_log"] = dal.reshape(dnh)
    g["dn_dt_bias"] = ddtb_dn.reshape(dnh)
    zero_b = jnp.zeros((1, 3 * w), F32)
    dproj, g["dn_conv_w"], _ = _conv_bwd(proj, cb[0], p["dn_conv_w"], zero_b, 2 * dnh, dqkv_dn, dproj,
                                         name=f"dn_conv_bwd_{tag}")
    da = da_rows.reshape(dnh, s).T
    db = db_rows.reshape(dnh, s).T
    dsmall = jnp.concatenate([da, db, ddt, jnp.zeros((s, LANES - lay.n_small), F32)], axis=1).astype(MXU_DTYPE)
    dproj = lax.dynamic_update_slice(dproj, dsmall, (0, lay.small_col))
    g["w_in"] = lay.to_shards(_matmul(sv["h1"], dproj, ta=True, name=f"mm_in_dw_{tag}", out_dtypes=(BF16,)))
    if late is not None:
        dproj = late(g, dproj)
    dh1 = _matmul(dproj, p["w_in"], tb=True, name=f"mm_in_dx_{tag}")
    dx0, g["norm_mix"] = _rms_bwd(x, p["norm_mix"], dh1, dx1, name=f"rms_mix_bwd_{tag}")
    return dx0, g


BIG = ("w_in", "w_branch", "w_out", "w_up", "w_down")
CONV = ("dn_conv_w", "ssm_conv_w")
SMALL = ("norm_mix", "dn_conv_w", "dn_a_log", "dn_dt_bias", "dn_norm_w", "ssm_conv_w", "ssm_conv_b", "ssm_a_log",
         "ssm_dt_bias", "ssm_d", "ssm_norm_w", "norm_mlp", "norm_final")
WEIGHTS = ("norm_mix", "w_in", "dn_conv_w", "dn_a_log", "dn_dt_bias", "dn_norm_w", "ssm_conv_w", "ssm_conv_b", "ssm_a_log",
           "ssm_dt_bias", "ssm_d", "ssm_norm_w", "w_branch", "w_out", "norm_mlp", "w_up", "w_down", "norm_final")
SHARD_AXIS = {"w_in": 2, "dn_conv_w": 2, "ssm_conv_w": 2, "w_branch": 2, "w_out": 1, "w_up": 2, "w_down": 1}


def _to_shards(full, axis):
    shp = full.shape
    n = shp[axis] // N_DEV
    t = full.reshape(shp[:axis] + (N_DEV, n) + shp[axis + 1:])
    return jnp.moveaxis(t, axis, 0)


def _unshard(parts, axis, *, name):
    shard = parts.shape[1:]
    nd = len(shard)
    if axis == 0:
        return parts.reshape((N_DEV * shard[0],) + shard[1:])

    def copy_block(i_ref, o_ref):
        o_ref[...] = i_ref[...]

    if axis == nd - 1:
        rows, n = math.prod(shard[:-1]), shard[-1]
        out = pl.pallas_call(
            copy_block, grid=(N_DEV,),
            in_specs=[pl.BlockSpec((None, rows, n), lambda j: (j, 0, 0))],
            out_specs=pl.BlockSpec((rows, n), lambda j: (0, j)),
            out_shape=jax.ShapeDtypeStruct((rows, N_DEV * n), parts.dtype),
            compiler_params=_cparams(1), name=name,
        )(parts.reshape(N_DEV, rows, n))
        return out.reshape(shard[:-1] + (N_DEV * n,))
    assert axis == nd - 2, (parts.shape, axis)
    a, n, c = math.prod(shard[:-2]), shard[-2], shard[-1]
    out = pl.pallas_call(
        copy_block, grid=(N_DEV, a),
        in_specs=[pl.BlockSpec((None, None, n, c), lambda j, i: (j, i, 0, 0))],
        out_specs=pl.BlockSpec((None, n, c), lambda j, i: (i, j, 0)),
        out_shape=jax.ShapeDtypeStruct((a, N_DEV * n, c), parts.dtype),
        compiler_params=_cparams(2), name=name,
    )(parts.reshape(N_DEV, a, n, c))
    return out.reshape(shard[:-2] + (N_DEV * n, c))


def _step(w, m, v, x, target):
    s, d = x.shape
    lay = _Layout(d)
    me = 4 * lax.axis_index("x") + 2 * lax.axis_index("y") + lax.axis_index("c")

    def shard(n, l):
        return w[n][l].astype(BF16) if n in BIG else w[n][l]

    def empty_land(a):
        return lax.empty((N_DEV,) + a.shape, a.dtype)

    def with_own(land, own):
        return lax.dynamic_update_index_in_dim(land, own, me, 0)

    def assemble(n, parts, l):
        return lay.from_shards(parts) if n == "w_in" else _unshard(parts, SHARD_AXIS[n] - 1, name=f"unshard_{n}_l{l}")

    small_names = tuple(n for n in WEIGHTS if n not in BIG + CONV + ("norm_final",))

    first, rest = ("w_in",) + CONV, BIG[1:]
    got = _all_gather([shard(n, 0) for n in first], name="gather_l0_first")
    whole, sliced = (True, None), (False, None)
    names_a, names_b = rest, BIG + CONV
    srcs_a, srcs_b = [shard(n, 0) for n in names_a], [shard(n, 1) for n in names_b]
    sem_sa, sem_ra, srcs_a, lands_a, w_in0 = _split_start(
        srcs_a, [empty_land(a) for a in srcs_a], [whole] * len(srcs_a), got[0], name="gather_l0_rest_start")
    sem_sb, sem_rb, srcs_b, lands_b, w_in0 = _split_start(
        srcs_b, [empty_land(a) for a in srcs_b], [whole] * len(srcs_b), w_in0, name="gather_l1_start")
    p0 = {n: w[n][0] for n in small_names}
    p0.update({n: assemble(n, g, 0) for n, g in zip(first, [w_in0] + list(got[1:]))})

    def late_l0(after):
        lands = _split_wait(sem_sa, sem_ra, srcs_a, lands_a, [whole] * len(srcs_a), after, name="gather_l0_rest_wait")
        return {n: assemble(n, with_own(ld, s_), 0) for n, ld, s_ in zip(names_a, lands, srcs_a)}

    h, sv0 = _layer_fwd(x, p0, lay, "l0", late=late_l0)
    lands = _split_wait(sem_sb, sem_rb, srcs_b, lands_b, [whole] * len(srcs_b), h, name="gather_l1_wait")
    p1 = {n: w[n][1] for n in small_names}
    p1.update({n: assemble(n, with_own(ld, s_), 1) for n, ld, s_ in zip(names_b, lands, srcs_b)})
    h, sv1 = _layer_fwd(h, p1, lay, "l1")
    loss, dh, g_norm_final = _final_loss(h, w["norm_final"], target, name="final_loss")
    grads = [None] * DEPTH
    dh, grads[1] = _layer_bwd(dh, p1, sv1, lay, "l1")

    def exchange_start(names, g, carry, tag):
        srcs = [g[n] for n in names]
        return _split_start(srcs, [lax.empty(a.shape, a.dtype) for a in srcs], [sliced] * len(srcs), carry,
                            name=f"grad_{tag}_start")

    def exchange_wait(names, started, after, tag):
        sem_s, sem_r, srcs, lands_, _ = started
        lands_ = _split_wait(sem_s, sem_r, srcs, lands_, [sliced] * len(srcs), after, name=f"grad_{tag}_wait")
        return {n: with_own(ld, lax.dynamic_index_in_dim(s_, me, 0, keepdims=False)) for n, ld, s_ in zip(names, lands_, srcs)}

    x1_started = exchange_start(BIG, grads[1], dh, "l1")
    pending = {}

    def early_l0(g, carry):
        pending["rest"] = exchange_start(rest, g, carry, "l0_rest")
        return pending["rest"][4]

    def late_bwd_l0(g, carry):
        pending["w_in"] = exchange_start(("w_in",), g, carry, "l0_w_in")
        return pending["w_in"][4]

    grad_x, grads[0] = _layer_bwd(x1_started[4], p0, sv0, lay, "l0", early=early_l0, late=late_bwd_l0)

    out = {"grad": {}, "delta": {}, "new_m": {}, "new_v": {}}
    parts1 = exchange_wait(BIG, x1_started, grad_x, "l1")
    res1 = {n: _sum_adamw(parts1[n], w[n], m[n], v[n], 1, None, name=f"sum_adamw_{n}_l1") for n in BIG}
    parts0 = exchange_wait(rest, pending["rest"], res1["w_in"][0], "l0_rest")
    res0 = {n: _sum_adamw(parts0[n], w[n], m[n], v[n], 0, res1[n], name=f"sum_adamw_{n}_l0") for n in rest}
    parts0 = exchange_wait(("w_in",), pending["w_in"], res0["w_down"][0], "l0_w_in")
    res0["w_in"] = _sum_adamw(parts0["w_in"], w["w_in"], m["w_in"], v["w_in"], 0, res1["w_in"], name="sum_adamw_w_in_l0")
    for n in BIG:
        for key, a in zip(("grad", "delta", "new_m", "new_v"), res0[n]):
            out[key][n] = a

    gfull = {n: jnp.stack([grads[l][n] for l in range(DEPTH)]) for n in SMALL if n != "norm_final"}
    gfull["norm_final"] = g_norm_final
    small_send = _pack([gfull[n] for n in SMALL] + [loss.reshape(1)], F32)
    small_recv = _all_gather([small_send], name="gather_small_grads", after=res0["w_in"][0])[0]
    small_sum = _sum_parts(small_recv, name="sum_small")
    small_full = _unpack(small_sum, [gfull[n].shape for n in SMALL] + [(1,)])
    loss_total = small_full[-1][0]
    gsmall = {}
    for n, a in zip(SMALL, small_full[:-1]):
        if n in SHARD_AXIS:
            a = lax.dynamic_index_in_dim(_to_shards(a, SHARD_AXIS[n]), me, axis=0, keepdims=False)
        gsmall[n] = a
    small_shapes = [w[n].shape for n in SMALL]
    ws, gs, ms, vs = (_pack([t[n] for n in SMALL], F32) for t in (w, gsmall, m, v))
    ds, m1s, v1s = _adamw(ws, gs, ms, vs, name="adamw_small")
    for n in SMALL:
        out["grad"][n] = gsmall[n]
    for key, packed in (("delta", ds), ("new_m", m1s), ("new_v", v1s)):
        for n, a in zip(SMALL, _unpack(packed, small_shapes)):
            out[key][n] = a
    return loss_total, grad_x, out


def kernel(x, norm_mix, w_in, dn_conv_w, dn_a_log, dn_dt_bias, dn_norm_w, ssm_conv_w, ssm_conv_b, ssm_a_log, ssm_dt_bias, ssm_d, ssm_norm_w, w_branch, w_out, norm_mlp, w_up, w_down, norm_final, loss_target, m_norm_mix, m_w_in, m_dn_conv_w, m_dn_a_log, m_dn_dt_bias, m_dn_norm_w, m_ssm_conv_w, m_ssm_conv_b, m_ssm_a_log, m_ssm_dt_bias, m_ssm_d, m_ssm_norm_w, m_w_branch, m_w_out, m_norm_mlp, m_w_up, m_w_down, m_norm_final, v_norm_mix, v_w_in, v_dn_conv_w, v_dn_a_log, v_dn_dt_bias, v_dn_norm_w, v_ssm_conv_w, v_ssm_conv_b, v_ssm_a_log, v_ssm_dt_bias, v_ssm_d, v_ssm_norm_w, v_w_branch, v_w_out, v_norm_mlp, v_w_up, v_w_down, v_norm_final):
    w = dict(norm_mix=norm_mix, w_in=w_in, dn_conv_w=dn_conv_w, dn_a_log=dn_a_log, dn_dt_bias=dn_dt_bias, dn_norm_w=dn_norm_w,
             ssm_conv_w=ssm_conv_w, ssm_conv_b=ssm_conv_b, ssm_a_log=ssm_a_log, ssm_dt_bias=ssm_dt_bias, ssm_d=ssm_d,
             ssm_norm_w=ssm_norm_w, w_branch=w_branch, w_out=w_out, norm_mlp=norm_mlp, w_up=w_up, w_down=w_down,
             norm_final=norm_final)
    m = dict(norm_mix=m_norm_mix, w_in=m_w_in, dn_conv_w=m_dn_conv_w, dn_a_log=m_dn_a_log, dn_dt_bias=m_dn_dt_bias,
             dn_norm_w=m_dn_norm_w, ssm_conv_w=m_ssm_conv_w, ssm_conv_b=m_ssm_conv_b, ssm_a_log=m_ssm_a_log,
             ssm_dt_bias=m_ssm_dt_bias, ssm_d=m_ssm_d, ssm_norm_w=m_ssm_norm_w, w_branch=m_w_branch, w_out=m_w_out,
             norm_mlp=m_norm_mlp, w_up=m_w_up, w_down=m_w_down, norm_final=m_norm_final)
    v = dict(norm_mix=v_norm_mix, w_in=v_w_in, dn_conv_w=v_dn_conv_w, dn_a_log=v_dn_a_log, dn_dt_bias=v_dn_dt_bias,
             dn_norm_w=v_dn_norm_w, ssm_conv_w=v_ssm_conv_w, ssm_conv_b=v_ssm_conv_b, ssm_a_log=v_ssm_a_log,
             ssm_dt_bias=v_ssm_dt_bias, ssm_d=v_ssm_d, ssm_norm_w=v_ssm_norm_w, w_branch=v_w_branch, w_out=v_w_out,
             norm_mlp=v_norm_mlp, w_up=v_w_up, w_down=v_w_down, norm_final=v_norm_final)
    loss, grad_x, out = _step(w, m, v, x[0], loss_target[0])
    return (loss, grad_x[None], *[out["grad"][n] for n in WEIGHTS], *[out["delta"][n] for n in WEIGHTS],
            *[out["new_m"][n] for n in WEIGHTS], *[out["new_v"][n] for n in WEIGHTS])
```

```python
import math

import jax
import jax.numpy as jnp
from jax import lax
from jax.experimental import pallas as pl
from jax.experimental.pallas import tpu as pltpu

F32 = jnp.float32
BF16 = jnp.bfloat16
MXU_DTYPE = BF16
HIGHEST = lax.Precision.HIGHEST

N_DEV = 8
DEPTH = 2
EPS = 1e-6
CONV_K = 4
DN_HEAD_DIM = 128
SB_HEAD_DIM = 64
SSM_HEAD_DIM = 64
SSM_STATE = 128
SSM_GROUPS = 4
CHUNK = 64
SB_BLOCK = 128
LANES = 128
ADAM_LR, ADAM_B1, ADAM_B2, ADAM_EPS, ADAM_WD, ADAM_STEP = 0.001, 0.9, 0.999, 1e-08, 0.01, 10
NEG_BIG = -1e30
DN_HEADS_PER_STEP = 8
SSD_GROUPS_PER_STEP = 1
SB_UNROLL = 4
SB_SPLIT = 2

ARB = "arbitrary"


def _cparams(n_axes):
    return pltpu.CompilerParams(dimension_semantics=(ARB,) * n_axes)


def _softplus(x):
    return jnp.maximum(x, 0.0) + jnp.log1p(jnp.exp(-jnp.abs(x)))


def _sigmoid(x):
    return jax.nn.sigmoid(x)


def _silu(x):
    return x * _sigmoid(x)


def _silu_and_grad(x):
    s = _sigmoid(x)
    return x * s, s * (1.0 + x * (1.0 - s))


def _dot(a, b, dims, prec=None):
    return lax.dot_general(a, b, (dims, ((), ())), precision=prec, preferred_element_type=F32)


NN = ((1,), (0,))
NT = ((1,), (1,))
TN = ((0,), (0,))


def _mxu_dot(a, b, dims):
    return _dot(a.astype(MXU_DTYPE), b.astype(MXU_DTYPE), dims)


def _single_pass_dot(dims):
    grads = {NN: (lambda a, b, ct: (_mxu_dot(ct, b, NT), _mxu_dot(a, ct, TN))),
             NT: (lambda a, b, ct: (_mxu_dot(ct, b, NN), _mxu_dot(ct, a, TN))),
             TN: (lambda a, b, ct: (_mxu_dot(b, ct, NT), _mxu_dot(a, ct, NN)))}[dims]

    @jax.custom_vjp
    def f(a, b):
        return _mxu_dot(a, b, dims)

    f.defvjp(lambda a, b: (_mxu_dot(a, b, dims), (a, b)), lambda res, ct: grads(*res, ct))
    return f


_SDOT = {dims: _single_pass_dot(dims) for dims in (NN, NT, TN)}


def _sdot(a, b, dims=NN):
    return _SDOT[dims](a, b)


def _split_dot(a, m_bf16, nsplit=3):
    out = None
    rem = a
    for _ in range(nsplit):
        piece = rem.astype(BF16)
        rem = rem - piece.astype(F32)
        term = _dot(piece, m_bf16, NN)
        out = term if out is None else out + term
    return out


def _pick(n, pref):
    for t in pref:
        if n % t == 0:
            return t
    return n


def _matmul(a, b, *, ta=False, tb=False, name, epilogue=None, extras=(), out_dtypes=(F32,), col_shards=1, into=None,
            tm=None, tn=None, tk=None):
    m, k = (a.shape[1], a.shape[0]) if ta else a.shape
    k2, n = (b.shape[1], b.shape[0]) if tb else b.shape
    assert k == k2, (a.shape, b.shape, ta, tb)
    ncs = n // col_shards
    tm = tm or _pick(m, (1920, 1024, 512, 256, 128))
    tn = tn or _pick(ncs, (1920, 1024, 640, 512, 384, 256, 128))
    tk = tk or _pick(k, (1920, 1024, 640, 512, 256, 128))
    nk = k // tk
    a_spec = pl.BlockSpec((tk, tm), lambda i, j, kk: (kk, i)) if ta else pl.BlockSpec((tm, tk), lambda i, j, kk: (i, kk))
    b_spec = pl.BlockSpec((tn, tk), lambda i, j, kk: (j, kk)) if tb else pl.BlockSpec((tk, tn), lambda i, j, kk: (kk, j))
    e_spec = pl.BlockSpec((tm, tn), lambda i, j, kk: (i, j))
    if into is not None:
        buf, col_off = into
        off = col_off // tn
        assert col_shards == 1 and len(out_dtypes) == 1 and col_off % tn == 0 and out_dtypes[0] == buf.dtype
        o_spec, o_shape = pl.BlockSpec((tm, tn), lambda i, j, kk: (i, off + j)), buf.shape
    elif col_shards == 1:
        o_spec, o_shape = e_spec, (m, n)
    else:
        per = ncs // tn
        o_spec, o_shape = pl.BlockSpec((None, tm, tn), lambda i, j, kk: (j // per, i, j % per)), (col_shards, m, ncs)
    dims = (((0,) if ta else (1,)), ((1,) if tb else (0,)))
    n_extra = len(extras)
    n_out = len(out_dtypes)
    n_into = 0 if into is None else 1

    def body(*refs):
        a_ref, b_ref = refs[0], refs[1]
        extra_refs = refs[2:2 + n_extra]
        out_refs = refs[2 + n_extra + n_into:2 + n_extra + n_into + n_out]
        acc_ref = refs[-1]
        kk = pl.program_id(2)

        @pl.when(kk == 0)
        def _():
            acc_ref[...] = jnp.zeros_like(acc_ref)

        acc_ref[...] += _dot(a_ref[...].astype(MXU_DTYPE), b_ref[...].astype(MXU_DTYPE), dims)

        @pl.when(kk == nk - 1)
        def _():
            acc = acc_ref[...]
            outs = (acc,) if epilogue is None else epilogue(acc, *[r[...] for r in extra_refs])
            for o_ref, o in zip(out_refs, outs):
                o_ref[...] = o.astype(o_ref.dtype)

    outs = pl.pallas_call(
        body,
        grid=(m // tm, n // tn, nk),
        in_specs=[a_spec, b_spec] + [e_spec] * n_extra + [ANY] * n_into,
        out_specs=[o_spec] * n_out,
        out_shape=[jax.ShapeDtypeStruct(o_shape, dt) for dt in out_dtypes],
        input_output_aliases={2 + n_extra: 0} if n_into else {},
        scratch_shapes=[pltpu.VMEM((tm, tn), F32)],
        compiler_params=pltpu.CompilerParams(dimension_semantics=("parallel", "parallel", ARB)),
        name=name,
    )(a, b, *extras, *([] if into is None else [into[0]]))
    return outs[0] if n_out == 1 else tuple(outs)


def _rms_fwd(x, w, *, name, tm=512):
    s, d = x.shape
    out_dtype = MXU_DTYPE

    def body(x_ref, w_ref, o_ref):
        xv = x_ref[...]
        r = lax.rsqrt(jnp.mean(xv * xv, axis=-1, keepdims=True) + EPS)
        o_ref[...] = (xv * r * w_ref[...]).astype(o_ref.dtype)

    return pl.pallas_call(
        body, grid=(s // tm,),
        in_specs=[pl.BlockSpec((tm, d), lambda i: (i, 0)), pl.BlockSpec((1, d), lambda i: (0, 0))],
        out_specs=pl.BlockSpec((tm, d), lambda i: (i, 0)),
        out_shape=jax.ShapeDtypeStruct((s, d), out_dtype),
        compiler_params=_cparams(1), name=name,
    )(x, w.reshape(1, d))


def _rms_bwd(x, w, dh, dres, *, name, tm=512):
    s, d = x.shape

    def body(x_ref, w_ref, dh_ref, dres_ref, dx_ref, dw_ref):
        xv = x_ref[...]
        r = lax.rsqrt(jnp.mean(xv * xv, axis=-1, keepdims=True) + EPS)
        xh = xv * r
        dhv = dh_ref[...].astype(F32)
        dxn = dhv * w_ref[...]
        dx = r * (dxn - xh * jnp.mean(dxn * xh, axis=-1, keepdims=True))
        dx_ref[...] = dres_ref[...] + dx

        @pl.when(pl.program_id(0) == 0)
        def _():
            dw_ref[...] = jnp.zeros_like(dw_ref)

        dw_ref[...] += jnp.sum(dhv * xh, axis=0, keepdims=True)

    dx, dw = pl.pallas_call(
        body, grid=(s // tm,),
        in_specs=[pl.BlockSpec((tm, d), lambda i: (i, 0)), pl.BlockSpec((1, d), lambda i: (0, 0)),
                  pl.BlockSpec((tm, d), lambda i: (i, 0)), pl.BlockSpec((tm, d), lambda i: (i, 0))],
        out_specs=[pl.BlockSpec((tm, d), lambda i: (i, 0)), pl.BlockSpec((1, d), lambda i: (0, 0))],
        out_shape=[jax.ShapeDtypeStruct((s, d), F32), jax.ShapeDtypeStruct((1, d), F32)],
        compiler_params=_cparams(1), name=name,
    )(x, w.reshape(1, d), dh, dres)
    return dx, dw.reshape(d)


def _final_loss(x, w, target, *, name, tm=512):
    s, d = x.shape

    def body(x_ref, w_ref, t_ref, loss_ref, dx_ref, dw_ref):
        xv = x_ref[...]
        r = lax.rsqrt(jnp.mean(xv * xv, axis=-1, keepdims=True) + EPS)
        xh = xv * r
        err = xh * w_ref[...] - t_ref[...]
        dy = err * (1.0 / d)
        dxn = dy * w_ref[...]
        dx_ref[...] = r * (dxn - xh * jnp.mean(dxn * xh, axis=-1, keepdims=True))

        @pl.when(pl.program_id(0) == 0)
        def _():
            dw_ref[...] = jnp.zeros_like(dw_ref)
            loss_ref[...] = jnp.zeros_like(loss_ref)

        dw_ref[...] += jnp.sum(dy * xh, axis=0, keepdims=True)
        row = jnp.sum(err * err, axis=1, keepdims=True) * (0.5 / d)
        loss_ref[...] += jnp.sum(row, axis=0, keepdims=True)

    loss, dx, dw = pl.pallas_call(
        body, grid=(s // tm,),
        in_specs=[pl.BlockSpec((tm, d), lambda i: (i, 0)), pl.BlockSpec((1, d), lambda i: (0, 0)),
                  pl.BlockSpec((tm, d), lambda i: (i, 0))],
        out_specs=[pl.BlockSpec((1, 1), lambda i: (0, 0)), pl.BlockSpec((tm, d), lambda i: (i, 0)),
                   pl.BlockSpec((1, d), lambda i: (0, 0))],
        out_shape=[jax.ShapeDtypeStruct((1, 1), F32), jax.ShapeDtypeStruct((s, d), F32), jax.ShapeDtypeStruct((1, d), F32)],
        compiler_params=_cparams(1), name=name,
    )(x, w.reshape(1, d), target)
    return loss[0, 0], dx, dw.reshape(d)


def _shift_down(x, sh, t_idx):
    return jnp.where(t_idx >= sh, pltpu.roll(x, sh, 0), 0.0)


def _shift_up(x, sh, t_idx, s):
    return jnp.where(t_idx < s - sh, pltpu.roll(x, s - sh, 0), 0.0)


def _conv_pre(x, w_rows, b, t_idx):
    c = w_rows[CONV_K - 1] * x + b
    for sh in range(1, CONV_K):
        c = c + w_rows[CONV_K - 1 - sh] * _shift_down(x, sh, t_idx)
    return c


def _conv_fwd(src, col0, w, b, n_l2, *, name):
    s = src.shape[0]
    c_tot = w.shape[1]
    nblk = c_tot // LANES

    def body(x_ref, w_ref, b_ref, o_ref):
        j = pl.program_id(0)
        t_idx = lax.broadcasted_iota(jnp.int32, (s, LANES), 0)
        w_rows = [w_ref[kk:kk + 1, :] for kk in range(CONV_K)]
        y = _silu(_conv_pre(x_ref[...], w_rows, b_ref[...], t_idx))
        if n_l2 > 0:
            yn = y * lax.rsqrt(jnp.sum(y * y, axis=1, keepdims=True) + EPS)
            y = jnp.where(j < n_l2, yn, y)
        o_ref[...] = y

    return pl.pallas_call(
        body, grid=(nblk,),
        in_specs=[pl.BlockSpec((s, LANES), lambda j: (0, col0 + j)), pl.BlockSpec((CONV_K, LANES), lambda j: (0, j)),
                  pl.BlockSpec((1, LANES), lambda j: (0, j))],
        out_specs=pl.BlockSpec((s, LANES), lambda j: (0, j)),
        out_shape=jax.ShapeDtypeStruct((s, c_tot), F32),
        compiler_params=_cparams(1), name=name,
    )(src, w, b)


def _conv_bwd(src, col0, w, b, n_l2, dout, into, *, name):
    s = src.shape[0]
    c_tot = w.shape[1]
    nblk = c_tot // LANES

    def body(x_ref, w_ref, b_ref, do_ref, into_ref, dx_ref, dw_ref, db_ref):
        j = pl.program_id(0)
        t_idx = lax.broadcasted_iota(jnp.int32, (s, LANES), 0)
        xv = x_ref[...]
        w_rows = [w_ref[kk:kk + 1, :] for kk in range(CONV_K)]
        c = _conv_pre(xv, w_rows, b_ref[...], t_idx)
        dy = do_ref[...]
        y, y_grad = _silu_and_grad(c)
        if n_l2 > 0:
            r = lax.rsqrt(jnp.sum(y * y, axis=1, keepdims=True) + EPS)
            dyn = r * dy - y * (r * r * r) * jnp.sum(dy * y, axis=1, keepdims=True)
            dy = jnp.where(j < n_l2, dyn, dy)
        dc = dy * y_grad
        dx = w_rows[CONV_K - 1] * dc
        rows = [None] * CONV_K
        rows[CONV_K - 1] = jnp.sum(dc * xv, axis=0, keepdims=True)
        for sh in range(1, CONV_K):
            dx = dx + w_rows[CONV_K - 1 - sh] * _shift_up(dc, sh, t_idx, s)
            rows[CONV_K - 1 - sh] = jnp.sum(dc * _shift_down(xv, sh, t_idx), axis=0, keepdims=True)
        dx_ref[...] = dx.astype(dx_ref.dtype)
        for kk in range(CONV_K):
            dw_ref[kk:kk + 1, :] = rows[kk]
        db_ref[...] = jnp.sum(dc, axis=0, keepdims=True)

    return pl.pallas_call(
        body, grid=(nblk,),
        in_specs=[pl.BlockSpec((s, LANES), lambda j: (0, col0 + j)), pl.BlockSpec((CONV_K, LANES), lambda j: (0, j)),
                  pl.BlockSpec((1, LANES), lambda j: (0, j)), pl.BlockSpec((s, LANES), lambda j: (0, j)), ANY],
        out_specs=[pl.BlockSpec((s, LANES), lambda j: (0, col0 + j)), pl.BlockSpec((CONV_K, LANES), lambda j: (0, j)),
                   pl.BlockSpec((1, LANES), lambda j: (0, j))],
        out_shape=[jax.ShapeDtypeStruct(into.shape, into.dtype), jax.ShapeDtypeStruct((CONV_K, c_tot), F32),
                   jax.ShapeDtypeStruct((1, c_tot), F32)],
        input_output_aliases={4: 0},
        compiler_params=_cparams(1), name=name,
    )(src, w, b, dout, into)


def _chunk_masks(c):
    ii = lax.broadcasted_iota(jnp.int32, (c, c), 0)
    jj = lax.broadcasted_iota(jnp.int32, (c, c), 1)
    return ii, jj


def _row_to_col(row, eye):
    return jnp.sum(jnp.where(eye, row, 0.0), axis=1, keepdims=True)


def _each(f, *lists):
    return [f(*xs) for xs in zip(*lists)]


@jax.custom_vjp
def _nilpotent_inverse(nmats):
    c = nmats[0].shape[0]
    ii, jj = _chunk_masks(c)
    xinv = _each(lambda n: jnp.where(ii == jj, 1.0, 0.0) + n, nmats)
    pw = nmats
    for _ in range(int(math.log2(c)) - 1):
        pw = _each(lambda p: _dot(p, p, NN, HIGHEST), pw)
        xinv = _each(lambda x, p: x + _dot(x, p, NN, HIGHEST), xinv, pw)
    return xinv


def _nilpotent_inverse_fwd(nmats):
    xinv = _nilpotent_inverse(nmats)
    return xinv, xinv


def _nilpotent_inverse_bwd(xinv, cts):
    left = _each(lambda x, ct: _dot(x, ct, TN, HIGHEST), xinv, cts)
    return (_each(lambda l_, x: _dot(l_, x, NT, HIGHEST), left, xinv),)


_nilpotent_inverse.defvjp(_nilpotent_inverse_fwd, _nilpotent_inverse_bwd)


@jax.custom_vjp
def _saved_inverse(nmats, saved):
    return saved


def _saved_inverse_fwd(nmats, saved):
    return saved, saved


def _saved_inverse_bwd(xinv, cts):
    return _nilpotent_inverse_bwd(xinv, cts) + (_each(jnp.zeros_like, xinv),)


_saved_inverse.defvjp(_saved_inverse_fwd, _saved_inverse_bwd)


def _dn_chunk(q, k, v, a_row, b_row, alog, dtb, s0, saved_inverse=None):
    c = q[0].shape[0]
    ii, jj = _chunk_masks(c)
    causal, strict, eye = ii >= jj, ii > jj, ii == jj
    g_row = _each(lambda al, a, dt: -jnp.exp(al) * _softplus(a + dt), alog, a_row, dtb)
    beta_col = _each(lambda b: _row_to_col(_sigmoid(b), eye), b_row)
    g_col = _each(lambda g: _row_to_col(g, eye), g_row)
    gc_col = _each(lambda g: jnp.sum(jnp.where(causal, g, 0.0), axis=1, keepdims=True), g_row)
    gc_row = _each(lambda g: jnp.sum(jnp.where(jj >= ii, g, 0.0), axis=0, keepdims=True), g_col)
    decay = _each(lambda gc, gr: jnp.exp(jnp.where(causal, gc - gr, NEG_BIG)), gc_col, gc_row)
    kb = _each(jnp.multiply, k, beta_col)
    vb = _each(jnp.multiply, v, beta_col)
    nmat = _each(lambda kb_, k_, dc: -jnp.where(strict, _dot(kb_, k_, NT, HIGHEST) * dc, 0.0), kb, k, decay)
    xinv = _nilpotent_inverse(nmat) if saved_inverse is None else _saved_inverse(nmat, saved_inverse)
    egc = _each(jnp.exp, gc_col)
    dv = v[0].shape[1]
    uw = _each(lambda x, vb_, kb_, e: _dot(x, jnp.concatenate([vb_, kb_ * e], axis=1), NN, HIGHEST), xinv, vb, kb, egc)
    u = _each(lambda t: t[:, :dv], uw)
    w = _each(lambda t: t[:, dv:], uw)
    qs = _each(lambda q_: q_ * (q_.shape[1] ** -0.5), q)
    attn = _each(lambda q_, k_, dc: _sdot(q_, k_, NT) * dc, qs, k, decay)
    gl = _each(lambda g: jnp.sum(g, axis=1, keepdims=True), g_row)
    kd = _each(lambda k_, gl_, gc: k_ * jnp.exp(gl_ - gc), k, gl, gc_col)
    v_new = _each(lambda u_, w_, s: u_ - _sdot(w_, s), u, w, s0)
    o = _each(lambda q_, e, s, at, vn: _sdot(q_ * e, s) + _sdot(at, vn), qs, egc, s0, attn, v_new)
    s1 = _each(lambda s, gl_, kd_, vn: s * jnp.exp(gl_) + _sdot(kd_, vn, TN), s0, gl, kd, v_new)
    return (o, s1), xinv


def _dn_specs(nh, nc, hb, rev):
    n_of = (lambda n: nc - 1 - n) if rev else (lambda n: n)
    ng = nh // hb
    qkv = [pl.BlockSpec((CHUNK, hb * DN_HEAD_DIM), (lambda h, n, o=o: (n_of(n), o * ng + h))) for o in range(3)]
    row = pl.BlockSpec((hb, None, 1, CHUNK), lambda h, n: (h, n_of(n), 0, 0))
    scal = pl.BlockSpec((hb, 1, 1), lambda h, n: (h, 0, 0))
    o_spec = pl.BlockSpec((CHUNK, hb * DN_HEAD_DIM), lambda h, n: (n_of(n), h))
    st = pl.BlockSpec((hb, None, DN_HEAD_DIM, DN_HEAD_DIM), lambda h, n: (h, n_of(n), 0, 0))
    inv = pl.BlockSpec((hb, None, CHUNK, CHUNK), lambda h, n: (h, n_of(n), 0, 0))
    return qkv, row, scal, o_spec, st, inv


def _dn_fwd(qkv, a_rows, b_rows, alog, dtb, *, name):
    s = qkv.shape[0]
    nh, nc = a_rows.shape[0], a_rows.shape[1]
    hb = min(DN_HEADS_PER_STEP, nh)
    qkv_specs, row, scal, o_spec, st, inv = _dn_specs(nh, nc, hb, False)
    hd = DN_HEAD_DIM

    def body(q_ref, k_ref, v_ref, a_ref, b_ref, al_ref, dt_ref, o_ref, st_ref, inv_ref, state):
        @pl.when(pl.program_id(1) == 0)
        def _():
            state[...] = jnp.zeros_like(state)

        cols = [slice(h * hd, (h + 1) * hd) for h in range(hb)]
        s0 = [state[h] for h in range(hb)]
        for h in range(hb):
            st_ref[h] = s0[h]
        (o, s1), xinv = _dn_chunk(
            [q_ref[:, cl] for cl in cols], [k_ref[:, cl] for cl in cols], [v_ref[:, cl] for cl in cols],
            [a_ref[h] for h in range(hb)], [b_ref[h] for h in range(hb)],
            [al_ref[h] for h in range(hb)], [dt_ref[h] for h in range(hb)], s0)
        for h in range(hb):
            o_ref[:, cols[h]] = o[h]
            inv_ref[h] = xinv[h]
            state[h] = s1[h]

    return pl.pallas_call(
        body, grid=(nh // hb, nc),
        in_specs=qkv_specs + [row, row, scal, scal],
        out_specs=[o_spec, st, inv],
        out_shape=[jax.ShapeDtypeStruct((s, nh * hd), F32), jax.ShapeDtypeStruct((nh, nc, hd, hd), F32),
                   jax.ShapeDtypeStruct((nh, nc, CHUNK, CHUNK), F32)],
        scratch_shapes=[pltpu.VMEM((hb, hd, hd), F32)],
        compiler_params=_cparams(2), name=name,
    )(qkv, qkv, qkv, a_rows, b_rows, alog, dtb)


def _dn_bwd(qkv, a_rows, b_rows, alog, dtb, states, inverses, do, *, name):
    s = qkv.shape[0]
    nh, nc = a_rows.shape[0], a_rows.shape[1]
    hb = min(DN_HEADS_PER_STEP, nh)
    qkv_specs, row, scal, o_spec, st, inv = _dn_specs(nh, nc, hb, True)
    hd = DN_HEAD_DIM

    assert hb == nh, "dq | dk | dv are written as one [S, 3W] array: all heads in one grid step"
    w = nh * hd

    def body(q_ref, k_ref, v_ref, a_ref, b_ref, al_ref, dt_ref, st_ref, inv_ref, do_ref,
             dqkv_ref, da_ref, db_ref, dal_ref, ddt_ref, dstate):
        @pl.when(pl.program_id(1) == 0)
        def _():
            dstate[...] = jnp.zeros_like(dstate)
            dal_ref[...] = jnp.zeros_like(dal_ref)
            ddt_ref[...] = jnp.zeros_like(ddt_ref)

        cols = [slice(h * hd, (h + 1) * hd) for h in range(hb)]
        heads = range(hb)
        args = ([q_ref[:, cl] for cl in cols], [k_ref[:, cl] for cl in cols], [v_ref[:, cl] for cl in cols],
                [a_ref[h] for h in heads], [b_ref[h] for h in heads], [al_ref[h] for h in heads],
                [dt_ref[h] for h in heads], [st_ref[h] for h in heads])
        saved = [inv_ref[h] for h in heads]
        _, vjp, _ = jax.vjp(lambda *a: _dn_chunk(*a, saved_inverse=saved), *args, has_aux=True)
        dq, dk, dv, da, db, dal, ddt, ds0 = vjp(([do_ref[:, cl] for cl in cols], [dstate[h] for h in heads]))
        for h in heads:
            dqkv_ref[:, h * hd:(h + 1) * hd] = dq[h]
            dqkv_ref[:, w + h * hd:w + (h + 1) * hd] = dk[h]
            dqkv_ref[:, 2 * w + h * hd:2 * w + (h + 1) * hd] = dv[h]
            da_ref[h] = da[h]
            db_ref[h] = db[h]
            dal_ref[h] += dal[h]
            ddt_ref[h] += ddt[h]
            dstate[h] = ds0[h]

    n_of = lambda n: nc - 1 - n
    outs = pl.pallas_call(
        body, grid=(nh // hb, nc),
        in_specs=qkv_specs + [row, row, scal, scal, st, inv, o_spec],
        out_specs=[pl.BlockSpec((CHUNK, 3 * w), lambda h, n: (n_of(n), 0)), row, row, scal, scal],
        out_shape=[jax.ShapeDtypeStruct((s, 3 * w), F32)]
        + [jax.ShapeDtypeStruct(a_rows.shape, F32)] * 2 + [jax.ShapeDtypeStruct((nh, 1, 1), F32)] * 2,
        scratch_shapes=[pltpu.VMEM((hb, hd, hd), F32)],
        compiler_params=_cparams(2), name=name,
    )(qkv, qkv, qkv, a_rows, b_rows, alog, dtb, states, inverses, do)
    return outs


def _dn_post_fwd(o, src, gate_col0, nw, *, name, tm=512):
    s, w = o.shape
    hd = DN_HEAD_DIM
    gc = gate_col0 * LANES // w

    def body(o_ref, g_ref, w_ref, y_ref):
        for h in range(w // hd):
            cols = slice(h * hd, (h + 1) * hd)
            ov = o_ref[:, cols]
            r = lax.rsqrt(jnp.mean(ov * ov, axis=-1, keepdims=True) + EPS)
            y_ref[:, cols] = (ov * r * w_ref[...] * _silu(g_ref[:, cols])).astype(y_ref.dtype)

    blk = pl.BlockSpec((tm, w), lambda i: (i, 0))
    return pl.pallas_call(
        body, grid=(s // tm,),
        in_specs=[blk, pl.BlockSpec((tm, w), lambda i: (i, gc)), pl.BlockSpec((1, hd), lambda i: (0, 0))],
        out_specs=blk, out_shape=jax.ShapeDtypeStruct((s, w), MXU_DTYPE),
        compiler_params=_cparams(1), name=name,
    )(o, src, nw.reshape(1, hd))


def _dn_post_bwd(o, src, gate_col0, nw, dy, into, *, name, tm=512):
    s, w = o.shape
    hd = DN_HEAD_DIM
    gc = gate_col0 * LANES // w

    def body(o_ref, g_ref, w_ref, dy_ref, into_ref, do_ref, dg_ref, dw_ref):
        @pl.when(pl.program_id(0) == 0)
        def _():
            dw_ref[...] = jnp.zeros_like(dw_ref)

        dw = jnp.zeros((1, hd), F32)
        for h in range(w // hd):
            cols = slice(h * hd, (h + 1) * hd)
            ov, gv, dyv = o_ref[:, cols], g_ref[:, cols], dy_ref[:, cols]
            r = lax.rsqrt(jnp.mean(ov * ov, axis=-1, keepdims=True) + EPS)
            oh = ov * r
            sg, sg_grad = _silu_and_grad(gv)
            dn = dyv * sg
            dg_ref[:, cols] = (dyv * (oh * w_ref[...]) * sg_grad).astype(dg_ref.dtype)
            don = dn * w_ref[...]
            do_ref[:, cols] = r * (don - oh * jnp.mean(don * oh, axis=-1, keepdims=True))
            dw = dw + jnp.sum(dn * oh, axis=0, keepdims=True)
        dw_ref[...] += dw

    blk = pl.BlockSpec((tm, w), lambda i: (i, 0))
    wspec = pl.BlockSpec((1, hd), lambda i: (0, 0))
    gate_blk = pl.BlockSpec((tm, w), lambda i: (i, gc))
    do, dg, dw = pl.pallas_call(
        body, grid=(s // tm,),
        in_specs=[blk, gate_blk, wspec, blk, ANY],
        out_specs=[blk, gate_blk, wspec],
        out_shape=[jax.ShapeDtypeStruct((s, w), F32), jax.ShapeDtypeStruct(into.shape, into.dtype),
                   jax.ShapeDtypeStruct((1, hd), F32)],
        input_output_aliases={4: 1},
        compiler_params=_cparams(1), name=name,
    )(o, src, nw.reshape(1, hd), dy, into)
    return do, dg, dw.reshape(hd)


def _sb_consts():
    r2 = lax.broadcasted_iota(jnp.int32, (2 * SB_BLOCK, SB_BLOCK), 0)
    c2 = lax.broadcasted_iota(jnp.int32, (2 * SB_BLOCK, SB_BLOCK), 1)
    r = lax.broadcasted_iota(jnp.int32, (SB_BLOCK, SB_BLOCK), 0)
    c = lax.broadcasted_iota(jnp.int32, (SB_BLOCK, SB_BLOCK), 1)
    lm0 = c < SB_HEAD_DIM
    m_gt = jnp.where(r > c, 1.0, 0.0).astype(BF16)
    m_lt = jnp.where(r < c, 1.0, 0.0).astype(BF16)
    return r2, c2, lm0, m_gt, m_lt


def _sb_stack(x, lm0):
    return jnp.concatenate([jnp.where(lm0, x, 0.0), jnp.where(lm0, 0.0, x)], axis=0)


def _sb_unstack(x2, lm0):
    return jnp.where(lm0, x2[:SB_BLOCK], x2[SB_BLOCK:])


def _sb_fwd(src, col0, width, *, name):
    s = src.shape[0]
    nq = s // SB_BLOCK
    npair = width // LANES
    scale = SB_HEAD_DIM ** -0.5
    nu = math.gcd(SB_UNROLL, nq)

    def body(q_ref, k_ref, v_ref, o_ref, w_hbm, stage, sems):
        p, i = pl.program_id(0), pl.program_id(1)
        r2, c2, lm0, m_gt, _ = _sb_consts()
        t_glob = i * SB_BLOCK + (r2 & (SB_BLOCK - 1))
        q2 = (_sb_stack(q_ref[...], lm0) * scale).astype(MXU_DTYPE)

        t = p * nq + i
        half = t % 2
        ngrp = nq // nu

        def save(half_, grp, pp, ii):
            return pltpu.make_async_copy(stage.at[half_, grp], w_hbm.at[pp, ii, grp], sems.at[half_, grp])

        def drain(half_, pp, ii):
            for grp in range(ngrp):
                @pl.when(grp <= ii // nu)
                def _():
                    save(half_, grp, pp, ii).wait()

        def group(base, carry, masked):
            o2, rsum = carry
            js = [base + nu - 1 - u for u in range(nu)]
            offs = [pl.multiple_of(j * SB_BLOCK, SB_BLOCK) for j in js]
            zs = [_dot(q2, k_ref[pl.ds(off, SB_BLOCK), :].astype(MXU_DTYPE), NT) for off in offs]
            ts = [jnp.log(1.0 + jnp.exp(-jnp.abs(z))) for z in zs]
            lks = [-(jnp.maximum(z, 0.0) + t) for z, t in zip(zs, ts)]
            if masked:
                masks = [(j * SB_BLOCK + c2) < t_glob for j in js]
                lks = [jnp.where(mk, lk, 0.0) for mk, lk in zip(masks, lks)]
            sufs = [_split_dot(lk, m_gt, SB_SPLIT) for lk in lks]
            rs = [rsum]
            for lk in lks:
                rs.append(rs[-1] + jnp.sum(lk, axis=1, keepdims=True))
            wgts = [jnp.exp((jnp.minimum(z, 0.0) - t) + r_ + sf) for z, t, r_, sf in zip(zs, ts, rs, sufs)]
            if masked:
                wgts = [jnp.where(mk, wg, 0.0) for mk, wg in zip(masks, wgts)]
            wbs = [wg.astype(MXU_DTYPE) for wg in wgts]
            grp = base // nu
            for u, wb in enumerate(wbs):
                stage[half, grp, nu - 1 - u] = wb
            save(half, grp, p, i).start()
            for off, wb in zip(offs, wbs):
                o2 = o2 + _dot(wb, v_ref[pl.ds(off, SB_BLOCK), :].astype(MXU_DTYPE), NN)
            return o2, rs[-1]

        top0 = (i // nu) * nu
        last = i // nu
        carry = group(top0, (jnp.zeros((2 * SB_BLOCK, LANES), F32), jnp.zeros((2 * SB_BLOCK, 1), F32)), True)
        o2, _ = lax.fori_loop(1, last + 1, lambda g, cr: group(top0 - nu * g, cr, False), carry)
        o_ref[...] = _sb_unstack(o2, lm0)

        @pl.when(t >= 1)
        def _():
            drain(1 - half, (t - 1) // nq, (t - 1) % nq)

        @pl.when(t == npair * nq - 1)
        def _():
            drain(half, p, i)

    blk = pl.BlockSpec((SB_BLOCK, LANES), lambda p, i: (i, p))
    return pl.pallas_call(
        body, grid=(npair, nq),
        in_specs=[pl.BlockSpec((SB_BLOCK, LANES), lambda p, i: (i, col0 + p)),
                  pl.BlockSpec((s, LANES), lambda p, i: (0, col0 + npair + p)),
                  pl.BlockSpec((s, LANES), lambda p, i: (0, col0 + 2 * npair + p))],
        out_specs=[blk, ANY],
        out_shape=[jax.ShapeDtypeStruct((s, width), F32),
                   jax.ShapeDtypeStruct((npair, nq, nq // nu, nu, 2 * SB_BLOCK, LANES), MXU_DTYPE)],
        scratch_shapes=[pltpu.VMEM((2, nq // nu, nu, 2 * SB_BLOCK, LANES), MXU_DTYPE),
                        pltpu.SemaphoreType.DMA((2, nq // nu))],
        compiler_params=_cparams(2), name=name,
    )(src, src, src)


def _sb_bwd(src, col0, width, weights, do, *, name):
    s = src.shape[0]
    nq = s // SB_BLOCK
    npair = width // LANES
    scale = SB_HEAD_DIM ** -0.5
    nu = math.gcd(SB_UNROLL, nq)

    def body(q_ref, k_ref, v_ref, w_hbm, do_ref, dq_ref, dk_ref, dv_ref, stage, sems):
        p, i = pl.program_id(0), pl.program_id(1)

        @pl.when(i == 0)
        def _():
            dk_ref[...] = jnp.zeros_like(dk_ref)
            dv_ref[...] = jnp.zeros_like(dv_ref)

        r2, c2, lm0, _, m_lt = _sb_consts()
        t_glob = i * SB_BLOCK + (r2 & (SB_BLOCK - 1))
        q2 = (_sb_stack(q_ref[...], lm0) * scale).astype(MXU_DTYPE)
        do2 = _sb_stack(do_ref[...], lm0).astype(MXU_DTYPE)

        ngrp = nq // nu

        def load(half_, grp, pp, ii):
            return pltpu.make_async_copy(w_hbm.at[pp, ii, grp], stage.at[half_, grp], sems.at[half_, grp])

        def fetch_step(half_, pp, ii):
            for grp in range(ngrp):
                @pl.when(grp <= ii // nu)
                def _():
                    load(half_, grp, pp, ii).start()

        def group(g, carry, masked, slot):
            dq2, csum = carry
            js = [nu * g + u for u in range(nu)]
            offs = [pl.multiple_of(j * SB_BLOCK, SB_BLOCK) for j in js]
            kbs = [k_ref[pl.ds(off, SB_BLOCK), :].astype(MXU_DTYPE) for off in offs]
            zs = [_dot(q2, kb, NT) for kb in kbs]
            dws = [_dot(do2, v_ref[pl.ds(off, SB_BLOCK), :].astype(MXU_DTYPE), NT) for off in offs]
            wbs = [stage[slot[0], slot[1], u] for u in range(nu)]
            sigs = [_sigmoid(z) for z in zs]
            dlogas = [wb.astype(F32) * dw for wb, dw in zip(wbs, dws)]
            pres = [_split_dot(dl, m_lt, SB_SPLIT) for dl in dlogas]
            dlks = []
            for dl, pre in zip(dlogas, pres):
                dlks.append(csum + pre)
                csum = csum + jnp.sum(dl, axis=1, keepdims=True)
            if masked:
                dlks = [jnp.where((j * SB_BLOCK + c2) < t_glob, dlk, 0.0) for j, dlk in zip(js, dlks)]
            dzbs = [(dl * (1.0 - sg) - dlk * sg).astype(MXU_DTYPE) for dl, sg, dlk in zip(dlogas, sigs, dlks)]
            for off, dzb, wb, kb in zip(offs, dzbs, wbs, kbs):
                dk_ref[pl.ds(off, SB_BLOCK), :] += _dot(dzb, q2, TN)
                dv_ref[pl.ds(off, SB_BLOCK), :] += _dot(wb, do2, TN)
                dq2 = dq2 + _dot(dzb, kb, NN)
            return dq2, csum

        t = p * nq + i
        half = t % 2

        @pl.when(t == 0)
        def _():
            fetch_step(0, p, i)

        @pl.when(t + 1 < npair * nq)
        def _():
            fetch_step(1 - half, (t + 1) // nq, (t + 1) % nq)

        def step(g, carry):
            load(half, g, p, i).wait()
            return group(g, carry, False, (half, g))

        last = i // nu
        carry = lax.fori_loop(0, last, step, (jnp.zeros((2 * SB_BLOCK, LANES), F32), jnp.zeros((2 * SB_BLOCK, 1), F32)))
        load(half, last, p, i).wait()
        dq2, _ = group(last, carry, True, (half, last))
        dq_ref[...] = _sb_unstack(dq2, lm0) * scale

    blk = pl.BlockSpec((SB_BLOCK, LANES), lambda p, i: (i, p))
    full = pl.BlockSpec((s, LANES), lambda p, i: (0, p))
    return pl.pallas_call(
        body, grid=(npair, nq),
        in_specs=[pl.BlockSpec((SB_BLOCK, LANES), lambda p, i: (i, col0 + p)),
                  pl.BlockSpec((s, LANES), lambda p, i: (0, col0 + npair + p)),
                  pl.BlockSpec((s, LANES), lambda p, i: (0, col0 + 2 * npair + p)),
                  ANY, blk],
        out_specs=[blk, full, full],
        out_shape=[jax.ShapeDtypeStruct((s, width), F32)] * 3,
        scratch_shapes=[pltpu.VMEM((2, nq // nu, nu, 2 * SB_BLOCK, LANES), MXU_DTYPE),
                        pltpu.SemaphoreType.DMA((2, nq // nu))],
        compiler_params=_cparams(2), name=name,
    )(src, src, src, weights, do)


def _ssd_group(xs, dt_rows, alogs, dtbs, bms, cms, h0s):
    c = bms[0].shape[0]
    per = len(xs) // len(bms)
    ii, jj = _chunk_masks(c)
    causal, eye = ii >= jj, ii == jj
    scores = [t for t in _each(lambda c_, b_: _sdot(c_, b_, NT), cms, bms) for _ in range(per)]
    dt_r = _each(lambda dt, b: _softplus(dt + b), dt_rows, dtbs)
    a_r = _each(lambda al, dt: -jnp.exp(al) * dt, alogs, dt_r)
    dt_col = _each(lambda dt: _row_to_col(dt, eye), dt_r)
    a_col = _each(lambda a: _row_to_col(a, eye), a_r)
    ac_col = _each(lambda a: jnp.sum(jnp.where(causal, a, 0.0), axis=1, keepdims=True), a_r)
    ac_row = _each(lambda a: jnp.sum(jnp.where(jj >= ii, a, 0.0), axis=0, keepdims=True), a_col)
    lmat = _each(lambda c_, r_: jnp.exp(jnp.where(causal, c_ - r_, NEG_BIG)), ac_col, ac_row)
    xdt = _each(jnp.multiply, xs, dt_col)
    al = _each(lambda a: jnp.sum(a, axis=1, keepdims=True), a_r)
    bm = [t for t in bms for _ in range(per)]
    cm = [t for t in cms for _ in range(per)]
    ys = _each(lambda sc, lm, xd, cm_, h0, ac: _sdot(sc * lm, xd) + _sdot(cm_, h0, NT) * jnp.exp(ac),
               scores, lmat, xdt, cm, h0s, ac_col)
    h1s = _each(lambda h0, al_, xd, ac, bm_: h0 * jnp.exp(al_) + _sdot(xd * jnp.exp(al_ - ac), bm_, TN),
                h0s, al, xdt, ac_col, bm)
    return ys, h1s


def _ssd_specs(ng, nc, r, gb, rev):
    n_of = (lambda n: nc - 1 - n) if rev else (lambda n: n)
    xw, bw = gb * r * SSM_HEAD_DIM, gb * SSM_STATE
    b0, c0 = (ng * r * SSM_HEAD_DIM) // bw, (ng * r * SSM_HEAD_DIM + ng * SSM_STATE) // bw
    x_spec = pl.BlockSpec((CHUNK, xw), lambda g, n: (n_of(n), g))
    b_spec = pl.BlockSpec((CHUNK, bw), lambda g, n: (n_of(n), b0 + g))
    c_spec = pl.BlockSpec((CHUNK, bw), lambda g, n: (n_of(n), c0 + g))
    dt_spec = pl.BlockSpec((gb, None, r, CHUNK), lambda g, n: (g, n_of(n), 0, 0))
    sc_spec = pl.BlockSpec((gb, r, 1), lambda g, n: (g, 0, 0))
    st_spec = pl.BlockSpec((gb, None, r, SSM_HEAD_DIM, SSM_STATE), lambda g, n: (g, n_of(n), 0, 0, 0))
    bc_out = pl.BlockSpec((CHUNK, bw), lambda g, n: (n_of(n), g))
    return x_spec, b_spec, c_spec, dt_spec, sc_spec, st_spec, x_spec, bc_out


def _ssd_refs(gb, r, x_ref, b_ref, c_ref, dt_ref, al_ref, db_ref):
    p, n = SSM_HEAD_DIM, SSM_STATE
    heads = [(g, h) for g in range(gb) for h in range(r)]
    xs = [x_ref[:, (g * r + h) * p:(g * r + h + 1) * p] for g, h in heads]
    dts = [dt_ref[g, h:h + 1, :] for g, h in heads]
    als = [al_ref[g, h:h + 1, :] for g, h in heads]
    dbs = [db_ref[g, h:h + 1, :] for g, h in heads]
    bms = [b_ref[:, g * n:(g + 1) * n] for g in range(gb)]
    cms = [c_ref[:, g * n:(g + 1) * n] for g in range(gb)]
    return heads, xs, dts, als, dbs, bms, cms


def _ssd_fwd(xbc, dt_rows, alog, dtb, *, name):
    s = xbc.shape[0]
    ng, nc, r = dt_rows.shape[0], dt_rows.shape[1], dt_rows.shape[2]
    w = ng * r * SSM_HEAD_DIM
    gb = math.gcd(SSD_GROUPS_PER_STEP, ng)
    x_spec, b_spec, c_spec, dt_spec, sc_spec, st_spec, y_spec, _ = _ssd_specs(ng, nc, r, gb, False)
    p = SSM_HEAD_DIM

    def body(x_ref, b_ref, c_ref, dt_ref, al_ref, db_ref, y_ref, st_ref, state):
        @pl.when(pl.program_id(1) == 0)
        def _():
            state[...] = jnp.zeros_like(state)

        st_ref[...] = state[...]
        heads, xs, dts, als, dbs, bms, cms = _ssd_refs(gb, r, x_ref, b_ref, c_ref, dt_ref, al_ref, db_ref)
        ys, h1s = _ssd_group(xs, dts, als, dbs, bms, cms, [state[g, h] for g, h in heads])
        for i, (g, h) in enumerate(heads):
            y_ref[:, (g * r + h) * p:(g * r + h + 1) * p] = ys[i]
            state[g, h] = h1s[i]

    return pl.pallas_call(
        body, grid=(ng // gb, nc),
        in_specs=[x_spec, b_spec, c_spec, dt_spec, sc_spec, sc_spec],
        out_specs=[y_spec, st_spec],
        out_shape=[jax.ShapeDtypeStruct((s, w), F32), jax.ShapeDtypeStruct((ng, nc, r, p, SSM_STATE), F32)],
        scratch_shapes=[pltpu.VMEM((gb, r, p, SSM_STATE), F32)],
        compiler_params=_cparams(2), name=name,
    )(xbc, xbc, xbc, dt_rows, alog, dtb)


def _ssd_bwd(xbc, dt_rows, alog, dtb, states, dy, *, name):
    s = xbc.shape[0]
    ng, nc, r = dt_rows.shape[0], dt_rows.shape[1], dt_rows.shape[2]
    w = ng * r * SSM_HEAD_DIM
    gb = math.gcd(SSD_GROUPS_PER_STEP, ng)
    x_spec, b_spec, c_spec, dt_spec, sc_spec, st_spec, y_spec, bc_out = _ssd_specs(ng, nc, r, gb, True)
    p = SSM_HEAD_DIM

    def body(x_ref, b_ref, c_ref, dt_ref, al_ref, db_ref, st_ref, dy_ref,
             dx_ref, dbm_ref, dcm_ref, ddt_ref, dal_ref, ddb_ref, dstate):
        @pl.when(pl.program_id(1) == 0)
        def _():
            dstate[...] = jnp.zeros_like(dstate)
            dal_ref[...] = jnp.zeros_like(dal_ref)
            ddb_ref[...] = jnp.zeros_like(ddb_ref)

        heads, xs, dts, als, dbs, bms, cms = _ssd_refs(gb, r, x_ref, b_ref, c_ref, dt_ref, al_ref, db_ref)
        _, vjp = jax.vjp(_ssd_group, xs, dts, als, dbs, bms, cms, [st_ref[g, h] for g, h in heads])
        dys = [dy_ref[:, (g * r + h) * p:(g * r + h + 1) * p] for g, h in heads]
        dxs, ddts, dals, ddbs, dbms, dcms, dh0s = vjp((dys, [dstate[g, h] for g, h in heads]))
        for g in range(gb):
            dbm_ref[:, g * SSM_STATE:(g + 1) * SSM_STATE] = dbms[g]
            dcm_ref[:, g * SSM_STATE:(g + 1) * SSM_STATE] = dcms[g]
        for i, (g, h) in enumerate(heads):
            dx_ref[:, (g * r + h) * p:(g * r + h + 1) * p] = dxs[i]
            ddt_ref[g, h:h + 1, :] = ddts[i]
            dal_ref[g, h:h + 1, :] += dals[i]
            ddb_ref[g, h:h + 1, :] += ddbs[i]
            dstate[g, h] = dh0s[i]

    gn = ng * SSM_STATE
    return pl.pallas_call(
        body, grid=(ng // gb, nc),
        in_specs=[x_spec, b_spec, c_spec, dt_spec, sc_spec, sc_spec, st_spec, y_spec],
        out_specs=[y_spec, bc_out, bc_out, dt_spec, sc_spec, sc_spec],
        out_shape=[jax.ShapeDtypeStruct((s, w), F32), jax.ShapeDtypeStruct((s, gn), F32), jax.ShapeDtypeStruct((s, gn), F32),
                   jax.ShapeDtypeStruct(dt_rows.shape, F32), jax.ShapeDtypeStruct((ng, r, 1), F32),
                   jax.ShapeDtypeStruct((ng, r, 1), F32)],
        scratch_shapes=[pltpu.VMEM((gb, r, p, SSM_STATE), F32)],
        compiler_params=_cparams(2), name=name,
    )(xbc, xbc, xbc, dt_rows, alog, dtb, states, dy)


def _ssm_post_fwd(y, xbc, src, z_col0, dexp, nw, *, name, tm=512):
    s, w = y.shape
    gw = w // SSM_GROUPS
    zc = z_col0 * LANES // gw

    def body(y_ref, x_ref, z_ref, d_ref, w_ref, o_ref):
        yy = (y_ref[...] + x_ref[...] * d_ref[...]) * _silu(z_ref[...])
        r = lax.rsqrt(jnp.mean(yy * yy, axis=-1, keepdims=True) + EPS)
        o_ref[...] = (yy * r * w_ref[...]).astype(o_ref.dtype)

    blk = pl.BlockSpec((tm, gw), lambda g, i: (i, g))
    vec = pl.BlockSpec((1, gw), lambda g, i: (0, g))
    return pl.pallas_call(
        body, grid=(SSM_GROUPS, s // tm),
        in_specs=[blk, blk, pl.BlockSpec((tm, gw), lambda g, i: (i, zc + g)), vec, vec],
        out_specs=blk, out_shape=jax.ShapeDtypeStruct((s, w), MXU_DTYPE),
        compiler_params=_cparams(2), name=name,
    )(y, xbc, src, dexp.reshape(1, w), nw.reshape(1, w))


def _ssm_post_bwd(y, xbc, src, z_col0, dexp, nw, dout, into, *, name, tm=512):
    s, w = y.shape
    gw = w // SSM_GROUPS
    zc = z_col0 * LANES // gw

    def body(y_ref, x_ref, z_ref, d_ref, w_ref, do_ref, into_ref, dy_ref, dx_ref, dz_ref, dd_ref, dw_ref):
        xv, zv, dv = x_ref[...], z_ref[...], d_ref[...]
        pre = y_ref[...] + xv * dv
        sz, sz_grad = _silu_and_grad(zv)
        yy = pre * sz
        r = lax.rsqrt(jnp.mean(yy * yy, axis=-1, keepdims=True) + EPS)
        yh = yy * r
        dov = do_ref[...]
        dyn = dov * w_ref[...]
        dyy = r * (dyn - yh * jnp.mean(dyn * yh, axis=-1, keepdims=True))
        dpre = dyy * sz
        dy_ref[...] = dpre
        dx_ref[...] = dpre * dv
        dz_ref[...] = (dyy * pre * sz_grad).astype(dz_ref.dtype)

        @pl.when(pl.program_id(1) == 0)
        def _():
            dd_ref[...] = jnp.zeros_like(dd_ref)
            dw_ref[...] = jnp.zeros_like(dw_ref)

        dd_ref[...] += jnp.sum(dpre * xv, axis=0, keepdims=True)
        dw_ref[...] += jnp.sum(dov * yh, axis=0, keepdims=True)

    blk = pl.BlockSpec((tm, gw), lambda g, i: (i, g))
    vec = pl.BlockSpec((1, gw), lambda g, i: (0, g))
    z_blk = pl.BlockSpec((tm, gw), lambda g, i: (i, zc + g))
    dy, dx, dz, dd, dw = pl.pallas_call(
        body, grid=(SSM_GROUPS, s // tm),
        in_specs=[blk, blk, z_blk, vec, vec, blk, ANY],
        out_specs=[blk, blk, z_blk, vec, vec],
        out_shape=[jax.ShapeDtypeStruct((s, w), F32), jax.ShapeDtypeStruct((s, w), F32),
                   jax.ShapeDtypeStruct(into.shape, into.dtype), jax.ShapeDtypeStruct((1, w), F32),
                   jax.ShapeDtypeStruct((1, w), F32)],
        input_output_aliases={6: 2},
        compiler_params=_cparams(2), name=name,
    )(y, xbc, src, dexp.reshape(1, w), nw.reshape(1, w), dout, into)
    return dy, dx, dz, dd.reshape(w), dw.reshape(w)


def _merge_fwd(proj3, src, gate_col0, d, *, name, tm=512):
    s = proj3.shape[0]
    nb = proj3.shape[1] // d
    gc = gate_col0 * LANES // d

    def body(*refs):
        p_refs, g_refs, o_ref = refs[:nb], refs[nb:2 * nb], refs[-1]
        acc = None
        for p_ref, g_ref in zip(p_refs, g_refs):
            term = _sigmoid(g_ref[...]) * p_ref[...]
            acc = term if acc is None else acc + term
        o_ref[...] = acc.astype(o_ref.dtype)

    p_specs = [pl.BlockSpec((tm, d), lambda i, b=b: (i, b)) for b in range(nb)]
    g_specs = [pl.BlockSpec((tm, d), lambda i, b=b: (i, gc + b)) for b in range(nb)]
    return pl.pallas_call(
        body, grid=(s // tm,), in_specs=p_specs + g_specs,
        out_specs=pl.BlockSpec((tm, d), lambda i: (i, 0)), out_shape=jax.ShapeDtypeStruct((s, d), MXU_DTYPE),
        compiler_params=_cparams(1), name=name,
    )(*([proj3] * nb), *([src] * nb))


def _merge_bwd(proj3, src, gate_col0, d, dmerged, into, *, name, tm=512):
    s = proj3.shape[0]
    nb = proj3.shape[1] // d
    gc = gate_col0 * LANES // d

    def body(p_ref, g_ref, dm_ref, into_ref, dp_ref, dg_ref):
        sg = _sigmoid(g_ref[...])
        dm = dm_ref[...]
        dp_ref[...] = (dm * sg).astype(dp_ref.dtype)
        dg_ref[...] = (dm * p_ref[...] * sg * (1.0 - sg)).astype(dg_ref.dtype)

    blk = pl.BlockSpec((tm, d), lambda i, b: (i, b))
    gate_blk = pl.BlockSpec((tm, d), lambda i, b: (i, gc + b))
    return pl.pallas_call(
        body, grid=(s // tm, nb),
        in_specs=[blk, gate_blk, pl.BlockSpec((tm, d), lambda i, b: (i, 0)), ANY],
        out_specs=[blk, gate_blk],
        out_shape=[jax.ShapeDtypeStruct(proj3.shape, MXU_DTYPE), jax.ShapeDtypeStruct(into.shape, into.dtype)],
        input_output_aliases={3: 1},
        compiler_params=_cparams(2), name=name,
    )(proj3, src, dmerged, into)


ANY = pl.BlockSpec(memory_space=pl.ANY)
MESH = pl.DeviceIdType.MESH


def _all_gather(shards, *, name, after=None):
    nt = len(shards)
    n_after = 0 if after is None else 1

    def body(*refs):
        x_refs, out_refs = refs[:nt], refs[nt + n_after:2 * nt + n_after]
        send_sems, recv_sems, local_sems = refs[2 * nt + n_after:]
        x, y, c = lax.axis_index("x"), lax.axis_index("y"), lax.axis_index("c")
        me, sibling = (x, y, c), (x, y, 1 - c)
        chips = [(1 - x, y), (x, 1 - y), (1 - x, 1 - y)]

        def slot(t, px, py, pc):
            return out_refs[t].at[4 * px + 2 * py + pc]

        def copy(t, k, block, to, from_input=False):
            return pltpu.make_async_remote_copy(
                src_ref=x_refs[t] if from_input else slot(t, *block), dst_ref=slot(t, *block),
                send_sem=send_sems.at[7 * t + k], recv_sem=recv_sems.at[7 * t + k], device_id=to, device_id_type=MESH)

        mine = [pltpu.make_async_copy(x_refs[t], slot(t, *me), local_sems.at[t]) for t in range(nt)]
        for cp in mine:
            cp.start()
        first = [copy(t, 0, me, sibling, True) for t in range(nt)]
        first += [copy(t, 1 + j, me, (*chip, c), True) for j, chip in enumerate(chips) for t in range(nt)]
        for cp in first:
            cp.start()
        passed = []
        for j, chip in enumerate(chips):
            for t in range(nt):
                copy(t, 1 + j, (*chip, c), me).wait_recv()
                fwd = copy(t, 4 + j, (*chip, c), sibling)
                fwd.start()
                passed.append(fwd)
        for t in range(nt):
            copy(t, 0, sibling, me).wait_recv()
            for j, chip in enumerate(chips):
                copy(t, 4 + j, (*chip, 1 - c), me).wait_recv()
        for cp in first + passed:
            cp.wait_send()
        for cp in mine:
            cp.wait()

    return pl.pallas_call(
        body, out_shape=[jax.ShapeDtypeStruct((N_DEV,) + a.shape, a.dtype) for a in shards],
        in_specs=[ANY] * (nt + n_after), out_specs=[ANY] * nt,
        scratch_shapes=[pltpu.SemaphoreType.DMA((7 * nt,)), pltpu.SemaphoreType.DMA((7 * nt,)),
                        pltpu.SemaphoreType.DMA((nt,))],
        name=name,
    )(*shards, *([] if after is None else [after]))


HBM = pl.BlockSpec(memory_space=pltpu.HBM)
SEM = pl.BlockSpec(memory_space=pltpu.SEMAPHORE)
EFFECT = pltpu.SideEffectType.DATAFLOW_SIDE_EFFECTING


def _peers():
    x, y, c = lax.axis_index("x"), lax.axis_index("y"), lax.axis_index("c")
    peers = []
    for k in range(1, N_DEV):
        px, py, pc = x ^ ((k >> 2) & 1), y ^ ((k >> 1) & 1), c ^ (k & 1)
        peers.append(((px, py, pc), 4 * px + 2 * py + pc))
    return 4 * x + 2 * y + c, peers


def _split_copies(slots, src_refs, land_refs, send_sems, recv_sems):
    me, peers = _peers()
    copies = []
    for t, (whole, layer) in enumerate(slots):
        dst = land_refs[t].at[me] if layer is None else land_refs[t].at[me, layer]
        for k, (dev, lin) in enumerate(peers):
            copies.append(pltpu.make_async_remote_copy(
                src_ref=src_refs[t] if whole else src_refs[t].at[lin], dst_ref=dst,
                send_sem=send_sems.at[7 * t + k], recv_sem=recv_sems.at[7 * t + k], device_id=dev, device_id_type=MESH))
    return copies


def _split_start(srcs, lands, slots, carry, *, name):
    n = len(srcs)

    def body(*refs):
        copies = _split_copies(slots, refs[:n], refs[n:2 * n], refs[2 * n + 1], refs[2 * n + 2])
        for cp in copies:
            cp.start()

    def hbm(a):
        return pltpu.HBM(a.shape, a.dtype)

    outs = pl.pallas_call(
        body, name=name,
        out_shape=[pltpu.SemaphoreType.DMA((7 * n,)), pltpu.SemaphoreType.DMA((7 * n,))]
        + [hbm(a) for a in srcs] + [hbm(a) for a in lands] + [hbm(carry)],
        in_specs=[HBM] * (2 * n + 1), out_specs=[SEM, SEM] + [HBM] * (2 * n + 1),
        input_output_aliases={i: 2 + i for i in range(2 * n + 1)},
        compiler_params=pltpu.CompilerParams(has_side_effects=EFFECT),
    )(*[pltpu.with_memory_space_constraint(a, pltpu.HBM) for a in list(srcs) + list(lands) + [carry]])
    return outs[0], outs[1], outs[2:2 + n], outs[2 + n:2 + 2 * n], outs[2 + 2 * n]


def _split_wait(send_sems, recv_sems, srcs, lands, slots, after, *, name):
    n = len(srcs)

    def body(*refs):
        copies = _split_copies(slots, refs[:n], refs[n:2 * n], refs[2 * n], refs[2 * n + 1])
        for cp in copies:
            cp.wait_send()
        for cp in copies:
            cp.wait_recv()

    outs = pl.pallas_call(
        body, name=name,
        out_shape=[pltpu.HBM(a.shape, a.dtype) for a in list(srcs) + list(lands)],
        in_specs=[HBM] * (2 * n) + [SEM, SEM, ANY], out_specs=[HBM] * (2 * n),
        input_output_aliases={i: i for i in range(2 * n)},
        compiler_params=pltpu.CompilerParams(has_side_effects=EFFECT),
    )(*srcs, *lands, send_sems, recv_sems, after)
    return outs[n:]


def _adam_math(w, g, m, v):
    m1 = ADAM_B1 * m + (1.0 - ADAM_B1) * g
    v1 = ADAM_B2 * v + (1.0 - ADAM_B2) * (g * g)
    m_hat = m1 / (1.0 - ADAM_B1 ** ADAM_STEP)
    v_hat = v1 / (1.0 - ADAM_B2 ** ADAM_STEP)
    delta = -ADAM_LR * (m_hat / (jnp.sqrt(v_hat) + ADAM_EPS) + ADAM_WD * w)
    return delta, m1, v1


def _sum_adamw(parts, w, m, v, layer, prev, *, name):
    shape = w.shape
    r, c = shape[-2], shape[-1]
    a_l = math.prod(shape[1:-2])
    a = shape[0] * a_l
    base = layer * a_l
    if r % 256 == 0:
        tr, tc = 256, c
    else:
        tr, tc = r, _pick(c, (256, 128))
    w3, m3, v3 = (t.reshape(a, r, c) for t in (w, m, v))
    n_prev = 0 if prev is None else 4

    def body(*refs):
        p_ref, w_ref, m_ref, v_ref = refs[:4]
        g_ref, d_ref, m1_ref, v1_ref = refs[4 + n_prev:]
        g = p_ref[0].astype(F32)
        for src in range(1, N_DEV):
            g = g + p_ref[src].astype(F32)
        delta, m1, v1 = _adam_math(w_ref[...], g, m_ref[...], v_ref[...])
        g_ref[...] = g
        d_ref[...] = delta
        m1_ref[...] = m1
        v1_ref[...] = v1

    nr, ncol = r // tr, c // tc
    blk = pl.BlockSpec((None, tr, tc), lambda i, j: (base + i, j // ncol, j % ncol))
    prev3 = [] if prev is None else [t.reshape(a, r, c) for t in prev]
    outs = pl.pallas_call(
        body, grid=(a_l, nr * ncol),
        in_specs=[pl.BlockSpec((N_DEV, None, tr, tc), lambda i, j: (0, i, j // ncol, j % ncol)), blk, blk, blk]
        + [ANY] * n_prev,
        out_specs=[blk] * 4, out_shape=[jax.ShapeDtypeStruct((a, r, c), F32)] * 4,
        input_output_aliases={4 + k: k for k in range(n_prev)},
        compiler_params=_cparams(2), name=name,
    )(parts.reshape(N_DEV, a_l, r, c), w3, m3, v3, *prev3)
    return [o.reshape(shape) for o in outs]


def _sum_parts(parts, *, name):
    rows = parts.shape[1]

    def body(p_ref, o_ref):
        g = p_ref[0]
        for src in range(1, N_DEV):
            g = g + p_ref[src]
        o_ref[...] = g

    return pl.pallas_call(
        body, grid=(1,), in_specs=[pl.BlockSpec((N_DEV, rows, LANES), lambda i: (0, 0, 0))],
        out_specs=pl.BlockSpec((rows, LANES), lambda i: (0, 0)), out_shape=jax.ShapeDtypeStruct((rows, LANES), F32),
        compiler_params=_cparams(1), name=name,
    )(parts)


def _adamw(w, g, m, v, *, name):
    rows = w.shape[0]

    def body(w_ref, g_ref, m_ref, v_ref, d_ref, m1_ref, v1_ref):
        delta, m1, v1 = _adam_math(w_ref[...], g_ref[...], m_ref[...], v_ref[...])
        d_ref[...] = delta
        m1_ref[...] = m1
        v1_ref[...] = v1

    blk = pl.BlockSpec((rows, LANES), lambda i: (0, 0))
    return pl.pallas_call(
        body, grid=(1,), in_specs=[blk] * 4, out_specs=[blk] * 3,
        out_shape=[jax.ShapeDtypeStruct((rows, LANES), F32)] * 3,
        compiler_params=_cparams(1), name=name,
    )(w, g, m, v)


def _pack(arrs, dtype, row_mult=16):
    flat = jnp.concatenate([a.reshape(-1).astype(dtype) for a in arrs])
    n = flat.shape[0]
    rows = -(-n // (LANES * row_mult)) * row_mult
    flat = jnp.pad(flat, (0, rows * LANES - n))
    return flat.reshape(rows, LANES)


def _unpack(packed, shapes):
    flat = packed.reshape(-1)
    out, off = [], 0
    for shp in shapes:
        n = math.prod(shp)
        out.append(flat[off:off + n].reshape(shp))
        off += n
    return out


class _Layout:
    def __init__(self, d):
        self.d = d
        w = d
        self.dn_heads = w // DN_HEAD_DIM
        self.ssm_heads = w // SSM_HEAD_DIM
        gn = SSM_GROUPS * SSM_STATE
        self.sizes = (3 * w, w, self.dn_heads, self.dn_heads, 3 * w, w, w + 2 * gn, self.ssm_heads, 3 * d)
        offs, o = [], 0
        for sz in self.sizes:
            offs.append(o)
            o += sz
        self.offs = offs
        self.in_dim = o
        self.big = (0, 1, 4, 5, 6, 8)
        self.small = (2, 3, 7)
        cols, o = {}, 0
        for idx in self.big:
            cols[idx] = o
            o += self.sizes[idx]
        self.small_col = o
        self.cols = cols
        self.padded = o + LANES
        self.n_small = sum(self.sizes[i] for i in self.small)

    def from_shards(self, parts):
        cs = self.in_dim // N_DEV
        pieces = []
        for i in self.big + self.small:
            a, b = self.offs[i], self.offs[i] + self.sizes[i]
            while a < b:
                j = a // cs
                hi = min(b, (j + 1) * cs)
                pieces.append(parts[j][:, a - j * cs:hi - j * cs])
                a = hi
        pieces.append(jnp.zeros((parts.shape[1], LANES - self.n_small), parts.dtype))
        return jnp.concatenate(pieces, axis=1)

    def to_shards(self, wp):
        cs = self.in_dim // N_DEV
        pcol = dict(self.cols)
        o = self.small_col
        for i in self.small:
            pcol[i] = o
            o += self.sizes[i]
        shards = []
        for j in range(N_DEV):
            a, b = j * cs, (j + 1) * cs
            pieces = []
            for i in range(len(self.sizes)):
                lo, hi = max(a, self.offs[i]), min(b, self.offs[i] + self.sizes[i])
                if lo < hi:
                    pieces.append(wp[:, pcol[i] + lo - self.offs[i]:pcol[i] + hi - self.offs[i]])
            shards.append(jnp.concatenate(pieces, axis=1))
        return jnp.stack(shards)

def _rows_form(cols_t, nh, nc):
    return cols_t.T.reshape(nh, nc, 1, CHUNK)


def _layer_fwd(x, p, lay, tag, late=None):
    s, d = x.shape
    nc = s // CHUNK
    w = d
    dnh, smh = lay.dn_heads, lay.ssm_heads
    r = smh // SSM_GROUPS
    cb = {k: v // LANES for k, v in lay.cols.items()}
    sv = {}
    h1 = _rms_fwd(x, p["norm_mix"], name=f"rms_mix_{tag}")
    proj = _matmul(h1, p["w_in"], name=f"mm_in_{tag}")
    small = proj[:, lay.small_col:lay.small_col + LANES]
    a_rows = _rows_form(small[:, 0:dnh], dnh, nc)
    b_rows = _rows_form(small[:, dnh:2 * dnh], dnh, nc)
    dt_rows = small[:, 2 * dnh:2 * dnh + smh].T.reshape(SSM_GROUPS, r, nc, CHUNK).transpose(0, 2, 1, 3)
    zero_b = jnp.zeros((1, 3 * w), F32)
    dn_qkv = _conv_fwd(proj, cb[0], p["dn_conv_w"], zero_b, 2 * dnh, name=f"dn_conv_{tag}")
    dn_alog = p["dn_a_log"].reshape(dnh, 1, 1)
    dn_dtb = p["dn_dt_bias"].reshape(dnh, 1, 1)
    o_dn, dn_states, dn_inv = _dn_fwd(dn_qkv, a_rows, b_rows, dn_alog, dn_dtb, name=f"dn_chunk_{tag}")
    y_dn = _dn_post_fwd(o_dn, proj, cb[1], p["dn_norm_w"], name=f"dn_post_{tag}")
    o_sb, sb_r = _sb_fwd(proj, cb[4], w, name=f"sb_{tag}")
    xbc = _conv_fwd(proj, cb[6], p["ssm_conv_w"], p["ssm_conv_b"].reshape(1, -1), 0, name=f"ssm_conv_{tag}")
    ssm_alog = p["ssm_a_log"].reshape(SSM_GROUPS, r, 1)
    ssm_dtb = p["ssm_dt_bias"].reshape(SSM_GROUPS, r, 1)
    y_ssd, ssm_states = _ssd_fwd(xbc, dt_rows, ssm_alog, ssm_dtb, name=f"ssd_{tag}")
    dexp = jnp.repeat(p["ssm_d"], SSM_HEAD_DIM)
    y_ssm = _ssm_post_fwd(y_ssd, xbc, proj, cb[5], dexp, p["ssm_norm_w"], name=f"ssm_post_{tag}")
    if late is not None:
        p.update(late(y_ssm))
    branches = (y_dn, o_sb, y_ssm)
    proj3 = lax.empty((s, 3 * d), F32)
    for i, br in enumerate(branches):
        proj3 = _matmul(br, p["w_branch"][i], into=(proj3, i * d), name=f"mm_branch{i}_{tag}")
    merged = _merge_fwd(proj3, proj, cb[8], d, name=f"merge_{tag}")
    x1 = _matmul(merged, p["w_out"], name=f"mm_out_{tag}", epilogue=lambda acc, res: (acc + res,), extras=(x,))
    h2 = _rms_fwd(x1, p["norm_mlp"], name=f"rms_mlp_{tag}")
    u, act = _matmul(h2, p["w_up"], name=f"mm_up_{tag}", out_dtypes=(F32, MXU_DTYPE),
                     epilogue=lambda acc: (acc, jnp.square(jnp.maximum(acc, 0.0))))
    x2 = _matmul(act, p["w_down"], name=f"mm_down_{tag}", epilogue=lambda acc, res: (acc + res,), extras=(x1,))
    sv.update(x=x, h1=h1, proj=proj, a_rows=a_rows, b_rows=b_rows, dt_rows=dt_rows, dn_qkv=dn_qkv, dn_alog=dn_alog,
              dn_dtb=dn_dtb, o_dn=o_dn, dn_states=dn_states, dn_inv=dn_inv, y_dn=y_dn, o_sb=o_sb, sb_r=sb_r, xbc=xbc, ssm_alog=ssm_alog,
              ssm_dtb=ssm_dtb, y_ssd=y_ssd, ssm_states=ssm_states, dexp=dexp, y_ssm=y_ssm, proj3=proj3, merged=merged,
              x1=x1, h2=h2, u=u, act=act)
    return x2, sv


def _layer_bwd(dx2, p, sv, lay, tag, early=None, late=None):
    x = sv["x"]
    s, d = x.shape
    nc = s // CHUNK
    w = d
    dnh, smh = lay.dn_heads, lay.ssm_heads
    r = smh // SSM_GROUPS
    gn = SSM_GROUPS * SSM_STATE
    cb = {k: v // LANES for k, v in lay.cols.items()}
    proj = sv["proj"]
    g = {}
    dx2_b = dx2.astype(MXU_DTYPE)
    du = _matmul(dx2_b, p["w_down"], tb=True, name=f"mm_down_dx_{tag}", out_dtypes=(MXU_DTYPE,),
                 epilogue=lambda acc, uu: (acc * (2.0 * jnp.maximum(uu, 0.0)),), extras=(sv["u"],))
    g["w_down"] = _matmul(sv["act"], dx2_b, ta=True, name=f"mm_down_dw_{tag}", out_dtypes=(BF16,)).reshape(N_DEV, -1, d)
    g["w_up"] = _matmul(sv["h2"], du, ta=True, name=f"mm_up_dw_{tag}", out_dtypes=(BF16,), col_shards=N_DEV)
    dh2 = _matmul(du, p["w_up"], tb=True, name=f"mm_up_dx_{tag}")
    dx1, g["norm_mlp"] = _rms_bwd(sv["x1"], p["norm_mlp"], dh2, dx2, name=f"rms_mlp_bwd_{tag}")
    dx1_b = dx1.astype(MXU_DTYPE)
    dmerged = _matmul(dx1_b, p["w_out"], tb=True, name=f"mm_out_dx_{tag}")
    g["w_out"] = _matmul(sv["merged"], dx1_b, ta=True, name=f"mm_out_dw_{tag}", out_dtypes=(BF16,)).reshape(N_DEV, -1, d)
    dproj = lax.empty((s, lay.padded), MXU_DTYPE)
    dproj3, dproj = _merge_bwd(sv["proj3"], proj, cb[8], d, dmerged, dproj, name=f"merge_bwd_{tag}")
    branches = (sv["y_dn"], sv["o_sb"], sv["y_ssm"])
    dwb, dbr = [], []
    for i, br in enumerate(branches):
        dp_i = dproj3[:, i * d:(i + 1) * d]
        dwb.append(_matmul(br, dp_i, ta=True, name=f"mm_branch{i}_dw_{tag}", out_dtypes=(BF16,)).reshape(N_DEV, -1, d))
        dbr.append(_matmul(dp_i, p["w_branch"][i], tb=True, name=f"mm_branch{i}_dx_{tag}"))
    g["w_branch"] = jnp.stack(dwb, axis=1)
    dy_dn, do_sb, dy_ssm = dbr
    if early is not None:
        dy_ssm = early(g, dy_ssm)
    dy_ssd, dxs_skip, dproj, ddexp, g["ssm_norm_w"] = _ssm_post_bwd(
        sv["y_ssd"], sv["xbc"], proj, cb[5], sv["dexp"], p["ssm_norm_w"], dy_ssm, dproj, name=f"ssm_post_bwd_{tag}")
    g["ssm_d"] = ddexp.reshape(smh, SSM_HEAD_DIM).sum(axis=1)
    dxs, dbm, dcm, ddt_rows, dalog, ddtb = _ssd_bwd(
        sv["xbc"], sv["dt_rows"], sv["ssm_alog"], sv["ssm_dtb"], sv["ssm_states"], dy_ssd, name=f"ssd_bwd_{tag}")
    g["ssm_a_log"] = dalog.reshape(smh)
    g["ssm_dt_bias"] = ddtb.reshape(smh)
    dxbc_post = jnp.concatenate([dxs + dxs_skip, dbm, dcm], axis=1)
    dproj, g["ssm_conv_w"], dcb = _conv_bwd(proj, cb[6], p["ssm_conv_w"], p["ssm_conv_b"].reshape(1, -1), 0, dxbc_post,
                                            dproj, name=f"ssm_conv_bwd_{tag}")
    g["ssm_conv_b"] = dcb.reshape(-1)
    ddt = ddt_rows.transpose(0, 2, 1, 3).reshape(smh, s).T
    dqkv_sb = _sb_bwd(proj, cb[4], w, sv["sb_r"], do_sb, name=f"sb_bwd_{tag}")
    dproj = lax.dynamic_update_slice(dproj, jnp.concatenate([t.astype(MXU_DTYPE) for t in dqkv_sb], axis=1), (0, lay.cols[4]))
    do_dn, dproj, g["dn_norm_w"] = _dn_post_bwd(sv["o_dn"], proj, cb[1], p["dn_norm_w"], dy_dn, dproj,
                                                name=f"dn_post_bwd_{tag}")
    dqkv_dn, da_rows, db_rows, dal, ddtb_dn = _dn_bwd(
        sv["dn_qkv"], sv["a_rows"], sv["b_rows"], sv["dn_alog"], sv["dn_dtb"], sv["dn_states"], sv["dn_inv"], do_dn,
        name=f"dn_chunk_bwd_{tag}")
    g["dn_a_log"] = dal.reshape(dnh)
    g["dn_dt_bias"] = ddtb_dn.reshape(dnh)
    zero_b = jnp.zeros((1, 3 * w), F32)
    dproj, g["dn_conv_w"], _ = _conv_bwd(proj, cb[0], p["dn_conv_w"], zero_b, 2 * dnh, dqkv_dn, dproj,
                                         name=f"dn_conv_bwd_{tag}")
    da = da_rows.reshape(dnh, s).T
    db = db_rows.reshape(dnh, s).T
    dsmall = jnp.concatenate([da, db, ddt, jnp.zeros((s, LANES - lay.n_small), F32)], axis=1).astype(MXU_DTYPE)
    dproj = lax.dynamic_update_slice(dproj, dsmall, (0, lay.small_col))
    g["w_in"] = lay.to_shards(_matmul(sv["h1"], dproj, ta=True, name=f"mm_in_dw_{tag}", out_dtypes=(BF16,)))
    if late is not None:
        dproj = late(g, dproj)
    dh1 = _matmul(dproj, p["w_in"], tb=True, name=f"mm_in_dx_{tag}")
    dx0, g["norm_mix"] = _rms_bwd(x, p["norm_mix"], dh1, dx1, name=f"rms_mix_bwd_{tag}")
    return dx0, g


BIG = ("w_in", "w_branch", "w_out", "w_up", "w_down")
CONV = ("dn_conv_w", "ssm_conv_w")
SMALL = ("norm_mix", "dn_conv_w", "dn_a_log", "dn_dt_bias", "dn_norm_w", "ssm_conv_w", "ssm_conv_b", "ssm_a_log",
         "ssm_dt_bias", "ssm_d", "ssm_norm_w", "norm_mlp", "norm_final")
WEIGHTS = ("norm_mix", "w_in", "dn_conv_w", "dn_a_log", "dn_dt_bias", "dn_norm_w", "ssm_conv_w", "ssm_conv_b", "ssm_a_log",
           "ssm_dt_bias", "ssm_d", "ssm_norm_w", "w_branch", "w_out", "norm_mlp", "w_up", "w_down", "norm_final")
SHARD_AXIS = {"w_in": 2, "dn_conv_w": 2, "ssm_conv_w": 2, "w_branch": 2, "w_out": 1, "w_up": 2, "w_down": 1}


def _to_shards(full, axis):
    shp = full.shape
    n = shp[axis] // N_DEV
    t = full.reshape(shp[:axis] + (N_DEV, n) + shp[axis + 1:])
    return jnp.moveaxis(t, axis, 0)


def _unshard(parts, axis, *, name):
    shard = parts.shape[1:]
    nd = len(shard)
    if axis == 0:
        return parts.reshape((N_DEV * shard[0],) + shard[1:])

    def copy_block(i_ref, o_ref):
        o_ref[...] = i_ref[...]

    if axis == nd - 1:
        rows, n = math.prod(shard[:-1]), shard[-1]
        out = pl.pallas_call(
            copy_block, grid=(N_DEV,),
            in_specs=[pl.BlockSpec((None, rows, n), lambda j: (j, 0, 0))],
            out_specs=pl.BlockSpec((rows, n), lambda j: (0, j)),
            out_shape=jax.ShapeDtypeStruct((rows, N_DEV * n), parts.dtype),
            compiler_params=_cparams(1), name=name,
        )(parts.reshape(N_DEV, rows, n))
        return out.reshape(shard[:-1] + (N_DEV * n,))
    assert axis == nd - 2, (parts.shape, axis)
    a, n, c = math.prod(shard[:-2]), shard[-2], shard[-1]
    out = pl.pallas_call(
        copy_block, grid=(N_DEV, a),
        in_specs=[pl.BlockSpec((None, None, n, c), lambda j, i: (j, i, 0, 0))],
        out_specs=pl.BlockSpec((None, n, c), lambda j, i: (i, j, 0)),
        out_shape=jax.ShapeDtypeStruct((a, N_DEV * n, c), parts.dtype),
        compiler_params=_cparams(2), name=name,
    )(parts.reshape(N_DEV, a, n, c))
    return out.reshape(shard[:-2] + (N_DEV * n, c))


def _step(w, m, v, x, target):
    s, d = x.shape
    lay = _Layout(d)
    me = 4 * lax.axis_index("x") + 2 * lax.axis_index("y") + lax.axis_index("c")

    def shard(n, l):
        return w[n][l].astype(BF16) if n in BIG else w[n][l]

    def empty_land(a):
        return lax.empty((N_DEV,) + a.shape, a.dtype)

    def with_own(land, own):
        return lax.dynamic_update_index_in_dim(land, own, me, 0)

    def assemble(n, parts, l):
        return lay.from_shards(parts) if n == "w_in" else _unshard(parts, SHARD_AXIS[n] - 1, name=f"unshard_{n}_l{l}")

    small_names = tuple(n for n in WEIGHTS if n not in BIG + CONV + ("norm_final",))

    first, rest = ("w_in",) + CONV, BIG[1:]
    got = _all_gather([shard(n, 0) for n in first], name="gather_l0_first")
    whole, sliced = (True, None), (False, None)
    names_a, names_b = rest, BIG + CONV
    srcs_a, srcs_b = [shard(n, 0) for n in names_a], [shard(n, 1) for n in names_b]
    sem_sa, sem_ra, srcs_a, lands_a, w_in0 = _split_start(
        srcs_a, [empty_land(a) for a in srcs_a], [whole] * len(srcs_a), got[0], name="gather_l0_rest_start")
    sem_sb, sem_rb, srcs_b, lands_b, w_in0 = _split_start(
        srcs_b, [empty_land(a) for a in srcs_b], [whole] * len(srcs_b), w_in0, name="gather_l1_start")
    p0 = {n: w[n][0] for n in small_names}
    p0.update({n: assemble(n, g, 0) for n, g in zip(first, [w_in0] + list(got[1:]))})

    def late_l0(after):
        lands = _split_wait(sem_sa, sem_ra, srcs_a, lands_a, [whole] * len(srcs_a), after, name="gather_l0_rest_wait")
        return {n: assemble(n, with_own(ld, s_), 0) for n, ld, s_ in zip(names_a, lands, srcs_a)}

    h, sv0 = _layer_fwd(x, p0, lay, "l0", late=late_l0)
    lands = _split_wait(sem_sb, sem_rb, srcs_b, lands_b, [whole] * len(srcs_b), h, name="gather_l1_wait")
    p1 = {n: w[n][1] for n in small_names}
    p1.update({n: assemble(n, with_own(ld, s_), 1) for n, ld, s_ in zip(names_b, lands, srcs_b)})
    h, sv1 = _layer_fwd(h, p1, lay, "l1")
    loss, dh, g_norm_final = _final_loss(h, w["norm_final"], target, name="final_loss")
    grads = [None] * DEPTH
    dh, grads[1] = _layer_bwd(dh, p1, sv1, lay, "l1")

    def exchange_start(names, g, carry, tag):
        srcs = [g[n] for n in names]
        return _split_start(srcs, [lax.empty(a.shape, a.dtype) for a in srcs], [sliced] * len(srcs), carry,
                            name=f"grad_{tag}_start")

    def exchange_wait(names, started, after, tag):
        sem_s, sem_r, srcs, lands_, _ = started
        lands_ = _split_wait(sem_s, sem_r, srcs, lands_, [sliced] * len(srcs), after, name=f"grad_{tag}_wait")
        return {n: with_own(ld, lax.dynamic_index_in_dim(s_, me, 0, keepdims=False)) for n, ld, s_ in zip(names, lands_, srcs)}

    x1_started = exchange_start(BIG, grads[1], dh, "l1")
    pending = {}

    def early_l0(g, carry):
        pending["rest"] = exchange_start(rest, g, carry, "l0_rest")
        return pending["rest"][4]

    def late_bwd_l0(g, carry):
        pending["w_in"] = exchange_start(("w_in",), g, carry, "l0_w_in")
        return pending["w_in"][4]

    grad_x, grads[0] = _layer_bwd(x1_started[4], p0, sv0, lay, "l0", early=early_l0, late=late_bwd_l0)

    out = {"grad": {}, "delta": {}, "new_m": {}, "new_v": {}}
    parts1 = exchange_wait(BIG, x1_started, grad_x, "l1")
    res1 = {n: _sum_adamw(parts1[n], w[n], m[n], v[n], 1, None, name=f"sum_adamw_{n}_l1") for n in BIG}
    parts0 = exchange_wait(rest, pending["rest"], res1["w_in"][0], "l0_rest")
    res0 = {n: _sum_adamw(parts0[n], w[n], m[n], v[n], 0, res1[n], name=f"sum_adamw_{n}_l0") for n in rest}
    parts0 = exchange_wait(("w_in",), pending["w_in"], res0["w_down"][0], "l0_w_in")
    res0["w_in"] = _sum_adamw(parts0["w_in"], w["w_in"], m["w_in"], v["w_in"], 0, res1["w_in"], name="sum_adamw_w_in_l0")
    for n in BIG:
        for key, a in zip(("grad", "delta", "new_m", "new_v"), res0[n]):
            out[key][n] = a

    gfull = {n: jnp.stack([grads[l][n] for l in range(DEPTH)]) for n in SMALL if n != "norm_final"}
    gfull["norm_final"] = g_norm_final
    small_send = _pack([gfull[n] for n in SMALL] + [loss.reshape(1)], F32)
    small_recv = _all_gather([small_send], name="gather_small_grads", after=res0["w_in"][0])[0]
    small_sum = _sum_parts(small_recv, name="sum_small")
    small_full = _unpack(small_sum, [gfull[n].shape for n in SMALL] + [(1,)])
    loss_total = small_full[-1][0]
    gsmall = {}
    for n, a in zip(SMALL, small_full[:-1]):
        if n in SHARD_AXIS:
            a = lax.dynamic_index_in_dim(_to_shards(a, SHARD_AXIS[n]), me, axis=0, keepdims=False)
        gsmall[n] = a
    small_shapes = [w[n].shape for n in SMALL]
    ws, gs, ms, vs = (_pack([t[n] for n in SMALL], F32) for t in (w, gsmall, m, v))
    ds, m1s, v1s = _adamw(ws, gs, ms, vs, name="adamw_small")
    for n in SMALL:
        out["grad"][n] = gsmall[n]
    for key, packed in (("delta", ds), ("new_m", m1s), ("new_v", v1s)):
        for n, a in zip(SMALL, _unpack(packed, small_shapes)):
            out[key][n] = a
    return loss_total, grad_x, out


def kernel(x, norm_mix, w_in, dn_conv_w, dn_a_log, dn_dt_bias, dn_norm_w, ssm_conv_w, ssm_conv_b, ssm_a_log, ssm_dt_bias, ssm_d, ssm_norm_w, w_branch, w_out, norm_mlp, w_up, w_down, norm_final, loss_target, m_norm_mix, m_w_in, m_dn_conv_w, m_dn_a_log, m_dn_dt_bias, m_dn_norm_w, m_ssm_conv_w, m_ssm_conv_b, m_ssm_a_log, m_ssm_dt_bias, m_ssm_d, m_ssm_norm_w, m_w_branch, m_w_out, m_norm_mlp, m_w_up, m_w_down, m_norm_final, v_norm_mix, v_w_in, v_dn_conv_w, v_dn_a_log, v_dn_dt_bias, v_dn_norm_w, v_ssm_conv_w, v_ssm_conv_b, v_ssm_a_log, v_ssm_dt_bias, v_ssm_d, v_ssm_norm_w, v_w_branch, v_w_out, v_norm_mlp, v_w_up, v_w_down, v_norm_final):
    w = dict(norm_mix=norm_mix, w_in=w_in, dn_conv_w=dn_conv_w, dn_a_log=dn_a_log, dn_dt_bias=dn_dt_bias, dn_norm_w=dn_norm_w,
             ssm_conv_w=ssm_conv_w, ssm_conv_b=ssm_conv_b, ssm_a_log=ssm_a_log, ssm_dt_bias=ssm_dt_bias, ssm_d=ssm_d,
             ssm_norm_w=ssm_norm_w, w_branch=w_branch, w_out=w_out, norm_mlp=norm_mlp, w_up=w_up, w_down=w_down,
             norm_final=norm_final)
    m = dict(norm_mix=m_norm_mix, w_in=m_w_in, dn_conv_w=m_dn_conv_w, dn_a_log=m_dn_a_log, dn_dt_bias=m_dn_dt_bias,
             dn_norm_w=m_dn_norm_w, ssm_conv_w=m_ssm_conv_w, ssm_conv_b=m_ssm_conv_b, ssm_a_log=m_ssm_a_log,
             ssm_dt_bias=m_ssm_dt_bias, ssm_d=m_ssm_d, ssm_norm_w=m_ssm_norm_w, w_branch=m_w_branch, w_out=m_w_out,
             norm_mlp=m_norm_mlp, w_up=m_w_up, w_down=m_w_down, norm_final=m_norm_final)
    v = dict(norm_mix=v_norm_mix, w_in=v_w_in, dn_conv_w=v_dn_conv_w, dn_a_log=v_dn_a_log, dn_dt_bias=v_dn_dt_bias,
             dn_norm_w=v_dn_norm_w, ssm_conv_w=v_ssm_conv_w, ssm_conv_b=v_ssm_conv_b, ssm_a_log=v_ssm_a_log,
             ssm_dt_bias=v_ssm_dt_bias, ssm_d=v_ssm_d, ssm_norm_w=v_ssm_norm_w, w_branch=v_w_branch, w_out=v_w_out,
             norm_mlp=v_norm_mlp, w_up=v_w_up, w_down=v_w_down, norm_final=v_norm_final)
    loss, grad_x, out = _step(w, m, v, x[0], loss_target[0])
    return (loss, grad_x[None], *[out["grad"][n] for n in WEIGHTS], *[out["delta"][n] for n in WEIGHTS],
            *[out["new_m"][n] for n in WEIGHTS], *[out["new_v"][n] for n in WEIGHTS])
```

```python
import math

import jax
import jax.numpy as jnp
from jax import lax
from jax.experimental import pallas as pl
from jax.experimental.pallas import tpu as pltpu

F32 = jnp.float32
BF16 = jnp.bfloat16
MXU_DTYPE = BF16
HIGHEST = lax.Precision.HIGHEST

N_DEV = 8
DEPTH = 2
EPS = 1e-6
CONV_K = 4
DN_HEAD_DIM = 128
SB_HEAD_DIM = 64
SSM_HEAD_DIM = 64
SSM_STATE = 128
SSM_GROUPS = 4
CHUNK = 64
SB_BLOCK = 128
LANES = 128
ADAM_LR, ADAM_B1, ADAM_B2, ADAM_EPS, ADAM_WD, ADAM_STEP = 0.001, 0.9, 0.999, 1e-08, 0.01, 10
NEG_BIG = -1e30
DN_HEADS_PER_STEP = 8
SSD_GROUPS_PER_STEP = 1
SB_UNROLL = 4
SB_SPLIT = 2

ARB = "arbitrary"


def _cparams(n_axes):
    return pltpu.CompilerParams(dimension_semantics=(ARB,) * n_axes)


def _softplus(x):
    return jnp.maximum(x, 0.0) + jnp.log1p(jnp.exp(-jnp.abs(x)))


def _sigmoid(x):
    return jax.nn.sigmoid(x)


def _silu(x):
    return x * _sigmoid(x)


def _silu_and_grad(x):
    s = _sigmoid(x)
    return x * s, s * (1.0 + x * (1.0 - s))


def _dot(a, b, dims, prec=None):
    return lax.dot_general(a, b, (dims, ((), ())), precision=prec, preferred_element_type=F32)


NN = ((1,), (0,))
NT = ((1,), (1,))
TN = ((0,), (0,))


def _mxu_dot(a, b, dims):
    return _dot(a.astype(MXU_DTYPE), b.astype(MXU_DTYPE), dims)


def _single_pass_dot(dims):
    grads = {NN: (lambda a, b, ct: (_mxu_dot(ct, b, NT), _mxu_dot(a, ct, TN))),
             NT: (lambda a, b, ct: (_mxu_dot(ct, b, NN), _mxu_dot(ct, a, TN))),
             TN: (lambda a, b, ct: (_mxu_dot(b, ct, NT), _mxu_dot(a, ct, NN)))}[dims]

    @jax.custom_vjp
    def f(a, b):
        return _mxu_dot(a, b, dims)

    f.defvjp(lambda a, b: (_mxu_dot(a, b, dims), (a, b)), lambda res, ct: grads(*res, ct))
    return f


_SDOT = {dims: _single_pass_dot(dims) for dims in (NN, NT, TN)}


def _sdot(a, b, dims=NN):
    return _SDOT[dims](a, b)


def _split_dot(a, m_bf16, nsplit=3):
    out = None
    rem = a
    for _ in range(nsplit):
        piece = rem.astype(BF16)
        rem = rem - piece.astype(F32)
        term = _dot(piece, m_bf16, NN)
        out = term if out is None else out + term
    return out


def _pick(n, pref):
    for t in pref:
        if n % t == 0:
            return t
    return n


def _matmul(a, b, *, ta=False, tb=False, name, epilogue=None, extras=(), out_dtypes=(F32,), col_shards=1, into=None,
            a_cols=None, b_cols=None, tm=None, tn=None, tk=None):
    a_shape = a.shape if a_cols is None else (a.shape[0], a_cols[1])
    b_shape = b.shape if b_cols is None else (b.shape[0], b_cols[1])
    assert (a_cols is None or not ta) and (b_cols is None or not tb)
    m, k = (a_shape[1], a_shape[0]) if ta else a_shape
    k2, n = (b_shape[1], b_shape[0]) if tb else b_shape
    assert k == k2, (a.shape, b.shape, ta, tb)
    ncs = n // col_shards
    tm = tm or _pick(m, (1920, 1024, 512, 256, 128))
    tn = tn or _pick(ncs, (1920, 1024, 640, 512, 384, 256, 128))
    tk = tk or _pick(k, (1920, 1024, 640, 512, 256, 128))
    nk = k // tk
    a_off = 0 if a_cols is None else a_cols[0] // tk
    b_off = 0 if b_cols is None else b_cols[0] // tn
    assert (a_cols is None or a_cols[0] % tk == 0) and (b_cols is None or b_cols[0] % tn == 0)
    a_spec = (pl.BlockSpec((tk, tm), lambda i, j, kk: (kk, i)) if ta
              else pl.BlockSpec((tm, tk), lambda i, j, kk: (i, kk + a_off)))
    b_spec = (pl.BlockSpec((tn, tk), lambda i, j, kk: (j, kk)) if tb
              else pl.BlockSpec((tk, tn), lambda i, j, kk: (kk, j + b_off)))
    e_spec = pl.BlockSpec((tm, tn), lambda i, j, kk: (i, j))
    if into is not None:
        buf, col_off = into
        off = col_off // tn
        assert col_shards == 1 and len(out_dtypes) == 1 and col_off % tn == 0 and out_dtypes[0] == buf.dtype
        o_spec, o_shape = pl.BlockSpec((tm, tn), lambda i, j, kk: (i, off + j)), buf.shape
    elif col_shards == 1:
        o_spec, o_shape = e_spec, (m, n)
    else:
        per = ncs // tn
        o_spec, o_shape = pl.BlockSpec((None, tm, tn), lambda i, j, kk: (j // per, i, j % per)), (col_shards, m, ncs)
    dims = (((0,) if ta else (1,)), ((1,) if tb else (0,)))
    n_extra = len(extras)
    n_out = len(out_dtypes)
    n_into = 0 if into is None else 1

    def body(*refs):
        a_ref, b_ref = refs[0], refs[1]
        extra_refs = refs[2:2 + n_extra]
        out_refs = refs[2 + n_extra + n_into:2 + n_extra + n_into + n_out]
        acc_ref = refs[-1]
        kk = pl.program_id(2)

        @pl.when(kk == 0)
        def _():
            acc_ref[...] = jnp.zeros_like(acc_ref)

        acc_ref[...] += _dot(a_ref[...].astype(MXU_DTYPE), b_ref[...].astype(MXU_DTYPE), dims)

        @pl.when(kk == nk - 1)
        def _():
            acc = acc_ref[...]
            outs = (acc,) if epilogue is None else epilogue(acc, *[r[...] for r in extra_refs])
            for o_ref, o in zip(out_refs, outs):
                o_ref[...] = o.astype(o_ref.dtype)

    outs = pl.pallas_call(
        body,
        grid=(m // tm, n // tn, nk),
        in_specs=[a_spec, b_spec] + [e_spec] * n_extra + [ANY] * n_into,
        out_specs=[o_spec] * n_out,
        out_shape=[jax.ShapeDtypeStruct(o_shape, dt) for dt in out_dtypes],
        input_output_aliases={2 + n_extra: 0} if n_into else {},
        scratch_shapes=[pltpu.VMEM((tm, tn), F32)],
        compiler_params=pltpu.CompilerParams(dimension_semantics=("parallel", "parallel", ARB)),
        name=name,
    )(a, b, *extras, *([] if into is None else [into[0]]))
    return outs[0] if n_out == 1 else tuple(outs)


def _rms_fwd(x, w, *, name, tm=512):
    s, d = x.shape
    out_dtype = MXU_DTYPE

    def body(x_ref, w_ref, o_ref):
        xv = x_ref[...]
        r = lax.rsqrt(jnp.mean(xv * xv, axis=-1, keepdims=True) + EPS)
        o_ref[...] = (xv * r * w_ref[...]).astype(o_ref.dtype)

    return pl.pallas_call(
        body, grid=(s // tm,),
        in_specs=[pl.BlockSpec((tm, d), lambda i: (i, 0)), pl.BlockSpec((1, d), lambda i: (0, 0))],
        out_specs=pl.BlockSpec((tm, d), lambda i: (i, 0)),
        out_shape=jax.ShapeDtypeStruct((s, d), out_dtype),
        compiler_params=_cparams(1), name=name,
    )(x, w.reshape(1, d))


def _rms_bwd(x, w, dh, dres, *, name, tm=512):
    s, d = x.shape

    def body(x_ref, w_ref, dh_ref, dres_ref, dx_ref, dw_ref):
        xv = x_ref[...]
        r = lax.rsqrt(jnp.mean(xv * xv, axis=-1, keepdims=True) + EPS)
        xh = xv * r
        dhv = dh_ref[...].astype(F32)
        dxn = dhv * w_ref[...]
        dx = r * (dxn - xh * jnp.mean(dxn * xh, axis=-1, keepdims=True))
        dx_ref[...] = dres_ref[...] + dx

        @pl.when(pl.program_id(0) == 0)
        def _():
            dw_ref[...] = jnp.zeros_like(dw_ref)

        dw_ref[...] += jnp.sum(dhv * xh, axis=0, keepdims=True)

    dx, dw = pl.pallas_call(
        body, grid=(s // tm,),
        in_specs=[pl.BlockSpec((tm, d), lambda i: (i, 0)), pl.BlockSpec((1, d), lambda i: (0, 0)),
                  pl.BlockSpec((tm, d), lambda i: (i, 0)), pl.BlockSpec((tm, d), lambda i: (i, 0))],
        out_specs=[pl.BlockSpec((tm, d), lambda i: (i, 0)), pl.BlockSpec((1, d), lambda i: (0, 0))],
        out_shape=[jax.ShapeDtypeStruct((s, d), F32), jax.ShapeDtypeStruct((1, d), F32)],
        compiler_params=_cparams(1), name=name,
    )(x, w.reshape(1, d), dh, dres)
    return dx, dw.reshape(d)


def _final_loss(x, w, target, *, name, tm=512):
    s, d = x.shape

    def body(x_ref, w_ref, t_ref, loss_ref, dx_ref, dw_ref):
        xv = x_ref[...]
        r = lax.rsqrt(jnp.mean(xv * xv, axis=-1, keepdims=True) + EPS)
        xh = xv * r
        err = xh * w_ref[...] - t_ref[...]
        dy = err * (1.0 / d)
        dxn = dy * w_ref[...]
        dx_ref[...] = r * (dxn - xh * jnp.mean(dxn * xh, axis=-1, keepdims=True))

        @pl.when(pl.program_id(0) == 0)
        def _():
            dw_ref[...] = jnp.zeros_like(dw_ref)
            loss_ref[...] = jnp.zeros_like(loss_ref)

        dw_ref[...] += jnp.sum(dy * xh, axis=0, keepdims=True)
        row = jnp.sum(err * err, axis=1, keepdims=True) * (0.5 / d)
        loss_ref[...] += jnp.sum(row, axis=0, keepdims=True)

    loss, dx, dw = pl.pallas_call(
        body, grid=(s // tm,),
        in_specs=[pl.BlockSpec((tm, d), lambda i: (i, 0)), pl.BlockSpec((1, d), lambda i: (0, 0)),
                  pl.BlockSpec((tm, d), lambda i: (i, 0))],
        out_specs=[pl.BlockSpec((1, 1), lambda i: (0, 0)), pl.BlockSpec((tm, d), lambda i: (i, 0)),
                   pl.BlockSpec((1, d), lambda i: (0, 0))],
        out_shape=[jax.ShapeDtypeStruct((1, 1), F32), jax.ShapeDtypeStruct((s, d), F32), jax.ShapeDtypeStruct((1, d), F32)],
        compiler_params=_cparams(1), name=name,
    )(x, w.reshape(1, d), target)
    return loss[0, 0], dx, dw.reshape(d)


def _shift_down(x, sh, t_idx):
    return jnp.where(t_idx >= sh, pltpu.roll(x, sh, 0), 0.0)


def _shift_up(x, sh, t_idx, s):
    return jnp.where(t_idx < s - sh, pltpu.roll(x, s - sh, 0), 0.0)


def _conv_pre(x, w_rows, b, t_idx):
    c = w_rows[CONV_K - 1] * x + b
    for sh in range(1, CONV_K):
        c = c + w_rows[CONV_K - 1 - sh] * _shift_down(x, sh, t_idx)
    return c


def _conv_fwd(src, col0, w, b, n_l2, *, name):
    s = src.shape[0]
    c_tot = w.shape[1]
    nblk = c_tot // LANES

    def body(x_ref, w_ref, b_ref, o_ref):
        j = pl.program_id(0)
        t_idx = lax.broadcasted_iota(jnp.int32, (s, LANES), 0)
        w_rows = [w_ref[kk:kk + 1, :] for kk in range(CONV_K)]
        y = _silu(_conv_pre(x_ref[...], w_rows, b_ref[...], t_idx))
        if n_l2 > 0:
            yn = y * lax.rsqrt(jnp.sum(y * y, axis=1, keepdims=True) + EPS)
            y = jnp.where(j < n_l2, yn, y)
        o_ref[...] = y

    return pl.pallas_call(
        body, grid=(nblk,),
        in_specs=[pl.BlockSpec((s, LANES), lambda j: (0, col0 + j)), pl.BlockSpec((CONV_K, LANES), lambda j: (0, j)),
                  pl.BlockSpec((1, LANES), lambda j: (0, j))],
        out_specs=pl.BlockSpec((s, LANES), lambda j: (0, j)),
        out_shape=jax.ShapeDtypeStruct((s, c_tot), F32),
        compiler_params=_cparams(1), name=name,
    )(src, w, b)


def _conv_bwd(src, col0, w, b, n_l2, dout, into, *, name):
    s = src.shape[0]
    c_tot = w.shape[1]
    nblk = c_tot // LANES

    def body(x_ref, w_ref, b_ref, do_ref, into_ref, dx_ref, dw_ref, db_ref):
        j = pl.program_id(0)
        t_idx = lax.broadcasted_iota(jnp.int32, (s, LANES), 0)
        xv = x_ref[...]
        w_rows = [w_ref[kk:kk + 1, :] for kk in range(CONV_K)]
        c = _conv_pre(xv, w_rows, b_ref[...], t_idx)
        dy = do_ref[...]
        y, y_grad = _silu_and_grad(c)
        if n_l2 > 0:
            r = lax.rsqrt(jnp.sum(y * y, axis=1, keepdims=True) + EPS)
            dyn = r * dy - y * (r * r * r) * jnp.sum(dy * y, axis=1, keepdims=True)
            dy = jnp.where(j < n_l2, dyn, dy)
        dc = dy * y_grad
        dx = w_rows[CONV_K - 1] * dc
        rows = [None] * CONV_K
        rows[CONV_K - 1] = jnp.sum(dc * xv, axis=0, keepdims=True)
        for sh in range(1, CONV_K):
            dx = dx + w_rows[CONV_K - 1 - sh] * _shift_up(dc, sh, t_idx, s)
            rows[CONV_K - 1 - sh] = jnp.sum(dc * _shift_down(xv, sh, t_idx), axis=0, keepdims=True)
        dx_ref[...] = dx.astype(dx_ref.dtype)
        for kk in range(CONV_K):
            dw_ref[kk:kk + 1, :] = rows[kk]
        db_ref[...] = jnp.sum(dc, axis=0, keepdims=True)

    return pl.pallas_call(
        body, grid=(nblk,),
        in_specs=[pl.BlockSpec((s, LANES), lambda j: (0, col0 + j)), pl.BlockSpec((CONV_K, LANES), lambda j: (0, j)),
                  pl.BlockSpec((1, LANES), lambda j: (0, j)), pl.BlockSpec((s, LANES), lambda j: (0, j)), ANY],
        out_specs=[pl.BlockSpec((s, LANES), lambda j: (0, col0 + j)), pl.BlockSpec((CONV_K, LANES), lambda j: (0, j)),
                   pl.BlockSpec((1, LANES), lambda j: (0, j))],
        out_shape=[jax.ShapeDtypeStruct(into.shape, into.dtype), jax.ShapeDtypeStruct((CONV_K, c_tot), F32),
                   jax.ShapeDtypeStruct((1, c_tot), F32)],
        input_output_aliases={4: 0},
        compiler_params=_cparams(1), name=name,
    )(src, w, b, dout, into)


def _chunk_masks(c):
    ii = lax.broadcasted_iota(jnp.int32, (c, c), 0)
    jj = lax.broadcasted_iota(jnp.int32, (c, c), 1)
    return ii, jj


def _row_to_col(row, eye):
    return jnp.sum(jnp.where(eye, row, 0.0), axis=1, keepdims=True)


def _each(f, *lists):
    return [f(*xs) for xs in zip(*lists)]


@jax.custom_vjp
def _nilpotent_inverse(nmats):
    c = nmats[0].shape[0]
    ii, jj = _chunk_masks(c)
    xinv = _each(lambda n: jnp.where(ii == jj, 1.0, 0.0) + n, nmats)
    pw = nmats
    for _ in range(int(math.log2(c)) - 1):
        pw = _each(lambda p: _dot(p, p, NN, HIGHEST), pw)
        xinv = _each(lambda x, p: x + _dot(x, p, NN, HIGHEST), xinv, pw)
    return xinv


def _nilpotent_inverse_fwd(nmats):
    xinv = _nilpotent_inverse(nmats)
    return xinv, xinv


def _nilpotent_inverse_bwd(xinv, cts):
    left = _each(lambda x, ct: _dot(x, ct, TN, HIGHEST), xinv, cts)
    return (_each(lambda l_, x: _dot(l_, x, NT, HIGHEST), left, xinv),)


_nilpotent_inverse.defvjp(_nilpotent_inverse_fwd, _nilpotent_inverse_bwd)


@jax.custom_vjp
def _saved_inverse(nmats, saved):
    return saved


def _saved_inverse_fwd(nmats, saved):
    return saved, saved


def _saved_inverse_bwd(xinv, cts):
    return _nilpotent_inverse_bwd(xinv, cts) + (_each(jnp.zeros_like, xinv),)


_saved_inverse.defvjp(_saved_inverse_fwd, _saved_inverse_bwd)


def _dn_chunk(q, k, v, a_row, b_row, alog, dtb, s0, saved_inverse=None):
    c = q[0].shape[0]
    ii, jj = _chunk_masks(c)
    causal, strict, eye = ii >= jj, ii > jj, ii == jj
    g_row = _each(lambda al, a, dt: -jnp.exp(al) * _softplus(a + dt), alog, a_row, dtb)
    beta_col = _each(lambda b: _row_to_col(_sigmoid(b), eye), b_row)
    g_col = _each(lambda g: _row_to_col(g, eye), g_row)
    gc_col = _each(lambda g: jnp.sum(jnp.where(causal, g, 0.0), axis=1, keepdims=True), g_row)
    gc_row = _each(lambda g: jnp.sum(jnp.where(jj >= ii, g, 0.0), axis=0, keepdims=True), g_col)
    decay = _each(lambda gc, gr: jnp.exp(jnp.where(causal, gc - gr, NEG_BIG)), gc_col, gc_row)
    kb = _each(jnp.multiply, k, beta_col)
    vb = _each(jnp.multiply, v, beta_col)
    nmat = _each(lambda kb_, k_, dc: -jnp.where(strict, _dot(kb_, k_, NT, HIGHEST) * dc, 0.0), kb, k, decay)
    xinv = _nilpotent_inverse(nmat) if saved_inverse is None else _saved_inverse(nmat, saved_inverse)
    egc = _each(jnp.exp, gc_col)
    dv = v[0].shape[1]
    uw = _each(lambda x, vb_, kb_, e: _dot(x, jnp.concatenate([vb_, kb_ * e], axis=1), NN, HIGHEST), xinv, vb, kb, egc)
    u = _each(lambda t: t[:, :dv], uw)
    w = _each(lambda t: t[:, dv:], uw)
    qs = _each(lambda q_: q_ * (q_.shape[1] ** -0.5), q)
    attn = _each(lambda q_, k_, dc: _sdot(q_, k_, NT) * dc, qs, k, decay)
    gl = _each(lambda g: jnp.sum(g, axis=1, keepdims=True), g_row)
    kd = _each(lambda k_, gl_, gc: k_ * jnp.exp(gl_ - gc), k, gl, gc_col)
    v_new = _each(lambda u_, w_, s: u_ - _sdot(w_, s), u, w, s0)
    o = _each(lambda q_, e, s, at, vn: _sdot(q_ * e, s) + _sdot(at, vn), qs, egc, s0, attn, v_new)
    s1 = _each(lambda s, gl_, kd_, vn: s * jnp.exp(gl_) + _sdot(kd_, vn, TN), s0, gl, kd, v_new)
    return (o, s1), xinv


def _dn_specs(nh, nc, hb, rev):
    n_of = (lambda n: nc - 1 - n) if rev else (lambda n: n)
    ng = nh // hb
    qkv = [pl.BlockSpec((CHUNK, hb * DN_HEAD_DIM), (lambda h, n, o=o: (n_of(n), o * ng + h))) for o in range(3)]
    row = pl.BlockSpec((hb, None, 1, CHUNK), lambda h, n: (h, n_of(n), 0, 0))
    scal = pl.BlockSpec((hb, 1, 1), lambda h, n: (h, 0, 0))
    o_spec = pl.BlockSpec((CHUNK, hb * DN_HEAD_DIM), lambda h, n: (n_of(n), h))
    st = pl.BlockSpec((hb, None, DN_HEAD_DIM, DN_HEAD_DIM), lambda h, n: (h, n_of(n), 0, 0))
    inv = pl.BlockSpec((hb, None, CHUNK, CHUNK), lambda h, n: (h, n_of(n), 0, 0))
    return qkv, row, scal, o_spec, st, inv


def _dn_fwd(qkv, a_rows, b_rows, alog, dtb, *, name):
    s = qkv.shape[0]
    nh, nc = a_rows.shape[0], a_rows.shape[1]
    hb = min(DN_HEADS_PER_STEP, nh)
    qkv_specs, row, scal, o_spec, st, inv = _dn_specs(nh, nc, hb, False)
    hd = DN_HEAD_DIM

    def body(q_ref, k_ref, v_ref, a_ref, b_ref, al_ref, dt_ref, o_ref, st_ref, inv_ref, state):
        @pl.when(pl.program_id(1) == 0)
        def _():
            state[...] = jnp.zeros_like(state)

        cols = [slice(h * hd, (h + 1) * hd) for h in range(hb)]
        s0 = [state[h] for h in range(hb)]
        for h in range(hb):
            st_ref[h] = s0[h]
        (o, s1), xinv = _dn_chunk(
            [q_ref[:, cl] for cl in cols], [k_ref[:, cl] for cl in cols], [v_ref[:, cl] for cl in cols],
            [a_ref[h] for h in range(hb)], [b_ref[h] for h in range(hb)],
            [al_ref[h] for h in range(hb)], [dt_ref[h] for h in range(hb)], s0)
        for h in range(hb):
            o_ref[:, cols[h]] = o[h]
            inv_ref[h] = xinv[h]
            state[h] = s1[h]

    return pl.pallas_call(
        body, grid=(nh // hb, nc),
        in_specs=qkv_specs + [row, row, scal, scal],
        out_specs=[o_spec, st, inv],
        out_shape=[jax.ShapeDtypeStruct((s, nh * hd), F32), jax.ShapeDtypeStruct((nh, nc, hd, hd), F32),
                   jax.ShapeDtypeStruct((nh, nc, CHUNK, CHUNK), F32)],
        scratch_shapes=[pltpu.VMEM((hb, hd, hd), F32)],
        compiler_params=_cparams(2), name=name,
    )(qkv, qkv, qkv, a_rows, b_rows, alog, dtb)


def _dn_bwd(qkv, a_rows, b_rows, alog, dtb, states, inverses, do, *, name):
    s = qkv.shape[0]
    nh, nc = a_rows.shape[0], a_rows.shape[1]
    hb = min(DN_HEADS_PER_STEP, nh)
    qkv_specs, row, scal, o_spec, st, inv = _dn_specs(nh, nc, hb, True)
    hd = DN_HEAD_DIM

    assert hb == nh, "dq | dk | dv are written as one [S, 3W] array: all heads in one grid step"
    w = nh * hd

    def body(q_ref, k_ref, v_ref, a_ref, b_ref, al_ref, dt_ref, st_ref, inv_ref, do_ref,
             dqkv_ref, da_ref, db_ref, dal_ref, ddt_ref, dstate):
        @pl.when(pl.program_id(1) == 0)
        def _():
            dstate[...] = jnp.zeros_like(dstate)
            dal_ref[...] = jnp.zeros_like(dal_ref)
            ddt_ref[...] = jnp.zeros_like(ddt_ref)

        cols = [slice(h * hd, (h + 1) * hd) for h in range(hb)]
        heads = range(hb)
        args = ([q_ref[:, cl] for cl in cols], [k_ref[:, cl] for cl in cols], [v_ref[:, cl] for cl in cols],
                [a_ref[h] for h in heads], [b_ref[h] for h in heads], [al_ref[h] for h in heads],
                [dt_ref[h] for h in heads], [st_ref[h] for h in heads])
        saved = [inv_ref[h] for h in heads]
        _, vjp, _ = jax.vjp(lambda *a: _dn_chunk(*a, saved_inverse=saved), *args, has_aux=True)
        dq, dk, dv, da, db, dal, ddt, ds0 = vjp(([do_ref[:, cl] for cl in cols], [dstate[h] for h in heads]))
        for h in heads:
            dqkv_ref[:, h * hd:(h + 1) * hd] = dq[h]
            dqkv_ref[:, w + h * hd:w + (h + 1) * hd] = dk[h]
            dqkv_ref[:, 2 * w + h * hd:2 * w + (h + 1) * hd] = dv[h]
            da_ref[h] = da[h]
            db_ref[h] = db[h]
            dal_ref[h] += dal[h]
            ddt_ref[h] += ddt[h]
            dstate[h] = ds0[h]

    n_of = lambda n: nc - 1 - n
    outs = pl.pallas_call(
        body, grid=(nh // hb, nc),
        in_specs=qkv_specs + [row, row, scal, scal, st, inv, o_spec],
        out_specs=[pl.BlockSpec((CHUNK, 3 * w), lambda h, n: (n_of(n), 0)), row, row, scal, scal],
        out_shape=[jax.ShapeDtypeStruct((s, 3 * w), F32)]
        + [jax.ShapeDtypeStruct(a_rows.shape, F32)] * 2 + [jax.ShapeDtypeStruct((nh, 1, 1), F32)] * 2,
        scratch_shapes=[pltpu.VMEM((hb, hd, hd), F32)],
        compiler_params=_cparams(2), name=name,
    )(qkv, qkv, qkv, a_rows, b_rows, alog, dtb, states, inverses, do)
    return outs


def _dn_post_fwd(o, src, gate_col0, nw, *, name, tm=512):
    s, w = o.shape
    hd = DN_HEAD_DIM
    gc = gate_col0 * LANES // w

    def body(o_ref, g_ref, w_ref, y_ref):
        for h in range(w // hd):
            cols = slice(h * hd, (h + 1) * hd)
            ov = o_ref[:, cols]
            r = lax.rsqrt(jnp.mean(ov * ov, axis=-1, keepdims=True) + EPS)
            y_ref[:, cols] = (ov * r * w_ref[...] * _silu(g_ref[:, cols])).astype(y_ref.dtype)

    blk = pl.BlockSpec((tm, w), lambda i: (i, 0))
    return pl.pallas_call(
        body, grid=(s // tm,),
        in_specs=[blk, pl.BlockSpec((tm, w), lambda i: (i, gc)), pl.BlockSpec((1, hd), lambda i: (0, 0))],
        out_specs=blk, out_shape=jax.ShapeDtypeStruct((s, w), MXU_DTYPE),
        compiler_params=_cparams(1), name=name,
    )(o, src, nw.reshape(1, hd))


def _dn_post_bwd(o, src, gate_col0, nw, dy, into, *, name, tm=512):
    s, w = o.shape
    hd = DN_HEAD_DIM
    gc = gate_col0 * LANES // w

    def body(o_ref, g_ref, w_ref, dy_ref, into_ref, do_ref, dg_ref, dw_ref):
        @pl.when(pl.program_id(0) == 0)
        def _():
            dw_ref[...] = jnp.zeros_like(dw_ref)

        dw = jnp.zeros((1, hd), F32)
        for h in range(w // hd):
            cols = slice(h * hd, (h + 1) * hd)
            ov, gv, dyv = o_ref[:, cols], g_ref[:, cols], dy_ref[:, cols]
            r = lax.rsqrt(jnp.mean(ov * ov, axis=-1, keepdims=True) + EPS)
            oh = ov * r
            sg, sg_grad = _silu_and_grad(gv)
            dn = dyv * sg
            dg_ref[:, cols] = (dyv * (oh * w_ref[...]) * sg_grad).astype(dg_ref.dtype)
            don = dn * w_ref[...]
            do_ref[:, cols] = r * (don - oh * jnp.mean(don * oh, axis=-1, keepdims=True))
            dw = dw + jnp.sum(dn * oh, axis=0, keepdims=True)
        dw_ref[...] += dw

    blk = pl.BlockSpec((tm, w), lambda i: (i, 0))
    wspec = pl.BlockSpec((1, hd), lambda i: (0, 0))
    gate_blk = pl.BlockSpec((tm, w), lambda i: (i, gc))
    do, dg, dw = pl.pallas_call(
        body, grid=(s // tm,),
        in_specs=[blk, gate_blk, wspec, blk, ANY],
        out_specs=[blk, gate_blk, wspec],
        out_shape=[jax.ShapeDtypeStruct((s, w), F32), jax.ShapeDtypeStruct(into.shape, into.dtype),
                   jax.ShapeDtypeStruct((1, hd), F32)],
        input_output_aliases={4: 1},
        compiler_params=_cparams(1), name=name,
    )(o, src, nw.reshape(1, hd), dy, into)
    return do, dg, dw.reshape(hd)


def _sb_consts():
    r2 = lax.broadcasted_iota(jnp.int32, (2 * SB_BLOCK, SB_BLOCK), 0)
    c2 = lax.broadcasted_iota(jnp.int32, (2 * SB_BLOCK, SB_BLOCK), 1)
    r = lax.broadcasted_iota(jnp.int32, (SB_BLOCK, SB_BLOCK), 0)
    c = lax.broadcasted_iota(jnp.int32, (SB_BLOCK, SB_BLOCK), 1)
    lm0 = c < SB_HEAD_DIM
    m_gt = jnp.where(r > c, 1.0, 0.0).astype(BF16)
    m_lt = jnp.where(r < c, 1.0, 0.0).astype(BF16)
    return r2, c2, lm0, m_gt, m_lt


def _sb_stack(x, lm0):
    return jnp.concatenate([jnp.where(lm0, x, 0.0), jnp.where(lm0, 0.0, x)], axis=0)


def _sb_unstack(x2, lm0):
    return jnp.where(lm0, x2[:SB_BLOCK], x2[SB_BLOCK:])


def _sb_fwd(src, col0, width, *, name):
    s = src.shape[0]
    nq = s // SB_BLOCK
    npair = width // LANES
    scale = SB_HEAD_DIM ** -0.5
    nu = math.gcd(SB_UNROLL, nq)

    def body(q_ref, k_ref, v_ref, o_ref, w_hbm, stage, sems):
        p, i = pl.program_id(0), pl.program_id(1)
        r2, c2, lm0, m_gt, _ = _sb_consts()
        t_glob = i * SB_BLOCK + (r2 & (SB_BLOCK - 1))
        q2 = (_sb_stack(q_ref[...], lm0) * scale).astype(MXU_DTYPE)

        t = p * nq + i
        half = t % 2
        ngrp = nq // nu

        def save(half_, grp, pp, ii):
            return pltpu.make_async_copy(stage.at[half_, grp], w_hbm.at[pp, ii, grp], sems.at[half_, grp])

        def drain(half_, pp, ii):
            for grp in range(ngrp):
                @pl.when(grp <= ii // nu)
                def _():
                    save(half_, grp, pp, ii).wait()

        def group(base, carry, masked):
            o2, rsum = carry
            js = [base + nu - 1 - u for u in range(nu)]
            offs = [pl.multiple_of(j * SB_BLOCK, SB_BLOCK) for j in js]
            zs = [_dot(q2, k_ref[pl.ds(off, SB_BLOCK), :].astype(MXU_DTYPE), NT) for off in offs]
            ts = [jnp.log(1.0 + jnp.exp(-jnp.abs(z))) for z in zs]
            lks = [-(jnp.maximum(z, 0.0) + t) for z, t in zip(zs, ts)]
            if masked:
                masks = [(j * SB_BLOCK + c2) < t_glob for j in js]
                lks = [jnp.where(mk, lk, 0.0) for mk, lk in zip(masks, lks)]
            sufs = [_split_dot(lk, m_gt, SB_SPLIT) for lk in lks]
            rs = [rsum]
            for lk in lks:
                rs.append(rs[-1] + jnp.sum(lk, axis=1, keepdims=True))
            wgts = [jnp.exp((jnp.minimum(z, 0.0) - t) + r_ + sf) for z, t, r_, sf in zip(zs, ts, rs, sufs)]
            if masked:
                wgts = [jnp.where(mk, wg, 0.0) for mk, wg in zip(masks, wgts)]
            wbs = [wg.astype(MXU_DTYPE) for wg in wgts]
            grp = base // nu
            for u, wb in enumerate(wbs):
                stage[half, grp, nu - 1 - u] = wb
            save(half, grp, p, i).start()
            for off, wb in zip(offs, wbs):
                o2 = o2 + _dot(wb, v_ref[pl.ds(off, SB_BLOCK), :].astype(MXU_DTYPE), NN)
            return o2, rs[-1]

        top0 = (i // nu) * nu
        last = i // nu
        carry = group(top0, (jnp.zeros((2 * SB_BLOCK, LANES), F32), jnp.zeros((2 * SB_BLOCK, 1), F32)), True)
        o2, _ = lax.fori_loop(1, last + 1, lambda g, cr: group(top0 - nu * g, cr, False), carry)
        o_ref[...] = _sb_unstack(o2, lm0)

        @pl.when(t >= 1)
        def _():
            drain(1 - half, (t - 1) // nq, (t - 1) % nq)

        @pl.when(t == npair * nq - 1)
        def _():
            drain(half, p, i)

    blk = pl.BlockSpec((SB_BLOCK, LANES), lambda p, i: (i, p))
    return pl.pallas_call(
        body, grid=(npair, nq),
        in_specs=[pl.BlockSpec((SB_BLOCK, LANES), lambda p, i: (i, col0 + p)),
                  pl.BlockSpec((s, LANES), lambda p, i: (0, col0 + npair + p)),
                  pl.BlockSpec((s, LANES), lambda p, i: (0, col0 + 2 * npair + p))],
        out_specs=[blk, ANY],
        out_shape=[jax.ShapeDtypeStruct((s, width), F32),
                   jax.ShapeDtypeStruct((npair, nq, nq // nu, nu, 2 * SB_BLOCK, LANES), MXU_DTYPE)],
        scratch_shapes=[pltpu.VMEM((2, nq // nu, nu, 2 * SB_BLOCK, LANES), MXU_DTYPE),
                        pltpu.SemaphoreType.DMA((2, nq // nu))],
        compiler_params=_cparams(2), name=name,
    )(src, src, src)


def _sb_bwd(src, col0, width, weights, do, *, name):
    s = src.shape[0]
    nq = s // SB_BLOCK
    npair = width // LANES
    scale = SB_HEAD_DIM ** -0.5
    nu = math.gcd(SB_UNROLL, nq)

    def body(q_ref, k_ref, v_ref, w_hbm, do_ref, dq_ref, dk_ref, dv_ref, stage, sems):
        p, i = pl.program_id(0), pl.program_id(1)

        @pl.when(i == 0)
        def _():
            dk_ref[...] = jnp.zeros_like(dk_ref)
            dv_ref[...] = jnp.zeros_like(dv_ref)

        r2, c2, lm0, _, m_lt = _sb_consts()
        t_glob = i * SB_BLOCK + (r2 & (SB_BLOCK - 1))
        q2 = (_sb_stack(q_ref[...], lm0) * scale).astype(MXU_DTYPE)
        do2 = _sb_stack(do_ref[...], lm0).astype(MXU_DTYPE)

        ngrp = nq // nu

        def load(half_, grp, pp, ii):
            return pltpu.make_async_copy(w_hbm.at[pp, ii, grp], stage.at[half_, grp], sems.at[half_, grp])

        def fetch_step(half_, pp, ii):
            for grp in range(ngrp):
                @pl.when(grp <= ii // nu)
                def _():
                    load(half_, grp, pp, ii).start()

        def group(g, carry, masked, slot):
            dq2, csum = carry
            js = [nu * g + u for u in range(nu)]
            offs = [pl.multiple_of(j * SB_BLOCK, SB_BLOCK) for j in js]
            kbs = [k_ref[pl.ds(off, SB_BLOCK), :].astype(MXU_DTYPE) for off in offs]
            zs = [_dot(q2, kb, NT) for kb in kbs]
            dws = [_dot(do2, v_ref[pl.ds(off, SB_BLOCK), :].astype(MXU_DTYPE), NT) for off in offs]
            wbs = [stage[slot[0], slot[1], u] for u in range(nu)]
            sigs = [_sigmoid(z) for z in zs]
            dlogas = [wb.astype(F32) * dw for wb, dw in zip(wbs, dws)]
            pres = [_split_dot(dl, m_lt, SB_SPLIT) for dl in dlogas]
            dlks = []
            for dl, pre in zip(dlogas, pres):
                dlks.append(csum + pre)
                csum = csum + jnp.sum(dl, axis=1, keepdims=True)
            if masked:
                dlks = [jnp.where((j * SB_BLOCK + c2) < t_glob, dlk, 0.0) for j, dlk in zip(js, dlks)]
            dzbs = [(dl * (1.0 - sg) - dlk * sg).astype(MXU_DTYPE) for dl, sg, dlk in zip(dlogas, sigs, dlks)]
            for off, dzb, wb, kb in zip(offs, dzbs, wbs, kbs):
                dk_ref[pl.ds(off, SB_BLOCK), :] += _dot(dzb, q2, TN)
                dv_ref[pl.ds(off, SB_BLOCK), :] += _dot(wb, do2, TN)
                dq2 = dq2 + _dot(dzb, kb, NN)
            return dq2, csum

        t = p * nq + i
        half = t % 2

        @pl.when(t == 0)
        def _():
            fetch_step(0, p, i)

        @pl.when(t + 1 < npair * nq)
        def _():
            fetch_step(1 - half, (t + 1) // nq, (t + 1) % nq)

        def step(g, carry):
            load(half, g, p, i).wait()
            return group(g, carry, False, (half, g))

        last = i // nu
        carry = lax.fori_loop(0, last, step, (jnp.zeros((2 * SB_BLOCK, LANES), F32), jnp.zeros((2 * SB_BLOCK, 1), F32)))
        load(half, last, p, i).wait()
        dq2, _ = group(last, carry, True, (half, last))
        dq_ref[...] = _sb_unstack(dq2, lm0) * scale

    blk = pl.BlockSpec((SB_BLOCK, LANES), lambda p, i: (i, p))
    full = pl.BlockSpec((s, LANES), lambda p, i: (0, p))
    return pl.pallas_call(
        body, grid=(npair, nq),
        in_specs=[pl.BlockSpec((SB_BLOCK, LANES), lambda p, i: (i, col0 + p)),
                  pl.BlockSpec((s, LANES), lambda p, i: (0, col0 + npair + p)),
                  pl.BlockSpec((s, LANES), lambda p, i: (0, col0 + 2 * npair + p)),
                  ANY, blk],
        out_specs=[blk, full, full],
        out_shape=[jax.ShapeDtypeStruct((s, width), F32)] * 3,
        scratch_shapes=[pltpu.VMEM((2, nq // nu, nu, 2 * SB_BLOCK, LANES), MXU_DTYPE),
                        pltpu.SemaphoreType.DMA((2, nq // nu))],
        compiler_params=_cparams(2), name=name,
    )(src, src, src, weights, do)


def _ssd_group(xs, dt_rows, alogs, dtbs, bms, cms, h0s):
    c = bms[0].shape[0]
    per = len(xs) // len(bms)
    ii, jj = _chunk_masks(c)
    causal, eye = ii >= jj, ii == jj
    scores = [t for t in _each(lambda c_, b_: _sdot(c_, b_, NT), cms, bms) for _ in range(per)]
    dt_r = _each(lambda dt, b: _softplus(dt + b), dt_rows, dtbs)
    a_r = _each(lambda al, dt: -jnp.exp(al) * dt, alogs, dt_r)
    dt_col = _each(lambda dt: _row_to_col(dt, eye), dt_r)
    a_col = _each(lambda a: _row_to_col(a, eye), a_r)
    ac_col = _each(lambda a: jnp.sum(jnp.where(causal, a, 0.0), axis=1, keepdims=True), a_r)
    ac_row = _each(lambda a: jnp.sum(jnp.where(jj >= ii, a, 0.0), axis=0, keepdims=True), a_col)
    lmat = _each(lambda c_, r_: jnp.exp(jnp.where(causal, c_ - r_, NEG_BIG)), ac_col, ac_row)
    xdt = _each(jnp.multiply, xs, dt_col)
    al = _each(lambda a: jnp.sum(a, axis=1, keepdims=True), a_r)
    bm = [t for t in bms for _ in range(per)]
    cm = [t for t in cms for _ in range(per)]
    ys = _each(lambda sc, lm, xd, cm_, h0, ac: _sdot(sc * lm, xd) + _sdot(cm_, h0, NT) * jnp.exp(ac),
               scores, lmat, xdt, cm, h0s, ac_col)
    h1s = _each(lambda h0, al_, xd, ac, bm_: h0 * jnp.exp(al_) + _sdot(xd * jnp.exp(al_ - ac), bm_, TN),
                h0s, al, xdt, ac_col, bm)
    return ys, h1s


def _ssd_specs(ng, nc, r, gb, rev):
    n_of = (lambda n: nc - 1 - n) if rev else (lambda n: n)
    xw, bw = gb * r * SSM_HEAD_DIM, gb * SSM_STATE
    b0, c0 = (ng * r * SSM_HEAD_DIM) // bw, (ng * r * SSM_HEAD_DIM + ng * SSM_STATE) // bw
    x_spec = pl.BlockSpec((CHUNK, xw), lambda g, n: (n_of(n), g))
    b_spec = pl.BlockSpec((CHUNK, bw), lambda g, n: (n_of(n), b0 + g))
    c_spec = pl.BlockSpec((CHUNK, bw), lambda g, n: (n_of(n), c0 + g))
    dt_spec = pl.BlockSpec((gb, None, r, CHUNK), lambda g, n: (g, n_of(n), 0, 0))
    sc_spec = pl.BlockSpec((gb, r, 1), lambda g, n: (g, 0, 0))
    st_spec = pl.BlockSpec((gb, None, r, SSM_HEAD_DIM, SSM_STATE), lambda g, n: (g, n_of(n), 0, 0, 0))
    bc_out = pl.BlockSpec((CHUNK, bw), lambda g, n: (n_of(n), g))
    return x_spec, b_spec, c_spec, dt_spec, sc_spec, st_spec, x_spec, bc_out


def _ssd_refs(gb, r, x_ref, b_ref, c_ref, dt_ref, al_ref, db_ref):
    p, n = SSM_HEAD_DIM, SSM_STATE
    heads = [(g, h) for g in range(gb) for h in range(r)]
    xs = [x_ref[:, (g * r + h) * p:(g * r + h + 1) * p] for g, h in heads]
    dts = [dt_ref[g, h:h + 1, :] for g, h in heads]
    als = [al_ref[g, h:h + 1, :] for g, h in heads]
    dbs = [db_ref[g, h:h + 1, :] for g, h in heads]
    bms = [b_ref[:, g * n:(g + 1) * n] for g in range(gb)]
    cms = [c_ref[:, g * n:(g + 1) * n] for g in range(gb)]
    return heads, xs, dts, als, dbs, bms, cms


def _ssd_fwd(xbc, dt_rows, alog, dtb, *, name):
    s = xbc.shape[0]
    ng, nc, r = dt_rows.shape[0], dt_rows.shape[1], dt_rows.shape[2]
    w = ng * r * SSM_HEAD_DIM
    gb = math.gcd(SSD_GROUPS_PER_STEP, ng)
    x_spec, b_spec, c_spec, dt_spec, sc_spec, st_spec, y_spec, _ = _ssd_specs(ng, nc, r, gb, False)
    p = SSM_HEAD_DIM

    def body(x_ref, b_ref, c_ref, dt_ref, al_ref, db_ref, y_ref, st_ref, state):
        @pl.when(pl.program_id(1) == 0)
        def _():
            state[...] = jnp.zeros_like(state)

        st_ref[...] = state[...]
        heads, xs, dts, als, dbs, bms, cms = _ssd_refs(gb, r, x_ref, b_ref, c_ref, dt_ref, al_ref, db_ref)
        ys, h1s = _ssd_group(xs, dts, als, dbs, bms, cms, [state[g, h] for g, h in heads])
        for i, (g, h) in enumerate(heads):
            y_ref[:, (g * r + h) * p:(g * r + h + 1) * p] = ys[i]
            state[g, h] = h1s[i]

    return pl.pallas_call(
        body, grid=(ng // gb, nc),
        in_specs=[x_spec, b_spec, c_spec, dt_spec, sc_spec, sc_spec],
        out_specs=[y_spec, st_spec],
        out_shape=[jax.ShapeDtypeStruct((s, w), F32), jax.ShapeDtypeStruct((ng, nc, r, p, SSM_STATE), F32)],
        scratch_shapes=[pltpu.VMEM((gb, r, p, SSM_STATE), F32)],
        compiler_params=_cparams(2), name=name,
    )(xbc, xbc, xbc, dt_rows, alog, dtb)


def _ssd_bwd(xbc, dt_rows, alog, dtb, states, dy, dx_extra, *, name):
    s = xbc.shape[0]
    ng, nc, r = dt_rows.shape[0], dt_rows.shape[1], dt_rows.shape[2]
    w = ng * r * SSM_HEAD_DIM
    gb = math.gcd(SSD_GROUPS_PER_STEP, ng)
    x_spec, b_spec, c_spec, dt_spec, sc_spec, st_spec, y_spec, bc_out = _ssd_specs(ng, nc, r, gb, True)
    p = SSM_HEAD_DIM

    def body(x_ref, b_ref, c_ref, dt_ref, al_ref, db_ref, st_ref, dy_ref, dxe_ref,
             dx_ref, dbm_ref, dcm_ref, ddt_ref, dal_ref, ddb_ref, dstate):
        @pl.when(pl.program_id(1) == 0)
        def _():
            dstate[...] = jnp.zeros_like(dstate)
            dal_ref[...] = jnp.zeros_like(dal_ref)
            ddb_ref[...] = jnp.zeros_like(ddb_ref)

        heads, xs, dts, als, dbs, bms, cms = _ssd_refs(gb, r, x_ref, b_ref, c_ref, dt_ref, al_ref, db_ref)
        _, vjp = jax.vjp(_ssd_group, xs, dts, als, dbs, bms, cms, [st_ref[g, h] for g, h in heads])
        dys = [dy_ref[:, (g * r + h) * p:(g * r + h + 1) * p] for g, h in heads]
        dxs, ddts, dals, ddbs, dbms, dcms, dh0s = vjp((dys, [dstate[g, h] for g, h in heads]))
        for g in range(gb):
            dbm_ref[:, g * SSM_STATE:(g + 1) * SSM_STATE] = dbms[g]
            dcm_ref[:, g * SSM_STATE:(g + 1) * SSM_STATE] = dcms[g]
        for i, (g, h) in enumerate(heads):
            dx_ref[:, (g * r + h) * p:(g * r + h + 1) * p] = dxs[i] + dxe_ref[:, (g * r + h) * p:(g * r + h + 1) * p]
            ddt_ref[g, h:h + 1, :] = ddts[i]
            dal_ref[g, h:h + 1, :] += dals[i]
            ddb_ref[g, h:h + 1, :] += ddbs[i]
            dstate[g, h] = dh0s[i]

    gn = ng * SSM_STATE
    return pl.pallas_call(
        body, grid=(ng // gb, nc),
        in_specs=[x_spec, b_spec, c_spec, dt_spec, sc_spec, sc_spec, st_spec, y_spec, y_spec],
        out_specs=[y_spec, bc_out, bc_out, dt_spec, sc_spec, sc_spec],
        out_shape=[jax.ShapeDtypeStruct((s, w), F32), jax.ShapeDtypeStruct((s, gn), F32), jax.ShapeDtypeStruct((s, gn), F32),
                   jax.ShapeDtypeStruct(dt_rows.shape, F32), jax.ShapeDtypeStruct((ng, r, 1), F32),
                   jax.ShapeDtypeStruct((ng, r, 1), F32)],
        scratch_shapes=[pltpu.VMEM((gb, r, p, SSM_STATE), F32)],
        compiler_params=_cparams(2), name=name,
    )(xbc, xbc, xbc, dt_rows, alog, dtb, states, dy, dx_extra)


def _ssm_post_fwd(y, xbc, src, z_col0, dexp, nw, *, name, tm=512):
    s, w = y.shape
    gw = w // SSM_GROUPS
    zc = z_col0 * LANES // gw

    def body(y_ref, x_ref, z_ref, d_ref, w_ref, o_ref):
        yy = (y_ref[...] + x_ref[...] * d_ref[...]) * _silu(z_ref[...])
        r = lax.rsqrt(jnp.mean(yy * yy, axis=-1, keepdims=True) + EPS)
        o_ref[...] = (yy * r * w_ref[...]).astype(o_ref.dtype)

    blk = pl.BlockSpec((tm, gw), lambda g, i: (i, g))
    vec = pl.BlockSpec((1, gw), lambda g, i: (0, g))
    return pl.pallas_call(
        body, grid=(SSM_GROUPS, s // tm),
        in_specs=[blk, blk, pl.BlockSpec((tm, gw), lambda g, i: (i, zc + g)), vec, vec],
        out_specs=blk, out_shape=jax.ShapeDtypeStruct((s, w), MXU_DTYPE),
        compiler_params=_cparams(2), name=name,
    )(y, xbc, src, dexp.reshape(1, w), nw.reshape(1, w))


def _ssm_post_bwd(y, xbc, src, z_col0, dexp, nw, dout, into, *, name, tm=512):
    s, w = y.shape
    gw = w // SSM_GROUPS
    zc = z_col0 * LANES // gw

    def body(y_ref, x_ref, z_ref, d_ref, w_ref, do_ref, into_ref, dy_ref, dx_ref, dz_ref, dd_ref, dw_ref):
        xv, zv, dv = x_ref[...], z_ref[...], d_ref[...]
        pre = y_ref[...] + xv * dv
        sz, sz_grad = _silu_and_grad(zv)
        yy = pre * sz
        r = lax.rsqrt(jnp.mean(yy * yy, axis=-1, keepdims=True) + EPS)
        yh = yy * r
        dov = do_ref[...]
        dyn = dov * w_ref[...]
        dyy = r * (dyn - yh * jnp.mean(dyn * yh, axis=-1, keepdims=True))
        dpre = dyy * sz
        dy_ref[...] = dpre
        dx_ref[...] = dpre * dv
        dz_ref[...] = (dyy * pre * sz_grad).astype(dz_ref.dtype)

        @pl.when(pl.program_id(1) == 0)
        def _():
            dd_ref[...] = jnp.zeros_like(dd_ref)
            dw_ref[...] = jnp.zeros_like(dw_ref)

        dd_ref[...] += jnp.sum(dpre * xv, axis=0, keepdims=True)
        dw_ref[...] += jnp.sum(dov * yh, axis=0, keepdims=True)

    blk = pl.BlockSpec((tm, gw), lambda g, i: (i, g))
    vec = pl.BlockSpec((1, gw), lambda g, i: (0, g))
    z_blk = pl.BlockSpec((tm, gw), lambda g, i: (i, zc + g))
    dy, dx, dz, dd, dw = pl.pallas_call(
        body, grid=(SSM_GROUPS, s // tm),
        in_specs=[blk, blk, z_blk, vec, vec, blk, ANY],
        out_specs=[blk, blk, z_blk, vec, vec],
        out_shape=[jax.ShapeDtypeStruct((s, w), F32), jax.ShapeDtypeStruct((s, w), F32),
                   jax.ShapeDtypeStruct(into.shape, into.dtype), jax.ShapeDtypeStruct((1, w), F32),
                   jax.ShapeDtypeStruct((1, w), F32)],
        input_output_aliases={6: 2},
        compiler_params=_cparams(2), name=name,
    )(y, xbc, src, dexp.reshape(1, w), nw.reshape(1, w), dout, into)
    return dy, dx, dz, dd.reshape(w), dw.reshape(w)


def _merge_fwd(proj3, src, gate_col0, d, *, name, tm=512):
    s = proj3.shape[0]
    nb = proj3.shape[1] // d
    gc = gate_col0 * LANES // d

    def body(*refs):
        p_refs, g_refs, o_ref = refs[:nb], refs[nb:2 * nb], refs[-1]
        acc = None
        for p_ref, g_ref in zip(p_refs, g_refs):
            term = _sigmoid(g_ref[...]) * p_ref[...]
            acc = term if acc is None else acc + term
        o_ref[...] = acc.astype(o_ref.dtype)

    p_specs = [pl.BlockSpec((tm, d), lambda i, b=b: (i, b)) for b in range(nb)]
    g_specs = [pl.BlockSpec((tm, d), lambda i, b=b: (i, gc + b)) for b in range(nb)]
    return pl.pallas_call(
        body, grid=(s // tm,), in_specs=p_specs + g_specs,
        out_specs=pl.BlockSpec((tm, d), lambda i: (i, 0)), out_shape=jax.ShapeDtypeStruct((s, d), MXU_DTYPE),
        compiler_params=_cparams(1), name=name,
    )(*([proj3] * nb), *([src] * nb))


def _merge_bwd(proj3, src, gate_col0, d, dmerged, into, *, name, tm=512):
    s = proj3.shape[0]
    nb = proj3.shape[1] // d
    gc = gate_col0 * LANES // d

    def body(p_ref, g_ref, dm_ref, into_ref, dp_ref, dg_ref):
        sg = _sigmoid(g_ref[...])
        dm = dm_ref[...]
        dp_ref[...] = (dm * sg).astype(dp_ref.dtype)
        dg_ref[...] = (dm * p_ref[...] * sg * (1.0 - sg)).astype(dg_ref.dtype)

    blk = pl.BlockSpec((tm, d), lambda i, b: (i, b))
    gate_blk = pl.BlockSpec((tm, d), lambda i, b: (i, gc + b))
    return pl.pallas_call(
        body, grid=(s // tm, nb),
        in_specs=[blk, gate_blk, pl.BlockSpec((tm, d), lambda i, b: (i, 0)), ANY],
        out_specs=[blk, gate_blk],
        out_shape=[jax.ShapeDtypeStruct(proj3.shape, MXU_DTYPE), jax.ShapeDtypeStruct(into.shape, into.dtype)],
        input_output_aliases={3: 1},
        compiler_params=_cparams(2), name=name,
    )(proj3, src, dmerged, into)


ANY = pl.BlockSpec(memory_space=pl.ANY)
MESH = pl.DeviceIdType.MESH


def _all_gather(shards, *, name, after=None):
    nt = len(shards)
    n_after = 0 if after is None else 1

    def body(*refs):
        x_refs, out_refs = refs[:nt], refs[nt + n_after:2 * nt + n_after]
        send_sems, recv_sems, local_sems = refs[2 * nt + n_after:]
        x, y, c = lax.axis_index("x"), lax.axis_index("y"), lax.axis_index("c")
        me, sibling = (x, y, c), (x, y, 1 - c)
        chips = [(1 - x, y), (x, 1 - y), (1 - x, 1 - y)]

        def slot(t, px, py, pc):
            return out_refs[t].at[4 * px + 2 * py + pc]

        def copy(t, k, block, to, from_input=False):
            return pltpu.make_async_remote_copy(
                src_ref=x_refs[t] if from_input else slot(t, *block), dst_ref=slot(t, *block),
                send_sem=send_sems.at[7 * t + k], recv_sem=recv_sems.at[7 * t + k], device_id=to, device_id_type=MESH)

        mine = [pltpu.make_async_copy(x_refs[t], slot(t, *me), local_sems.at[t]) for t in range(nt)]
        for cp in mine:
            cp.start()
        first = [copy(t, 0, me, sibling, True) for t in range(nt)]
        first += [copy(t, 1 + j, me, (*chip, c), True) for j, chip in enumerate(chips) for t in range(nt)]
        for cp in first:
            cp.start()
        passed = []
        for j, chip in enumerate(chips):
            for t in range(nt):
                copy(t, 1 + j, (*chip, c), me).wait_recv()
                fwd = copy(t, 4 + j, (*chip, c), sibling)
                fwd.start()
                passed.append(fwd)
        for t in range(nt):
            copy(t, 0, sibling, me).wait_recv()
            for j, chip in enumerate(chips):
                copy(t, 4 + j, (*chip, 1 - c), me).wait_recv()
        for cp in first + passed:
            cp.wait_send()
        for cp in mine:
            cp.wait()

    return pl.pallas_call(
        body, out_shape=[jax.ShapeDtypeStruct((N_DEV,) + a.shape, a.dtype) for a in shards],
        in_specs=[ANY] * (nt + n_after), out_specs=[ANY] * nt,
        scratch_shapes=[pltpu.SemaphoreType.DMA((7 * nt,)), pltpu.SemaphoreType.DMA((7 * nt,)),
                        pltpu.SemaphoreType.DMA((nt,))],
        name=name,
    )(*shards, *([] if after is None else [after]))


HBM = pl.BlockSpec(memory_space=pltpu.HBM)
SEM = pl.BlockSpec(memory_space=pltpu.SEMAPHORE)
EFFECT = pltpu.SideEffectType.DATAFLOW_SIDE_EFFECTING


def _peers():
    x, y, c = lax.axis_index("x"), lax.axis_index("y"), lax.axis_index("c")
    peers = []
    for k in range(1, N_DEV):
        px, py, pc = x ^ ((k >> 2) & 1), y ^ ((k >> 1) & 1), c ^ (k & 1)
        peers.append(((px, py, pc), 4 * px + 2 * py + pc))
    return 4 * x + 2 * y + c, peers


def _split_copies(slots, src_refs, land_refs, send_sems, recv_sems):
    me, peers = _peers()
    copies = []
    for t, (whole, layer) in enumerate(slots):
        dst = land_refs[t].at[me] if layer is None else land_refs[t].at[me, layer]
        for k, (dev, lin) in enumerate(peers):
            copies.append(pltpu.make_async_remote_copy(
                src_ref=src_refs[t] if whole else src_refs[t].at[lin], dst_ref=dst,
                send_sem=send_sems.at[7 * t + k], recv_sem=recv_sems.at[7 * t + k], device_id=dev, device_id_type=MESH))
    return copies


def _split_start(srcs, lands, slots, carry, *, name):
    n = len(srcs)

    def body(*refs):
        copies = _split_copies(slots, refs[:n], refs[n:2 * n], refs[2 * n + 1], refs[2 * n + 2])
        for cp in copies:
            cp.start()

    def hbm(a):
        return pltpu.HBM(a.shape, a.dtype)

    outs = pl.pallas_call(
        body, name=name,
        out_shape=[pltpu.SemaphoreType.DMA((7 * n,)), pltpu.SemaphoreType.DMA((7 * n,))]
        + [hbm(a) for a in srcs] + [hbm(a) for a in lands] + [hbm(carry)],
        in_specs=[HBM] * (2 * n + 1), out_specs=[SEM, SEM] + [HBM] * (2 * n + 1),
        input_output_aliases={i: 2 + i for i in range(2 * n + 1)},
        compiler_params=pltpu.CompilerParams(has_side_effects=EFFECT),
    )(*[pltpu.with_memory_space_constraint(a, pltpu.HBM) for a in list(srcs) + list(lands) + [carry]])
    return outs[0], outs[1], outs[2:2 + n], outs[2 + n:2 + 2 * n], outs[2 + 2 * n]


def _split_wait(send_sems, recv_sems, srcs, lands, slots, after, *, name):
    n = len(srcs)

    def body(*refs):
        copies = _split_copies(slots, refs[:n], refs[n:2 * n], refs[2 * n], refs[2 * n + 1])
        for cp in copies:
            cp.wait_send()
        for cp in copies:
            cp.wait_recv()

    outs = pl.pallas_call(
        body, name=name,
        out_shape=[pltpu.HBM(a.shape, a.dtype) for a in list(srcs) + list(lands)],
        in_specs=[HBM] * (2 * n) + [SEM, SEM, ANY], out_specs=[HBM] * (2 * n),
        input_output_aliases={i: i for i in range(2 * n)},
        compiler_params=pltpu.CompilerParams(has_side_effects=EFFECT),
    )(*srcs, *lands, send_sems, recv_sems, after)
    return outs[n:]


def _adam_math(w, g, m, v):
    m1 = ADAM_B1 * m + (1.0 - ADAM_B1) * g
    v1 = ADAM_B2 * v + (1.0 - ADAM_B2) * (g * g)
    m_hat = m1 / (1.0 - ADAM_B1 ** ADAM_STEP)
    v_hat = v1 / (1.0 - ADAM_B2 ** ADAM_STEP)
    delta = -ADAM_LR * (m_hat / (jnp.sqrt(v_hat) + ADAM_EPS) + ADAM_WD * w)
    return delta, m1, v1


def _sum_adamw(parts, w, m, v, layer, prev, *, name):
    shape = w.shape
    r, c = shape[-2], shape[-1]
    a_l = math.prod(shape[1:-2])
    a = shape[0] * a_l
    base = layer * a_l
    if r % 256 == 0:
        tr, tc = 256, c
    else:
        tr, tc = r, _pick(c, (256, 128))
    w3, m3, v3 = (t.reshape(a, r, c) for t in (w, m, v))
    n_prev = 0 if prev is None else 4

    def body(*refs):
        p_ref, w_ref, m_ref, v_ref = refs[:4]
        g_ref, d_ref, m1_ref, v1_ref = refs[4 + n_prev:]
        g = p_ref[0].astype(F32)
        for src in range(1, N_DEV):
            g = g + p_ref[src].astype(F32)
        delta, m1, v1 = _adam_math(w_ref[...], g, m_ref[...], v_ref[...])
        g_ref[...] = g
        d_ref[...] = delta
        m1_ref[...] = m1
        v1_ref[...] = v1

    nr, ncol = r // tr, c // tc
    blk = pl.BlockSpec((None, tr, tc), lambda i, j: (base + i, j // ncol, j % ncol))
    prev3 = [] if prev is None else [t.reshape(a, r, c) for t in prev]
    outs = pl.pallas_call(
        body, grid=(a_l, nr * ncol),
        in_specs=[pl.BlockSpec((N_DEV, None, tr, tc), lambda i, j: (0, i, j // ncol, j % ncol)), blk, blk, blk]
        + [ANY] * n_prev,
        out_specs=[blk] * 4, out_shape=[jax.ShapeDtypeStruct((a, r, c), F32)] * 4,
        input_output_aliases={4 + k: k for k in range(n_prev)},
        compiler_params=_cparams(2), name=name,
    )(parts.reshape(N_DEV, a_l, r, c), w3, m3, v3, *prev3)
    return [o.reshape(shape) for o in outs]


def _sum_parts(parts, *, name):
    rows = parts.shape[1]

    def body(p_ref, o_ref):
        g = p_ref[0]
        for src in range(1, N_DEV):
            g = g + p_ref[src]
        o_ref[...] = g

    return pl.pallas_call(
        body, grid=(1,), in_specs=[pl.BlockSpec((N_DEV, rows, LANES), lambda i: (0, 0, 0))],
        out_specs=pl.BlockSpec((rows, LANES), lambda i: (0, 0)), out_shape=jax.ShapeDtypeStruct((rows, LANES), F32),
        compiler_params=_cparams(1), name=name,
    )(parts)


def _adamw(w, g, m, v, *, name):
    rows = w.shape[0]

    def body(w_ref, g_ref, m_ref, v_ref, d_ref, m1_ref, v1_ref):
        delta, m1, v1 = _adam_math(w_ref[...], g_ref[...], m_ref[...], v_ref[...])
        d_ref[...] = delta
        m1_ref[...] = m1
        v1_ref[...] = v1

    blk = pl.BlockSpec((rows, LANES), lambda i: (0, 0))
    return pl.pallas_call(
        body, grid=(1,), in_specs=[blk] * 4, out_specs=[blk] * 3,
        out_shape=[jax.ShapeDtypeStruct((rows, LANES), F32)] * 3,
        compiler_params=_cparams(1), name=name,
    )(w, g, m, v)


def _pack(arrs, dtype, row_mult=16):
    flat = jnp.concatenate([a.reshape(-1).astype(dtype) for a in arrs])
    n = flat.shape[0]
    rows = -(-n // (LANES * row_mult)) * row_mult
    flat = jnp.pad(flat, (0, rows * LANES - n))
    return flat.reshape(rows, LANES)


def _unpack(packed, shapes):
    flat = packed.reshape(-1)
    out, off = [], 0
    for shp in shapes:
        n = math.prod(shp)
        out.append(flat[off:off + n].reshape(shp))
        off += n
    return out


class _Layout:
    def __init__(self, d):
        self.d = d
        w = d
        self.dn_heads = w // DN_HEAD_DIM
        self.ssm_heads = w // SSM_HEAD_DIM
        gn = SSM_GROUPS * SSM_STATE
        self.sizes = (3 * w, w, self.dn_heads, self.dn_heads, 3 * w, w, w + 2 * gn, self.ssm_heads, 3 * d)
        offs, o = [], 0
        for sz in self.sizes:
            offs.append(o)
            o += sz
        self.offs = offs
        self.in_dim = o
        self.big = (0, 1, 4, 5, 6, 8)
        self.small = (2, 3, 7)
        cols, o = {}, 0
        for idx in self.big:
            cols[idx] = o
            o += self.sizes[idx]
        self.small_col = o
        self.cols = cols
        self.padded = o + LANES
        self.n_small = sum(self.sizes[i] for i in self.small)

    def from_shards(self, parts):
        cs = self.in_dim // N_DEV
        pieces = []
        for i in self.big + self.small:
            a, b = self.offs[i], self.offs[i] + self.sizes[i]
            while a < b:
                j = a // cs
                hi = min(b, (j + 1) * cs)
                pieces.append(parts[j][:, a - j * cs:hi - j * cs])
                a = hi
        pieces.append(jnp.zeros((parts.shape[1], LANES - self.n_small), parts.dtype))
        return jnp.concatenate(pieces, axis=1)

    def to_shards(self, wp):
        cs = self.in_dim // N_DEV
        pcol = dict(self.cols)
        o = self.small_col
        for i in self.small:
            pcol[i] = o
            o += self.sizes[i]
        shards = []
        for j in range(N_DEV):
            a, b = j * cs, (j + 1) * cs
            pieces = []
            for i in range(len(self.sizes)):
                lo, hi = max(a, self.offs[i]), min(b, self.offs[i] + self.sizes[i])
                if lo < hi:
                    pieces.append(wp[:, pcol[i] + lo - self.offs[i]:pcol[i] + hi - self.offs[i]])
            shards.append(jnp.concatenate(pieces, axis=1))
        return jnp.stack(shards)

def _rows_form(cols_t, nh, nc):
    return cols_t.T.reshape(nh, nc, 1, CHUNK)


def _layer_fwd(x, p, lay, tag, late=None):
    s, d = x.shape
    nc = s // CHUNK
    w = d
    dnh, smh = lay.dn_heads, lay.ssm_heads
    r = smh // SSM_GROUPS
    cb = {k: v // LANES for k, v in lay.cols.items()}
    sv = {}
    h1 = _rms_fwd(x, p["norm_mix"], name=f"rms_mix_{tag}")
    proj = _matmul(h1, p["w_in"], name=f"mm_in_{tag}")
    small = proj[:, lay.small_col:lay.small_col + LANES]
    a_rows = _rows_form(small[:, 0:dnh], dnh, nc)
    b_rows = _rows_form(small[:, dnh:2 * dnh], dnh, nc)
    dt_rows = small[:, 2 * dnh:2 * dnh + smh].T.reshape(SSM_GROUPS, r, nc, CHUNK).transpose(0, 2, 1, 3)
    zero_b = jnp.zeros((1, 3 * w), F32)
    dn_qkv = _conv_fwd(proj, cb[0], p["dn_conv_w"], zero_b, 2 * dnh, name=f"dn_conv_{tag}")
    dn_alog = p["dn_a_log"].reshape(dnh, 1, 1)
    dn_dtb = p["dn_dt_bias"].reshape(dnh, 1, 1)
    o_dn, dn_states, dn_inv = _dn_fwd(dn_qkv, a_rows, b_rows, dn_alog, dn_dtb, name=f"dn_chunk_{tag}")
    y_dn = _dn_post_fwd(o_dn, proj, cb[1], p["dn_norm_w"], name=f"dn_post_{tag}")
    o_sb, sb_r = _sb_fwd(proj, cb[4], w, name=f"sb_{tag}")
    xbc = _conv_fwd(proj, cb[6], p["ssm_conv_w"], p["ssm_conv_b"].reshape(1, -1), 0, name=f"ssm_conv_{tag}")
    ssm_alog = p["ssm_a_log"].reshape(SSM_GROUPS, r, 1)
    ssm_dtb = p["ssm_dt_bias"].reshape(SSM_GROUPS, r, 1)
    y_ssd, ssm_states = _ssd_fwd(xbc, dt_rows, ssm_alog, ssm_dtb, name=f"ssd_{tag}")
    dexp = jnp.repeat(p["ssm_d"], SSM_HEAD_DIM)
    y_ssm = _ssm_post_fwd(y_ssd, xbc, proj, cb[5], dexp, p["ssm_norm_w"], name=f"ssm_post_{tag}")
    if late is not None:
        p.update(late(y_ssm))
    branches = (y_dn, o_sb, y_ssm)
    proj3 = lax.empty((s, 3 * d), F32)
    for i, br in enumerate(branches):
        proj3 = _matmul(br, p["w_branch"][i], into=(proj3, i * d), name=f"mm_branch{i}_{tag}")
    merged = _merge_fwd(proj3, proj, cb[8], d, name=f"merge_{tag}")
    x1 = _matmul(merged, p["w_out"], name=f"mm_out_{tag}", epilogue=lambda acc, res: (acc + res,), extras=(x,))
    h2 = _rms_fwd(x1, p["norm_mlp"], name=f"rms_mlp_{tag}")
    u, act = _matmul(h2, p["w_up"], name=f"mm_up_{tag}", out_dtypes=(F32, MXU_DTYPE),
                     epilogue=lambda acc: (acc, jnp.square(jnp.maximum(acc, 0.0))))
    x2 = _matmul(act, p["w_down"], name=f"mm_down_{tag}", epilogue=lambda acc, res: (acc + res,), extras=(x1,))
    sv.update(x=x, h1=h1, proj=proj, a_rows=a_rows, b_rows=b_rows, dt_rows=dt_rows, dn_qkv=dn_qkv, dn_alog=dn_alog,
              dn_dtb=dn_dtb, o_dn=o_dn, dn_states=dn_states, dn_inv=dn_inv, y_dn=y_dn, o_sb=o_sb, sb_r=sb_r, xbc=xbc, ssm_alog=ssm_alog,
              ssm_dtb=ssm_dtb, y_ssd=y_ssd, ssm_states=ssm_states, dexp=dexp, y_ssm=y_ssm, proj3=proj3, merged=merged,
              x1=x1, h2=h2, u=u, act=act)
    return x2, sv


def _layer_bwd(dx2, p, sv, lay, tag, early=None, late=None):
    x = sv["x"]
    s, d = x.shape
    nc = s // CHUNK
    w = d
    dnh, smh = lay.dn_heads, lay.ssm_heads
    r = smh // SSM_GROUPS
    gn = SSM_GROUPS * SSM_STATE
    cb = {k: v // LANES for k, v in lay.cols.items()}
    proj = sv["proj"]
    g = {}
    dx2_b = dx2.astype(MXU_DTYPE)
    du = _matmul(dx2_b, p["w_down"], tb=True, name=f"mm_down_dx_{tag}", out_dtypes=(MXU_DTYPE,),
                 epilogue=lambda acc, uu: (acc * (2.0 * jnp.maximum(uu, 0.0)),), extras=(sv["u"],))
    g["w_down"] = _matmul(sv["act"], dx2_b, ta=True, name=f"mm_down_dw_{tag}", out_dtypes=(BF16,)).reshape(N_DEV, -1, d)
    g["w_up"] = _matmul(sv["h2"], du, ta=True, name=f"mm_up_dw_{tag}", out_dtypes=(BF16,), col_shards=N_DEV)
    dh2 = _matmul(du, p["w_up"], tb=True, name=f"mm_up_dx_{tag}")
    dx1, g["norm_mlp"] = _rms_bwd(sv["x1"], p["norm_mlp"], dh2, dx2, name=f"rms_mlp_bwd_{tag}")
    dx1_b = dx1.astype(MXU_DTYPE)
    dmerged = _matmul(dx1_b, p["w_out"], tb=True, name=f"mm_out_dx_{tag}")
    g["w_out"] = _matmul(sv["merged"], dx1_b, ta=True, name=f"mm_out_dw_{tag}", out_dtypes=(BF16,)).reshape(N_DEV, -1, d)
    dproj = lax.empty((s, lay.padded), MXU_DTYPE)
    dproj3, dproj = _merge_bwd(sv["proj3"], proj, cb[8], d, dmerged, dproj, name=f"merge_bwd_{tag}")
    branches = (sv["y_dn"], sv["o_sb"], sv["y_ssm"])
    dwb, dbr = [], []
    for i, br in enumerate(branches):
        cols = (i * d, d)
        dwb.append(_matmul(br, dproj3, ta=True, b_cols=cols, name=f"mm_branch{i}_dw_{tag}",
                           out_dtypes=(BF16,)).reshape(N_DEV, -1, d))
        dbr.append(_matmul(dproj3, p["w_branch"][i], tb=True, a_cols=cols, name=f"mm_branch{i}_dx_{tag}"))
    g["w_branch"] = jnp.stack(dwb, axis=1)
    dy_dn, do_sb, dy_ssm = dbr
    if early is not None:
        dy_ssm = early(g, dy_ssm)
    dy_ssd, dxs_skip, dproj, ddexp, g["ssm_norm_w"] = _ssm_post_bwd(
        sv["y_ssd"], sv["xbc"], proj, cb[5], sv["dexp"], p["ssm_norm_w"], dy_ssm, dproj, name=f"ssm_post_bwd_{tag}")
    g["ssm_d"] = ddexp.reshape(smh, SSM_HEAD_DIM).sum(axis=1)
    dxs, dbm, dcm, ddt_rows, dalog, ddtb = _ssd_bwd(
        sv["xbc"], sv["dt_rows"], sv["ssm_alog"], sv["ssm_dtb"], sv["ssm_states"], dy_ssd, dxs_skip, name=f"ssd_bwd_{tag}")
    g["ssm_a_log"] = dalog.reshape(smh)
    g["ssm_dt_bias"] = ddtb.reshape(smh)
    dxbc_post = jnp.concatenate([dxs, dbm, dcm], axis=1)
    dproj, g["ssm_conv_w"], dcb = _conv_bwd(proj, cb[6], p["ssm_conv_w"], p["ssm_conv_b"].reshape(1, -1), 0, dxbc_post,
                                            dproj, name=f"ssm_conv_bwd_{tag}")
    g["ssm_conv_b"] = dcb.reshape(-1)
    ddt = ddt_rows.transpose(0, 2, 1, 3).reshape(smh, s).T
    dqkv_sb = _sb_bwd(proj, cb[4], w, sv["sb_r"], do_sb, name=f"sb_bwd_{tag}")
    dproj = lax.dynamic_update_slice(dproj, jnp.concatenate([t.astype(MXU_DTYPE) for t in dqkv_sb], axis=1), (0, lay.cols[4]))
    do_dn, dproj, g["dn_norm_w"] = _dn_post_bwd(sv["o_dn"], proj, cb[1], p["dn_norm_w"], dy_dn, dproj,
                                                name=f"dn_post_bwd_{tag}")
    dqkv_dn, da_rows, db_rows, dal, ddtb_dn = _dn_bwd(
        sv["dn_qkv"], sv["a_rows"], sv["b_rows"], sv["dn_alog"], sv["dn_dtb"], sv["dn_states"], sv["dn_inv"], do_dn,
        name=f"dn_chunk_bwd_{tag}")
    g["dn_a_log"] = dal.reshape(dnh)
    g["dn_dt_bias"] = ddtb_dn.reshape(dnh)
    zero_b = jnp.zeros((1, 3 * w), F32)
    dproj, g["dn_conv_w"], _ = _conv_bwd(proj, cb[0], p["dn_conv_w"], zero_b, 2 * dnh, dqkv_dn, dproj,
                                         name=f"dn_conv_bwd_{tag}")
    da = da_rows.reshape(dnh, s).T
    db = db_rows.reshape(dnh, s).T
    dsmall = jnp.concatenate([da, db, ddt, jnp.zeros((s, LANES - lay.n_small), F32)], axis=1).astype(MXU_DTYPE)
    dproj = lax.dynamic_update_slice(dproj, dsmall, (0, lay.small_col))
    g["w_in"] = lay.to_shards(_matmul(sv["h1"], dproj, ta=True, name=f"mm_in_dw_{tag}", out_dtypes=(BF16,)))
    if late is not None:
        dproj = late(g, dproj)
    dh1 = _matmul(dproj, p["w_in"], tb=True, name=f"mm_in_dx_{tag}")
    dx0, g["norm_mix"] = _rms_bwd(x, p["norm_mix"], dh1, dx1, name=f"rms_mix_bwd_{tag}")
    return dx0, g


BIG = ("w_in", "w_branch", "w_out", "w_up", "w_down")
CONV = ("dn_conv_w", "ssm_conv_w")
SMALL = ("norm_mix", "dn_conv_w", "dn_a_log", "dn_dt_bias", "dn_norm_w", "ssm_conv_w", "ssm_conv_b", "ssm_a_log",
         "ssm_dt_bias", "ssm_d", "ssm_norm_w", "norm_mlp", "norm_final")
WEIGHTS = ("norm_mix", "w_in", "dn_conv_w", "dn_a_log", "dn_dt_bias", "dn_norm_w", "ssm_conv_w", "ssm_conv_b", "ssm_a_log",
           "ssm_dt_bias", "ssm_d", "ssm_norm_w", "w_branch", "w_out", "norm_mlp", "w_up", "w_down", "norm_final")
SHARD_AXIS = {"w_in": 2, "dn_conv_w": 2, "ssm_conv_w": 2, "w_branch": 2, "w_out": 1, "w_up": 2, "w_down": 1}


def _to_shards(full, axis):
    shp = full.shape
    n = shp[axis] // N_DEV
    t = full.reshape(shp[:axis] + (N_DEV, n) + shp[axis + 1:])
    return jnp.moveaxis(t, axis, 0)


def _unshard(parts, axis, *, name):
    shard = parts.shape[1:]
    nd = len(shard)
    if axis == 0:
        return parts.reshape((N_DEV * shard[0],) + shard[1:])

    def copy_block(i_ref, o_ref):
        o_ref[...] = i_ref[...]

    if axis == nd - 1:
        rows, n = math.prod(shard[:-1]), shard[-1]
        out = pl.pallas_call(
            copy_block, grid=(N_DEV,),
            in_specs=[pl.BlockSpec((None, rows, n), lambda j: (j, 0, 0))],
            out_specs=pl.BlockSpec((rows, n), lambda j: (0, j)),
            out_shape=jax.ShapeDtypeStruct((rows, N_DEV * n), parts.dtype),
            compiler_params=_cparams(1), name=name,
        )(parts.reshape(N_DEV, rows, n))
        return out.reshape(shard[:-1] + (N_DEV * n,))
    assert axis == nd - 2, (parts.shape, axis)
    a, n, c = math.prod(shard[:-2]), shard[-2], shard[-1]
    out = pl.pallas_call(
        copy_block, grid=(N_DEV, a),
        in_specs=[pl.BlockSpec((None, None, n, c), lambda j, i: (j, i, 0, 0))],
        out_specs=pl.BlockSpec((None, n, c), lambda j, i: (i, j, 0)),
        out_shape=jax.ShapeDtypeStruct((a, N_DEV * n, c), parts.dtype),
        compiler_params=_cparams(2), name=name,
    )(parts.reshape(N_DEV, a, n, c))
    return out.reshape(shard[:-2] + (N_DEV * n, c))


def _step(w, m, v, x, target):
    s, d = x.shape
    lay = _Layout(d)
    me = 4 * lax.axis_index("x") + 2 * lax.axis_index("y") + lax.axis_index("c")

    def shard(n, l):
        return w[n][l].astype(BF16) if n in BIG else w[n][l]

    def empty_land(a):
        return lax.empty((N_DEV,) + a.shape, a.dtype)

    def with_own(land, own):
        return lax.dynamic_update_index_in_dim(land, own, me, 0)

    def assemble(n, parts, l):
        return lay.from_shards(parts) if n == "w_in" else _unshard(parts, SHARD_AXIS[n] - 1, name=f"unshard_{n}_l{l}")

    small_names = tuple(n for n in WEIGHTS if n not in BIG + CONV + ("norm_final",))

    first, rest = ("w_in",) + CONV, BIG[1:]
    got = _all_gather([shard(n, 0) for n in first], name="gather_l0_first")
    whole, sliced = (True, None), (False, None)
    names_a, names_b = rest, BIG + CONV
    srcs_a, srcs_b = [shard(n, 0) for n in names_a], [shard(n, 1) for n in names_b]
    sem_sa, sem_ra, srcs_a, lands_a, w_in0 = _split_start(
        srcs_a, [empty_land(a) for a in srcs_a], [whole] * len(srcs_a), got[0], name="gather_l0_rest_start")
    sem_sb, sem_rb, srcs_b, lands_b, w_in0 = _split_start(
        srcs_b, [empty_land(a) for a in srcs_b], [whole] * len(srcs_b), w_in0, name="gather_l1_start")
    p0 = {n: w[n][0] for n in small_names}
    p0.update({n: assemble(n, g, 0) for n, g in zip(first, [w_in0] + list(got[1:]))})

    def late_l0(after):
        lands = _split_wait(sem_sa, sem_ra, srcs_a, lands_a, [whole] * len(srcs_a), after, name="gather_l0_rest_wait")
        return {n: assemble(n, with_own(ld, s_), 0) for n, ld, s_ in zip(names_a, lands, srcs_a)}

    h, sv0 = _layer_fwd(x, p0, lay, "l0", late=late_l0)
    lands = _split_wait(sem_sb, sem_rb, srcs_b, lands_b, [whole] * len(srcs_b), h, name="gather_l1_wait")
    p1 = {n: w[n][1] for n in small_names}
    p1.update({n: assemble(n, with_own(ld, s_), 1) for n, ld, s_ in zip(names_b, lands, srcs_b)})
    h, sv1 = _layer_fwd(h, p1, lay, "l1")
    loss, dh, g_norm_final = _final_loss(h, w["norm_final"], target, name="final_loss")
    grads = [None] * DEPTH
    dh, grads[1] = _layer_bwd(dh, p1, sv1, lay, "l1")

    def exchange_start(names, g, carry, tag):
        srcs = [g[n] for n in names]
        return _split_start(srcs, [lax.empty(a.shape, a.dtype) for a in srcs], [sliced] * len(srcs), carry,
                            name=f"grad_{tag}_start")

    def exchange_wait(names, started, after, tag):
        sem_s, sem_r, srcs, lands_, _ = started
        lands_ = _split_wait(sem_s, sem_r, srcs, lands_, [sliced] * len(srcs), after, name=f"grad_{tag}_wait")
        return {n: with_own(ld, lax.dynamic_index_in_dim(s_, me, 0, keepdims=False)) for n, ld, s_ in zip(names, lands_, srcs)}

    x1_started = exchange_start(BIG, grads[1], dh, "l1")
    pending = {}

    def early_l0(g, carry):
        pending["rest"] = exchange_start(rest, g, carry, "l0_rest")
        return pending["rest"][4]

    def late_bwd_l0(g, carry):
        pending["w_in"] = exchange_start(("w_in",), g, carry, "l0_w_in")
        return pending["w_in"][4]

    grad_x, grads[0] = _layer_bwd(x1_started[4], p0, sv0, lay, "l0", early=early_l0, late=late_bwd_l0)

    out = {"grad": {}, "delta": {}, "new_m": {}, "new_v": {}}
    parts1 = exchange_wait(BIG, x1_started, grad_x, "l1")
    res1 = {n: _sum_adamw(parts1[n], w[n], m[n], v[n], 1, None, name=f"sum_adamw_{n}_l1") for n in BIG}
    parts0 = exchange_wait(rest, pending["rest"], res1["w_in"][0], "l0_rest")
    res0 = {n: _sum_adamw(parts0[n], w[n], m[n], v[n], 0, res1[n], name=f"sum_adamw_{n}_l0") for n in rest}
    parts0 = exchange_wait(("w_in",), pending["w_in"], res0["w_down"][0], "l0_w_in")
    res0["w_in"] = _sum_adamw(parts0["w_in"], w["w_in"], m["w_in"], v["w_in"], 0, res1["w_in"], name="sum_adamw_w_in_l0")
    for n in BIG:
        for key, a in zip(("grad", "delta", "new_m", "new_v"), res0[n]):
            out[key][n] = a

    gfull = {n: jnp.stack([grads[l][n] for l in range(DEPTH)]) for n in SMALL if n != "norm_final"}
    gfull["norm_final"] = g_norm_final
    small_send = _pack([gfull[n] for n in SMALL] + [loss.reshape(1)], F32)
    small_recv = _all_gather([small_send], name="gather_small_grads", after=res0["w_in"][0])[0]
    small_sum = _sum_parts(small_recv, name="sum_small")
    small_full = _unpack(small_sum, [gfull[n].shape for n in SMALL] + [(1,)])
    loss_total = small_full[-1][0]
    gsmall = {}
    for n, a in zip(SMALL, small_full[:-1]):
        if n in SHARD_AXIS:
            a = lax.dynamic_index_in_dim(_to_shards(a, SHARD_AXIS[n]), me, axis=0, keepdims=False)
        gsmall[n] = a
    small_shapes = [w[n].shape for n in SMALL]
    ws, gs, ms, vs = (_pack([t[n] for n in SMALL], F32) for t in (w, gsmall, m, v))
    ds, m1s, v1s = _adamw(ws, gs, ms, vs, name="adamw_small")
    for n in SMALL:
        out["grad"][n] = gsmall[n]
    for key, packed in (("delta", ds), ("new_m", m1s), ("new_v", v1s)):
        for n, a in zip(SMALL, _unpack(packed, small_shapes)):
            out[key][n] = a
    return loss_total, grad_x, out


def kernel(x, norm_mix, w_in, dn_conv_w, dn_a_log, dn_dt_bias, dn_norm_w, ssm_conv_w, ssm_conv_b, ssm_a_log, ssm_dt_bias, ssm_d, ssm_norm_w, w_branch, w_out, norm_mlp, w_up, w_down, norm_final, loss_target, m_norm_mix, m_w_in, m_dn_conv_w, m_dn_a_log, m_dn_dt_bias, m_dn_norm_w, m_ssm_conv_w, m_ssm_conv_b, m_ssm_a_log, m_ssm_dt_bias, m_ssm_d, m_ssm_norm_w, m_w_branch, m_w_out, m_norm_mlp, m_w_up, m_w_down, m_norm_final, v_norm_mix, v_w_in, v_dn_conv_w, v_dn_a_log, v_dn_dt_bias, v_dn_norm_w, v_ssm_conv_w, v_ssm_conv_b, v_ssm_a_log, v_ssm_dt_bias, v_ssm_d, v_ssm_norm_w, v_w_branch, v_w_out, v_norm_mlp, v_w_up, v_w_down, v_norm_final):
    w = dict(norm_mix=norm_mix, w_in=w_in, dn_conv_w=dn_conv_w, dn_a_log=dn_a_log, dn_dt_bias=dn_dt_bias, dn_norm_w=dn_norm_w,
             ssm_conv_w=ssm_conv_w, ssm_conv_b=ssm_conv_b, ssm_a_log=ssm_a_log, ssm_dt_bias=ssm_dt_bias, ssm_d=ssm_d,
             ssm_norm_w=ssm_norm_w, w_branch=w_branch, w_out=w_out, norm_mlp=norm_mlp, w_up=w_up, w_down=w_down,
             norm_final=norm_final)
    m = dict(norm_mix=m_norm_mix, w_in=m_w_in, dn_conv_w=m_dn_conv_w, dn_a_log=m_dn_a_log, dn_dt_bias=m_dn_dt_bias,
             dn_norm_w=m_dn_norm_w, ssm_conv_w=m_ssm_conv_w, ssm_conv_b=m_ssm_conv_b, ssm_a_log=m_ssm_a_log,
             ssm_dt_bias=m_ssm_dt_bias, ssm_d=m_ssm_d, ssm_norm_w=m_ssm_norm_w, w_branch=m_w_branch, w_out=m_w_out,
             norm_mlp=m_norm_mlp, w_up=m_w_up, w_down=m_w_down, norm_final=m_norm_final)
    v = dict(norm_mix=v_norm_mix, w_in=v_w_in, dn_conv_w=v_dn_conv_w, dn_a_log=v_dn_a_log, dn_dt_bias=v_dn_dt_bias,
             dn_norm_w=v_dn_norm_w, ssm_conv_w=v_ssm_conv_w, ssm_conv_b=v_ssm_conv_b, ssm_a_log=v_ssm_a_log,
             ssm_dt_bias=v_ssm_dt_bias, ssm_d=v_ssm_d, ssm_norm_w=v_ssm_norm_w, w_branch=v_w_branch, w_out=v_w_out,
             norm_mlp=v_norm_mlp, w_up=v_w_up, w_down=v_w_down, norm_final=v_norm_final)
    loss, grad_x, out = _step(w, m, v, x[0], loss_target[0])
    return (loss, grad_x[None], *[out["grad"][n] for n in WEIGHTS], *[out["delta"][n] for n in WEIGHTS],
            *[out["new_m"][n] for n in WEIGHTS], *[out["new_v"][n] for n in WEIGHTS])
```

```python
import math

import jax
import jax.numpy as jnp
from jax import lax
from jax.experimental import pallas as pl
from jax.experimental.pallas import tpu as pltpu

F32 = jnp.float32
BF16 = jnp.bfloat16
MXU_DTYPE = BF16
HIGHEST = lax.Precision.HIGHEST

N_DEV = 8
DEPTH = 2
EPS = 1e-6
CONV_K = 4
DN_HEAD_DIM = 128
SB_HEAD_DIM = 64
SSM_HEAD_DIM = 64
SSM_STATE = 128
SSM_GROUPS = 4
CHUNK = 64
SB_BLOCK = 128
LANES = 128
ADAM_LR, ADAM_B1, ADAM_B2, ADAM_EPS, ADAM_WD, ADAM_STEP = 0.001, 0.9, 0.999, 1e-08, 0.01, 10
NEG_BIG = -1e30
DN_HEADS_PER_STEP = 8
SSD_GROUPS_PER_STEP = 1
SB_UNROLL = 4
SB_SPLIT = 2

ARB = "arbitrary"


def _cparams(n_axes):
    return pltpu.CompilerParams(dimension_semantics=(ARB,) * n_axes)


def _softplus(x):
    return jnp.maximum(x, 0.0) + jnp.log1p(jnp.exp(-jnp.abs(x)))


def _sigmoid(x):
    return jax.nn.sigmoid(x)


def _silu(x):
    return x * _sigmoid(x)


def _silu_and_grad(x):
    s = _sigmoid(x)
    return x * s, s * (1.0 + x * (1.0 - s))


def _dot(a, b, dims, prec=None):
    return lax.dot_general(a, b, (dims, ((), ())), precision=prec, preferred_element_type=F32)


NN = ((1,), (0,))
NT = ((1,), (1,))
TN = ((0,), (0,))


def _mxu_dot(a, b, dims):
    return _dot(a.astype(MXU_DTYPE), b.astype(MXU_DTYPE), dims)


def _single_pass_dot(dims):
    grads = {NN: (lambda a, b, ct: (_mxu_dot(ct, b, NT), _mxu_dot(a, ct, TN))),
             NT: (lambda a, b, ct: (_mxu_dot(ct, b, NN), _mxu_dot(ct, a, TN))),
             TN: (lambda a, b, ct: (_mxu_dot(b, ct, NT), _mxu_dot(a, ct, NN)))}[dims]

    @jax.custom_vjp
    def f(a, b):
        return _mxu_dot(a, b, dims)

    f.defvjp(lambda a, b: (_mxu_dot(a, b, dims), (a, b)), lambda res, ct: grads(*res, ct))
    return f


_SDOT = {dims: _single_pass_dot(dims) for dims in (NN, NT, TN)}


def _sdot(a, b, dims=NN):
    return _SDOT[dims](a, b)


def _split_dot(a, m_bf16, nsplit=3):
    out = None
    rem = a
    for _ in range(nsplit):
        piece = rem.astype(BF16)
        rem = rem - piece.astype(F32)
        term = _dot(piece, m_bf16, NN)
        out = term if out is None else out + term
    return out


def _pick(n, pref):
    for t in pref:
        if n % t == 0:
            return t
    return n


def _matmul(a, b, *, ta=False, tb=False, name, epilogue=None, extras=(), out_dtypes=(F32,), col_shards=1, into=None,
            a_cols=None, b_cols=None, tm=None, tn=None, tk=None):
    a_shape = a.shape if a_cols is None else (a.shape[0], a_cols[1])
    b_shape = b.shape if b_cols is None else (b.shape[0], b_cols[1])
    assert (a_cols is None or not ta) and (b_cols is None or not tb)
    m, k = (a_shape[1], a_shape[0]) if ta else a_shape
    k2, n = (b_shape[1], b_shape[0]) if tb else b_shape
    assert k == k2, (a.shape, b.shape, ta, tb)
    ncs = n // col_shards
    tm = tm or _pick(m, (1920, 1024, 512, 256, 128))
    tn = tn or _pick(ncs, (1920, 1024, 640, 512, 384, 256, 128))
    tk = tk or _pick(k, (1920, 1024, 640, 512, 256, 128))
    nk = k // tk
    a_off = 0 if a_cols is None else a_cols[0] // tk
    b_off = 0 if b_cols is None else b_cols[0] // tn
    assert (a_cols is None or a_cols[0] % tk == 0) and (b_cols is None or b_cols[0] % tn == 0)
    a_spec = (pl.BlockSpec((tk, tm), lambda i, j, kk: (kk, i)) if ta
              else pl.BlockSpec((tm, tk), lambda i, j, kk: (i, kk + a_off)))
    b_spec = (pl.BlockSpec((tn, tk), lambda i, j, kk: (j, kk)) if tb
              else pl.BlockSpec((tk, tn), lambda i, j, kk: (kk, j + b_off)))
    e_spec = pl.BlockSpec((tm, tn), lambda i, j, kk: (i, j))
    if into is not None:
        buf, col_off = into
        off = col_off // tn
        assert col_shards == 1 and len(out_dtypes) == 1 and col_off % tn == 0 and out_dtypes[0] == buf.dtype
        o_spec, o_shape = pl.BlockSpec((tm, tn), lambda i, j, kk: (i, off + j)), buf.shape
    elif col_shards == 1:
        o_spec, o_shape = e_spec, (m, n)
    else:
        per = ncs // tn
        o_spec, o_shape = pl.BlockSpec((None, tm, tn), lambda i, j, kk: (j // per, i, j % per)), (col_shards, m, ncs)
    dims = (((0,) if ta else (1,)), ((1,) if tb else (0,)))
    n_extra = len(extras)
    n_out = len(out_dtypes)
    n_into = 0 if into is None else 1

    def body(*refs):
        a_ref, b_ref = refs[0], refs[1]
        extra_refs = refs[2:2 + n_extra]
        out_refs = refs[2 + n_extra + n_into:2 + n_extra + n_into + n_out]
        acc_ref = refs[-1]
        kk = pl.program_id(2)

        @pl.when(kk == 0)
        def _():
            acc_ref[...] = jnp.zeros_like(acc_ref)

        acc_ref[...] += _dot(a_ref[...].astype(MXU_DTYPE), b_ref[...].astype(MXU_DTYPE), dims)

        @pl.when(kk == nk - 1)
        def _():
            acc = acc_ref[...]
            outs = (acc,) if epilogue is None else epilogue(acc, *[r[...] for r in extra_refs])
            for o_ref, o in zip(out_refs, outs):
                o_ref[...] = o.astype(o_ref.dtype)

    outs = pl.pallas_call(
        body,
        grid=(m // tm, n // tn, nk),
        in_specs=[a_spec, b_spec] + [e_spec] * n_extra + [ANY] * n_into,
        out_specs=[o_spec] * n_out,
        out_shape=[jax.ShapeDtypeStruct(o_shape, dt) for dt in out_dtypes],
        input_output_aliases={2 + n_extra: 0} if n_into else {},
        scratch_shapes=[pltpu.VMEM((tm, tn), F32)],
        compiler_params=pltpu.CompilerParams(dimension_semantics=("parallel", "parallel", ARB)),
        name=name,
    )(a, b, *extras, *([] if into is None else [into[0]]))
    return outs[0] if n_out == 1 else tuple(outs)


def _rms_fwd(x, w, *, name, tm=512):
    s, d = x.shape
    out_dtype = MXU_DTYPE

    def body(x_ref, w_ref, o_ref):
        xv = x_ref[...]
        r = lax.rsqrt(jnp.mean(xv * xv, axis=-1, keepdims=True) + EPS)
        o_ref[...] = (xv * r * w_ref[...]).astype(o_ref.dtype)

    return pl.pallas_call(
        body, grid=(s // tm,),
        in_specs=[pl.BlockSpec((tm, d), lambda i: (i, 0)), pl.BlockSpec((1, d), lambda i: (0, 0))],
        out_specs=pl.BlockSpec((tm, d), lambda i: (i, 0)),
        out_shape=jax.ShapeDtypeStruct((s, d), out_dtype),
        compiler_params=_cparams(1), name=name,
    )(x, w.reshape(1, d))


def _rms_bwd(x, w, dh, dres, *, name, tm=512):
    s, d = x.shape

    def body(x_ref, w_ref, dh_ref, dres_ref, dx_ref, dw_ref):
        xv = x_ref[...]
        r = lax.rsqrt(jnp.mean(xv * xv, axis=-1, keepdims=True) + EPS)
        xh = xv * r
        dhv = dh_ref[...].astype(F32)
        dxn = dhv * w_ref[...]
        dx = r * (dxn - xh * jnp.mean(dxn * xh, axis=-1, keepdims=True))
        dx_ref[...] = dres_ref[...] + dx

        @pl.when(pl.program_id(0) == 0)
        def _():
            dw_ref[...] = jnp.zeros_like(dw_ref)

        dw_ref[...] += jnp.sum(dhv * xh, axis=0, keepdims=True)

    dx, dw = pl.pallas_call(
        body, grid=(s // tm,),
        in_specs=[pl.BlockSpec((tm, d), lambda i: (i, 0)), pl.BlockSpec((1, d), lambda i: (0, 0)),
                  pl.BlockSpec((tm, d), lambda i: (i, 0)), pl.BlockSpec((tm, d), lambda i: (i, 0))],
        out_specs=[pl.BlockSpec((tm, d), lambda i: (i, 0)), pl.BlockSpec((1, d), lambda i: (0, 0))],
        out_shape=[jax.ShapeDtypeStruct((s, d), F32), jax.ShapeDtypeStruct((1, d), F32)],
        compiler_params=_cparams(1), name=name,
    )(x, w.reshape(1, d), dh, dres)
    return dx, dw.reshape(d)


def _final_loss(x, w, target, *, name, tm=512):
    s, d = x.shape

    def body(x_ref, w_ref, t_ref, loss_ref, dx_ref, dw_ref):
        xv = x_ref[...]
        r = lax.rsqrt(jnp.mean(xv * xv, axis=-1, keepdims=True) + EPS)
        xh = xv * r
        err = xh * w_ref[...] - t_ref[...]
        dy = err * (1.0 / d)
        dxn = dy * w_ref[...]
        dx_ref[...] = r * (dxn - xh * jnp.mean(dxn * xh, axis=-1, keepdims=True))

        @pl.when(pl.program_id(0) == 0)
        def _():
            dw_ref[...] = jnp.zeros_like(dw_ref)
            loss_ref[...] = jnp.zeros_like(loss_ref)

        dw_ref[...] += jnp.sum(dy * xh, axis=0, keepdims=True)
        row = jnp.sum(err * err, axis=1, keepdims=True) * (0.5 / d)
        loss_ref[...] += jnp.sum(row, axis=0, keepdims=True)

    loss, dx, dw = pl.pallas_call(
        body, grid=(s // tm,),
        in_specs=[pl.BlockSpec((tm, d), lambda i: (i, 0)), pl.BlockSpec((1, d), lambda i: (0, 0)),
                  pl.BlockSpec((tm, d), lambda i: (i, 0))],
        out_specs=[pl.BlockSpec((1, 1), lambda i: (0, 0)), pl.BlockSpec((tm, d), lambda i: (i, 0)),
                   pl.BlockSpec((1, d), lambda i: (0, 0))],
        out_shape=[jax.ShapeDtypeStruct((1, 1), F32), jax.ShapeDtypeStruct((s, d), F32), jax.ShapeDtypeStruct((1, d), F32)],
        compiler_params=_cparams(1), name=name,
    )(x, w.reshape(1, d), target)
    return loss[0, 0], dx, dw.reshape(d)


def _shift_down(x, sh, t_idx):
    return jnp.where(t_idx >= sh, pltpu.roll(x, sh, 0), 0.0)


def _shift_up(x, sh, t_idx, s):
    return jnp.where(t_idx < s - sh, pltpu.roll(x, s - sh, 0), 0.0)


def _conv_pre(x, w_rows, b, t_idx):
    c = w_rows[CONV_K - 1] * x + b
    for sh in range(1, CONV_K):
        c = c + w_rows[CONV_K - 1 - sh] * _shift_down(x, sh, t_idx)
    return c


def _conv_fwd(src, col0, w, b, n_l2, *, name):
    s = src.shape[0]
    c_tot = w.shape[1]
    nblk = c_tot // LANES

    def body(x_ref, w_ref, b_ref, o_ref):
        j = pl.program_id(0)
        t_idx = lax.broadcasted_iota(jnp.int32, (s, LANES), 0)
        w_rows = [w_ref[kk:kk + 1, :] for kk in range(CONV_K)]
        y = _silu(_conv_pre(x_ref[...], w_rows, b_ref[...], t_idx))
        if n_l2 > 0:
            yn = y * lax.rsqrt(jnp.sum(y * y, axis=1, keepdims=True) + EPS)
            y = jnp.where(j < n_l2, yn, y)
        o_ref[...] = y

    return pl.pallas_call(
        body, grid=(nblk,),
        in_specs=[pl.BlockSpec((s, LANES), lambda j: (0, col0 + j)), pl.BlockSpec((CONV_K, LANES), lambda j: (0, j)),
                  pl.BlockSpec((1, LANES), lambda j: (0, j))],
        out_specs=pl.BlockSpec((s, LANES), lambda j: (0, j)),
        out_shape=jax.ShapeDtypeStruct((s, c_tot), F32),
        compiler_params=_cparams(1), name=name,
    )(src, w, b)


def _conv_bwd(src, col0, w, b, n_l2, dout, into, *, name):
    s = src.shape[0]
    c_tot = w.shape[1]
    nblk = c_tot // LANES

    def body(x_ref, w_ref, b_ref, do_ref, into_ref, dx_ref, dw_ref, db_ref):
        j = pl.program_id(0)
        t_idx = lax.broadcasted_iota(jnp.int32, (s, LANES), 0)
        xv = x_ref[...]
        w_rows = [w_ref[kk:kk + 1, :] for kk in range(CONV_K)]
        c = _conv_pre(xv, w_rows, b_ref[...], t_idx)
        dy = do_ref[...]
        y, y_grad = _silu_and_grad(c)
        if n_l2 > 0:
            r = lax.rsqrt(jnp.sum(y * y, axis=1, keepdims=True) + EPS)
            dyn = r * dy - y * (r * r * r) * jnp.sum(dy * y, axis=1, keepdims=True)
            dy = jnp.where(j < n_l2, dyn, dy)
        dc = dy * y_grad
        dx = w_rows[CONV_K - 1] * dc
        rows = [None] * CONV_K
        rows[CONV_K - 1] = jnp.sum(dc * xv, axis=0, keepdims=True)
        for sh in range(1, CONV_K):
            dx = dx + w_rows[CONV_K - 1 - sh] * _shift_up(dc, sh, t_idx, s)
            rows[CONV_K - 1 - sh] = jnp.sum(dc * _shift_down(xv, sh, t_idx), axis=0, keepdims=True)
        dx_ref[...] = dx.astype(dx_ref.dtype)
        for kk in range(CONV_K):
            dw_ref[kk:kk + 1, :] = rows[kk]
        db_ref[...] = jnp.sum(dc, axis=0, keepdims=True)

    return pl.pallas_call(
        body, grid=(nblk,),
        in_specs=[pl.BlockSpec((s, LANES), lambda j: (0, col0 + j)), pl.BlockSpec((CONV_K, LANES), lambda j: (0, j)),
                  pl.BlockSpec((1, LANES), lambda j: (0, j)), pl.BlockSpec((s, LANES), lambda j: (0, j)), ANY],
        out_specs=[pl.BlockSpec((s, LANES), lambda j: (0, col0 + j)), pl.BlockSpec((CONV_K, LANES), lambda j: (0, j)),
                   pl.BlockSpec((1, LANES), lambda j: (0, j))],
        out_shape=[jax.ShapeDtypeStruct(into.shape, into.dtype), jax.ShapeDtypeStruct((CONV_K, c_tot), F32),
                   jax.ShapeDtypeStruct((1, c_tot), F32)],
        input_output_aliases={4: 0},
        compiler_params=_cparams(1), name=name,
    )(src, w, b, dout, into)


def _chunk_masks(c):
    ii = lax.broadcasted_iota(jnp.int32, (c, c), 0)
    jj = lax.broadcasted_iota(jnp.int32, (c, c), 1)
    return ii, jj


def _row_to_col(row, eye):
    return jnp.sum(jnp.where(eye, row, 0.0), axis=1, keepdims=True)


def _each(f, *lists):
    return [f(*xs) for xs in zip(*lists)]


@jax.custom_vjp
def _nilpotent_inverse(nmats):
    c = nmats[0].shape[0]
    ii, jj = _chunk_masks(c)
    xinv = _each(lambda n: jnp.where(ii == jj, 1.0, 0.0) + n, nmats)
    pw = nmats
    for _ in range(int(math.log2(c)) - 1):
        pw = _each(lambda p: _dot(p, p, NN, HIGHEST), pw)
        xinv = _each(lambda x, p: x + _dot(x, p, NN, HIGHEST), xinv, pw)
    return xinv


def _nilpotent_inverse_fwd(nmats):
    xinv = _nilpotent_inverse(nmats)
    return xinv, xinv


def _nilpotent_inverse_bwd(xinv, cts):
    left = _each(lambda x, ct: _dot(x, ct, TN, HIGHEST), xinv, cts)
    return (_each(lambda l_, x: _dot(l_, x, NT, HIGHEST), left, xinv),)


_nilpotent_inverse.defvjp(_nilpotent_inverse_fwd, _nilpotent_inverse_bwd)


@jax.custom_vjp
def _saved_inverse(nmats, saved):
    return saved


def _saved_inverse_fwd(nmats, saved):
    return saved, saved


def _saved_inverse_bwd(xinv, cts):
    return _nilpotent_inverse_bwd(xinv, cts) + (_each(jnp.zeros_like, xinv),)


_saved_inverse.defvjp(_saved_inverse_fwd, _saved_inverse_bwd)


def _dn_chunk(q, k, v, a_row, b_row, alog, dtb, s0, saved_inverse=None):
    c = q[0].shape[0]
    ii, jj = _chunk_masks(c)
    causal, strict, eye = ii >= jj, ii > jj, ii == jj
    g_row = _each(lambda al, a, dt: -jnp.exp(al) * _softplus(a + dt), alog, a_row, dtb)
    beta_col = _each(lambda b: _row_to_col(_sigmoid(b), eye), b_row)
    g_col = _each(lambda g: _row_to_col(g, eye), g_row)
    gc_col = _each(lambda g: jnp.sum(jnp.where(causal, g, 0.0), axis=1, keepdims=True), g_row)
    gc_row = _each(lambda g: jnp.sum(jnp.where(jj >= ii, g, 0.0), axis=0, keepdims=True), g_col)
    decay = _each(lambda gc, gr: jnp.exp(jnp.where(causal, gc - gr, NEG_BIG)), gc_col, gc_row)
    kb = _each(jnp.multiply, k, beta_col)
    vb = _each(jnp.multiply, v, beta_col)
    nmat = _each(lambda kb_, k_, dc: -jnp.where(strict, _dot(kb_, k_, NT, HIGHEST) * dc, 0.0), kb, k, decay)
    xinv = _nilpotent_inverse(nmat) if saved_inverse is None else _saved_inverse(nmat, saved_inverse)
    egc = _each(jnp.exp, gc_col)
    dv = v[0].shape[1]
    uw = _each(lambda x, vb_, kb_, e: _dot(x, jnp.concatenate([vb_, kb_ * e], axis=1), NN, HIGHEST), xinv, vb, kb, egc)
    u = _each(lambda t: t[:, :dv], uw)
    w = _each(lambda t: t[:, dv:], uw)
    qs = _each(lambda q_: q_ * (q_.shape[1] ** -0.5), q)
    attn = _each(lambda q_, k_, dc: _sdot(q_, k_, NT) * dc, qs, k, decay)
    gl = _each(lambda g: jnp.sum(g, axis=1, keepdims=True), g_row)
    kd = _each(lambda k_, gl_, gc: k_ * jnp.exp(gl_ - gc), k, gl, gc_col)
    v_new = _each(lambda u_, w_, s: u_ - _sdot(w_, s), u, w, s0)
    o = _each(lambda q_, e, s, at, vn: _sdot(q_ * e, s) + _sdot(at, vn), qs, egc, s0, attn, v_new)
    s1 = _each(lambda s, gl_, kd_, vn: s * jnp.exp(gl_) + _sdot(kd_, vn, TN), s0, gl, kd, v_new)
    return (o, s1), xinv


def _dn_specs(nh, nc, hb, rev):
    n_of = (lambda n: nc - 1 - n) if rev else (lambda n: n)
    ng = nh // hb
    qkv = [pl.BlockSpec((CHUNK, hb * DN_HEAD_DIM), (lambda h, n, o=o: (n_of(n), o * ng + h))) for o in range(3)]
    row = pl.BlockSpec((hb, None, 1, CHUNK), lambda h, n: (h, n_of(n), 0, 0))
    scal = pl.BlockSpec((hb, 1, 1), lambda h, n: (h, 0, 0))
    o_spec = pl.BlockSpec((CHUNK, hb * DN_HEAD_DIM), lambda h, n: (n_of(n), h))
    st = pl.BlockSpec((hb, None, DN_HEAD_DIM, DN_HEAD_DIM), lambda h, n: (h, n_of(n), 0, 0))
    inv = pl.BlockSpec((hb, None, CHUNK, CHUNK), lambda h, n: (h, n_of(n), 0, 0))
    return qkv, row, scal, o_spec, st, inv


def _dn_fwd(qkv, a_rows, b_rows, alog, dtb, *, name):
    s = qkv.shape[0]
    nh, nc = a_rows.shape[0], a_rows.shape[1]
    hb = min(DN_HEADS_PER_STEP, nh)
    qkv_specs, row, scal, o_spec, st, inv = _dn_specs(nh, nc, hb, False)
    hd = DN_HEAD_DIM

    def body(q_ref, k_ref, v_ref, a_ref, b_ref, al_ref, dt_ref, o_ref, st_ref, inv_ref, state):
        @pl.when(pl.program_id(1) == 0)
        def _():
            state[...] = jnp.zeros_like(state)

        cols = [slice(h * hd, (h + 1) * hd) for h in range(hb)]
        s0 = [state[h] for h in range(hb)]
        for h in range(hb):
            st_ref[h] = s0[h]
        (o, s1), xinv = _dn_chunk(
            [q_ref[:, cl] for cl in cols], [k_ref[:, cl] for cl in cols], [v_ref[:, cl] for cl in cols],
            [a_ref[h] for h in range(hb)], [b_ref[h] for h in range(hb)],
            [al_ref[h] for h in range(hb)], [dt_ref[h] for h in range(hb)], s0)
        for h in range(hb):
            o_ref[:, cols[h]] = o[h]
            inv_ref[h] = xinv[h]
            state[h] = s1[h]

    return pl.pallas_call(
        body, grid=(nh // hb, nc),
        in_specs=qkv_specs + [row, row, scal, scal],
        out_specs=[o_spec, st, inv],
        out_shape=[jax.ShapeDtypeStruct((s, nh * hd), F32), jax.ShapeDtypeStruct((nh, nc, hd, hd), F32),
                   jax.ShapeDtypeStruct((nh, nc, CHUNK, CHUNK), F32)],
        scratch_shapes=[pltpu.VMEM((hb, hd, hd), F32)],
        compiler_params=_cparams(2), name=name,
    )(qkv, qkv, qkv, a_rows, b_rows, alog, dtb)


def _dn_bwd(qkv, a_rows, b_rows, alog, dtb, states, inverses, do, *, name):
    s = qkv.shape[0]
    nh, nc = a_rows.shape[0], a_rows.shape[1]
    hb = min(DN_HEADS_PER_STEP, nh)
    qkv_specs, row, scal, o_spec, st, inv = _dn_specs(nh, nc, hb, True)
    hd = DN_HEAD_DIM

    assert hb == nh, "dq | dk | dv are written as one [S, 3W] array: all heads in one grid step"
    w = nh * hd

    def body(q_ref, k_ref, v_ref, a_ref, b_ref, al_ref, dt_ref, st_ref, inv_ref, do_ref,
             dqkv_ref, da_ref, db_ref, dal_ref, ddt_ref, dstate):
        @pl.when(pl.program_id(1) == 0)
        def _():
            dstate[...] = jnp.zeros_like(dstate)
            dal_ref[...] = jnp.zeros_like(dal_ref)
            ddt_ref[...] = jnp.zeros_like(ddt_ref)

        cols = [slice(h * hd, (h + 1) * hd) for h in range(hb)]
        heads = range(hb)
        args = ([q_ref[:, cl] for cl in cols], [k_ref[:, cl] for cl in cols], [v_ref[:, cl] for cl in cols],
                [a_ref[h] for h in heads], [b_ref[h] for h in heads], [al_ref[h] for h in heads],
                [dt_ref[h] for h in heads], [st_ref[h] for h in heads])
        saved = [inv_ref[h] for h in heads]
        _, vjp, _ = jax.vjp(lambda *a: _dn_chunk(*a, saved_inverse=saved), *args, has_aux=True)
        dq, dk, dv, da, db, dal, ddt, ds0 = vjp(([do_ref[:, cl] for cl in cols], [dstate[h] for h in heads]))
        for h in heads:
            dqkv_ref[:, h * hd:(h + 1) * hd] = dq[h]
            dqkv_ref[:, w + h * hd:w + (h + 1) * hd] = dk[h]
            dqkv_ref[:, 2 * w + h * hd:2 * w + (h + 1) * hd] = dv[h]
            da_ref[h] = da[h]
            db_ref[h] = db[h]
            dal_ref[h] += dal[h]
            ddt_ref[h] += ddt[h]
            dstate[h] = ds0[h]

    n_of = lambda n: nc - 1 - n
    outs = pl.pallas_call(
        body, grid=(nh // hb, nc),
        in_specs=qkv_specs + [row, row, scal, scal, st, inv, o_spec],
        out_specs=[pl.BlockSpec((CHUNK, 3 * w), lambda h, n: (n_of(n), 0)), row, row, scal, scal],
        out_shape=[jax.ShapeDtypeStruct((s, 3 * w), F32)]
        + [jax.ShapeDtypeStruct(a_rows.shape, F32)] * 2 + [jax.ShapeDtypeStruct((nh, 1, 1), F32)] * 2,
        scratch_shapes=[pltpu.VMEM((hb, hd, hd), F32)],
        compiler_params=_cparams(2), name=name,
    )(qkv, qkv, qkv, a_rows, b_rows, alog, dtb, states, inverses, do)
    return outs


def _dn_post_fwd(o, src, gate_col0, nw, *, name, tm=512):
    s, w = o.shape
    hd = DN_HEAD_DIM
    gc = gate_col0 * LANES // w

    def body(o_ref, g_ref, w_ref, y_ref):
        for h in range(w // hd):
            cols = slice(h * hd, (h + 1) * hd)
            ov = o_ref[:, cols]
            r = lax.rsqrt(jnp.mean(ov * ov, axis=-1, keepdims=True) + EPS)
            y_ref[:, cols] = (ov * r * w_ref[...] * _silu(g_ref[:, cols])).astype(y_ref.dtype)

    blk = pl.BlockSpec((tm, w), lambda i: (i, 0))
    return pl.pallas_call(
        body, grid=(s // tm,),
        in_specs=[blk, pl.BlockSpec((tm, w), lambda i: (i, gc)), pl.BlockSpec((1, hd), lambda i: (0, 0))],
        out_specs=blk, out_shape=jax.ShapeDtypeStruct((s, w), MXU_DTYPE),
        compiler_params=_cparams(1), name=name,
    )(o, src, nw.reshape(1, hd))


def _dn_post_bwd(o, src, gate_col0, nw, dy, into, *, name, tm=512):
    s, w = o.shape
    hd = DN_HEAD_DIM
    gc = gate_col0 * LANES // w

    def body(o_ref, g_ref, w_ref, dy_ref, into_ref, do_ref, dg_ref, dw_ref):
        @pl.when(pl.program_id(0) == 0)
        def _():
            dw_ref[...] = jnp.zeros_like(dw_ref)

        dw = jnp.zeros((1, hd), F32)
        for h in range(w // hd):
            cols = slice(h * hd, (h + 1) * hd)
            ov, gv, dyv = o_ref[:, cols], g_ref[:, cols], dy_ref[:, cols]
            r = lax.rsqrt(jnp.mean(ov * ov, axis=-1, keepdims=True) + EPS)
            oh = ov * r
            sg, sg_grad = _silu_and_grad(gv)
            dn = dyv * sg
            dg_ref[:, cols] = (dyv * (oh * w_ref[...]) * sg_grad).astype(dg_ref.dtype)
            don = dn * w_ref[...]
            do_ref[:, cols] = r * (don - oh * jnp.mean(don * oh, axis=-1, keepdims=True))
            dw = dw + jnp.sum(dn * oh, axis=0, keepdims=True)
        dw_ref[...] += dw

    blk = pl.BlockSpec((tm, w), lambda i: (i, 0))
    wspec = pl.BlockSpec((1, hd), lambda i: (0, 0))
    gate_blk = pl.BlockSpec((tm, w), lambda i: (i, gc))
    do, dg, dw = pl.pallas_call(
        body, grid=(s // tm,),
        in_specs=[blk, gate_blk, wspec, blk, ANY],
        out_specs=[blk, gate_blk, wspec],
        out_shape=[jax.ShapeDtypeStruct((s, w), F32), jax.ShapeDtypeStruct(into.shape, into.dtype),
                   jax.ShapeDtypeStruct((1, hd), F32)],
        input_output_aliases={4: 1},
        compiler_params=_cparams(1), name=name,
    )(o, src, nw.reshape(1, hd), dy, into)
    return do, dg, dw.reshape(hd)


def _sb_consts():
    r2 = lax.broadcasted_iota(jnp.int32, (2 * SB_BLOCK, SB_BLOCK), 0)
    c2 = lax.broadcasted_iota(jnp.int32, (2 * SB_BLOCK, SB_BLOCK), 1)
    r = lax.broadcasted_iota(jnp.int32, (SB_BLOCK, SB_BLOCK), 0)
    c = lax.broadcasted_iota(jnp.int32, (SB_BLOCK, SB_BLOCK), 1)
    lm0 = c < SB_HEAD_DIM
    m_gt = jnp.where(r > c, 1.0, 0.0).astype(BF16)
    m_lt = jnp.where(r < c, 1.0, 0.0).astype(BF16)
    return r2, c2, lm0, m_gt, m_lt


def _sb_stack(x, lm0):
    return jnp.concatenate([jnp.where(lm0, x, 0.0), jnp.where(lm0, 0.0, x)], axis=0)


def _sb_unstack(x2, lm0):
    return jnp.where(lm0, x2[:SB_BLOCK], x2[SB_BLOCK:])


def _sb_fwd(src, col0, width, *, name):
    s = src.shape[0]
    nq = s // SB_BLOCK
    npair = width // LANES
    scale = SB_HEAD_DIM ** -0.5
    nu = math.gcd(SB_UNROLL, nq)

    def body(q_ref, k_ref, v_ref, o_ref, w_hbm, stage, sems):
        p, i = pl.program_id(0), pl.program_id(1)
        r2, c2, lm0, m_gt, _ = _sb_consts()
        t_glob = i * SB_BLOCK + (r2 & (SB_BLOCK - 1))
        q2 = (_sb_stack(q_ref[...], lm0) * scale).astype(MXU_DTYPE)

        t = p * nq + i
        half = t % 2
        ngrp = nq // nu

        def save(half_, grp, pp, ii):
            return pltpu.make_async_copy(stage.at[half_, grp], w_hbm.at[pp, ii, grp], sems.at[half_, grp])

        def drain(half_, pp, ii):
            for grp in range(ngrp):
                @pl.when(grp <= ii // nu)
                def _():
                    save(half_, grp, pp, ii).wait()

        def group(base, carry, masked):
            o2, rsum = carry
            js = [base + nu - 1 - u for u in range(nu)]
            offs = [pl.multiple_of(j * SB_BLOCK, SB_BLOCK) for j in js]
            zs = [_dot(q2, k_ref[pl.ds(off, SB_BLOCK), :].astype(MXU_DTYPE), NT) for off in offs]
            ts = [jnp.log(1.0 + jnp.exp(-jnp.abs(z))) for z in zs]
            lks = [-(jnp.maximum(z, 0.0) + t) for z, t in zip(zs, ts)]
            if masked:
                masks = [(j * SB_BLOCK + c2) < t_glob for j in js]
                lks = [jnp.where(mk, lk, 0.0) for mk, lk in zip(masks, lks)]
            sufs = [_split_dot(lk, m_gt, SB_SPLIT) for lk in lks]
            rs = [rsum]
            for lk in lks:
                rs.append(rs[-1] + jnp.sum(lk, axis=1, keepdims=True))
            wgts = [jnp.exp((jnp.minimum(z, 0.0) - t) + r_ + sf) for z, t, r_, sf in zip(zs, ts, rs, sufs)]
            if masked:
                wgts = [jnp.where(mk, wg, 0.0) for mk, wg in zip(masks, wgts)]
            wbs = [wg.astype(MXU_DTYPE) for wg in wgts]
            grp = base // nu
            for u, wb in enumerate(wbs):
                stage[half, grp, nu - 1 - u] = wb
            save(half, grp, p, i).start()
            for off, wb in zip(offs, wbs):
                o2 = o2 + _dot(wb, v_ref[pl.ds(off, SB_BLOCK), :].astype(MXU_DTYPE), NN)
            return o2, rs[-1]

        top0 = (i // nu) * nu
        last = i // nu
        carry = group(top0, (jnp.zeros((2 * SB_BLOCK, LANES), F32), jnp.zeros((2 * SB_BLOCK, 1), F32)), True)
        o2, _ = lax.fori_loop(1, last + 1, lambda g, cr: group(top0 - nu * g, cr, False), carry)
        o_ref[...] = _sb_unstack(o2, lm0)

        @pl.when(t >= 1)
        def _():
            drain(1 - half, (t - 1) // nq, (t - 1) % nq)

        @pl.when(t == npair * nq - 1)
        def _():
            drain(half, p, i)

    blk = pl.BlockSpec((SB_BLOCK, LANES), lambda p, i: (i, p))
    return pl.pallas_call(
        body, grid=(npair, nq),
        in_specs=[pl.BlockSpec((SB_BLOCK, LANES), lambda p, i: (i, col0 + p)),
                  pl.BlockSpec((s, LANES), lambda p, i: (0, col0 + npair + p)),
                  pl.BlockSpec((s, LANES), lambda p, i: (0, col0 + 2 * npair + p))],
        out_specs=[blk, ANY],
        out_shape=[jax.ShapeDtypeStruct((s, width), F32),
                   jax.ShapeDtypeStruct((npair, nq, nq // nu, nu, 2 * SB_BLOCK, LANES), MXU_DTYPE)],
        scratch_shapes=[pltpu.VMEM((2, nq // nu, nu, 2 * SB_BLOCK, LANES), MXU_DTYPE),
                        pltpu.SemaphoreType.DMA((2, nq // nu))],
        compiler_params=_cparams(2), name=name,
    )(src, src, src)


def _sb_bwd(src, col0, width, weights, do, *, name):
    s = src.shape[0]
    nq = s // SB_BLOCK
    npair = width // LANES
    scale = SB_HEAD_DIM ** -0.5
    nu = math.gcd(SB_UNROLL, nq)

    def body(q_ref, k_ref, v_ref, w_hbm, do_ref, dq_ref, dk_ref, dv_ref, stage, sems):
        p, i = pl.program_id(0), pl.program_id(1)

        @pl.when(i == 0)
        def _():
            dk_ref[...] = jnp.zeros_like(dk_ref)
            dv_ref[...] = jnp.zeros_like(dv_ref)

        r2, c2, lm0, _, m_lt = _sb_consts()
        t_glob = i * SB_BLOCK + (r2 & (SB_BLOCK - 1))
        q2 = (_sb_stack(q_ref[...], lm0) * scale).astype(MXU_DTYPE)
        do2 = _sb_stack(do_ref[...], lm0).astype(MXU_DTYPE)

        ngrp = nq // nu

        def load(half_, grp, pp, ii):
            return pltpu.make_async_copy(w_hbm.at[pp, ii, grp], stage.at[half_, grp], sems.at[half_, grp])

        def fetch_step(half_, pp, ii):
            for grp in range(ngrp):
                @pl.when(grp <= ii // nu)
                def _():
                    load(half_, grp, pp, ii).start()

        def group(g, carry, masked, slot):
            dq2, csum = carry
            js = [nu * g + u for u in range(nu)]
            offs = [pl.multiple_of(j * SB_BLOCK, SB_BLOCK) for j in js]
            kbs = [k_ref[pl.ds(off, SB_BLOCK), :].astype(MXU_DTYPE) for off in offs]
            zs = [_dot(q2, kb, NT) for kb in kbs]
            dws = [_dot(do2, v_ref[pl.ds(off, SB_BLOCK), :].astype(MXU_DTYPE), NT) for off in offs]
            wbs = [stage[slot[0], slot[1], u] for u in range(nu)]
            sigs = [_sigmoid(z) for z in zs]
            dlogas = [wb.astype(F32) * dw for wb, dw in zip(wbs, dws)]
            pres = [_split_dot(dl, m_lt, 1) for dl in dlogas]
            dlks = []
            for dl, pre in zip(dlogas, pres):
                dlks.append(csum + pre)
                csum = csum + jnp.sum(dl, axis=1, keepdims=True)
            if masked:
                dlks = [jnp.where((j * SB_BLOCK + c2) < t_glob, dlk, 0.0) for j, dlk in zip(js, dlks)]
            dzbs = [(dl * (1.0 - sg) - dlk * sg).astype(MXU_DTYPE) for dl, sg, dlk in zip(dlogas, sigs, dlks)]
            for off, dzb, wb, kb in zip(offs, dzbs, wbs, kbs):
                dk_ref[pl.ds(off, SB_BLOCK), :] += _dot(dzb, q2, TN)
                dv_ref[pl.ds(off, SB_BLOCK), :] += _dot(wb, do2, TN)
                dq2 = dq2 + _dot(dzb, kb, NN)
            return dq2, csum

        t = p * nq + i
        half = t % 2

        @pl.when(t == 0)
        def _():
            fetch_step(0, p, i)

        @pl.when(t + 1 < npair * nq)
        def _():
            fetch_step(1 - half, (t + 1) // nq, (t + 1) % nq)

        def step(g, carry):
            load(half, g, p, i).wait()
            return group(g, carry, False, (half, g))

        last = i // nu
        carry = lax.fori_loop(0, last, step, (jnp.zeros((2 * SB_BLOCK, LANES), F32), jnp.zeros((2 * SB_BLOCK, 1), F32)))
        load(half, last, p, i).wait()
        dq2, _ = group(last, carry, True, (half, last))
        dq_ref[...] = _sb_unstack(dq2, lm0) * scale

    blk = pl.BlockSpec((SB_BLOCK, LANES), lambda p, i: (i, p))
    full = pl.BlockSpec((s, LANES), lambda p, i: (0, p))
    return pl.pallas_call(
        body, grid=(npair, nq),
        in_specs=[pl.BlockSpec((SB_BLOCK, LANES), lambda p, i: (i, col0 + p)),
                  pl.BlockSpec((s, LANES), lambda p, i: (0, col0 + npair + p)),
                  pl.BlockSpec((s, LANES), lambda p, i: (0, col0 + 2 * npair + p)),
                  ANY, blk],
        out_specs=[blk, full, full],
        out_shape=[jax.ShapeDtypeStruct((s, width), F32)] * 3,
        scratch_shapes=[pltpu.VMEM((2, nq // nu, nu, 2 * SB_BLOCK, LANES), MXU_DTYPE),
                        pltpu.SemaphoreType.DMA((2, nq // nu))],
        compiler_params=_cparams(2), name=name,
    )(src, src, src, weights, do)


def _ssd_group(xs, dt_rows, alogs, dtbs, bms, cms, h0s):
    c = bms[0].shape[0]
    per = len(xs) // len(bms)
    ii, jj = _chunk_masks(c)
    causal, eye = ii >= jj, ii == jj
    scores = [t for t in _each(lambda c_, b_: _sdot(c_, b_, NT), cms, bms) for _ in range(per)]
    dt_r = _each(lambda dt, b: _softplus(dt + b), dt_rows, dtbs)
    a_r = _each(lambda al, dt: -jnp.exp(al) * dt, alogs, dt_r)
    dt_col = _each(lambda dt: _row_to_col(dt, eye), dt_r)
    a_col = _each(lambda a: _row_to_col(a, eye), a_r)
    ac_col = _each(lambda a: jnp.sum(jnp.where(causal, a, 0.0), axis=1, keepdims=True), a_r)
    ac_row = _each(lambda a: jnp.sum(jnp.where(jj >= ii, a, 0.0), axis=0, keepdims=True), a_col)
    lmat = _each(lambda c_, r_: jnp.exp(jnp.where(causal, c_ - r_, NEG_BIG)), ac_col, ac_row)
    xdt = _each(jnp.multiply, xs, dt_col)
    al = _each(lambda a: jnp.sum(a, axis=1, keepdims=True), a_r)
    bm = [t for t in bms for _ in range(per)]
    cm = [t for t in cms for _ in range(per)]
    ys = _each(lambda sc, lm, xd, cm_, h0, ac: _sdot(sc * lm, xd) + _sdot(cm_, h0, NT) * jnp.exp(ac),
               scores, lmat, xdt, cm, h0s, ac_col)
    h1s = _each(lambda h0, al_, xd, ac, bm_: h0 * jnp.exp(al_) + _sdot(xd * jnp.exp(al_ - ac), bm_, TN),
                h0s, al, xdt, ac_col, bm)
    return ys, h1s


def _ssd_specs(ng, nc, r, gb, rev):
    n_of = (lambda n: nc - 1 - n) if rev else (lambda n: n)
    xw, bw = gb * r * SSM_HEAD_DIM, gb * SSM_STATE
    b0, c0 = (ng * r * SSM_HEAD_DIM) // bw, (ng * r * SSM_HEAD_DIM + ng * SSM_STATE) // bw
    x_spec = pl.BlockSpec((CHUNK, xw), lambda g, n: (n_of(n), g))
    b_spec = pl.BlockSpec((CHUNK, bw), lambda g, n: (n_of(n), b0 + g))
    c_spec = pl.BlockSpec((CHUNK, bw), lambda g, n: (n_of(n), c0 + g))
    dt_spec = pl.BlockSpec((gb, None, r, CHUNK), lambda g, n: (g, n_of(n), 0, 0))
    sc_spec = pl.BlockSpec((gb, r, 1), lambda g, n: (g, 0, 0))
    st_spec = pl.BlockSpec((gb, None, r, SSM_HEAD_DIM, SSM_STATE), lambda g, n: (g, n_of(n), 0, 0, 0))
    bc_out = pl.BlockSpec((CHUNK, bw), lambda g, n: (n_of(n), g))
    return x_spec, b_spec, c_spec, dt_spec, sc_spec, st_spec, x_spec, bc_out


def _ssd_refs(gb, r, x_ref, b_ref, c_ref, dt_ref, al_ref, db_ref):
    p, n = SSM_HEAD_DIM, SSM_STATE
    heads = [(g, h) for g in range(gb) for h in range(r)]
    xs = [x_ref[:, (g * r + h) * p:(g * r + h + 1) * p] for g, h in heads]
    dts = [dt_ref[g, h:h + 1, :] for g, h in heads]
    als = [al_ref[g, h:h + 1, :] for g, h in heads]
    dbs = [db_ref[g, h:h + 1, :] for g, h in heads]
    bms = [b_ref[:, g * n:(g + 1) * n] for g in range(gb)]
    cms = [c_ref[:, g * n:(g + 1) * n] for g in range(gb)]
    return heads, xs, dts, als, dbs, bms, cms


def _ssd_fwd(xbc, dt_rows, alog, dtb, *, name):
    s = xbc.shape[0]
    ng, nc, r = dt_rows.shape[0], dt_rows.shape[1], dt_rows.shape[2]
    w = ng * r * SSM_HEAD_DIM
    gb = math.gcd(SSD_GROUPS_PER_STEP, ng)
    x_spec, b_spec, c_spec, dt_spec, sc_spec, st_spec, y_spec, _ = _ssd_specs(ng, nc, r, gb, False)
    p = SSM_HEAD_DIM

    def body(x_ref, b_ref, c_ref, dt_ref, al_ref, db_ref, y_ref, st_ref, state):
        @pl.when(pl.program_id(1) == 0)
        def _():
            state[...] = jnp.zeros_like(state)

        st_ref[...] = state[...]
        heads, xs, dts, als, dbs, bms, cms = _ssd_refs(gb, r, x_ref, b_ref, c_ref, dt_ref, al_ref, db_ref)
        ys, h1s = _ssd_group(xs, dts, als, dbs, bms, cms, [state[g, h] for g, h in heads])
        for i, (g, h) in enumerate(heads):
            y_ref[:, (g * r + h) * p:(g * r + h + 1) * p] = ys[i]
            state[g, h] = h1s[i]

    return pl.pallas_call(
        body, grid=(ng // gb, nc),
        in_specs=[x_spec, b_spec, c_spec, dt_spec, sc_spec, sc_spec],
        out_specs=[y_spec, st_spec],
        out_shape=[jax.ShapeDtypeStruct((s, w), F32), jax.ShapeDtypeStruct((ng, nc, r, p, SSM_STATE), F32)],
        scratch_shapes=[pltpu.VMEM((gb, r, p, SSM_STATE), F32)],
        compiler_params=_cparams(2), name=name,
    )(xbc, xbc, xbc, dt_rows, alog, dtb)


def _ssd_bwd(xbc, dt_rows, alog, dtb, states, dy, dx_extra, *, name):
    s = xbc.shape[0]
    ng, nc, r = dt_rows.shape[0], dt_rows.shape[1], dt_rows.shape[2]
    w = ng * r * SSM_HEAD_DIM
    gb = math.gcd(SSD_GROUPS_PER_STEP, ng)
    x_spec, b_spec, c_spec, dt_spec, sc_spec, st_spec, y_spec, bc_out = _ssd_specs(ng, nc, r, gb, True)
    p = SSM_HEAD_DIM

    def body(x_ref, b_ref, c_ref, dt_ref, al_ref, db_ref, st_ref, dy_ref, dxe_ref,
             dx_ref, dbm_ref, dcm_ref, ddt_ref, dal_ref, ddb_ref, dstate):
        @pl.when(pl.program_id(1) == 0)
        def _():
            dstate[...] = jnp.zeros_like(dstate)
            dal_ref[...] = jnp.zeros_like(dal_ref)
            ddb_ref[...] = jnp.zeros_like(ddb_ref)

        heads, xs, dts, als, dbs, bms, cms = _ssd_refs(gb, r, x_ref, b_ref, c_ref, dt_ref, al_ref, db_ref)
        _, vjp = jax.vjp(_ssd_group, xs, dts, als, dbs, bms, cms, [st_ref[g, h] for g, h in heads])
        dys = [dy_ref[:, (g * r + h) * p:(g * r + h + 1) * p] for g, h in heads]
        dxs, ddts, dals, ddbs, dbms, dcms, dh0s = vjp((dys, [dstate[g, h] for g, h in heads]))
        for g in range(gb):
            dbm_ref[:, g * SSM_STATE:(g + 1) * SSM_STATE] = dbms[g]
            dcm_ref[:, g * SSM_STATE:(g + 1) * SSM_STATE] = dcms[g]
        for i, (g, h) in enumerate(heads):
            dx_ref[:, (g * r + h) * p:(g * r + h + 1) * p] = dxs[i] + dxe_ref[:, (g * r + h) * p:(g * r + h + 1) * p]
            ddt_ref[g, h:h + 1, :] = ddts[i]
            dal_ref[g, h:h + 1, :] += dals[i]
            ddb_ref[g, h:h + 1, :] += ddbs[i]
            dstate[g, h] = dh0s[i]

    gn = ng * SSM_STATE
    return pl.pallas_call(
        body, grid=(ng // gb, nc),
        in_specs=[x_spec, b_spec, c_spec, dt_spec, sc_spec, sc_spec, st_spec, y_spec, y_spec],
        out_specs=[y_spec, bc_out, bc_out, dt_spec, sc_spec, sc_spec],
        out_shape=[jax.ShapeDtypeStruct((s, w), F32), jax.ShapeDtypeStruct((s, gn), F32), jax.ShapeDtypeStruct((s, gn), F32),
                   jax.ShapeDtypeStruct(dt_rows.shape, F32), jax.ShapeDtypeStruct((ng, r, 1), F32),
                   jax.ShapeDtypeStruct((ng, r, 1), F32)],
        scratch_shapes=[pltpu.VMEM((gb, r, p, SSM_STATE), F32)],
        compiler_params=_cparams(2), name=name,
    )(xbc, xbc, xbc, dt_rows, alog, dtb, states, dy, dx_extra)


def _ssm_post_fwd(y, xbc, src, z_col0, dexp, nw, *, name, tm=512):
    s, w = y.shape
    gw = w // SSM_GROUPS
    zc = z_col0 * LANES // gw

    def body(y_ref, x_ref, z_ref, d_ref, w_ref, o_ref):
        yy = (y_ref[...] + x_ref[...] * d_ref[...]) * _silu(z_ref[...])
        r = lax.rsqrt(jnp.mean(yy * yy, axis=-1, keepdims=True) + EPS)
        o_ref[...] = (yy * r * w_ref[...]).astype(o_ref.dtype)

    blk = pl.BlockSpec((tm, gw), lambda g, i: (i, g))
    vec = pl.BlockSpec((1, gw), lambda g, i: (0, g))
    return pl.pallas_call(
        body, grid=(SSM_GROUPS, s // tm),
        in_specs=[blk, blk, pl.BlockSpec((tm, gw), lambda g, i: (i, zc + g)), vec, vec],
        out_specs=blk, out_shape=jax.ShapeDtypeStruct((s, w), MXU_DTYPE),
        compiler_params=_cparams(2), name=name,
    )(y, xbc, src, dexp.reshape(1, w), nw.reshape(1, w))


def _ssm_post_bwd(y, xbc, src, z_col0, dexp, nw, dout, into, *, name, tm=512):
    s, w = y.shape
    gw = w // SSM_GROUPS
    zc = z_col0 * LANES // gw

    def body(y_ref, x_ref, z_ref, d_ref, w_ref, do_ref, into_ref, dy_ref, dx_ref, dz_ref, dd_ref, dw_ref):
        xv, zv, dv = x_ref[...], z_ref[...], d_ref[...]
        pre = y_ref[...] + xv * dv
        sz, sz_grad = _silu_and_grad(zv)
        yy = pre * sz
        r = lax.rsqrt(jnp.mean(yy * yy, axis=-1, keepdims=True) + EPS)
        yh = yy * r
        dov = do_ref[...]
        dyn = dov * w_ref[...]
        dyy = r * (dyn - yh * jnp.mean(dyn * yh, axis=-1, keepdims=True))
        dpre = dyy * sz
        dy_ref[...] = dpre
        dx_ref[...] = dpre * dv
        dz_ref[...] = (dyy * pre * sz_grad).astype(dz_ref.dtype)

        @pl.when(pl.program_id(1) == 0)
        def _():
            dd_ref[...] = jnp.zeros_like(dd_ref)
            dw_ref[...] = jnp.zeros_like(dw_ref)

        dd_ref[...] += jnp.sum(dpre * xv, axis=0, keepdims=True)
        dw_ref[...] += jnp.sum(dov * yh, axis=0, keepdims=True)

    blk = pl.BlockSpec((tm, gw), lambda g, i: (i, g))
    vec = pl.BlockSpec((1, gw), lambda g, i: (0, g))
    z_blk = pl.BlockSpec((tm, gw), lambda g, i: (i, zc + g))
    dy, dx, dz, dd, dw = pl.pallas_call(
        body, grid=(SSM_GROUPS, s // tm),
        in_specs=[blk, blk, z_blk, vec, vec, blk, ANY],
        out_specs=[blk, blk, z_blk, vec, vec],
        out_shape=[jax.ShapeDtypeStruct((s, w), F32), jax.ShapeDtypeStruct((s, w), F32),
                   jax.ShapeDtypeStruct(into.shape, into.dtype), jax.ShapeDtypeStruct((1, w), F32),
                   jax.ShapeDtypeStruct((1, w), F32)],
        input_output_aliases={6: 2},
        compiler_params=_cparams(2), name=name,
    )(y, xbc, src, dexp.reshape(1, w), nw.reshape(1, w), dout, into)
    return dy, dx, dz, dd.reshape(w), dw.reshape(w)


def _merge_fwd(proj3, src, gate_col0, d, *, name, tm=512):
    s = proj3.shape[0]
    nb = proj3.shape[1] // d
    gc = gate_col0 * LANES // d

    def body(*refs):
        p_refs, g_refs, o_ref = refs[:nb], refs[nb:2 * nb], refs[-1]
        acc = None
        for p_ref, g_ref in zip(p_refs, g_refs):
            term = _sigmoid(g_ref[...]) * p_ref[...]
            acc = term if acc is None else acc + term
        o_ref[...] = acc.astype(o_ref.dtype)

    p_specs = [pl.BlockSpec((tm, d), lambda i, b=b: (i, b)) for b in range(nb)]
    g_specs = [pl.BlockSpec((tm, d), lambda i, b=b: (i, gc + b)) for b in range(nb)]
    return pl.pallas_call(
        body, grid=(s // tm,), in_specs=p_specs + g_specs,
        out_specs=pl.BlockSpec((tm, d), lambda i: (i, 0)), out_shape=jax.ShapeDtypeStruct((s, d), MXU_DTYPE),
        compiler_params=_cparams(1), name=name,
    )(*([proj3] * nb), *([src] * nb))


def _merge_bwd(proj3, src, gate_col0, d, dmerged, into, *, name, tm=512):
    s = proj3.shape[0]
    nb = proj3.shape[1] // d
    gc = gate_col0 * LANES // d

    def body(p_ref, g_ref, dm_ref, into_ref, dp_ref, dg_ref):
        sg = _sigmoid(g_ref[...])
        dm = dm_ref[...]
        dp_ref[...] = (dm * sg).astype(dp_ref.dtype)
        dg_ref[...] = (dm * p_ref[...] * sg * (1.0 - sg)).astype(dg_ref.dtype)

    blk = pl.BlockSpec((tm, d), lambda i, b: (i, b))
    gate_blk = pl.BlockSpec((tm, d), lambda i, b: (i, gc + b))
    return pl.pallas_call(
        body, grid=(s // tm, nb),
        in_specs=[blk, gate_blk, pl.BlockSpec((tm, d), lambda i, b: (i, 0)), ANY],
        out_specs=[blk, gate_blk],
        out_shape=[jax.ShapeDtypeStruct(proj3.shape, MXU_DTYPE), jax.ShapeDtypeStruct(into.shape, into.dtype)],
        input_output_aliases={3: 1},
        compiler_params=_cparams(2), name=name,
    )(proj3, src, dmerged, into)


ANY = pl.BlockSpec(memory_space=pl.ANY)
MESH = pl.DeviceIdType.MESH


def _all_gather(shards, *, name, after=None):
    nt = len(shards)
    n_after = 0 if after is None else 1

    def body(*refs):
        x_refs, out_refs = refs[:nt], refs[nt + n_after:2 * nt + n_after]
        send_sems, recv_sems, local_sems = refs[2 * nt + n_after:]
        x, y, c = lax.axis_index("x"), lax.axis_index("y"), lax.axis_index("c")
        me, sibling = (x, y, c), (x, y, 1 - c)
        chips = [(1 - x, y), (x, 1 - y), (1 - x, 1 - y)]

        def slot(t, px, py, pc):
            return out_refs[t].at[4 * px + 2 * py + pc]

        def copy(t, k, block, to, from_input=False):
            return pltpu.make_async_remote_copy(
                src_ref=x_refs[t] if from_input else slot(t, *block), dst_ref=slot(t, *block),
                send_sem=send_sems.at[7 * t + k], recv_sem=recv_sems.at[7 * t + k], device_id=to, device_id_type=MESH)

        mine = [pltpu.make_async_copy(x_refs[t], slot(t, *me), local_sems.at[t]) for t in range(nt)]
        for cp in mine:
            cp.start()
        first = [copy(t, 0, me, sibling, True) for t in range(nt)]
        first += [copy(t, 1 + j, me, (*chip, c), True) for j, chip in enumerate(chips) for t in range(nt)]
        for cp in first:
            cp.start()
        passed = []
        for j, chip in enumerate(chips):
            for t in range(nt):
                copy(t, 1 + j, (*chip, c), me).wait_recv()
                fwd = copy(t, 4 + j, (*chip, c), sibling)
                fwd.start()
                passed.append(fwd)
        for t in range(nt):
            copy(t, 0, sibling, me).wait_recv()
            for j, chip in enumerate(chips):
                copy(t, 4 + j, (*chip, 1 - c), me).wait_recv()
        for cp in first + passed:
            cp.wait_send()
        for cp in mine:
            cp.wait()

    return pl.pallas_call(
        body, out_shape=[jax.ShapeDtypeStruct((N_DEV,) + a.shape, a.dtype) for a in shards],
        in_specs=[ANY] * (nt + n_after), out_specs=[ANY] * nt,
        scratch_shapes=[pltpu.SemaphoreType.DMA((7 * nt,)), pltpu.SemaphoreType.DMA((7 * nt,)),
                        pltpu.SemaphoreType.DMA((nt,))],
        name=name,
    )(*shards, *([] if after is None else [after]))


HBM = pl.BlockSpec(memory_space=pltpu.HBM)
SEM = pl.BlockSpec(memory_space=pltpu.SEMAPHORE)
EFFECT = pltpu.SideEffectType.DATAFLOW_SIDE_EFFECTING


def _peers():
    x, y, c = lax.axis_index("x"), lax.axis_index("y"), lax.axis_index("c")
    peers = []
    for k in range(1, N_DEV):
        px, py, pc = x ^ ((k >> 2) & 1), y ^ ((k >> 1) & 1), c ^ (k & 1)
        peers.append(((px, py, pc), 4 * px + 2 * py + pc))
    return 4 * x + 2 * y + c, peers


def _split_copies(slots, src_refs, land_refs, send_sems, recv_sems):
    me, peers = _peers()
    copies = []
    for t, (whole, layer) in enumerate(slots):
        dst = land_refs[t].at[me] if layer is None else land_refs[t].at[me, layer]
        for k, (dev, lin) in enumerate(peers):
            copies.append(pltpu.make_async_remote_copy(
                src_ref=src_refs[t] if whole else src_refs[t].at[lin], dst_ref=dst,
                send_sem=send_sems.at[7 * t + k], recv_sem=recv_sems.at[7 * t + k], device_id=dev, device_id_type=MESH))
    return copies


def _split_start(srcs, lands, slots, carry, *, name):
    n = len(srcs)

    def body(*refs):
        copies = _split_copies(slots, refs[:n], refs[n:2 * n], refs[2 * n + 1], refs[2 * n + 2])
        for cp in copies:
            cp.start()

    def hbm(a):
        return pltpu.HBM(a.shape, a.dtype)

    outs = pl.pallas_call(
        body, name=name,
        out_shape=[pltpu.SemaphoreType.DMA((7 * n,)), pltpu.SemaphoreType.DMA((7 * n,))]
        + [hbm(a) for a in srcs] + [hbm(a) for a in lands] + [hbm(carry)],
        in_specs=[HBM] * (2 * n + 1), out_specs=[SEM, SEM] + [HBM] * (2 * n + 1),
        input_output_aliases={i: 2 + i for i in range(2 * n + 1)},
        compiler_params=pltpu.CompilerParams(has_side_effects=EFFECT),
    )(*[pltpu.with_memory_space_constraint(a, pltpu.HBM) for a in list(srcs) + list(lands) + [carry]])
    return outs[0], outs[1], outs[2:2 + n], outs[2 + n:2 + 2 * n], outs[2 + 2 * n]


def _split_wait(send_sems, recv_sems, srcs, lands, slots, after, *, name):
    n = len(srcs)

    def body(*refs):
        copies = _split_copies(slots, refs[:n], refs[n:2 * n], refs[2 * n], refs[2 * n + 1])
        for cp in copies:
            cp.wait_send()
        for cp in copies:
            cp.wait_recv()

    outs = pl.pallas_call(
        body, name=name,
        out_shape=[pltpu.HBM(a.shape, a.dtype) for a in list(srcs) + list(lands)],
        in_specs=[HBM] * (2 * n) + [SEM, SEM, ANY], out_specs=[HBM] * (2 * n),
        input_output_aliases={i: i for i in range(2 * n)},
        compiler_params=pltpu.CompilerParams(has_side_effects=EFFECT),
    )(*srcs, *lands, send_sems, recv_sems, after)
    return outs[n:]


def _adam_math(w, g, m, v):
    m1 = ADAM_B1 * m + (1.0 - ADAM_B1) * g
    v1 = ADAM_B2 * v + (1.0 - ADAM_B2) * (g * g)
    m_hat = m1 / (1.0 - ADAM_B1 ** ADAM_STEP)
    v_hat = v1 / (1.0 - ADAM_B2 ** ADAM_STEP)
    delta = -ADAM_LR * (m_hat / (jnp.sqrt(v_hat) + ADAM_EPS) + ADAM_WD * w)
    return delta, m1, v1


def _sum_adamw(parts, w, m, v, layer, prev, *, name):
    shape = w.shape
    r, c = shape[-2], shape[-1]
    a_l = math.prod(shape[1:-2])
    a = shape[0] * a_l
    base = layer * a_l
    if r % 256 == 0:
        tr, tc = 256, c
    else:
        tr, tc = r, _pick(c, (256, 128))
    w3, m3, v3 = (t.reshape(a, r, c) for t in (w, m, v))
    n_prev = 0 if prev is None else 4

    def body(*refs):
        p_ref, w_ref, m_ref, v_ref = refs[:4]
        g_ref, d_ref, m1_ref, v1_ref = refs[4 + n_prev:]
        g = p_ref[0].astype(F32)
        for src in range(1, N_DEV):
            g = g + p_ref[src].astype(F32)
        delta, m1, v1 = _adam_math(w_ref[...], g, m_ref[...], v_ref[...])
        g_ref[...] = g
        d_ref[...] = delta
        m1_ref[...] = m1
        v1_ref[...] = v1

    nr, ncol = r // tr, c // tc
    blk = pl.BlockSpec((None, tr, tc), lambda i, j: (base + i, j // ncol, j % ncol))
    prev3 = [] if prev is None else [t.reshape(a, r, c) for t in prev]
    outs = pl.pallas_call(
        body, grid=(a_l, nr * ncol),
        in_specs=[pl.BlockSpec((N_DEV, None, tr, tc), lambda i, j: (0, i, j // ncol, j % ncol)), blk, blk, blk]
        + [ANY] * n_prev,
        out_specs=[blk] * 4, out_shape=[jax.ShapeDtypeStruct((a, r, c), F32)] * 4,
        input_output_aliases={4 + k: k for k in range(n_prev)},
        compiler_params=_cparams(2), name=name,
    )(parts.reshape(N_DEV, a_l, r, c), w3, m3, v3, *prev3)
    return [o.reshape(shape) for o in outs]


def _sum_parts(parts, *, name):
    rows = parts.shape[1]

    def body(p_ref, o_ref):
        g = p_ref[0]
        for src in range(1, N_DEV):
            g = g + p_ref[src]
        o_ref[...] = g

    return pl.pallas_call(
        body, grid=(1,), in_specs=[pl.BlockSpec((N_DEV, rows, LANES), lambda i: (0, 0, 0))],
        out_specs=pl.BlockSpec((rows, LANES), lambda i: (0, 0)), out_shape=jax.ShapeDtypeStruct((rows, LANES), F32),
        compiler_params=_cparams(1), name=name,
    )(parts)


def _adamw(w, g, m, v, *, name):
    rows = w.shape[0]

    def body(w_ref, g_ref, m_ref, v_ref, d_ref, m1_ref, v1_ref):
        delta, m1, v1 = _adam_math(w_ref[...], g_ref[...], m_ref[...], v_ref[...])
        d_ref[...] = delta
        m1_ref[...] = m1
        v1_ref[...] = v1

    blk = pl.BlockSpec((rows, LANES), lambda i: (0, 0))
    return pl.pallas_call(
        body, grid=(1,), in_specs=[blk] * 4, out_specs=[blk] * 3,
        out_shape=[jax.ShapeDtypeStruct((rows, LANES), F32)] * 3,
        compiler_params=_cparams(1), name=name,
    )(w, g, m, v)


def _pack(arrs, dtype, row_mult=16):
    flat = jnp.concatenate([a.reshape(-1).astype(dtype) for a in arrs])
    n = flat.shape[0]
    rows = -(-n // (LANES * row_mult)) * row_mult
    flat = jnp.pad(flat, (0, rows * LANES - n))
    return flat.reshape(rows, LANES)


def _unpack(packed, shapes):
    flat = packed.reshape(-1)
    out, off = [], 0
    for shp in shapes:
        n = math.prod(shp)
        out.append(flat[off:off + n].reshape(shp))
        off += n
    return out


class _Layout:
    def __init__(self, d):
        self.d = d
        w = d
        self.dn_heads = w // DN_HEAD_DIM
        self.ssm_heads = w // SSM_HEAD_DIM
        gn = SSM_GROUPS * SSM_STATE
        self.sizes = (3 * w, w, self.dn_heads, self.dn_heads, 3 * w, w, w + 2 * gn, self.ssm_heads, 3 * d)
        offs, o = [], 0
        for sz in self.sizes:
            offs.append(o)
            o += sz
        self.offs = offs
        self.in_dim = o
        self.big = (0, 1, 4, 5, 6, 8)
        self.small = (2, 3, 7)
        cols, o = {}, 0
        for idx in self.big:
            cols[idx] = o
            o += self.sizes[idx]
        self.small_col = o
        self.cols = cols
        self.padded = o + LANES
        self.n_small = sum(self.sizes[i] for i in self.small)

    def from_shards(self, parts):
        cs = self.in_dim // N_DEV
        pieces = []
        for i in self.big + self.small:
            a, b = self.offs[i], self.offs[i] + self.sizes[i]
            while a < b:
                j = a // cs
                hi = min(b, (j + 1) * cs)
                pieces.append(parts[j][:, a - j * cs:hi - j * cs])
                a = hi
        pieces.append(jnp.zeros((parts.shape[1], LANES - self.n_small), parts.dtype))
        return jnp.concatenate(pieces, axis=1)

    def to_shards(self, wp):
        cs = self.in_dim // N_DEV
        pcol = dict(self.cols)
        o = self.small_col
        for i in self.small:
            pcol[i] = o
            o += self.sizes[i]
        shards = []
        for j in range(N_DEV):
            a, b = j * cs, (j + 1) * cs
            pieces = []
            for i in range(len(self.sizes)):
                lo, hi = max(a, self.offs[i]), min(b, self.offs[i] + self.sizes[i])
                if lo < hi:
                    pieces.append(wp[:, pcol[i] + lo - self.offs[i]:pcol[i] + hi - self.offs[i]])
            shards.append(jnp.concatenate(pieces, axis=1))
        return jnp.stack(shards)

def _rows_form(cols_t, nh, nc):
    return cols_t.T.reshape(nh, nc, 1, CHUNK)


def _layer_fwd(x, p, lay, tag, late=None):
    s, d = x.shape
    nc = s // CHUNK
    w = d
    dnh, smh = lay.dn_heads, lay.ssm_heads
    r = smh // SSM_GROUPS
    cb = {k: v // LANES for k, v in lay.cols.items()}
    sv = {}
    h1 = _rms_fwd(x, p["norm_mix"], name=f"rms_mix_{tag}")
    proj = _matmul(h1, p["w_in"], name=f"mm_in_{tag}")
    small = proj[:, lay.small_col:lay.small_col + LANES]
    a_rows = _rows_form(small[:, 0:dnh], dnh, nc)
    b_rows = _rows_form(small[:, dnh:2 * dnh], dnh, nc)
    dt_rows = small[:, 2 * dnh:2 * dnh + smh].T.reshape(SSM_GROUPS, r, nc, CHUNK).transpose(0, 2, 1, 3)
    zero_b = jnp.zeros((1, 3 * w), F32)
    dn_qkv = _conv_fwd(proj, cb[0], p["dn_conv_w"], zero_b, 2 * dnh, name=f"dn_conv_{tag}")
    dn_alog = p["dn_a_log"].reshape(dnh, 1, 1)
    dn_dtb = p["dn_dt_bias"].reshape(dnh, 1, 1)
    o_dn, dn_states, dn_inv = _dn_fwd(dn_qkv, a_rows, b_rows, dn_alog, dn_dtb, name=f"dn_chunk_{tag}")
    y_dn = _dn_post_fwd(o_dn, proj, cb[1], p["dn_norm_w"], name=f"dn_post_{tag}")
    o_sb, sb_r = _sb_fwd(proj, cb[4], w, name=f"sb_{tag}")
    xbc = _conv_fwd(proj, cb[6], p["ssm_conv_w"], p["ssm_conv_b"].reshape(1, -1), 0, name=f"ssm_conv_{tag}")
    ssm_alog = p["ssm_a_log"].reshape(SSM_GROUPS, r, 1)
    ssm_dtb = p["ssm_dt_bias"].reshape(SSM_GROUPS, r, 1)
    y_ssd, ssm_states = _ssd_fwd(xbc, dt_rows, ssm_alog, ssm_dtb, name=f"ssd_{tag}")
    dexp = jnp.repeat(p["ssm_d"], SSM_HEAD_DIM)
    y_ssm = _ssm_post_fwd(y_ssd, xbc, proj, cb[5], dexp, p["ssm_norm_w"], name=f"ssm_post_{tag}")
    if late is not None:
        p.update(late(y_ssm))
    branches = (y_dn, o_sb, y_ssm)
    proj3 = lax.empty((s, 3 * d), F32)
    for i, br in enumerate(branches):
        proj3 = _matmul(br, p["w_branch"][i], into=(proj3, i * d), name=f"mm_branch{i}_{tag}")
    merged = _merge_fwd(proj3, proj, cb[8], d, name=f"merge_{tag}")
    x1 = _matmul(merged, p["w_out"], name=f"mm_out_{tag}", epilogue=lambda acc, res: (acc + res,), extras=(x,))
    h2 = _rms_fwd(x1, p["norm_mlp"], name=f"rms_mlp_{tag}")
    u, act = _matmul(h2, p["w_up"], name=f"mm_up_{tag}", out_dtypes=(F32, MXU_DTYPE),
                     epilogue=lambda acc: (acc, jnp.square(jnp.maximum(acc, 0.0))))
    x2 = _matmul(act, p["w_down"], name=f"mm_down_{tag}", epilogue=lambda acc, res: (acc + res,), extras=(x1,))
    sv.update(x=x, h1=h1, proj=proj, a_rows=a_rows, b_rows=b_rows, dt_rows=dt_rows, dn_qkv=dn_qkv, dn_alog=dn_alog,
              dn_dtb=dn_dtb, o_dn=o_dn, dn_states=dn_states, dn_inv=dn_inv, y_dn=y_dn, o_sb=o_sb, sb_r=sb_r, xbc=xbc, ssm_alog=ssm_alog,
              ssm_dtb=ssm_dtb, y_ssd=y_ssd, ssm_states=ssm_states, dexp=dexp, y_ssm=y_ssm, proj3=proj3, merged=merged,
              x1=x1, h2=h2, u=u, act=act)
    return x2, sv


def _layer_bwd(dx2, p, sv, lay, tag, early=None, late=None):
    x = sv["x"]
    s, d = x.shape
    nc = s // CHUNK
    w = d
    dnh, smh = lay.dn_heads, lay.ssm_heads
    r = smh // SSM_GROUPS
    gn = SSM_GROUPS * SSM_STATE
    cb = {k: v // LANES for k, v in lay.cols.items()}
    proj = sv["proj"]
    g = {}
    dx2_b = dx2.astype(MXU_DTYPE)
    du = _matmul(dx2_b, p["w_down"], tb=True, name=f"mm_down_dx_{tag}", out_dtypes=(MXU_DTYPE,),
                 epilogue=lambda acc, uu: (acc * (2.0 * jnp.maximum(uu, 0.0)),), extras=(sv["u"],))
    g["w_down"] = _matmul(sv["act"], dx2_b, ta=True, name=f"mm_down_dw_{tag}", out_dtypes=(BF16,)).reshape(N_DEV, -1, d)
    g["w_up"] = _matmul(sv["h2"], du, ta=True, name=f"mm_up_dw_{tag}", out_dtypes=(BF16,), col_shards=N_DEV)
    dh2 = _matmul(du, p["w_up"], tb=True, name=f"mm_up_dx_{tag}")
    dx1, g["norm_mlp"] = _rms_bwd(sv["x1"], p["norm_mlp"], dh2, dx2, name=f"rms_mlp_bwd_{tag}")
    dx1_b = dx1.astype(MXU_DTYPE)
    dmerged = _matmul(dx1_b, p["w_out"], tb=True, name=f"mm_out_dx_{tag}")
    g["w_out"] = _matmul(sv["merged"], dx1_b, ta=True, name=f"mm_out_dw_{tag}", out_dtypes=(BF16,)).reshape(N_DEV, -1, d)
    dproj = lax.empty((s, lay.padded), MXU_DTYPE)
    dproj3, dproj = _merge_bwd(sv["proj3"], proj, cb[8], d, dmerged, dproj, name=f"merge_bwd_{tag}")
    branches = (sv["y_dn"], sv["o_sb"], sv["y_ssm"])
    dwb, dbr = [], []
    for i, br in enumerate(branches):
        cols = (i * d, d)
        dwb.append(_matmul(br, dproj3, ta=True, b_cols=cols, name=f"mm_branch{i}_dw_{tag}",
                           out_dtypes=(BF16,)).reshape(N_DEV, -1, d))
        dbr.append(_matmul(dproj3, p["w_branch"][i], tb=True, a_cols=cols, name=f"mm_branch{i}_dx_{tag}"))
    g["w_branch"] = jnp.stack(dwb, axis=1)
    dy_dn, do_sb, dy_ssm = dbr
    if early is not None:
        dy_ssm = early(g, dy_ssm)
    dy_ssd, dxs_skip, dproj, ddexp, g["ssm_norm_w"] = _ssm_post_bwd(
        sv["y_ssd"], sv["xbc"], proj, cb[5], sv["dexp"], p["ssm_norm_w"], dy_ssm, dproj, name=f"ssm_post_bwd_{tag}")
    g["ssm_d"] = ddexp.reshape(smh, SSM_HEAD_DIM).sum(axis=1)
    dxs, dbm, dcm, ddt_rows, dalog, ddtb = _ssd_bwd(
        sv["xbc"], sv["dt_rows"], sv["ssm_alog"], sv["ssm_dtb"], sv["ssm_states"], dy_ssd, dxs_skip, name=f"ssd_bwd_{tag}")
    g["ssm_a_log"] = dalog.reshape(smh)
    g["ssm_dt_bias"] = ddtb.reshape(smh)
    dxbc_post = jnp.concatenate([dxs, dbm, dcm], axis=1)
    dproj, g["ssm_conv_w"], dcb = _conv_bwd(proj, cb[6], p["ssm_conv_w"], p["ssm_conv_b"].reshape(1, -1), 0, dxbc_post,
                                            dproj, name=f"ssm_conv_bwd_{tag}")
    g["ssm_conv_b"] = dcb.reshape(-1)
    ddt = ddt_rows.transpose(0, 2, 1, 3).reshape(smh, s).T
    dqkv_sb = _sb_bwd(proj, cb[4], w, sv["sb_r"], do_sb, name=f"sb_bwd_{tag}")
    dproj = lax.dynamic_update_slice(dproj, jnp.concatenate([t.astype(MXU_DTYPE) for t in dqkv_sb], axis=1), (0, lay.cols[4]))
    do_dn, dproj, g["dn_norm_w"] = _dn_post_bwd(sv["o_dn"], proj, cb[1], p["dn_norm_w"], dy_dn, dproj,
                                                name=f"dn_post_bwd_{tag}")
    dqkv_dn, da_rows, db_rows, dal, ddtb_dn = _dn_bwd(
        sv["dn_qkv"], sv["a_rows"], sv["b_rows"], sv["dn_alog"], sv["dn_dtb"], sv["dn_states"], sv["dn_inv"], do_dn,
        name=f"dn_chunk_bwd_{tag}")
    g["dn_a_log"] = dal.reshape(dnh)
    g["dn_dt_bias"] = ddtb_dn.reshape(dnh)
    zero_b = jnp.zeros((1, 3 * w), F32)
    dproj, g["dn_conv_w"], _ = _conv_bwd(proj, cb[0], p["dn_conv_w"], zero_b, 2 * dnh, dqkv_dn, dproj,
                                         name=f"dn_conv_bwd_{tag}")
    da = da_rows.reshape(dnh, s).T
    db = db_rows.reshape(dnh, s).T
    dsmall = jnp.concatenate([da, db, ddt, jnp.zeros((s, LANES - lay.n_small), F32)], axis=1).astype(MXU_DTYPE)
    dproj = lax.dynamic_update_slice(dproj, dsmall, (0, lay.small_col))
    g["w_in"] = lay.to_shards(_matmul(sv["h1"], dproj, ta=True, name=f"mm_in_dw_{tag}", out_dtypes=(BF16,)))
    if late is not None:
        dproj = late(g, dproj)
    dh1 = _matmul(dproj, p["w_in"], tb=True, name=f"mm_in_dx_{tag}")
    dx0, g["norm_mix"] = _rms_bwd(x, p["norm_mix"], dh1, dx1, name=f"rms_mix_bwd_{tag}")
    return dx0, g


BIG = ("w_in", "w_branch", "w_out", "w_up", "w_down")
CONV = ("dn_conv_w", "ssm_conv_w")
SMALL = ("norm_mix", "dn_conv_w", "dn_a_log", "dn_dt_bias", "dn_norm_w", "ssm_conv_w", "ssm_conv_b", "ssm_a_log",
         "ssm_dt_bias", "ssm_d", "ssm_norm_w", "norm_mlp", "norm_final")
WEIGHTS = ("norm_mix", "w_in", "dn_conv_w", "dn_a_log", "dn_dt_bias", "dn_norm_w", "ssm_conv_w", "ssm_conv_b", "ssm_a_log",
           "ssm_dt_bias", "ssm_d", "ssm_norm_w", "w_branch", "w_out", "norm_mlp", "w_up", "w_down", "norm_final")
SHARD_AXIS = {"w_in": 2, "dn_conv_w": 2, "ssm_conv_w": 2, "w_branch": 2, "w_out": 1, "w_up": 2, "w_down": 1}


def _to_shards(full, axis):
    shp = full.shape
    n = shp[axis] // N_DEV
    t = full.reshape(shp[:axis] + (N_DEV, n) + shp[axis + 1:])
    return jnp.moveaxis(t, axis, 0)


def _unshard(parts, axis, *, name):
    shard = parts.shape[1:]
    nd = len(shard)
    if axis == 0:
        return parts.reshape((N_DEV * shard[0],) + shard[1:])

    def copy_block(i_ref, o_ref):
        o_ref[...] = i_ref[...]

    if axis == nd - 1:
        rows, n = math.prod(shard[:-1]), shard[-1]
        out = pl.pallas_call(
            copy_block, grid=(N_DEV,),
            in_specs=[pl.BlockSpec((None, rows, n), lambda j: (j, 0, 0))],
            out_specs=pl.BlockSpec((rows, n), lambda j: (0, j)),
            out_shape=jax.ShapeDtypeStruct((rows, N_DEV * n), parts.dtype),
            compiler_params=_cparams(1), name=name,
        )(parts.reshape(N_DEV, rows, n))
        return out.reshape(shard[:-1] + (N_DEV * n,))
    assert axis == nd - 2, (parts.shape, axis)
    a, n, c = math.prod(shard[:-2]), shard[-2], shard[-1]
    out = pl.pallas_call(
        copy_block, grid=(N_DEV, a),
        in_specs=[pl.BlockSpec((None, None, n, c), lambda j, i: (j, i, 0, 0))],
        out_specs=pl.BlockSpec((None, n, c), lambda j, i: (i, j, 0)),
        out_shape=jax.ShapeDtypeStruct((a, N_DEV * n, c), parts.dtype),
        compiler_params=_cparams(2), name=name,
    )(parts.reshape(N_DEV, a, n, c))
    return out.reshape(shard[:-2] + (N_DEV * n, c))


def _step(w, m, v, x, target):
    s, d = x.shape
    lay = _Layout(d)
    me = 4 * lax.axis_index("x") + 2 * lax.axis_index("y") + lax.axis_index("c")

    def shard(n, l):
        return w[n][l].astype(BF16) if n in BIG else w[n][l]

    def empty_land(a):
        return lax.empty((N_DEV,) + a.shape, a.dtype)

    def with_own(land, own):
        return lax.dynamic_update_index_in_dim(land, own, me, 0)

    def assemble(n, parts, l):
        return lay.from_shards(parts) if n == "w_in" else _unshard(parts, SHARD_AXIS[n] - 1, name=f"unshard_{n}_l{l}")

    small_names = tuple(n for n in WEIGHTS if n not in BIG + CONV + ("norm_final",))

    first, rest = ("w_in",) + CONV, BIG[1:]
    got = _all_gather([shard(n, 0) for n in first], name="gather_l0_first")
    whole, sliced = (True, None), (False, None)
    names_a, names_b = rest, BIG + CONV
    srcs_a, srcs_b = [shard(n, 0) for n in names_a], [shard(n, 1) for n in names_b]
    sem_sa, sem_ra, srcs_a, lands_a, w_in0 = _split_start(
        srcs_a, [empty_land(a) for a in srcs_a], [whole] * len(srcs_a), got[0], name="gather_l0_rest_start")
    sem_sb, sem_rb, srcs_b, lands_b, w_in0 = _split_start(
        srcs_b, [empty_land(a) for a in srcs_b], [whole] * len(srcs_b), w_in0, name="gather_l1_start")
    p0 = {n: w[n][0] for n in small_names}
    p0.update({n: assemble(n, g, 0) for n, g in zip(first, [w_in0] + list(got[1:]))})

    def late_l0(after):
        lands = _split_wait(sem_sa, sem_ra, srcs_a, lands_a, [whole] * len(srcs_a), after, name="gather_l0_rest_wait")
        return {n: assemble(n, with_own(ld, s_), 0) for n, ld, s_ in zip(names_a, lands, srcs_a)}

    h, sv0 = _layer_fwd(x, p0, lay, "l0", late=late_l0)
    lands = _split_wait(sem_sb, sem_rb, srcs_b, lands_b, [whole] * len(srcs_b), h, name="gather_l1_wait")
    p1 = {n: w[n][1] for n in small_names}
    p1.update({n: assemble(n, with_own(ld, s_), 1) for n, ld, s_ in zip(names_b, lands, srcs_b)})
    h, sv1 = _layer_fwd(h, p1, lay, "l1")
    loss, dh, g_norm_final = _final_loss(h, w["norm_final"], target, name="final_loss")
    grads = [None] * DEPTH
    dh, grads[1] = _layer_bwd(dh, p1, sv1, lay, "l1")

    def exchange_start(names, g, carry, tag):
        srcs = [g[n] for n in names]
        return _split_start(srcs, [lax.empty(a.shape, a.dtype) for a in srcs], [sliced] * len(srcs), carry,
                            name=f"grad_{tag}_start")

    def exchange_wait(names, started, after, tag):
        sem_s, sem_r, srcs, lands_, _ = started
        lands_ = _split_wait(sem_s, sem_r, srcs, lands_, [sliced] * len(srcs), after, name=f"grad_{tag}_wait")
        return {n: with_own(ld, lax.dynamic_index_in_dim(s_, me, 0, keepdims=False)) for n, ld, s_ in zip(names, lands_, srcs)}

    x1_started = exchange_start(BIG, grads[1], dh, "l1")
    pending = {}

    def early_l0(g, carry):
        pending["rest"] = exchange_start(rest, g, carry, "l0_rest")
        return pending["rest"][4]

    def late_bwd_l0(g, carry):
        pending["w_in"] = exchange_start(("w_in",), g, carry, "l0_w_in")
        return pending["w_in"][4]

    grad_x, grads[0] = _layer_bwd(x1_started[4], p0, sv0, lay, "l0", early=early_l0, late=late_bwd_l0)

    out = {"grad": {}, "delta": {}, "new_m": {}, "new_v": {}}
    parts1 = exchange_wait(BIG, x1_started, grad_x, "l1")
    res1 = {n: _sum_adamw(parts1[n], w[n], m[n], v[n], 1, None, name=f"sum_adamw_{n}_l1") for n in BIG}
    parts0 = exchange_wait(rest, pending["rest"], res1["w_in"][0], "l0_rest")
    res0 = {n: _sum_adamw(parts0[n], w[n], m[n], v[n], 0, res1[n], name=f"sum_adamw_{n}_l0") for n in rest}
    parts0 = exchange_wait(("w_in",), pending["w_in"], res0["w_down"][0], "l0_w_in")
    res0["w_in"] = _sum_adamw(parts0["w_in"], w["w_in"], m["w_in"], v["w_in"], 0, res1["w_in"], name="sum_adamw_w_in_l0")
    for n in BIG:
        for key, a in zip(("grad", "delta", "new_m", "new_v"), res0[n]):
            out[key][n] = a

    gfull = {n: jnp.stack([grads[l][n] for l in range(DEPTH)]) for n in SMALL if n != "norm_final"}
    gfull["norm_final"] = g_norm_final
    small_send = _pack([gfull[n] for n in SMALL] + [loss.reshape(1)], F32)
    small_recv = _all_gather([small_send], name="gather_small_grads", after=res0["w_in"][0])[0]
    small_sum = _sum_parts(small_recv, name="sum_small")
    small_full = _unpack(small_sum, [gfull[n].shape for n in SMALL] + [(1,)])
    loss_total = small_full[-1][0]
    gsmall = {}
    for n, a in zip(SMALL, small_full[:-1]):
        if n in SHARD_AXIS:
            a = lax.dynamic_index_in_dim(_to_shards(a, SHARD_AXIS[n]), me, axis=0, keepdims=False)
        gsmall[n] = a
    small_shapes = [w[n].shape for n in SMALL]
    ws, gs, ms, vs = (_pack([t[n] for n in SMALL], F32) for t in (w, gsmall, m, v))
    ds, m1s, v1s = _adamw(ws, gs, ms, vs, name="adamw_small")
    for n in SMALL:
        out["grad"][n] = gsmall[n]
    for key, packed in (("delta", ds), ("new_m", m1s), ("new_v", v1s)):
        for n, a in zip(SMALL, _unpack(packed, small_shapes)):
            out[key][n] = a
    return loss_total, grad_x, out


def kernel(x, norm_mix, w_in, dn_conv_w, dn_a_log, dn_dt_bias, dn_norm_w, ssm_conv_w, ssm_conv_b, ssm_a_log, ssm_dt_bias, ssm_d, ssm_norm_w, w_branch, w_out, norm_mlp, w_up, w_down, norm_final, loss_target, m_norm_mix, m_w_in, m_dn_conv_w, m_dn_a_log, m_dn_dt_bias, m_dn_norm_w, m_ssm_conv_w, m_ssm_conv_b, m_ssm_a_log, m_ssm_dt_bias, m_ssm_d, m_ssm_norm_w, m_w_branch, m_w_out, m_norm_mlp, m_w_up, m_w_down, m_norm_final, v_norm_mix, v_w_in, v_dn_conv_w, v_dn_a_log, v_dn_dt_bias, v_dn_norm_w, v_ssm_conv_w, v_ssm_conv_b, v_ssm_a_log, v_ssm_dt_bias, v_ssm_d, v_ssm_norm_w, v_w_branch, v_w_out, v_norm_mlp, v_w_up, v_w_down, v_norm_final):
    w = dict(norm_mix=norm_mix, w_in=w_in, dn_conv_w=dn_conv_w, dn_a_log=dn_a_log, dn_dt_bias=dn_dt_bias, dn_norm_w=dn_norm_w,
             ssm_conv_w=ssm_conv_w, ssm_conv_b=ssm_conv_b, ssm_a_log=ssm_a_log, ssm_dt_bias=ssm_dt_bias, ssm_d=ssm_d,
             ssm_norm_w=ssm_norm_w, w_branch=w_branch, w_out=w_out, norm_mlp=norm_mlp, w_up=w_up, w_down=w_down,
             norm_final=norm_final)
    m = dict(norm_mix=m_norm_mix, w_in=m_w_in, dn_conv_w=m_dn_conv_w, dn_a_log=m_dn_a_log, dn_dt_bias=m_dn_dt_bias,
             dn_norm_w=m_dn_norm_w, ssm_conv_w=m_ssm_conv_w, ssm_conv_b=m_ssm_conv_b, ssm_a_log=m_ssm_a_log,
             ssm_dt_bias=m_ssm_dt_bias, ssm_d=m_ssm_d, ssm_norm_w=m_ssm_norm_w, w_branch=m_w_branch, w_out=m_w_out,
             norm_mlp=m_norm_mlp, w_up=m_w_up, w_down=m_w_down, norm_final=m_norm_final)
    v = dict(norm_mix=v_norm_mix, w_in=v_w_in, dn_conv_w=v_dn_conv_w, dn_a_log=v_dn_a_log, dn_dt_bias=v_dn_dt_bias,
             dn_norm_w=v_dn_norm_w, ssm_conv_w=v_ssm_conv_w, ssm_conv_b=v_ssm_conv_b, ssm_a_log=v_ssm_a_log,
             ssm_dt_bias=v_ssm_dt_bias, ssm_d=v_ssm_d, ssm_norm_w=v_ssm_norm_w, w_branch=v_w_branch, w_out=v_w_out,
             norm_mlp=v_norm_mlp, w_up=v_w_up, w_down=v_w_down, norm_final=v_norm_final)
    loss, grad_x, out = _step(w, m, v, x[0], loss_target[0])
    return (loss, grad_x[None], *[out["grad"][n] for n in WEIGHTS], *[out["delta"][n] for n in WEIGHTS],
            *[out["new_m"][n] for n in WEIGHTS], *[out["new_v"][n] for n in WEIGHTS])
```

```python
import math

import jax
import jax.numpy as jnp
from jax import lax
from jax.experimental import pallas as pl
from jax.experimental.pallas import tpu as pltpu

F32 = jnp.float32
BF16 = jnp.bfloat16
MXU_DTYPE = BF16
HIGHEST = lax.Precision.HIGHEST

N_DEV = 8
DEPTH = 2
EPS = 1e-6
CONV_K = 4
DN_HEAD_DIM = 128
SB_HEAD_DIM = 64
SSM_HEAD_DIM = 64
SSM_STATE = 128
SSM_GROUPS = 4
CHUNK = 64
SB_BLOCK = 128
LANES = 128
ADAM_LR, ADAM_B1, ADAM_B2, ADAM_EPS, ADAM_WD, ADAM_STEP = 0.001, 0.9, 0.999, 1e-08, 0.01, 10
NEG_BIG = -1e30
DN_HEADS_PER_STEP = 8
SSD_GROUPS_PER_STEP = 1
SB_UNROLL = 4
SB_SPLIT = 2

ARB = "arbitrary"


def _cparams(n_axes):
    return pltpu.CompilerParams(dimension_semantics=(ARB,) * n_axes)


def _softplus(x):
    return jnp.maximum(x, 0.0) + jnp.log1p(jnp.exp(-jnp.abs(x)))


def _sigmoid(x):
    return jax.nn.sigmoid(x)


def _silu(x):
    return x * _sigmoid(x)


def _silu_and_grad(x):
    s = _sigmoid(x)
    return x * s, s * (1.0 + x * (1.0 - s))


def _dot(a, b, dims, prec=None):
    return lax.dot_general(a, b, (dims, ((), ())), precision=prec, preferred_element_type=F32)


NN = ((1,), (0,))
NT = ((1,), (1,))
TN = ((0,), (0,))


def _mxu_dot(a, b, dims):
    return _dot(a.astype(MXU_DTYPE), b.astype(MXU_DTYPE), dims)


def _single_pass_dot(dims):
    grads = {NN: (lambda a, b, ct: (_mxu_dot(ct, b, NT), _mxu_dot(a, ct, TN))),
             NT: (lambda a, b, ct: (_mxu_dot(ct, b, NN), _mxu_dot(ct, a, TN))),
             TN: (lambda a, b, ct: (_mxu_dot(b, ct, NT), _mxu_dot(a, ct, NN)))}[dims]

    @jax.custom_vjp
    def f(a, b):
        return _mxu_dot(a, b, dims)

    f.defvjp(lambda a, b: (_mxu_dot(a, b, dims), (a, b)), lambda res, ct: grads(*res, ct))
    return f


_SDOT = {dims: _single_pass_dot(dims) for dims in (NN, NT, TN)}


def _sdot(a, b, dims=NN):
    return _SDOT[dims](a, b)


def _split_dot(a, m_bf16, nsplit=3):
    out = None
    rem = a
    for _ in range(nsplit):
        piece = rem.astype(BF16)
        rem = rem - piece.astype(F32)
        term = _dot(piece, m_bf16, NN)
        out = term if out is None else out + term
    return out


def _pick(n, pref):
    for t in pref:
        if n % t == 0:
            return t
    return n


def _matmul(a, b, *, ta=False, tb=False, name, epilogue=None, extras=(), out_dtypes=(F32,), col_shards=1, into=None,
            a_cols=None, b_cols=None, tm=None, tn=None, tk=None):
    a_shape = a.shape if a_cols is None else (a.shape[0], a_cols[1])
    b_shape = b.shape if b_cols is None else (b.shape[0], b_cols[1])
    assert (a_cols is None or not ta) and (b_cols is None or not tb)
    m, k = (a_shape[1], a_shape[0]) if ta else a_shape
    k2, n = (b_shape[1], b_shape[0]) if tb else b_shape
    assert k == k2, (a.shape, b.shape, ta, tb)
    ncs = n // col_shards
    tm = tm or _pick(m, (1920, 1024, 512, 256, 128))
    tn = tn or _pick(ncs, (1920, 1024, 640, 512, 384, 256, 128))
    tk = tk or _pick(k, (1920, 1024, 640, 512, 256, 128))
    nk = k // tk
    a_off = 0 if a_cols is None else a_cols[0] // tk
    b_off = 0 if b_cols is None else b_cols[0] // tn
    assert (a_cols is None or a_cols[0] % tk == 0) and (b_cols is None or b_cols[0] % tn == 0)
    a_spec = (pl.BlockSpec((tk, tm), lambda i, j, kk: (kk, i)) if ta
              else pl.BlockSpec((tm, tk), lambda i, j, kk: (i, kk + a_off)))
    b_spec = (pl.BlockSpec((tn, tk), lambda i, j, kk: (j, kk)) if tb
              else pl.BlockSpec((tk, tn), lambda i, j, kk: (kk, j + b_off)))
    e_spec = pl.BlockSpec((tm, tn), lambda i, j, kk: (i, j))
    if into is not None:
        buf, col_off = into
        off = col_off // tn
        assert col_shards == 1 and len(out_dtypes) == 1 and col_off % tn == 0 and out_dtypes[0] == buf.dtype
        o_spec, o_shape = pl.BlockSpec((tm, tn), lambda i, j, kk: (i, off + j)), buf.shape
    elif col_shards == 1:
        o_spec, o_shape = e_spec, (m, n)
    else:
        per = ncs // tn
        o_spec, o_shape = pl.BlockSpec((None, tm, tn), lambda i, j, kk: (j // per, i, j % per)), (col_shards, m, ncs)
    dims = (((0,) if ta else (1,)), ((1,) if tb else (0,)))
    n_extra = len(extras)
    n_out = len(out_dtypes)
    n_into = 0 if into is None else 1

    def body(*refs):
        a_ref, b_ref = refs[0], refs[1]
        extra_refs = refs[2:2 + n_extra]
        out_refs = refs[2 + n_extra + n_into:2 + n_extra + n_into + n_out]
        acc_ref = refs[-1]
        kk = pl.program_id(2)

        @pl.when(kk == 0)
        def _():
            acc_ref[...] = jnp.zeros_like(acc_ref)

        acc_ref[...] += _dot(a_ref[...].astype(MXU_DTYPE), b_ref[...].astype(MXU_DTYPE), dims)

        @pl.when(kk == nk - 1)
        def _():
            acc = acc_ref[...]
            outs = (acc,) if epilogue is None else epilogue(acc, *[r[...] for r in extra_refs])
            for o_ref, o in zip(out_refs, outs):
                o_ref[...] = o.astype(o_ref.dtype)

    outs = pl.pallas_call(
        body,
        grid=(m // tm, n // tn, nk),
        in_specs=[a_spec, b_spec] + [e_spec] * n_extra + [ANY] * n_into,
        out_specs=[o_spec] * n_out,
        out_shape=[jax.ShapeDtypeStruct(o_shape, dt) for dt in out_dtypes],
        input_output_aliases={2 + n_extra: 0} if n_into else {},
        scratch_shapes=[pltpu.VMEM((tm, tn), F32)],
        compiler_params=pltpu.CompilerParams(dimension_semantics=("parallel", "parallel", ARB)),
        name=name,
    )(a, b, *extras, *([] if into is None else [into[0]]))
    return outs[0] if n_out == 1 else tuple(outs)


def _rms_fwd(x, w, *, name, tm=512):
    s, d = x.shape
    out_dtype = MXU_DTYPE

    def body(x_ref, w_ref, o_ref):
        xv = x_ref[...]
        r = lax.rsqrt(jnp.mean(xv * xv, axis=-1, keepdims=True) + EPS)
        o_ref[...] = (xv * r * w_ref[...]).astype(o_ref.dtype)

    return pl.pallas_call(
        body, grid=(s // tm,),
        in_specs=[pl.BlockSpec((tm, d), lambda i: (i, 0)), pl.BlockSpec((1, d), lambda i: (0, 0))],
        out_specs=pl.BlockSpec((tm, d), lambda i: (i, 0)),
        out_shape=jax.ShapeDtypeStruct((s, d), out_dtype),
        compiler_params=_cparams(1), name=name,
    )(x, w.reshape(1, d))


def _rms_bwd(x, w, dh, dres, *, name, tm=512):
    s, d = x.shape

    def body(x_ref, w_ref, dh_ref, dres_ref, dx_ref, dw_ref):
        xv = x_ref[...]
        r = lax.rsqrt(jnp.mean(xv * xv, axis=-1, keepdims=True) + EPS)
        xh = xv * r
        dhv = dh_ref[...].astype(F32)
        dxn = dhv * w_ref[...]
        dx = r * (dxn - xh * jnp.mean(dxn * xh, axis=-1, keepdims=True))
        dx_ref[...] = dres_ref[...] + dx

        @pl.when(pl.program_id(0) == 0)
        def _():
            dw_ref[...] = jnp.zeros_like(dw_ref)

        dw_ref[...] += jnp.sum(dhv * xh, axis=0, keepdims=True)

    dx, dw = pl.pallas_call(
        body, grid=(s // tm,),
        in_specs=[pl.BlockSpec((tm, d), lambda i: (i, 0)), pl.BlockSpec((1, d), lambda i: (0, 0)),
                  pl.BlockSpec((tm, d), lambda i: (i, 0)), pl.BlockSpec((tm, d), lambda i: (i, 0))],
        out_specs=[pl.BlockSpec((tm, d), lambda i: (i, 0)), pl.BlockSpec((1, d), lambda i: (0, 0))],
        out_shape=[jax.ShapeDtypeStruct((s, d), F32), jax.ShapeDtypeStruct((1, d), F32)],
        compiler_params=_cparams(1), name=name,
    )(x, w.reshape(1, d), dh, dres)
    return dx, dw.reshape(d)


def _final_loss(x, w, target, *, name, tm=512):
    s, d = x.shape

    def body(x_ref, w_ref, t_ref, loss_ref, dx_ref, dw_ref):
        xv = x_ref[...]
        r = lax.rsqrt(jnp.mean(xv * xv, axis=-1, keepdims=True) + EPS)
        xh = xv * r
        err = xh * w_ref[...] - t_ref[...]
        dy = err * (1.0 / d)
        dxn = dy * w_ref[...]
        dx_ref[...] = r * (dxn - xh * jnp.mean(dxn * xh, axis=-1, keepdims=True))

        @pl.when(pl.program_id(0) == 0)
        def _():
            dw_ref[...] = jnp.zeros_like(dw_ref)
            loss_ref[...] = jnp.zeros_like(loss_ref)

        dw_ref[...] += jnp.sum(dy * xh, axis=0, keepdims=True)
        row = jnp.sum(err * err, axis=1, keepdims=True) * (0.5 / d)
        loss_ref[...] += jnp.sum(row, axis=0, keepdims=True)

    loss, dx, dw = pl.pallas_call(
        body, grid=(s // tm,),
        in_specs=[pl.BlockSpec((tm, d), lambda i: (i, 0)), pl.BlockSpec((1, d), lambda i: (0, 0)),
                  pl.BlockSpec((tm, d), lambda i: (i, 0))],
        out_specs=[pl.BlockSpec((1, 1), lambda i: (0, 0)), pl.BlockSpec((tm, d), lambda i: (i, 0)),
                   pl.BlockSpec((1, d), lambda i: (0, 0))],
        out_shape=[jax.ShapeDtypeStruct((1, 1), F32), jax.ShapeDtypeStruct((s, d), F32), jax.ShapeDtypeStruct((1, d), F32)],
        compiler_params=_cparams(1), name=name,
    )(x, w.reshape(1, d), target)
    return loss[0, 0], dx, dw.reshape(d)


def _shift_down(x, sh, t_idx):
    return jnp.where(t_idx >= sh, pltpu.roll(x, sh, 0), 0.0)


def _shift_up(x, sh, t_idx, s):
    return jnp.where(t_idx < s - sh, pltpu.roll(x, s - sh, 0), 0.0)


def _conv_pre(x, w_rows, b, t_idx):
    c = w_rows[CONV_K - 1] * x + b
    for sh in range(1, CONV_K):
        c = c + w_rows[CONV_K - 1 - sh] * _shift_down(x, sh, t_idx)
    return c


def _conv_fwd(src, col0, w, b, n_l2, *, name):
    s = src.shape[0]
    c_tot = w.shape[1]
    nblk = c_tot // LANES

    def body(x_ref, w_ref, b_ref, o_ref):
        j = pl.program_id(0)
        t_idx = lax.broadcasted_iota(jnp.int32, (s, LANES), 0)
        w_rows = [w_ref[kk:kk + 1, :] for kk in range(CONV_K)]
        y = _silu(_conv_pre(x_ref[...], w_rows, b_ref[...], t_idx))
        if n_l2 > 0:
            yn = y * lax.rsqrt(jnp.sum(y * y, axis=1, keepdims=True) + EPS)
            y = jnp.where(j < n_l2, yn, y)
        o_ref[...] = y

    return pl.pallas_call(
        body, grid=(nblk,),
        in_specs=[pl.BlockSpec((s, LANES), lambda j: (0, col0 + j)), pl.BlockSpec((CONV_K, LANES), lambda j: (0, j)),
                  pl.BlockSpec((1, LANES), lambda j: (0, j))],
        out_specs=pl.BlockSpec((s, LANES), lambda j: (0, j)),
        out_shape=jax.ShapeDtypeStruct((s, c_tot), F32),
        compiler_params=_cparams(1), name=name,
    )(src, w, b)


def _conv_bwd(src, col0, w, b, n_l2, dout, into, *, name):
    s = src.shape[0]
    c_tot = w.shape[1]
    nblk = c_tot // LANES

    def body(x_ref, w_ref, b_ref, do_ref, into_ref, dx_ref, dw_ref, db_ref):
        j = pl.program_id(0)
        t_idx = lax.broadcasted_iota(jnp.int32, (s, LANES), 0)
        xv = x_ref[...]
        w_rows = [w_ref[kk:kk + 1, :] for kk in range(CONV_K)]
        c = _conv_pre(xv, w_rows, b_ref[...], t_idx)
        dy = do_ref[...]
        y, y_grad = _silu_and_grad(c)
        if n_l2 > 0:
            r = lax.rsqrt(jnp.sum(y * y, axis=1, keepdims=True) + EPS)
            dyn = r * dy - y * (r * r * r) * jnp.sum(dy * y, axis=1, keepdims=True)
            dy = jnp.where(j < n_l2, dyn, dy)
        dc = dy * y_grad
        dx = w_rows[CONV_K - 1] * dc
        rows = [None] * CONV_K
        rows[CONV_K - 1] = jnp.sum(dc * xv, axis=0, keepdims=True)
        for sh in range(1, CONV_K):
            dx = dx + w_rows[CONV_K - 1 - sh] * _shift_up(dc, sh, t_idx, s)
            rows[CONV_K - 1 - sh] = jnp.sum(dc * _shift_down(xv, sh, t_idx), axis=0, keepdims=True)
        dx_ref[...] = dx.astype(dx_ref.dtype)
        for kk in range(CONV_K):
            dw_ref[kk:kk + 1, :] = rows[kk]
        db_ref[...] = jnp.sum(dc, axis=0, keepdims=True)

    return pl.pallas_call(
        body, grid=(nblk,),
        in_specs=[pl.BlockSpec((s, LANES), lambda j: (0, col0 + j)), pl.BlockSpec((CONV_K, LANES), lambda j: (0, j)),
                  pl.BlockSpec((1, LANES), lambda j: (0, j)), pl.BlockSpec((s, LANES), lambda j: (0, j)), ANY],
        out_specs=[pl.BlockSpec((s, LANES), lambda j: (0, col0 + j)), pl.BlockSpec((CONV_K, LANES), lambda j: (0, j)),
                   pl.BlockSpec((1, LANES), lambda j: (0, j))],
        out_shape=[jax.ShapeDtypeStruct(into.shape, into.dtype), jax.ShapeDtypeStruct((CONV_K, c_tot), F32),
                   jax.ShapeDtypeStruct((1, c_tot), F32)],
        input_output_aliases={4: 0},
        compiler_params=_cparams(1), name=name,
    )(src, w, b, dout, into)


def _chunk_masks(c):
    ii = lax.broadcasted_iota(jnp.int32, (c, c), 0)
    jj = lax.broadcasted_iota(jnp.int32, (c, c), 1)
    return ii, jj


def _row_to_col(row, eye):
    return jnp.sum(jnp.where(eye, row, 0.0), axis=1, keepdims=True)


def _each(f, *lists):
    return [f(*xs) for xs in zip(*lists)]


@jax.custom_vjp
def _nilpotent_inverse(nmats):
    c = nmats[0].shape[0]
    ii, jj = _chunk_masks(c)
    xinv = _each(lambda n: jnp.where(ii == jj, 1.0, 0.0) + n, nmats)
    pw = nmats
    for _ in range(int(math.log2(c)) - 1):
        pw = _each(lambda p: _dot(p, p, NN, HIGHEST), pw)
        xinv = _each(lambda x, p: x + _dot(x, p, NN, HIGHEST), xinv, pw)
    return xinv


def _nilpotent_inverse_fwd(nmats):
    xinv = _nilpotent_inverse(nmats)
    return xinv, xinv


def _nilpotent_inverse_bwd(xinv, cts):
    left = _each(lambda x, ct: _dot(x, ct, TN, HIGHEST), xinv, cts)
    return (_each(lambda l_, x: _dot(l_, x, NT, HIGHEST), left, xinv),)


_nilpotent_inverse.defvjp(_nilpotent_inverse_fwd, _nilpotent_inverse_bwd)


@jax.custom_vjp
def _saved_inverse(nmats, saved):
    return saved


def _saved_inverse_fwd(nmats, saved):
    return saved, saved


def _saved_inverse_bwd(xinv, cts):
    return _nilpotent_inverse_bwd(xinv, cts) + (_each(jnp.zeros_like, xinv),)


_saved_inverse.defvjp(_saved_inverse_fwd, _saved_inverse_bwd)


def _dn_chunk(q, k, v, a_row, b_row, alog, dtb, s0, saved_inverse=None):
    c = q[0].shape[0]
    ii, jj = _chunk_masks(c)
    causal, strict, eye = ii >= jj, ii > jj, ii == jj
    g_row = _each(lambda al, a, dt: -jnp.exp(al) * _softplus(a + dt), alog, a_row, dtb)
    beta_col = _each(lambda b: _row_to_col(_sigmoid(b), eye), b_row)
    g_col = _each(lambda g: _row_to_col(g, eye), g_row)
    gc_col = _each(lambda g: jnp.sum(jnp.where(causal, g, 0.0), axis=1, keepdims=True), g_row)
    gc_row = _each(lambda g: jnp.sum(jnp.where(jj >= ii, g, 0.0), axis=0, keepdims=True), g_col)
    decay = _each(lambda gc, gr: jnp.exp(jnp.where(causal, gc - gr, NEG_BIG)), gc_col, gc_row)
    kb = _each(jnp.multiply, k, beta_col)
    vb = _each(jnp.multiply, v, beta_col)
    nmat = _each(lambda kb_, k_, dc: -jnp.where(strict, _sdot(kb_, k_, NT) * dc, 0.0), kb, k, decay)
    xinv = _nilpotent_inverse(nmat) if saved_inverse is None else _saved_inverse(nmat, saved_inverse)
    egc = _each(jnp.exp, gc_col)
    dv = v[0].shape[1]
    uw = _each(lambda x, vb_, kb_, e: _dot(x, jnp.concatenate([vb_, kb_ * e], axis=1), NN, HIGHEST), xinv, vb, kb, egc)
    u = _each(lambda t: t[:, :dv], uw)
    w = _each(lambda t: t[:, dv:], uw)
    qs = _each(lambda q_: q_ * (q_.shape[1] ** -0.5), q)
    attn = _each(lambda q_, k_, dc: _sdot(q_, k_, NT) * dc, qs, k, decay)
    gl = _each(lambda g: jnp.sum(g, axis=1, keepdims=True), g_row)
    kd = _each(lambda k_, gl_, gc: k_ * jnp.exp(gl_ - gc), k, gl, gc_col)
    v_new = _each(lambda u_, w_, s: u_ - _sdot(w_, s), u, w, s0)
    o = _each(lambda q_, e, s, at, vn: _sdot(q_ * e, s) + _sdot(at, vn), qs, egc, s0, attn, v_new)
    s1 = _each(lambda s, gl_, kd_, vn: s * jnp.exp(gl_) + _sdot(kd_, vn, TN), s0, gl, kd, v_new)
    return (o, s1), xinv


def _dn_specs(nh, nc, hb, rev):
    n_of = (lambda n: nc - 1 - n) if rev else (lambda n: n)
    ng = nh // hb
    qkv = [pl.BlockSpec((CHUNK, hb * DN_HEAD_DIM), (lambda h, n, o=o: (n_of(n), o * ng + h))) for o in range(3)]
    row = pl.BlockSpec((hb, None, 1, CHUNK), lambda h, n: (h, n_of(n), 0, 0))
    scal = pl.BlockSpec((hb, 1, 1), lambda h, n: (h, 0, 0))
    o_spec = pl.BlockSpec((CHUNK, hb * DN_HEAD_DIM), lambda h, n: (n_of(n), h))
    st = pl.BlockSpec((hb, None, DN_HEAD_DIM, DN_HEAD_DIM), lambda h, n: (h, n_of(n), 0, 0))
    inv = pl.BlockSpec((hb, None, CHUNK, CHUNK), lambda h, n: (h, n_of(n), 0, 0))
    return qkv, row, scal, o_spec, st, inv


def _dn_fwd(qkv, a_rows, b_rows, alog, dtb, *, name):
    s = qkv.shape[0]
    nh, nc = a_rows.shape[0], a_rows.shape[1]
    hb = min(DN_HEADS_PER_STEP, nh)
    qkv_specs, row, scal, o_spec, st, inv = _dn_specs(nh, nc, hb, False)
    hd = DN_HEAD_DIM

    def body(q_ref, k_ref, v_ref, a_ref, b_ref, al_ref, dt_ref, o_ref, st_ref, inv_ref, state):
        @pl.when(pl.program_id(1) == 0)
        def _():
            state[...] = jnp.zeros_like(state)

        cols = [slice(h * hd, (h + 1) * hd) for h in range(hb)]
        s0 = [state[h] for h in range(hb)]
        for h in range(hb):
            st_ref[h] = s0[h]
        (o, s1), xinv = _dn_chunk(
            [q_ref[:, cl] for cl in cols], [k_ref[:, cl] for cl in cols], [v_ref[:, cl] for cl in cols],
            [a_ref[h] for h in range(hb)], [b_ref[h] for h in range(hb)],
            [al_ref[h] for h in range(hb)], [dt_ref[h] for h in range(hb)], s0)
        for h in range(hb):
            o_ref[:, cols[h]] = o[h]
            inv_ref[h] = xinv[h]
            state[h] = s1[h]

    return pl.pallas_call(
        body, grid=(nh // hb, nc),
        in_specs=qkv_specs + [row, row, scal, scal],
        out_specs=[o_spec, st, inv],
        out_shape=[jax.ShapeDtypeStruct((s, nh * hd), F32), jax.ShapeDtypeStruct((nh, nc, hd, hd), F32),
                   jax.ShapeDtypeStruct((nh, nc, CHUNK, CHUNK), F32)],
        scratch_shapes=[pltpu.VMEM((hb, hd, hd), F32)],
        compiler_params=_cparams(2), name=name,
    )(qkv, qkv, qkv, a_rows, b_rows, alog, dtb)


def _dn_bwd(qkv, a_rows, b_rows, alog, dtb, states, inverses, do, *, name):
    s = qkv.shape[0]
    nh, nc = a_rows.shape[0], a_rows.shape[1]
    hb = min(DN_HEADS_PER_STEP, nh)
    qkv_specs, row, scal, o_spec, st, inv = _dn_specs(nh, nc, hb, True)
    hd = DN_HEAD_DIM

    assert hb == nh, "dq | dk | dv are written as one [S, 3W] array: all heads in one grid step"
    w = nh * hd

    def body(q_ref, k_ref, v_ref, a_ref, b_ref, al_ref, dt_ref, st_ref, inv_ref, do_ref,
             dqkv_ref, da_ref, db_ref, dal_ref, ddt_ref, dstate):
        @pl.when(pl.program_id(1) == 0)
        def _():
            dstate[...] = jnp.zeros_like(dstate)
            dal_ref[...] = jnp.zeros_like(dal_ref)
            ddt_ref[...] = jnp.zeros_like(ddt_ref)

        cols = [slice(h * hd, (h + 1) * hd) for h in range(hb)]
        heads = range(hb)
        args = ([q_ref[:, cl] for cl in cols], [k_ref[:, cl] for cl in cols], [v_ref[:, cl] for cl in cols],
                [a_ref[h] for h in heads], [b_ref[h] for h in heads], [al_ref[h] for h in heads],
                [dt_ref[h] for h in heads], [st_ref[h] for h in heads])
        saved = [inv_ref[h] for h in heads]
        _, vjp, _ = jax.vjp(lambda *a: _dn_chunk(*a, saved_inverse=saved), *args, has_aux=True)
        dq, dk, dv, da, db, dal, ddt, ds0 = vjp(([do_ref[:, cl] for cl in cols], [dstate[h] for h in heads]))
        for h in heads:
            dqkv_ref[:, h * hd:(h + 1) * hd] = dq[h]
            dqkv_ref[:, w + h * hd:w + (h + 1) * hd] = dk[h]
            dqkv_ref[:, 2 * w + h * hd:2 * w + (h + 1) * hd] = dv[h]
            da_ref[h] = da[h]
            db_ref[h] = db[h]
            dal_ref[h] += dal[h]
            ddt_ref[h] += ddt[h]
            dstate[h] = ds0[h]

    n_of = lambda n: nc - 1 - n
    outs = pl.pallas_call(
        body, grid=(nh // hb, nc),
        in_specs=qkv_specs + [row, row, scal, scal, st, inv, o_spec],
        out_specs=[pl.BlockSpec((CHUNK, 3 * w), lambda h, n: (n_of(n), 0)), row, row, scal, scal],
        out_shape=[jax.ShapeDtypeStruct((s, 3 * w), F32)]
        + [jax.ShapeDtypeStruct(a_rows.shape, F32)] * 2 + [jax.ShapeDtypeStruct((nh, 1, 1), F32)] * 2,
        scratch_shapes=[pltpu.VMEM((hb, hd, hd), F32)],
        compiler_params=_cparams(2), name=name,
    )(qkv, qkv, qkv, a_rows, b_rows, alog, dtb, states, inverses, do)
    return outs


def _dn_post_fwd(o, src, gate_col0, nw, *, name, tm=512):
    s, w = o.shape
    hd = DN_HEAD_DIM
    gc = gate_col0 * LANES // w

    def body(o_ref, g_ref, w_ref, y_ref):
        for h in range(w // hd):
            cols = slice(h * hd, (h + 1) * hd)
            ov = o_ref[:, cols]
            r = lax.rsqrt(jnp.mean(ov * ov, axis=-1, keepdims=True) + EPS)
            y_ref[:, cols] = (ov * r * w_ref[...] * _silu(g_ref[:, cols])).astype(y_ref.dtype)

    blk = pl.BlockSpec((tm, w), lambda i: (i, 0))
    return pl.pallas_call(
        body, grid=(s // tm,),
        in_specs=[blk, pl.BlockSpec((tm, w), lambda i: (i, gc)), pl.BlockSpec((1, hd), lambda i: (0, 0))],
        out_specs=blk, out_shape=jax.ShapeDtypeStruct((s, w), MXU_DTYPE),
        compiler_params=_cparams(1), name=name,
    )(o, src, nw.reshape(1, hd))


def _dn_post_bwd(o, src, gate_col0, nw, dy, into, *, name, tm=512):
    s, w = o.shape
    hd = DN_HEAD_DIM
    gc = gate_col0 * LANES // w

    def body(o_ref, g_ref, w_ref, dy_ref, into_ref, do_ref, dg_ref, dw_ref):
        @pl.when(pl.program_id(0) == 0)
        def _():
            dw_ref[...] = jnp.zeros_like(dw_ref)

        dw = jnp.zeros((1, hd), F32)
        for h in range(w // hd):
            cols = slice(h * hd, (h + 1) * hd)
            ov, gv, dyv = o_ref[:, cols], g_ref[:, cols], dy_ref[:, cols]
            r = lax.rsqrt(jnp.mean(ov * ov, axis=-1, keepdims=True) + EPS)
            oh = ov * r
            sg, sg_grad = _silu_and_grad(gv)
            dn = dyv * sg
            dg_ref[:, cols] = (dyv * (oh * w_ref[...]) * sg_grad).astype(dg_ref.dtype)
            don = dn * w_ref[...]
            do_ref[:, cols] = r * (don - oh * jnp.mean(don * oh, axis=-1, keepdims=True))
            dw = dw + jnp.sum(dn * oh, axis=0, keepdims=True)
        dw_ref[...] += dw

    blk = pl.BlockSpec((tm, w), lambda i: (i, 0))
    wspec = pl.BlockSpec((1, hd), lambda i: (0, 0))
    gate_blk = pl.BlockSpec((tm, w), lambda i: (i, gc))
    do, dg, dw = pl.pallas_call(
        body, grid=(s // tm,),
        in_specs=[blk, gate_blk, wspec, blk, ANY],
        out_specs=[blk, gate_blk, wspec],
        out_shape=[jax.ShapeDtypeStruct((s, w), F32), jax.ShapeDtypeStruct(into.shape, into.dtype),
                   jax.ShapeDtypeStruct((1, hd), F32)],
        input_output_aliases={4: 1},
        compiler_params=_cparams(1), name=name,
    )(o, src, nw.reshape(1, hd), dy, into)
    return do, dg, dw.reshape(hd)


def _sb_consts():
    r2 = lax.broadcasted_iota(jnp.int32, (2 * SB_BLOCK, SB_BLOCK), 0)
    c2 = lax.broadcasted_iota(jnp.int32, (2 * SB_BLOCK, SB_BLOCK), 1)
    r = lax.broadcasted_iota(jnp.int32, (SB_BLOCK, SB_BLOCK), 0)
    c = lax.broadcasted_iota(jnp.int32, (SB_BLOCK, SB_BLOCK), 1)
    lm0 = c < SB_HEAD_DIM
    m_gt = jnp.where(r > c, 1.0, 0.0).astype(BF16)
    m_lt = jnp.where(r < c, 1.0, 0.0).astype(BF16)
    return r2, c2, lm0, m_gt, m_lt


def _sb_stack(x, lm0):
    return jnp.concatenate([jnp.where(lm0, x, 0.0), jnp.where(lm0, 0.0, x)], axis=0)


def _sb_unstack(x2, lm0):
    return jnp.where(lm0, x2[:SB_BLOCK], x2[SB_BLOCK:])


def _sb_fwd(src, col0, width, *, name):
    s = src.shape[0]
    nq = s // SB_BLOCK
    npair = width // LANES
    scale = SB_HEAD_DIM ** -0.5
    nu = math.gcd(SB_UNROLL, nq)

    def body(q_ref, k_ref, v_ref, o_ref, w_hbm, stage, sems):
        p, i = pl.program_id(0), pl.program_id(1)
        r2, c2, lm0, m_gt, _ = _sb_consts()
        t_glob = i * SB_BLOCK + (r2 & (SB_BLOCK - 1))
        q2 = (_sb_stack(q_ref[...], lm0) * scale).astype(MXU_DTYPE)

        t = p * nq + i
        half = t % 2
        ngrp = nq // nu

        def save(half_, grp, pp, ii):
            return pltpu.make_async_copy(stage.at[half_, grp], w_hbm.at[pp, ii, grp], sems.at[half_, grp])

        def drain(half_, pp, ii):
            for grp in range(ngrp):
                @pl.when(grp <= ii // nu)
                def _():
                    save(half_, grp, pp, ii).wait()

        def group(base, carry, masked):
            o2, rsum = carry
            js = [base + nu - 1 - u for u in range(nu)]
            offs = [pl.multiple_of(j * SB_BLOCK, SB_BLOCK) for j in js]
            zs = [_dot(q2, k_ref[pl.ds(off, SB_BLOCK), :].astype(MXU_DTYPE), NT) for off in offs]
            ts = [jnp.log(1.0 + jnp.exp(-jnp.abs(z))) for z in zs]
            lks = [-(jnp.maximum(z, 0.0) + t) for z, t in zip(zs, ts)]
            if masked:
                masks = [(j * SB_BLOCK + c2) < t_glob for j in js]
                lks = [jnp.where(mk, lk, 0.0) for mk, lk in zip(masks, lks)]
            sufs = [_split_dot(lk, m_gt, SB_SPLIT) for lk in lks]
            rs = [rsum]
            for lk in lks:
                rs.append(rs[-1] + jnp.sum(lk, axis=1, keepdims=True))
            wgts = [jnp.exp((jnp.minimum(z, 0.0) - t) + r_ + sf) for z, t, r_, sf in zip(zs, ts, rs, sufs)]
            if masked:
                wgts = [jnp.where(mk, wg, 0.0) for mk, wg in zip(masks, wgts)]
            wbs = [wg.astype(MXU_DTYPE) for wg in wgts]
            grp = base // nu
            for u, wb in enumerate(wbs):
                stage[half, grp, nu - 1 - u] = wb
            save(half, grp, p, i).start()
            for off, wb in zip(offs, wbs):
                o2 = o2 + _dot(wb, v_ref[pl.ds(off, SB_BLOCK), :].astype(MXU_DTYPE), NN)
            return o2, rs[-1]

        top0 = (i // nu) * nu
        last = i // nu
        carry = group(top0, (jnp.zeros((2 * SB_BLOCK, LANES), F32), jnp.zeros((2 * SB_BLOCK, 1), F32)), True)
        o2, _ = lax.fori_loop(1, last + 1, lambda g, cr: group(top0 - nu * g, cr, False), carry)
        o_ref[...] = _sb_unstack(o2, lm0)

        @pl.when(t >= 1)
        def _():
            drain(1 - half, (t - 1) // nq, (t - 1) % nq)

        @pl.when(t == npair * nq - 1)
        def _():
            drain(half, p, i)

    blk = pl.BlockSpec((SB_BLOCK, LANES), lambda p, i: (i, p))
    return pl.pallas_call(
        body, grid=(npair, nq),
        in_specs=[pl.BlockSpec((SB_BLOCK, LANES), lambda p, i: (i, col0 + p)),
                  pl.BlockSpec((s, LANES), lambda p, i: (0, col0 + npair + p)),
                  pl.BlockSpec((s, LANES), lambda p, i: (0, col0 + 2 * npair + p))],
        out_specs=[blk, ANY],
        out_shape=[jax.ShapeDtypeStruct((s, width), F32),
                   jax.ShapeDtypeStruct((npair, nq, nq // nu, nu, 2 * SB_BLOCK, LANES), MXU_DTYPE)],
        scratch_shapes=[pltpu.VMEM((2, nq // nu, nu, 2 * SB_BLOCK, LANES), MXU_DTYPE),
                        pltpu.SemaphoreType.DMA((2, nq // nu))],
        compiler_params=_cparams(2), name=name,
    )(src, src, src)


def _sb_bwd(src, col0, width, weights, do, *, name):
    s = src.shape[0]
    nq = s // SB_BLOCK
    npair = width // LANES
    scale = SB_HEAD_DIM ** -0.5
    nu = math.gcd(SB_UNROLL, nq)

    def body(q_ref, k_ref, v_ref, w_hbm, do_ref, dq_ref, dk_ref, dv_ref, stage, sems):
        p, i = pl.program_id(0), pl.program_id(1)

        @pl.when(i == 0)
        def _():
            dk_ref[...] = jnp.zeros_like(dk_ref)
            dv_ref[...] = jnp.zeros_like(dv_ref)

        r2, c2, lm0, _, m_lt = _sb_consts()
        t_glob = i * SB_BLOCK + (r2 & (SB_BLOCK - 1))
        q2 = (_sb_stack(q_ref[...], lm0) * scale).astype(MXU_DTYPE)
        do2 = _sb_stack(do_ref[...], lm0).astype(MXU_DTYPE)

        ngrp = nq // nu

        def load(half_, grp, pp, ii):
            return pltpu.make_async_copy(w_hbm.at[pp, ii, grp], stage.at[half_, grp], sems.at[half_, grp])

        def fetch_step(half_, pp, ii):
            for grp in range(ngrp):
                @pl.when(grp <= ii // nu)
                def _():
                    load(half_, grp, pp, ii).start()

        def group(g, carry, masked, slot):
            dq2, csum = carry
            js = [nu * g + u for u in range(nu)]
            offs = [pl.multiple_of(j * SB_BLOCK, SB_BLOCK) for j in js]
            kbs = [k_ref[pl.ds(off, SB_BLOCK), :].astype(MXU_DTYPE) for off in offs]
            zs = [_dot(q2, kb, NT) for kb in kbs]
            dws = [_dot(do2, v_ref[pl.ds(off, SB_BLOCK), :].astype(MXU_DTYPE), NT) for off in offs]
            wbs = [stage[slot[0], slot[1], u] for u in range(nu)]
            sigs = [_sigmoid(z) for z in zs]
            dlogas = [wb.astype(F32) * dw for wb, dw in zip(wbs, dws)]
            pres = [_split_dot(dl, m_lt, 1) for dl in dlogas]
            dlks = []
            for dl, pre in zip(dlogas, pres):
                dlks.append(csum + pre)
                csum = csum + jnp.sum(dl, axis=1, keepdims=True)
            if masked:
                dlks = [jnp.where((j * SB_BLOCK + c2) < t_glob, dlk, 0.0) for j, dlk in zip(js, dlks)]
            dzbs = [(dl * (1.0 - sg) - dlk * sg).astype(MXU_DTYPE) for dl, sg, dlk in zip(dlogas, sigs, dlks)]
            for off, dzb, wb, kb in zip(offs, dzbs, wbs, kbs):
                dk_ref[pl.ds(off, SB_BLOCK), :] += _dot(dzb, q2, TN)
                dv_ref[pl.ds(off, SB_BLOCK), :] += _dot(wb, do2, TN)
                dq2 = dq2 + _dot(dzb, kb, NN)
            return dq2, csum

        t = p * nq + i
        half = t % 2

        @pl.when(t == 0)
        def _():
            fetch_step(0, p, i)

        @pl.when(t + 1 < npair * nq)
        def _():
            fetch_step(1 - half, (t + 1) // nq, (t + 1) % nq)

        def step(g, carry):
            load(half, g, p, i).wait()
            return group(g, carry, False, (half, g))

        last = i // nu
        carry = lax.fori_loop(0, last, step, (jnp.zeros((2 * SB_BLOCK, LANES), F32), jnp.zeros((2 * SB_BLOCK, 1), F32)))
        load(half, last, p, i).wait()
        dq2, _ = group(last, carry, True, (half, last))
        dq_ref[...] = _sb_unstack(dq2, lm0) * scale

    blk = pl.BlockSpec((SB_BLOCK, LANES), lambda p, i: (i, p))
    full = pl.BlockSpec((s, LANES), lambda p, i: (0, p))
    return pl.pallas_call(
        body, grid=(npair, nq),
        in_specs=[pl.BlockSpec((SB_BLOCK, LANES), lambda p, i: (i, col0 + p)),
                  pl.BlockSpec((s, LANES), lambda p, i: (0, col0 + npair + p)),
                  pl.BlockSpec((s, LANES), lambda p, i: (0, col0 + 2 * npair + p)),
                  ANY, blk],
        out_specs=[blk, full, full],
        out_shape=[jax.ShapeDtypeStruct((s, width), F32)] * 3,
        scratch_shapes=[pltpu.VMEM((2, nq // nu, nu, 2 * SB_BLOCK, LANES), MXU_DTYPE),
                        pltpu.SemaphoreType.DMA((2, nq // nu))],
        compiler_params=_cparams(2), name=name,
    )(src, src, src, weights, do)


def _ssd_group(xs, dt_rows, alogs, dtbs, bms, cms, h0s):
    c = bms[0].shape[0]
    per = len(xs) // len(bms)
    ii, jj = _chunk_masks(c)
    causal, eye = ii >= jj, ii == jj
    scores = [t for t in _each(lambda c_, b_: _sdot(c_, b_, NT), cms, bms) for _ in range(per)]
    dt_r = _each(lambda dt, b: _softplus(dt + b), dt_rows, dtbs)
    a_r = _each(lambda al, dt: -jnp.exp(al) * dt, alogs, dt_r)
    dt_col = _each(lambda dt: _row_to_col(dt, eye), dt_r)
    a_col = _each(lambda a: _row_to_col(a, eye), a_r)
    ac_col = _each(lambda a: jnp.sum(jnp.where(causal, a, 0.0), axis=1, keepdims=True), a_r)
    ac_row = _each(lambda a: jnp.sum(jnp.where(jj >= ii, a, 0.0), axis=0, keepdims=True), a_col)
    lmat = _each(lambda c_, r_: jnp.exp(jnp.where(causal, c_ - r_, NEG_BIG)), ac_col, ac_row)
    xdt = _each(jnp.multiply, xs, dt_col)
    al = _each(lambda a: jnp.sum(a, axis=1, keepdims=True), a_r)
    bm = [t for t in bms for _ in range(per)]
    cm = [t for t in cms for _ in range(per)]
    ys = _each(lambda sc, lm, xd, cm_, h0, ac: _sdot(sc * lm, xd) + _sdot(cm_, h0, NT) * jnp.exp(ac),
               scores, lmat, xdt, cm, h0s, ac_col)
    h1s = _each(lambda h0, al_, xd, ac, bm_: h0 * jnp.exp(al_) + _sdot(xd * jnp.exp(al_ - ac), bm_, TN),
                h0s, al, xdt, ac_col, bm)
    return ys, h1s


def _ssd_specs(ng, nc, r, gb, rev):
    n_of = (lambda n: nc - 1 - n) if rev else (lambda n: n)
    xw, bw = gb * r * SSM_HEAD_DIM, gb * SSM_STATE
    b0, c0 = (ng * r * SSM_HEAD_DIM) // bw, (ng * r * SSM_HEAD_DIM + ng * SSM_STATE) // bw
    x_spec = pl.BlockSpec((CHUNK, xw), lambda g, n: (n_of(n), g))
    b_spec = pl.BlockSpec((CHUNK, bw), lambda g, n: (n_of(n), b0 + g))
    c_spec = pl.BlockSpec((CHUNK, bw), lambda g, n: (n_of(n), c0 + g))
    dt_spec = pl.BlockSpec((gb, None, r, CHUNK), lambda g, n: (g, n_of(n), 0, 0))
    sc_spec = pl.BlockSpec((gb, r, 1), lambda g, n: (g, 0, 0))
    st_spec = pl.BlockSpec((gb, None, r, SSM_HEAD_DIM, SSM_STATE), lambda g, n: (g, n_of(n), 0, 0, 0))
    bc_out = pl.BlockSpec((CHUNK, bw), lambda g, n: (n_of(n), g))
    return x_spec, b_spec, c_spec, dt_spec, sc_spec, st_spec, x_spec, bc_out


def _ssd_refs(gb, r, x_ref, b_ref, c_ref, dt_ref, al_ref, db_ref):
    p, n = SSM_HEAD_DIM, SSM_STATE
    heads = [(g, h) for g in range(gb) for h in range(r)]
    xs = [x_ref[:, (g * r + h) * p:(g * r + h + 1) * p] for g, h in heads]
    dts = [dt_ref[g, h:h + 1, :] for g, h in heads]
    als = [al_ref[g, h:h + 1, :] for g, h in heads]
    dbs = [db_ref[g, h:h + 1, :] for g, h in heads]
    bms = [b_ref[:, g * n:(g + 1) * n] for g in range(gb)]
    cms = [c_ref[:, g * n:(g + 1) * n] for g in range(gb)]
    return heads, xs, dts, als, dbs, bms, cms


def _ssd_fwd(xbc, dt_rows, alog, dtb, *, name):
    s = xbc.shape[0]
    ng, nc, r = dt_rows.shape[0], dt_rows.shape[1], dt_rows.shape[2]
    w = ng * r * SSM_HEAD_DIM
    gb = math.gcd(SSD_GROUPS_PER_STEP, ng)
    x_spec, b_spec, c_spec, dt_spec, sc_spec, st_spec, y_spec, _ = _ssd_specs(ng, nc, r, gb, False)
    p = SSM_HEAD_DIM

    def body(x_ref, b_ref, c_ref, dt_ref, al_ref, db_ref, y_ref, st_ref, state):
        @pl.when(pl.program_id(1) == 0)
        def _():
            state[...] = jnp.zeros_like(state)

        st_ref[...] = state[...]
        heads, xs, dts, als, dbs, bms, cms = _ssd_refs(gb, r, x_ref, b_ref, c_ref, dt_ref, al_ref, db_ref)
        ys, h1s = _ssd_group(xs, dts, als, dbs, bms, cms, [state[g, h] for g, h in heads])
        for i, (g, h) in enumerate(heads):
            y_ref[:, (g * r + h) * p:(g * r + h + 1) * p] = ys[i]
            state[g, h] = h1s[i]

    return pl.pallas_call(
        body, grid=(ng // gb, nc),
        in_specs=[x_spec, b_spec, c_spec, dt_spec, sc_spec, sc_spec],
        out_specs=[y_spec, st_spec],
        out_shape=[jax.ShapeDtypeStruct((s, w), F32), jax.ShapeDtypeStruct((ng, nc, r, p, SSM_STATE), F32)],
        scratch_shapes=[pltpu.VMEM((gb, r, p, SSM_STATE), F32)],
        compiler_params=_cparams(2), name=name,
    )(xbc, xbc, xbc, dt_rows, alog, dtb)


def _ssd_bwd(xbc, dt_rows, alog, dtb, states, dy, dx_extra, *, name):
    s = xbc.shape[0]
    ng, nc, r = dt_rows.shape[0], dt_rows.shape[1], dt_rows.shape[2]
    w = ng * r * SSM_HEAD_DIM
    gb = math.gcd(SSD_GROUPS_PER_STEP, ng)
    x_spec, b_spec, c_spec, dt_spec, sc_spec, st_spec, y_spec, bc_out = _ssd_specs(ng, nc, r, gb, True)
    p = SSM_HEAD_DIM

    def body(x_ref, b_ref, c_ref, dt_ref, al_ref, db_ref, st_ref, dy_ref, dxe_ref,
             dx_ref, dbm_ref, dcm_ref, ddt_ref, dal_ref, ddb_ref, dstate):
        @pl.when(pl.program_id(1) == 0)
        def _():
            dstate[...] = jnp.zeros_like(dstate)
            dal_ref[...] = jnp.zeros_like(dal_ref)
            ddb_ref[...] = jnp.zeros_like(ddb_ref)

        heads, xs, dts, als, dbs, bms, cms = _ssd_refs(gb, r, x_ref, b_ref, c_ref, dt_ref, al_ref, db_ref)
        _, vjp = jax.vjp(_ssd_group, xs, dts, als, dbs, bms, cms, [st_ref[g, h] for g, h in heads])
        dys = [dy_ref[:, (g * r + h) * p:(g * r + h + 1) * p] for g, h in heads]
        dxs, ddts, dals, ddbs, dbms, dcms, dh0s = vjp((dys, [dstate[g, h] for g, h in heads]))
        for g in range(gb):
            dbm_ref[:, g * SSM_STATE:(g + 1) * SSM_STATE] = dbms[g]
            dcm_ref[:, g * SSM_STATE:(g + 1) * SSM_STATE] = dcms[g]
        for i, (g, h) in enumerate(heads):
            dx_ref[:, (g * r + h) * p:(g * r + h + 1) * p] = dxs[i] + dxe_ref[:, (g * r + h) * p:(g * r + h + 1) * p]
            ddt_ref[g, h:h + 1, :] = ddts[i]
            dal_ref[g, h:h + 1, :] += dals[i]
            ddb_ref[g, h:h + 1, :] += ddbs[i]
            dstate[g, h] = dh0s[i]

    gn = ng * SSM_STATE
    return pl.pallas_call(
        body, grid=(ng // gb, nc),
        in_specs=[x_spec, b_spec, c_spec, dt_spec, sc_spec, sc_spec, st_spec, y_spec, y_spec],
        out_specs=[y_spec, bc_out, bc_out, dt_spec, sc_spec, sc_spec],
        out_shape=[jax.ShapeDtypeStruct((s, w), F32), jax.ShapeDtypeStruct((s, gn), F32), jax.ShapeDtypeStruct((s, gn), F32),
                   jax.ShapeDtypeStruct(dt_rows.shape, F32), jax.ShapeDtypeStruct((ng, r, 1), F32),
                   jax.ShapeDtypeStruct((ng, r, 1), F32)],
        scratch_shapes=[pltpu.VMEM((gb, r, p, SSM_STATE), F32)],
        compiler_params=_cparams(2), name=name,
    )(xbc, xbc, xbc, dt_rows, alog, dtb, states, dy, dx_extra)


def _ssm_post_fwd(y, xbc, src, z_col0, dexp, nw, *, name, tm=512):
    s, w = y.shape
    gw = w // SSM_GROUPS
    zc = z_col0 * LANES // gw

    def body(y_ref, x_ref, z_ref, d_ref, w_ref, o_ref):
        yy = (y_ref[...] + x_ref[...] * d_ref[...]) * _silu(z_ref[...])
        r = lax.rsqrt(jnp.mean(yy * yy, axis=-1, keepdims=True) + EPS)
        o_ref[...] = (yy * r * w_ref[...]).astype(o_ref.dtype)

    blk = pl.BlockSpec((tm, gw), lambda g, i: (i, g))
    vec = pl.BlockSpec((1, gw), lambda g, i: (0, g))
    return pl.pallas_call(
        body, grid=(SSM_GROUPS, s // tm),
        in_specs=[blk, blk, pl.BlockSpec((tm, gw), lambda g, i: (i, zc + g)), vec, vec],
        out_specs=blk, out_shape=jax.ShapeDtypeStruct((s, w), MXU_DTYPE),
        compiler_params=_cparams(2), name=name,
    )(y, xbc, src, dexp.reshape(1, w), nw.reshape(1, w))


def _ssm_post_bwd(y, xbc, src, z_col0, dexp, nw, dout, into, *, name, tm=512):
    s, w = y.shape
    gw = w // SSM_GROUPS
    zc = z_col0 * LANES // gw

    def body(y_ref, x_ref, z_ref, d_ref, w_ref, do_ref, into_ref, dy_ref, dx_ref, dz_ref, dd_ref, dw_ref):
        xv, zv, dv = x_ref[...], z_ref[...], d_ref[...]
        pre = y_ref[...] + xv * dv
        sz, sz_grad = _silu_and_grad(zv)
        yy = pre * sz
        r = lax.rsqrt(jnp.mean(yy * yy, axis=-1, keepdims=True) + EPS)
        yh = yy * r
        dov = do_ref[...]
        dyn = dov * w_ref[...]
        dyy = r * (dyn - yh * jnp.mean(dyn * yh, axis=-1, keepdims=True))
        dpre = dyy * sz
        dy_ref[...] = dpre
        dx_ref[...] = dpre * dv
        dz_ref[...] = (dyy * pre * sz_grad).astype(dz_ref.dtype)

        @pl.when(pl.program_id(1) == 0)
        def _():
            dd_ref[...] = jnp.zeros_like(dd_ref)
            dw_ref[...] = jnp.zeros_like(dw_ref)

        dd_ref[...] += jnp.sum(dpre * xv, axis=0, keepdims=True)
        dw_ref[...] += jnp.sum(dov * yh, axis=0, keepdims=True)

    blk = pl.BlockSpec((tm, gw), lambda g, i: (i, g))
    vec = pl.BlockSpec((1, gw), lambda g, i: (0, g))
    z_blk = pl.BlockSpec((tm, gw), lambda g, i: (i, zc + g))
    dy, dx, dz, dd, dw = pl.pallas_call(
        body, grid=(SSM_GROUPS, s // tm),
        in_specs=[blk, blk, z_blk, vec, vec, blk, ANY],
        out_specs=[blk, blk, z_blk, vec, vec],
        out_shape=[jax.ShapeDtypeStruct((s, w), F32), jax.ShapeDtypeStruct((s, w), F32),
                   jax.ShapeDtypeStruct(into.shape, into.dtype), jax.ShapeDtypeStruct((1, w), F32),
                   jax.ShapeDtypeStruct((1, w), F32)],
        input_output_aliases={6: 2},
        compiler_params=_cparams(2), name=name,
    )(y, xbc, src, dexp.reshape(1, w), nw.reshape(1, w), dout, into)
    return dy, dx, dz, dd.reshape(w), dw.reshape(w)


def _merge_fwd(proj3, src, gate_col0, d, *, name, tm=512):
    s = proj3.shape[0]
    nb = proj3.shape[1] // d
    gc = gate_col0 * LANES // d

    def body(*refs):
        p_refs, g_refs, o_ref = refs[:nb], refs[nb:2 * nb], refs[-1]
        acc = None
        for p_ref, g_ref in zip(p_refs, g_refs):
            term = _sigmoid(g_ref[...]) * p_ref[...]
            acc = term if acc is None else acc + term
        o_ref[...] = acc.astype(o_ref.dtype)

    p_specs = [pl.BlockSpec((tm, d), lambda i, b=b: (i, b)) for b in range(nb)]
    g_specs = [pl.BlockSpec((tm, d), lambda i, b=b: (i, gc + b)) for b in range(nb)]
    return pl.pallas_call(
        body, grid=(s // tm,), in_specs=p_specs + g_specs,
        out_specs=pl.BlockSpec((tm, d), lambda i: (i, 0)), out_shape=jax.ShapeDtypeStruct((s, d), MXU_DTYPE),
        compiler_params=_cparams(1), name=name,
    )(*([proj3] * nb), *([src] * nb))


def _merge_bwd(proj3, src, gate_col0, d, dmerged, into, *, name, tm=512):
    s = proj3.shape[0]
    nb = proj3.shape[1] // d
    gc = gate_col0 * LANES // d

    def body(p_ref, g_ref, dm_ref, into_ref, dp_ref, dg_ref):
        sg = _sigmoid(g_ref[...])
        dm = dm_ref[...]
        dp_ref[...] = (dm * sg).astype(dp_ref.dtype)
        dg_ref[...] = (dm * p_ref[...] * sg * (1.0 - sg)).astype(dg_ref.dtype)

    blk = pl.BlockSpec((tm, d), lambda i, b: (i, b))
    gate_blk = pl.BlockSpec((tm, d), lambda i, b: (i, gc + b))
    return pl.pallas_call(
        body, grid=(s // tm, nb),
        in_specs=[blk, gate_blk, pl.BlockSpec((tm, d), lambda i, b: (i, 0)), ANY],
        out_specs=[blk, gate_blk],
        out_shape=[jax.ShapeDtypeStruct(proj3.shape, MXU_DTYPE), jax.ShapeDtypeStruct(into.shape, into.dtype)],
        input_output_aliases={3: 1},
        compiler_params=_cparams(2), name=name,
    )(proj3, src, dmerged, into)


ANY = pl.BlockSpec(memory_space=pl.ANY)
MESH = pl.DeviceIdType.MESH


def _all_gather(shards, *, name, after=None):
    nt = len(shards)
    n_after = 0 if after is None else 1

    def body(*refs):
        x_refs, out_refs = refs[:nt], refs[nt + n_after:2 * nt + n_after]
        send_sems, recv_sems, local_sems = refs[2 * nt + n_after:]
        x, y, c = lax.axis_index("x"), lax.axis_index("y"), lax.axis_index("c")
        me, sibling = (x, y, c), (x, y, 1 - c)
        chips = [(1 - x, y), (x, 1 - y), (1 - x, 1 - y)]

        def slot(t, px, py, pc):
            return out_refs[t].at[4 * px + 2 * py + pc]

        def copy(t, k, block, to, from_input=False):
            return pltpu.make_async_remote_copy(
                src_ref=x_refs[t] if from_input else slot(t, *block), dst_ref=slot(t, *block),
                send_sem=send_sems.at[7 * t + k], recv_sem=recv_sems.at[7 * t + k], device_id=to, device_id_type=MESH)

        mine = [pltpu.make_async_copy(x_refs[t], slot(t, *me), local_sems.at[t]) for t in range(nt)]
        for cp in mine:
            cp.start()
        first = [copy(t, 0, me, sibling, True) for t in range(nt)]
        first += [copy(t, 1 + j, me, (*chip, c), True) for j, chip in enumerate(chips) for t in range(nt)]
        for cp in first:
            cp.start()
        passed = []
        for j, chip in enumerate(chips):
            for t in range(nt):
                copy(t, 1 + j, (*chip, c), me).wait_recv()
                fwd = copy(t, 4 + j, (*chip, c), sibling)
                fwd.start()
                passed.append(fwd)
        for t in range(nt):
            copy(t, 0, sibling, me).wait_recv()
            for j, chip in enumerate(chips):
                copy(t, 4 + j, (*chip, 1 - c), me).wait_recv()
        for cp in first + passed:
            cp.wait_send()
        for cp in mine:
            cp.wait()

    return pl.pallas_call(
        body, out_shape=[jax.ShapeDtypeStruct((N_DEV,) + a.shape, a.dtype) for a in shards],
        in_specs=[ANY] * (nt + n_after), out_specs=[ANY] * nt,
        scratch_shapes=[pltpu.SemaphoreType.DMA((7 * nt,)), pltpu.SemaphoreType.DMA((7 * nt,)),
                        pltpu.SemaphoreType.DMA((nt,))],
        name=name,
    )(*shards, *([] if after is None else [after]))


HBM = pl.BlockSpec(memory_space=pltpu.HBM)
SEM = pl.BlockSpec(memory_space=pltpu.SEMAPHORE)
EFFECT = pltpu.SideEffectType.DATAFLOW_SIDE_EFFECTING


def _peers():
    x, y, c = lax.axis_index("x"), lax.axis_index("y"), lax.axis_index("c")
    peers = []
    for k in range(1, N_DEV):
        px, py, pc = x ^ ((k >> 2) & 1), y ^ ((k >> 1) & 1), c ^ (k & 1)
        peers.append(((px, py, pc), 4 * px + 2 * py + pc))
    return 4 * x + 2 * y + c, peers


def _split_copies(slots, src_refs, land_refs, send_sems, recv_sems):
    me, peers = _peers()
    copies = []
    for t, (whole, layer) in enumerate(slots):
        dst = land_refs[t].at[me] if layer is None else land_refs[t].at[me, layer]
        for k, (dev, lin) in enumerate(peers):
            copies.append(pltpu.make_async_remote_copy(
                src_ref=src_refs[t] if whole else src_refs[t].at[lin], dst_ref=dst,
                send_sem=send_sems.at[7 * t + k], recv_sem=recv_sems.at[7 * t + k], device_id=dev, device_id_type=MESH))
    return copies


def _split_start(srcs, lands, slots, carry, *, name):
    n = len(srcs)

    def body(*refs):
        copies = _split_copies(slots, refs[:n], refs[n:2 * n], refs[2 * n + 1], refs[2 * n + 2])
        for cp in copies:
            cp.start()

    def hbm(a):
        return pltpu.HBM(a.shape, a.dtype)

    outs = pl.pallas_call(
        body, name=name,
        out_shape=[pltpu.SemaphoreType.DMA((7 * n,)), pltpu.SemaphoreType.DMA((7 * n,))]
        + [hbm(a) for a in srcs] + [hbm(a) for a in lands] + [hbm(carry)],
        in_specs=[HBM] * (2 * n + 1), out_specs=[SEM, SEM] + [HBM] * (2 * n + 1),
        input_output_aliases={i: 2 + i for i in range(2 * n + 1)},
        compiler_params=pltpu.CompilerParams(has_side_effects=EFFECT),
    )(*[pltpu.with_memory_space_constraint(a, pltpu.HBM) for a in list(srcs) + list(lands) + [carry]])
    return outs[0], outs[1], outs[2:2 + n], outs[2 + n:2 + 2 * n], outs[2 + 2 * n]


def _split_wait(send_sems, recv_sems, srcs, lands, slots, after, *, name):
    n = len(srcs)

    def body(*refs):
        copies = _split_copies(slots, refs[:n], refs[n:2 * n], refs[2 * n], refs[2 * n + 1])
        for cp in copies:
            cp.wait_send()
        for cp in copies:
            cp.wait_recv()

    outs = pl.pallas_call(
        body, name=name,
        out_shape=[pltpu.HBM(a.shape, a.dtype) for a in list(srcs) + list(lands)],
        in_specs=[HBM] * (2 * n) + [SEM, SEM, ANY], out_specs=[HBM] * (2 * n),
        input_output_aliases={i: i for i in range(2 * n)},
        compiler_params=pltpu.CompilerParams(has_side_effects=EFFECT),
    )(*srcs, *lands, send_sems, recv_sems, after)
    return outs[n:]


def _adam_math(w, g, m, v):
    m1 = ADAM_B1 * m + (1.0 - ADAM_B1) * g
    v1 = ADAM_B2 * v + (1.0 - ADAM_B2) * (g * g)
    m_hat = m1 / (1.0 - ADAM_B1 ** ADAM_STEP)
    v_hat = v1 / (1.0 - ADAM_B2 ** ADAM_STEP)
    delta = -ADAM_LR * (m_hat / (jnp.sqrt(v_hat) + ADAM_EPS) + ADAM_WD * w)
    return delta, m1, v1


def _sum_adamw(parts, w, m, v, layer, prev, *, name):
    shape = w.shape
    r, c = shape[-2], shape[-1]
    a_l = math.prod(shape[1:-2])
    a = shape[0] * a_l
    base = layer * a_l
    if r % 256 == 0:
        tr, tc = 256, c
    else:
        tr, tc = r, _pick(c, (256, 128))
    w3, m3, v3 = (t.reshape(a, r, c) for t in (w, m, v))
    n_prev = 0 if prev is None else 4

    def body(*refs):
        p_ref, w_ref, m_ref, v_ref = refs[:4]
        g_ref, d_ref, m1_ref, v1_ref = refs[4 + n_prev:]
        g = p_ref[0].astype(F32)
        for src in range(1, N_DEV):
            g = g + p_ref[src].astype(F32)
        delta, m1, v1 = _adam_math(w_ref[...], g, m_ref[...], v_ref[...])
        g_ref[...] = g
        d_ref[...] = delta
        m1_ref[...] = m1
        v1_ref[...] = v1

    nr, ncol = r // tr, c // tc
    blk = pl.BlockSpec((None, tr, tc), lambda i, j: (base + i, j // ncol, j % ncol))
    prev3 = [] if prev is None else [t.reshape(a, r, c) for t in prev]
    outs = pl.pallas_call(
        body, grid=(a_l, nr * ncol),
        in_specs=[pl.BlockSpec((N_DEV, None, tr, tc), lambda i, j: (0, i, j // ncol, j % ncol)), blk, blk, blk]
        + [ANY] * n_prev,
        out_specs=[blk] * 4, out_shape=[jax.ShapeDtypeStruct((a, r, c), F32)] * 4,
        input_output_aliases={4 + k: k for k in range(n_prev)},
        compiler_params=_cparams(2), name=name,
    )(parts.reshape(N_DEV, a_l, r, c), w3, m3, v3, *prev3)
    return [o.reshape(shape) for o in outs]


def _sum_parts(parts, *, name):
    rows = parts.shape[1]

    def body(p_ref, o_ref):
        g = p_ref[0]
        for src in range(1, N_DEV):
            g = g + p_ref[src]
        o_ref[...] = g

    return pl.pallas_call(
        body, grid=(1,), in_specs=[pl.BlockSpec((N_DEV, rows, LANES), lambda i: (0, 0, 0))],
        out_specs=pl.BlockSpec((rows, LANES), lambda i: (0, 0)), out_shape=jax.ShapeDtypeStruct((rows, LANES), F32),
        compiler_params=_cparams(1), name=name,
    )(parts)


def _adamw(w, g, m, v, *, name):
    rows = w.shape[0]

    def body(w_ref, g_ref, m_ref, v_ref, d_ref, m1_ref, v1_ref):
        delta, m1, v1 = _adam_math(w_ref[...], g_ref[...], m_ref[...], v_ref[...])
        d_ref[...] = delta
        m1_ref[...] = m1
        v1_ref[...] = v1

    blk = pl.BlockSpec((rows, LANES), lambda i: (0, 0))
    return pl.pallas_call(
        body, grid=(1,), in_specs=[blk] * 4, out_specs=[blk] * 3,
        out_shape=[jax.ShapeDtypeStruct((rows, LANES), F32)] * 3,
        compiler_params=_cparams(1), name=name,
    )(w, g, m, v)


def _pack(arrs, dtype, row_mult=16):
    flat = jnp.concatenate([a.reshape(-1).astype(dtype) for a in arrs])
    n = flat.shape[0]
    rows = -(-n // (LANES * row_mult)) * row_mult
    flat = jnp.pad(flat, (0, rows * LANES - n))
    return flat.reshape(rows, LANES)


def _unpack(packed, shapes):
    flat = packed.reshape(-1)
    out, off = [], 0
    for shp in shapes:
        n = math.prod(shp)
        out.append(flat[off:off + n].reshape(shp))
        off += n
    return out


class _Layout:
    def __init__(self, d):
        self.d = d
        w = d
        self.dn_heads = w // DN_HEAD_DIM
        self.ssm_heads = w // SSM_HEAD_DIM
        gn = SSM_GROUPS * SSM_STATE
        self.sizes = (3 * w, w, self.dn_heads, self.dn_heads, 3 * w, w, w + 2 * gn, self.ssm_heads, 3 * d)
        offs, o = [], 0
        for sz in self.sizes:
            offs.append(o)
            o += sz
        self.offs = offs
        self.in_dim = o
        self.big = (0, 1, 4, 5, 6, 8)
        self.small = (2, 3, 7)
        cols, o = {}, 0
        for idx in self.big:
            cols[idx] = o
            o += self.sizes[idx]
        self.small_col = o
        self.cols = cols
        self.padded = o + LANES
        self.n_small = sum(self.sizes[i] for i in self.small)

    def from_shards(self, parts):
        cs = self.in_dim // N_DEV
        pieces = []
        for i in self.big + self.small:
            a, b = self.offs[i], self.offs[i] + self.sizes[i]
            while a < b:
                j = a // cs
                hi = min(b, (j + 1) * cs)
                pieces.append(parts[j][:, a - j * cs:hi - j * cs])
                a = hi
        pieces.append(jnp.zeros((parts.shape[1], LANES - self.n_small), parts.dtype))
        return jnp.concatenate(pieces, axis=1)

    def to_shards(self, wp):
        cs = self.in_dim // N_DEV
        pcol = dict(self.cols)
        o = self.small_col
        for i in self.small:
            pcol[i] = o
            o += self.sizes[i]
        shards = []
        for j in range(N_DEV):
            a, b = j * cs, (j + 1) * cs
            pieces = []
            for i in range(len(self.sizes)):
                lo, hi = max(a, self.offs[i]), min(b, self.offs[i] + self.sizes[i])
                if lo < hi:
                    pieces.append(wp[:, pcol[i] + lo - self.offs[i]:pcol[i] + hi - self.offs[i]])
            shards.append(jnp.concatenate(pieces, axis=1))
        return jnp.stack(shards)

def _rows_form(cols_t, nh, nc):
    return cols_t.T.reshape(nh, nc, 1, CHUNK)


def _layer_fwd(x, p, lay, tag, late=None):
    s, d = x.shape
    nc = s // CHUNK
    w = d
    dnh, smh = lay.dn_heads, lay.ssm_heads
    r = smh // SSM_GROUPS
    cb = {k: v // LANES for k, v in lay.cols.items()}
    sv = {}
    h1 = _rms_fwd(x, p["norm_mix"], name=f"rms_mix_{tag}")
    proj = _matmul(h1, p["w_in"], name=f"mm_in_{tag}")
    small = proj[:, lay.small_col:lay.small_col + LANES]
    a_rows = _rows_form(small[:, 0:dnh], dnh, nc)
    b_rows = _rows_form(small[:, dnh:2 * dnh], dnh, nc)
    dt_rows = small[:, 2 * dnh:2 * dnh + smh].T.reshape(SSM_GROUPS, r, nc, CHUNK).transpose(0, 2, 1, 3)
    zero_b = jnp.zeros((1, 3 * w), F32)
    dn_qkv = _conv_fwd(proj, cb[0], p["dn_conv_w"], zero_b, 2 * dnh, name=f"dn_conv_{tag}")
    dn_alog = p["dn_a_log"].reshape(dnh, 1, 1)
    dn_dtb = p["dn_dt_bias"].reshape(dnh, 1, 1)
    o_dn, dn_states, dn_inv = _dn_fwd(dn_qkv, a_rows, b_rows, dn_alog, dn_dtb, name=f"dn_chunk_{tag}")
    y_dn = _dn_post_fwd(o_dn, proj, cb[1], p["dn_norm_w"], name=f"dn_post_{tag}")
    o_sb, sb_r = _sb_fwd(proj, cb[4], w, name=f"sb_{tag}")
    xbc = _conv_fwd(proj, cb[6], p["ssm_conv_w"], p["ssm_conv_b"].reshape(1, -1), 0, name=f"ssm_conv_{tag}")
    ssm_alog = p["ssm_a_log"].reshape(SSM_GROUPS, r, 1)
    ssm_dtb = p["ssm_dt_bias"].reshape(SSM_GROUPS, r, 1)
    y_ssd, ssm_states = _ssd_fwd(xbc, dt_rows, ssm_alog, ssm_dtb, name=f"ssd_{tag}")
    dexp = jnp.repeat(p["ssm_d"], SSM_HEAD_DIM)
    y_ssm = _ssm_post_fwd(y_ssd, xbc, proj, cb[5], dexp, p["ssm_norm_w"], name=f"ssm_post_{tag}")
    if late is not None:
        p.update(late(y_ssm))
    branches = (y_dn, o_sb, y_ssm)
    proj3 = lax.empty((s, 3 * d), F32)
    for i, br in enumerate(branches):
        proj3 = _matmul(br, p["w_branch"][i], into=(proj3, i * d), name=f"mm_branch{i}_{tag}")
    merged = _merge_fwd(proj3, proj, cb[8], d, name=f"merge_{tag}")
    x1 = _matmul(merged, p["w_out"], name=f"mm_out_{tag}", epilogue=lambda acc, res: (acc + res,), extras=(x,))
    h2 = _rms_fwd(x1, p["norm_mlp"], name=f"rms_mlp_{tag}")
    u, act = _matmul(h2, p["w_up"], name=f"mm_up_{tag}", out_dtypes=(F32, MXU_DTYPE),
                     epilogue=lambda acc: (acc, jnp.square(jnp.maximum(acc, 0.0))))
    x2 = _matmul(act, p["w_down"], name=f"mm_down_{tag}", epilogue=lambda acc, res: (acc + res,), extras=(x1,))
    sv.update(x=x, h1=h1, proj=proj, a_rows=a_rows, b_rows=b_rows, dt_rows=dt_rows, dn_qkv=dn_qkv, dn_alog=dn_alog,
              dn_dtb=dn_dtb, o_dn=o_dn, dn_states=dn_states, dn_inv=dn_inv, y_dn=y_dn, o_sb=o_sb, sb_r=sb_r, xbc=xbc, ssm_alog=ssm_alog,
              ssm_dtb=ssm_dtb, y_ssd=y_ssd, ssm_states=ssm_states, dexp=dexp, y_ssm=y_ssm, proj3=proj3, merged=merged,
              x1=x1, h2=h2, u=u, act=act)
    return x2, sv


def _layer_bwd(dx2, p, sv, lay, tag, early=None, late=None):
    x = sv["x"]
    s, d = x.shape
    nc = s // CHUNK
    w = d
    dnh, smh = lay.dn_heads, lay.ssm_heads
    r = smh // SSM_GROUPS
    gn = SSM_GROUPS * SSM_STATE
    cb = {k: v // LANES for k, v in lay.cols.items()}
    proj = sv["proj"]
    g = {}
    dx2_b = dx2.astype(MXU_DTYPE)
    du = _matmul(dx2_b, p["w_down"], tb=True, name=f"mm_down_dx_{tag}", out_dtypes=(MXU_DTYPE,),
                 epilogue=lambda acc, uu: (acc * (2.0 * jnp.maximum(uu, 0.0)),), extras=(sv["u"],))
    g["w_down"] = _matmul(sv["act"], dx2_b, ta=True, name=f"mm_down_dw_{tag}", out_dtypes=(BF16,)).reshape(N_DEV, -1, d)
    g["w_up"] = _matmul(sv["h2"], du, ta=True, name=f"mm_up_dw_{tag}", out_dtypes=(BF16,), col_shards=N_DEV)
    dh2 = _matmul(du, p["w_up"], tb=True, name=f"mm_up_dx_{tag}")
    dx1, g["norm_mlp"] = _rms_bwd(sv["x1"], p["norm_mlp"], dh2, dx2, name=f"rms_mlp_bwd_{tag}")
    dx1_b = dx1.astype(MXU_DTYPE)
    dmerged = _matmul(dx1_b, p["w_out"], tb=True, name=f"mm_out_dx_{tag}")
    g["w_out"] = _matmul(sv["merged"], dx1_b, ta=True, name=f"mm_out_dw_{tag}", out_dtypes=(BF16,)).reshape(N_DEV, -1, d)
    dproj = lax.empty((s, lay.padded), MXU_DTYPE)
    dproj3, dproj = _merge_bwd(sv["proj3"], proj, cb[8], d, dmerged, dproj, name=f"merge_bwd_{tag}")
    branches = (sv["y_dn"], sv["o_sb"], sv["y_ssm"])
    dwb, dbr = [], []
    for i, br in enumerate(branches):
        cols = (i * d, d)
        dwb.append(_matmul(br, dproj3, ta=True, b_cols=cols, name=f"mm_branch{i}_dw_{tag}",
                           out_dtypes=(BF16,)).reshape(N_DEV, -1, d))
        dbr.append(_matmul(dproj3, p["w_branch"][i], tb=True, a_cols=cols, name=f"mm_branch{i}_dx_{tag}"))
    g["w_branch"] = jnp.stack(dwb, axis=1)
    dy_dn, do_sb, dy_ssm = dbr
    if early is not None:
        dy_ssm = early(g, dy_ssm)
    dy_ssd, dxs_skip, dproj, ddexp, g["ssm_norm_w"] = _ssm_post_bwd(
        sv["y_ssd"], sv["xbc"], proj, cb[5], sv["dexp"], p["ssm_norm_w"], dy_ssm, dproj, name=f"ssm_post_bwd_{tag}")
    g["ssm_d"] = ddexp.reshape(smh, SSM_HEAD_DIM).sum(axis=1)
    dxs, dbm, dcm, ddt_rows, dalog, ddtb = _ssd_bwd(
        sv["xbc"], sv["dt_rows"], sv["ssm_alog"], sv["ssm_dtb"], sv["ssm_states"], dy_ssd, dxs_skip, name=f"ssd_bwd_{tag}")
    g["ssm_a_log"] = dalog.reshape(smh)
    g["ssm_dt_bias"] = ddtb.reshape(smh)
    dxbc_post = jnp.concatenate([dxs, dbm, dcm], axis=1)
    dproj, g["ssm_conv_w"], dcb = _conv_bwd(proj, cb[6], p["ssm_conv_w"], p["ssm_conv_b"].reshape(1, -1), 0, dxbc_post,
                                            dproj, name=f"ssm_conv_bwd_{tag}")
    g["ssm_conv_b"] = dcb.reshape(-1)
    ddt = ddt_rows.transpose(0, 2, 1, 3).reshape(smh, s).T
    dqkv_sb = _sb_bwd(proj, cb[4], w, sv["sb_r"], do_sb, name=f"sb_bwd_{tag}")
    dproj = lax.dynamic_update_slice(dproj, jnp.concatenate([t.astype(MXU_DTYPE) for t in dqkv_sb], axis=1), (0, lay.cols[4]))
    do_dn, dproj, g["dn_norm_w"] = _dn_post_bwd(sv["o_dn"], proj, cb[1], p["dn_norm_w"], dy_dn, dproj,
                                                name=f"dn_post_bwd_{tag}")
    dqkv_dn, da_rows, db_rows, dal, ddtb_dn = _dn_bwd(
        sv["dn_qkv"], sv["a_rows"], sv["b_rows"], sv["dn_alog"], sv["dn_dtb"], sv["dn_states"], sv["dn_inv"], do_dn,
        name=f"dn_chunk_bwd_{tag}")
    g["dn_a_log"] = dal.reshape(dnh)
    g["dn_dt_bias"] = ddtb_dn.reshape(dnh)
    zero_b = jnp.zeros((1, 3 * w), F32)
    dproj, g["dn_conv_w"], _ = _conv_bwd(proj, cb[0], p["dn_conv_w"], zero_b, 2 * dnh, dqkv_dn, dproj,
                                         name=f"dn_conv_bwd_{tag}")
    da = da_rows.reshape(dnh, s).T
    db = db_rows.reshape(dnh, s).T
    dsmall = jnp.concatenate([da, db, ddt, jnp.zeros((s, LANES - lay.n_small), F32)], axis=1).astype(MXU_DTYPE)
    dproj = lax.dynamic_update_slice(dproj, dsmall, (0, lay.small_col))
    g["w_in"] = lay.to_shards(_matmul(sv["h1"], dproj, ta=True, name=f"mm_in_dw_{tag}", out_dtypes=(BF16,)))
    if late is not None:
        dproj = late(g, dproj)
    dh1 = _matmul(dproj, p["w_in"], tb=True, name=f"mm_in_dx_{tag}")
    dx0, g["norm_mix"] = _rms_bwd(x, p["norm_mix"], dh1, dx1, name=f"rms_mix_bwd_{tag}")
    return dx0, g


BIG = ("w_in", "w_branch", "w_out", "w_up", "w_down")
CONV = ("dn_conv_w", "ssm_conv_w")
SMALL = ("norm_mix", "dn_conv_w", "dn_a_log", "dn_dt_bias", "dn_norm_w", "ssm_conv_w", "ssm_conv_b", "ssm_a_log",
         "ssm_dt_bias", "ssm_d", "ssm_norm_w", "norm_mlp", "norm_final")
WEIGHTS = ("norm_mix", "w_in", "dn_conv_w", "dn_a_log", "dn_dt_bias", "dn_norm_w", "ssm_conv_w", "ssm_conv_b", "ssm_a_log",
           "ssm_dt_bias", "ssm_d", "ssm_norm_w", "w_branch", "w_out", "norm_mlp", "w_up", "w_down", "norm_final")
SHARD_AXIS = {"w_in": 2, "dn_conv_w": 2, "ssm_conv_w": 2, "w_branch": 2, "w_out": 1, "w_up": 2, "w_down": 1}


def _to_shards(full, axis):
    shp = full.shape
    n = shp[axis] // N_DEV
    t = full.reshape(shp[:axis] + (N_DEV, n) + shp[axis + 1:])
    return jnp.moveaxis(t, axis, 0)


def _unshard(parts, axis, *, name):
    shard = parts.shape[1:]
    nd = len(shard)
    if axis == 0:
        return parts.reshape((N_DEV * shard[0],) + shard[1:])

    def copy_block(i_ref, o_ref):
        o_ref[...] = i_ref[...]

    if axis == nd - 1:
        rows, n = math.prod(shard[:-1]), shard[-1]
        out = pl.pallas_call(
            copy_block, grid=(N_DEV,),
            in_specs=[pl.BlockSpec((None, rows, n), lambda j: (j, 0, 0))],
            out_specs=pl.BlockSpec((rows, n), lambda j: (0, j)),
            out_shape=jax.ShapeDtypeStruct((rows, N_DEV * n), parts.dtype),
            compiler_params=_cparams(1), name=name,
        )(parts.reshape(N_DEV, rows, n))
        return out.reshape(shard[:-1] + (N_DEV * n,))
    assert axis == nd - 2, (parts.shape, axis)
    a, n, c = math.prod(shard[:-2]), shard[-2], shard[-1]
    out = pl.pallas_call(
        copy_block, grid=(N_DEV, a),
        in_specs=[pl.BlockSpec((None, None, n, c), lambda j, i: (j, i, 0, 0))],
        out_specs=pl.BlockSpec((None, n, c), lambda j, i: (i, j, 0)),
        out_shape=jax.ShapeDtypeStruct((a, N_DEV * n, c), parts.dtype),
        compiler_params=_cparams(2), name=name,
    )(parts.reshape(N_DEV, a, n, c))
    return out.reshape(shard[:-2] + (N_DEV * n, c))


def _step(w, m, v, x, target):
    s, d = x.shape
    lay = _Layout(d)
    me = 4 * lax.axis_index("x") + 2 * lax.axis_index("y") + lax.axis_index("c")

    def shard(n, l):
        return w[n][l].astype(BF16) if n in BIG else w[n][l]

    def empty_land(a):
        return lax.empty((N_DEV,) + a.shape, a.dtype)

    def with_own(land, own):
        return lax.dynamic_update_index_in_dim(land, own, me, 0)

    def assemble(n, parts, l):
        return lay.from_shards(parts) if n == "w_in" else _unshard(parts, SHARD_AXIS[n] - 1, name=f"unshard_{n}_l{l}")

    small_names = tuple(n for n in WEIGHTS if n not in BIG + CONV + ("norm_final",))

    first, rest = ("w_in",) + CONV, BIG[1:]
    got = _all_gather([shard(n, 0) for n in first], name="gather_l0_first")
    whole, sliced = (True, None), (False, None)
    names_a, names_b = rest, BIG + CONV
    srcs_a, srcs_b = [shard(n, 0) for n in names_a], [shard(n, 1) for n in names_b]
    sem_sa, sem_ra, srcs_a, lands_a, w_in0 = _split_start(
        srcs_a, [empty_land(a) for a in srcs_a], [whole] * len(srcs_a), got[0], name="gather_l0_rest_start")
    sem_sb, sem_rb, srcs_b, lands_b, w_in0 = _split_start(
        srcs_b, [empty_land(a) for a in srcs_b], [whole] * len(srcs_b), w_in0, name="gather_l1_start")
    p0 = {n: w[n][0] for n in small_names}
    p0.update({n: assemble(n, g, 0) for n, g in zip(first, [w_in0] + list(got[1:]))})

    def late_l0(after):
        lands = _split_wait(sem_sa, sem_ra, srcs_a, lands_a, [whole] * len(srcs_a), after, name="gather_l0_rest_wait")
        return {n: assemble(n, with_own(ld, s_), 0) for n, ld, s_ in zip(names_a, lands, srcs_a)}

    h, sv0 = _layer_fwd(x, p0, lay, "l0", late=late_l0)
    lands = _split_wait(sem_sb, sem_rb, srcs_b, lands_b, [whole] * len(srcs_b), h, name="gather_l1_wait")
    p1 = {n: w[n][1] for n in small_names}
    p1.update({n: assemble(n, with_own(ld, s_), 1) for n, ld, s_ in zip(names_b, lands, srcs_b)})
    h, sv1 = _layer_fwd(h, p1, lay, "l1")
    loss, dh, g_norm_final = _final_loss(h, w["norm_final"], target, name="final_loss")
    grads = [None] * DEPTH
    dh, grads[1] = _layer_bwd(dh, p1, sv1, lay, "l1")

    def exchange_start(names, g, carry, tag):
        srcs = [g[n] for n in names]
        return _split_start(srcs, [lax.empty(a.shape, a.dtype) for a in srcs], [sliced] * len(srcs), carry,
                            name=f"grad_{tag}_start")

    def exchange_wait(names, started, after, tag):
        sem_s, sem_r, srcs, lands_, _ = started
        lands_ = _split_wait(sem_s, sem_r, srcs, lands_, [sliced] * len(srcs), after, name=f"grad_{tag}_wait")
        return {n: with_own(ld, lax.dynamic_index_in_dim(s_, me, 0, keepdims=False)) for n, ld, s_ in zip(names, lands_, srcs)}

    x1_started = exchange_start(BIG, grads[1], dh, "l1")
    pending = {}

    def early_l0(g, carry):
        pending["rest"] = exchange_start(rest, g, carry, "l0_rest")
        return pending["rest"][4]

    def late_bwd_l0(g, carry):
        pending["w_in"] = exchange_start(("w_in",), g, carry, "l0_w_in")
        return pending["w_in"][4]

    grad_x, grads[0] = _layer_bwd(x1_started[4], p0, sv0, lay, "l0", early=early_l0, late=late_bwd_l0)

    out = {"grad": {}, "delta": {}, "new_m": {}, "new_v": {}}
    parts1 = exchange_wait(BIG, x1_started, grad_x, "l1")
    res1 = {n: _sum_adamw(parts1[n], w[n], m[n], v[n], 1, None, name=f"sum_adamw_{n}_l1") for n in BIG}
    parts0 = exchange_wait(rest, pending["rest"], res1["w_in"][0], "l0_rest")
    res0 = {n: _sum_adamw(parts0[n], w[n], m[n], v[n], 0, res1[n], name=f"sum_adamw_{n}_l0") for n in rest}
    parts0 = exchange_wait(("w_in",), pending["w_in"], res0["w_down"][0], "l0_w_in")
    res0["w_in"] = _sum_adamw(parts0["w_in"], w["w_in"], m["w_in"], v["w_in"], 0, res1["w_in"], name="sum_adamw_w_in_l0")
    for n in BIG:
        for key, a in zip(("grad", "delta", "new_m", "new_v"), res0[n]):
            out[key][n] = a

    gfull = {n: jnp.stack([grads[l][n] for l in range(DEPTH)]) for n in SMALL if n != "norm_final"}
    gfull["norm_final"] = g_norm_final
    small_send = _pack([gfull[n] for n in SMALL] + [loss.reshape(1)], F32)
    small_recv = _all_gather([small_send], name="gather_small_grads", after=res0["w_in"][0])[0]
    small_sum = _sum_parts(small_recv, name="sum_small")
    small_full = _unpack(small_sum, [gfull[n].shape for n in SMALL] + [(1,)])
    loss_total = small_full[-1][0]
    gsmall = {}
    for n, a in zip(SMALL, small_full[:-1]):
        if n in SHARD_AXIS:
            a = lax.dynamic_index_in_dim(_to_shards(a, SHARD_AXIS[n]), me, axis=0, keepdims=False)
        gsmall[n] = a
    small_shapes = [w[n].shape for n in SMALL]
    ws, gs, ms, vs = (_pack([t[n] for n in SMALL], F32) for t in (w, gsmall, m, v))
    ds, m1s, v1s = _adamw(ws, gs, ms, vs, name="adamw_small")
    for n in SMALL:
        out["grad"][n] = gsmall[n]
    for key, packed in (("delta", ds), ("new_m", m1s), ("new_v", v1s)):
        for n, a in zip(SMALL, _unpack(packed, small_shapes)):
            out[key][n] = a
    return loss_total, grad_x, out


def kernel(x, norm_mix, w_in, dn_conv_w, dn_a_log, dn_dt_bias, dn_norm_w, ssm_conv_w, ssm_conv_b, ssm_a_log, ssm_dt_bias, ssm_d, ssm_norm_w, w_branch, w_out, norm_mlp, w_up, w_down, norm_final, loss_target, m_norm_mix, m_w_in, m_dn_conv_w, m_dn_a_log, m_dn_dt_bias, m_dn_norm_w, m_ssm_conv_w, m_ssm_conv_b, m_ssm_a_log, m_ssm_dt_bias, m_ssm_d, m_ssm_norm_w, m_w_branch, m_w_out, m_norm_mlp, m_w_up, m_w_down, m_norm_final, v_norm_mix, v_w_in, v_dn_conv_w, v_dn_a_log, v_dn_dt_bias, v_dn_norm_w, v_ssm_conv_w, v_ssm_conv_b, v_ssm_a_log, v_ssm_dt_bias, v_ssm_d, v_ssm_norm_w, v_w_branch, v_w_out, v_norm_mlp, v_w_up, v_w_down, v_norm_final):
    w = dict(norm_mix=norm_mix, w_in=w_in, dn_conv_w=dn_conv_w, dn_a_log=dn_a_log, dn_dt_bias=dn_dt_bias, dn_norm_w=dn_norm_w,
             ssm_conv_w=ssm_conv_w, ssm_conv_b=ssm_conv_b, ssm_a_log=ssm_a_log, ssm_dt_bias=ssm_dt_bias, ssm_d=ssm_d,
             ssm_norm_w=ssm_norm_w, w_branch=w_branch, w_out=w_out, norm_mlp=norm_mlp, w_up=w_up, w_down=w_down,
             norm_final=norm_final)
    m = dict(norm_mix=m_norm_mix, w_in=m_w_in, dn_conv_w=m_dn_conv_w, dn_a_log=m_dn_a_log, dn_dt_bias=m_dn_dt_bias,
             dn_norm_w=m_dn_norm_w, ssm_conv_w=m_ssm_conv_w, ssm_conv_b=m_ssm_conv_b, ssm_a_log=m_ssm_a_log,
             ssm_dt_bias=m_ssm_dt_bias, ssm_d=m_ssm_d, ssm_norm_w=m_ssm_norm_w, w_branch=m_w_branch, w_out=m_w_out,
             norm_mlp=m_norm_mlp, w_up=m_w_up, w_down=m_w_down, norm_final=m_norm_final)
    v = dict(norm_mix=v_norm_mix, w_in=v_w_in, dn_conv_w=v_dn_conv_w, dn_a_log=v_dn_a_log, dn_dt_bias=v_dn_dt_bias,
             dn_norm_w=v_dn_norm_w, ssm_conv_w=v_ssm_conv_w, ssm_conv_b=v_ssm_conv_b, ssm_a_log=v_ssm_a_log,
             ssm_dt_bias=v_ssm_dt_bias, ssm_d=v_ssm_d, ssm_norm_w=v_ssm_norm_w, w_branch=v_w_branch, w_out=v_w_out,
             norm_mlp=v_norm_mlp, w_up=v_w_up, w_down=v_w_down, norm_final=v_norm_final)
    loss, grad_x, out = _step(w, m, v, x[0], loss_target[0])
    return (loss, grad_x[None], *[out["grad"][n] for n in WEIGHTS], *[out["delta"][n] for n in WEIGHTS],
            *[out["new_m"][n] for n in WEIGHTS], *[out["new_v"][n] for n in WEIGHTS])
```
